```python
import math
import jax
import jax.numpy as jnp
from jax import lax
import numpy as np

D_MODEL = 1024
BATCH = 4
SEQ = 8192
DEPTH = 2

HEAD_DIM = 64
MIX_WIDTH = D_MODEL // 4
N_BRANCH = 4
Q_BLOCK = 128

LRU_WIDTH = MIX_WIDTH
LRU_BLOCKS = 4
LRU_BLOCK = LRU_WIDTH // LRU_BLOCKS
CONV_WIDTH = 4
LRU_C = 8.0
LRU_A_MIN = 0.9
LRU_A_MAX = 0.999

SWA_HEADS = MIX_WIDTH // HEAD_DIM
SWA_KV_HEADS = 2
SWA_WINDOW = 128

DIFF_HEADS = 4
DIFF_DIM = MIX_WIDTH // (2 * DIFF_HEADS)
DIFF_SUBLN_EPS = 1e-5

NSA_HEADS = MIX_WIDTH // HEAD_DIM
NSA_CMP_LEN = 32
NSA_CMP_STRIDE = 16
NSA_SLC_LEN = 64
NSA_TOPK = 16
NSA_WINDOW = 512
NSA_FORCED_SCORE = 1e4

REL_BUCKETS = 32
REL_MAX_DIST = 128
N_BIAS_HEADS = SWA_HEADS + DIFF_HEADS + NSA_HEADS

N_GROUPS = 4
EXPERTS_PER_GROUP = 8
N_EXPERTS = N_GROUPS * EXPERTS_PER_GROUP
TOPK_IN_GROUP = 2
EXPERT_FF = D_MODEL // 4
ROUTER_BIAS_SCALE = 0.01

NORM_EPS = 1e-6
NEG_INF = -1e30
TINY = 1e-30

IN_SPLITS = (
    LRU_WIDTH, LRU_WIDTH,
    SWA_HEADS * HEAD_DIM, SWA_KV_HEADS * HEAD_DIM, SWA_KV_HEADS * HEAD_DIM,
    2 * DIFF_HEADS * DIFF_DIM, 2 * DIFF_HEADS * DIFF_DIM, 2 * DIFF_HEADS * DIFF_DIM,
    NSA_HEADS * HEAD_DIM,
    HEAD_DIM, HEAD_DIM, HEAD_DIM, HEAD_DIM, HEAD_DIM, HEAD_DIM,
    3 * NSA_HEADS,
)
N_IN = sum(IN_SPLITS)

kernel_name = 'hybrid_griffin_gated_nsa_diff_moe'


def rmsnorm(x, g, eps=NORM_EPS):
    xf = x.astype(jnp.float32)
    y = xf * lax.rsqrt(jnp.mean(xf * xf, axis=-1, keepdims=True) + eps)
    return (y * g.astype(jnp.float32)).astype(x.dtype)


def t5_bucket(dist):
    n = jnp.maximum(dist, 0)
    exact = REL_BUCKETS // 2
    scaled = jnp.log(jnp.maximum(n, 1).astype(jnp.float32) / exact) / math.log(REL_MAX_DIST / exact)
    large = jnp.minimum(exact + (scaled * (REL_BUCKETS - exact)).astype(jnp.int32), REL_BUCKETS - 1)
    return jnp.where(n < exact, n, large)


def masked_softmax(s, mask, sink=None):
    s = jnp.where(mask, s, NEG_INF)
    m = jnp.max(s, axis=-1, keepdims=True)
    if sink is not None:
        m = jnp.maximum(m, sink)
    p = jnp.where(mask, jnp.exp(s - m), 0.0)
    denom = jnp.sum(p, axis=-1, keepdims=True)
    if sink is not None:
        denom = denom + jnp.exp(sink - m)
    return p / jnp.maximum(denom, TINY)


def split_columns(proj):
    offs = np.cumsum(IN_SPLITS)[:-1].tolist()
    return jnp.split(proj, offs, axis=-1)


def rglru_mixer(xa, gate, conv_w, conv_b, w_a, b_a, w_x, b_x, lam):
    f32 = jnp.float32
    Bsz, S, W = xa.shape
    xp = jnp.pad(xa.astype(f32), ((0, 0), (CONV_WIDTH - 1, 0), (0, 0)))
    xc = conv_b.astype(f32) + xp[:, 0:S] * conv_w[0].astype(f32)
    for i in range(1, CONV_WIDTH):
        xc = xc + xp[:, i:i + S] * conv_w[i].astype(f32)
    xb = xc.reshape(Bsz, S, LRU_BLOCKS, LRU_BLOCK)
    r = jax.nn.sigmoid(jnp.einsum('bshi,hij->bshj', xb, w_a.astype(f32)).reshape(Bsz, S, W) + b_a.astype(f32))
    i_g = jax.nn.sigmoid(jnp.einsum('bshi,hij->bshj', xb, w_x.astype(f32)).reshape(Bsz, S, W) + b_x.astype(f32))
    log_a = -LRU_C * r * jax.nn.softplus(-lam.astype(f32))
    a = jnp.exp(log_a)
    mult = jnp.sqrt(-jnp.expm1(2.0 * log_a))
    mult = jnp.where(jnp.arange(S)[None, :, None] == 0, 1.0, mult)
    b = mult * i_g * xc

    def combine(e, l):
        return (e[0] * l[0], l[0] * e[1] + l[1])

    _, hs = lax.associative_scan(combine, (a, b), axis=1)
    return (hs * jax.nn.gelu(gate.astype(f32))).astype(xa.dtype)


def banded_attention(q, k, v, window, bias_tab, sinks=None):
    f32 = jnp.float32
    Bsz, S, Hq, dh = q.shape
    Hkv = k.shape[2]
    G = Hq // Hkv
    nb = S // Q_BLOCK
    nback = -(-window // Q_BLOCK)
    kw_len = (nback + 1) * Q_BLOCK

    def windows(t):
        tp = jnp.pad(t.astype(f32), ((0, 0), (nback * Q_BLOCK, 0), (0, 0), (0, 0)))
        tp = tp.reshape(Bsz, nb + nback, Q_BLOCK, Hkv, dh)
        return jnp.concatenate([tp[:, i:i + nb] for i in range(nback + 1)], axis=2)

    kb, vb = windows(k), windows(v)
    qb = q.astype(f32).reshape(Bsz, nb, Q_BLOCK, Hkv, G, dh)
    s = jnp.einsum('bnqhgd,bnkhd->bnhgqk', qb, kb) * dh ** -0.5
    dist = jnp.arange(Q_BLOCK)[:, None] - jnp.arange(kw_len)[None, :] + nback * Q_BLOCK
    kpos = (jnp.arange(nb)[:, None, None] - nback) * Q_BLOCK + jnp.arange(kw_len)[None, None, :]
    mask = (dist >= 0) & (dist < window) & (kpos >= 0)
    bias = jnp.transpose(bias_tab[t5_bucket(dist)], (2, 0, 1)).reshape(Hkv, G, Q_BLOCK, kw_len)
    sink = None if sinks is None else sinks.astype(f32).reshape(1, 1, Hkv, G, 1, 1)
    p = masked_softmax(s + bias, mask[None, :, None, None], sink)
    o = jnp.einsum('bnhgqk,bnkhd->bnqhgd', p, vb)
    return o.reshape(Bsz, S, Hq * dh).astype(q.dtype)


def diff_lambda_init(layer):
    return 0.8 - 0.6 * math.exp(-0.3 * layer)


def diff_attention(q, k, v, lq1, lk1, lq2, lk2, subln_g, lambda_init, bias_tab):
    f32 = jnp.float32
    Bsz, S, H, _, d = q.shape
    nb = S // Q_BLOCK
    lam = (jnp.exp(jnp.sum(lq1.astype(f32) * lk1.astype(f32)))
           - jnp.exp(jnp.sum(lq2.astype(f32) * lk2.astype(f32))) + lambda_init)
    kf = k.astype(f32)
    vf = v.astype(f32)
    qb = jnp.moveaxis(q.astype(f32).reshape(Bsz, nb, Q_BLOCK, H, 2, d), 1, 0)
    kpos = jnp.arange(S)

    def one_block(args):
        q_blk, n = args
        qpos = n * Q_BLOCK + jnp.arange(Q_BLOCK)
        dist = qpos[:, None] - kpos[None, :]
        bias = jnp.transpose(bias_tab[t5_bucket(dist)], (2, 0, 1))[None, :, None]
        s = jnp.einsum('bqhmd,bkhmd->bhmqk', q_blk, kf) * d ** -0.5 + bias
        p = masked_softmax(s, dist >= 0)
        attn = p[:, :, 0] - lam * p[:, :, 1]
        return jnp.einsum('bhqk,bkhe->bqhe', attn, vf)

    o = lax.map(one_block, (qb, jnp.arange(nb)))
    o = jnp.moveaxis(o, 0, 1).reshape(Bsz, S, H, 2 * d)
    o = rmsnorm(o, subln_g, DIFF_SUBLN_EPS) * (1.0 - lambda_init)
    return o.reshape(Bsz, S, H * 2 * d).astype(q.dtype)


def nsa_compress(t, pos_emb, w1, w2, cmp_idx):
    f32 = jnp.float32
    blocks = t.astype(f32)[:, cmp_idx] + pos_emb.astype(f32)
    flat = blocks.reshape(t.shape[0], cmp_idx.shape[0], -1)
    return jax.nn.gelu(flat @ w1.astype(f32)) @ w2.astype(f32)


def nsa_selection_aggregator(n_cmp, n_slc):
    ratio_s = NSA_SLC_LEN // NSA_CMP_STRIDE
    ratio_c = NSA_CMP_LEN // NSA_CMP_STRIDE
    w = np.convolve(np.ones(ratio_s), np.ones(ratio_c)).astype(np.float32)
    cj = np.arange(n_slc)[:, None] * ratio_s - (ratio_c - 1) + np.arange(w.size)[None, :]
    jj = np.broadcast_to(np.arange(n_slc)[:, None], cj.shape)
    ww = np.broadcast_to(w[None, :], cj.shape)
    keep = (cj >= 0) & (cj < n_cmp)
    agg = np.zeros((n_cmp, n_slc), np.float32)
    np.add.at(agg, (cj[keep], jj[keep]), ww[keep])
    return jnp.asarray(agg)


def nsa_attention(q, k_cmp, v_cmp, k_slc, v_slc, k_win, v_win, gates,
                  pos_k, w1_k, w2_k, pos_v, w1_v, w2_v, bias_tab):
    f32 = jnp.float32
    Bsz, S, H, dh = q.shape
    nb = S // Q_BLOCK
    n_cmp = (S - NSA_CMP_LEN) // NSA_CMP_STRIDE + 1
    n_slc = S // NSA_SLC_LEN
    top_n = min(NSA_TOPK, n_slc)
    cmp_idx = jnp.asarray(np.arange(n_cmp)[:, None] * NSA_CMP_STRIDE + np.arange(NSA_CMP_LEN)[None, :])
    kc = nsa_compress(k_cmp, pos_k, w1_k, w2_k, cmp_idx)
    vc = nsa_compress(v_cmp, pos_v, w1_v, w2_v, cmp_idx)
    cmp_end = jnp.arange(n_cmp) * NSA_CMP_STRIDE + NSA_CMP_LEN - 1
    agg = nsa_selection_aggregator(n_cmp, n_slc)
    slc_ids = jnp.arange(n_slc)
    slc_start = slc_ids * NSA_SLC_LEN
    ks = k_slc.astype(f32)
    vs = v_slc.astype(f32)
    scale = dh ** -0.5
    qb = jnp.moveaxis(q.astype(f32).reshape(Bsz, nb, Q_BLOCK, H, dh), 1, 0)

    def one_block(args):
        q_blk, n = args
        qpos = n * Q_BLOCK + jnp.arange(Q_BLOCK)
        dist_c = qpos[:, None] - cmp_end[None, :]
        bias_c = jnp.transpose(bias_tab[t5_bucket(dist_c)], (2, 0, 1))
        s_c = jnp.einsum('bqhd,bcd->bhqc', q_blk, kc) * scale + bias_c
        p_c = masked_softmax(s_c, dist_c >= 0)
        o_c = jnp.einsum('bhqc,bcd->bqhd', p_c, vc)
        imp = jnp.einsum('bhqc,cj->bqj', p_c, agg)
        cur = qpos // NSA_SLC_LEN
        valid = slc_start[None, :] <= qpos[:, None]
        forced = ((slc_ids[None, :] == 0) | (slc_ids[None, :] == cur[:, None])
                  | (slc_ids[None, :] == cur[:, None] - 1))
        imp = jnp.where(forced, NSA_FORCED_SCORE, jnp.where(valid, imp, -1.0))
        _, sel = lax.top_k(imp, top_n)
        kpos = (sel[..., None] * NSA_SLC_LEN + jnp.arange(NSA_SLC_LEN)).reshape(Bsz, Q_BLOCK, top_n * NSA_SLC_LEN)
        kg = jax.vmap(lambda kk, pp: kk[pp])(ks, kpos)
        vg = jax.vmap(lambda vv, pp: vv[pp])(vs, kpos)
        dist_s = qpos[None, :, None] - kpos
        bias_s = jnp.moveaxis(bias_tab[t5_bucket(dist_s)], -1, 1)
        s_s = jnp.einsum('bqhd,bqkd->bhqk', q_blk, kg) * scale + bias_s
        p_s = masked_softmax(s_s, (dist_s >= 0)[:, None])
        o_s = jnp.einsum('bhqk,bqkd->bqhd', p_s, vg)
        return o_c, o_s

    o_c, o_s = lax.map(one_block, (qb, jnp.arange(nb)))
    o_c = jnp.moveaxis(o_c, 0, 1).reshape(Bsz, S, H, dh)
    o_s = jnp.moveaxis(o_s, 0, 1).reshape(Bsz, S, H, dh)
    o_w = banded_attention(q, k_win[:, :, None], v_win[:, :, None], NSA_WINDOW, bias_tab)
    o_w = o_w.astype(f32).reshape(Bsz, S, H, dh)
    g = jax.nn.sigmoid(gates.astype(f32))
    o = g[..., 0:1] * o_c + g[..., 1:2] * o_s + g[..., 2:3] * o_w
    return o.reshape(Bsz, S, H * dh).astype(q.dtype)


def hier_moe(h, w_grp, b_grp, w_exp, b_exp, w_g, w_u, w_d):
    f32 = jnp.float32
    Bsz, S, D = h.shape
    t = h.reshape(-1, D)
    T = t.shape[0]
    grp_logits = (t @ w_grp).astype(f32) + b_grp.astype(f32)
    grp_prob = jax.nn.softmax(grp_logits, axis=-1)
    _, g_sel = lax.top_k(grp_logits, 1)
    g_w = jnp.take_along_axis(grp_prob, g_sel, axis=-1)
    exp_logits = ((t @ w_exp).astype(f32) + b_exp.astype(f32)).reshape(T, N_GROUPS, EXPERTS_PER_GROUP)
    in_grp = jnp.take_along_axis(exp_logits, g_sel[:, :, None], axis=1)[:, 0]
    top_v, top_i = lax.top_k(in_grp, TOPK_IN_GROUP)
    top_w = jax.nn.softmax(top_v, axis=-1) * g_w
    eid = g_sel * EXPERTS_PER_GROUP + top_i
    combine = jnp.sum(jax.nn.one_hot(eid, N_EXPERTS, dtype=f32) * top_w[..., None], axis=1)
    y = jnp.zeros((T, D), f32)
    for e in range(N_EXPERTS):
        he = jax.nn.silu(t @ w_g[e]) * (t @ w_u[e])
        y = y + combine[:, e:e + 1] * (he @ w_d[e]).astype(f32)
    return y.reshape(Bsz, S, D).astype(h.dtype)


def setup_inputs(seed: int = 0) -> dict:
    key = jax.random.key(seed)
    ks = iter(jax.random.split(key, 64))
    f32 = jnp.float32
    L = DEPTH

    def nrm(shape, scale):
        return jax.random.normal(next(ks), shape, f32) * scale

    def gain(shape):
        return 1.0 + nrm(shape, 0.02)

    x = nrm((BATCH, SEQ, D_MODEL), 1.0)
    u = jax.random.uniform(next(ks), (L, LRU_WIDTH), f32, LRU_A_MIN, LRU_A_MAX)
    a0 = u ** (1.0 / LRU_C)
    lru_lambda = jnp.log(a0) - jnp.log1p(-a0)
    flat_cmp = NSA_CMP_LEN * HEAD_DIM
    return {
        'x': x,
        'rel_bias': nrm((REL_BUCKETS, N_BIAS_HEADS), 0.5),
        'final_norm_g': gain((D_MODEL,)),
        'mix_norm_g': gain((L, D_MODEL)),
        'w_in': nrm((L, D_MODEL, N_IN), D_MODEL ** -0.5),
        'conv_w': nrm((L, CONV_WIDTH, LRU_WIDTH), CONV_WIDTH ** -0.5),
        'conv_b': nrm((L, LRU_WIDTH), 0.01),
        'lru_w_a': nrm((L, LRU_BLOCKS, LRU_BLOCK, LRU_BLOCK), LRU_BLOCK ** -0.5),
        'lru_b_a': nrm((L, LRU_WIDTH), 0.01),
        'lru_w_x': nrm((L, LRU_BLOCKS, LRU_BLOCK, LRU_BLOCK), LRU_BLOCK ** -0.5),
        'lru_b_x': nrm((L, LRU_WIDTH), 0.01),
        'lru_lambda': lru_lambda,
        'swa_sinks': nrm((L, SWA_HEADS), 0.5),
        'diff_lq1': nrm((L, DIFF_DIM), 0.1),
        'diff_lk1': nrm((L, DIFF_DIM), 0.1),
        'diff_lq2': nrm((L, DIFF_DIM), 0.1),
        'diff_lk2': nrm((L, DIFF_DIM), 0.1),
        'diff_subln_g': gain((L, 2 * DIFF_DIM)),
        'nsa_pos_k': nrm((L, NSA_CMP_LEN, HEAD_DIM), 0.02),
        'nsa_w1_k': nrm((L, flat_cmp, HEAD_DIM), flat_cmp ** -0.5),
        'nsa_w2_k': nrm((L, HEAD_DIM, HEAD_DIM), HEAD_DIM ** -0.5),
        'nsa_pos_v': nrm((L, NSA_CMP_LEN, HEAD_DIM), 0.02),
        'nsa_w1_v': nrm((L, flat_cmp, HEAD_DIM), flat_cmp ** -0.5),
        'nsa_w2_v': nrm((L, HEAD_DIM, HEAD_DIM), HEAD_DIM ** -0.5),
        'w_gate': nrm((L, N_BRANCH, D_MODEL, D_MODEL), D_MODEL ** -0.5),
        'b_gate': nrm((L, N_BRANCH, D_MODEL), 0.01),
        'w_branch': nrm((L, N_BRANCH, MIX_WIDTH, D_MODEL), MIX_WIDTH ** -0.5),
        'w_out': nrm((L, D_MODEL, D_MODEL), D_MODEL ** -0.5),
        'ffn_norm_g': gain((L, D_MODEL)),
        'w_router_grp': nrm((L, D_MODEL, N_GROUPS), D_MODEL ** -0.5),
        'b_router_grp': nrm((L, N_GROUPS), ROUTER_BIAS_SCALE),
        'w_router_exp': nrm((L, D_MODEL, N_EXPERTS), D_MODEL ** -0.5),
        'b_router_exp': nrm((L, N_EXPERTS), ROUTER_BIAS_SCALE),
        'w_exp_gate': nrm((L, N_EXPERTS, D_MODEL, EXPERT_FF), D_MODEL ** -0.5),
        'w_exp_up': nrm((L, N_EXPERTS, D_MODEL, EXPERT_FF), D_MODEL ** -0.5),
        'w_exp_down': nrm((L, N_EXPERTS, EXPERT_FF, D_MODEL), EXPERT_FF ** -0.5),
    }


def reference(x, rel_bias, final_norm_g, mix_norm_g, w_in, conv_w, conv_b, lru_w_a, lru_b_a,
              lru_w_x, lru_b_x, lru_lambda, swa_sinks, diff_lq1, diff_lk1, diff_lq2, diff_lk2,
              diff_subln_g, nsa_pos_k, nsa_w1_k, nsa_w2_k, nsa_pos_v, nsa_w1_v, nsa_w2_v,
              w_gate, b_gate, w_branch, w_out, ffn_norm_g, w_router_grp, b_router_grp,
              w_router_exp, b_router_exp, w_exp_gate, w_exp_up, w_exp_down):
    Bsz, S, _ = x.shape
    bias_f = rel_bias.astype(jnp.float32)
    bias_swa = bias_f[:, :SWA_HEADS]
    bias_diff = bias_f[:, SWA_HEADS:SWA_HEADS + DIFF_HEADS]
    bias_nsa = bias_f[:, SWA_HEADS + DIFF_HEADS:]
    for l in range(DEPTH):
        h = rmsnorm(x, mix_norm_g[l])
        (a_x, a_g, b_q, b_k, b_v, c_q, c_k, c_v, d_q,
         d_kc, d_vc, d_ks, d_vs, d_kw, d_vw, d_g) = split_columns(h @ w_in[l])
        y_a = rglru_mixer(a_x, a_g, conv_w[l], conv_b[l], lru_w_a[l], lru_b_a[l],
                          lru_w_x[l], lru_b_x[l], lru_lambda[l])
        y_b = banded_attention(b_q.reshape(Bsz, S, SWA_HEADS, HEAD_DIM),
                               b_k.reshape(Bsz, S, SWA_KV_HEADS, HEAD_DIM),
                               b_v.reshape(Bsz, S, SWA_KV_HEADS, HEAD_DIM),
                               SWA_WINDOW, bias_swa, swa_sinks[l])
        y_c = diff_attention(c_q.reshape(Bsz, S, DIFF_HEADS, 2, DIFF_DIM),
                             c_k.reshape(Bsz, S, DIFF_HEADS, 2, DIFF_DIM),
                             c_v.reshape(Bsz, S, DIFF_HEADS, 2 * DIFF_DIM),
                             diff_lq1[l], diff_lk1[l], diff_lq2[l], diff_lk2[l],
                             diff_subln_g[l], diff_lambda_init(l), bias_diff)
        y_d = nsa_attention(d_q.reshape(Bsz, S, NSA_HEADS, HEAD_DIM), d_kc, d_vc, d_ks, d_vs,
                            d_kw, d_vw, d_g.reshape(Bsz, S, NSA_HEADS, 3),
                            nsa_pos_k[l], nsa_w1_k[l], nsa_w2_k[l],
                            nsa_pos_v[l], nsa_w1_v[l], nsa_w2_v[l], bias_nsa)
        ys = (y_a, y_b, y_c, y_d)
        merged = jnp.zeros_like(x)
        for br in range(N_BRANCH):
            gate = jax.nn.sigmoid(h @ w_gate[l, br] + b_gate[l, br])
            merged = merged + gate * (ys[br] @ w_branch[l, br])
        x = x + merged @ w_out[l]
        x = x + hier_moe(rmsnorm(x, ffn_norm_g[l]), w_router_grp[l], b_router_grp[l],
                         w_router_exp[l], b_router_exp[l], w_exp_gate[l], w_exp_up[l], w_exp_down[l])
    return rmsnorm(x, final_norm_g)
```

```python
import functools
import math

import numpy as np
import jax
import jax.numpy as jnp
from jax import lax
from jax.experimental import pallas as pl
from jax.experimental.pallas import tpu as pltpu

F32 = jnp.float32
BF16 = jnp.bfloat16
I32 = jnp.int32

D_MODEL = 1024
HEAD_DIM = 64
MIX_WIDTH = 256
LRU_C = 8.0
CONV_WIDTH = 4
SWA_HEADS = 4
SWA_WINDOW = 128
DIFF_HEADS = 4
DIFF_DIM = 32
DIFF_SUBLN_EPS = 1e-5
NSA_HEADS = 4
NSA_CMP_LEN = 32
NSA_CMP_STRIDE = 16
NSA_SLC_LEN = 64
NSA_TOPK = 16
NSA_WINDOW = 512
NSA_FORCED_SCORE = 1e4
REL_BUCKETS = 32
REL_MAX_DIST = 128
N_GROUPS = 4
EXPERTS_PER_GROUP = 8
N_EXPERTS = 32
EXPERT_FF = 256
NORM_EPS = 1e-6
NEG_INF = -1e30
TINY = 1e-30

LANES = 128
Q_BLOCK = 128
FLASH_BLOCK = 256
N_IN_PAD = 2560
VMEM_LIMIT = 56 * 1024 * 1024

_NT = (((1,), (1,)), ((), ()))


def _cparams(*sem):
    return pltpu.CompilerParams(dimension_semantics=sem, vmem_limit_bytes=VMEM_LIMIT)


def _t5_bucket_np(dist):
    n = np.maximum(dist, 0)
    exact = REL_BUCKETS // 2
    scaled = (np.log(np.maximum(n, 1).astype(np.float32) / np.float32(exact))
              / np.float32(math.log(REL_MAX_DIST / exact))).astype(np.float32)
    large = np.minimum(exact + (scaled * np.float32(REL_BUCKETS - exact)).astype(np.int32), REL_BUCKETS - 1)
    return np.where(n < exact, n, large).astype(np.int32)


def _bias_tile_kernel(tab_ref, idx_ref, o_ref, *, head0, shift):
    h = pl.program_id(0) + head0
    idx = idx_ref[0]
    acc = jnp.zeros(idx.shape, F32)
    for b in range(REL_BUCKETS):
        acc = jnp.where(idx == b, tab_ref[b, h], acc)
    if shift:
        acc = acc - tab_ref[REL_BUCKETS - 1, h]
    o_ref[0, 0] = acc


def _bias_tiles(rel_bias, buckets, head0, nheads, shift):
    R, M, N = buckets.shape
    return pl.pallas_call(
        functools.partial(_bias_tile_kernel, head0=head0, shift=shift),
        grid=(nheads, R),
        in_specs=[pl.BlockSpec(memory_space=pltpu.SMEM),
                  pl.BlockSpec((1, M, N), lambda h, r: (r, 0, 0))],
        out_specs=pl.BlockSpec((1, 1, M, N), lambda h, r: (h, r, 0, 0)),
        out_shape=jax.ShapeDtypeStruct((nheads, R, M, N), F32),
        compiler_params=_cparams("arbitrary", "arbitrary"),
        name="bias_tiles",
    )(rel_bias, jnp.asarray(buckets))


def _banded_buckets(nback):
    i = np.arange(Q_BLOCK)[:, None]
    j = np.arange((nback + 1) * Q_BLOCK)[None, :]
    return _t5_bucket_np(i - j + nback * Q_BLOCK)[None]


def _flash_buckets():
    i = np.arange(FLASH_BLOCK)[:, None]
    j = np.arange(FLASH_BLOCK)[None, :]
    return np.stack([_t5_bucket_np(i - j), _t5_bucket_np(i - j + FLASH_BLOCK)])


def _cmp_buckets():
    i = np.arange(Q_BLOCK)[None, :, None]
    c2 = np.arange(2 * LANES)[None, None, :]
    r = np.arange(16)[:, None, None]
    dist = i - NSA_CMP_STRIDE * (c2 - LANES - 8 * r) - (NSA_CMP_LEN - 1)
    return _t5_bucket_np(dist)


def _rms(x, g):
    return x * lax.rsqrt(jnp.mean(x * x, axis=-1, keepdims=True) + NORM_EPS) * g


def _proj_kernel(x_ref, g_ref, w_ref, oa_ref, ob_ref, oc_ref, od_ref):
    h = _rms(x_ref[...], g_ref[...]).astype(BF16)
    oa_ref[...] = jnp.dot(h, w_ref[:, 0:512], preferred_element_type=F32)
    ob_ref[...] = jnp.dot(h, w_ref[:, 512:1024], preferred_element_type=F32).astype(BF16)
    oc_ref[...] = jnp.dot(h, w_ref[:, 1024:1792], preferred_element_type=F32).astype(BF16)
    od_ref[...] = jnp.dot(h, w_ref[:, 1792:2560], preferred_element_type=F32).astype(BF16)


def _in_proj(x, g, w):
    T = x.shape[0]
    tm = 512
    return pl.pallas_call(
        _proj_kernel,
        grid=(T // tm,),
        in_specs=[pl.BlockSpec((tm, D_MODEL), lambda i: (i, 0)),
                  pl.BlockSpec((1, D_MODEL), lambda i: (0, 0)),
                  pl.BlockSpec((D_MODEL, N_IN_PAD), lambda i: (0, 0))],
        out_specs=[pl.BlockSpec((tm, 512), lambda i: (i, 0)),
                   pl.BlockSpec((tm, 512), lambda i: (i, 0)),
                   pl.BlockSpec((tm, 768), lambda i: (i, 0)),
                   pl.BlockSpec((tm, 768), lambda i: (i, 0))],
        out_shape=[jax.ShapeDtypeStruct((T, 512), F32),
                   jax.ShapeDtypeStruct((T, 512), BF16),
                   jax.ShapeDtypeStruct((T, 768), BF16),
                   jax.ShapeDtypeStruct((T, 768), BF16)],
        compiler_params=_cparams("arbitrary"),
        name="in_proj",
    )(x, g, w)


def _lru_kernel(xg_ref, cw_ref, cb_ref, wa_ref, ba_ref, wx_ref, bx_ref, lam_ref, o_ref,
                ext_ref, h_ref, *, Tt):
    t = pl.program_id(1)

    W = MIX_WIDTH

    @pl.when(t == 0)
    def _():
        ext_ref[0:8, :] = jnp.zeros((8, W), F32)
        h_ref[...] = jnp.zeros_like(h_ref)

    x = xg_ref[:, 0:W]
    gate = xg_ref[:, W:2 * W]
    ext_ref[8:Tt + 8, :] = x
    row = lax.broadcasted_iota(I32, (Tt, W), 0)
    xc = cb_ref[...] + x * cw_ref[CONV_WIDTH - 1:CONV_WIDTH, :]
    for s in range(1, CONV_WIDTH):
        xc = xc + ext_ref[8 - s:8 - s + Tt, :] * cw_ref[CONV_WIDTH - 1 - s:CONV_WIDTH - s, :]
    ext_ref[0:8, :] = ext_ref[Tt:Tt + 8, :]

    xcb = xc.astype(BF16)
    r = jax.nn.sigmoid(jnp.dot(xcb, wa_ref[...], preferred_element_type=F32) + ba_ref[...])
    ig = jax.nn.sigmoid(jnp.dot(xcb, wx_ref[...], preferred_element_type=F32) + bx_ref[...])
    z = -lam_ref[...]
    softplus = jnp.maximum(z, 0.0) + jnp.log1p(jnp.exp(-jnp.abs(z)))
    log_a = (-LRU_C) * r * softplus
    a = jnp.exp(log_a)
    mult = jnp.sqrt(1.0 - a * a)
    mult = jnp.where((row + t * Tt) == 0, 1.0, mult)
    b = mult * ig * xc

    d = 1
    while d < Tt:
        keep = row >= d
        b = jnp.where(keep, a * pltpu.roll(b, d, 0) + b, b)
        a = jnp.where(keep, a * pltpu.roll(a, d, 0), a)
        d *= 2
    hs = a * h_ref[7:8, :] + b
    h_ref[...] = hs[Tt - 8:Tt]
    o_ref[...] = (hs * jax.nn.gelu(gate)).astype(o_ref.dtype)


def _rglru(a_xg, conv_w, conv_b, wa_bd, b_a, wx_bd, b_x, lam, B, S):
    Tt = 256
    W = MIX_WIDTH
    vec = pl.BlockSpec((1, W), lambda b, t: (0, 0))
    mat = pl.BlockSpec((W, W), lambda b, t: (0, 0))
    out = pl.pallas_call(
        functools.partial(_lru_kernel, Tt=Tt),
        grid=(B, S // Tt),
        in_specs=[pl.BlockSpec((None, Tt, 2 * W), lambda b, t: (b, t, 0)),
                  pl.BlockSpec((CONV_WIDTH, W), lambda b, t: (0, 0)),
                  vec, mat, vec, mat, vec, vec],
        out_specs=pl.BlockSpec((None, Tt, W), lambda b, t: (b, t, 0)),
        out_shape=jax.ShapeDtypeStruct((B, S, W), BF16),
        scratch_shapes=[pltpu.VMEM((Tt + 8, W), F32), pltpu.VMEM((8, W), F32)],
        compiler_params=_cparams("arbitrary", "arbitrary"),
        name="rglru",
    )(a_xg.reshape(B, S, 2 * W), conv_w, conv_b, wa_bd, b_a, wx_bd, b_x, lam)
    return out.reshape(B * S, W)


def _lane_window(q128, lo, width, roll, scale):
    q = q128 * scale
    if roll:
        q = pltpu.roll(q, LANES // 2, 1)
    lane = lax.broadcasted_iota(I32, q.shape, 1)
    return jnp.where((lane >= lo) & (lane < lo + width), q, 0.0).astype(BF16)


def _pair_halves(even, odd, even_half, odd_half):
    if even_half == 1:
        even = pltpu.roll(even, LANES // 2, 1)
    if odd_half == 0:
        odd = pltpu.roll(odd, LANES // 2, 1)
    lane = lax.broadcasted_iota(I32, even.shape, 1)
    return jnp.where(lane < LANES // 2, even, odd)


def _banded_kernel(*refs, nback, window, q_cfg, has_sink, shared_kv):
    nk = nback + 1
    q_ref = refs[0]
    k_refs = refs[1:1 + nk]
    pos = 1 + nk
    if shared_kv:
        v_refs = k_refs
    else:
        v_refs = refs[pos:pos + nk]
        pos += nk
    bias_ref = refs[pos]
    pos += 1
    sink_ref = None
    if has_sink:
        sink_ref = refs[pos]
        pos += 1
    o_ref = refs[pos]

    n = pl.program_id(1)
    KW = nk * Q_BLOCK
    kcat = jnp.concatenate([k_refs[nback - i][...] for i in range(nk)], axis=0)
    vcat = kcat if shared_kv else jnp.concatenate([v_refs[nback - i][...] for i in range(nk)], axis=0)
    ii = lax.broadcasted_iota(I32, (Q_BLOCK, KW), 0)
    jj = lax.broadcasted_iota(I32, (Q_BLOCK, KW), 1)
    dist = ii - jj + nback * Q_BLOCK
    valid = (dist >= 0) & (dist < window) & (jj >= (nback - n) * Q_BLOCK)
    scale = HEAD_DIM ** -0.5
    outs = []
    for h, (lo, roll, vhalf) in enumerate(q_cfg):
        g = h // 2
        qx = _lane_window(q_ref[:, LANES * g:LANES * (g + 1)].astype(F32), lo, HEAD_DIM, roll, scale)
        s = lax.dot_general(qx, kcat, _NT, preferred_element_type=F32) + bias_ref[h, 0]
        s = jnp.where(valid, s, NEG_INF)
        m = jnp.max(s, axis=1, keepdims=True)
        if has_sink:
            sink = sink_ref[0:1, h:h + 1]
            m = jnp.maximum(m, sink)
        p = jnp.exp(s - m)
        den = jnp.sum(p, axis=1, keepdims=True)
        if has_sink:
            den = den + jnp.exp(sink - m)
        o = jnp.dot(p.astype(BF16), vcat, preferred_element_type=F32)
        outs.append((o * (1.0 / jnp.maximum(den, TINY)), vhalf))
    for g in range(len(q_cfg) // 2):
        (oe, he), (oo, ho) = outs[2 * g], outs[2 * g + 1]
        o_ref[:, LANES * g:LANES * (g + 1)] = _pair_halves(oe, oo, he, ho).astype(o_ref.dtype)


def _banded_attention(qsrc, q_col, ksrc, k_col, vsrc, v_col, bias, sinks, *, B, S, window, q_cfg):
    nback = -(-window // Q_BLOCK)
    nk = nback + 1
    shared_kv = vsrc is None
    has_sink = sinks is not None

    def kv_spec(col, i):
        return pl.BlockSpec((None, Q_BLOCK, LANES), lambda b, n: (b, jnp.maximum(n - i, 0), col))

    in_specs = [pl.BlockSpec((None, Q_BLOCK, 2 * LANES), lambda b, n: (b, n, q_col))]
    args = [qsrc]
    in_specs += [kv_spec(k_col, i) for i in range(nk)]
    args += [ksrc] * nk
    if not shared_kv:
        in_specs += [kv_spec(v_col, i) for i in range(nk)]
        args += [vsrc] * nk
    in_specs.append(pl.BlockSpec(bias.shape, lambda b, n: (0, 0, 0, 0)))
    args.append(bias)
    if has_sink:
        in_specs.append(pl.BlockSpec(sinks.shape, lambda b, n: (0, 0)))
        args.append(sinks)
    return pl.pallas_call(
        functools.partial(_banded_kernel, nback=nback, window=window, q_cfg=q_cfg,
                          has_sink=has_sink, shared_kv=shared_kv),
        grid=(B, S // Q_BLOCK),
        in_specs=in_specs,
        out_specs=pl.BlockSpec((None, Q_BLOCK, 2 * LANES), lambda b, n: (b, n, 0)),
        out_shape=jax.ShapeDtypeStruct((B, S, 2 * LANES), BF16),
        compiler_params=_cparams("arbitrary", "arbitrary"),
        name="banded_attn",
    )(*args)


def _flash_kernel(*refs, mode, lambda_init):
    TB = FLASH_BLOCK
    if mode == "diff":
        q_ref, k_ref, v_ref, bias_ref, lam_ref, g_ref, o_ref, qx_ref, m_ref, l_ref, acc_ref = refs
        sel_ref = None
        maps = [dict(g=h // 2, lo=(h % 2) * 64 + 32 * mm, w=DIFF_DIM, roll=False, kg=h // 2, vg=h // 2, h=h)
                for h in range(DIFF_HEADS) for mm in range(2)]
        scale = DIFF_DIM ** -0.5
    else:
        q_ref, k_ref, sel_ref, bias_ref, o_ref, qx_ref, m_ref, l_ref, acc_ref = refs
        v_ref = k_ref
        maps = [dict(g=h // 2, lo=0, w=HEAD_DIM, roll=(h % 2 == 1), kg=0, vg=0, h=h) for h in range(NSA_HEADS)]
        scale = HEAD_DIM ** -0.5
    n = pl.program_id(1)

    for mp, c in enumerate(maps):
        qx_ref[mp] = _lane_window(q_ref[:, LANES * c["g"]:LANES * (c["g"] + 1)].astype(F32),
                                  c["lo"], c["w"], c["roll"], scale)
    m_ref[...] = jnp.full(m_ref.shape, NEG_INF, F32)
    l_ref[...] = jnp.zeros_like(l_ref)
    acc_ref[...] = jnp.zeros_like(acc_ref)

    row = lax.broadcasted_iota(I32, (TB, TB), 0)
    col = lax.broadcasted_iota(I32, (TB, TB), 1)

    def step(j, kind):
        koff = pl.multiple_of(j * TB, TB)
        valid = None
        if kind == "diag":
            valid = row >= col
        if sel_ref is not None:
            kk = lax.broadcasted_iota(I32, (TB, LANES), 0)
            cc = lax.broadcasted_iota(I32, (TB, LANES), 1)
            expand = jnp.where(((koff + kk) >> 6) == cc, 1.0, 0.0).astype(BF16)
            picked = lax.dot_general(sel_ref[...], expand, _NT, preferred_element_type=F32) > 0.5
            valid = picked if valid is None else (valid & picked)
        for mp, c in enumerate(maps):
            kt = k_ref[pl.ds(koff, TB), LANES * c["kg"]:LANES * (c["kg"] + 1)]
            s = lax.dot_general(qx_ref[mp], kt, _NT, preferred_element_type=F32)
            if kind == "near":
                s = s + bias_ref[c["h"], 1]
            elif kind == "diag":
                s = s + bias_ref[c["h"], 0]
            if valid is not None:
                s = jnp.where(valid, s, NEG_INF)
            m_prev = m_ref[mp]
            m_new = jnp.maximum(m_prev, jnp.max(s, axis=1, keepdims=True))
            alpha = jnp.exp(m_prev - m_new)
            p = jnp.exp(s - m_new)
            psum = p[:, 0:LANES]
            for t in range(1, TB // LANES):
                psum = psum + p[:, LANES * t:LANES * (t + 1)]
            l_ref[mp] = alpha * l_ref[mp] + psum
            vt = v_ref[pl.ds(koff, TB), LANES * c["vg"]:LANES * (c["vg"] + 1)]
            acc_ref[mp] = alpha * acc_ref[mp] + jnp.dot(p.astype(BF16), vt, preferred_element_type=F32)
            m_ref[mp] = m_new

    def far_body(j, carry):
        step(j, "far")
        return carry

    lax.fori_loop(0, jnp.maximum(n - 1, 0), far_body, 0)

    @pl.when(n >= 1)
    def _():
        step(n - 1, "near")

    step(n, "diag")

    lane = lax.broadcasted_iota(I32, (TB, LANES), 1)

    def normalised(mp):
        den = jnp.sum(l_ref[mp], axis=1, keepdims=True)
        return acc_ref[mp] * (1.0 / jnp.maximum(den, TINY))

    if mode == "diff":
        lq = lam_ref[...]
        lam = (jnp.exp(jnp.sum(lq[0:1] * lq[1:2], axis=1, keepdims=True))
               - jnp.exp(jnp.sum(lq[2:3] * lq[3:4], axis=1, keepdims=True)) + lambda_init)
        for g in range(DIFF_HEADS // 2):
            res = []
            for h in (2 * g, 2 * g + 1):
                o = normalised(2 * h) - lam * normalised(2 * h + 1)
                half = (lane >= 64 * (h % 2)) & (lane < 64 * (h % 2) + 64)
                ms = jnp.sum(jnp.where(half, o * o, 0.0), axis=1, keepdims=True) * (1.0 / (2 * DIFF_DIM))
                res.append(o * lax.rsqrt(ms + DIFF_SUBLN_EPS))
            out = jnp.where(lane < 64, res[0], res[1]) * g_ref[...] * (1.0 - lambda_init)
            o_ref[:, LANES * g:LANES * (g + 1)] = out.astype(o_ref.dtype)
    else:
        for g in range(NSA_HEADS // 2):
            out = _pair_halves(normalised(2 * g), normalised(2 * g + 1), 1, 1)
            o_ref[:, LANES * g:LANES * (g + 1)] = out.astype(o_ref.dtype)


def _flash_attention(mode, qsrc, q_col, ksrc, k_col, vsrc, v_col, bias, extra, *, B, S, lambda_init=0.0):
    TB = FLASH_BLOCK
    q_spec = pl.BlockSpec((None, TB, 2 * LANES), lambda b, n: (b, n, q_col))
    bias_spec = pl.BlockSpec(bias.shape, lambda b, n: (0, 0, 0, 0))
    if mode == "diff":
        lam4, gsub = extra
        nmaps = 2 * DIFF_HEADS
        in_specs = [q_spec,
                    pl.BlockSpec((None, S, 2 * LANES), lambda b, n: (b, 0, k_col)),
                    pl.BlockSpec((None, S, 2 * LANES), lambda b, n: (b, 0, v_col)),
                    bias_spec,
                    pl.BlockSpec(lam4.shape, lambda b, n: (0, 0)),
                    pl.BlockSpec(gsub.shape, lambda b, n: (0, 0))]
        args = [qsrc, ksrc, vsrc, bias, lam4, gsub]
    else:
        (sel,) = extra
        nmaps = NSA_HEADS
        in_specs = [q_spec,
                    pl.BlockSpec((None, S, LANES), lambda b, n: (b, 0, k_col)),
                    pl.BlockSpec((None, TB, LANES), lambda b, n: (b, n, 0)),
                    bias_spec]
        args = [qsrc, ksrc, sel, bias]
    return pl.pallas_call(
        functools.partial(_flash_kernel, mode=mode, lambda_init=lambda_init),
        grid=(B, S // TB),
        in_specs=in_specs,
        out_specs=pl.BlockSpec((None, TB, 2 * LANES), lambda b, n: (b, n, 0)),
        out_shape=jax.ShapeDtypeStruct((B, S, 2 * LANES), BF16),
        scratch_shapes=[pltpu.VMEM((nmaps, TB, LANES), BF16),
                        pltpu.VMEM((nmaps, TB, 1), F32),
                        pltpu.VMEM((nmaps, TB, LANES), F32),
                        pltpu.VMEM((nmaps, TB, LANES), F32)],
        compiler_params=_cparams("arbitrary", "arbitrary"),
        name="flash_" + mode,
    )(*args)


def _compress_kernel(ck_ref, cv_ref, pk_ref, pv_ref, w1k_ref, w1v_ref, w2_ref, o_ref):
    half = NSA_CMP_STRIDE * HEAD_DIM
    ck = ck_ref[...]
    cv = cv_ref[...]
    nrow = ck.shape[0]

    def mm(a, w):
        return jnp.dot(a, w, preferred_element_type=F32)

    top = mm(ck, w1k_ref[0:half, :]) + mm(cv, w1v_ref[0:half, :])
    bot = mm(ck, w1k_ref[half:2 * half, :]) + mm(cv, w1v_ref[half:2 * half, :])
    posw = (mm(pk_ref[...], w1k_ref[...]) + mm(pv_ref[...], w1v_ref[...]))[0:1]
    pre = top + pltpu.roll(bot, nrow - 1, 0) + posw
    kcv = mm(jax.nn.gelu(pre).astype(BF16), w2_ref[...])
    o_ref[0:LANES, :] = jnp.zeros((LANES, LANES), o_ref.dtype)
    o_ref[LANES:LANES + nrow, :] = kcv.astype(o_ref.dtype)


def _nsa_compress(ck, cv, pk, pv, w1k, w1v, w2, ncp):
    B, nchunk, width = ck.shape
    assert ncp == LANES + nchunk

    def full(a):
        return pl.BlockSpec(a.shape, lambda b: (0,) * a.ndim)

    chunk = pl.BlockSpec((None, nchunk, width), lambda b: (b, 0, 0))
    return pl.pallas_call(
        _compress_kernel,
        grid=(B,),
        in_specs=[chunk, chunk, full(pk), full(pv), full(w1k), full(w1v), full(w2)],
        out_specs=pl.BlockSpec((None, ncp, LANES), lambda b: (b, 0, 0)),
        out_shape=jax.ShapeDtypeStruct((B, ncp, LANES), BF16),
        compiler_params=_cparams("arbitrary"),
        name="nsa_compress",
    )(ck, cv, pk, pv, w1k, w1v, w2)


def _cmp_kernel(q_ref, kcv_ref, bias_ref, agg_ref, oc_ref, sel_ref, s_scr, *, n_cmp, ncp):
    n = pl.program_id(1)
    QB = Q_BLOCK
    kcv = kcv_ref[...]
    cp = lax.broadcasted_iota(I32, (QB, ncp), 1)
    qpos = n * QB + lax.broadcasted_iota(I32, (QB, ncp), 0)
    valid = ((cp >= LANES) & (cp < LANES + n_cmp)
             & (qpos - NSA_CMP_STRIDE * (cp - LANES) - (NSA_CMP_LEN - 1) >= 0))
    woff = pl.multiple_of((n // 16) * LANES, LANES)
    scale = HEAD_DIM ** -0.5
    psum = jnp.zeros((QB, ncp), F32)
    outs = []
    for h in range(NSA_HEADS):
        g = h // 2
        qx = _lane_window(q_ref[:, LANES * g:LANES * (g + 1)].astype(F32), 0, HEAD_DIM, h % 2 == 1, scale)
        s_scr[...] = lax.dot_general(qx, kcv, _NT, preferred_element_type=F32)
        s_scr[:, pl.ds(woff, 2 * LANES)] += bias_ref[h]
        s = jnp.where(valid, s_scr[...], NEG_INF)
        m = jnp.max(s, axis=1, keepdims=True)
        p = jnp.where(valid, jnp.exp(s - m), 0.0)
        den = jnp.sum(p, axis=1, keepdims=True)
        p = p * (1.0 / jnp.maximum(den, TINY))
        psum = psum + p
        outs.append(jnp.dot(p.astype(BF16), kcv, preferred_element_type=F32))
    for g in range(NSA_HEADS // 2):
        oc_ref[:, LANES * g:LANES * (g + 1)] = _pair_halves(outs[2 * g], outs[2 * g + 1], 1, 1).astype(oc_ref.dtype)

    imp = jnp.dot(psum.astype(BF16), agg_ref[...], preferred_element_type=F32)
    qp = n * QB + lax.broadcasted_iota(I32, (QB, LANES), 0)
    jb = lax.broadcasted_iota(I32, (QB, LANES), 1)
    cur = qp >> 6
    forced = (jb == 0) | (jb == cur) | (jb == cur - 1)
    imp = jnp.where(forced, NSA_FORCED_SCORE, jnp.where(jb * NSA_SLC_LEN <= qp, imp, -1.0))

    v = imp.T
    rid = lax.broadcasted_iota(I32, v.shape, 0).astype(F32)
    picked = jnp.zeros(v.shape, F32)
    for _ in range(NSA_TOPK):
        mx = jnp.max(v, axis=0, keepdims=True)
        first = jnp.min(jnp.where(v == mx, rid, float(LANES)), axis=0, keepdims=True)
        hit = rid == first
        picked = jnp.where(hit, 1.0, picked)
        v = jnp.where(hit, -jnp.inf, v)
    sel_ref[...] = picked.T.astype(sel_ref.dtype)


def _nsa_cmp_select(dsrc, kcv, bias, agg, *, B, S, n_cmp, ncp):
    return pl.pallas_call(
        functools.partial(_cmp_kernel, n_cmp=n_cmp, ncp=ncp),
        grid=(B, S // Q_BLOCK),
        in_specs=[pl.BlockSpec((None, Q_BLOCK, 2 * LANES), lambda b, n: (b, n, 0)),
                  pl.BlockSpec((None, ncp, LANES), lambda b, n: (b, 0, 0)),
                  pl.BlockSpec((NSA_HEADS, None, Q_BLOCK, 2 * LANES), lambda b, n: (0, n % 16, 0, 0)),
                  pl.BlockSpec((ncp, LANES), lambda b, n: (0, 0))],
        out_specs=[pl.BlockSpec((None, Q_BLOCK, 2 * LANES), lambda b, n: (b, n, 0)),
                   pl.BlockSpec((None, Q_BLOCK, LANES), lambda b, n: (b, n, 0))],
        out_shape=[jax.ShapeDtypeStruct((B, S, 2 * LANES), BF16),
                   jax.ShapeDtypeStruct((B, S, LANES), BF16)],
        scratch_shapes=[pltpu.VMEM((Q_BLOCK, ncp), F32)],
        compiler_params=_cparams("arbitrary", "arbitrary"),
        name="nsa_cmp_select",
    )(dsrc, kcv, bias, agg)


def _merge_kernel(x_ref, g_ref, ya_ref, yb_ref, yc_ref, oc_ref, os_ref, ow_ref, dg_ref,
                  wg_ref, bg_ref, wb_ref, wo_ref, o_ref):
    x = x_ref[...]
    h = _rms(x, g_ref[...]).astype(BF16)
    tm = x.shape[0]
    lane = lax.broadcasted_iota(I32, (tm, LANES), 1)
    gd = jax.nn.sigmoid(dg_ref[...].astype(F32))
    yd = []
    for g in range(NSA_HEADS // 2):
        acc = jnp.zeros((tm, LANES), F32)
        for br, src in enumerate((oc_ref, os_ref, ow_ref)):
            ge = gd[:, 3 * (2 * g) + br:3 * (2 * g) + br + 1]
            go = gd[:, 3 * (2 * g + 1) + br:3 * (2 * g + 1) + br + 1]
            acc = acc + jnp.where(lane < 64, ge, go) * src[:, LANES * g:LANES * (g + 1)].astype(F32)
        yd.append(acc)
    y_d = jnp.concatenate(yd, axis=1).astype(BF16)
    ys = (ya_ref[...], yb_ref[...], yc_ref[...], y_d)
    merged = jnp.zeros((tm, D_MODEL), F32)
    for br in range(4):
        gate = jax.nn.sigmoid(jnp.dot(h, wg_ref[br], preferred_element_type=F32) + bg_ref[br:br + 1, :])
        merged = merged + gate * jnp.dot(ys[br], wb_ref[br], preferred_element_type=F32)
    o_ref[...] = x + jnp.dot(merged.astype(BF16), wo_ref[...], preferred_element_type=F32)


def _merge(x, g, ya, yb, yc, oc, osel, ow, dproj, wg, bg, wb, wo):
    T = x.shape[0]
    tm = 256

    def tile(width, col=0):
        return pl.BlockSpec((tm, width), lambda i: (i, col))

    def full(a):
        return pl.BlockSpec(a.shape, lambda i: (0,) * a.ndim)

    return pl.pallas_call(
        _merge_kernel,
        grid=(T // tm,),
        in_specs=[tile(D_MODEL), full(g), tile(256), tile(256), tile(256), tile(256), tile(256), tile(256),
                  tile(LANES, 5), full(wg), full(bg), full(wb), full(wo)],
        out_specs=tile(D_MODEL),
        out_shape=jax.ShapeDtypeStruct((T, D_MODEL), F32),
        compiler_params=_cparams("arbitrary"),
        name="merge",
    )(x, g, ya, yb, yc, oc, osel, ow, dproj, wg, bg, wb, wo)


def _moe_kernel(x_ref, g_ref, wr_ref, br_ref, wg_ref, wu_ref, wd_ref, fg_ref, o_ref,
                hn_ref, comb_ref, acc_ref, *, final_norm):
    e = pl.program_id(1)
    tm = x_ref.shape[0]
    lane = lax.broadcasted_iota(I32, (tm, LANES), 1)

    @pl.when(e == 0)
    def _():
        hn = _rms(x_ref[...], g_ref[...])
        hn_ref[...] = hn.astype(BF16)
        logits = jnp.dot(hn, wr_ref[...], preferred_element_type=F32,
                         precision=lax.Precision.HIGHEST) + br_ref[...]
        is_grp = (lane >= N_EXPERTS) & (lane < N_EXPERTS + N_GROUPS)
        lanef = lane.astype(F32)
        lg = jnp.where(is_grp, logits, -jnp.inf)
        mg = jnp.max(lg, axis=1, keepdims=True)
        gsel = jnp.min(jnp.where(lg == mg, lanef, 2.0 * LANES), axis=1, keepdims=True) - N_EXPERTS
        g_w = 1.0 / jnp.sum(jnp.where(is_grp, jnp.exp(lg - mg), 0.0), axis=1, keepdims=True)
        in_grp = (lane < N_EXPERTS) & ((lane >> 3).astype(F32) == gsel)
        v1 = jnp.where(in_grp, logits, -jnp.inf)
        m1 = jnp.max(v1, axis=1, keepdims=True)
        i1 = jnp.min(jnp.where(v1 == m1, lanef, 2.0 * LANES), axis=1, keepdims=True)
        v2 = jnp.where(lanef == i1, -jnp.inf, v1)
        m2 = jnp.max(v2, axis=1, keepdims=True)
        i2 = jnp.min(jnp.where(v2 == m2, lanef, 2.0 * LANES), axis=1, keepdims=True)
        e2 = jnp.exp(m2 - m1)
        w1 = g_w / (1.0 + e2)
        w2 = g_w * e2 / (1.0 + e2)
        comb_ref[...] = jnp.where(lanef == i1, w1, 0.0) + jnp.where(lanef == i2, w2, 0.0)
        acc_ref[...] = jnp.zeros_like(acc_ref)

    hn = hn_ref[...]
    gt = jnp.dot(hn, wg_ref[...], preferred_element_type=F32)
    up = jnp.dot(hn, wu_ref[...], preferred_element_type=F32)
    he = (gt * jax.nn.sigmoid(gt) * up).astype(BF16)
    ye = jnp.dot(he, wd_ref[...], preferred_element_type=F32)
    cw = jnp.sum(jnp.where(lane == e, comb_ref[...], 0.0), axis=1, keepdims=True)
    acc_ref[...] += cw * ye

    @pl.when(e == N_EXPERTS - 1)
    def _():
        y = x_ref[...] + acc_ref[...]
        if final_norm:
            y = _rms(y, fg_ref[...])
        o_ref[...] = y


def _moe(x, g, w_router, b_router, wg, wu, wd, final_g, final_norm):
    T = x.shape[0]
    tm = 1024
    return pl.pallas_call(
        functools.partial(_moe_kernel, final_norm=final_norm),
        grid=(T // tm, N_EXPERTS),
        in_specs=[pl.BlockSpec((tm, D_MODEL), lambda i, e: (i, 0)),
                  pl.BlockSpec((1, D_MODEL), lambda i, e: (0, 0)),
                  pl.BlockSpec((D_MODEL, LANES), lambda i, e: (0, 0)),
                  pl.BlockSpec((1, LANES), lambda i, e: (0, 0)),
                  pl.BlockSpec((None, D_MODEL, EXPERT_FF), lambda i, e: (e, 0, 0)),
                  pl.BlockSpec((None, D_MODEL, EXPERT_FF), lambda i, e: (e, 0, 0)),
                  pl.BlockSpec((None, EXPERT_FF, D_MODEL), lambda i, e: (e, 0, 0)),
                  pl.BlockSpec((1, D_MODEL), lambda i, e: (0, 0))],
        out_specs=pl.BlockSpec((tm, D_MODEL), lambda i, e: (i, 0)),
        out_shape=jax.ShapeDtypeStruct((T, D_MODEL), F32),
        scratch_shapes=[pltpu.VMEM((tm, D_MODEL), BF16),
                        pltpu.VMEM((tm, LANES), F32),
                        pltpu.VMEM((tm, D_MODEL), F32)],
        compiler_params=_cparams("arbitrary", "arbitrary"),
        name="moe",
    )(x, g, w_router, b_router, wg, wu, wd, final_g)


def _block_diag(w):
    nb, bs, _ = w.shape
    out = jnp.zeros((nb * bs, nb * bs), w.dtype)
    for i in range(nb):
        out = out.at[i * bs:(i + 1) * bs, i * bs:(i + 1) * bs].set(w[i])
    return out


def _selection_aggregator_np(n_cmp, n_slc):
    ratio_s = NSA_SLC_LEN // NSA_CMP_STRIDE
    ratio_c = NSA_CMP_LEN // NSA_CMP_STRIDE
    w = np.convolve(np.ones(ratio_s), np.ones(ratio_c)).astype(np.float32)
    cj = np.arange(n_slc)[:, None] * ratio_s - (ratio_c - 1) + np.arange(w.size)[None, :]
    jj = np.broadcast_to(np.arange(n_slc)[:, None], cj.shape)
    ww = np.broadcast_to(w[None, :], cj.shape)
    keep = (cj >= 0) & (cj < n_cmp)
    agg = np.zeros((n_cmp, n_slc), np.float32)
    np.add.at(agg, (cj[keep], jj[keep]), ww[keep])
    return agg


def _row(v):
    return v.reshape(1, -1).astype(F32)


def _diff_lambda_init(layer):
    return 0.8 - 0.6 * math.exp(-0.3 * layer)


def _mixer_layer(x, l, B, S, tiles, p):
    (bias_swa, bias_diff, bias_slc, bias_win, bias_cmp, agg_pad, n_cmp, ncp) = tiles
    w_in = jnp.pad(p["w_in"][l], ((0, 0), (0, N_IN_PAD - p["w_in"].shape[2]))).astype(BF16)
    a_xg, b_qkv, c_qkv, d_all = _in_proj(x, _row(p["mix_norm_g"][l]), w_in)

    y_a = _rglru(a_xg, p["conv_w"][l].astype(F32), _row(p["conv_b"][l]),
                 _block_diag(p["lru_w_a"][l]).astype(BF16), _row(p["lru_b_a"][l]),
                 _block_diag(p["lru_w_x"][l]).astype(BF16), _row(p["lru_b_x"][l]),
                 _row(p["lru_lambda"][l]), B, S)

    b3 = b_qkv.reshape(B, S, 512)
    swa_cfg = tuple(((h // 2) * 64, (h % 2) != (h // 2), h // 2) for h in range(SWA_HEADS))
    y_b = _banded_attention(b3, 0, b3, 2, b3, 3, bias_swa, _row(p["swa_sinks"][l]),
                            B=B, S=S, window=SWA_WINDOW, q_cfg=swa_cfg)

    c3 = c_qkv.reshape(B, S, 768)
    lam4 = jnp.stack([p["diff_lq1"][l], p["diff_lk1"][l], p["diff_lq2"][l], p["diff_lk2"][l]]).astype(F32)
    gsub = jnp.tile(_row(p["diff_subln_g"][l]), (1, 2))
    y_c = _flash_attention("diff", c3, 0, c3, 1, c3, 2, bias_diff, (lam4, gsub), B=B, S=S,
                           lambda_init=_diff_lambda_init(l))

    d3 = d_all.reshape(B, S, 768)
    nchunk = S // NSA_CMP_STRIDE
    ck = d3[:, :, 256:320].reshape(B, nchunk, NSA_CMP_STRIDE * HEAD_DIM)
    cv = d3[:, :, 320:384].reshape(B, nchunk, NSA_CMP_STRIDE * HEAD_DIM)

    def pos8(a):
        return jnp.broadcast_to(a.reshape(1, -1), (8, a.size)).astype(BF16)

    w1k = jnp.pad(p["nsa_w1_k"][l], ((0, 0), (0, HEAD_DIM))).astype(BF16)
    w1v = jnp.pad(p["nsa_w1_v"][l], ((0, 0), (HEAD_DIM, 0))).astype(BF16)
    w2 = _block_diag(jnp.stack([p["nsa_w2_k"][l], p["nsa_w2_v"][l]])).astype(BF16)
    kcv = _nsa_compress(ck, cv, pos8(p["nsa_pos_k"][l]), pos8(p["nsa_pos_v"][l]), w1k, w1v, w2, ncp)
    o_c, sel = _nsa_cmp_select(d3, kcv, bias_cmp, agg_pad, B=B, S=S, n_cmp=n_cmp, ncp=ncp)
    o_s = _flash_attention("nsa", d3, 0, d3, 3, None, None, bias_slc, (sel,), B=B, S=S)
    win_cfg = tuple((0, h % 2 == 1, 1) for h in range(NSA_HEADS))
    o_w = _banded_attention(d3, 0, d3, 4, None, None, bias_win, None, B=B, S=S, window=NSA_WINDOW, q_cfg=win_cfg)

    T = B * S
    return _merge(x, _row(p["mix_norm_g"][l]), y_a, y_b.reshape(T, 256), y_c.reshape(T, 256),
                  o_c.reshape(T, 256), o_s.reshape(T, 256), o_w.reshape(T, 256), d_all,
                  p["w_gate"][l].astype(BF16), p["b_gate"][l].astype(F32), p["w_branch"][l].astype(BF16),
                  p["w_out"][l].astype(BF16))


def _moe_layer(x, l, p, final_norm):
    w_router = jnp.concatenate([p["w_router_exp"][l], p["w_router_grp"][l]], axis=1)
    w_router = jnp.pad(w_router, ((0, 0), (0, LANES - w_router.shape[1]))).astype(F32)
    b_router = jnp.concatenate([p["b_router_exp"][l], p["b_router_grp"][l]])
    b_router = jnp.pad(b_router, (0, LANES - b_router.shape[0])).reshape(1, LANES).astype(F32)
    return _moe(x, _row(p["ffn_norm_g"][l]), w_router, b_router, p["w_exp_gate"][l].astype(BF16),
                p["w_exp_up"][l].astype(BF16), p["w_exp_down"][l].astype(BF16), _row(p["final_norm_g"]), final_norm)


def _forward(p):
    x = p["x"]
    B, S, D = x.shape
    depth = p["w_in"].shape[0]
    assert D == D_MODEL and S % (16 * Q_BLOCK) == 0 and S // NSA_SLC_LEN <= LANES
    rel = p["rel_bias"].astype(F32)
    n_cmp = (S - NSA_CMP_LEN) // NSA_CMP_STRIDE + 1
    ncp = LANES + S // NSA_CMP_STRIDE
    agg = _selection_aggregator_np(n_cmp, S // NSA_SLC_LEN)
    agg_pad = np.zeros((ncp, LANES), np.float32)
    agg_pad[LANES:LANES + n_cmp, :agg.shape[1]] = agg
    tiles = (
        _bias_tiles(rel, _banded_buckets(1), 0, SWA_HEADS, False),
        _bias_tiles(rel, _flash_buckets(), SWA_HEADS, DIFF_HEADS, True),
        _bias_tiles(rel, _flash_buckets(), SWA_HEADS + DIFF_HEADS, NSA_HEADS, True),
        _bias_tiles(rel, _banded_buckets(NSA_WINDOW // Q_BLOCK), SWA_HEADS + DIFF_HEADS, NSA_HEADS, False),
        _bias_tiles(rel, _cmp_buckets(), SWA_HEADS + DIFF_HEADS, NSA_HEADS, True),
        jnp.asarray(agg_pad, BF16), n_cmp, ncp,
    )
    xt = x.reshape(B * S, D).astype(F32)
    for l in range(depth):
        xt = _mixer_layer(xt, l, B, S, tiles, p)
        xt = _moe_layer(xt, l, p, final_norm=(l == depth - 1))
    return xt.reshape(B, S, D).astype(x.dtype)


def kernel(x, rel_bias, final_norm_g, mix_norm_g, w_in, conv_w, conv_b, lru_w_a, lru_b_a, lru_w_x, lru_b_x,
           lru_lambda, swa_sinks, diff_lq1, diff_lk1, diff_lq2, diff_lk2, diff_subln_g, nsa_pos_k, nsa_w1_k,
           nsa_w2_k, nsa_pos_v, nsa_w1_v, nsa_w2_v, w_gate, b_gate, w_branch, w_out, ffn_norm_g, w_router_grp,
           b_router_grp, w_router_exp, b_router_exp, w_exp_gate, w_exp_up, w_exp_down):
    return _forward(dict(
        x=x, rel_bias=rel_bias, final_norm_g=final_norm_g, mix_norm_g=mix_norm_g, w_in=w_in, conv_w=conv_w,
        conv_b=conv_b, lru_w_a=lru_w_a, lru_b_a=lru_b_a, lru_w_x=lru_w_x, lru_b_x=lru_b_x, lru_lambda=lru_lambda,
        swa_sinks=swa_sinks, diff_lq1=diff_lq1, diff_lk1=diff_lk1, diff_lq2=diff_lq2, diff_lk2=diff_lk2,
        diff_subln_g=diff_subln_g, nsa_pos_k=nsa_pos_k, nsa_w1_k=nsa_w1_k, nsa_w2_k=nsa_w2_k, nsa_pos_v=nsa_pos_v,
        nsa_w1_v=nsa_w1_v, nsa_w2_v=nsa_w2_v, w_gate=w_gate, b_gate=b_gate, w_branch=w_branch, w_out=w_out,
        ffn_norm_g=ffn_norm_g, w_router_grp=w_router_grp, b_router_grp=b_router_grp, w_router_exp=w_router_exp,
        b_router_exp=b_router_exp, w_exp_gate=w_exp_gate, w_exp_up=w_exp_up, w_exp_down=w_exp_down))
```

```python
import functools
import math

import numpy as np
import jax
import jax.numpy as jnp
from jax import lax
from jax.experimental import pallas as pl
from jax.experimental.pallas import tpu as pltpu

F32 = jnp.float32
BF16 = jnp.bfloat16
I32 = jnp.int32

D_MODEL = 1024
HEAD_DIM = 64
MIX_WIDTH = 256
LRU_C = 8.0
CONV_WIDTH = 4
SWA_HEADS = 4
SWA_WINDOW = 128
DIFF_HEADS = 4
DIFF_DIM = 32
DIFF_SUBLN_EPS = 1e-5
NSA_HEADS = 4
NSA_CMP_LEN = 32
NSA_CMP_STRIDE = 16
NSA_SLC_LEN = 64
NSA_TOPK = 16
NSA_WINDOW = 512
NSA_FORCED_SCORE = 1e4
REL_BUCKETS = 32
REL_MAX_DIST = 128
N_GROUPS = 4
EXPERTS_PER_GROUP = 8
N_EXPERTS = 32
EXPERT_FF = 256
NORM_EPS = 1e-6
NEG_INF = -1e30
TINY = 1e-30
LOG2E = math.log2(math.e)

LANES = 128
Q_BLOCK = 128
FLASH_BLOCK = 256
N_IN_PAD = 2560
VMEM_LIMIT = 56 * 1024 * 1024

_NT = (((1,), (1,)), ((), ()))


def _cparams(*sem):
    return pltpu.CompilerParams(dimension_semantics=sem, vmem_limit_bytes=VMEM_LIMIT)


def _t5_bucket_np(dist):
    n = np.maximum(dist, 0)
    exact = REL_BUCKETS // 2
    scaled = (np.log(np.maximum(n, 1).astype(np.float32) / np.float32(exact))
              / np.float32(math.log(REL_MAX_DIST / exact))).astype(np.float32)
    large = np.minimum(exact + (scaled * np.float32(REL_BUCKETS - exact)).astype(np.int32), REL_BUCKETS - 1)
    return np.where(n < exact, n, large).astype(np.int32)


def _bias_tile_kernel(tab_ref, idx_ref, o_ref, *, head0, shift, scale):
    h = pl.program_id(0) + head0
    idx = idx_ref[0]
    acc = jnp.zeros(idx.shape, F32)
    for b in range(REL_BUCKETS):
        acc = jnp.where(idx == b, tab_ref[b, h], acc)
    if shift:
        acc = acc - tab_ref[REL_BUCKETS - 1, h]
    if scale != 1.0:
        acc = acc * scale
    o_ref[0, 0] = acc


def _bias_tiles(rel_bias, buckets, head0, nheads, shift, scale=1.0):
    R, M, N = buckets.shape
    return pl.pallas_call(
        functools.partial(_bias_tile_kernel, head0=head0, shift=shift, scale=scale),
        grid=(nheads, R),
        in_specs=[pl.BlockSpec(memory_space=pltpu.SMEM),
                  pl.BlockSpec((1, M, N), lambda h, r: (r, 0, 0))],
        out_specs=pl.BlockSpec((1, 1, M, N), lambda h, r: (h, r, 0, 0)),
        out_shape=jax.ShapeDtypeStruct((nheads, R, M, N), F32),
        compiler_params=_cparams("arbitrary", "arbitrary"),
        name="bias_tiles",
    )(rel_bias, jnp.asarray(buckets))


def _banded_buckets(nback):
    i = np.arange(Q_BLOCK)[:, None]
    j = np.arange((nback + 1) * Q_BLOCK)[None, :]
    return _t5_bucket_np(i - j + nback * Q_BLOCK)[None]


def _flash_buckets():
    k = np.arange(FLASH_BLOCK)[:, None]
    q = np.arange(FLASH_BLOCK)[None, :]
    return np.stack([_t5_bucket_np(q - k), _t5_bucket_np(q - k + FLASH_BLOCK)])


def _cmp_buckets():
    i = np.arange(Q_BLOCK)[None, :, None]
    c2 = np.arange(2 * LANES)[None, None, :]
    r = np.arange(16)[:, None, None]
    dist = i - NSA_CMP_STRIDE * (c2 - LANES - 8 * r) - (NSA_CMP_LEN - 1)
    return _t5_bucket_np(dist)


def _rms(x, g):
    return x * lax.rsqrt(jnp.mean(x * x, axis=-1, keepdims=True) + NORM_EPS) * g


def _proj_kernel(x_ref, g_ref, w_ref, wt_ref, oa_ref, ob_ref, oc_ref, od_ref, ot_ref):
    h = _rms(x_ref[...], g_ref[...]).astype(BF16)
    oa_ref[...] = jnp.dot(h, w_ref[:, 0:512], preferred_element_type=F32)
    ob_ref[...] = jnp.dot(h, w_ref[:, 512:1024], preferred_element_type=F32).astype(BF16)
    oc_ref[...] = jnp.dot(h, w_ref[:, 1024:1792], preferred_element_type=F32).astype(BF16)
    od_ref[...] = jnp.dot(h, w_ref[:, 1792:2560], preferred_element_type=F32).astype(BF16)
    ot_ref[...] = lax.dot_general(wt_ref[...], h, _NT, preferred_element_type=F32).astype(BF16)


def _in_proj(x, g, w, wt):
    T = x.shape[0]
    tm = 512
    nt = wt.shape[0]
    return pl.pallas_call(
        _proj_kernel,
        grid=(T // tm,),
        in_specs=[pl.BlockSpec((tm, D_MODEL), lambda i: (i, 0)),
                  pl.BlockSpec((1, D_MODEL), lambda i: (0, 0)),
                  pl.BlockSpec((D_MODEL, N_IN_PAD), lambda i: (0, 0)),
                  pl.BlockSpec((nt, D_MODEL), lambda i: (0, 0))],
        out_specs=[pl.BlockSpec((tm, 512), lambda i: (i, 0)),
                   pl.BlockSpec((tm, 512), lambda i: (i, 0)),
                   pl.BlockSpec((tm, 768), lambda i: (i, 0)),
                   pl.BlockSpec((tm, 768), lambda i: (i, 0)),
                   pl.BlockSpec((nt, tm), lambda i: (0, i))],
        out_shape=[jax.ShapeDtypeStruct((T, 512), F32),
                   jax.ShapeDtypeStruct((T, 512), BF16),
                   jax.ShapeDtypeStruct((T, 768), BF16),
                   jax.ShapeDtypeStruct((T, 768), BF16),
                   jax.ShapeDtypeStruct((nt, T), BF16)],
        compiler_params=_cparams("arbitrary"),
        name="in_proj",
    )(x, g, w, wt)


def _lru_kernel(xg_ref, cw_ref, cb_ref, wa_ref, ba_ref, wx_ref, bx_ref, lam_ref, o_ref,
                ext_ref, h_ref, *, Tt):
    t = pl.program_id(1)

    W = MIX_WIDTH

    @pl.when(t == 0)
    def _():
        ext_ref[0:8, :] = jnp.zeros((8, W), F32)
        h_ref[...] = jnp.zeros_like(h_ref)

    x = xg_ref[:, 0:W]
    gate = xg_ref[:, W:2 * W]
    ext_ref[8:Tt + 8, :] = x
    row = lax.broadcasted_iota(I32, (Tt, W), 0)
    xc = cb_ref[...] + x * cw_ref[CONV_WIDTH - 1:CONV_WIDTH, :]
    for s in range(1, CONV_WIDTH):
        xc = xc + ext_ref[8 - s:8 - s + Tt, :] * cw_ref[CONV_WIDTH - 1 - s:CONV_WIDTH - s, :]
    ext_ref[0:8, :] = ext_ref[Tt:Tt + 8, :]

    xcb = xc.astype(BF16)
    r = jax.nn.sigmoid(jnp.dot(xcb, wa_ref[...], preferred_element_type=F32) + ba_ref[...])
    ig = jax.nn.sigmoid(jnp.dot(xcb, wx_ref[...], preferred_element_type=F32) + bx_ref[...])
    z = -lam_ref[...]
    softplus = jnp.maximum(z, 0.0) + jnp.log1p(jnp.exp(-jnp.abs(z)))
    log_a = (-LRU_C) * r * softplus
    a = jnp.exp(log_a)
    mult = jnp.sqrt(1.0 - a * a)
    mult = jnp.where((row + t * Tt) == 0, 1.0, mult)
    b = mult * ig * xc

    d = 1
    while d < Tt:
        keep = row >= d
        b = jnp.where(keep, a * pltpu.roll(b, d, 0) + b, b)
        a = jnp.where(keep, a * pltpu.roll(a, d, 0), a)
        d *= 2
    hs = a * h_ref[7:8, :] + b
    h_ref[...] = hs[Tt - 8:Tt]
    o_ref[...] = (hs * jax.nn.gelu(gate)).astype(o_ref.dtype)


def _rglru(a_xg, conv_w, conv_b, wa_bd, b_a, wx_bd, b_x, lam, B, S):
    Tt = 256
    W = MIX_WIDTH
    vec = pl.BlockSpec((1, W), lambda b, t: (0, 0))
    mat = pl.BlockSpec((W, W), lambda b, t: (0, 0))
    out = pl.pallas_call(
        functools.partial(_lru_kernel, Tt=Tt),
        grid=(B, S // Tt),
        in_specs=[pl.BlockSpec((None, Tt, 2 * W), lambda b, t: (b, t, 0)),
                  pl.BlockSpec((CONV_WIDTH, W), lambda b, t: (0, 0)),
                  vec, mat, vec, mat, vec, vec],
        out_specs=pl.BlockSpec((None, Tt, W), lambda b, t: (b, t, 0)),
        out_shape=jax.ShapeDtypeStruct((B, S, W), BF16),
        scratch_shapes=[pltpu.VMEM((Tt + 8, W), F32), pltpu.VMEM((8, W), F32)],
        compiler_params=_cparams("arbitrary", "arbitrary"),
        name="rglru",
    )(a_xg.reshape(B, S, 2 * W), conv_w, conv_b, wa_bd, b_a, wx_bd, b_x, lam)
    return out.reshape(B * S, W)


def _lane_window(q128, lo, width, roll, scale):
    q = q128 * scale
    if roll:
        q = pltpu.roll(q, LANES // 2, 1)
    lane = lax.broadcasted_iota(I32, q.shape, 1)
    return jnp.where((lane >= lo) & (lane < lo + width), q, 0.0).astype(BF16)


def _pair_halves(even, odd, even_half, odd_half):
    if even_half == 1:
        even = pltpu.roll(even, LANES // 2, 1)
    if odd_half == 0:
        odd = pltpu.roll(odd, LANES // 2, 1)
    lane = lax.broadcasted_iota(I32, even.shape, 1)
    return jnp.where(lane < LANES // 2, even, odd)


def _banded_kernel(*refs, nback, window, q_cfg, has_sink, shared_kv):
    nk = nback + 1
    q_ref = refs[0]
    k_refs = refs[1:1 + nk]
    pos = 1 + nk
    if shared_kv:
        v_refs = k_refs
    else:
        v_refs = refs[pos:pos + nk]
        pos += nk
    bias_ref = refs[pos]
    pos += 1
    sink_ref = None
    if has_sink:
        sink_ref = refs[pos]
        pos += 1
    o_ref = refs[pos]

    n = pl.program_id(1)
    KW = nk * Q_BLOCK
    kcat = jnp.concatenate([k_refs[nback - i][...] for i in range(nk)], axis=0)
    vcat = kcat if shared_kv else jnp.concatenate([v_refs[nback - i][...] for i in range(nk)], axis=0)
    ii = lax.broadcasted_iota(I32, (Q_BLOCK, KW), 0)
    jj = lax.broadcasted_iota(I32, (Q_BLOCK, KW), 1)
    dist = ii - jj + nback * Q_BLOCK
    valid = (dist >= 0) & (dist < window) & (jj >= (nback - n) * Q_BLOCK)
    scale = HEAD_DIM ** -0.5
    outs = []
    for h, (lo, roll, vhalf) in enumerate(q_cfg):
        g = h // 2
        qx = _lane_window(q_ref[:, LANES * g:LANES * (g + 1)].astype(F32), lo, HEAD_DIM, roll, scale)
        s = lax.dot_general(qx, kcat, _NT, preferred_element_type=F32) + bias_ref[h, 0]
        s = jnp.where(valid, s, NEG_INF)
        m = jnp.max(s, axis=1, keepdims=True)
        if has_sink:
            sink = sink_ref[0:1, h:h + 1]
            m = jnp.maximum(m, sink)
        p = jnp.exp(s - m)
        den = jnp.sum(p, axis=1, keepdims=True)
        if has_sink:
            den = den + jnp.exp(sink - m)
        o = jnp.dot(p.astype(BF16), vcat, preferred_element_type=F32)
        outs.append((o * (1.0 / jnp.maximum(den, TINY)), vhalf))
    for g in range(len(q_cfg) // 2):
        (oe, he), (oo, ho) = outs[2 * g], outs[2 * g + 1]
        o_ref[:, LANES * g:LANES * (g + 1)] = _pair_halves(oe, oo, he, ho).astype(o_ref.dtype)


def _banded_attention(qsrc, q_col, ksrc, k_col, vsrc, v_col, bias, sinks, *, B, S, window, q_cfg):
    nback = -(-window // Q_BLOCK)
    nk = nback + 1
    shared_kv = vsrc is None
    has_sink = sinks is not None

    def kv_spec(col, i):
        return pl.BlockSpec((None, Q_BLOCK, LANES), lambda b, n: (b, jnp.maximum(n - i, 0), col))

    in_specs = [pl.BlockSpec((None, Q_BLOCK, 2 * LANES), lambda b, n: (b, n, q_col))]
    args = [qsrc]
    in_specs += [kv_spec(k_col, i) for i in range(nk)]
    args += [ksrc] * nk
    if not shared_kv:
        in_specs += [kv_spec(v_col, i) for i in range(nk)]
        args += [vsrc] * nk
    in_specs.append(pl.BlockSpec(bias.shape, lambda b, n: (0, 0, 0, 0)))
    args.append(bias)
    if has_sink:
        in_specs.append(pl.BlockSpec(sinks.shape, lambda b, n: (0, 0)))
        args.append(sinks)
    return pl.pallas_call(
        functools.partial(_banded_kernel, nback=nback, window=window, q_cfg=q_cfg,
                          has_sink=has_sink, shared_kv=shared_kv),
        grid=(B, S // Q_BLOCK),
        in_specs=in_specs,
        out_specs=pl.BlockSpec((None, Q_BLOCK, 2 * LANES), lambda b, n: (b, n, 0)),
        out_shape=jax.ShapeDtypeStruct((B, S, 2 * LANES), BF16),
        compiler_params=_cparams("arbitrary", "arbitrary"),
        name="banded_attn",
    )(*args)


def _flash_kernel(*refs, mode, lambda_init):
    TB = FLASH_BLOCK
    if mode == "diff":
        (q_ref, k_ref, vt_ref, bias_ref, lam_ref, g_ref, o_ref,
         qx_ref, m_ref, l_ref, acc_ref, s_ref, p_ref, mc_ref) = refs
        sel_ref = None
        maps = [dict(g=h // 2, lo=(h % 2) * 64 + 32 * mm, w=DIFF_DIM, roll=False, kg=h // 2, vg=h // 2, h=h)
                for h in range(DIFF_HEADS) for mm in range(2)]
        scale = DIFF_DIM ** -0.5
    else:
        (q_ref, k_ref, vt_ref, sel_ref, bias_ref, o_ref,
         qx_ref, m_ref, l_ref, acc_ref, s_ref, p_ref, mc_ref) = refs
        maps = [dict(g=h // 2, lo=0, w=HEAD_DIM, roll=(h % 2 == 1), kg=0, vg=0, h=h) for h in range(NSA_HEADS)]
        scale = HEAD_DIM ** -0.5
    n = pl.program_id(1)

    for mp, c in enumerate(maps):
        qx_ref[mp] = _lane_window(q_ref[:, LANES * c["g"]:LANES * (c["g"] + 1)].astype(F32),
                                  c["lo"], c["w"], c["roll"], scale * LOG2E)
    m_ref[...] = jnp.full(m_ref.shape, NEG_INF, F32)
    l_ref[...] = jnp.zeros_like(l_ref)
    acc_ref[...] = jnp.zeros_like(acc_ref)

    key = lax.broadcasted_iota(I32, (TB, TB), 0)
    qry = lax.broadcasted_iota(I32, (TB, TB), 1)

    def step(j, kind):
        koff = pl.multiple_of(j * TB, TB)
        valid = None
        if kind == "diag":
            valid = qry >= key
        if sel_ref is not None:
            per = TB // NSA_SLC_LEN
            rows = [jnp.broadcast_to(sel_ref[pl.ds(j * per + i, 1), :], (NSA_SLC_LEN, TB)) for i in range(per)]
            picked = jnp.concatenate(rows, axis=0) > 0.5
            valid = picked if valid is None else (valid & picked)
        for mp, c in enumerate(maps):
            kt = k_ref[pl.ds(koff, TB), LANES * c["kg"]:LANES * (c["kg"] + 1)]
            s = lax.dot_general(kt, qx_ref[mp], _NT, preferred_element_type=F32)
            if kind == "near":
                s = s + bias_ref[c["h"], 1]
            elif kind == "diag":
                s = s + bias_ref[c["h"], 0]
            if valid is not None:
                s = jnp.where(valid, s, NEG_INF)
            s_ref[mp] = s
            mc_ref[mp] = jnp.max(s, axis=0, keepdims=True)
        for mp, c in enumerate(maps):
            m_prev = m_ref[mp]
            m_new = jnp.maximum(m_prev, mc_ref[mp])
            alpha = jnp.exp2(m_prev - m_new)
            p = jnp.exp2(s_ref[mp] - m_new)
            l_ref[mp] = alpha * l_ref[mp] + jnp.sum(p, axis=0, keepdims=True)
            p_ref[mp] = p.astype(BF16)
            mc_ref[mp] = alpha
            m_ref[mp] = m_new
        for mp, c in enumerate(maps):
            vt = vt_ref[LANES * c["vg"]:LANES * (c["vg"] + 1), pl.ds(koff, TB)]
            acc_ref[mp] = mc_ref[mp] * acc_ref[mp] + jnp.dot(vt, p_ref[mp], preferred_element_type=F32)

    def far_body(j, carry):
        step(j, "far")
        return carry

    lax.fori_loop(0, jnp.maximum(n - 1, 0), far_body, 0)

    @pl.when(n >= 1)
    def _():
        step(n - 1, "near")

    step(n, "diag")

    def normalised(mp):
        return acc_ref[mp] * (1.0 / jnp.maximum(l_ref[mp], TINY))

    half = LANES // 2
    if mode == "diff":
        lq = lam_ref[...]
        lam = (jnp.exp(jnp.sum(lq[0:1] * lq[1:2], axis=1, keepdims=True))
               - jnp.exp(jnp.sum(lq[2:3] * lq[3:4], axis=1, keepdims=True)) + lambda_init)
        for g in range(DIFF_HEADS // 2):
            res = []
            for h in (2 * g, 2 * g + 1):
                o = normalised(2 * h) - lam * normalised(2 * h + 1)
                o = o[half * (h % 2):half * (h % 2) + half]
                ms = jnp.sum(o * o, axis=0, keepdims=True) * (1.0 / (2 * DIFF_DIM))
                res.append(o * lax.rsqrt(ms + DIFF_SUBLN_EPS))
            out = jnp.concatenate(res, axis=0) * g_ref[...] * (1.0 - lambda_init)
            o_ref[:, LANES * g:LANES * (g + 1)] = out.T.astype(o_ref.dtype)
    else:
        for g in range(NSA_HEADS // 2):
            out = jnp.concatenate([normalised(2 * g)[half:], normalised(2 * g + 1)[half:]], axis=0)
            o_ref[:, LANES * g:LANES * (g + 1)] = out.T.astype(o_ref.dtype)


def _flash_attention(mode, qsrc, q_col, ksrc, k_col, vt, vt_rows, vt_blk, bias, extra, *, B, S, lambda_init=0.0):
    TB = FLASH_BLOCK
    q_spec = pl.BlockSpec((None, TB, 2 * LANES), lambda b, n: (b, n, q_col))
    bias_spec = pl.BlockSpec(bias.shape, lambda b, n: (0, 0, 0, 0))
    vt_spec = pl.BlockSpec((vt_rows, S), lambda b, n: (vt_blk, b))
    if mode == "diff":
        lam4, gsub = extra
        nmaps = 2 * DIFF_HEADS
        in_specs = [q_spec,
                    pl.BlockSpec((None, S, 2 * LANES), lambda b, n: (b, 0, k_col)),
                    vt_spec, bias_spec,
                    pl.BlockSpec(lam4.shape, lambda b, n: (0, 0)),
                    pl.BlockSpec(gsub.shape, lambda b, n: (0, 0))]
        args = [qsrc, ksrc, vt, bias, lam4, gsub]
    else:
        (sel,) = extra
        nmaps = NSA_HEADS
        in_specs = [q_spec,
                    pl.BlockSpec((None, S, LANES), lambda b, n: (b, 0, k_col)),
                    vt_spec,
                    pl.BlockSpec((None, LANES, TB), lambda b, n: (b, 0, n)),
                    bias_spec]
        args = [qsrc, ksrc, vt, sel, bias]
    return pl.pallas_call(
        functools.partial(_flash_kernel, mode=mode, lambda_init=lambda_init),
        grid=(B, S // TB),
        in_specs=in_specs,
        out_specs=pl.BlockSpec((None, TB, 2 * LANES), lambda b, n: (b, n, 0)),
        out_shape=jax.ShapeDtypeStruct((B, S, 2 * LANES), BF16),
        scratch_shapes=[pltpu.VMEM((nmaps, TB, LANES), BF16),
                        pltpu.VMEM((nmaps, 1, TB), F32),
                        pltpu.VMEM((nmaps, 1, TB), F32),
                        pltpu.VMEM((nmaps, LANES, TB), F32),
                        pltpu.VMEM((nmaps, TB, TB), F32),
                        pltpu.VMEM((nmaps, TB, TB), BF16),
                        pltpu.VMEM((nmaps, 1, TB), F32)],
        compiler_params=_cparams("arbitrary", "arbitrary"),
        name="flash_" + mode,
    )(*args)


def _compress_kernel(ck_ref, cv_ref, pk_ref, pv_ref, w1k_ref, w1v_ref, w2_ref, o_ref):
    half = NSA_CMP_STRIDE * HEAD_DIM
    ck = ck_ref[...]
    cv = cv_ref[...]
    nrow = ck.shape[0]

    def mm(a, w):
        return jnp.dot(a, w, preferred_element_type=F32)

    top = mm(ck, w1k_ref[0:half, :]) + mm(cv, w1v_ref[0:half, :])
    bot = mm(ck, w1k_ref[half:2 * half, :]) + mm(cv, w1v_ref[half:2 * half, :])
    posw = (mm(pk_ref[...], w1k_ref[...]) + mm(pv_ref[...], w1v_ref[...]))[0:1]
    pre = top + pltpu.roll(bot, nrow - 1, 0) + posw
    kcv = mm(jax.nn.gelu(pre).astype(BF16), w2_ref[...])
    o_ref[0:LANES, :] = jnp.zeros((LANES, LANES), o_ref.dtype)
    o_ref[LANES:LANES + nrow, :] = kcv.astype(o_ref.dtype)


def _nsa_compress(ck, cv, pk, pv, w1k, w1v, w2, ncp):
    B, nchunk, width = ck.shape
    assert ncp == LANES + nchunk

    def full(a):
        return pl.BlockSpec(a.shape, lambda b: (0,) * a.ndim)

    chunk = pl.BlockSpec((None, nchunk, width), lambda b: (b, 0, 0))
    return pl.pallas_call(
        _compress_kernel,
        grid=(B,),
        in_specs=[chunk, chunk, full(pk), full(pv), full(w1k), full(w1v), full(w2)],
        out_specs=pl.BlockSpec((None, ncp, LANES), lambda b: (b, 0, 0)),
        out_shape=jax.ShapeDtypeStruct((B, ncp, LANES), BF16),
        compiler_params=_cparams("arbitrary"),
        name="nsa_compress",
    )(ck, cv, pk, pv, w1k, w1v, w2)


def _cmp_kernel(q_ref, kcv_ref, bias_ref, agg_ref, oc_ref, sel_ref, s_scr, *, n_cmp, ncp):
    n = pl.program_id(1)
    QB = Q_BLOCK
    kcv = kcv_ref[...]
    cp = lax.broadcasted_iota(I32, (QB, ncp), 1)
    qpos = n * QB + lax.broadcasted_iota(I32, (QB, ncp), 0)
    valid = ((cp >= LANES) & (cp < LANES + n_cmp)
             & (qpos - NSA_CMP_STRIDE * (cp - LANES) - (NSA_CMP_LEN - 1) >= 0))
    woff = pl.multiple_of((n // 16) * LANES, LANES)
    scale = HEAD_DIM ** -0.5
    psum = jnp.zeros((QB, ncp), F32)
    outs = []
    for h in range(NSA_HEADS):
        g = h // 2
        qx = _lane_window(q_ref[:, LANES * g:LANES * (g + 1)].astype(F32), 0, HEAD_DIM, h % 2 == 1, scale)
        s_scr[...] = lax.dot_general(qx, kcv, _NT, preferred_element_type=F32)
        s_scr[:, pl.ds(woff, 2 * LANES)] += bias_ref[h]
        s = jnp.where(valid, s_scr[...], NEG_INF)
        m = jnp.max(s, axis=1, keepdims=True)
        p = jnp.where(valid, jnp.exp(s - m), 0.0)
        den = jnp.sum(p, axis=1, keepdims=True)
        p = p * (1.0 / jnp.maximum(den, TINY))
        psum = psum + p
        outs.append(jnp.dot(p.astype(BF16), kcv, preferred_element_type=F32))
    for g in range(NSA_HEADS // 2):
        oc_ref[:, LANES * g:LANES * (g + 1)] = _pair_halves(outs[2 * g], outs[2 * g + 1], 1, 1).astype(oc_ref.dtype)

    imp = jnp.dot(psum.astype(BF16), agg_ref[...], preferred_element_type=F32)
    qp = n * QB + lax.broadcasted_iota(I32, (QB, LANES), 0)
    jb = lax.broadcasted_iota(I32, (QB, LANES), 1)
    cur = qp >> 6
    forced = (jb == 0) | (jb == cur) | (jb == cur - 1)
    imp = jnp.where(forced, NSA_FORCED_SCORE, jnp.where(jb * NSA_SLC_LEN <= qp, imp, -1.0))

    v = imp.T
    rid = lax.broadcasted_iota(I32, v.shape, 0).astype(F32)
    picked = jnp.zeros(v.shape, F32)
    for _ in range(NSA_TOPK):
        mx = jnp.max(v, axis=0, keepdims=True)
        first = jnp.min(jnp.where(v == mx, rid, float(LANES)), axis=0, keepdims=True)
        hit = rid == first
        picked = jnp.where(hit, 1.0, picked)
        v = jnp.where(hit, -jnp.inf, v)
    sel_ref[...] = picked


def _nsa_cmp_select(dsrc, kcv, bias, agg, *, B, S, n_cmp, ncp):
    return pl.pallas_call(
        functools.partial(_cmp_kernel, n_cmp=n_cmp, ncp=ncp),
        grid=(B, S // Q_BLOCK),
        in_specs=[pl.BlockSpec((None, Q_BLOCK, 2 * LANES), lambda b, n: (b, n, 0)),
                  pl.BlockSpec((None, ncp, LANES), lambda b, n: (b, 0, 0)),
                  pl.BlockSpec((NSA_HEADS, None, Q_BLOCK, 2 * LANES), lambda b, n: (0, n % 16, 0, 0)),
                  pl.BlockSpec((ncp, LANES), lambda b, n: (0, 0))],
        out_specs=[pl.BlockSpec((None, Q_BLOCK, 2 * LANES), lambda b, n: (b, n, 0)),
                   pl.BlockSpec((None, LANES, Q_BLOCK), lambda b, n: (b, 0, n))],
        out_shape=[jax.ShapeDtypeStruct((B, S, 2 * LANES), BF16),
                   jax.ShapeDtypeStruct((B, LANES, S), F32)],
        scratch_shapes=[pltpu.VMEM((Q_BLOCK, ncp), F32)],
        compiler_params=_cparams("arbitrary", "arbitrary"),
        name="nsa_cmp_select",
    )(dsrc, kcv, bias, agg)


def _merge_kernel(x_ref, g_ref, ya_ref, yb_ref, yc_ref, oc_ref, os_ref, ow_ref, dg_ref,
                  wg_ref, bg_ref, wb_ref, wo_ref, o_ref):
    x = x_ref[...]
    h = _rms(x, g_ref[...]).astype(BF16)
    tm = x.shape[0]
    lane = lax.broadcasted_iota(I32, (tm, LANES), 1)
    gd = jax.nn.sigmoid(dg_ref[...].astype(F32))
    yd = []
    for g in range(NSA_HEADS // 2):
        acc = jnp.zeros((tm, LANES), F32)
        for br, src in enumerate((oc_ref, os_ref, ow_ref)):
            ge = gd[:, 3 * (2 * g) + br:3 * (2 * g) + br + 1]
            go = gd[:, 3 * (2 * g + 1) + br:3 * (2 * g + 1) + br + 1]
            acc = acc + jnp.where(lane < 64, ge, go) * src[:, LANES * g:LANES * (g + 1)].astype(F32)
        yd.append(acc)
    y_d = jnp.concatenate(yd, axis=1).astype(BF16)
    ys = (ya_ref[...], yb_ref[...], yc_ref[...], y_d)
    merged = jnp.zeros((tm, D_MODEL), F32)
    for br in range(4):
        gate = jax.nn.sigmoid(jnp.dot(h, wg_ref[br], preferred_element_type=F32) + bg_ref[br:br + 1, :])
        merged = merged + gate * jnp.dot(ys[br], wb_ref[br], preferred_element_type=F32)
    o_ref[...] = x + jnp.dot(merged.astype(BF16), wo_ref[...], preferred_element_type=F32)


def _merge(x, g, ya, yb, yc, oc, osel, ow, dproj, wg, bg, wb, wo):
    T = x.shape[0]
    tm = 256

    def tile(width, col=0):
        return pl.BlockSpec((tm, width), lambda i: (i, col))

    def full(a):
        return pl.BlockSpec(a.shape, lambda i: (0,) * a.ndim)

    return pl.pallas_call(
        _merge_kernel,
        grid=(T // tm,),
        in_specs=[tile(D_MODEL), full(g), tile(256), tile(256), tile(256), tile(256), tile(256), tile(256),
                  tile(LANES, 5), full(wg), full(bg), full(wb), full(wo)],
        out_specs=tile(D_MODEL),
        out_shape=jax.ShapeDtypeStruct((T, D_MODEL), F32),
        compiler_params=_cparams("arbitrary"),
        name="merge",
    )(x, g, ya, yb, yc, oc, osel, ow, dproj, wg, bg, wb, wo)


def _moe_kernel(x_ref, g_ref, wr_ref, br_ref, wg_ref, wu_ref, wd_ref, fg_ref, o_ref,
                hn_ref, comb_ref, acc_ref, *, final_norm):
    e = pl.program_id(1)
    tm = x_ref.shape[0]
    lane = lax.broadcasted_iota(I32, (tm, LANES), 1)

    @pl.when(e == 0)
    def _():
        hn = _rms(x_ref[...], g_ref[...])
        hn_ref[...] = hn.astype(BF16)
        logits = jnp.dot(hn, wr_ref[...], preferred_element_type=F32,
                         precision=lax.Precision.HIGHEST) + br_ref[...]
        is_grp = (lane >= N_EXPERTS) & (lane < N_EXPERTS + N_GROUPS)
        lanef = lane.astype(F32)
        lg = jnp.where(is_grp, logits, -jnp.inf)
        mg = jnp.max(lg, axis=1, keepdims=True)
        gsel = jnp.min(jnp.where(lg == mg, lanef, 2.0 * LANES), axis=1, keepdims=True) - N_EXPERTS
        g_w = 1.0 / jnp.sum(jnp.where(is_grp, jnp.exp(lg - mg), 0.0), axis=1, keepdims=True)
        in_grp = (lane < N_EXPERTS) & ((lane >> 3).astype(F32) == gsel)
        v1 = jnp.where(in_grp, logits, -jnp.inf)
        m1 = jnp.max(v1, axis=1, keepdims=True)
        i1 = jnp.min(jnp.where(v1 == m1, lanef, 2.0 * LANES), axis=1, keepdims=True)
        v2 = jnp.where(lanef == i1, -jnp.inf, v1)
        m2 = jnp.max(v2, axis=1, keepdims=True)
        i2 = jnp.min(jnp.where(v2 == m2, lanef, 2.0 * LANES), axis=1, keepdims=True)
        e2 = jnp.exp(m2 - m1)
        w1 = g_w / (1.0 + e2)
        w2 = g_w * e2 / (1.0 + e2)
        comb_ref[...] = jnp.where(lanef == i1, w1, 0.0) + jnp.where(lanef == i2, w2, 0.0)
        acc_ref[...] = jnp.zeros_like(acc_ref)

    hn = hn_ref[...]
    gt = jnp.dot(hn, wg_ref[...], preferred_element_type=F32)
    up = jnp.dot(hn, wu_ref[...], preferred_element_type=F32)
    he = (gt * jax.nn.sigmoid(gt) * up).astype(BF16)
    ye = jnp.dot(he, wd_ref[...], preferred_element_type=F32)
    cw = jnp.sum(jnp.where(lane == e, comb_ref[...], 0.0), axis=1, keepdims=True)
    acc_ref[...] += cw * ye

    @pl.when(e == N_EXPERTS - 1)
    def _():
        y = x_ref[...] + acc_ref[...]
        if final_norm:
            y = _rms(y, fg_ref[...])
        o_ref[...] = y


def _moe(x, g, w_router, b_router, wg, wu, wd, final_g, final_norm):
    T = x.shape[0]
    tm = 1024
    return pl.pallas_call(
        functools.partial(_moe_kernel, final_norm=final_norm),
        grid=(T // tm, N_EXPERTS),
        in_specs=[pl.BlockSpec((tm, D_MODEL), lambda i, e: (i, 0)),
                  pl.BlockSpec((1, D_MODEL), lambda i, e: (0, 0)),
                  pl.BlockSpec((D_MODEL, LANES), lambda i, e: (0, 0)),
                  pl.BlockSpec((1, LANES), lambda i, e: (0, 0)),
                  pl.BlockSpec((None, D_MODEL, EXPERT_FF), lambda i, e: (e, 0, 0)),
                  pl.BlockSpec((None, D_MODEL, EXPERT_FF), lambda i, e: (e, 0, 0)),
                  pl.BlockSpec((None, EXPERT_FF, D_MODEL), lambda i, e: (e, 0, 0)),
                  pl.BlockSpec((1, D_MODEL), lambda i, e: (0, 0))],
        out_specs=pl.BlockSpec((tm, D_MODEL), lambda i, e: (i, 0)),
        out_shape=jax.ShapeDtypeStruct((T, D_MODEL), F32),
        scratch_shapes=[pltpu.VMEM((tm, D_MODEL), BF16),
                        pltpu.VMEM((tm, LANES), F32),
                        pltpu.VMEM((tm, D_MODEL), F32)],
        compiler_params=_cparams("arbitrary", "arbitrary"),
        name="moe",
    )(x, g, w_router, b_router, wg, wu, wd, final_g)


def _block_diag(w):
    nb, bs, _ = w.shape
    out = jnp.zeros((nb * bs, nb * bs), w.dtype)
    for i in range(nb):
        out = out.at[i * bs:(i + 1) * bs, i * bs:(i + 1) * bs].set(w[i])
    return out


def _selection_aggregator_np(n_cmp, n_slc):
    ratio_s = NSA_SLC_LEN // NSA_CMP_STRIDE
    ratio_c = NSA_CMP_LEN // NSA_CMP_STRIDE
    w = np.convolve(np.ones(ratio_s), np.ones(ratio_c)).astype(np.float32)
    cj = np.arange(n_slc)[:, None] * ratio_s - (ratio_c - 1) + np.arange(w.size)[None, :]
    jj = np.broadcast_to(np.arange(n_slc)[:, None], cj.shape)
    ww = np.broadcast_to(w[None, :], cj.shape)
    keep = (cj >= 0) & (cj < n_cmp)
    agg = np.zeros((n_cmp, n_slc), np.float32)
    np.add.at(agg, (cj[keep], jj[keep]), ww[keep])
    return agg


def _row(v):
    return v.reshape(1, -1).astype(F32)


def _diff_lambda_init(layer):
    return 0.8 - 0.6 * math.exp(-0.3 * layer)


def _mixer_layer(x, l, B, S, tiles, p):
    (bias_swa, bias_diff, bias_slc, bias_win, bias_cmp, agg_pad, n_cmp, ncp) = tiles
    w_in = jnp.pad(p["w_in"][l], ((0, 0), (0, N_IN_PAD - p["w_in"].shape[2]))).astype(BF16)
    w_t = jnp.concatenate([w_in[:, 1536:1792], w_in[:, 2176:2304]], axis=1).T
    a_xg, b_qkv, c_qkv, d_all, vals_t = _in_proj(x, _row(p["mix_norm_g"][l]), w_in, w_t)

    y_a = _rglru(a_xg, p["conv_w"][l].astype(F32), _row(p["conv_b"][l]),
                 _block_diag(p["lru_w_a"][l]).astype(BF16), _row(p["lru_b_a"][l]),
                 _block_diag(p["lru_w_x"][l]).astype(BF16), _row(p["lru_b_x"][l]),
                 _row(p["lru_lambda"][l]), B, S)

    b3 = b_qkv.reshape(B, S, 512)
    swa_cfg = tuple(((h // 2) * 64, (h % 2) != (h // 2), h // 2) for h in range(SWA_HEADS))
    y_b = _banded_attention(b3, 0, b3, 2, b3, 3, bias_swa, _row(p["swa_sinks"][l]),
                            B=B, S=S, window=SWA_WINDOW, q_cfg=swa_cfg)

    c3 = c_qkv.reshape(B, S, 768)
    lam4 = jnp.stack([p["diff_lq1"][l], p["diff_lk1"][l], p["diff_lq2"][l], p["diff_lk2"][l]]).astype(F32)
    gsub = jnp.tile(p["diff_subln_g"][l].reshape(-1, 1).astype(F32), (2, 1))
    y_c = _flash_attention("diff", c3, 0, c3, 1, vals_t, 256, 0, bias_diff, (lam4, gsub), B=B, S=S,
                           lambda_init=_diff_lambda_init(l))

    d3 = d_all.reshape(B, S, 768)
    nchunk = S // NSA_CMP_STRIDE
    ck = d3[:, :, 256:320].reshape(B, nchunk, NSA_CMP_STRIDE * HEAD_DIM)
    cv = d3[:, :, 320:384].reshape(B, nchunk, NSA_CMP_STRIDE * HEAD_DIM)

    def pos8(a):
        return jnp.broadcast_to(a.reshape(1, -1), (8, a.size)).astype(BF16)

    w1k = jnp.pad(p["nsa_w1_k"][l], ((0, 0), (0, HEAD_DIM))).astype(BF16)
    w1v = jnp.pad(p["nsa_w1_v"][l], ((0, 0), (HEAD_DIM, 0))).astype(BF16)
    w2 = _block_diag(jnp.stack([p["nsa_w2_k"][l], p["nsa_w2_v"][l]])).astype(BF16)
    kcv = _nsa_compress(ck, cv, pos8(p["nsa_pos_k"][l]), pos8(p["nsa_pos_v"][l]), w1k, w1v, w2, ncp)
    o_c, sel = _nsa_cmp_select(d3, kcv, bias_cmp, agg_pad, B=B, S=S, n_cmp=n_cmp, ncp=ncp)
    o_s = _flash_attention("nsa", d3, 0, d3, 3, vals_t, 128, 2, bias_slc, (sel,), B=B, S=S)
    win_cfg = tuple((0, h % 2 == 1, 1) for h in range(NSA_HEADS))
    o_w = _banded_attention(d3, 0, d3, 4, None, None, bias_win, None, B=B, S=S, window=NSA_WINDOW, q_cfg=win_cfg)

    T = B * S
    return _merge(x, _row(p["mix_norm_g"][l]), y_a, y_b.reshape(T, 256), y_c.reshape(T, 256),
                  o_c.reshape(T, 256), o_s.reshape(T, 256), o_w.reshape(T, 256), d_all,
                  p["w_gate"][l].astype(BF16), p["b_gate"][l].astype(F32), p["w_branch"][l].astype(BF16),
                  p["w_out"][l].astype(BF16))


def _moe_layer(x, l, p, final_norm):
    w_router = jnp.concatenate([p["w_router_exp"][l], p["w_router_grp"][l]], axis=1)
    w_router = jnp.pad(w_router, ((0, 0), (0, LANES - w_router.shape[1]))).astype(F32)
    b_router = jnp.concatenate([p["b_router_exp"][l], p["b_router_grp"][l]])
    b_router = jnp.pad(b_router, (0, LANES - b_router.shape[0])).reshape(1, LANES).astype(F32)
    return _moe(x, _row(p["ffn_norm_g"][l]), w_router, b_router, p["w_exp_gate"][l].astype(BF16),
                p["w_exp_up"][l].astype(BF16), p["w_exp_down"][l].astype(BF16), _row(p["final_norm_g"]), final_norm)


def _forward(p):
    x = p["x"]
    B, S, D = x.shape
    depth = p["w_in"].shape[0]
    assert D == D_MODEL and S % (16 * Q_BLOCK) == 0 and S // NSA_SLC_LEN <= LANES
    rel = p["rel_bias"].astype(F32)
    n_cmp = (S - NSA_CMP_LEN) // NSA_CMP_STRIDE + 1
    ncp = LANES + S // NSA_CMP_STRIDE
    agg = _selection_aggregator_np(n_cmp, S // NSA_SLC_LEN)
    agg_pad = np.zeros((ncp, LANES), np.float32)
    agg_pad[LANES:LANES + n_cmp, :agg.shape[1]] = agg
    tiles = (
        _bias_tiles(rel, _banded_buckets(1), 0, SWA_HEADS, False),
        _bias_tiles(rel, _flash_buckets(), SWA_HEADS, DIFF_HEADS, True, LOG2E),
        _bias_tiles(rel, _flash_buckets(), SWA_HEADS + DIFF_HEADS, NSA_HEADS, True, LOG2E),
        _bias_tiles(rel, _banded_buckets(NSA_WINDOW // Q_BLOCK), SWA_HEADS + DIFF_HEADS, NSA_HEADS, False),
        _bias_tiles(rel, _cmp_buckets(), SWA_HEADS + DIFF_HEADS, NSA_HEADS, True),
        jnp.asarray(agg_pad, BF16), n_cmp, ncp,
    )
    xt = x.reshape(B * S, D).astype(F32)
    for l in range(depth):
        xt = _mixer_layer(xt, l, B, S, tiles, p)
        xt = _moe_layer(xt, l, p, final_norm=(l == depth - 1))
    return xt.reshape(B, S, D).astype(x.dtype)


def kernel(x, rel_bias, final_norm_g, mix_norm_g, w_in, conv_w, conv_b, lru_w_a, lru_b_a, lru_w_x, lru_b_x,
           lru_lambda, swa_sinks, diff_lq1, diff_lk1, diff_lq2, diff_lk2, diff_subln_g, nsa_pos_k, nsa_w1_k,
           nsa_w2_k, nsa_pos_v, nsa_w1_v, nsa_w2_v, w_gate, b_gate, w_branch, w_out, ffn_norm_g, w_router_grp,
           b_router_grp, w_router_exp, b_router_exp, w_exp_gate, w_exp_up, w_exp_down):
    return _forward(dict(
        x=x, rel_bias=rel_bias, final_norm_g=final_norm_g, mix_norm_g=mix_norm_g, w_in=w_in, conv_w=conv_w,
        conv_b=conv_b, lru_w_a=lru_w_a, lru_b_a=lru_b_a, lru_w_x=lru_w_x, lru_b_x=lru_b_x, lru_lambda=lru_lambda,
        swa_sinks=swa_sinks, diff_lq1=diff_lq1, diff_lk1=diff_lk1, diff_lq2=diff_lq2, diff_lk2=diff_lk2,
        diff_subln_g=diff_subln_g, nsa_pos_k=nsa_pos_k, nsa_w1_k=nsa_w1_k, nsa_w2_k=nsa_w2_k, nsa_pos_v=nsa_pos_v,
        nsa_w1_v=nsa_w1_v, nsa_w2_v=nsa_w2_v, w_gate=w_gate, b_gate=b_gate, w_branch=w_branch, w_out=w_out,
        ffn_norm_g=ffn_norm_g, w_router_grp=w_router_grp, b_router_grp=b_router_grp, w_router_exp=w_router_exp,
        b_router_exp=b_router_exp, w_exp_gate=w_exp_gate, w_exp_up=w_exp_up, w_exp_down=w_exp_down))
```

```python
import functools
import math

import numpy as np
import jax
import jax.numpy as jnp
from jax import lax
from jax.experimental import pallas as pl
from jax.experimental.pallas import tpu as pltpu

F32 = jnp.float32
BF16 = jnp.bfloat16
I32 = jnp.int32

D_MODEL = 1024
HEAD_DIM = 64
MIX_WIDTH = 256
LRU_C = 8.0
CONV_WIDTH = 4
SWA_HEADS = 4
SWA_WINDOW = 128
DIFF_HEADS = 4
DIFF_DIM = 32
DIFF_SUBLN_EPS = 1e-5
NSA_HEADS = 4
NSA_CMP_LEN = 32
NSA_CMP_STRIDE = 16
NSA_SLC_LEN = 64
NSA_TOPK = 16
NSA_WINDOW = 512
NSA_FORCED_SCORE = 1e4
REL_BUCKETS = 32
REL_MAX_DIST = 128
N_GROUPS = 4
EXPERTS_PER_GROUP = 8
N_EXPERTS = 32
EXPERT_FF = 256
NORM_EPS = 1e-6
NEG_INF = -1e30
TINY = 1e-30
LOG2E = math.log2(math.e)

LANES = 128
Q_BLOCK = 128
BANDED_Q_BLOCKS = 2
FLASH_BLOCK = 256
FLASH_FAR_BLOCKS = 2
N_IN_PAD = 2560
VMEM_LIMIT = 56 * 1024 * 1024

_NT = (((1,), (1,)), ((), ()))


def _cparams(*sem):
    return pltpu.CompilerParams(dimension_semantics=sem, vmem_limit_bytes=VMEM_LIMIT)


def _t5_bucket_np(dist):
    n = np.maximum(dist, 0)
    exact = REL_BUCKETS // 2
    scaled = (np.log(np.maximum(n, 1).astype(np.float32) / np.float32(exact))
              / np.float32(math.log(REL_MAX_DIST / exact))).astype(np.float32)
    large = np.minimum(exact + (scaled * np.float32(REL_BUCKETS - exact)).astype(np.int32), REL_BUCKETS - 1)
    return np.where(n < exact, n, large).astype(np.int32)


def _bias_tile_kernel(tab_ref, idx_ref, o_ref, *, head0, shift, scale):
    h = pl.program_id(0) + head0
    idx = idx_ref[0]
    acc = jnp.zeros(idx.shape, F32)
    for b in range(REL_BUCKETS):
        acc = jnp.where(idx == b, tab_ref[b, h], acc)
    if shift:
        acc = acc - tab_ref[REL_BUCKETS - 1, h]
    if scale != 1.0:
        acc = acc * scale
    o_ref[0, 0] = acc


def _bias_tiles(rel_bias, buckets, head0, nheads, shift, scale=1.0):
    R, M, N = buckets.shape
    return pl.pallas_call(
        functools.partial(_bias_tile_kernel, head0=head0, shift=shift, scale=scale),
        grid=(nheads, R),
        in_specs=[pl.BlockSpec(memory_space=pltpu.SMEM),
                  pl.BlockSpec((1, M, N), lambda h, r: (r, 0, 0))],
        out_specs=pl.BlockSpec((1, 1, M, N), lambda h, r: (h, r, 0, 0)),
        out_shape=jax.ShapeDtypeStruct((nheads, R, M, N), F32),
        compiler_params=_cparams("arbitrary", "arbitrary"),
        name="bias_tiles",
    )(rel_bias, jnp.asarray(buckets))


def _banded_buckets(nback):
    i = np.arange(BANDED_Q_BLOCKS * Q_BLOCK)[:, None]
    j = np.arange((nback + BANDED_Q_BLOCKS) * Q_BLOCK)[None, :]
    return _t5_bucket_np(i - j + nback * Q_BLOCK)[None]


def _flash_buckets():
    k = np.arange(FLASH_BLOCK)[:, None]
    q = np.arange(FLASH_BLOCK)[None, :]
    return np.stack([_t5_bucket_np(q - k), _t5_bucket_np(q - k + FLASH_BLOCK)])


def _cmp_buckets():
    i = np.arange(Q_BLOCK)[None, :, None]
    c2 = np.arange(2 * LANES)[None, None, :]
    r = np.arange(16)[:, None, None]
    dist = i - NSA_CMP_STRIDE * (c2 - LANES - 8 * r) - (NSA_CMP_LEN - 1)
    return _t5_bucket_np(dist)


def _rms(x, g):
    return x * lax.rsqrt(jnp.mean(x * x, axis=-1, keepdims=True) + NORM_EPS) * g


def _proj_kernel(x_ref, g_ref, w_ref, wt_ref, oa_ref, ob_ref, oc_ref, od_ref, ot_ref):
    h = _rms(x_ref[...], g_ref[...]).astype(BF16)
    oa_ref[...] = jnp.dot(h, w_ref[:, 0:512], preferred_element_type=F32)
    ob_ref[...] = jnp.dot(h, w_ref[:, 512:1024], preferred_element_type=F32).astype(BF16)
    oc_ref[...] = jnp.dot(h, w_ref[:, 1024:1792], preferred_element_type=F32).astype(BF16)
    od_ref[...] = jnp.dot(h, w_ref[:, 1792:2560], preferred_element_type=F32).astype(BF16)
    ot_ref[...] = lax.dot_general(wt_ref[...], h, _NT, preferred_element_type=F32).astype(BF16)


def _in_proj(x, g, w, wt):
    T = x.shape[0]
    tm = 512
    nt = wt.shape[0]
    return pl.pallas_call(
        _proj_kernel,
        grid=(T // tm,),
        in_specs=[pl.BlockSpec((tm, D_MODEL), lambda i: (i, 0)),
                  pl.BlockSpec((1, D_MODEL), lambda i: (0, 0)),
                  pl.BlockSpec((D_MODEL, N_IN_PAD), lambda i: (0, 0)),
                  pl.BlockSpec((nt, D_MODEL), lambda i: (0, 0))],
        out_specs=[pl.BlockSpec((tm, 512), lambda i: (i, 0)),
                   pl.BlockSpec((tm, 512), lambda i: (i, 0)),
                   pl.BlockSpec((tm, 768), lambda i: (i, 0)),
                   pl.BlockSpec((tm, 768), lambda i: (i, 0)),
                   pl.BlockSpec((nt, tm), lambda i: (0, i))],
        out_shape=[jax.ShapeDtypeStruct((T, 512), F32),
                   jax.ShapeDtypeStruct((T, 512), BF16),
                   jax.ShapeDtypeStruct((T, 768), BF16),
                   jax.ShapeDtypeStruct((T, 768), BF16),
                   jax.ShapeDtypeStruct((nt, T), BF16)],
        compiler_params=_cparams("arbitrary"),
        name="in_proj",
    )(x, g, w, wt)


def _lru_kernel(xg_ref, cw_ref, cb_ref, wa_ref, ba_ref, wx_ref, bx_ref, lam_ref, o_ref,
                ext_ref, h_ref, *, Tt):
    t = pl.program_id(1)

    W = MIX_WIDTH

    @pl.when(t == 0)
    def _():
        ext_ref[0:8, :] = jnp.zeros((8, W), F32)
        h_ref[...] = jnp.zeros_like(h_ref)

    x = xg_ref[:, 0:W]
    gate = xg_ref[:, W:2 * W]
    ext_ref[8:Tt + 8, :] = x
    row = lax.broadcasted_iota(I32, (Tt, W), 0)
    xc = cb_ref[...] + x * cw_ref[CONV_WIDTH - 1:CONV_WIDTH, :]
    for s in range(1, CONV_WIDTH):
        xc = xc + ext_ref[8 - s:8 - s + Tt, :] * cw_ref[CONV_WIDTH - 1 - s:CONV_WIDTH - s, :]
    ext_ref[0:8, :] = ext_ref[Tt:Tt + 8, :]

    xcb = xc.astype(BF16)
    r = jax.nn.sigmoid(jnp.dot(xcb, wa_ref[...], preferred_element_type=F32) + ba_ref[...])
    ig = jax.nn.sigmoid(jnp.dot(xcb, wx_ref[...], preferred_element_type=F32) + bx_ref[...])
    z = -lam_ref[...]
    softplus = jnp.maximum(z, 0.0) + jnp.log1p(jnp.exp(-jnp.abs(z)))
    log_a = (-LRU_C) * r * softplus
    a = jnp.exp(log_a)
    mult = jnp.sqrt(1.0 - a * a)
    mult = jnp.where((row + t * Tt) == 0, 1.0, mult)
    b = mult * ig * xc

    d = 1
    while d < Tt:
        keep = row >= d
        b = jnp.where(keep, a * pltpu.roll(b, d, 0) + b, b)
        a = jnp.where(keep, a * pltpu.roll(a, d, 0), a)
        d *= 2
    hs = a * h_ref[7:8, :] + b
    h_ref[...] = hs[Tt - 8:Tt]
    o_ref[...] = (hs * jax.nn.gelu(gate)).astype(o_ref.dtype)


def _rglru(a_xg, conv_w, conv_b, wa_bd, b_a, wx_bd, b_x, lam, B, S):
    Tt = 256
    W = MIX_WIDTH
    vec = pl.BlockSpec((1, W), lambda b, t: (0, 0))
    mat = pl.BlockSpec((W, W), lambda b, t: (0, 0))
    out = pl.pallas_call(
        functools.partial(_lru_kernel, Tt=Tt),
        grid=(B, S // Tt),
        in_specs=[pl.BlockSpec((None, Tt, 2 * W), lambda b, t: (b, t, 0)),
                  pl.BlockSpec((CONV_WIDTH, W), lambda b, t: (0, 0)),
                  vec, mat, vec, mat, vec, vec],
        out_specs=pl.BlockSpec((None, Tt, W), lambda b, t: (b, t, 0)),
        out_shape=jax.ShapeDtypeStruct((B, S, W), BF16),
        scratch_shapes=[pltpu.VMEM((Tt + 8, W), F32), pltpu.VMEM((8, W), F32)],
        compiler_params=_cparams("arbitrary", "arbitrary"),
        name="rglru",
    )(a_xg.reshape(B, S, 2 * W), conv_w, conv_b, wa_bd, b_a, wx_bd, b_x, lam)
    return out.reshape(B * S, W)


def _lane_window(q128, lo, width, roll, scale):
    q = q128 * scale
    if roll:
        q = pltpu.roll(q, LANES // 2, 1)
    lane = lax.broadcasted_iota(I32, q.shape, 1)
    return jnp.where((lane >= lo) & (lane < lo + width), q, 0.0).astype(BF16)


def _pair_halves(even, odd, even_half, odd_half):
    if even_half == 1:
        even = pltpu.roll(even, LANES // 2, 1)
    if odd_half == 0:
        odd = pltpu.roll(odd, LANES // 2, 1)
    lane = lax.broadcasted_iota(I32, even.shape, 1)
    return jnp.where(lane < LANES // 2, even, odd)


def _banded_kernel(*refs, nback, window, q_cfg, has_sink, shared_kv):
    nk = nback + BANDED_Q_BLOCKS
    TQ = BANDED_Q_BLOCKS * Q_BLOCK
    q_ref = refs[0]
    k_refs = refs[1:1 + nk]
    pos = 1 + nk
    if shared_kv:
        v_refs = k_refs
    else:
        v_refs = refs[pos:pos + nk]
        pos += nk
    bias_ref = refs[pos]
    pos += 1
    sink_ref = None
    if has_sink:
        sink_ref = refs[pos]
        pos += 1
    o_ref = refs[pos]

    n = pl.program_id(1)
    KW = nk * Q_BLOCK
    kcat = jnp.concatenate([k_refs[nk - 1 - i][...] for i in range(nk)], axis=0)
    vcat = kcat if shared_kv else jnp.concatenate([v_refs[nk - 1 - i][...] for i in range(nk)], axis=0)
    ii = lax.broadcasted_iota(I32, (TQ, KW), 0)
    jj = lax.broadcasted_iota(I32, (TQ, KW), 1)
    dist = ii - jj + nback * Q_BLOCK
    valid = (dist >= 0) & (dist < window) & (jj >= (nback - BANDED_Q_BLOCKS * n) * Q_BLOCK)
    scale = HEAD_DIM ** -0.5

    def logits(h):
        lo, roll, _ = q_cfg[h]
        g = h // 2
        qx = _lane_window(q_ref[:, LANES * g:LANES * (g + 1)].astype(F32), lo, HEAD_DIM, roll, scale)
        s = lax.dot_general(qx, kcat, _NT, preferred_element_type=F32) + bias_ref[h, 0]
        return jnp.where(valid, s, NEG_INF)

    def softmax(h, s):
        m = jnp.max(s, axis=1, keepdims=True)
        if has_sink:
            sink = sink_ref[0:1, h:h + 1]
            m = jnp.maximum(m, sink)
        p = jnp.exp(s - m)
        den = jnp.sum(p, axis=1, keepdims=True)
        if has_sink:
            den = den + jnp.exp(sink - m)
        return p.astype(BF16), 1.0 / jnp.maximum(den, TINY)

    nh = len(q_cfg)
    ss = [logits(h) for h in range(nh)]
    pr = [softmax(h, ss[h]) for h in range(nh)]
    outs = [(jnp.dot(pr[h][0], vcat, preferred_element_type=F32) * pr[h][1], q_cfg[h][2]) for h in range(nh)]
    for g in range(len(q_cfg) // 2):
        (oe, he), (oo, ho) = outs[2 * g], outs[2 * g + 1]
        o_ref[:, LANES * g:LANES * (g + 1)] = _pair_halves(oe, oo, he, ho).astype(o_ref.dtype)


def _banded_attention(qsrc, q_col, ksrc, k_col, vsrc, v_col, bias, sinks, *, B, S, window, q_cfg):
    nback = -(-window // Q_BLOCK)
    nq = BANDED_Q_BLOCKS
    nk = nback + nq
    TQ = nq * Q_BLOCK
    shared_kv = vsrc is None
    has_sink = sinks is not None

    def kv_spec(col, i):
        return pl.BlockSpec((None, Q_BLOCK, LANES), lambda b, n: (b, jnp.maximum(nq * n + nq - 1 - i, 0), col))

    in_specs = [pl.BlockSpec((None, TQ, 2 * LANES), lambda b, n: (b, n, q_col))]
    args = [qsrc]
    in_specs += [kv_spec(k_col, i) for i in range(nk)]
    args += [ksrc] * nk
    if not shared_kv:
        in_specs += [kv_spec(v_col, i) for i in range(nk)]
        args += [vsrc] * nk
    in_specs.append(pl.BlockSpec(bias.shape, lambda b, n: (0, 0, 0, 0)))
    args.append(bias)
    if has_sink:
        in_specs.append(pl.BlockSpec(sinks.shape, lambda b, n: (0, 0)))
        args.append(sinks)
    return pl.pallas_call(
        functools.partial(_banded_kernel, nback=nback, window=window, q_cfg=q_cfg,
                          has_sink=has_sink, shared_kv=shared_kv),
        grid=(B, S // TQ),
        in_specs=in_specs,
        out_specs=pl.BlockSpec((None, TQ, 2 * LANES), lambda b, n: (b, n, 0)),
        out_shape=jax.ShapeDtypeStruct((B, S, 2 * LANES), BF16),
        compiler_params=_cparams("arbitrary", "arbitrary"),
        name="banded_attn",
    )(*args)


def _flash_kernel(*refs, mode, lambda_init):
    TB = FLASH_BLOCK
    if mode == "diff":
        (q_ref, k_ref, vt_ref, bias_ref, lam_ref, g_ref, o_ref,
         qx_ref, m_ref, l_ref, acc_ref) = refs
        sel_ref = None
        maps = [dict(g=h // 2, lo=(h % 2) * 64 + 32 * mm, w=DIFF_DIM, roll=False, kg=h // 2, vrow=HEAD_DIM * h, h=h)
                for h in range(DIFF_HEADS) for mm in range(2)]
        scale = DIFF_DIM ** -0.5
    else:
        (q_ref, k_ref, vt_ref, sel_ref, bias_ref, o_ref,
         qx_ref, m_ref, l_ref, acc_ref) = refs
        maps = [dict(g=h // 2, lo=0, w=HEAD_DIM, roll=(h % 2 == 1), kg=0, vrow=HEAD_DIM, h=h)
                for h in range(NSA_HEADS)]
        scale = HEAD_DIM ** -0.5
    n = pl.program_id(1)

    for mp, c in enumerate(maps):
        qx_ref[mp] = _lane_window(q_ref[:, LANES * c["g"]:LANES * (c["g"] + 1)].astype(F32),
                                  c["lo"], c["w"], c["roll"], scale * LOG2E)
    m_ref[...] = jnp.full(m_ref.shape, NEG_INF, F32)
    l_ref[...] = jnp.zeros_like(l_ref)
    acc_ref[...] = jnp.zeros_like(acc_ref)

    key = lax.broadcasted_iota(I32, (TB, TB), 0)
    qry = lax.broadcasted_iota(I32, (TB, TB), 1)

    def step(j, kind, TK=TB):
        koff = pl.multiple_of(j * TB, TB)
        valid = None
        if kind == "diag":
            valid = qry >= key
        if sel_ref is not None:
            per = TK // NSA_SLC_LEN
            base = j * (TB // NSA_SLC_LEN)
            rows = [jnp.broadcast_to(sel_ref[pl.ds(base + i, 1), :], (NSA_SLC_LEN, TB)) for i in range(per)]
            picked = jnp.concatenate(rows, axis=0) > 0.5
            valid = picked if valid is None else (valid & picked)

        def logits(mp):
            c = maps[mp]
            kt = k_ref[pl.ds(koff, TK), LANES * c["kg"]:LANES * (c["kg"] + 1)]
            s = lax.dot_general(kt, qx_ref[mp], _NT, preferred_element_type=F32)
            if kind == "near":
                s = s + bias_ref[c["h"], 1]
            elif kind == "diag":
                s = s + bias_ref[c["h"], 0]
            if valid is not None:
                s = jnp.where(valid, s, NEG_INF)
            return s

        def softmax(mp, s):
            m_prev = m_ref[mp]
            m_new = jnp.maximum(m_prev, jnp.max(s, axis=0, keepdims=True))
            alpha = jnp.exp2(m_prev - m_new)
            p = jnp.exp2(s - m_new)
            l_ref[mp] = alpha * l_ref[mp] + jnp.sum(p, axis=0, keepdims=True)
            m_ref[mp] = m_new
            return p.astype(BF16), alpha

        def values(mp, p, alpha):
            c = maps[mp]
            vt = vt_ref[c["vrow"]:c["vrow"] + HEAD_DIM, pl.ds(koff, TK)]
            acc_ref[mp] = alpha * acc_ref[mp] + jnp.dot(vt, p, preferred_element_type=F32)

        ss = [logits(mp) for mp in range(len(maps))]
        pa = [softmax(mp, ss[mp]) for mp in range(len(maps))]
        for mp in range(len(maps)):
            values(mp, *pa[mp])

    nfar = jnp.maximum(n - 1, 0)
    FW = FLASH_FAR_BLOCKS

    def far_body(j, carry):
        step(FW * j, "far", FW * TB)
        return carry

    lax.fori_loop(0, nfar // FW, far_body, 0)

    def rest_body(j, carry):
        step(j, "far")
        return carry

    lax.fori_loop((nfar // FW) * FW, nfar, rest_body, 0)

    @pl.when(n >= 1)
    def _():
        step(n - 1, "near")

    step(n, "diag")

    def normalised(mp):
        return acc_ref[mp] * (1.0 / jnp.maximum(l_ref[mp], TINY))

    if mode == "diff":
        lq = lam_ref[...]
        lam = (jnp.exp(jnp.sum(lq[0:1] * lq[1:2], axis=1, keepdims=True))
               - jnp.exp(jnp.sum(lq[2:3] * lq[3:4], axis=1, keepdims=True)) + lambda_init)
        for g in range(DIFF_HEADS // 2):
            res = []
            for h in (2 * g, 2 * g + 1):
                o = normalised(2 * h) - lam * normalised(2 * h + 1)
                ms = jnp.sum(o * o, axis=0, keepdims=True) * (1.0 / (2 * DIFF_DIM))
                res.append(o * lax.rsqrt(ms + DIFF_SUBLN_EPS))
            out = jnp.concatenate(res, axis=0) * g_ref[...] * (1.0 - lambda_init)
            o_ref[:, LANES * g:LANES * (g + 1)] = out.T.astype(o_ref.dtype)
    else:
        for g in range(NSA_HEADS // 2):
            out = jnp.concatenate([normalised(2 * g), normalised(2 * g + 1)], axis=0)
            o_ref[:, LANES * g:LANES * (g + 1)] = out.T.astype(o_ref.dtype)


def _flash_attention(mode, qsrc, q_col, ksrc, k_col, vt, vt_rows, vt_blk, bias, extra, *, B, S, lambda_init=0.0):
    TB = FLASH_BLOCK
    q_spec = pl.BlockSpec((None, TB, 2 * LANES), lambda b, n: (b, n, q_col))
    bias_spec = pl.BlockSpec(bias.shape, lambda b, n: (0, 0, 0, 0))
    vt_spec = pl.BlockSpec((vt_rows, S), lambda b, n: (vt_blk, b))
    if mode == "diff":
        lam4, gsub = extra
        nmaps = 2 * DIFF_HEADS
        in_specs = [q_spec,
                    pl.BlockSpec((None, S, 2 * LANES), lambda b, n: (b, 0, k_col)),
                    vt_spec, bias_spec,
                    pl.BlockSpec(lam4.shape, lambda b, n: (0, 0)),
                    pl.BlockSpec(gsub.shape, lambda b, n: (0, 0))]
        args = [qsrc, ksrc, vt, bias, lam4, gsub]
    else:
        (sel,) = extra
        nmaps = NSA_HEADS
        in_specs = [q_spec,
                    pl.BlockSpec((None, S, LANES), lambda b, n: (b, 0, k_col)),
                    vt_spec,
                    pl.BlockSpec((None, LANES, TB), lambda b, n: (b, 0, n)),
                    bias_spec]
        args = [qsrc, ksrc, vt, sel, bias]
    return pl.pallas_call(
        functools.partial(_flash_kernel, mode=mode, lambda_init=lambda_init),
        grid=(B, S // TB),
        in_specs=in_specs,
        out_specs=pl.BlockSpec((None, TB, 2 * LANES), lambda b, n: (b, n, 0)),
        out_shape=jax.ShapeDtypeStruct((B, S, 2 * LANES), BF16),
        scratch_shapes=[pltpu.VMEM((nmaps, TB, LANES), BF16),
                        pltpu.VMEM((nmaps, 1, TB), F32),
                        pltpu.VMEM((nmaps, 1, TB), F32),
                        pltpu.VMEM((nmaps, HEAD_DIM, TB), F32)],
        compiler_params=_cparams("arbitrary", "arbitrary"),
        name="flash_" + mode,
    )(*args)


def _compress_kernel(ck_ref, cv_ref, pk_ref, pv_ref, w1k_ref, w1v_ref, w2_ref, o_ref):
    half = NSA_CMP_STRIDE * HEAD_DIM
    ck = ck_ref[...]
    cv = cv_ref[...]
    nrow = ck.shape[0]

    def mm(a, w):
        return jnp.dot(a, w, preferred_element_type=F32)

    top = mm(ck, w1k_ref[0:half, :]) + mm(cv, w1v_ref[0:half, :])
    bot = mm(ck, w1k_ref[half:2 * half, :]) + mm(cv, w1v_ref[half:2 * half, :])
    posw = (mm(pk_ref[...], w1k_ref[...]) + mm(pv_ref[...], w1v_ref[...]))[0:1]
    pre = top + pltpu.roll(bot, nrow - 1, 0) + posw
    kcv = mm(jax.nn.gelu(pre).astype(BF16), w2_ref[...])
    o_ref[0:LANES, :] = jnp.zeros((LANES, LANES), o_ref.dtype)
    o_ref[LANES:LANES + nrow, :] = kcv.astype(o_ref.dtype)


def _nsa_compress(ck, cv, pk, pv, w1k, w1v, w2, ncp):
    B, nchunk, width = ck.shape
    assert ncp == LANES + nchunk

    def full(a):
        return pl.BlockSpec(a.shape, lambda b: (0,) * a.ndim)

    chunk = pl.BlockSpec((None, nchunk, width), lambda b: (b, 0, 0))
    return pl.pallas_call(
        _compress_kernel,
        grid=(B,),
        in_specs=[chunk, chunk, full(pk), full(pv), full(w1k), full(w1v), full(w2)],
        out_specs=pl.BlockSpec((None, ncp, LANES), lambda b: (b, 0, 0)),
        out_shape=jax.ShapeDtypeStruct((B, ncp, LANES), BF16),
        compiler_params=_cparams("arbitrary"),
        name="nsa_compress",
    )(ck, cv, pk, pv, w1k, w1v, w2)


def _cmp_kernel(q_ref, kcv_ref, bias_ref, agg_ref, oc_ref, sel_ref, s_scr, *, n_cmp, ncp):
    n = pl.program_id(1)
    QB = Q_BLOCK
    kcv = kcv_ref[...]
    cp = lax.broadcasted_iota(I32, (QB, ncp), 1)
    qpos = n * QB + lax.broadcasted_iota(I32, (QB, ncp), 0)
    valid = ((cp >= LANES) & (cp < LANES + n_cmp)
             & (qpos - NSA_CMP_STRIDE * (cp - LANES) - (NSA_CMP_LEN - 1) >= 0))
    woff = pl.multiple_of((n // 16) * LANES, LANES)
    scale = HEAD_DIM ** -0.5

    def logits(h):
        g = h // 2
        qx = _lane_window(q_ref[:, LANES * g:LANES * (g + 1)].astype(F32), 0, HEAD_DIM, h % 2 == 1, scale)
        s_scr[h] = lax.dot_general(qx, kcv, _NT, preferred_element_type=F32)
        s_scr[h, :, pl.ds(woff, 2 * LANES)] += bias_ref[h]

    def softmax(h):
        s = jnp.where(valid, s_scr[h], NEG_INF)
        m = jnp.max(s, axis=1, keepdims=True)
        p = jnp.where(valid, jnp.exp(s - m), 0.0)
        den = jnp.sum(p, axis=1, keepdims=True)
        return p * (1.0 / jnp.maximum(den, TINY))

    for h in range(NSA_HEADS):
        logits(h)
    ps = [softmax(h) for h in range(NSA_HEADS)]
    outs = [jnp.dot(p.astype(BF16), kcv, preferred_element_type=F32) for p in ps]
    for g in range(NSA_HEADS // 2):
        oc_ref[:, LANES * g:LANES * (g + 1)] = _pair_halves(outs[2 * g], outs[2 * g + 1], 1, 1).astype(oc_ref.dtype)

    psum = (ps[0] + ps[1]) + (ps[2] + ps[3])
    imp = jnp.dot(psum.astype(BF16), agg_ref[...], preferred_element_type=F32)

    v = imp.T
    jb = lax.broadcasted_iota(I32, v.shape, 0)
    qp = n * QB + lax.broadcasted_iota(I32, v.shape, 1)
    cur = qp >> 6
    forced = (jb == 0) | (jb == cur) | (jb == cur - 1)
    v = jnp.where(forced, -jnp.inf, jnp.where(jb * NSA_SLC_LEN <= qp, v, -1.0))
    rid = jb.astype(F32)
    picked = jnp.where(forced, 1.0, 0.0)
    for _ in range(NSA_TOPK - 3):
        mx = jnp.max(v, axis=0, keepdims=True)
        first = jnp.min(jnp.where(v == mx, rid, float(LANES)), axis=0, keepdims=True)
        hit = rid == first
        picked = jnp.where(hit, 1.0, picked)
        v = jnp.where(hit, -jnp.inf, v)
    sel_ref[...] = picked


def _nsa_cmp_select(dsrc, kcv, bias, agg, *, B, S, n_cmp, ncp):
    return pl.pallas_call(
        functools.partial(_cmp_kernel, n_cmp=n_cmp, ncp=ncp),
        grid=(B, S // Q_BLOCK),
        in_specs=[pl.BlockSpec((None, Q_BLOCK, 2 * LANES), lambda b, n: (b, n, 0)),
                  pl.BlockSpec((None, ncp, LANES), lambda b, n: (b, 0, 0)),
                  pl.BlockSpec((NSA_HEADS, None, Q_BLOCK, 2 * LANES), lambda b, n: (0, n % 16, 0, 0)),
                  pl.BlockSpec((ncp, LANES), lambda b, n: (0, 0))],
        out_specs=[pl.BlockSpec((None, Q_BLOCK, 2 * LANES), lambda b, n: (b, n, 0)),
                   pl.BlockSpec((None, LANES, Q_BLOCK), lambda b, n: (b, 0, n))],
        out_shape=[jax.ShapeDtypeStruct((B, S, 2 * LANES), BF16),
                   jax.ShapeDtypeStruct((B, LANES, S), F32)],
        scratch_shapes=[pltpu.VMEM((NSA_HEADS, Q_BLOCK, ncp), F32)],
        compiler_params=_cparams("arbitrary", "arbitrary"),
        name="nsa_cmp_select",
    )(dsrc, kcv, bias, agg)


def _merge_kernel(x_ref, g_ref, ya_ref, yb_ref, yc_ref, oc_ref, os_ref, ow_ref, dg_ref,
                  wg_ref, bg_ref, wb_ref, wo_ref, o_ref):
    x = x_ref[...]
    h = _rms(x, g_ref[...]).astype(BF16)
    tm = x.shape[0]
    lane = lax.broadcasted_iota(I32, (tm, LANES), 1)
    gd = jax.nn.sigmoid(dg_ref[...].astype(F32))
    yd = []
    for g in range(NSA_HEADS // 2):
        acc = jnp.zeros((tm, LANES), F32)
        for br, src in enumerate((oc_ref, os_ref, ow_ref)):
            ge = gd[:, 3 * (2 * g) + br:3 * (2 * g) + br + 1]
            go = gd[:, 3 * (2 * g + 1) + br:3 * (2 * g + 1) + br + 1]
            acc = acc + jnp.where(lane < 64, ge, go) * src[:, LANES * g:LANES * (g + 1)].astype(F32)
        yd.append(acc)
    y_d = jnp.concatenate(yd, axis=1).astype(BF16)
    ys = (ya_ref[...], yb_ref[...], yc_ref[...], y_d)
    merged = jnp.zeros((tm, D_MODEL), F32)
    for br in range(4):
        gate = jax.nn.sigmoid(jnp.dot(h, wg_ref[br], preferred_element_type=F32) + bg_ref[br:br + 1, :])
        merged = merged + gate * jnp.dot(ys[br], wb_ref[br], preferred_element_type=F32)
    o_ref[...] = x + jnp.dot(merged.astype(BF16), wo_ref[...], preferred_element_type=F32)


def _merge(x, g, ya, yb, yc, oc, osel, ow, dproj, wg, bg, wb, wo):
    T = x.shape[0]
    tm = 256

    def tile(width, col=0):
        return pl.BlockSpec((tm, width), lambda i: (i, col))

    def full(a):
        return pl.BlockSpec(a.shape, lambda i: (0,) * a.ndim)

    return pl.pallas_call(
        _merge_kernel,
        grid=(T // tm,),
        in_specs=[tile(D_MODEL), full(g), tile(256), tile(256), tile(256), tile(256), tile(256), tile(256),
                  tile(LANES, 5), full(wg), full(bg), full(wb), full(wo)],
        out_specs=tile(D_MODEL),
        out_shape=jax.ShapeDtypeStruct((T, D_MODEL), F32),
        compiler_params=_cparams("arbitrary"),
        name="merge",
    )(x, g, ya, yb, yc, oc, osel, ow, dproj, wg, bg, wb, wo)


def _moe_kernel(x_ref, g_ref, wr_ref, br_ref, wg_ref, wu_ref, wd_ref, fg_ref, o_ref,
                hn_ref, comb_ref, acc_ref, *, final_norm):
    e = pl.program_id(1)
    tm = x_ref.shape[0]
    lane = lax.broadcasted_iota(I32, (tm, LANES), 1)

    @pl.when(e == 0)
    def _():
        hn = _rms(x_ref[...], g_ref[...])
        hn_ref[...] = hn.astype(BF16)
        logits = jnp.dot(hn, wr_ref[...], preferred_element_type=F32,
                         precision=lax.Precision.HIGHEST) + br_ref[...]
        is_grp = (lane >= N_EXPERTS) & (lane < N_EXPERTS + N_GROUPS)
        lanef = lane.astype(F32)
        lg = jnp.where(is_grp, logits, -jnp.inf)
        mg = jnp.max(lg, axis=1, keepdims=True)
        gsel = jnp.min(jnp.where(lg == mg, lanef, 2.0 * LANES), axis=1, keepdims=True) - N_EXPERTS
        g_w = 1.0 / jnp.sum(jnp.where(is_grp, jnp.exp(lg - mg), 0.0), axis=1, keepdims=True)
        in_grp = (lane < N_EXPERTS) & ((lane >> 3).astype(F32) == gsel)
        v1 = jnp.where(in_grp, logits, -jnp.inf)
        m1 = jnp.max(v1, axis=1, keepdims=True)
        i1 = jnp.min(jnp.where(v1 == m1, lanef, 2.0 * LANES), axis=1, keepdims=True)
        v2 = jnp.where(lanef == i1, -jnp.inf, v1)
        m2 = jnp.max(v2, axis=1, keepdims=True)
        i2 = jnp.min(jnp.where(v2 == m2, lanef, 2.0 * LANES), axis=1, keepdims=True)
        e2 = jnp.exp(m2 - m1)
        w1 = g_w / (1.0 + e2)
        w2 = g_w * e2 / (1.0 + e2)
        comb_ref[...] = jnp.where(lanef == i1, w1, 0.0) + jnp.where(lanef == i2, w2, 0.0)
        acc_ref[...] = jnp.zeros_like(acc_ref)

    hn = hn_ref[...]
    gt = jnp.dot(hn, wg_ref[...], preferred_element_type=F32)
    up = jnp.dot(hn, wu_ref[...], preferred_element_type=F32)
    he = (gt * jax.nn.sigmoid(gt) * up).astype(BF16)
    ye = jnp.dot(he, wd_ref[...], preferred_element_type=F32)
    cw = jnp.sum(jnp.where(lane == e, comb_ref[...], 0.0), axis=1, keepdims=True)
    acc_ref[...] += cw * ye

    @pl.when(e == N_EXPERTS - 1)
    def _():
        y = x_ref[...] + acc_ref[...]
        if final_norm:
            y = _rms(y, fg_ref[...])
        o_ref[...] = y


def _moe(x, g, w_router, b_router, wg, wu, wd, final_g, final_norm):
    T = x.shape[0]
    tm = 1024
    return pl.pallas_call(
        functools.partial(_moe_kernel, final_norm=final_norm),
        grid=(T // tm, N_EXPERTS),
        in_specs=[pl.BlockSpec((tm, D_MODEL), lambda i, e: (i, 0)),
                  pl.BlockSpec((1, D_MODEL), lambda i, e: (0, 0)),
                  pl.BlockSpec((D_MODEL, LANES), lambda i, e: (0, 0)),
                  pl.BlockSpec((1, LANES), lambda i, e: (0, 0)),
                  pl.BlockSpec((None, D_MODEL, EXPERT_FF), lambda i, e: (e, 0, 0)),
                  pl.BlockSpec((None, D_MODEL, EXPERT_FF), lambda i, e: (e, 0, 0)),
                  pl.BlockSpec((None, EXPERT_FF, D_MODEL), lambda i, e: (e, 0, 0)),
                  pl.BlockSpec((1, D_MODEL), lambda i, e: (0, 0))],
        out_specs=pl.BlockSpec((tm, D_MODEL), lambda i, e: (i, 0)),
        out_shape=jax.ShapeDtypeStruct((T, D_MODEL), F32),
        scratch_shapes=[pltpu.VMEM((tm, D_MODEL), BF16),
                        pltpu.VMEM((tm, LANES), F32),
                        pltpu.VMEM((tm, D_MODEL), F32)],
        compiler_params=_cparams("arbitrary", "arbitrary"),
        name="moe",
    )(x, g, w_router, b_router, wg, wu, wd, final_g)


def _block_diag(w):
    nb, bs, _ = w.shape
    out = jnp.zeros((nb * bs, nb * bs), w.dtype)
    for i in range(nb):
        out = out.at[i * bs:(i + 1) * bs, i * bs:(i + 1) * bs].set(w[i])
    return out


def _selection_aggregator_np(n_cmp, n_slc):
    ratio_s = NSA_SLC_LEN // NSA_CMP_STRIDE
    ratio_c = NSA_CMP_LEN // NSA_CMP_STRIDE
    w = np.convolve(np.ones(ratio_s), np.ones(ratio_c)).astype(np.float32)
    cj = np.arange(n_slc)[:, None] * ratio_s - (ratio_c - 1) + np.arange(w.size)[None, :]
    jj = np.broadcast_to(np.arange(n_slc)[:, None], cj.shape)
    ww = np.broadcast_to(w[None, :], cj.shape)
    keep = (cj >= 0) & (cj < n_cmp)
    agg = np.zeros((n_cmp, n_slc), np.float32)
    np.add.at(agg, (cj[keep], jj[keep]), ww[keep])
    return agg


def _row(v):
    return v.reshape(1, -1).astype(F32)


def _diff_lambda_init(layer):
    return 0.8 - 0.6 * math.exp(-0.3 * layer)


def _mixer_layer(x, l, B, S, tiles, p):
    (bias_swa, bias_diff, bias_slc, bias_win, bias_cmp, agg_pad, n_cmp, ncp) = tiles
    w_in = jnp.pad(p["w_in"][l], ((0, 0), (0, N_IN_PAD - p["w_in"].shape[2]))).astype(BF16)
    w_t = jnp.concatenate([w_in[:, 1536:1792], w_in[:, 2176:2304]], axis=1).T
    a_xg, b_qkv, c_qkv, d_all, vals_t = _in_proj(x, _row(p["mix_norm_g"][l]), w_in, w_t)

    y_a = _rglru(a_xg, p["conv_w"][l].astype(F32), _row(p["conv_b"][l]),
                 _block_diag(p["lru_w_a"][l]).astype(BF16), _row(p["lru_b_a"][l]),
                 _block_diag(p["lru_w_x"][l]).astype(BF16), _row(p["lru_b_x"][l]),
                 _row(p["lru_lambda"][l]), B, S)

    b3 = b_qkv.reshape(B, S, 512)
    swa_cfg = tuple(((h // 2) * 64, (h % 2) != (h // 2), h // 2) for h in range(SWA_HEADS))
    y_b = _banded_attention(b3, 0, b3, 2, b3, 3, bias_swa, _row(p["swa_sinks"][l]),
                            B=B, S=S, window=SWA_WINDOW, q_cfg=swa_cfg)

    c3 = c_qkv.reshape(B, S, 768)
    lam4 = jnp.stack([p["diff_lq1"][l], p["diff_lk1"][l], p["diff_lq2"][l], p["diff_lk2"][l]]).astype(F32)
    gsub = jnp.tile(p["diff_subln_g"][l].reshape(-1, 1).astype(F32), (2, 1))
    y_c = _flash_attention("diff", c3, 0, c3, 1, vals_t, 256, 0, bias_diff, (lam4, gsub), B=B, S=S,
                           lambda_init=_diff_lambda_init(l))

    d3 = d_all.reshape(B, S, 768)
    nchunk = S // NSA_CMP_STRIDE
    ck = d3[:, :, 256:320].reshape(B, nchunk, NSA_CMP_STRIDE * HEAD_DIM)
    cv = d3[:, :, 320:384].reshape(B, nchunk, NSA_CMP_STRIDE * HEAD_DIM)

    def pos8(a):
        return jnp.broadcast_to(a.reshape(1, -1), (8, a.size)).astype(BF16)

    w1k = jnp.pad(p["nsa_w1_k"][l], ((0, 0), (0, HEAD_DIM))).astype(BF16)
    w1v = jnp.pad(p["nsa_w1_v"][l], ((0, 0), (HEAD_DIM, 0))).astype(BF16)
    w2 = _block_diag(jnp.stack([p["nsa_w2_k"][l], p["nsa_w2_v"][l]])).astype(BF16)
    kcv = _nsa_compress(ck, cv, pos8(p["nsa_pos_k"][l]), pos8(p["nsa_pos_v"][l]), w1k, w1v, w2, ncp)
    o_c, sel = _nsa_cmp_select(d3, kcv, bias_cmp, agg_pad, B=B, S=S, n_cmp=n_cmp, ncp=ncp)
    o_s = _flash_attention("nsa", d3, 0, d3, 3, vals_t, 128, 2, bias_slc, (sel,), B=B, S=S)
    win_cfg = tuple((0, h % 2 == 1, 1) for h in range(NSA_HEADS))
    o_w = _banded_attention(d3, 0, d3, 4, None, None, bias_win, None, B=B, S=S, window=NSA_WINDOW, q_cfg=win_cfg)

    T = B * S
    return _merge(x, _row(p["mix_norm_g"][l]), y_a, y_b.reshape(T, 256), y_c.reshape(T, 256),
                  o_c.reshape(T, 256), o_s.reshape(T, 256), o_w.reshape(T, 256), d_all,
                  p["w_gate"][l].astype(BF16), p["b_gate"][l].astype(F32), p["w_branch"][l].astype(BF16),
                  p["w_out"][l].astype(BF16))


def _moe_layer(x, l, p, final_norm):
    w_router = jnp.concatenate([p["w_router_exp"][l], p["w_router_grp"][l]], axis=1)
    w_router = jnp.pad(w_router, ((0, 0), (0, LANES - w_router.shape[1]))).astype(F32)
    b_router = jnp.concatenate([p["b_router_exp"][l], p["b_router_grp"][l]])
    b_router = jnp.pad(b_router, (0, LANES - b_router.shape[0])).reshape(1, LANES).astype(F32)
    return _moe(x, _row(p["ffn_norm_g"][l]), w_router, b_router, p["w_exp_gate"][l].astype(BF16),
                p["w_exp_up"][l].astype(BF16), p["w_exp_down"][l].astype(BF16), _row(p["final_norm_g"]), final_norm)


def _forward(p):
    x = p["x"]
    B, S, D = x.shape
    depth = p["w_in"].shape[0]
    assert D == D_MODEL and S % (16 * Q_BLOCK) == 0 and S // NSA_SLC_LEN <= LANES
    rel = p["rel_bias"].astype(F32)
    n_cmp = (S - NSA_CMP_LEN) // NSA_CMP_STRIDE + 1
    ncp = LANES + S // NSA_CMP_STRIDE
    agg = _selection_aggregator_np(n_cmp, S // NSA_SLC_LEN)
    agg_pad = np.zeros((ncp, LANES), np.float32)
    agg_pad[LANES:LANES + n_cmp, :agg.shape[1]] = agg
    tiles = (
        _bias_tiles(rel, _banded_buckets(1), 0, SWA_HEADS, False),
        _bias_tiles(rel, _flash_buckets(), SWA_HEADS, DIFF_HEADS, True, LOG2E),
        _bias_tiles(rel, _flash_buckets(), SWA_HEADS + DIFF_HEADS, NSA_HEADS, True, LOG2E),
        _bias_tiles(rel, _banded_buckets(NSA_WINDOW // Q_BLOCK), SWA_HEADS + DIFF_HEADS, NSA_HEADS, False),
        _bias_tiles(rel, _cmp_buckets(), SWA_HEADS + DIFF_HEADS, NSA_HEADS, True),
        jnp.asarray(agg_pad, BF16), n_cmp, ncp,
    )
    xt = x.reshape(B * S, D).astype(F32)
    for l in range(depth):
        xt = _mixer_layer(xt, l, B, S, tiles, p)
        xt = _moe_layer(xt, l, p, final_norm=(l == depth - 1))
    return xt.reshape(B, S, D).astype(x.dtype)


def kernel(x, rel_bias, final_norm_g, mix_norm_g, w_in, conv_w, conv_b, lru_w_a, lru_b_a, lru_w_x, lru_b_x,
           lru_lambda, swa_sinks, diff_lq1, diff_lk1, diff_lq2, diff_lk2, diff_subln_g, nsa_pos_k, nsa_w1_k,
           nsa_w2_k, nsa_pos_v, nsa_w1_v, nsa_w2_v, w_gate, b_gate, w_branch, w_out, ffn_norm_g, w_router_grp,
           b_router_grp, w_router_exp, b_router_exp, w_exp_gate, w_exp_up, w_exp_down):
    return _forward(dict(
        x=x, rel_bias=rel_bias, final_norm_g=final_norm_g, mix_norm_g=mix_norm_g, w_in=w_in, conv_w=conv_w,
        conv_b=conv_b, lru_w_a=lru_w_a, lru_b_a=lru_b_a, lru_w_x=lru_w_x, lru_b_x=lru_b_x, lru_lambda=lru_lambda,
        swa_sinks=swa_sinks, diff_lq1=diff_lq1, diff_lk1=diff_lk1, diff_lq2=diff_lq2, diff_lk2=diff_lk2,
        diff_subln_g=diff_subln_g, nsa_pos_k=nsa_pos_k, nsa_w1_k=nsa_w1_k, nsa_w2_k=nsa_w2_k, nsa_pos_v=nsa_pos_v,
        nsa_w1_v=nsa_w1_v, nsa_w2_v=nsa_w2_v, w_gate=w_gate, b_gate=b_gate, w_branch=w_branch, w_out=w_out,
        ffn_norm_g=ffn_norm_g, w_router_grp=w_router_grp, b_router_grp=b_router_grp, w_router_exp=w_router_exp,
        b_router_exp=b_router_exp, w_exp_gate=w_exp_gate, w_exp_up=w_exp_up, w_exp_down=w_exp_down))
```

```python
import functools
import math

import numpy as np
import jax
import jax.numpy as jnp
from jax import lax
from jax.experimental import pallas as pl
from jax.experimental.pallas import tpu as pltpu

F32 = jnp.float32
BF16 = jnp.bfloat16
I32 = jnp.int32

D_MODEL = 1024
HEAD_DIM = 64
MIX_WIDTH = 256
LRU_C = 8.0
CONV_WIDTH = 4
SWA_HEADS = 4
SWA_WINDOW = 128
DIFF_HEADS = 4
DIFF_DIM = 32
DIFF_SUBLN_EPS = 1e-5
NSA_HEADS = 4
NSA_CMP_LEN = 32
NSA_CMP_STRIDE = 16
NSA_SLC_LEN = 64
NSA_TOPK = 16
NSA_WINDOW = 512
NSA_FORCED_SCORE = 1e4
REL_BUCKETS = 32
REL_MAX_DIST = 128
N_GROUPS = 4
EXPERTS_PER_GROUP = 8
N_EXPERTS = 32
EXPERT_FF = 256
NORM_EPS = 1e-6
NEG_INF = -1e30
TINY = 1e-30
LOG2E = math.log2(math.e)

LANES = 128
Q_BLOCK = 128
BANDED_Q_BLOCKS = 2
FLASH_BLOCK = 256
MOE_TILE = 512
MOE_CAP = 64
MOE_TILES_PER_STEP = 8
FLASH_FAR_BLOCKS = 2
N_IN_PAD = 2560
VMEM_LIMIT = 56 * 1024 * 1024

_NT = (((1,), (1,)), ((), ()))


def _cparams(*sem):
    return pltpu.CompilerParams(dimension_semantics=sem, vmem_limit_bytes=VMEM_LIMIT)


def _t5_bucket_np(dist):
    n = np.maximum(dist, 0)
    exact = REL_BUCKETS // 2
    scaled = (np.log(np.maximum(n, 1).astype(np.float32) / np.float32(exact))
              / np.float32(math.log(REL_MAX_DIST / exact))).astype(np.float32)
    large = np.minimum(exact + (scaled * np.float32(REL_BUCKETS - exact)).astype(np.int32), REL_BUCKETS - 1)
    return np.where(n < exact, n, large).astype(np.int32)


def _bias_tile_kernel(tab_ref, idx_ref, o_ref, *, head0, shift, scale):
    h = pl.program_id(0) + head0
    idx = idx_ref[0]
    acc = jnp.zeros(idx.shape, F32)
    for b in range(REL_BUCKETS):
        acc = jnp.where(idx == b, tab_ref[b, h], acc)
    if shift:
        acc = acc - tab_ref[REL_BUCKETS - 1, h]
    if scale != 1.0:
        acc = acc * scale
    o_ref[0, 0] = acc


def _bias_tiles(rel_bias, buckets, head0, nheads, shift, scale=1.0):
    R, M, N = buckets.shape
    return pl.pallas_call(
        functools.partial(_bias_tile_kernel, head0=head0, shift=shift, scale=scale),
        grid=(nheads, R),
        in_specs=[pl.BlockSpec(memory_space=pltpu.SMEM),
                  pl.BlockSpec((1, M, N), lambda h, r: (r, 0, 0))],
        out_specs=pl.BlockSpec((1, 1, M, N), lambda h, r: (h, r, 0, 0)),
        out_shape=jax.ShapeDtypeStruct((nheads, R, M, N), F32),
        compiler_params=_cparams("arbitrary", "arbitrary"),
        name="bias_tiles",
    )(rel_bias, jnp.asarray(buckets))


def _banded_buckets(nback):
    i = np.arange(BANDED_Q_BLOCKS * Q_BLOCK)[:, None]
    j = np.arange((nback + BANDED_Q_BLOCKS) * Q_BLOCK)[None, :]
    return _t5_bucket_np(i - j + nback * Q_BLOCK)[None]


def _flash_buckets():
    k = np.arange(FLASH_BLOCK)[:, None]
    q = np.arange(FLASH_BLOCK)[None, :]
    return np.stack([_t5_bucket_np(q - k), _t5_bucket_np(q - k + FLASH_BLOCK)])


def _cmp_buckets():
    i = np.arange(Q_BLOCK)[None, :, None]
    c2 = np.arange(2 * LANES)[None, None, :]
    r = np.arange(16)[:, None, None]
    dist = i - NSA_CMP_STRIDE * (c2 - LANES - 8 * r) - (NSA_CMP_LEN - 1)
    return _t5_bucket_np(dist)


def _rms(x, g):
    return x * lax.rsqrt(jnp.mean(x * x, axis=-1, keepdims=True) + NORM_EPS) * g


def _proj_kernel(x_ref, g_ref, w_ref, wt_ref, oa_ref, ob_ref, oc_ref, od_ref, ot_ref):
    h = _rms(x_ref[...], g_ref[...]).astype(BF16)
    oa_ref[...] = jnp.dot(h, w_ref[:, 0:512], preferred_element_type=F32)
    ob_ref[...] = jnp.dot(h, w_ref[:, 512:1024], preferred_element_type=F32).astype(BF16)
    oc_ref[...] = jnp.dot(h, w_ref[:, 1024:1792], preferred_element_type=F32).astype(BF16)
    od_ref[...] = jnp.dot(h, w_ref[:, 1792:2560], preferred_element_type=F32).astype(BF16)
    ot_ref[...] = lax.dot_general(wt_ref[...], h, _NT, preferred_element_type=F32).astype(BF16)


def _in_proj(x, g, w, wt):
    T = x.shape[0]
    tm = 512
    nt = wt.shape[0]
    return pl.pallas_call(
        _proj_kernel,
        grid=(T // tm,),
        in_specs=[pl.BlockSpec((tm, D_MODEL), lambda i: (i, 0)),
                  pl.BlockSpec((1, D_MODEL), lambda i: (0, 0)),
                  pl.BlockSpec((D_MODEL, N_IN_PAD), lambda i: (0, 0)),
                  pl.BlockSpec((nt, D_MODEL), lambda i: (0, 0))],
        out_specs=[pl.BlockSpec((tm, 512), lambda i: (i, 0)),
                   pl.BlockSpec((tm, 512), lambda i: (i, 0)),
                   pl.BlockSpec((tm, 768), lambda i: (i, 0)),
                   pl.BlockSpec((tm, 768), lambda i: (i, 0)),
                   pl.BlockSpec((nt, tm), lambda i: (0, i))],
        out_shape=[jax.ShapeDtypeStruct((T, 512), F32),
                   jax.ShapeDtypeStruct((T, 512), BF16),
                   jax.ShapeDtypeStruct((T, 768), BF16),
                   jax.ShapeDtypeStruct((T, 768), BF16),
                   jax.ShapeDtypeStruct((nt, T), BF16)],
        compiler_params=_cparams("arbitrary"),
        name="in_proj",
    )(x, g, w, wt)


def _lru_kernel(xg_ref, cw_ref, cb_ref, wa_ref, ba_ref, wx_ref, bx_ref, lam_ref, o_ref,
                ext_ref, h_ref, *, Tt):
    t = pl.program_id(1)

    W = MIX_WIDTH

    @pl.when(t == 0)
    def _():
        ext_ref[0:8, :] = jnp.zeros((8, W), F32)
        h_ref[...] = jnp.zeros_like(h_ref)

    x = xg_ref[:, 0:W]
    gate = xg_ref[:, W:2 * W]
    ext_ref[8:Tt + 8, :] = x
    row = lax.broadcasted_iota(I32, (Tt, W), 0)
    xc = cb_ref[...] + x * cw_ref[CONV_WIDTH - 1:CONV_WIDTH, :]
    for s in range(1, CONV_WIDTH):
        xc = xc + ext_ref[8 - s:8 - s + Tt, :] * cw_ref[CONV_WIDTH - 1 - s:CONV_WIDTH - s, :]
    ext_ref[0:8, :] = ext_ref[Tt:Tt + 8, :]

    xcb = xc.astype(BF16)
    r = jax.nn.sigmoid(jnp.dot(xcb, wa_ref[...], preferred_element_type=F32) + ba_ref[...])
    ig = jax.nn.sigmoid(jnp.dot(xcb, wx_ref[...], preferred_element_type=F32) + bx_ref[...])
    z = -lam_ref[...]
    softplus = jnp.maximum(z, 0.0) + jnp.log1p(jnp.exp(-jnp.abs(z)))
    log_a = (-LRU_C) * r * softplus
    a = jnp.exp(log_a)
    mult = jnp.sqrt(1.0 - a * a)
    mult = jnp.where((row + t * Tt) == 0, 1.0, mult)
    b = mult * ig * xc

    d = 1
    while d < Tt:
        keep = row >= d
        b = jnp.where(keep, a * pltpu.roll(b, d, 0) + b, b)
        a = jnp.where(keep, a * pltpu.roll(a, d, 0), a)
        d *= 2
    hs = a * h_ref[7:8, :] + b
    h_ref[...] = hs[Tt - 8:Tt]
    o_ref[...] = (hs * jax.nn.gelu(gate)).astype(o_ref.dtype)


def _rglru(a_xg, conv_w, conv_b, wa_bd, b_a, wx_bd, b_x, lam, B, S):
    Tt = 256
    W = MIX_WIDTH
    vec = pl.BlockSpec((1, W), lambda b, t: (0, 0))
    mat = pl.BlockSpec((W, W), lambda b, t: (0, 0))
    out = pl.pallas_call(
        functools.partial(_lru_kernel, Tt=Tt),
        grid=(B, S // Tt),
        in_specs=[pl.BlockSpec((None, Tt, 2 * W), lambda b, t: (b, t, 0)),
                  pl.BlockSpec((CONV_WIDTH, W), lambda b, t: (0, 0)),
                  vec, mat, vec, mat, vec, vec],
        out_specs=pl.BlockSpec((None, Tt, W), lambda b, t: (b, t, 0)),
        out_shape=jax.ShapeDtypeStruct((B, S, W), BF16),
        scratch_shapes=[pltpu.VMEM((Tt + 8, W), F32), pltpu.VMEM((8, W), F32)],
        compiler_params=_cparams("arbitrary", "arbitrary"),
        name="rglru",
    )(a_xg.reshape(B, S, 2 * W), conv_w, conv_b, wa_bd, b_a, wx_bd, b_x, lam)
    return out.reshape(B * S, W)


def _lane_window(q128, lo, width, roll, scale):
    q = q128 * scale
    if roll:
        q = pltpu.roll(q, LANES // 2, 1)
    lane = lax.broadcasted_iota(I32, q.shape, 1)
    return jnp.where((lane >= lo) & (lane < lo + width), q, 0.0).astype(BF16)


def _pair_halves(even, odd, even_half, odd_half):
    if even_half == 1:
        even = pltpu.roll(even, LANES // 2, 1)
    if odd_half == 0:
        odd = pltpu.roll(odd, LANES // 2, 1)
    lane = lax.broadcasted_iota(I32, even.shape, 1)
    return jnp.where(lane < LANES // 2, even, odd)


def _banded_kernel(*refs, nback, window, q_cfg, has_sink, shared_kv):
    nk = nback + BANDED_Q_BLOCKS
    TQ = BANDED_Q_BLOCKS * Q_BLOCK
    q_ref = refs[0]
    k_refs = refs[1:1 + nk]
    pos = 1 + nk
    if shared_kv:
        v_refs = k_refs
    else:
        v_refs = refs[pos:pos + nk]
        pos += nk
    bias_ref = refs[pos]
    pos += 1
    sink_ref = None
    if has_sink:
        sink_ref = refs[pos]
        pos += 1
    o_ref = refs[pos]

    n = pl.program_id(1)
    KW = nk * Q_BLOCK
    kcat = jnp.concatenate([k_refs[nk - 1 - i][...] for i in range(nk)], axis=0)
    vcat = kcat if shared_kv else jnp.concatenate([v_refs[nk - 1 - i][...] for i in range(nk)], axis=0)
    ii = lax.broadcasted_iota(I32, (TQ, KW), 0)
    jj = lax.broadcasted_iota(I32, (TQ, KW), 1)
    dist = ii - jj + nback * Q_BLOCK
    valid = (dist >= 0) & (dist < window) & (jj >= (nback - BANDED_Q_BLOCKS * n) * Q_BLOCK)
    scale = HEAD_DIM ** -0.5

    def logits(h):
        lo, roll, _ = q_cfg[h]
        g = h // 2
        qx = _lane_window(q_ref[:, LANES * g:LANES * (g + 1)].astype(F32), lo, HEAD_DIM, roll, scale)
        s = lax.dot_general(qx, kcat, _NT, preferred_element_type=F32) + bias_ref[h, 0]
        return jnp.where(valid, s, NEG_INF)

    def softmax(h, s):
        m = jnp.max(s, axis=1, keepdims=True)
        if has_sink:
            sink = sink_ref[0:1, h:h + 1]
            m = jnp.maximum(m, sink)
        p = jnp.exp(s - m)
        den = jnp.sum(p, axis=1, keepdims=True)
        if has_sink:
            den = den + jnp.exp(sink - m)
        return p.astype(BF16), 1.0 / jnp.maximum(den, TINY)

    nh = len(q_cfg)
    ss = [logits(h) for h in range(nh)]
    pr = [softmax(h, ss[h]) for h in range(nh)]
    outs = [(jnp.dot(pr[h][0], vcat, preferred_element_type=F32) * pr[h][1], q_cfg[h][2]) for h in range(nh)]
    for g in range(len(q_cfg) // 2):
        (oe, he), (oo, ho) = outs[2 * g], outs[2 * g + 1]
        o_ref[:, LANES * g:LANES * (g + 1)] = _pair_halves(oe, oo, he, ho).astype(o_ref.dtype)


def _banded_attention(qsrc, q_col, ksrc, k_col, vsrc, v_col, bias, sinks, *, B, S, window, q_cfg):
    nback = -(-window // Q_BLOCK)
    nq = BANDED_Q_BLOCKS
    nk = nback + nq
    TQ = nq * Q_BLOCK
    shared_kv = vsrc is None
    has_sink = sinks is not None

    def kv_spec(col, i):
        return pl.BlockSpec((None, Q_BLOCK, LANES), lambda b, n: (b, jnp.maximum(nq * n + nq - 1 - i, 0), col))

    in_specs = [pl.BlockSpec((None, TQ, 2 * LANES), lambda b, n: (b, n, q_col))]
    args = [qsrc]
    in_specs += [kv_spec(k_col, i) for i in range(nk)]
    args += [ksrc] * nk
    if not shared_kv:
        in_specs += [kv_spec(v_col, i) for i in range(nk)]
        args += [vsrc] * nk
    in_specs.append(pl.BlockSpec(bias.shape, lambda b, n: (0, 0, 0, 0)))
    args.append(bias)
    if has_sink:
        in_specs.append(pl.BlockSpec(sinks.shape, lambda b, n: (0, 0)))
        args.append(sinks)
    return pl.pallas_call(
        functools.partial(_banded_kernel, nback=nback, window=window, q_cfg=q_cfg,
                          has_sink=has_sink, shared_kv=shared_kv),
        grid=(B, S // TQ),
        in_specs=in_specs,
        out_specs=pl.BlockSpec((None, TQ, 2 * LANES), lambda b, n: (b, n, 0)),
        out_shape=jax.ShapeDtypeStruct((B, S, 2 * LANES), BF16),
        compiler_params=_cparams("arbitrary", "arbitrary"),
        name="banded_attn",
    )(*args)


def _flash_kernel(*refs, mode, lambda_init):
    TB = FLASH_BLOCK
    if mode == "diff":
        (q_ref, k_ref, vt_ref, bias_ref, lam_ref, g_ref, o_ref,
         qx_ref, m_ref, l_ref, acc_ref) = refs
        sel_ref = None
        maps = [dict(g=h // 2, lo=(h % 2) * 64 + 32 * mm, w=DIFF_DIM, roll=False, kg=h // 2, vrow=HEAD_DIM * h, h=h)
                for h in range(DIFF_HEADS) for mm in range(2)]
        scale = DIFF_DIM ** -0.5
    else:
        (q_ref, k_ref, vt_ref, sel_ref, bias_ref, o_ref,
         qx_ref, m_ref, l_ref, acc_ref) = refs
        maps = [dict(g=h // 2, lo=0, w=HEAD_DIM, roll=(h % 2 == 1), kg=0, vrow=HEAD_DIM, h=h)
                for h in range(NSA_HEADS)]
        scale = HEAD_DIM ** -0.5
    n = pl.program_id(1)

    for mp, c in enumerate(maps):
        qx_ref[mp] = _lane_window(q_ref[:, LANES * c["g"]:LANES * (c["g"] + 1)].astype(F32),
                                  c["lo"], c["w"], c["roll"], scale * LOG2E)
    m_ref[...] = jnp.full(m_ref.shape, NEG_INF, F32)
    l_ref[...] = jnp.zeros_like(l_ref)
    acc_ref[...] = jnp.zeros_like(acc_ref)

    key = lax.broadcasted_iota(I32, (TB, TB), 0)
    qry = lax.broadcasted_iota(I32, (TB, TB), 1)

    def step(j, kind, TK=TB):
        koff = pl.multiple_of(j * TB, TB)
        valid = None
        if kind == "diag":
            valid = qry >= key
        if sel_ref is not None:
            per = TK // NSA_SLC_LEN
            base = j * (TB // NSA_SLC_LEN)
            rows = [jnp.broadcast_to(sel_ref[pl.ds(base + i, 1), :], (NSA_SLC_LEN, TB)) for i in range(per)]
            picked = jnp.concatenate(rows, axis=0) > 0.5
            valid = picked if valid is None else (valid & picked)

        def logits(mp):
            c = maps[mp]
            kt = k_ref[pl.ds(koff, TK), LANES * c["kg"]:LANES * (c["kg"] + 1)]
            s = lax.dot_general(kt, qx_ref[mp], _NT, preferred_element_type=F32)
            if kind == "near":
                s = s + bias_ref[c["h"], 1]
            elif kind == "diag":
                s = s + bias_ref[c["h"], 0]
            if valid is not None:
                s = jnp.where(valid, s, NEG_INF)
            return s

        def softmax(mp, s):
            m_prev = m_ref[mp]
            m_new = jnp.maximum(m_prev, jnp.max(s, axis=0, keepdims=True))
            alpha = jnp.exp2(m_prev - m_new)
            p = jnp.exp2(s - m_new)
            l_ref[mp] = alpha * l_ref[mp] + jnp.sum(p, axis=0, keepdims=True)
            m_ref[mp] = m_new
            return p.astype(BF16), alpha

        def values(mp, p, alpha):
            c = maps[mp]
            vt = vt_ref[c["vrow"]:c["vrow"] + HEAD_DIM, pl.ds(koff, TK)]
            acc_ref[mp] = alpha * acc_ref[mp] + jnp.dot(vt, p, preferred_element_type=F32)

        ss = [logits(mp) for mp in range(len(maps))]
        pa = [softmax(mp, ss[mp]) for mp in range(len(maps))]
        for mp in range(len(maps)):
            values(mp, *pa[mp])

    nfar = jnp.maximum(n - 1, 0)
    FW = FLASH_FAR_BLOCKS

    def far_body(j, carry):
        step(FW * j, "far", FW * TB)
        return carry

    lax.fori_loop(0, nfar // FW, far_body, 0)

    def rest_body(j, carry):
        step(j, "far")
        return carry

    lax.fori_loop((nfar // FW) * FW, nfar, rest_body, 0)

    @pl.when(n >= 1)
    def _():
        step(n - 1, "near")

    step(n, "diag")

    def normalised(mp):
        return acc_ref[mp] * (1.0 / jnp.maximum(l_ref[mp], TINY))

    if mode == "diff":
        lq = lam_ref[...]
        lam = (jnp.exp(jnp.sum(lq[0:1] * lq[1:2], axis=1, keepdims=True))
               - jnp.exp(jnp.sum(lq[2:3] * lq[3:4], axis=1, keepdims=True)) + lambda_init)
        for g in range(DIFF_HEADS // 2):
            res = []
            for h in (2 * g, 2 * g + 1):
                o = normalised(2 * h) - lam * normalised(2 * h + 1)
                ms = jnp.sum(o * o, axis=0, keepdims=True) * (1.0 / (2 * DIFF_DIM))
                res.append(o * lax.rsqrt(ms + DIFF_SUBLN_EPS))
            out = jnp.concatenate(res, axis=0) * g_ref[...] * (1.0 - lambda_init)
            o_ref[:, LANES * g:LANES * (g + 1)] = out.T.astype(o_ref.dtype)
    else:
        for g in range(NSA_HEADS // 2):
            out = jnp.concatenate([normalised(2 * g), normalised(2 * g + 1)], axis=0)
            o_ref[:, LANES * g:LANES * (g + 1)] = out.T.astype(o_ref.dtype)


def _flash_attention(mode, qsrc, q_col, ksrc, k_col, vt, vt_rows, vt_blk, bias, extra, *, B, S, lambda_init=0.0):
    TB = FLASH_BLOCK
    q_spec = pl.BlockSpec((None, TB, 2 * LANES), lambda b, n: (b, n, q_col))
    bias_spec = pl.BlockSpec(bias.shape, lambda b, n: (0, 0, 0, 0))
    vt_spec = pl.BlockSpec((vt_rows, S), lambda b, n: (vt_blk, b))
    if mode == "diff":
        lam4, gsub = extra
        nmaps = 2 * DIFF_HEADS
        in_specs = [q_spec,
                    pl.BlockSpec((None, S, 2 * LANES), lambda b, n: (b, 0, k_col)),
                    vt_spec, bias_spec,
                    pl.BlockSpec(lam4.shape, lambda b, n: (0, 0)),
                    pl.BlockSpec(gsub.shape, lambda b, n: (0, 0))]
        args = [qsrc, ksrc, vt, bias, lam4, gsub]
    else:
        (sel,) = extra
        nmaps = NSA_HEADS
        in_specs = [q_spec,
                    pl.BlockSpec((None, S, LANES), lambda b, n: (b, 0, k_col)),
                    vt_spec,
                    pl.BlockSpec((None, LANES, TB), lambda b, n: (b, 0, n)),
                    bias_spec]
        args = [qsrc, ksrc, vt, sel, bias]
    return pl.pallas_call(
        functools.partial(_flash_kernel, mode=mode, lambda_init=lambda_init),
        grid=(B, S // TB),
        in_specs=in_specs,
        out_specs=pl.BlockSpec((None, TB, 2 * LANES), lambda b, n: (b, n, 0)),
        out_shape=jax.ShapeDtypeStruct((B, S, 2 * LANES), BF16),
        scratch_shapes=[pltpu.VMEM((nmaps, TB, LANES), BF16),
                        pltpu.VMEM((nmaps, 1, TB), F32),
                        pltpu.VMEM((nmaps, 1, TB), F32),
                        pltpu.VMEM((nmaps, HEAD_DIM, TB), F32)],
        compiler_params=_cparams("arbitrary", "arbitrary"),
        name="flash_" + mode,
    )(*args)


def _compress_kernel(ck_ref, cv_ref, pk_ref, pv_ref, w1k_ref, w1v_ref, w2_ref, o_ref):
    half = NSA_CMP_STRIDE * HEAD_DIM
    ck = ck_ref[...]
    cv = cv_ref[...]
    nrow = ck.shape[0]

    def mm(a, w):
        return jnp.dot(a, w, preferred_element_type=F32)

    top = mm(ck, w1k_ref[0:half, :]) + mm(cv, w1v_ref[0:half, :])
    bot = mm(ck, w1k_ref[half:2 * half, :]) + mm(cv, w1v_ref[half:2 * half, :])
    posw = (mm(pk_ref[...], w1k_ref[...]) + mm(pv_ref[...], w1v_ref[...]))[0:1]
    pre = top + pltpu.roll(bot, nrow - 1, 0) + posw
    kcv = mm(jax.nn.gelu(pre).astype(BF16), w2_ref[...])
    o_ref[0:LANES, :] = jnp.zeros((LANES, LANES), o_ref.dtype)
    o_ref[LANES:LANES + nrow, :] = kcv.astype(o_ref.dtype)


def _nsa_compress(ck, cv, pk, pv, w1k, w1v, w2, ncp):
    B, nchunk, width = ck.shape
    assert ncp == LANES + nchunk

    def full(a):
        return pl.BlockSpec(a.shape, lambda b: (0,) * a.ndim)

    chunk = pl.BlockSpec((None, nchunk, width), lambda b: (b, 0, 0))
    return pl.pallas_call(
        _compress_kernel,
        grid=(B,),
        in_specs=[chunk, chunk, full(pk), full(pv), full(w1k), full(w1v), full(w2)],
        out_specs=pl.BlockSpec((None, ncp, LANES), lambda b: (b, 0, 0)),
        out_shape=jax.ShapeDtypeStruct((B, ncp, LANES), BF16),
        compiler_params=_cparams("arbitrary"),
        name="nsa_compress",
    )(ck, cv, pk, pv, w1k, w1v, w2)


def _cmp_kernel(q_ref, kcv_ref, bias_ref, agg_ref, oc_ref, sel_ref, s_scr, *, n_cmp, ncp):
    n = pl.program_id(1)
    QB = Q_BLOCK
    kcv = kcv_ref[...]
    cp = lax.broadcasted_iota(I32, (QB, ncp), 1)
    qpos = n * QB + lax.broadcasted_iota(I32, (QB, ncp), 0)
    valid = ((cp >= LANES) & (cp < LANES + n_cmp)
             & (qpos - NSA_CMP_STRIDE * (cp - LANES) - (NSA_CMP_LEN - 1) >= 0))
    woff = pl.multiple_of((n // 16) * LANES, LANES)
    scale = HEAD_DIM ** -0.5

    def logits(h):
        g = h // 2
        qx = _lane_window(q_ref[:, LANES * g:LANES * (g + 1)].astype(F32), 0, HEAD_DIM, h % 2 == 1, scale)
        s_scr[h] = lax.dot_general(qx, kcv, _NT, preferred_element_type=F32)
        s_scr[h, :, pl.ds(woff, 2 * LANES)] += bias_ref[h]

    def softmax(h):
        s = jnp.where(valid, s_scr[h], NEG_INF)
        m = jnp.max(s, axis=1, keepdims=True)
        p = jnp.where(valid, jnp.exp(s - m), 0.0)
        den = jnp.sum(p, axis=1, keepdims=True)
        return p * (1.0 / jnp.maximum(den, TINY))

    for h in range(NSA_HEADS):
        logits(h)
    ps = [softmax(h) for h in range(NSA_HEADS)]
    outs = [jnp.dot(p.astype(BF16), kcv, preferred_element_type=F32) for p in ps]
    for g in range(NSA_HEADS // 2):
        oc_ref[:, LANES * g:LANES * (g + 1)] = _pair_halves(outs[2 * g], outs[2 * g + 1], 1, 1).astype(oc_ref.dtype)

    psum = (ps[0] + ps[1]) + (ps[2] + ps[3])
    imp = jnp.dot(psum.astype(BF16), agg_ref[...], preferred_element_type=F32)

    v = imp.T
    jb = lax.broadcasted_iota(I32, v.shape, 0)
    qp = n * QB + lax.broadcasted_iota(I32, v.shape, 1)
    cur = qp >> 6
    forced = (jb == 0) | (jb == cur) | (jb == cur - 1)
    v = jnp.where(forced, -jnp.inf, jnp.where(jb * NSA_SLC_LEN <= qp, v, -1.0))
    rid = jb.astype(F32)
    picked = jnp.where(forced, 1.0, 0.0)
    for _ in range(NSA_TOPK - 3):
        mx = jnp.max(v, axis=0, keepdims=True)
        first = jnp.min(jnp.where(v == mx, rid, float(LANES)), axis=0, keepdims=True)
        hit = rid == first
        picked = jnp.where(hit, 1.0, picked)
        v = jnp.where(hit, -jnp.inf, v)
    sel_ref[...] = picked


def _nsa_cmp_select(dsrc, kcv, bias, agg, *, B, S, n_cmp, ncp):
    return pl.pallas_call(
        functools.partial(_cmp_kernel, n_cmp=n_cmp, ncp=ncp),
        grid=(B, S // Q_BLOCK),
        in_specs=[pl.BlockSpec((None, Q_BLOCK, 2 * LANES), lambda b, n: (b, n, 0)),
                  pl.BlockSpec((None, ncp, LANES), lambda b, n: (b, 0, 0)),
                  pl.BlockSpec((NSA_HEADS, None, Q_BLOCK, 2 * LANES), lambda b, n: (0, n % 16, 0, 0)),
                  pl.BlockSpec((ncp, LANES), lambda b, n: (0, 0))],
        out_specs=[pl.BlockSpec((None, Q_BLOCK, 2 * LANES), lambda b, n: (b, n, 0)),
                   pl.BlockSpec((None, LANES, Q_BLOCK), lambda b, n: (b, 0, n))],
        out_shape=[jax.ShapeDtypeStruct((B, S, 2 * LANES), BF16),
                   jax.ShapeDtypeStruct((B, LANES, S), F32)],
        scratch_shapes=[pltpu.VMEM((NSA_HEADS, Q_BLOCK, ncp), F32)],
        compiler_params=_cparams("arbitrary", "arbitrary"),
        name="nsa_cmp_select",
    )(dsrc, kcv, bias, agg)


def _merge_kernel(x_ref, g_ref, ya_ref, yb_ref, yc_ref, oc_ref, os_ref, ow_ref, dg_ref,
                  wg_ref, bg_ref, wb_ref, wo_ref, o_ref):
    x = x_ref[...]
    h = _rms(x, g_ref[...]).astype(BF16)
    tm = x.shape[0]
    lane = lax.broadcasted_iota(I32, (tm, LANES), 1)
    gd = jax.nn.sigmoid(dg_ref[...].astype(F32))
    yd = []
    for g in range(NSA_HEADS // 2):
        acc = jnp.zeros((tm, LANES), F32)
        for br, src in enumerate((oc_ref, os_ref, ow_ref)):
            ge = gd[:, 3 * (2 * g) + br:3 * (2 * g) + br + 1]
            go = gd[:, 3 * (2 * g + 1) + br:3 * (2 * g + 1) + br + 1]
            acc = acc + jnp.where(lane < 64, ge, go) * src[:, LANES * g:LANES * (g + 1)].astype(F32)
        yd.append(acc)
    y_d = jnp.concatenate(yd, axis=1).astype(BF16)
    ys = (ya_ref[...], yb_ref[...], yc_ref[...], y_d)
    merged = jnp.zeros((tm, D_MODEL), F32)
    for br in range(4):
        gate = jax.nn.sigmoid(jnp.dot(h, wg_ref[br], preferred_element_type=F32) + bg_ref[br:br + 1, :])
        merged = merged + gate * jnp.dot(ys[br], wb_ref[br], preferred_element_type=F32)
    o_ref[...] = x + jnp.dot(merged.astype(BF16), wo_ref[...], preferred_element_type=F32)


def _merge(x, g, ya, yb, yc, oc, osel, ow, dproj, wg, bg, wb, wo):
    T = x.shape[0]
    tm = 256

    def tile(width, col=0):
        return pl.BlockSpec((tm, width), lambda i: (i, col))

    def full(a):
        return pl.BlockSpec(a.shape, lambda i: (0,) * a.ndim)

    return pl.pallas_call(
        _merge_kernel,
        grid=(T // tm,),
        in_specs=[tile(D_MODEL), full(g), tile(256), tile(256), tile(256), tile(256), tile(256), tile(256),
                  tile(LANES, 5), full(wg), full(bg), full(wb), full(wo)],
        out_specs=tile(D_MODEL),
        out_shape=jax.ShapeDtypeStruct((T, D_MODEL), F32),
        compiler_params=_cparams("arbitrary"),
        name="merge",
    )(x, g, ya, yb, yc, oc, osel, ow, dproj, wg, bg, wb, wo)


def _route(hn, w_split, b_router):
    lane = lax.broadcasted_iota(I32, (hn.shape[0], LANES), 1)
    hi = hn.astype(BF16)
    lo = (hn - hi.astype(F32)).astype(BF16)
    a = jnp.dot(hi, w_split, preferred_element_type=F32)
    b = jnp.dot(lo, w_split[:, 0:LANES], preferred_element_type=F32)
    logits = (a[:, 0:LANES] + a[:, LANES:2 * LANES]) + b + b_router
    is_grp = (lane >= N_EXPERTS) & (lane < N_EXPERTS + N_GROUPS)
    lanef = lane.astype(F32)
    lg = jnp.where(is_grp, logits, -jnp.inf)
    mg = jnp.max(lg, axis=1, keepdims=True)
    gsel = jnp.min(jnp.where(lg == mg, lanef, 2.0 * LANES), axis=1, keepdims=True) - N_EXPERTS
    g_w = 1.0 / jnp.sum(jnp.where(is_grp, jnp.exp(lg - mg), 0.0), axis=1, keepdims=True)
    in_grp = (lane < N_EXPERTS) & ((lane >> 3).astype(F32) == gsel)
    v1 = jnp.where(in_grp, logits, -jnp.inf)
    m1 = jnp.max(v1, axis=1, keepdims=True)
    i1 = jnp.min(jnp.where(v1 == m1, lanef, 2.0 * LANES), axis=1, keepdims=True)
    v2 = jnp.where(lanef == i1, -jnp.inf, v1)
    m2 = jnp.max(v2, axis=1, keepdims=True)
    i2 = jnp.min(jnp.where(v2 == m2, lanef, 2.0 * LANES), axis=1, keepdims=True)
    e2 = jnp.exp(m2 - m1)
    w1 = g_w / (1.0 + e2)
    w2 = g_w * e2 / (1.0 + e2)
    first, second = lanef == i1, lanef == i2
    comb = jnp.where(first, w1, 0.0) + jnp.where(second, w2, 0.0)
    return comb, jnp.where(first | second, 1.0, 0.0)


def _moe_kernel(x_ref, g_ref, wr_ref, br_ref, wg_ref, wu_ref, wd_ref, fg_ref, o_ref,
                hn_ref, comb_ref, acc_ref, *, final_norm):
    e = pl.program_id(1)
    tm = x_ref.shape[0]
    lane = lax.broadcasted_iota(I32, (tm, LANES), 1)

    @pl.when(e == 0)
    def _():
        hn = _rms(x_ref[...], g_ref[...])
        hn_ref[...] = hn.astype(BF16)
        comb_ref[...] = _route(hn, wr_ref[...], br_ref[...])[0]
        acc_ref[...] = jnp.zeros_like(acc_ref)

    hn = hn_ref[...]
    gt = jnp.dot(hn, wg_ref[...], preferred_element_type=F32)
    up = jnp.dot(hn, wu_ref[...], preferred_element_type=F32)
    he = (gt * jax.nn.sigmoid(gt) * up).astype(BF16)
    ye = jnp.dot(he, wd_ref[...], preferred_element_type=F32)
    cw = jnp.sum(jnp.where(lane == e, comb_ref[...], 0.0), axis=1, keepdims=True)
    acc_ref[...] += cw * ye

    @pl.when(e == N_EXPERTS - 1)
    def _():
        y = x_ref[...] + acc_ref[...]
        if final_norm:
            y = _rms(y, fg_ref[...])
        o_ref[...] = y


def _moe(x, g, w_router, b_router, wg, wu, wd, final_g, final_norm):
    T = x.shape[0]
    tm = 1024
    return pl.pallas_call(
        functools.partial(_moe_kernel, final_norm=final_norm),
        grid=(T // tm, N_EXPERTS),
        in_specs=[pl.BlockSpec((tm, D_MODEL), lambda i, e: (i, 0)),
                  pl.BlockSpec((1, D_MODEL), lambda i, e: (0, 0)),
                  pl.BlockSpec((D_MODEL, 2 * LANES), lambda i, e: (0, 0)),
                  pl.BlockSpec((1, LANES), lambda i, e: (0, 0)),
                  pl.BlockSpec((None, D_MODEL, EXPERT_FF), lambda i, e: (e, 0, 0)),
                  pl.BlockSpec((None, D_MODEL, EXPERT_FF), lambda i, e: (e, 0, 0)),
                  pl.BlockSpec((None, EXPERT_FF, D_MODEL), lambda i, e: (e, 0, 0)),
                  pl.BlockSpec((1, D_MODEL), lambda i, e: (0, 0))],
        out_specs=pl.BlockSpec((tm, D_MODEL), lambda i, e: (i, 0)),
        out_shape=jax.ShapeDtypeStruct((T, D_MODEL), F32),
        scratch_shapes=[pltpu.VMEM((tm, D_MODEL), BF16),
                        pltpu.VMEM((tm, LANES), F32),
                        pltpu.VMEM((tm, D_MODEL), F32)],
        compiler_params=_cparams("arbitrary", "arbitrary"),
        name="moe",
    )(x, g, w_router, b_router, wg, wu, wd, final_g)


def _slot_matrix(pos_t, val_t, ntok):
    slot = lax.broadcasted_iota(I32, (MOE_CAP, ntok), 0).astype(F32)
    blocks = []
    for e in range(N_EXPERTS):
        hit = pos_t[e:e + 1, :] == slot
        blocks.append(jnp.where(hit, 1.0 if val_t is None else val_t[e:e + 1, :], 0.0).astype(BF16))
    return jnp.concatenate(blocks, axis=0)


def _dispatch_kernel(x_ref, g_ref, wr_ref, br_ref, tri_ref, xd_ref, pos_ref, wt_ref, cnt_ref):
    tm = x_ref.shape[0]
    hn = _rms(x_ref[...], g_ref[...])
    comb, routed = _route(hn, wr_ref[...], br_ref[...])
    rank = jnp.dot(tri_ref[...], routed.astype(BF16), preferred_element_type=F32)
    pos = jnp.where(routed > 0.0, rank, -1.0)
    pos_t = pos.T
    pos_ref[...] = pos_t
    wt_ref[...] = comb.T
    cnt_ref[...] = jnp.broadcast_to(jnp.sum(routed, axis=0, keepdims=True), cnt_ref.shape)
    gathered = jnp.dot(_slot_matrix(pos_t, None, tm), hn.astype(BF16), preferred_element_type=F32)
    xd_ref[...] = gathered.reshape(N_EXPERTS, MOE_CAP, D_MODEL).astype(xd_ref.dtype)


def _expert_kernel(x_ref, wg_ref, wu_ref, wd_ref, o_ref):
    rows = x_ref.shape[0] * x_ref.shape[1]
    x = x_ref[...].reshape(rows, D_MODEL)
    gt = jnp.dot(x, wg_ref[...], preferred_element_type=F32)
    up = jnp.dot(x, wu_ref[...], preferred_element_type=F32)
    he = (gt * jax.nn.sigmoid(gt) * up).astype(BF16)
    ye = jnp.dot(he, wd_ref[...], preferred_element_type=F32)
    o_ref[...] = ye.reshape(o_ref.shape).astype(o_ref.dtype)


def _combine_kernel(x_ref, yd_ref, pos_ref, wt_ref, fg_ref, o_ref, *, final_norm):
    tm = x_ref.shape[0]
    weights = _slot_matrix(pos_ref[...], wt_ref[...], tm)
    yd = yd_ref[...].reshape(N_EXPERTS * MOE_CAP, D_MODEL)
    y = x_ref[...] + lax.dot_general(weights, yd, (((0,), (0,)), ((), ())), preferred_element_type=F32)
    if final_norm:
        y = _rms(y, fg_ref[...])
    o_ref[...] = y


def _moe_dispatch(x, g, w_router, b_router):
    T = x.shape[0]
    tm = MOE_TILE
    nt = T // tm
    tri = jnp.asarray(np.tril(np.ones((tm, tm), np.float32), -1), BF16)
    xd, pos_t, w_t, cnt = pl.pallas_call(
        _dispatch_kernel,
        grid=(nt,),
        in_specs=[pl.BlockSpec((tm, D_MODEL), lambda i: (i, 0)),
                  pl.BlockSpec((1, D_MODEL), lambda i: (0, 0)),
                  pl.BlockSpec((D_MODEL, 2 * LANES), lambda i: (0, 0)),
                  pl.BlockSpec((1, LANES), lambda i: (0, 0)),
                  pl.BlockSpec((tm, tm), lambda i: (0, 0))],
        out_specs=[pl.BlockSpec((N_EXPERTS, None, MOE_CAP, D_MODEL), lambda i: (0, i, 0, 0)),
                   pl.BlockSpec((LANES, tm), lambda i: (0, i)),
                   pl.BlockSpec((LANES, tm), lambda i: (0, i)),
                   pl.BlockSpec((None, 8, LANES), lambda i: (i, 0, 0))],
        out_shape=[jax.ShapeDtypeStruct((N_EXPERTS, nt, MOE_CAP, D_MODEL), BF16),
                   jax.ShapeDtypeStruct((LANES, T), F32),
                   jax.ShapeDtypeStruct((LANES, T), F32),
                   jax.ShapeDtypeStruct((nt, 8, LANES), F32)],
        compiler_params=_cparams("arbitrary"),
        name="moe_dispatch",
    )(x, g, w_router, b_router, tri)
    return xd, pos_t, w_t, jnp.max(cnt) > MOE_CAP


def _moe_finish(x, xd, pos_t, w_t, wg, wu, wd, final_g, final_norm):
    T = x.shape[0]
    tm = MOE_TILE
    nt = T // tm
    tps = min(MOE_TILES_PER_STEP, nt)
    assert nt % tps == 0
    yd = pl.pallas_call(
        _expert_kernel,
        grid=(N_EXPERTS, nt // tps),
        in_specs=[pl.BlockSpec((None, tps, MOE_CAP, D_MODEL), lambda e, c: (e, c, 0, 0)),
                  pl.BlockSpec((None, D_MODEL, EXPERT_FF), lambda e, c: (e, 0, 0)),
                  pl.BlockSpec((None, D_MODEL, EXPERT_FF), lambda e, c: (e, 0, 0)),
                  pl.BlockSpec((None, EXPERT_FF, D_MODEL), lambda e, c: (e, 0, 0))],
        out_specs=pl.BlockSpec((None, tps, MOE_CAP, D_MODEL), lambda e, c: (e, c, 0, 0)),
        out_shape=jax.ShapeDtypeStruct((N_EXPERTS, nt, MOE_CAP, D_MODEL), BF16),
        compiler_params=_cparams("arbitrary", "arbitrary"),
        name="moe_experts",
    )(xd, wg, wu, wd)
    return pl.pallas_call(
        functools.partial(_combine_kernel, final_norm=final_norm),
        grid=(nt,),
        in_specs=[pl.BlockSpec((tm, D_MODEL), lambda i: (i, 0)),
                  pl.BlockSpec((N_EXPERTS, None, MOE_CAP, D_MODEL), lambda i: (0, i, 0, 0)),
                  pl.BlockSpec((LANES, tm), lambda i: (0, i)),
                  pl.BlockSpec((LANES, tm), lambda i: (0, i)),
                  pl.BlockSpec((1, D_MODEL), lambda i: (0, 0))],
        out_specs=pl.BlockSpec((tm, D_MODEL), lambda i: (i, 0)),
        out_shape=jax.ShapeDtypeStruct((T, D_MODEL), F32),
        compiler_params=_cparams("arbitrary"),
        name="moe_combine",
    )(x, yd, pos_t, w_t, final_g)


def _block_diag(w):
    nb, bs, _ = w.shape
    out = jnp.zeros((nb * bs, nb * bs), w.dtype)
    for i in range(nb):
        out = out.at[i * bs:(i + 1) * bs, i * bs:(i + 1) * bs].set(w[i])
    return out


def _selection_aggregator_np(n_cmp, n_slc):
    ratio_s = NSA_SLC_LEN // NSA_CMP_STRIDE
    ratio_c = NSA_CMP_LEN // NSA_CMP_STRIDE
    w = np.convolve(np.ones(ratio_s), np.ones(ratio_c)).astype(np.float32)
    cj = np.arange(n_slc)[:, None] * ratio_s - (ratio_c - 1) + np.arange(w.size)[None, :]
    jj = np.broadcast_to(np.arange(n_slc)[:, None], cj.shape)
    ww = np.broadcast_to(w[None, :], cj.shape)
    keep = (cj >= 0) & (cj < n_cmp)
    agg = np.zeros((n_cmp, n_slc), np.float32)
    np.add.at(agg, (cj[keep], jj[keep]), ww[keep])
    return agg


def _row(v):
    return v.reshape(1, -1).astype(F32)


def _diff_lambda_init(layer):
    return 0.8 - 0.6 * math.exp(-0.3 * layer)


def _mixer_layer(x, l, B, S, tiles, p):
    (bias_swa, bias_diff, bias_slc, bias_win, bias_cmp, agg_pad, n_cmp, ncp) = tiles
    w_in = jnp.pad(p["w_in"][l], ((0, 0), (0, N_IN_PAD - p["w_in"].shape[2]))).astype(BF16)
    w_t = jnp.concatenate([w_in[:, 1536:1792], w_in[:, 2176:2304]], axis=1).T
    a_xg, b_qkv, c_qkv, d_all, vals_t = _in_proj(x, _row(p["mix_norm_g"][l]), w_in, w_t)

    y_a = _rglru(a_xg, p["conv_w"][l].astype(F32), _row(p["conv_b"][l]),
                 _block_diag(p["lru_w_a"][l]).astype(BF16), _row(p["lru_b_a"][l]),
                 _block_diag(p["lru_w_x"][l]).astype(BF16), _row(p["lru_b_x"][l]),
                 _row(p["lru_lambda"][l]), B, S)

    b3 = b_qkv.reshape(B, S, 512)
    swa_cfg = tuple(((h // 2) * 64, (h % 2) != (h // 2), h // 2) for h in range(SWA_HEADS))
    y_b = _banded_attention(b3, 0, b3, 2, b3, 3, bias_swa, _row(p["swa_sinks"][l]),
                            B=B, S=S, window=SWA_WINDOW, q_cfg=swa_cfg)

    c3 = c_qkv.reshape(B, S, 768)
    lam4 = jnp.stack([p["diff_lq1"][l], p["diff_lk1"][l], p["diff_lq2"][l], p["diff_lk2"][l]]).astype(F32)
    gsub = jnp.tile(p["diff_subln_g"][l].reshape(-1, 1).astype(F32), (2, 1))
    y_c = _flash_attention("diff", c3, 0, c3, 1, vals_t, 256, 0, bias_diff, (lam4, gsub), B=B, S=S,
                           lambda_init=_diff_lambda_init(l))

    d3 = d_all.reshape(B, S, 768)
    nchunk = S // NSA_CMP_STRIDE
    ck = d3[:, :, 256:320].reshape(B, nchunk, NSA_CMP_STRIDE * HEAD_DIM)
    cv = d3[:, :, 320:384].reshape(B, nchunk, NSA_CMP_STRIDE * HEAD_DIM)

    def pos8(a):
        return jnp.broadcast_to(a.reshape(1, -1), (8, a.size)).astype(BF16)

    w1k = jnp.pad(p["nsa_w1_k"][l], ((0, 0), (0, HEAD_DIM))).astype(BF16)
    w1v = jnp.pad(p["nsa_w1_v"][l], ((0, 0), (HEAD_DIM, 0))).astype(BF16)
    w2 = _block_diag(jnp.stack([p["nsa_w2_k"][l], p["nsa_w2_v"][l]])).astype(BF16)
    kcv = _nsa_compress(ck, cv, pos8(p["nsa_pos_k"][l]), pos8(p["nsa_pos_v"][l]), w1k, w1v, w2, ncp)
    o_c, sel = _nsa_cmp_select(d3, kcv, bias_cmp, agg_pad, B=B, S=S, n_cmp=n_cmp, ncp=ncp)
    o_s = _flash_attention("nsa", d3, 0, d3, 3, vals_t, 128, 2, bias_slc, (sel,), B=B, S=S)
    win_cfg = tuple((0, h % 2 == 1, 1) for h in range(NSA_HEADS))
    o_w = _banded_attention(d3, 0, d3, 4, None, None, bias_win, None, B=B, S=S, window=NSA_WINDOW, q_cfg=win_cfg)

    T = B * S
    return _merge(x, _row(p["mix_norm_g"][l]), y_a, y_b.reshape(T, 256), y_c.reshape(T, 256),
                  o_c.reshape(T, 256), o_s.reshape(T, 256), o_w.reshape(T, 256), d_all,
                  p["w_gate"][l].astype(BF16), p["b_gate"][l].astype(F32), p["w_branch"][l].astype(BF16),
                  p["w_out"][l].astype(BF16))


def _moe_layer(x, l, p, final_norm):
    w_router = jnp.concatenate([p["w_router_exp"][l], p["w_router_grp"][l]], axis=1)
    w_router = jnp.pad(w_router, ((0, 0), (0, LANES - w_router.shape[1]))).astype(F32)
    w_hi = w_router.astype(BF16)
    w_router = jnp.concatenate([w_hi, (w_router - w_hi.astype(F32)).astype(BF16)], axis=1)
    b_router = jnp.concatenate([p["b_router_exp"][l], p["b_router_grp"][l]])
    b_router = jnp.pad(b_router, (0, LANES - b_router.shape[0])).reshape(1, LANES).astype(F32)
    g = _row(p["ffn_norm_g"][l])
    fg = _row(p["final_norm_g"])
    wg, wu, wd = (p[k][l].astype(BF16) for k in ("w_exp_gate", "w_exp_up", "w_exp_down"))
    xd, pos_t, w_t, overflow = _moe_dispatch(x, g, w_router, b_router)
    return lax.cond(overflow,
                    lambda: _moe(x, g, w_router, b_router, wg, wu, wd, fg, final_norm),
                    lambda: _moe_finish(x, xd, pos_t, w_t, wg, wu, wd, fg, final_norm))


def _forward(p):
    x = p["x"]
    B, S, D = x.shape
    depth = p["w_in"].shape[0]
    assert D == D_MODEL and S % (16 * Q_BLOCK) == 0 and S // NSA_SLC_LEN <= LANES
    rel = p["rel_bias"].astype(F32)
    n_cmp = (S - NSA_CMP_LEN) // NSA_CMP_STRIDE + 1
    ncp = LANES + S // NSA_CMP_STRIDE
    agg = _selection_aggregator_np(n_cmp, S // NSA_SLC_LEN)
    agg_pad = np.zeros((ncp, LANES), np.float32)
    agg_pad[LANES:LANES + n_cmp, :agg.shape[1]] = agg
    tiles = (
        _bias_tiles(rel, _banded_buckets(1), 0, SWA_HEADS, False),
        _bias_tiles(rel, _flash_buckets(), SWA_HEADS, DIFF_HEADS, True, LOG2E),
        _bias_tiles(rel, _flash_buckets(), SWA_HEADS + DIFF_HEADS, NSA_HEADS, True, LOG2E),
        _bias_tiles(rel, _banded_buckets(NSA_WINDOW // Q_BLOCK), SWA_HEADS + DIFF_HEADS, NSA_HEADS, False),
        _bias_tiles(rel, _cmp_buckets(), SWA_HEADS + DIFF_HEADS, NSA_HEADS, True),
        jnp.asarray(agg_pad, BF16), n_cmp, ncp,
    )
    xt = x.reshape(B * S, D).astype(F32)
    for l in range(depth):
        xt = _mixer_layer(xt, l, B, S, tiles, p)
        xt = _moe_layer(xt, l, p, final_norm=(l == depth - 1))
    return xt.reshape(B, S, D).astype(x.dtype)


def kernel(x, rel_bias, final_norm_g, mix_norm_g, w_in, conv_w, conv_b, lru_w_a, lru_b_a, lru_w_x, lru_b_x,
           lru_lambda, swa_sinks, diff_lq1, diff_lk1, diff_lq2, diff_lk2, diff_subln_g, nsa_pos_k, nsa_w1_k,
           nsa_w2_k, nsa_pos_v, nsa_w1_v, nsa_w2_v, w_gate, b_gate, w_branch, w_out, ffn_norm_g, w_router_grp,
           b_router_grp, w_router_exp, b_router_exp, w_exp_gate, w_exp_up, w_exp_down):
    return _forward(dict(
        x=x, rel_bias=rel_bias, final_norm_g=final_norm_g, mix_norm_g=mix_norm_g, w_in=w_in, conv_w=conv_w,
        conv_b=conv_b, lru_w_a=lru_w_a, lru_b_a=lru_b_a, lru_w_x=lru_w_x, lru_b_x=lru_b_x, lru_lambda=lru_lambda,
        swa_sinks=swa_sinks, diff_lq1=diff_lq1, diff_lk1=diff_lk1, diff_lq2=diff_lq2, diff_lk2=diff_lk2,
        diff_subln_g=diff_subln_g, nsa_pos_k=nsa_pos_k, nsa_w1_k=nsa_w1_k, nsa_w2_k=nsa_w2_k, nsa_pos_v=nsa_pos_v,
        nsa_w1_v=nsa_w1_v, nsa_w2_v=nsa_w2_v, w_gate=w_gate, b_gate=b_gate, w_branch=w_branch, w_out=w_out,
        ffn_norm_g=ffn_norm_g, w_router_grp=w_router_grp, b_router_grp=b_router_grp, w_router_exp=w_router_exp,
        b_router_exp=b_router_exp, w_exp_gate=w_exp_gate, w_exp_up=w_exp_up, w_exp_down=w_exp_down))
```

```python
import functools
import math

import numpy as np
import jax
import jax.numpy as jnp
from jax import lax
from jax.experimental import pallas as pl
from jax.experimental.pallas import tpu as pltpu

F32 = jnp.float32
BF16 = jnp.bfloat16
I32 = jnp.int32

D_MODEL = 1024
HEAD_DIM = 64
MIX_WIDTH = 256
LRU_C = 8.0
CONV_WIDTH = 4
SWA_HEADS = 4
SWA_WINDOW = 128
DIFF_HEADS = 4
DIFF_DIM = 32
DIFF_SUBLN_EPS = 1e-5
NSA_HEADS = 4
NSA_CMP_LEN = 32
NSA_CMP_STRIDE = 16
NSA_SLC_LEN = 64
NSA_TOPK = 16
NSA_WINDOW = 512
NSA_FORCED_SCORE = 1e4
REL_BUCKETS = 32
REL_MAX_DIST = 128
N_GROUPS = 4
EXPERTS_PER_GROUP = 8
N_EXPERTS = 32
EXPERT_FF = 256
NORM_EPS = 1e-6
NEG_INF = -1e30
TINY = 1e-30
LOG2E = math.log2(math.e)

LANES = 128
Q_BLOCK = 128
BANDED_Q_BLOCKS = 2
FLASH_BLOCK = 256
MOE_TILE = 512
MOE_CAP = 128
MOE_TILES_PER_STEP = 8
FLASH_FAR_BLOCKS = 2
N_IN_PAD = 2560
VMEM_LIMIT = 56 * 1024 * 1024

_NT = (((1,), (1,)), ((), ()))


def _cparams(*sem):
    return pltpu.CompilerParams(dimension_semantics=sem, vmem_limit_bytes=VMEM_LIMIT)


def _t5_bucket_np(dist):
    n = np.maximum(dist, 0)
    exact = REL_BUCKETS // 2
    scaled = (np.log(np.maximum(n, 1).astype(np.float32) / np.float32(exact))
              / np.float32(math.log(REL_MAX_DIST / exact))).astype(np.float32)
    large = np.minimum(exact + (scaled * np.float32(REL_BUCKETS - exact)).astype(np.int32), REL_BUCKETS - 1)
    return np.where(n < exact, n, large).astype(np.int32)


def _bias_tile_kernel(tab_ref, idx_ref, o_ref, *, head0, shift, scale):
    h = pl.program_id(0) + head0
    idx = idx_ref[0]
    acc = jnp.zeros(idx.shape, F32)
    for b in range(REL_BUCKETS):
        acc = jnp.where(idx == b, tab_ref[b, h], acc)
    if shift:
        acc = acc - tab_ref[REL_BUCKETS - 1, h]
    if scale != 1.0:
        acc = acc * scale
    o_ref[0, 0] = acc


def _bias_tiles(rel_bias, buckets, head0, nheads, shift, scale=1.0):
    R, M, N = buckets.shape
    return pl.pallas_call(
        functools.partial(_bias_tile_kernel, head0=head0, shift=shift, scale=scale),
        grid=(nheads, R),
        in_specs=[pl.BlockSpec(memory_space=pltpu.SMEM),
                  pl.BlockSpec((1, M, N), lambda h, r: (r, 0, 0))],
        out_specs=pl.BlockSpec((1, 1, M, N), lambda h, r: (h, r, 0, 0)),
        out_shape=jax.ShapeDtypeStruct((nheads, R, M, N), F32),
        compiler_params=_cparams("arbitrary", "arbitrary"),
        name="bias_tiles",
    )(rel_bias, jnp.asarray(buckets))


def _banded_buckets(nback):
    i = np.arange(BANDED_Q_BLOCKS * Q_BLOCK)[:, None]
    j = np.arange((nback + BANDED_Q_BLOCKS) * Q_BLOCK)[None, :]
    return _t5_bucket_np(i - j + nback * Q_BLOCK)[None]


def _flash_buckets():
    k = np.arange(FLASH_BLOCK)[:, None]
    q = np.arange(FLASH_BLOCK)[None, :]
    return np.stack([_t5_bucket_np(q - k), _t5_bucket_np(q - k + FLASH_BLOCK)])


def _cmp_buckets():
    i = np.arange(Q_BLOCK)[None, :, None]
    c2 = np.arange(2 * LANES)[None, None, :]
    r = np.arange(16)[:, None, None]
    dist = i - NSA_CMP_STRIDE * (c2 - LANES - 8 * r) - (NSA_CMP_LEN - 1)
    return _t5_bucket_np(dist)


def _rms(x, g):
    return x * lax.rsqrt(jnp.mean(x * x, axis=-1, keepdims=True) + NORM_EPS) * g


def _proj_kernel(x_ref, g_ref, w_ref, wt_ref, oa_ref, ob_ref, oc_ref, od_ref, ot_ref):
    h = _rms(x_ref[...], g_ref[...]).astype(BF16)
    oa_ref[...] = jnp.dot(h, w_ref[:, 0:512], preferred_element_type=F32)
    ob_ref[...] = jnp.dot(h, w_ref[:, 512:1024], preferred_element_type=F32).astype(BF16)
    oc_ref[...] = jnp.dot(h, w_ref[:, 1024:1792], preferred_element_type=F32).astype(BF16)
    od_ref[...] = jnp.dot(h, w_ref[:, 1792:2560], preferred_element_type=F32).astype(BF16)
    ot_ref[...] = lax.dot_general(wt_ref[...], h, _NT, preferred_element_type=F32).astype(BF16)


def _in_proj(x, g, w, wt):
    T = x.shape[0]
    tm = 512
    nt = wt.shape[0]
    return pl.pallas_call(
        _proj_kernel,
        grid=(T // tm,),
        in_specs=[pl.BlockSpec((tm, D_MODEL), lambda i: (i, 0)),
                  pl.BlockSpec((1, D_MODEL), lambda i: (0, 0)),
                  pl.BlockSpec((D_MODEL, N_IN_PAD), lambda i: (0, 0)),
                  pl.BlockSpec((nt, D_MODEL), lambda i: (0, 0))],
        out_specs=[pl.BlockSpec((tm, 512), lambda i: (i, 0)),
                   pl.BlockSpec((tm, 512), lambda i: (i, 0)),
                   pl.BlockSpec((tm, 768), lambda i: (i, 0)),
                   pl.BlockSpec((tm, 768), lambda i: (i, 0)),
                   pl.BlockSpec((nt, tm), lambda i: (0, i))],
        out_shape=[jax.ShapeDtypeStruct((T, 512), F32),
                   jax.ShapeDtypeStruct((T, 512), BF16),
                   jax.ShapeDtypeStruct((T, 768), BF16),
                   jax.ShapeDtypeStruct((T, 768), BF16),
                   jax.ShapeDtypeStruct((nt, T), BF16)],
        compiler_params=_cparams("arbitrary"),
        name="in_proj",
    )(x, g, w, wt)


def _lru_kernel(xg_ref, cw_ref, cb_ref, wa_ref, ba_ref, wx_ref, bx_ref, lam_ref, o_ref,
                ext_ref, h_ref, *, Tt):
    t = pl.program_id(1)

    W = MIX_WIDTH

    @pl.when(t == 0)
    def _():
        ext_ref[0:8, :] = jnp.zeros((8, W), F32)
        h_ref[...] = jnp.zeros_like(h_ref)

    x = xg_ref[:, 0:W]
    gate = xg_ref[:, W:2 * W]
    ext_ref[8:Tt + 8, :] = x
    row = lax.broadcasted_iota(I32, (Tt, W), 0)
    xc = cb_ref[...] + x * cw_ref[CONV_WIDTH - 1:CONV_WIDTH, :]
    for s in range(1, CONV_WIDTH):
        xc = xc + ext_ref[8 - s:8 - s + Tt, :] * cw_ref[CONV_WIDTH - 1 - s:CONV_WIDTH - s, :]
    ext_ref[0:8, :] = ext_ref[Tt:Tt + 8, :]

    xcb = xc.astype(BF16)
    r = jax.nn.sigmoid(jnp.dot(xcb, wa_ref[...], preferred_element_type=F32) + ba_ref[...])
    ig = jax.nn.sigmoid(jnp.dot(xcb, wx_ref[...], preferred_element_type=F32) + bx_ref[...])
    z = -lam_ref[...]
    softplus = jnp.maximum(z, 0.0) + jnp.log1p(jnp.exp(-jnp.abs(z)))
    log_a = (-LRU_C) * r * softplus
    a = jnp.exp(log_a)
    mult = jnp.sqrt(1.0 - a * a)
    mult = jnp.where((row + t * Tt) == 0, 1.0, mult)
    b = mult * ig * xc

    d = 1
    while d < Tt:
        keep = row >= d
        b = jnp.where(keep, a * pltpu.roll(b, d, 0) + b, b)
        a = jnp.where(keep, a * pltpu.roll(a, d, 0), a)
        d *= 2
    hs = a * h_ref[7:8, :] + b
    h_ref[...] = hs[Tt - 8:Tt]
    o_ref[...] = (hs * jax.nn.gelu(gate)).astype(o_ref.dtype)


def _rglru(a_xg, conv_w, conv_b, wa_bd, b_a, wx_bd, b_x, lam, B, S):
    Tt = 256
    W = MIX_WIDTH
    vec = pl.BlockSpec((1, W), lambda b, t: (0, 0))
    mat = pl.BlockSpec((W, W), lambda b, t: (0, 0))
    out = pl.pallas_call(
        functools.partial(_lru_kernel, Tt=Tt),
        grid=(B, S // Tt),
        in_specs=[pl.BlockSpec((None, Tt, 2 * W), lambda b, t: (b, t, 0)),
                  pl.BlockSpec((CONV_WIDTH, W), lambda b, t: (0, 0)),
                  vec, mat, vec, mat, vec, vec],
        out_specs=pl.BlockSpec((None, Tt, W), lambda b, t: (b, t, 0)),
        out_shape=jax.ShapeDtypeStruct((B, S, W), BF16),
        scratch_shapes=[pltpu.VMEM((Tt + 8, W), F32), pltpu.VMEM((8, W), F32)],
        compiler_params=_cparams("arbitrary", "arbitrary"),
        name="rglru",
    )(a_xg.reshape(B, S, 2 * W), conv_w, conv_b, wa_bd, b_a, wx_bd, b_x, lam)
    return out.reshape(B * S, W)


def _lane_window(q128, lo, width, roll, scale):
    q = q128 * scale
    if roll:
        q = pltpu.roll(q, LANES // 2, 1)
    lane = lax.broadcasted_iota(I32, q.shape, 1)
    return jnp.where((lane >= lo) & (lane < lo + width), q, 0.0).astype(BF16)


def _pair_halves(even, odd, even_half, odd_half):
    if even_half == 1:
        even = pltpu.roll(even, LANES // 2, 1)
    if odd_half == 0:
        odd = pltpu.roll(odd, LANES // 2, 1)
    lane = lax.broadcasted_iota(I32, even.shape, 1)
    return jnp.where(lane < LANES // 2, even, odd)


def _banded_kernel(*refs, nback, window, q_cfg, has_sink, shared_kv):
    nk = nback + BANDED_Q_BLOCKS
    TQ = BANDED_Q_BLOCKS * Q_BLOCK
    q_ref = refs[0]
    k_refs = refs[1:1 + nk]
    pos = 1 + nk
    if shared_kv:
        v_refs = k_refs
    else:
        v_refs = refs[pos:pos + nk]
        pos += nk
    bias_ref = refs[pos]
    pos += 1
    sink_ref = None
    if has_sink:
        sink_ref = refs[pos]
        pos += 1
    o_ref = refs[pos]

    n = pl.program_id(1)
    KW = nk * Q_BLOCK
    kcat = jnp.concatenate([k_refs[nk - 1 - i][...] for i in range(nk)], axis=0)
    vcat = kcat if shared_kv else jnp.concatenate([v_refs[nk - 1 - i][...] for i in range(nk)], axis=0)
    ii = lax.broadcasted_iota(I32, (TQ, KW), 0)
    jj = lax.broadcasted_iota(I32, (TQ, KW), 1)
    dist = ii - jj + nback * Q_BLOCK
    valid = (dist >= 0) & (dist < window) & (jj >= (nback - BANDED_Q_BLOCKS * n) * Q_BLOCK)
    scale = HEAD_DIM ** -0.5

    def logits(h):
        lo, roll, _ = q_cfg[h]
        g = h // 2
        qx = _lane_window(q_ref[:, LANES * g:LANES * (g + 1)].astype(F32), lo, HEAD_DIM, roll, scale)
        s = lax.dot_general(qx, kcat, _NT, preferred_element_type=F32) + bias_ref[h, 0]
        return jnp.where(valid, s, NEG_INF)

    def softmax(h, s):
        m = jnp.max(s, axis=1, keepdims=True)
        if has_sink:
            sink = sink_ref[0:1, h:h + 1]
            m = jnp.maximum(m, sink)
        p = jnp.exp(s - m)
        den = jnp.sum(p, axis=1, keepdims=True)
        if has_sink:
            den = den + jnp.exp(sink - m)
        return p.astype(BF16), 1.0 / jnp.maximum(den, TINY)

    nh = len(q_cfg)
    ss = [logits(h) for h in range(nh)]
    pr = [softmax(h, ss[h]) for h in range(nh)]
    outs = [(jnp.dot(pr[h][0], vcat, preferred_element_type=F32) * pr[h][1], q_cfg[h][2]) for h in range(nh)]
    for g in range(len(q_cfg) // 2):
        (oe, he), (oo, ho) = outs[2 * g], outs[2 * g + 1]
        o_ref[:, LANES * g:LANES * (g + 1)] = _pair_halves(oe, oo, he, ho).astype(o_ref.dtype)


def _banded_attention(qsrc, q_col, ksrc, k_col, vsrc, v_col, bias, sinks, *, B, S, window, q_cfg):
    nback = -(-window // Q_BLOCK)
    nq = BANDED_Q_BLOCKS
    nk = nback + nq
    TQ = nq * Q_BLOCK
    shared_kv = vsrc is None
    has_sink = sinks is not None

    def kv_spec(col, i):
        return pl.BlockSpec((None, Q_BLOCK, LANES), lambda b, n: (b, jnp.maximum(nq * n + nq - 1 - i, 0), col))

    in_specs = [pl.BlockSpec((None, TQ, 2 * LANES), lambda b, n: (b, n, q_col))]
    args = [qsrc]
    in_specs += [kv_spec(k_col, i) for i in range(nk)]
    args += [ksrc] * nk
    if not shared_kv:
        in_specs += [kv_spec(v_col, i) for i in range(nk)]
        args += [vsrc] * nk
    in_specs.append(pl.BlockSpec(bias.shape, lambda b, n: (0, 0, 0, 0)))
    args.append(bias)
    if has_sink:
        in_specs.append(pl.BlockSpec(sinks.shape, lambda b, n: (0, 0)))
        args.append(sinks)
    return pl.pallas_call(
        functools.partial(_banded_kernel, nback=nback, window=window, q_cfg=q_cfg,
                          has_sink=has_sink, shared_kv=shared_kv),
        grid=(B, S // TQ),
        in_specs=in_specs,
        out_specs=pl.BlockSpec((None, TQ, 2 * LANES), lambda b, n: (b, n, 0)),
        out_shape=jax.ShapeDtypeStruct((B, S, 2 * LANES), BF16),
        compiler_params=_cparams("arbitrary", "arbitrary"),
        name="banded_attn",
    )(*args)


def _flash_kernel(*refs, mode, lambda_init):
    TB = FLASH_BLOCK
    if mode == "diff":
        (q_ref, k_ref, vt_ref, bias_ref, lam_ref, g_ref, o_ref,
         qx_ref, m_ref, l_ref, acc_ref) = refs
        sel_ref = None
        maps = [dict(g=h // 2, lo=(h % 2) * 64 + 32 * mm, w=DIFF_DIM, roll=False, kg=h // 2, vrow=HEAD_DIM * h, h=h)
                for h in range(DIFF_HEADS) for mm in range(2)]
        scale = DIFF_DIM ** -0.5
    else:
        (q_ref, k_ref, vt_ref, sel_ref, bias_ref, o_ref,
         qx_ref, m_ref, l_ref, acc_ref) = refs
        maps = [dict(g=h // 2, lo=0, w=HEAD_DIM, roll=(h % 2 == 1), kg=0, vrow=HEAD_DIM, h=h)
                for h in range(NSA_HEADS)]
        scale = HEAD_DIM ** -0.5
    n = pl.program_id(1)

    for mp, c in enumerate(maps):
        qx_ref[mp] = _lane_window(q_ref[:, LANES * c["g"]:LANES * (c["g"] + 1)].astype(F32),
                                  c["lo"], c["w"], c["roll"], scale * LOG2E)
    m_ref[...] = jnp.full(m_ref.shape, NEG_INF, F32)
    l_ref[...] = jnp.zeros_like(l_ref)
    acc_ref[...] = jnp.zeros_like(acc_ref)

    key = lax.broadcasted_iota(I32, (TB, TB), 0)
    qry = lax.broadcasted_iota(I32, (TB, TB), 1)

    def step(j, kind, TK=TB):
        koff = pl.multiple_of(j * TB, TB)
        valid = None
        if kind == "diag":
            valid = qry >= key
        if sel_ref is not None:
            per = TK // NSA_SLC_LEN
            base = j * (TB // NSA_SLC_LEN)
            rows = [jnp.broadcast_to(sel_ref[pl.ds(base + i, 1), :], (NSA_SLC_LEN, TB)) for i in range(per)]
            picked = jnp.concatenate(rows, axis=0) > 0.5
            valid = picked if valid is None else (valid & picked)

        def logits(mp):
            c = maps[mp]
            kt = k_ref[pl.ds(koff, TK), LANES * c["kg"]:LANES * (c["kg"] + 1)]
            s = lax.dot_general(kt, qx_ref[mp], _NT, preferred_element_type=F32)
            if kind == "near":
                s = s + bias_ref[c["h"], 1]
            elif kind == "diag":
                s = s + bias_ref[c["h"], 0]
            if valid is not None:
                s = jnp.where(valid, s, NEG_INF)
            return s

        def softmax(mp, s):
            m_prev = m_ref[mp]
            m_new = jnp.maximum(m_prev, jnp.max(s, axis=0, keepdims=True))
            alpha = jnp.exp2(m_prev - m_new)
            p = jnp.exp2(s - m_new)
            l_ref[mp] = alpha * l_ref[mp] + jnp.sum(p, axis=0, keepdims=True)
            m_ref[mp] = m_new
            return p.astype(BF16), alpha

        def values(mp, p, alpha):
            c = maps[mp]
            vt = vt_ref[c["vrow"]:c["vrow"] + HEAD_DIM, pl.ds(koff, TK)]
            acc_ref[mp] = alpha * acc_ref[mp] + jnp.dot(vt, p, preferred_element_type=F32)

        ss = [logits(mp) for mp in range(len(maps))]
        pa = [softmax(mp, ss[mp]) for mp in range(len(maps))]
        for mp in range(len(maps)):
            values(mp, *pa[mp])

    nfar = jnp.maximum(n - 1, 0)
    FW = FLASH_FAR_BLOCKS

    def far_body(j, carry):
        step(FW * j, "far", FW * TB)
        return carry

    lax.fori_loop(0, nfar // FW, far_body, 0)

    def rest_body(j, carry):
        step(j, "far")
        return carry

    lax.fori_loop((nfar // FW) * FW, nfar, rest_body, 0)

    @pl.when(n >= 1)
    def _():
        step(n - 1, "near")

    step(n, "diag")

    def normalised(mp):
        return acc_ref[mp] * (1.0 / jnp.maximum(l_ref[mp], TINY))

    if mode == "diff":
        lq = lam_ref[...]
        lam = (jnp.exp(jnp.sum(lq[0:1] * lq[1:2], axis=1, keepdims=True))
               - jnp.exp(jnp.sum(lq[2:3] * lq[3:4], axis=1, keepdims=True)) + lambda_init)
        for g in range(DIFF_HEADS // 2):
            res = []
            for h in (2 * g, 2 * g + 1):
                o = normalised(2 * h) - lam * normalised(2 * h + 1)
                ms = jnp.sum(o * o, axis=0, keepdims=True) * (1.0 / (2 * DIFF_DIM))
                res.append(o * lax.rsqrt(ms + DIFF_SUBLN_EPS))
            out = jnp.concatenate(res, axis=0) * g_ref[...] * (1.0 - lambda_init)
            o_ref[:, LANES * g:LANES * (g + 1)] = out.T.astype(o_ref.dtype)
    else:
        for g in range(NSA_HEADS // 2):
            out = jnp.concatenate([normalised(2 * g), normalised(2 * g + 1)], axis=0)
            o_ref[:, LANES * g:LANES * (g + 1)] = out.T.astype(o_ref.dtype)


def _flash_attention(mode, qsrc, q_col, ksrc, k_col, vt, vt_rows, vt_blk, bias, extra, *, B, S, lambda_init=0.0):
    TB = FLASH_BLOCK
    q_spec = pl.BlockSpec((None, TB, 2 * LANES), lambda b, n: (b, n, q_col))
    bias_spec = pl.BlockSpec(bias.shape, lambda b, n: (0, 0, 0, 0))
    vt_spec = pl.BlockSpec((vt_rows, S), lambda b, n: (vt_blk, b))
    if mode == "diff":
        lam4, gsub = extra
        nmaps = 2 * DIFF_HEADS
        in_specs = [q_spec,
                    pl.BlockSpec((None, S, 2 * LANES), lambda b, n: (b, 0, k_col)),
                    vt_spec, bias_spec,
                    pl.BlockSpec(lam4.shape, lambda b, n: (0, 0)),
                    pl.BlockSpec(gsub.shape, lambda b, n: (0, 0))]
        args = [qsrc, ksrc, vt, bias, lam4, gsub]
    else:
        (sel,) = extra
        nmaps = NSA_HEADS
        in_specs = [q_spec,
                    pl.BlockSpec((None, S, LANES), lambda b, n: (b, 0, k_col)),
                    vt_spec,
                    pl.BlockSpec((None, LANES, TB), lambda b, n: (b, 0, n)),
                    bias_spec]
        args = [qsrc, ksrc, vt, sel, bias]
    return pl.pallas_call(
        functools.partial(_flash_kernel, mode=mode, lambda_init=lambda_init),
        grid=(B, S // TB),
        in_specs=in_specs,
        out_specs=pl.BlockSpec((None, TB, 2 * LANES), lambda b, n: (b, n, 0)),
        out_shape=jax.ShapeDtypeStruct((B, S, 2 * LANES), BF16),
        scratch_shapes=[pltpu.VMEM((nmaps, TB, LANES), BF16),
                        pltpu.VMEM((nmaps, 1, TB), F32),
                        pltpu.VMEM((nmaps, 1, TB), F32),
                        pltpu.VMEM((nmaps, HEAD_DIM, TB), F32)],
        compiler_params=_cparams("arbitrary", "arbitrary"),
        name="flash_" + mode,
    )(*args)


def _compress_kernel(ck_ref, cv_ref, pk_ref, pv_ref, w1k_ref, w1v_ref, w2_ref, o_ref):
    half = NSA_CMP_STRIDE * HEAD_DIM
    ck = ck_ref[...]
    cv = cv_ref[...]
    nrow = ck.shape[0]

    def mm(a, w):
        return jnp.dot(a, w, preferred_element_type=F32)

    top = mm(ck, w1k_ref[0:half, :]) + mm(cv, w1v_ref[0:half, :])
    bot = mm(ck, w1k_ref[half:2 * half, :]) + mm(cv, w1v_ref[half:2 * half, :])
    posw = (mm(pk_ref[...], w1k_ref[...]) + mm(pv_ref[...], w1v_ref[...]))[0:1]
    pre = top + pltpu.roll(bot, nrow - 1, 0) + posw
    kcv = mm(jax.nn.gelu(pre).astype(BF16), w2_ref[...])
    o_ref[0:LANES, :] = jnp.zeros((LANES, LANES), o_ref.dtype)
    o_ref[LANES:LANES + nrow, :] = kcv.astype(o_ref.dtype)


def _nsa_compress(ck, cv, pk, pv, w1k, w1v, w2, ncp):
    B, nchunk, width = ck.shape
    assert ncp == LANES + nchunk

    def full(a):
        return pl.BlockSpec(a.shape, lambda b: (0,) * a.ndim)

    chunk = pl.BlockSpec((None, nchunk, width), lambda b: (b, 0, 0))
    return pl.pallas_call(
        _compress_kernel,
        grid=(B,),
        in_specs=[chunk, chunk, full(pk), full(pv), full(w1k), full(w1v), full(w2)],
        out_specs=pl.BlockSpec((None, ncp, LANES), lambda b: (b, 0, 0)),
        out_shape=jax.ShapeDtypeStruct((B, ncp, LANES), BF16),
        compiler_params=_cparams("arbitrary"),
        name="nsa_compress",
    )(ck, cv, pk, pv, w1k, w1v, w2)


def _cmp_kernel(q_ref, kcv_ref, bias_ref, agg_ref, oc_ref, sel_ref, s_scr, *, n_cmp, ncp):
    n = pl.program_id(1)
    QB = Q_BLOCK
    kcv = kcv_ref[...]
    cp = lax.broadcasted_iota(I32, (QB, ncp), 1)
    qpos = n * QB + lax.broadcasted_iota(I32, (QB, ncp), 0)
    valid = ((cp >= LANES) & (cp < LANES + n_cmp)
             & (qpos - NSA_CMP_STRIDE * (cp - LANES) - (NSA_CMP_LEN - 1) >= 0))
    woff = pl.multiple_of((n // 16) * LANES, LANES)
    scale = HEAD_DIM ** -0.5

    def logits(h):
        g = h // 2
        qx = _lane_window(q_ref[:, LANES * g:LANES * (g + 1)].astype(F32), 0, HEAD_DIM, h % 2 == 1, scale)
        s_scr[h] = lax.dot_general(qx, kcv, _NT, preferred_element_type=F32)
        s_scr[h, :, pl.ds(woff, 2 * LANES)] += bias_ref[h]

    def softmax(h):
        s = jnp.where(valid, s_scr[h], NEG_INF)
        m = jnp.max(s, axis=1, keepdims=True)
        p = jnp.where(valid, jnp.exp(s - m), 0.0)
        den = jnp.sum(p, axis=1, keepdims=True)
        return p * (1.0 / jnp.maximum(den, TINY))

    for h in range(NSA_HEADS):
        logits(h)
    ps = [softmax(h) for h in range(NSA_HEADS)]
    outs = [jnp.dot(p.astype(BF16), kcv, preferred_element_type=F32) for p in ps]
    for g in range(NSA_HEADS // 2):
        oc_ref[:, LANES * g:LANES * (g + 1)] = _pair_halves(outs[2 * g], outs[2 * g + 1], 1, 1).astype(oc_ref.dtype)

    psum = (ps[0] + ps[1]) + (ps[2] + ps[3])
    imp = jnp.dot(psum.astype(BF16), agg_ref[...], preferred_element_type=F32)

    v = imp.T
    jb = lax.broadcasted_iota(I32, v.shape, 0)
    qp = n * QB + lax.broadcasted_iota(I32, v.shape, 1)
    cur = qp >> 6
    forced = (jb == 0) | (jb == cur) | (jb == cur - 1)
    v = jnp.where(forced, -jnp.inf, jnp.where(jb * NSA_SLC_LEN <= qp, v, -1.0))
    rid = jb.astype(F32)
    picked = jnp.where(forced, 1.0, 0.0)
    for _ in range(NSA_TOPK - 3):
        mx = jnp.max(v, axis=0, keepdims=True)
        first = jnp.min(jnp.where(v == mx, rid, float(LANES)), axis=0, keepdims=True)
        hit = rid == first
        picked = jnp.where(hit, 1.0, picked)
        v = jnp.where(hit, -jnp.inf, v)
    sel_ref[...] = picked


def _nsa_cmp_select(dsrc, kcv, bias, agg, *, B, S, n_cmp, ncp):
    return pl.pallas_call(
        functools.partial(_cmp_kernel, n_cmp=n_cmp, ncp=ncp),
        grid=(B, S // Q_BLOCK),
        in_specs=[pl.BlockSpec((None, Q_BLOCK, 2 * LANES), lambda b, n: (b, n, 0)),
                  pl.BlockSpec((None, ncp, LANES), lambda b, n: (b, 0, 0)),
                  pl.BlockSpec((NSA_HEADS, None, Q_BLOCK, 2 * LANES), lambda b, n: (0, n % 16, 0, 0)),
                  pl.BlockSpec((ncp, LANES), lambda b, n: (0, 0))],
        out_specs=[pl.BlockSpec((None, Q_BLOCK, 2 * LANES), lambda b, n: (b, n, 0)),
                   pl.BlockSpec((None, LANES, Q_BLOCK), lambda b, n: (b, 0, n))],
        out_shape=[jax.ShapeDtypeStruct((B, S, 2 * LANES), BF16),
                   jax.ShapeDtypeStruct((B, LANES, S), F32)],
        scratch_shapes=[pltpu.VMEM((NSA_HEADS, Q_BLOCK, ncp), F32)],
        compiler_params=_cparams("arbitrary", "arbitrary"),
        name="nsa_cmp_select",
    )(dsrc, kcv, bias, agg)


def _merge_kernel(x_ref, g_ref, ya_ref, yb_ref, yc_ref, oc_ref, os_ref, ow_ref, dg_ref,
                  wg_ref, bg_ref, wb_ref, wo_ref, o_ref):
    x = x_ref[...]
    h = _rms(x, g_ref[...]).astype(BF16)
    tm = x.shape[0]
    lane = lax.broadcasted_iota(I32, (tm, LANES), 1)
    gd = jax.nn.sigmoid(dg_ref[...].astype(F32))
    yd = []
    for g in range(NSA_HEADS // 2):
        acc = jnp.zeros((tm, LANES), F32)
        for br, src in enumerate((oc_ref, os_ref, ow_ref)):
            ge = gd[:, 3 * (2 * g) + br:3 * (2 * g) + br + 1]
            go = gd[:, 3 * (2 * g + 1) + br:3 * (2 * g + 1) + br + 1]
            acc = acc + jnp.where(lane < 64, ge, go) * src[:, LANES * g:LANES * (g + 1)].astype(F32)
        yd.append(acc)
    y_d = jnp.concatenate(yd, axis=1).astype(BF16)
    ys = (ya_ref[...], yb_ref[...], yc_ref[...], y_d)
    merged = jnp.zeros((tm, D_MODEL), F32)
    for br in range(4):
        gate = jax.nn.sigmoid(jnp.dot(h, wg_ref[br], preferred_element_type=F32) + bg_ref[br:br + 1, :])
        merged = merged + gate * jnp.dot(ys[br], wb_ref[br], preferred_element_type=F32)
    o_ref[...] = x + jnp.dot(merged.astype(BF16), wo_ref[...], preferred_element_type=F32)


def _merge(x, g, ya, yb, yc, oc, osel, ow, dproj, wg, bg, wb, wo):
    T = x.shape[0]
    tm = 256

    def tile(width, col=0):
        return pl.BlockSpec((tm, width), lambda i: (i, col))

    def full(a):
        return pl.BlockSpec(a.shape, lambda i: (0,) * a.ndim)

    return pl.pallas_call(
        _merge_kernel,
        grid=(T // tm,),
        in_specs=[tile(D_MODEL), full(g), tile(256), tile(256), tile(256), tile(256), tile(256), tile(256),
                  tile(LANES, 5), full(wg), full(bg), full(wb), full(wo)],
        out_specs=tile(D_MODEL),
        out_shape=jax.ShapeDtypeStruct((T, D_MODEL), F32),
        compiler_params=_cparams("arbitrary"),
        name="merge",
    )(x, g, ya, yb, yc, oc, osel, ow, dproj, wg, bg, wb, wo)


def _route(hn, w_split, b_router):
    lane = lax.broadcasted_iota(I32, (hn.shape[0], LANES), 1)
    hi = hn.astype(BF16)
    lo = (hn - hi.astype(F32)).astype(BF16)
    a = jnp.dot(hi, w_split, preferred_element_type=F32)
    b = jnp.dot(lo, w_split[:, 0:LANES], preferred_element_type=F32)
    logits = (a[:, 0:LANES] + a[:, LANES:2 * LANES]) + b + b_router
    is_grp = (lane >= N_EXPERTS) & (lane < N_EXPERTS + N_GROUPS)
    lanef = lane.astype(F32)
    lg = jnp.where(is_grp, logits, -jnp.inf)
    mg = jnp.max(lg, axis=1, keepdims=True)
    gsel = jnp.min(jnp.where(lg == mg, lanef, 2.0 * LANES), axis=1, keepdims=True) - N_EXPERTS
    g_w = 1.0 / jnp.sum(jnp.where(is_grp, jnp.exp(lg - mg), 0.0), axis=1, keepdims=True)
    in_grp = (lane < N_EXPERTS) & ((lane >> 3).astype(F32) == gsel)
    v1 = jnp.where(in_grp, logits, -jnp.inf)
    m1 = jnp.max(v1, axis=1, keepdims=True)
    i1 = jnp.min(jnp.where(v1 == m1, lanef, 2.0 * LANES), axis=1, keepdims=True)
    v2 = jnp.where(lanef == i1, -jnp.inf, v1)
    m2 = jnp.max(v2, axis=1, keepdims=True)
    i2 = jnp.min(jnp.where(v2 == m2, lanef, 2.0 * LANES), axis=1, keepdims=True)
    e2 = jnp.exp(m2 - m1)
    w1 = g_w / (1.0 + e2)
    w2 = g_w * e2 / (1.0 + e2)
    first, second = lanef == i1, lanef == i2
    comb = jnp.where(first, w1, 0.0) + jnp.where(second, w2, 0.0)
    return comb, jnp.where(first | second, 1.0, 0.0)


def _moe_kernel(x_ref, g_ref, wr_ref, br_ref, wg_ref, wu_ref, wd_ref, fg_ref, o_ref,
                hn_ref, comb_ref, acc_ref, *, final_norm):
    e = pl.program_id(1)
    tm = x_ref.shape[0]
    lane = lax.broadcasted_iota(I32, (tm, LANES), 1)

    @pl.when(e == 0)
    def _():
        hn = _rms(x_ref[...], g_ref[...])
        hn_ref[...] = hn.astype(BF16)
        comb_ref[...] = _route(hn, wr_ref[...], br_ref[...])[0]
        acc_ref[...] = jnp.zeros_like(acc_ref)

    hn = hn_ref[...]
    gt = jnp.dot(hn, wg_ref[...], preferred_element_type=F32)
    up = jnp.dot(hn, wu_ref[...], preferred_element_type=F32)
    he = (gt * jax.nn.sigmoid(gt) * up).astype(BF16)
    ye = jnp.dot(he, wd_ref[...], preferred_element_type=F32)
    cw = jnp.sum(jnp.where(lane == e, comb_ref[...], 0.0), axis=1, keepdims=True)
    acc_ref[...] += cw * ye

    @pl.when(e == N_EXPERTS - 1)
    def _():
        y = x_ref[...] + acc_ref[...]
        if final_norm:
            y = _rms(y, fg_ref[...])
        o_ref[...] = y


def _moe(x, g, w_router, b_router, wg, wu, wd, final_g, final_norm):
    T = x.shape[0]
    tm = 1024
    return pl.pallas_call(
        functools.partial(_moe_kernel, final_norm=final_norm),
        grid=(T // tm, N_EXPERTS),
        in_specs=[pl.BlockSpec((tm, D_MODEL), lambda i, e: (i, 0)),
                  pl.BlockSpec((1, D_MODEL), lambda i, e: (0, 0)),
                  pl.BlockSpec((D_MODEL, 2 * LANES), lambda i, e: (0, 0)),
                  pl.BlockSpec((1, LANES), lambda i, e: (0, 0)),
                  pl.BlockSpec((None, D_MODEL, EXPERT_FF), lambda i, e: (e, 0, 0)),
                  pl.BlockSpec((None, D_MODEL, EXPERT_FF), lambda i, e: (e, 0, 0)),
                  pl.BlockSpec((None, EXPERT_FF, D_MODEL), lambda i, e: (e, 0, 0)),
                  pl.BlockSpec((1, D_MODEL), lambda i, e: (0, 0))],
        out_specs=pl.BlockSpec((tm, D_MODEL), lambda i, e: (i, 0)),
        out_shape=jax.ShapeDtypeStruct((T, D_MODEL), F32),
        scratch_shapes=[pltpu.VMEM((tm, D_MODEL), BF16),
                        pltpu.VMEM((tm, LANES), F32),
                        pltpu.VMEM((tm, D_MODEL), F32)],
        compiler_params=_cparams("arbitrary", "arbitrary"),
        name="moe",
    )(x, g, w_router, b_router, wg, wu, wd, final_g)


def _slot_matrix(pos_t, val_t, ntok, half):
    nslot = MOE_CAP // 2
    slot = (lax.broadcasted_iota(I32, (nslot, ntok), 0) + half * nslot).astype(F32)
    blocks = []
    for e in range(N_EXPERTS):
        hit = pos_t[e:e + 1, :] == slot
        blocks.append(jnp.where(hit, 1.0 if val_t is None else val_t[e:e + 1, :], 0.0).astype(BF16))
    return jnp.concatenate(blocks, axis=0)


def _dispatch_kernel(x_ref, g_ref, wr_ref, br_ref, tri_ref, xd_ref, pos_ref, wt_ref, cnt_ref):
    tm = x_ref.shape[0]
    hn = _rms(x_ref[...], g_ref[...])
    comb, routed = _route(hn, wr_ref[...], br_ref[...])
    rank = jnp.dot(tri_ref[...], routed.astype(BF16), preferred_element_type=F32)
    pos = jnp.where(routed > 0.0, rank, -1.0)
    pos_t = pos.T
    pos_ref[...] = pos_t
    wt_ref[...] = comb.T
    counts = jnp.sum(routed, axis=0, keepdims=True)
    cnt_ref[...] = jnp.broadcast_to(counts, cnt_ref.shape)
    hnb = hn.astype(BF16)
    nslot = MOE_CAP // 2
    lower = jnp.dot(_slot_matrix(pos_t, None, tm, 0), hnb, preferred_element_type=F32)
    xd_ref[:, 0:nslot, :] = lower.reshape(N_EXPERTS, nslot, D_MODEL).astype(xd_ref.dtype)

    busy = jnp.max(counts) > nslot

    @pl.when(busy)
    def _():
        upper = jnp.dot(_slot_matrix(pos_t, None, tm, 1), hnb, preferred_element_type=F32)
        xd_ref[:, nslot:MOE_CAP, :] = upper.reshape(N_EXPERTS, nslot, D_MODEL).astype(xd_ref.dtype)

    @pl.when(jnp.logical_not(busy))
    def _():
        xd_ref[:, nslot:MOE_CAP, :] = jnp.zeros((N_EXPERTS, nslot, D_MODEL), xd_ref.dtype)


def _expert_kernel(busy_ref, x_ref, wg_ref, wu_ref, wd_ref, o_ref):
    ntile, _, nslot, _ = x_ref.shape

    def ffn(half):
        x = x_ref[:, half].reshape(ntile * nslot, D_MODEL)
        gt = jnp.dot(x, wg_ref[...], preferred_element_type=F32)
        up = jnp.dot(x, wu_ref[...], preferred_element_type=F32)
        he = (gt * jax.nn.sigmoid(gt) * up).astype(BF16)
        ye = jnp.dot(he, wd_ref[...], preferred_element_type=F32)
        o_ref[:, half] = ye.reshape(ntile, nslot, D_MODEL).astype(o_ref.dtype)

    ffn(0)
    busy = busy_ref[pl.program_id(0), pl.program_id(1)] > 0

    @pl.when(busy)
    def _():
        ffn(1)

    @pl.when(jnp.logical_not(busy))
    def _():
        o_ref[:, 1] = jnp.zeros((ntile, nslot, D_MODEL), o_ref.dtype)


def _combine_kernel(busy_ref, x_ref, yd_ref, pos_ref, wt_ref, fg_ref, o_ref, *, final_norm):
    tm = x_ref.shape[0]
    nslot = MOE_CAP // 2

    def gathered(half):
        weights = _slot_matrix(pos_ref[...], wt_ref[...], tm, half)
        yd = yd_ref[:, half * nslot:(half + 1) * nslot, :].reshape(N_EXPERTS * nslot, D_MODEL)
        return lax.dot_general(weights, yd, (((0,), (0,)), ((), ())), preferred_element_type=F32)

    o_ref[...] = x_ref[...] + gathered(0)

    @pl.when(busy_ref[pl.program_id(0)] > 0)
    def _():
        o_ref[...] += gathered(1)

    if final_norm:
        o_ref[...] = _rms(o_ref[...], fg_ref[...])


def _moe_dispatch(x, g, w_router, b_router):
    T = x.shape[0]
    tm = MOE_TILE
    nt = T // tm
    tri = jnp.asarray(np.tril(np.ones((tm, tm), np.float32), -1), BF16)
    xd, pos_t, w_t, cnt = pl.pallas_call(
        _dispatch_kernel,
        grid=(nt,),
        in_specs=[pl.BlockSpec((tm, D_MODEL), lambda i: (i, 0)),
                  pl.BlockSpec((1, D_MODEL), lambda i: (0, 0)),
                  pl.BlockSpec((D_MODEL, 2 * LANES), lambda i: (0, 0)),
                  pl.BlockSpec((1, LANES), lambda i: (0, 0)),
                  pl.BlockSpec((tm, tm), lambda i: (0, 0))],
        out_specs=[pl.BlockSpec((N_EXPERTS, None, MOE_CAP, D_MODEL), lambda i: (0, i, 0, 0)),
                   pl.BlockSpec((LANES, tm), lambda i: (0, i)),
                   pl.BlockSpec((LANES, tm), lambda i: (0, i)),
                   pl.BlockSpec((None, 8, LANES), lambda i: (i, 0, 0))],
        out_shape=[jax.ShapeDtypeStruct((N_EXPERTS, nt, MOE_CAP, D_MODEL), BF16),
                   jax.ShapeDtypeStruct((LANES, T), F32),
                   jax.ShapeDtypeStruct((LANES, T), F32),
                   jax.ShapeDtypeStruct((nt, 8, LANES), F32)],
        compiler_params=_cparams("arbitrary"),
        name="moe_dispatch",
    )(x, g, w_router, b_router, tri)
    return xd, pos_t, w_t, cnt[:, 0, 0:N_EXPERTS]


def _moe_finish(x, xd, pos_t, w_t, counts, wg, wu, wd, final_g, final_norm):
    T = x.shape[0]
    tm = MOE_TILE
    nt = T // tm
    tps = min(MOE_TILES_PER_STEP, nt)
    assert nt % tps == 0
    nslot = MOE_CAP // 2
    hot = counts > nslot
    busy_blocks = jnp.any(hot.reshape(nt // tps, tps, N_EXPERTS), axis=1).T.astype(I32)
    busy_tiles = jnp.any(hot, axis=1).astype(I32)
    halves = (N_EXPERTS, nt, 2, nslot, D_MODEL)
    yd = pl.pallas_call(
        _expert_kernel,
        grid_spec=pltpu.PrefetchScalarGridSpec(
            num_scalar_prefetch=1,
            grid=(N_EXPERTS, nt // tps),
            in_specs=[pl.BlockSpec((None, tps, 2, nslot, D_MODEL), lambda e, c, b: (e, c, 0, 0, 0)),
                      pl.BlockSpec((None, D_MODEL, EXPERT_FF), lambda e, c, b: (e, 0, 0)),
                      pl.BlockSpec((None, D_MODEL, EXPERT_FF), lambda e, c, b: (e, 0, 0)),
                      pl.BlockSpec((None, EXPERT_FF, D_MODEL), lambda e, c, b: (e, 0, 0))],
            out_specs=pl.BlockSpec((None, tps, 2, nslot, D_MODEL), lambda e, c, b: (e, c, 0, 0, 0))),
        out_shape=jax.ShapeDtypeStruct(halves, BF16),
        compiler_params=_cparams("arbitrary", "arbitrary"),
        name="moe_experts",
    )(busy_blocks, xd.reshape(halves), wg, wu, wd)
    return pl.pallas_call(
        functools.partial(_combine_kernel, final_norm=final_norm),
        grid_spec=pltpu.PrefetchScalarGridSpec(
            num_scalar_prefetch=1,
            grid=(nt,),
            in_specs=[pl.BlockSpec((tm, D_MODEL), lambda i, b: (i, 0)),
                      pl.BlockSpec((N_EXPERTS, None, MOE_CAP, D_MODEL), lambda i, b: (0, i, 0, 0)),
                      pl.BlockSpec((LANES, tm), lambda i, b: (0, i)),
                      pl.BlockSpec((LANES, tm), lambda i, b: (0, i)),
                      pl.BlockSpec((1, D_MODEL), lambda i, b: (0, 0))],
            out_specs=pl.BlockSpec((tm, D_MODEL), lambda i, b: (i, 0))),
        out_shape=jax.ShapeDtypeStruct((T, D_MODEL), F32),
        compiler_params=_cparams("arbitrary"),
        name="moe_combine",
    )(busy_tiles, x, yd.reshape(N_EXPERTS, nt, MOE_CAP, D_MODEL), pos_t, w_t, final_g)


def _block_diag(w):
    nb, bs, _ = w.shape
    out = jnp.zeros((nb * bs, nb * bs), w.dtype)
    for i in range(nb):
        out = out.at[i * bs:(i + 1) * bs, i * bs:(i + 1) * bs].set(w[i])
    return out


def _selection_aggregator_np(n_cmp, n_slc):
    ratio_s = NSA_SLC_LEN // NSA_CMP_STRIDE
    ratio_c = NSA_CMP_LEN // NSA_CMP_STRIDE
    w = np.convolve(np.ones(ratio_s), np.ones(ratio_c)).astype(np.float32)
    cj = np.arange(n_slc)[:, None] * ratio_s - (ratio_c - 1) + np.arange(w.size)[None, :]
    jj = np.broadcast_to(np.arange(n_slc)[:, None], cj.shape)
    ww = np.broadcast_to(w[None, :], cj.shape)
    keep = (cj >= 0) & (cj < n_cmp)
    agg = np.zeros((n_cmp, n_slc), np.float32)
    np.add.at(agg, (cj[keep], jj[keep]), ww[keep])
    return agg


def _row(v):
    return v.reshape(1, -1).astype(F32)


def _diff_lambda_init(layer):
    return 0.8 - 0.6 * math.exp(-0.3 * layer)


def _mixer_layer(x, l, B, S, tiles, p):
    (bias_swa, bias_diff, bias_slc, bias_win, bias_cmp, agg_pad, n_cmp, ncp) = tiles
    w_in = jnp.pad(p["w_in"][l], ((0, 0), (0, N_IN_PAD - p["w_in"].shape[2]))).astype(BF16)
    w_t = jnp.concatenate([w_in[:, 1536:1792], w_in[:, 2176:2304]], axis=1).T
    a_xg, b_qkv, c_qkv, d_all, vals_t = _in_proj(x, _row(p["mix_norm_g"][l]), w_in, w_t)

    y_a = _rglru(a_xg, p["conv_w"][l].astype(F32), _row(p["conv_b"][l]),
                 _block_diag(p["lru_w_a"][l]).astype(BF16), _row(p["lru_b_a"][l]),
                 _block_diag(p["lru_w_x"][l]).astype(BF16), _row(p["lru_b_x"][l]),
                 _row(p["lru_lambda"][l]), B, S)

    b3 = b_qkv.reshape(B, S, 512)
    swa_cfg = tuple(((h // 2) * 64, (h % 2) != (h // 2), h // 2) for h in range(SWA_HEADS))
    y_b = _banded_attention(b3, 0, b3, 2, b3, 3, bias_swa, _row(p["swa_sinks"][l]),
                            B=B, S=S, window=SWA_WINDOW, q_cfg=swa_cfg)

    c3 = c_qkv.reshape(B, S, 768)
    lam4 = jnp.stack([p["diff_lq1"][l], p["diff_lk1"][l], p["diff_lq2"][l], p["diff_lk2"][l]]).astype(F32)
    gsub = jnp.tile(p["diff_subln_g"][l].reshape(-1, 1).astype(F32), (2, 1))
    y_c = _flash_attention("diff", c3, 0, c3, 1, vals_t, 256, 0, bias_diff, (lam4, gsub), B=B, S=S,
                           lambda_init=_diff_lambda_init(l))

    d3 = d_all.reshape(B, S, 768)
    nchunk = S // NSA_CMP_STRIDE
    ck = d3[:, :, 256:320].reshape(B, nchunk, NSA_CMP_STRIDE * HEAD_DIM)
    cv = d3[:, :, 320:384].reshape(B, nchunk, NSA_CMP_STRIDE * HEAD_DIM)

    def pos8(a):
        return jnp.broadcast_to(a.reshape(1, -1), (8, a.size)).astype(BF16)

    w1k = jnp.pad(p["nsa_w1_k"][l], ((0, 0), (0, HEAD_DIM))).astype(BF16)
    w1v = jnp.pad(p["nsa_w1_v"][l], ((0, 0), (HEAD_DIM, 0))).astype(BF16)
    w2 = _block_diag(jnp.stack([p["nsa_w2_k"][l], p["nsa_w2_v"][l]])).astype(BF16)
    kcv = _nsa_compress(ck, cv, pos8(p["nsa_pos_k"][l]), pos8(p["nsa_pos_v"][l]), w1k, w1v, w2, ncp)
    o_c, sel = _nsa_cmp_select(d3, kcv, bias_cmp, agg_pad, B=B, S=S, n_cmp=n_cmp, ncp=ncp)
    o_s = _flash_attention("nsa", d3, 0, d3, 3, vals_t, 128, 2, bias_slc, (sel,), B=B, S=S)
    win_cfg = tuple((0, h % 2 == 1, 1) for h in range(NSA_HEADS))
    o_w = _banded_attention(d3, 0, d3, 4, None, None, bias_win, None, B=B, S=S, window=NSA_WINDOW, q_cfg=win_cfg)

    T = B * S
    return _merge(x, _row(p["mix_norm_g"][l]), y_a, y_b.reshape(T, 256), y_c.reshape(T, 256),
                  o_c.reshape(T, 256), o_s.reshape(T, 256), o_w.reshape(T, 256), d_all,
                  p["w_gate"][l].astype(BF16), p["b_gate"][l].astype(F32), p["w_branch"][l].astype(BF16),
                  p["w_out"][l].astype(BF16))


def _moe_layer(x, l, p, final_norm):
    w_router = jnp.concatenate([p["w_router_exp"][l], p["w_router_grp"][l]], axis=1)
    w_router = jnp.pad(w_router, ((0, 0), (0, LANES - w_router.shape[1]))).astype(F32)
    w_hi = w_router.astype(BF16)
    w_router = jnp.concatenate([w_hi, (w_router - w_hi.astype(F32)).astype(BF16)], axis=1)
    b_router = jnp.concatenate([p["b_router_exp"][l], p["b_router_grp"][l]])
    b_router = jnp.pad(b_router, (0, LANES - b_router.shape[0])).reshape(1, LANES).astype(F32)
    g = _row(p["ffn_norm_g"][l])
    fg = _row(p["final_norm_g"])
    wg, wu, wd = (p[k][l].astype(BF16) for k in ("w_exp_gate", "w_exp_up", "w_exp_down"))
    xd, pos_t, w_t, counts = _moe_dispatch(x, g, w_router, b_router)
    return lax.cond(jnp.max(counts) > MOE_CAP,
                    lambda: _moe(x, g, w_router, b_router, wg, wu, wd, fg, final_norm),
                    lambda: _moe_finish(x, xd, pos_t, w_t, counts, wg, wu, wd, fg, final_norm))


def _forward(p):
    x = p["x"]
    B, S, D = x.shape
    depth = p["w_in"].shape[0]
    assert D == D_MODEL and S % (16 * Q_BLOCK) == 0 and S // NSA_SLC_LEN <= LANES
    rel = p["rel_bias"].astype(F32)
    n_cmp = (S - NSA_CMP_LEN) // NSA_CMP_STRIDE + 1
    ncp = LANES + S // NSA_CMP_STRIDE
    agg = _selection_aggregator_np(n_cmp, S // NSA_SLC_LEN)
    agg_pad = np.zeros((ncp, LANES), np.float32)
    agg_pad[LANES:LANES + n_cmp, :agg.shape[1]] = agg
    tiles = (
        _bias_tiles(rel, _banded_buckets(1), 0, SWA_HEADS, False),
        _bias_tiles(rel, _flash_buckets(), SWA_HEADS, DIFF_HEADS, True, LOG2E),
        _bias_tiles(rel, _flash_buckets(), SWA_HEADS + DIFF_HEADS, NSA_HEADS, True, LOG2E),
        _bias_tiles(rel, _banded_buckets(NSA_WINDOW // Q_BLOCK), SWA_HEADS + DIFF_HEADS, NSA_HEADS, False),
        _bias_tiles(rel, _cmp_buckets(), SWA_HEADS + DIFF_HEADS, NSA_HEADS, True),
        jnp.asarray(agg_pad, BF16), n_cmp, ncp,
    )
    xt = x.reshape(B * S, D).astype(F32)
    for l in range(depth):
        xt = _mixer_layer(xt, l, B, S, tiles, p)
        xt = _moe_layer(xt, l, p, final_norm=(l == depth - 1))
    return xt.reshape(B, S, D).astype(x.dtype)


def kernel(x, rel_bias, final_norm_g, mix_norm_g, w_in, conv_w, conv_b, lru_w_a, lru_b_a, lru_w_x, lru_b_x,
           lru_lambda, swa_sinks, diff_lq1, diff_lk1, diff_lq2, diff_lk2, diff_subln_g, nsa_pos_k, nsa_w1_k,
           nsa_w2_k, nsa_pos_v, nsa_w1_v, nsa_w2_v, w_gate, b_gate, w_branch, w_out, ffn_norm_g, w_router_grp,
           b_router_grp, w_router_exp, b_router_exp, w_exp_gate, w_exp_up, w_exp_down):
    return _forward(dict(
        x=x, rel_bias=rel_bias, final_norm_g=final_norm_g, mix_norm_g=mix_norm_g, w_in=w_in, conv_w=conv_w,
        conv_b=conv_b, lru_w_a=lru_w_a, lru_b_a=lru_b_a, lru_w_x=lru_w_x, lru_b_x=lru_b_x, lru_lambda=lru_lambda,
        swa_sinks=swa_sinks, diff_lq1=diff_lq1, diff_lk1=diff_lk1, diff_lq2=diff_lq2, diff_lk2=diff_lk2,
        diff_subln_g=diff_subln_g, nsa_pos_k=nsa_pos_k, nsa_w1_k=nsa_w1_k, nsa_w2_k=nsa_w2_k, nsa_pos_v=nsa_pos_v,
        nsa_w1_v=nsa_w1_v, nsa_w2_v=nsa_w2_v, w_gate=w_gate, b_gate=b_gate, w_branch=w_branch, w_out=w_out,
        ffn_norm_g=ffn_norm_g, w_router_grp=w_router_grp, b_router_grp=b_router_grp, w_router_exp=w_router_exp,
        b_router_exp=b_router_exp, w_exp_gate=w_exp_gate, w_exp_up=w_exp_up, w_exp_down=w_exp_down))
```

```python
import functools
import math

import numpy as np
import jax
import jax.numpy as jnp
from jax import lax
from jax.experimental import pallas as pl
from jax.experimental.pallas import tpu as pltpu

F32 = jnp.float32
BF16 = jnp.bfloat16
I32 = jnp.int32

D_MODEL = 1024
HEAD_DIM = 64
MIX_WIDTH = 256
LRU_C = 8.0
CONV_WIDTH = 4
SWA_HEADS = 4
SWA_WINDOW = 128
DIFF_HEADS = 4
DIFF_DIM = 32
DIFF_SUBLN_EPS = 1e-5
NSA_HEADS = 4
NSA_CMP_LEN = 32
NSA_CMP_STRIDE = 16
NSA_SLC_LEN = 64
NSA_TOPK = 16
NSA_WINDOW = 512
NSA_FORCED_SCORE = 1e4
REL_BUCKETS = 32
REL_MAX_DIST = 128
N_GROUPS = 4
EXPERTS_PER_GROUP = 8
N_EXPERTS = 32
EXPERT_FF = 256
NORM_EPS = 1e-6
NEG_INF = -1e30
TINY = 1e-30
LOG2E = math.log2(math.e)

LANES = 128
Q_BLOCK = 128
BANDED_Q_BLOCKS = 2
FLASH_BLOCK = 256
MOE_TILE = 512
MOE_CAP = 128
MOE_TILES_PER_STEP = 8
FLASH_FAR_BLOCKS = 2
N_IN_PAD = 2560
VMEM_LIMIT = 56 * 1024 * 1024

_NT = (((1,), (1,)), ((), ()))


def _cparams(*sem):
    return pltpu.CompilerParams(dimension_semantics=sem, vmem_limit_bytes=VMEM_LIMIT)


def _t5_bucket_np(dist):
    n = np.maximum(dist, 0)
    exact = REL_BUCKETS // 2
    scaled = (np.log(np.maximum(n, 1).astype(np.float32) / np.float32(exact))
              / np.float32(math.log(REL_MAX_DIST / exact))).astype(np.float32)
    large = np.minimum(exact + (scaled * np.float32(REL_BUCKETS - exact)).astype(np.int32), REL_BUCKETS - 1)
    return np.where(n < exact, n, large).astype(np.int32)


def _bias_tile_kernel(tab_ref, idx_ref, o_ref, *, head0, shift, scale):
    h = pl.program_id(0) + head0
    idx = idx_ref[0]
    acc = jnp.zeros(idx.shape, F32)
    for b in range(REL_BUCKETS):
        acc = jnp.where(idx == b, tab_ref[b, h], acc)
    if shift:
        acc = acc - tab_ref[REL_BUCKETS - 1, h]
    if scale != 1.0:
        acc = acc * scale
    o_ref[0, 0] = acc


def _bias_tiles(rel_bias, buckets, head0, nheads, shift, scale=1.0):
    R, M, N = buckets.shape
    return pl.pallas_call(
        functools.partial(_bias_tile_kernel, head0=head0, shift=shift, scale=scale),
        grid=(nheads, R),
        in_specs=[pl.BlockSpec(memory_space=pltpu.SMEM),
                  pl.BlockSpec((1, M, N), lambda h, r: (r, 0, 0))],
        out_specs=pl.BlockSpec((1, 1, M, N), lambda h, r: (h, r, 0, 0)),
        out_shape=jax.ShapeDtypeStruct((nheads, R, M, N), F32),
        compiler_params=_cparams("arbitrary", "arbitrary"),
        name="bias_tiles",
    )(rel_bias, jnp.asarray(buckets))


def _banded_buckets(nback):
    i = np.arange(BANDED_Q_BLOCKS * Q_BLOCK)[None, :]
    j = np.arange((nback + BANDED_Q_BLOCKS) * Q_BLOCK)[:, None]
    return _t5_bucket_np(i - j + nback * Q_BLOCK)[None]


def _flash_buckets():
    k = np.arange(FLASH_BLOCK)[:, None]
    q = np.arange(FLASH_BLOCK)[None, :]
    return np.stack([_t5_bucket_np(q - k), _t5_bucket_np(q - k + FLASH_BLOCK)])


def _cmp_buckets():
    i = np.arange(Q_BLOCK)[None, :, None]
    c2 = np.arange(2 * LANES)[None, None, :]
    r = np.arange(16)[:, None, None]
    dist = i - NSA_CMP_STRIDE * (c2 - LANES - 8 * r) - (NSA_CMP_LEN - 1)
    return _t5_bucket_np(dist)


def _rms(x, g):
    return x * lax.rsqrt(jnp.mean(x * x, axis=-1, keepdims=True) + NORM_EPS) * g


def _proj_kernel(x_ref, g_ref, w_ref, wt_ref, oa_ref, ob_ref, oc_ref, od_ref, ot_ref):
    h = _rms(x_ref[...], g_ref[...]).astype(BF16)
    oa_ref[...] = jnp.dot(h, w_ref[:, 0:512], preferred_element_type=F32)
    ob_ref[...] = jnp.dot(h, w_ref[:, 512:1024], preferred_element_type=F32).astype(BF16)
    oc_ref[...] = jnp.dot(h, w_ref[:, 1024:1792], preferred_element_type=F32).astype(BF16)
    od_ref[...] = jnp.dot(h, w_ref[:, 1792:2560], preferred_element_type=F32).astype(BF16)
    ot_ref[...] = lax.dot_general(wt_ref[...], h, _NT, preferred_element_type=F32).astype(BF16)


def _in_proj(x, g, w, wt):
    T = x.shape[0]
    tm = 512
    nt = wt.shape[0]
    return pl.pallas_call(
        _proj_kernel,
        grid=(T // tm,),
        in_specs=[pl.BlockSpec((tm, D_MODEL), lambda i: (i, 0)),
                  pl.BlockSpec((1, D_MODEL), lambda i: (0, 0)),
                  pl.BlockSpec((D_MODEL, N_IN_PAD), lambda i: (0, 0)),
                  pl.BlockSpec((nt, D_MODEL), lambda i: (0, 0))],
        out_specs=[pl.BlockSpec((tm, 512), lambda i: (i, 0)),
                   pl.BlockSpec((tm, 512), lambda i: (i, 0)),
                   pl.BlockSpec((tm, 768), lambda i: (i, 0)),
                   pl.BlockSpec((tm, 768), lambda i: (i, 0)),
                   pl.BlockSpec((nt, tm), lambda i: (0, i))],
        out_shape=[jax.ShapeDtypeStruct((T, 512), F32),
                   jax.ShapeDtypeStruct((T, 512), BF16),
                   jax.ShapeDtypeStruct((T, 768), BF16),
                   jax.ShapeDtypeStruct((T, 768), BF16),
                   jax.ShapeDtypeStruct((nt, T), BF16)],
        compiler_params=_cparams("arbitrary"),
        name="in_proj",
    )(x, g, w, wt)


def _lru_kernel(xg_ref, cw_ref, cb_ref, wa_ref, ba_ref, wx_ref, bx_ref, lam_ref, o_ref,
                ext_ref, h_ref, *, Tt):
    t = pl.program_id(1)

    W = MIX_WIDTH

    @pl.when(t == 0)
    def _():
        ext_ref[0:8, :] = jnp.zeros((8, W), F32)
        h_ref[...] = jnp.zeros_like(h_ref)

    x = xg_ref[:, 0:W]
    gate = xg_ref[:, W:2 * W]
    ext_ref[8:Tt + 8, :] = x
    row = lax.broadcasted_iota(I32, (Tt, W), 0)
    xc = cb_ref[...] + x * cw_ref[CONV_WIDTH - 1:CONV_WIDTH, :]
    for s in range(1, CONV_WIDTH):
        xc = xc + ext_ref[8 - s:8 - s + Tt, :] * cw_ref[CONV_WIDTH - 1 - s:CONV_WIDTH - s, :]
    ext_ref[0:8, :] = ext_ref[Tt:Tt + 8, :]

    xcb = xc.astype(BF16)
    r = jax.nn.sigmoid(jnp.dot(xcb, wa_ref[...], preferred_element_type=F32) + ba_ref[...])
    ig = jax.nn.sigmoid(jnp.dot(xcb, wx_ref[...], preferred_element_type=F32) + bx_ref[...])
    z = -lam_ref[...]
    softplus = jnp.maximum(z, 0.0) + jnp.log1p(jnp.exp(-jnp.abs(z)))
    log_a = (-LRU_C) * r * softplus
    a = jnp.exp(log_a)
    mult = jnp.sqrt(1.0 - a * a)
    mult = jnp.where((row + t * Tt) == 0, 1.0, mult)
    b = mult * ig * xc

    d = 1
    while d < Tt:
        keep = row >= d
        b = jnp.where(keep, a * pltpu.roll(b, d, 0) + b, b)
        a = jnp.where(keep, a * pltpu.roll(a, d, 0), a)
        d *= 2
    hs = a * h_ref[7:8, :] + b
    h_ref[...] = hs[Tt - 8:Tt]
    o_ref[...] = (hs * jax.nn.gelu(gate)).astype(o_ref.dtype)


def _rglru(a_xg, conv_w, conv_b, wa_bd, b_a, wx_bd, b_x, lam, B, S):
    Tt = 256
    W = MIX_WIDTH
    vec = pl.BlockSpec((1, W), lambda b, t: (0, 0))
    mat = pl.BlockSpec((W, W), lambda b, t: (0, 0))
    out = pl.pallas_call(
        functools.partial(_lru_kernel, Tt=Tt),
        grid=(B, S // Tt),
        in_specs=[pl.BlockSpec((None, Tt, 2 * W), lambda b, t: (b, t, 0)),
                  pl.BlockSpec((CONV_WIDTH, W), lambda b, t: (0, 0)),
                  vec, mat, vec, mat, vec, vec],
        out_specs=pl.BlockSpec((None, Tt, W), lambda b, t: (b, t, 0)),
        out_shape=jax.ShapeDtypeStruct((B, S, W), BF16),
        scratch_shapes=[pltpu.VMEM((Tt + 8, W), F32), pltpu.VMEM((8, W), F32)],
        compiler_params=_cparams("arbitrary", "arbitrary"),
        name="rglru",
    )(a_xg.reshape(B, S, 2 * W), conv_w, conv_b, wa_bd, b_a, wx_bd, b_x, lam)
    return out.reshape(B * S, W)


def _lane_window(q128, lo, width, roll, scale):
    q = q128 * scale
    if roll:
        q = pltpu.roll(q, LANES // 2, 1)
    lane = lax.broadcasted_iota(I32, q.shape, 1)
    return jnp.where((lane >= lo) & (lane < lo + width), q, 0.0).astype(BF16)


def _pair_halves(even, odd, even_half, odd_half):
    if even_half == 1:
        even = pltpu.roll(even, LANES // 2, 1)
    if odd_half == 0:
        odd = pltpu.roll(odd, LANES // 2, 1)
    lane = lax.broadcasted_iota(I32, even.shape, 1)
    return jnp.where(lane < LANES // 2, even, odd)


def _banded_kernel(*refs, nback, window, q_cfg, has_sink, shared_kv):
    nk = nback + BANDED_Q_BLOCKS
    TQ = BANDED_Q_BLOCKS * Q_BLOCK
    q_ref = refs[0]
    k_refs = refs[1:1 + nk]
    pos = 1 + nk
    if shared_kv:
        v_refs = k_refs
    else:
        v_refs = refs[pos:pos + nk]
        pos += nk
    bias_ref = refs[pos]
    pos += 1
    sink_ref = None
    if has_sink:
        sink_ref = refs[pos]
        pos += 1
    o_ref = refs[pos]

    n = pl.program_id(1)
    KW = nk * Q_BLOCK
    kcat = jnp.concatenate([k_refs[nk - 1 - i][...] for i in range(nk)], axis=0)
    vcat = kcat if shared_kv else jnp.concatenate([v_refs[nk - 1 - i][...] for i in range(nk)], axis=0)
    vt = vcat.astype(F32).T.astype(BF16)
    jj = lax.broadcasted_iota(I32, (KW, TQ), 0)
    ii = lax.broadcasted_iota(I32, (KW, TQ), 1)
    dist = ii - jj + nback * Q_BLOCK
    valid = (dist >= 0) & (dist < window) & (jj >= (nback - BANDED_Q_BLOCKS * n) * Q_BLOCK)
    scale = HEAD_DIM ** -0.5

    def logits(h):
        lo, roll, _ = q_cfg[h]
        g = h // 2
        qx = _lane_window(q_ref[:, LANES * g:LANES * (g + 1)].astype(F32), lo, HEAD_DIM, roll, scale)
        s = lax.dot_general(kcat, qx, _NT, preferred_element_type=F32) + bias_ref[h, 0]
        return jnp.where(valid, s, NEG_INF)

    def softmax(h, s):
        m = jnp.max(s, axis=0, keepdims=True)
        if has_sink:
            sink = sink_ref[0:1, h:h + 1]
            m = jnp.maximum(m, sink)
        p = jnp.exp(s - m)
        den = jnp.sum(p, axis=0, keepdims=True)
        if has_sink:
            den = den + jnp.exp(sink - m)
        return p.astype(BF16), 1.0 / jnp.maximum(den, TINY)

    def values(h, p, inv):
        half = q_cfg[h][2]
        return jnp.dot(vt[HEAD_DIM * half:HEAD_DIM * (half + 1)], p, preferred_element_type=F32) * inv

    nh = len(q_cfg)
    ss = [logits(h) for h in range(nh)]
    pr = [softmax(h, ss[h]) for h in range(nh)]
    outs = [values(h, *pr[h]) for h in range(nh)]
    for g in range(nh // 2):
        pair = jnp.concatenate([outs[2 * g], outs[2 * g + 1]], axis=0)
        o_ref[:, LANES * g:LANES * (g + 1)] = pair.T.astype(o_ref.dtype)


def _banded_attention(qsrc, q_col, ksrc, k_col, vsrc, v_col, bias, sinks, *, B, S, window, q_cfg):
    nback = -(-window // Q_BLOCK)
    nq = BANDED_Q_BLOCKS
    nk = nback + nq
    TQ = nq * Q_BLOCK
    shared_kv = vsrc is None
    has_sink = sinks is not None

    def kv_spec(col, i):
        return pl.BlockSpec((None, Q_BLOCK, LANES), lambda b, n: (b, jnp.maximum(nq * n + nq - 1 - i, 0), col))

    in_specs = [pl.BlockSpec((None, TQ, 2 * LANES), lambda b, n: (b, n, q_col))]
    args = [qsrc]
    in_specs += [kv_spec(k_col, i) for i in range(nk)]
    args += [ksrc] * nk
    if not shared_kv:
        in_specs += [kv_spec(v_col, i) for i in range(nk)]
        args += [vsrc] * nk
    in_specs.append(pl.BlockSpec(bias.shape, lambda b, n: (0, 0, 0, 0)))
    args.append(bias)
    if has_sink:
        in_specs.append(pl.BlockSpec(sinks.shape, lambda b, n: (0, 0)))
        args.append(sinks)
    return pl.pallas_call(
        functools.partial(_banded_kernel, nback=nback, window=window, q_cfg=q_cfg,
                          has_sink=has_sink, shared_kv=shared_kv),
        grid=(B, S // TQ),
        in_specs=in_specs,
        out_specs=pl.BlockSpec((None, TQ, 2 * LANES), lambda b, n: (b, n, 0)),
        out_shape=jax.ShapeDtypeStruct((B, S, 2 * LANES), BF16),
        compiler_params=_cparams("arbitrary", "arbitrary"),
        name="banded_attn",
    )(*args)


def _flash_kernel(*refs, mode, lambda_init):
    TB = FLASH_BLOCK
    if mode == "diff":
        (q_ref, k_ref, vt_ref, bias_ref, lam_ref, g_ref, o_ref,
         qx_ref, m_ref, l_ref, acc_ref) = refs
        sel_ref = None
        maps = [dict(g=h // 2, lo=(h % 2) * 64 + 32 * mm, w=DIFF_DIM, roll=False, kg=h // 2, vrow=HEAD_DIM * h, h=h)
                for h in range(DIFF_HEADS) for mm in range(2)]
        scale = DIFF_DIM ** -0.5
    else:
        (q_ref, k_ref, vt_ref, sel_ref, bias_ref, o_ref,
         qx_ref, m_ref, l_ref, acc_ref) = refs
        maps = [dict(g=h // 2, lo=0, w=HEAD_DIM, roll=(h % 2 == 1), kg=0, vrow=HEAD_DIM, h=h)
                for h in range(NSA_HEADS)]
        scale = HEAD_DIM ** -0.5
    n = pl.program_id(1)

    for mp, c in enumerate(maps):
        qx_ref[mp] = _lane_window(q_ref[:, LANES * c["g"]:LANES * (c["g"] + 1)].astype(F32),
                                  c["lo"], c["w"], c["roll"], scale * LOG2E)
    m_ref[...] = jnp.full(m_ref.shape, NEG_INF, F32)
    l_ref[...] = jnp.zeros_like(l_ref)
    acc_ref[...] = jnp.zeros_like(acc_ref)

    key = lax.broadcasted_iota(I32, (TB, TB), 0)
    qry = lax.broadcasted_iota(I32, (TB, TB), 1)

    def step(j, kind, TK=TB):
        koff = pl.multiple_of(j * TB, TB)
        valid = None
        if kind == "diag":
            valid = qry >= key
        if sel_ref is not None:
            per = TK // NSA_SLC_LEN
            base = j * (TB // NSA_SLC_LEN)
            rows = [jnp.broadcast_to(sel_ref[pl.ds(base + i, 1), :], (NSA_SLC_LEN, TB)) for i in range(per)]
            picked = jnp.concatenate(rows, axis=0) > 0.5
            valid = picked if valid is None else (valid & picked)

        def logits(mp):
            c = maps[mp]
            kt = k_ref[pl.ds(koff, TK), LANES * c["kg"]:LANES * (c["kg"] + 1)]
            s = lax.dot_general(kt, qx_ref[mp], _NT, preferred_element_type=F32)
            if kind == "near":
                s = s + bias_ref[c["h"], 1]
            elif kind == "diag":
                s = s + bias_ref[c["h"], 0]
            if valid is not None:
                s = jnp.where(valid, s, NEG_INF)
            return s

        def softmax(mp, s):
            m_prev = m_ref[mp]
            m_new = jnp.maximum(m_prev, jnp.max(s, axis=0, keepdims=True))
            alpha = jnp.exp2(m_prev - m_new)
            p = jnp.exp2(s - m_new)
            l_ref[mp] = alpha * l_ref[mp] + jnp.sum(p, axis=0, keepdims=True)
            m_ref[mp] = m_new
            return p.astype(BF16), alpha

        def values(mp, p, alpha):
            c = maps[mp]
            vt = vt_ref[c["vrow"]:c["vrow"] + HEAD_DIM, pl.ds(koff, TK)]
            acc_ref[mp] = alpha * acc_ref[mp] + jnp.dot(vt, p, preferred_element_type=F32)

        ss = [logits(mp) for mp in range(len(maps))]
        pa = [softmax(mp, ss[mp]) for mp in range(len(maps))]
        for mp in range(len(maps)):
            values(mp, *pa[mp])

    nfar = jnp.maximum(n - 1, 0)
    FW = FLASH_FAR_BLOCKS

    def far_body(j, carry):
        step(FW * j, "far", FW * TB)
        return carry

    lax.fori_loop(0, nfar // FW, far_body, 0)

    def rest_body(j, carry):
        step(j, "far")
        return carry

    lax.fori_loop((nfar // FW) * FW, nfar, rest_body, 0)

    @pl.when(n >= 1)
    def _():
        step(n - 1, "near")

    step(n, "diag")

    def normalised(mp):
        return acc_ref[mp] * (1.0 / jnp.maximum(l_ref[mp], TINY))

    if mode == "diff":
        lq = lam_ref[...]
        lam = (jnp.exp(jnp.sum(lq[0:1] * lq[1:2], axis=1, keepdims=True))
               - jnp.exp(jnp.sum(lq[2:3] * lq[3:4], axis=1, keepdims=True)) + lambda_init)
        for g in range(DIFF_HEADS // 2):
            res = []
            for h in (2 * g, 2 * g + 1):
                o = normalised(2 * h) - lam * normalised(2 * h + 1)
                ms = jnp.sum(o * o, axis=0, keepdims=True) * (1.0 / (2 * DIFF_DIM))
                res.append(o * lax.rsqrt(ms + DIFF_SUBLN_EPS))
            out = jnp.concatenate(res, axis=0) * g_ref[...] * (1.0 - lambda_init)
            o_ref[:, LANES * g:LANES * (g + 1)] = out.T.astype(o_ref.dtype)
    else:
        for g in range(NSA_HEADS // 2):
            out = jnp.concatenate([normalised(2 * g), normalised(2 * g + 1)], axis=0)
            o_ref[:, LANES * g:LANES * (g + 1)] = out.T.astype(o_ref.dtype)


def _flash_attention(mode, qsrc, q_col, ksrc, k_col, vt, vt_rows, vt_blk, bias, extra, *, B, S, lambda_init=0.0):
    TB = FLASH_BLOCK
    q_spec = pl.BlockSpec((None, TB, 2 * LANES), lambda b, n: (b, n, q_col))
    bias_spec = pl.BlockSpec(bias.shape, lambda b, n: (0, 0, 0, 0))
    vt_spec = pl.BlockSpec((vt_rows, S), lambda b, n: (vt_blk, b))
    if mode == "diff":
        lam4, gsub = extra
        nmaps = 2 * DIFF_HEADS
        in_specs = [q_spec,
                    pl.BlockSpec((None, S, 2 * LANES), lambda b, n: (b, 0, k_col)),
                    vt_spec, bias_spec,
                    pl.BlockSpec(lam4.shape, lambda b, n: (0, 0)),
                    pl.BlockSpec(gsub.shape, lambda b, n: (0, 0))]
        args = [qsrc, ksrc, vt, bias, lam4, gsub]
    else:
        (sel,) = extra
        nmaps = NSA_HEADS
        in_specs = [q_spec,
                    pl.BlockSpec((None, S, LANES), lambda b, n: (b, 0, k_col)),
                    vt_spec,
                    pl.BlockSpec((None, LANES, TB), lambda b, n: (b, 0, n)),
                    bias_spec]
        args = [qsrc, ksrc, vt, sel, bias]
    return pl.pallas_call(
        functools.partial(_flash_kernel, mode=mode, lambda_init=lambda_init),
        grid=(B, S // TB),
        in_specs=in_specs,
        out_specs=pl.BlockSpec((None, TB, 2 * LANES), lambda b, n: (b, n, 0)),
        out_shape=jax.ShapeDtypeStruct((B, S, 2 * LANES), BF16),
        scratch_shapes=[pltpu.VMEM((nmaps, TB, LANES), BF16),
                        pltpu.VMEM((nmaps, 1, TB), F32),
                        pltpu.VMEM((nmaps, 1, TB), F32),
                        pltpu.VMEM((nmaps, HEAD_DIM, TB), F32)],
        compiler_params=_cparams("arbitrary", "arbitrary"),
        name="flash_" + mode,
    )(*args)


def _compress_kernel(ck_ref, cv_ref, pk_ref, pv_ref, w1k_ref, w1v_ref, w2_ref, o_ref):
    half = NSA_CMP_STRIDE * HEAD_DIM
    ck = ck_ref[...]
    cv = cv_ref[...]
    nrow = ck.shape[0]

    def mm(a, w):
        return jnp.dot(a, w, preferred_element_type=F32)

    top = mm(ck, w1k_ref[0:half, :]) + mm(cv, w1v_ref[0:half, :])
    bot = mm(ck, w1k_ref[half:2 * half, :]) + mm(cv, w1v_ref[half:2 * half, :])
    posw = (mm(pk_ref[...], w1k_ref[...]) + mm(pv_ref[...], w1v_ref[...]))[0:1]
    pre = top + pltpu.roll(bot, nrow - 1, 0) + posw
    kcv = mm(jax.nn.gelu(pre).astype(BF16), w2_ref[...])
    o_ref[0:LANES, :] = jnp.zeros((LANES, LANES), o_ref.dtype)
    o_ref[LANES:LANES + nrow, :] = kcv.astype(o_ref.dtype)


def _nsa_compress(ck, cv, pk, pv, w1k, w1v, w2, ncp):
    B, nchunk, width = ck.shape
    assert ncp == LANES + nchunk

    def full(a):
        return pl.BlockSpec(a.shape, lambda b: (0,) * a.ndim)

    chunk = pl.BlockSpec((None, nchunk, width), lambda b: (b, 0, 0))
    return pl.pallas_call(
        _compress_kernel,
        grid=(B,),
        in_specs=[chunk, chunk, full(pk), full(pv), full(w1k), full(w1v), full(w2)],
        out_specs=pl.BlockSpec((None, ncp, LANES), lambda b: (b, 0, 0)),
        out_shape=jax.ShapeDtypeStruct((B, ncp, LANES), BF16),
        compiler_params=_cparams("arbitrary"),
        name="nsa_compress",
    )(ck, cv, pk, pv, w1k, w1v, w2)


def _cmp_kernel(q_ref, kcv_ref, bias_ref, agg_ref, oc_ref, sel_ref, s_scr, *, n_cmp, ncp):
    n = pl.program_id(1)
    QB = Q_BLOCK
    kcv = kcv_ref[...]
    cp = lax.broadcasted_iota(I32, (QB, ncp), 1)
    qpos = n * QB + lax.broadcasted_iota(I32, (QB, ncp), 0)
    valid = ((cp >= LANES) & (cp < LANES + n_cmp)
             & (qpos - NSA_CMP_STRIDE * (cp - LANES) - (NSA_CMP_LEN - 1) >= 0))
    woff = pl.multiple_of((n // 16) * LANES, LANES)
    scale = HEAD_DIM ** -0.5

    def logits(h):
        g = h // 2
        qx = _lane_window(q_ref[:, LANES * g:LANES * (g + 1)].astype(F32), 0, HEAD_DIM, h % 2 == 1, scale)
        s_scr[h] = lax.dot_general(qx, kcv, _NT, preferred_element_type=F32)
        s_scr[h, :, pl.ds(woff, 2 * LANES)] += bias_ref[h]

    def softmax(h):
        s = jnp.where(valid, s_scr[h], NEG_INF)
        m = jnp.max(s, axis=1, keepdims=True)
        p = jnp.where(valid, jnp.exp(s - m), 0.0)
        den = jnp.sum(p, axis=1, keepdims=True)
        return p * (1.0 / jnp.maximum(den, TINY))

    for h in range(NSA_HEADS):
        logits(h)
    ps = [softmax(h) for h in range(NSA_HEADS)]
    outs = [jnp.dot(p.astype(BF16), kcv, preferred_element_type=F32) for p in ps]
    for g in range(NSA_HEADS // 2):
        oc_ref[:, LANES * g:LANES * (g + 1)] = _pair_halves(outs[2 * g], outs[2 * g + 1], 1, 1).astype(oc_ref.dtype)

    psum = (ps[0] + ps[1]) + (ps[2] + ps[3])
    imp = jnp.dot(psum.astype(BF16), agg_ref[...], preferred_element_type=F32)

    v = imp.T
    jb = lax.broadcasted_iota(I32, v.shape, 0)
    qp = n * QB + lax.broadcasted_iota(I32, v.shape, 1)
    cur = qp >> 6
    forced = (jb == 0) | (jb == cur) | (jb == cur - 1)
    v = jnp.where(forced, -jnp.inf, jnp.where(jb * NSA_SLC_LEN <= qp, v, -1.0))
    rid = jb.astype(F32)
    picked = jnp.where(forced, 1.0, 0.0)
    for _ in range(NSA_TOPK - 3):
        mx = jnp.max(v, axis=0, keepdims=True)
        first = jnp.min(jnp.where(v == mx, rid, float(LANES)), axis=0, keepdims=True)
        hit = rid == first
        picked = jnp.where(hit, 1.0, picked)
        v = jnp.where(hit, -jnp.inf, v)
    sel_ref[...] = picked


def _nsa_cmp_select(dsrc, kcv, bias, agg, *, B, S, n_cmp, ncp):
    return pl.pallas_call(
        functools.partial(_cmp_kernel, n_cmp=n_cmp, ncp=ncp),
        grid=(B, S // Q_BLOCK),
        in_specs=[pl.BlockSpec((None, Q_BLOCK, 2 * LANES), lambda b, n: (b, n, 0)),
                  pl.BlockSpec((None, ncp, LANES), lambda b, n: (b, 0, 0)),
                  pl.BlockSpec((NSA_HEADS, None, Q_BLOCK, 2 * LANES), lambda b, n: (0, n % 16, 0, 0)),
                  pl.BlockSpec((ncp, LANES), lambda b, n: (0, 0))],
        out_specs=[pl.BlockSpec((None, Q_BLOCK, 2 * LANES), lambda b, n: (b, n, 0)),
                   pl.BlockSpec((None, LANES, Q_BLOCK), lambda b, n: (b, 0, n))],
        out_shape=[jax.ShapeDtypeStruct((B, S, 2 * LANES), BF16),
                   jax.ShapeDtypeStruct((B, LANES, S), F32)],
        scratch_shapes=[pltpu.VMEM((NSA_HEADS, Q_BLOCK, ncp), F32)],
        compiler_params=_cparams("arbitrary", "arbitrary"),
        name="nsa_cmp_select",
    )(dsrc, kcv, bias, agg)


def _merge_kernel(x_ref, g_ref, ya_ref, yb_ref, yc_ref, oc_ref, os_ref, ow_ref, dg_ref,
                  wg_ref, bg_ref, wb_ref, wo_ref, o_ref):
    x = x_ref[...]
    h = _rms(x, g_ref[...]).astype(BF16)
    tm = x.shape[0]
    lane = lax.broadcasted_iota(I32, (tm, LANES), 1)
    gd = jax.nn.sigmoid(dg_ref[...].astype(F32))
    yd = []
    for g in range(NSA_HEADS // 2):
        acc = jnp.zeros((tm, LANES), F32)
        for br, src in enumerate((oc_ref, os_ref, ow_ref)):
            ge = gd[:, 3 * (2 * g) + br:3 * (2 * g) + br + 1]
            go = gd[:, 3 * (2 * g + 1) + br:3 * (2 * g + 1) + br + 1]
            acc = acc + jnp.where(lane < 64, ge, go) * src[:, LANES * g:LANES * (g + 1)].astype(F32)
        yd.append(acc)
    y_d = jnp.concatenate(yd, axis=1).astype(BF16)
    ys = (ya_ref[...], yb_ref[...], yc_ref[...], y_d)
    merged = jnp.zeros((tm, D_MODEL), F32)
    for br in range(4):
        gate = jax.nn.sigmoid(jnp.dot(h, wg_ref[br], preferred_element_type=F32) + bg_ref[br:br + 1, :])
        merged = merged + gate * jnp.dot(ys[br], wb_ref[br], preferred_element_type=F32)
    o_ref[...] = x + jnp.dot(merged.astype(BF16), wo_ref[...], preferred_element_type=F32)


def _merge(x, g, ya, yb, yc, oc, osel, ow, dproj, wg, bg, wb, wo):
    T = x.shape[0]
    tm = 256

    def tile(width, col=0):
        return pl.BlockSpec((tm, width), lambda i: (i, col))

    def full(a):
        return pl.BlockSpec(a.shape, lambda i: (0,) * a.ndim)

    return pl.pallas_call(
        _merge_kernel,
        grid=(T // tm,),
        in_specs=[tile(D_MODEL), full(g), tile(256), tile(256), tile(256), tile(256), tile(256), tile(256),
                  tile(LANES, 5), full(wg), full(bg), full(wb), full(wo)],
        out_specs=tile(D_MODEL),
        out_shape=jax.ShapeDtypeStruct((T, D_MODEL), F32),
        compiler_params=_cparams("arbitrary"),
        name="merge",
    )(x, g, ya, yb, yc, oc, osel, ow, dproj, wg, bg, wb, wo)


def _route(hn, w_split, b_router):
    lane = lax.broadcasted_iota(I32, (hn.shape[0], LANES), 1)
    hi = hn.astype(BF16)
    lo = (hn - hi.astype(F32)).astype(BF16)
    a = jnp.dot(hi, w_split, preferred_element_type=F32)
    b = jnp.dot(lo, w_split[:, 0:LANES], preferred_element_type=F32)
    logits = (a[:, 0:LANES] + a[:, LANES:2 * LANES]) + b + b_router
    is_grp = (lane >= N_EXPERTS) & (lane < N_EXPERTS + N_GROUPS)
    lanef = lane.astype(F32)
    lg = jnp.where(is_grp, logits, -jnp.inf)
    mg = jnp.max(lg, axis=1, keepdims=True)
    gsel = jnp.min(jnp.where(lg == mg, lanef, 2.0 * LANES), axis=1, keepdims=True) - N_EXPERTS
    g_w = 1.0 / jnp.sum(jnp.where(is_grp, jnp.exp(lg - mg), 0.0), axis=1, keepdims=True)
    in_grp = (lane < N_EXPERTS) & ((lane >> 3).astype(F32) == gsel)
    v1 = jnp.where(in_grp, logits, -jnp.inf)
    m1 = jnp.max(v1, axis=1, keepdims=True)
    i1 = jnp.min(jnp.where(v1 == m1, lanef, 2.0 * LANES), axis=1, keepdims=True)
    v2 = jnp.where(lanef == i1, -jnp.inf, v1)
    m2 = jnp.max(v2, axis=1, keepdims=True)
    i2 = jnp.min(jnp.where(v2 == m2, lanef, 2.0 * LANES), axis=1, keepdims=True)
    e2 = jnp.exp(m2 - m1)
    w1 = g_w / (1.0 + e2)
    w2 = g_w * e2 / (1.0 + e2)
    first, second = lanef == i1, lanef == i2
    comb = jnp.where(first, w1, 0.0) + jnp.where(second, w2, 0.0)
    return comb, jnp.where(first | second, 1.0, 0.0)


def _moe_kernel(x_ref, g_ref, wr_ref, br_ref, wg_ref, wu_ref, wd_ref, fg_ref, o_ref,
                hn_ref, comb_ref, acc_ref, *, final_norm):
    e = pl.program_id(1)
    tm = x_ref.shape[0]
    lane = lax.broadcasted_iota(I32, (tm, LANES), 1)

    @pl.when(e == 0)
    def _():
        hn = _rms(x_ref[...], g_ref[...])
        hn_ref[...] = hn.astype(BF16)
        comb_ref[...] = _route(hn, wr_ref[...], br_ref[...])[0]
        acc_ref[...] = jnp.zeros_like(acc_ref)

    hn = hn_ref[...]
    gt = jnp.dot(hn, wg_ref[...], preferred_element_type=F32)
    up = jnp.dot(hn, wu_ref[...], preferred_element_type=F32)
    he = (gt * jax.nn.sigmoid(gt) * up).astype(BF16)
    ye = jnp.dot(he, wd_ref[...], preferred_element_type=F32)
    cw = jnp.sum(jnp.where(lane == e, comb_ref[...], 0.0), axis=1, keepdims=True)
    acc_ref[...] += cw * ye

    @pl.when(e == N_EXPERTS - 1)
    def _():
        y = x_ref[...] + acc_ref[...]
        if final_norm:
            y = _rms(y, fg_ref[...])
        o_ref[...] = y


def _moe(x, g, w_router, b_router, wg, wu, wd, final_g, final_norm):
    T = x.shape[0]
    tm = 1024
    return pl.pallas_call(
        functools.partial(_moe_kernel, final_norm=final_norm),
        grid=(T // tm, N_EXPERTS),
        in_specs=[pl.BlockSpec((tm, D_MODEL), lambda i, e: (i, 0)),
                  pl.BlockSpec((1, D_MODEL), lambda i, e: (0, 0)),
                  pl.BlockSpec((D_MODEL, 2 * LANES), lambda i, e: (0, 0)),
                  pl.BlockSpec((1, LANES), lambda i, e: (0, 0)),
                  pl.BlockSpec((None, D_MODEL, EXPERT_FF), lambda i, e: (e, 0, 0)),
                  pl.BlockSpec((None, D_MODEL, EXPERT_FF), lambda i, e: (e, 0, 0)),
                  pl.BlockSpec((None, EXPERT_FF, D_MODEL), lambda i, e: (e, 0, 0)),
                  pl.BlockSpec((1, D_MODEL), lambda i, e: (0, 0))],
        out_specs=pl.BlockSpec((tm, D_MODEL), lambda i, e: (i, 0)),
        out_shape=jax.ShapeDtypeStruct((T, D_MODEL), F32),
        scratch_shapes=[pltpu.VMEM((tm, D_MODEL), BF16),
                        pltpu.VMEM((tm, LANES), F32),
                        pltpu.VMEM((tm, D_MODEL), F32)],
        compiler_params=_cparams("arbitrary", "arbitrary"),
        name="moe",
    )(x, g, w_router, b_router, wg, wu, wd, final_g)


def _slot_matrix(pos_t, val_t, ntok, half):
    nslot = MOE_CAP // 2
    slot = (lax.broadcasted_iota(I32, (nslot, ntok), 0) + half * nslot).astype(F32)
    blocks = []
    for e in range(N_EXPERTS):
        hit = pos_t[e:e + 1, :] == slot
        blocks.append(jnp.where(hit, 1.0 if val_t is None else val_t[e:e + 1, :], 0.0).astype(BF16))
    return jnp.concatenate(blocks, axis=0)


def _dispatch_kernel(x_ref, g_ref, wr_ref, br_ref, tri_ref, xd_ref, pos_ref, wt_ref, cnt_ref):
    tm = x_ref.shape[0]
    hn = _rms(x_ref[...], g_ref[...])
    comb, routed = _route(hn, wr_ref[...], br_ref[...])
    rank = jnp.dot(tri_ref[...], routed.astype(BF16), preferred_element_type=F32)
    pos = jnp.where(routed > 0.0, rank, -1.0)
    pos_t = pos.T
    pos_ref[...] = pos_t
    wt_ref[...] = comb.T
    counts = jnp.sum(routed, axis=0, keepdims=True)
    cnt_ref[...] = jnp.broadcast_to(counts, cnt_ref.shape)
    hnb = hn.astype(BF16)
    nslot = MOE_CAP // 2
    lower = jnp.dot(_slot_matrix(pos_t, None, tm, 0), hnb, preferred_element_type=F32)
    xd_ref[:, 0:nslot, :] = lower.reshape(N_EXPERTS, nslot, D_MODEL).astype(xd_ref.dtype)

    busy = jnp.max(counts) > nslot

    @pl.when(busy)
    def _():
        upper = jnp.dot(_slot_matrix(pos_t, None, tm, 1), hnb, preferred_element_type=F32)
        xd_ref[:, nslot:MOE_CAP, :] = upper.reshape(N_EXPERTS, nslot, D_MODEL).astype(xd_ref.dtype)

    @pl.when(jnp.logical_not(busy))
    def _():
        xd_ref[:, nslot:MOE_CAP, :] = jnp.zeros((N_EXPERTS, nslot, D_MODEL), xd_ref.dtype)


def _expert_kernel(busy_ref, x_ref, wg_ref, wu_ref, wd_ref, o_ref):
    ntile, _, nslot, _ = x_ref.shape

    def ffn(half):
        x = x_ref[:, half].reshape(ntile * nslot, D_MODEL)
        gt = jnp.dot(x, wg_ref[...], preferred_element_type=F32)
        up = jnp.dot(x, wu_ref[...], preferred_element_type=F32)
        he = (gt * jax.nn.sigmoid(gt) * up).astype(BF16)
        ye = jnp.dot(he, wd_ref[...], preferred_element_type=F32)
        o_ref[:, half] = ye.reshape(ntile, nslot, D_MODEL).astype(o_ref.dtype)

    ffn(0)
    busy = busy_ref[pl.program_id(0), pl.program_id(1)] > 0

    @pl.when(busy)
    def _():
        ffn(1)

    @pl.when(jnp.logical_not(busy))
    def _():
        o_ref[:, 1] = jnp.zeros((ntile, nslot, D_MODEL), o_ref.dtype)


def _combine_kernel(busy_ref, x_ref, yd_ref, pos_ref, wt_ref, fg_ref, o_ref, *, final_norm):
    tm = x_ref.shape[0]
    nslot = MOE_CAP // 2

    def gathered(half):
        weights = _slot_matrix(pos_ref[...], wt_ref[...], tm, half)
        yd = yd_ref[:, half * nslot:(half + 1) * nslot, :].reshape(N_EXPERTS * nslot, D_MODEL)
        return lax.dot_general(weights, yd, (((0,), (0,)), ((), ())), preferred_element_type=F32)

    o_ref[...] = x_ref[...] + gathered(0)

    @pl.when(busy_ref[pl.program_id(0)] > 0)
    def _():
        o_ref[...] += gathered(1)

    if final_norm:
        o_ref[...] = _rms(o_ref[...], fg_ref[...])


def _moe_dispatch(x, g, w_router, b_router):
    T = x.shape[0]
    tm = MOE_TILE
    nt = T // tm
    tri = jnp.asarray(np.tril(np.ones((tm, tm), np.float32), -1), BF16)
    xd, pos_t, w_t, cnt = pl.pallas_call(
        _dispatch_kernel,
        grid=(nt,),
        in_specs=[pl.BlockSpec((tm, D_MODEL), lambda i: (i, 0)),
                  pl.BlockSpec((1, D_MODEL), lambda i: (0, 0)),
                  pl.BlockSpec((D_MODEL, 2 * LANES), lambda i: (0, 0)),
                  pl.BlockSpec((1, LANES), lambda i: (0, 0)),
                  pl.BlockSpec((tm, tm), lambda i: (0, 0))],
        out_specs=[pl.BlockSpec((N_EXPERTS, None, MOE_CAP, D_MODEL), lambda i: (0, i, 0, 0)),
                   pl.BlockSpec((LANES, tm), lambda i: (0, i)),
                   pl.BlockSpec((LANES, tm), lambda i: (0, i)),
                   pl.BlockSpec((None, 8, LANES), lambda i: (i, 0, 0))],
        out_shape=[jax.ShapeDtypeStruct((N_EXPERTS, nt, MOE_CAP, D_MODEL), BF16),
                   jax.ShapeDtypeStruct((LANES, T), F32),
                   jax.ShapeDtypeStruct((LANES, T), F32),
                   jax.ShapeDtypeStruct((nt, 8, LANES), F32)],
        compiler_params=_cparams("arbitrary"),
        name="moe_dispatch",
    )(x, g, w_router, b_router, tri)
    return xd, pos_t, w_t, cnt[:, 0, 0:N_EXPERTS]


def _moe_finish(x, xd, pos_t, w_t, counts, wg, wu, wd, final_g, final_norm):
    T = x.shape[0]
    tm = MOE_TILE
    nt = T // tm
    tps = min(MOE_TILES_PER_STEP, nt)
    assert nt % tps == 0
    nslot = MOE_CAP // 2
    hot = counts > nslot
    busy_blocks = jnp.any(hot.reshape(nt // tps, tps, N_EXPERTS), axis=1).T.astype(I32)
    busy_tiles = jnp.any(hot, axis=1).astype(I32)
    halves = (N_EXPERTS, nt, 2, nslot, D_MODEL)
    yd = pl.pallas_call(
        _expert_kernel,
        grid_spec=pltpu.PrefetchScalarGridSpec(
            num_scalar_prefetch=1,
            grid=(N_EXPERTS, nt // tps),
            in_specs=[pl.BlockSpec((None, tps, 2, nslot, D_MODEL), lambda e, c, b: (e, c, 0, 0, 0)),
                      pl.BlockSpec((None, D_MODEL, EXPERT_FF), lambda e, c, b: (e, 0, 0)),
                      pl.BlockSpec((None, D_MODEL, EXPERT_FF), lambda e, c, b: (e, 0, 0)),
                      pl.BlockSpec((None, EXPERT_FF, D_MODEL), lambda e, c, b: (e, 0, 0))],
            out_specs=pl.BlockSpec((None, tps, 2, nslot, D_MODEL), lambda e, c, b: (e, c, 0, 0, 0))),
        out_shape=jax.ShapeDtypeStruct(halves, BF16),
        compiler_params=_cparams("arbitrary", "arbitrary"),
        name="moe_experts",
    )(busy_blocks, xd.reshape(halves), wg, wu, wd)
    return pl.pallas_call(
        functools.partial(_combine_kernel, final_norm=final_norm),
        grid_spec=pltpu.PrefetchScalarGridSpec(
            num_scalar_prefetch=1,
            grid=(nt,),
            in_specs=[pl.BlockSpec((tm, D_MODEL), lambda i, b: (i, 0)),
                      pl.BlockSpec((N_EXPERTS, None, MOE_CAP, D_MODEL), lambda i, b: (0, i, 0, 0)),
                      pl.BlockSpec((LANES, tm), lambda i, b: (0, i)),
                      pl.BlockSpec((LANES, tm), lambda i, b: (0, i)),
                      pl.BlockSpec((1, D_MODEL), lambda i, b: (0, 0))],
            out_specs=pl.BlockSpec((tm, D_MODEL), lambda i, b: (i, 0))),
        out_shape=jax.ShapeDtypeStruct((T, D_MODEL), F32),
        compiler_params=_cparams("arbitrary"),
        name="moe_combine",
    )(busy_tiles, x, yd.reshape(N_EXPERTS, nt, MOE_CAP, D_MODEL), pos_t, w_t, final_g)


def _block_diag(w):
    nb, bs, _ = w.shape
    out = jnp.zeros((nb * bs, nb * bs), w.dtype)
    for i in range(nb):
        out = out.at[i * bs:(i + 1) * bs, i * bs:(i + 1) * bs].set(w[i])
    return out


def _selection_aggregator_np(n_cmp, n_slc):
    ratio_s = NSA_SLC_LEN // NSA_CMP_STRIDE
    ratio_c = NSA_CMP_LEN // NSA_CMP_STRIDE
    w = np.convolve(np.ones(ratio_s), np.ones(ratio_c)).astype(np.float32)
    cj = np.arange(n_slc)[:, None] * ratio_s - (ratio_c - 1) + np.arange(w.size)[None, :]
    jj = np.broadcast_to(np.arange(n_slc)[:, None], cj.shape)
    ww = np.broadcast_to(w[None, :], cj.shape)
    keep = (cj >= 0) & (cj < n_cmp)
    agg = np.zeros((n_cmp, n_slc), np.float32)
    np.add.at(agg, (cj[keep], jj[keep]), ww[keep])
    return agg


def _row(v):
    return v.reshape(1, -1).astype(F32)


def _diff_lambda_init(layer):
    return 0.8 - 0.6 * math.exp(-0.3 * layer)


def _mixer_layer(x, l, B, S, tiles, p):
    (bias_swa, bias_diff, bias_slc, bias_win, bias_cmp, agg_pad, n_cmp, ncp) = tiles
    w_in = jnp.pad(p["w_in"][l], ((0, 0), (0, N_IN_PAD - p["w_in"].shape[2]))).astype(BF16)
    w_t = jnp.concatenate([w_in[:, 1536:1792], w_in[:, 2176:2304]], axis=1).T
    a_xg, b_qkv, c_qkv, d_all, vals_t = _in_proj(x, _row(p["mix_norm_g"][l]), w_in, w_t)

    y_a = _rglru(a_xg, p["conv_w"][l].astype(F32), _row(p["conv_b"][l]),
                 _block_diag(p["lru_w_a"][l]).astype(BF16), _row(p["lru_b_a"][l]),
                 _block_diag(p["lru_w_x"][l]).astype(BF16), _row(p["lru_b_x"][l]),
                 _row(p["lru_lambda"][l]), B, S)

    b3 = b_qkv.reshape(B, S, 512)
    swa_cfg = tuple(((h // 2) * 64, (h % 2) != (h // 2), h // 2) for h in range(SWA_HEADS))
    y_b = _banded_attention(b3, 0, b3, 2, b3, 3, bias_swa, _row(p["swa_sinks"][l]),
                            B=B, S=S, window=SWA_WINDOW, q_cfg=swa_cfg)

    c3 = c_qkv.reshape(B, S, 768)
    lam4 = jnp.stack([p["diff_lq1"][l], p["diff_lk1"][l], p["diff_lq2"][l], p["diff_lk2"][l]]).astype(F32)
    gsub = jnp.tile(p["diff_subln_g"][l].reshape(-1, 1).astype(F32), (2, 1))
    y_c = _flash_attention("diff", c3, 0, c3, 1, vals_t, 256, 0, bias_diff, (lam4, gsub), B=B, S=S,
                           lambda_init=_diff_lambda_init(l))

    d3 = d_all.reshape(B, S, 768)
    nchunk = S // NSA_CMP_STRIDE
    ck = d3[:, :, 256:320].reshape(B, nchunk, NSA_CMP_STRIDE * HEAD_DIM)
    cv = d3[:, :, 320:384].reshape(B, nchunk, NSA_CMP_STRIDE * HEAD_DIM)

    def pos8(a):
        return jnp.broadcast_to(a.reshape(1, -1), (8, a.size)).astype(BF16)

    w1k = jnp.pad(p["nsa_w1_k"][l], ((0, 0), (0, HEAD_DIM))).astype(BF16)
    w1v = jnp.pad(p["nsa_w1_v"][l], ((0, 0), (HEAD_DIM, 0))).astype(BF16)
    w2 = _block_diag(jnp.stack([p["nsa_w2_k"][l], p["nsa_w2_v"][l]])).astype(BF16)
    kcv = _nsa_compress(ck, cv, pos8(p["nsa_pos_k"][l]), pos8(p["nsa_pos_v"][l]), w1k, w1v, w2, ncp)
    o_c, sel = _nsa_cmp_select(d3, kcv, bias_cmp, agg_pad, B=B, S=S, n_cmp=n_cmp, ncp=ncp)
    o_s = _flash_attention("nsa", d3, 0, d3, 3, vals_t, 128, 2, bias_slc, (sel,), B=B, S=S)
    win_cfg = tuple((0, h % 2 == 1, 1) for h in range(NSA_HEADS))
    o_w = _banded_attention(d3, 0, d3, 4, None, None, bias_win, None, B=B, S=S, window=NSA_WINDOW, q_cfg=win_cfg)

    T = B * S
    return _merge(x, _row(p["mix_norm_g"][l]), y_a, y_b.reshape(T, 256), y_c.reshape(T, 256),
                  o_c.reshape(T, 256), o_s.reshape(T, 256), o_w.reshape(T, 256), d_all,
                  p["w_gate"][l].astype(BF16), p["b_gate"][l].astype(F32), p["w_branch"][l].astype(BF16),
                  p["w_out"][l].astype(BF16))


def _moe_layer(x, l, p, final_norm):
    w_router = jnp.concatenate([p["w_router_exp"][l], p["w_router_grp"][l]], axis=1)
    w_router = jnp.pad(w_router, ((0, 0), (0, LANES - w_router.shape[1]))).astype(F32)
    w_hi = w_router.astype(BF16)
    w_router = jnp.concatenate([w_hi, (w_router - w_hi.astype(F32)).astype(BF16)], axis=1)
    b_router = jnp.concatenate([p["b_router_exp"][l], p["b_router_grp"][l]])
    b_router = jnp.pad(b_router, (0, LANES - b_router.shape[0])).reshape(1, LANES).astype(F32)
    g = _row(p["ffn_norm_g"][l])
    fg = _row(p["final_norm_g"])
    wg, wu, wd = (p[k][l].astype(BF16) for k in ("w_exp_gate", "w_exp_up", "w_exp_down"))
    xd, pos_t, w_t, counts = _moe_dispatch(x, g, w_router, b_router)
    return lax.cond(jnp.max(counts) > MOE_CAP,
                    lambda: _moe(x, g, w_router, b_router, wg, wu, wd, fg, final_norm),
                    lambda: _moe_finish(x, xd, pos_t, w_t, counts, wg, wu, wd, fg, final_norm))


def _forward(p):
    x = p["x"]
    B, S, D = x.shape
    depth = p["w_in"].shape[0]
    assert D == D_MODEL and S % (16 * Q_BLOCK) == 0 and S // NSA_SLC_LEN <= LANES
    rel = p["rel_bias"].astype(F32)
    n_cmp = (S - NSA_CMP_LEN) // NSA_CMP_STRIDE + 1
    ncp = LANES + S // NSA_CMP_STRIDE
    agg = _selection_aggregator_np(n_cmp, S // NSA_SLC_LEN)
    agg_pad = np.zeros((ncp, LANES), np.float32)
    agg_pad[LANES:LANES + n_cmp, :agg.shape[1]] = agg
    tiles = (
        _bias_tiles(rel, _banded_buckets(1), 0, SWA_HEADS, False),
        _bias_tiles(rel, _flash_buckets(), SWA_HEADS, DIFF_HEADS, True, LOG2E),
        _bias_tiles(rel, _flash_buckets(), SWA_HEADS + DIFF_HEADS, NSA_HEADS, True, LOG2E),
        _bias_tiles(rel, _banded_buckets(NSA_WINDOW // Q_BLOCK), SWA_HEADS + DIFF_HEADS, NSA_HEADS, False),
        _bias_tiles(rel, _cmp_buckets(), SWA_HEADS + DIFF_HEADS, NSA_HEADS, True),
        jnp.asarray(agg_pad, BF16), n_cmp, ncp,
    )
    xt = x.reshape(B * S, D).astype(F32)
    for l in range(depth):
        xt = _mixer_layer(xt, l, B, S, tiles, p)
        xt = _moe_layer(xt, l, p, final_norm=(l == depth - 1))
    return xt.reshape(B, S, D).astype(x.dtype)


def kernel(x, rel_bias, final_norm_g, mix_norm_g, w_in, conv_w, conv_b, lru_w_a, lru_b_a, lru_w_x, lru_b_x,
           lru_lambda, swa_sinks, diff_lq1, diff_lk1, diff_lq2, diff_lk2, diff_subln_g, nsa_pos_k, nsa_w1_k,
           nsa_w2_k, nsa_pos_v, nsa_w1_v, nsa_w2_v, w_gate, b_gate, w_branch, w_out, ffn_norm_g, w_router_grp,
           b_router_grp, w_router_exp, b_router_exp, w_exp_gate, w_exp_up, w_exp_down):
    return _forward(dict(
        x=x, rel_bias=rel_bias, final_norm_g=final_norm_g, mix_norm_g=mix_norm_g, w_in=w_in, conv_w=conv_w,
        conv_b=conv_b, lru_w_a=lru_w_a, lru_b_a=lru_b_a, lru_w_x=lru_w_x, lru_b_x=lru_b_x, lru_lambda=lru_lambda,
        swa_sinks=swa_sinks, diff_lq1=diff_lq1, diff_lk1=diff_lk1, diff_lq2=diff_lq2, diff_lk2=diff_lk2,
        diff_subln_g=diff_subln_g, nsa_pos_k=nsa_pos_k, nsa_w1_k=nsa_w1_k, nsa_w2_k=nsa_w2_k, nsa_pos_v=nsa_pos_v,
        nsa_w1_v=nsa_w1_v, nsa_w2_v=nsa_w2_v, w_gate=w_gate, b_gate=b_gate, w_branch=w_branch, w_out=w_out,
        ffn_norm_g=ffn_norm_g, w_router_grp=w_router_grp, b_router_grp=b_router_grp, w_router_exp=w_router_exp,
        b_router_exp=b_router_exp, w_exp_gate=w_exp_gate, w_exp_up=w_exp_up, w_exp_down=w_exp_down))
```

```python
import functools
import math

import numpy as np
import jax
import jax.numpy as jnp
from jax import lax
from jax.experimental import pallas as pl
from jax.experimental.pallas import tpu as pltpu

F32 = jnp.float32
BF16 = jnp.bfloat16
I32 = jnp.int32

D_MODEL = 1024
HEAD_DIM = 64
MIX_WIDTH = 256
LRU_C = 8.0
CONV_WIDTH = 4
SWA_HEADS = 4
SWA_WINDOW = 128
DIFF_HEADS = 4
DIFF_DIM = 32
DIFF_SUBLN_EPS = 1e-5
NSA_HEADS = 4
NSA_CMP_LEN = 32
NSA_CMP_STRIDE = 16
NSA_SLC_LEN = 64
NSA_TOPK = 16
NSA_WINDOW = 512
NSA_FORCED_SCORE = 1e4
REL_BUCKETS = 32
REL_MAX_DIST = 128
N_GROUPS = 4
EXPERTS_PER_GROUP = 8
N_EXPERTS = 32
EXPERT_FF = 256
NORM_EPS = 1e-6
NEG_INF = -1e30
TINY = 1e-30
LOG2E = math.log2(math.e)

LANES = 128
Q_BLOCK = 128
BANDED_Q_BLOCKS = 2
FLASH_BLOCK = 256
MOE_TILE = 512
MOE_CAP = 128
MOE_TILES_PER_STEP = 8
FLASH_MAX_GROWTH = 64.0
FLASH_FAR_BLOCKS = 2
N_IN_PAD = 2560
VMEM_LIMIT = 56 * 1024 * 1024

_NT = (((1,), (1,)), ((), ()))


def _cparams(*sem):
    return pltpu.CompilerParams(dimension_semantics=sem, vmem_limit_bytes=VMEM_LIMIT)


def _t5_bucket_np(dist):
    n = np.maximum(dist, 0)
    exact = REL_BUCKETS // 2
    scaled = (np.log(np.maximum(n, 1).astype(np.float32) / np.float32(exact))
              / np.float32(math.log(REL_MAX_DIST / exact))).astype(np.float32)
    large = np.minimum(exact + (scaled * np.float32(REL_BUCKETS - exact)).astype(np.int32), REL_BUCKETS - 1)
    return np.where(n < exact, n, large).astype(np.int32)


def _bias_tile_kernel(tab_ref, idx_ref, o_ref, *, head0, shift, scale):
    h = pl.program_id(0) + head0
    idx = idx_ref[0]
    acc = jnp.zeros(idx.shape, F32)
    for b in range(REL_BUCKETS):
        acc = jnp.where(idx == b, tab_ref[b, h], acc)
    if shift:
        acc = acc - tab_ref[REL_BUCKETS - 1, h]
    if scale != 1.0:
        acc = acc * scale
    o_ref[0, 0] = acc


def _bias_tiles(rel_bias, buckets, head0, nheads, shift, scale=1.0):
    R, M, N = buckets.shape
    return pl.pallas_call(
        functools.partial(_bias_tile_kernel, head0=head0, shift=shift, scale=scale),
        grid=(nheads, R),
        in_specs=[pl.BlockSpec(memory_space=pltpu.SMEM),
                  pl.BlockSpec((1, M, N), lambda h, r: (r, 0, 0))],
        out_specs=pl.BlockSpec((1, 1, M, N), lambda h, r: (h, r, 0, 0)),
        out_shape=jax.ShapeDtypeStruct((nheads, R, M, N), F32),
        compiler_params=_cparams("arbitrary", "arbitrary"),
        name="bias_tiles",
    )(rel_bias, jnp.asarray(buckets))


def _banded_buckets(nback):
    i = np.arange(BANDED_Q_BLOCKS * Q_BLOCK)[None, :]
    j = np.arange((nback + BANDED_Q_BLOCKS) * Q_BLOCK)[:, None]
    return _t5_bucket_np(i - j + nback * Q_BLOCK)[None]


def _flash_buckets():
    k = np.arange(FLASH_BLOCK)[:, None]
    q = np.arange(FLASH_BLOCK)[None, :]
    return np.stack([_t5_bucket_np(q - k), _t5_bucket_np(q - k + FLASH_BLOCK)])


def _cmp_buckets():
    i = np.arange(Q_BLOCK)[None, :, None]
    c2 = np.arange(2 * LANES)[None, None, :]
    r = np.arange(16)[:, None, None]
    dist = i - NSA_CMP_STRIDE * (c2 - LANES - 8 * r) - (NSA_CMP_LEN - 1)
    return _t5_bucket_np(dist)


def _rms(x, g):
    return x * lax.rsqrt(jnp.mean(x * x, axis=-1, keepdims=True) + NORM_EPS) * g


def _proj_kernel(x_ref, g_ref, w_ref, wt_ref, oa_ref, ob_ref, oc_ref, od_ref, ot_ref):
    h = _rms(x_ref[...], g_ref[...]).astype(BF16)
    oa_ref[...] = jnp.dot(h, w_ref[:, 0:512], preferred_element_type=F32)
    ob_ref[...] = jnp.dot(h, w_ref[:, 512:1024], preferred_element_type=F32).astype(BF16)
    oc_ref[...] = jnp.dot(h, w_ref[:, 1024:1792], preferred_element_type=F32).astype(BF16)
    od_ref[...] = jnp.dot(h, w_ref[:, 1792:2560], preferred_element_type=F32).astype(BF16)
    ot_ref[...] = lax.dot_general(wt_ref[...], h, _NT, preferred_element_type=F32).astype(BF16)


def _in_proj(x, g, w, wt):
    T = x.shape[0]
    tm = 512
    nt = wt.shape[0]
    return pl.pallas_call(
        _proj_kernel,
        grid=(T // tm,),
        in_specs=[pl.BlockSpec((tm, D_MODEL), lambda i: (i, 0)),
                  pl.BlockSpec((1, D_MODEL), lambda i: (0, 0)),
                  pl.BlockSpec((D_MODEL, N_IN_PAD), lambda i: (0, 0)),
                  pl.BlockSpec((nt, D_MODEL), lambda i: (0, 0))],
        out_specs=[pl.BlockSpec((tm, 512), lambda i: (i, 0)),
                   pl.BlockSpec((tm, 512), lambda i: (i, 0)),
                   pl.BlockSpec((tm, 768), lambda i: (i, 0)),
                   pl.BlockSpec((tm, 768), lambda i: (i, 0)),
                   pl.BlockSpec((nt, tm), lambda i: (0, i))],
        out_shape=[jax.ShapeDtypeStruct((T, 512), F32),
                   jax.ShapeDtypeStruct((T, 512), BF16),
                   jax.ShapeDtypeStruct((T, 768), BF16),
                   jax.ShapeDtypeStruct((T, 768), BF16),
                   jax.ShapeDtypeStruct((nt, T), BF16)],
        compiler_params=_cparams("arbitrary"),
        name="in_proj",
    )(x, g, w, wt)


def _lru_kernel(xg_ref, cw_ref, cb_ref, wa_ref, ba_ref, wx_ref, bx_ref, lam_ref, o_ref,
                ext_ref, h_ref, *, Tt):
    t = pl.program_id(1)

    W = MIX_WIDTH

    @pl.when(t == 0)
    def _():
        ext_ref[0:8, :] = jnp.zeros((8, W), F32)
        h_ref[...] = jnp.zeros_like(h_ref)

    x = xg_ref[:, 0:W]
    gate = xg_ref[:, W:2 * W]
    ext_ref[8:Tt + 8, :] = x
    row = lax.broadcasted_iota(I32, (Tt, W), 0)
    xc = cb_ref[...] + x * cw_ref[CONV_WIDTH - 1:CONV_WIDTH, :]
    for s in range(1, CONV_WIDTH):
        xc = xc + ext_ref[8 - s:8 - s + Tt, :] * cw_ref[CONV_WIDTH - 1 - s:CONV_WIDTH - s, :]
    ext_ref[0:8, :] = ext_ref[Tt:Tt + 8, :]

    xcb = xc.astype(BF16)
    r = jax.nn.sigmoid(jnp.dot(xcb, wa_ref[...], preferred_element_type=F32) + ba_ref[...])
    ig = jax.nn.sigmoid(jnp.dot(xcb, wx_ref[...], preferred_element_type=F32) + bx_ref[...])
    z = -lam_ref[...]
    softplus = jnp.maximum(z, 0.0) + jnp.log1p(jnp.exp(-jnp.abs(z)))
    log_a = (-LRU_C) * r * softplus
    a = jnp.exp(log_a)
    mult = jnp.sqrt(1.0 - a * a)
    mult = jnp.where((row + t * Tt) == 0, 1.0, mult)
    b = mult * ig * xc

    d = 1
    while d < Tt:
        keep = row >= d
        b = jnp.where(keep, a * pltpu.roll(b, d, 0) + b, b)
        a = jnp.where(keep, a * pltpu.roll(a, d, 0), a)
        d *= 2
    hs = a * h_ref[7:8, :] + b
    h_ref[...] = hs[Tt - 8:Tt]
    o_ref[...] = (hs * jax.nn.gelu(gate)).astype(o_ref.dtype)


def _rglru(a_xg, conv_w, conv_b, wa_bd, b_a, wx_bd, b_x, lam, B, S):
    Tt = 256
    W = MIX_WIDTH
    vec = pl.BlockSpec((1, W), lambda b, t: (0, 0))
    mat = pl.BlockSpec((W, W), lambda b, t: (0, 0))
    out = pl.pallas_call(
        functools.partial(_lru_kernel, Tt=Tt),
        grid=(B, S // Tt),
        in_specs=[pl.BlockSpec((None, Tt, 2 * W), lambda b, t: (b, t, 0)),
                  pl.BlockSpec((CONV_WIDTH, W), lambda b, t: (0, 0)),
                  vec, mat, vec, mat, vec, vec],
        out_specs=pl.BlockSpec((None, Tt, W), lambda b, t: (b, t, 0)),
        out_shape=jax.ShapeDtypeStruct((B, S, W), BF16),
        scratch_shapes=[pltpu.VMEM((Tt + 8, W), F32), pltpu.VMEM((8, W), F32)],
        compiler_params=_cparams("arbitrary", "arbitrary"),
        name="rglru",
    )(a_xg.reshape(B, S, 2 * W), conv_w, conv_b, wa_bd, b_a, wx_bd, b_x, lam)
    return out.reshape(B * S, W)


def _lane_window(q128, lo, width, roll, scale):
    q = q128 * scale
    if roll:
        q = pltpu.roll(q, LANES // 2, 1)
    lane = lax.broadcasted_iota(I32, q.shape, 1)
    return jnp.where((lane >= lo) & (lane < lo + width), q, 0.0).astype(BF16)


def _pair_halves(even, odd, even_half, odd_half):
    if even_half == 1:
        even = pltpu.roll(even, LANES // 2, 1)
    if odd_half == 0:
        odd = pltpu.roll(odd, LANES // 2, 1)
    lane = lax.broadcasted_iota(I32, even.shape, 1)
    return jnp.where(lane < LANES // 2, even, odd)


def _banded_kernel(*refs, nback, window, q_cfg, has_sink, shared_kv):
    nk = nback + BANDED_Q_BLOCKS
    TQ = BANDED_Q_BLOCKS * Q_BLOCK
    q_ref = refs[0]
    k_refs = refs[1:1 + nk]
    pos = 1 + nk
    if shared_kv:
        v_refs = k_refs
    else:
        v_refs = refs[pos:pos + nk]
        pos += nk
    bias_ref = refs[pos]
    pos += 1
    sink_ref = None
    if has_sink:
        sink_ref = refs[pos]
        pos += 1
    o_ref = refs[pos]

    n = pl.program_id(1)
    KW = nk * Q_BLOCK
    kcat = jnp.concatenate([k_refs[nk - 1 - i][...] for i in range(nk)], axis=0)
    vcat = kcat if shared_kv else jnp.concatenate([v_refs[nk - 1 - i][...] for i in range(nk)], axis=0)
    vt = vcat.astype(F32).T.astype(BF16)
    jj = lax.broadcasted_iota(I32, (KW, TQ), 0)
    ii = lax.broadcasted_iota(I32, (KW, TQ), 1)
    dist = ii - jj + nback * Q_BLOCK
    valid = (dist >= 0) & (dist < window) & (jj >= (nback - BANDED_Q_BLOCKS * n) * Q_BLOCK)
    scale = HEAD_DIM ** -0.5

    def logits(h):
        lo, roll, _ = q_cfg[h]
        g = h // 2
        qx = _lane_window(q_ref[:, LANES * g:LANES * (g + 1)].astype(F32), lo, HEAD_DIM, roll, scale)
        s = lax.dot_general(kcat, qx, _NT, preferred_element_type=F32) + bias_ref[h, 0]
        return jnp.where(valid, s, NEG_INF)

    def softmax(h, s):
        m = jnp.max(s, axis=0, keepdims=True)
        if has_sink:
            sink = sink_ref[0:1, h:h + 1]
            m = jnp.maximum(m, sink)
        p = jnp.exp(s - m)
        den = jnp.sum(p, axis=0, keepdims=True)
        if has_sink:
            den = den + jnp.exp(sink - m)
        return p.astype(BF16), 1.0 / jnp.maximum(den, TINY)

    def values(h, p, inv):
        half = q_cfg[h][2]
        return jnp.dot(vt[HEAD_DIM * half:HEAD_DIM * (half + 1)], p, preferred_element_type=F32) * inv

    nh = len(q_cfg)
    ss = [logits(h) for h in range(nh)]
    pr = [softmax(h, ss[h]) for h in range(nh)]
    outs = [values(h, *pr[h]) for h in range(nh)]
    for g in range(nh // 2):
        pair = jnp.concatenate([outs[2 * g], outs[2 * g + 1]], axis=0)
        o_ref[:, LANES * g:LANES * (g + 1)] = pair.T.astype(o_ref.dtype)


def _banded_attention(qsrc, q_col, ksrc, k_col, vsrc, v_col, bias, sinks, *, B, S, window, q_cfg):
    nback = -(-window // Q_BLOCK)
    nq = BANDED_Q_BLOCKS
    nk = nback + nq
    TQ = nq * Q_BLOCK
    shared_kv = vsrc is None
    has_sink = sinks is not None

    def kv_spec(col, i):
        return pl.BlockSpec((None, Q_BLOCK, LANES), lambda b, n: (b, jnp.maximum(nq * n + nq - 1 - i, 0), col))

    in_specs = [pl.BlockSpec((None, TQ, 2 * LANES), lambda b, n: (b, n, q_col))]
    args = [qsrc]
    in_specs += [kv_spec(k_col, i) for i in range(nk)]
    args += [ksrc] * nk
    if not shared_kv:
        in_specs += [kv_spec(v_col, i) for i in range(nk)]
        args += [vsrc] * nk
    in_specs.append(pl.BlockSpec(bias.shape, lambda b, n: (0, 0, 0, 0)))
    args.append(bias)
    if has_sink:
        in_specs.append(pl.BlockSpec(sinks.shape, lambda b, n: (0, 0)))
        args.append(sinks)
    return pl.pallas_call(
        functools.partial(_banded_kernel, nback=nback, window=window, q_cfg=q_cfg,
                          has_sink=has_sink, shared_kv=shared_kv),
        grid=(B, S // TQ),
        in_specs=in_specs,
        out_specs=pl.BlockSpec((None, TQ, 2 * LANES), lambda b, n: (b, n, 0)),
        out_shape=jax.ShapeDtypeStruct((B, S, 2 * LANES), BF16),
        compiler_params=_cparams("arbitrary", "arbitrary"),
        name="banded_attn",
    )(*args)


def _flash_kernel(*refs, mode, lambda_init):
    TB = FLASH_BLOCK
    if mode == "diff":
        (q_ref, k_ref, vt_ref, bias_ref, lam_ref, g_ref, o_ref,
         qx_ref, m_ref, l_ref, acc_ref) = refs
        sel_ref = None
        maps = [dict(g=h // 2, lo=(h % 2) * 64 + 32 * mm, w=DIFF_DIM, roll=False, kg=h // 2, vrow=HEAD_DIM * h, h=h)
                for h in range(DIFF_HEADS) for mm in range(2)]
        scale = DIFF_DIM ** -0.5
    else:
        (q_ref, k_ref, vt_ref, sel_ref, bias_ref, o_ref,
         qx_ref, m_ref, l_ref, acc_ref) = refs
        maps = [dict(g=h // 2, lo=0, w=HEAD_DIM, roll=(h % 2 == 1), kg=0, vrow=HEAD_DIM, h=h)
                for h in range(NSA_HEADS)]
        scale = HEAD_DIM ** -0.5
    n = pl.program_id(1)

    for mp, c in enumerate(maps):
        qx_ref[mp] = _lane_window(q_ref[:, LANES * c["g"]:LANES * (c["g"] + 1)].astype(F32),
                                  c["lo"], c["w"], c["roll"], scale * LOG2E)
    m_ref[...] = jnp.full(m_ref.shape, NEG_INF, F32)
    l_ref[...] = jnp.zeros_like(l_ref)
    acc_ref[...] = jnp.zeros_like(acc_ref)

    key = lax.broadcasted_iota(I32, (TB, TB), 0)
    qry = lax.broadcasted_iota(I32, (TB, TB), 1)

    def step(j, kind, TK=TB):
        koff = pl.multiple_of(j * TB, TB)
        valid = None
        if kind == "diag":
            valid = qry >= key
        if sel_ref is not None:
            per = TK // NSA_SLC_LEN
            base = j * (TB // NSA_SLC_LEN)
            rows = [jnp.broadcast_to(sel_ref[pl.ds(base + i, 1), :], (NSA_SLC_LEN, TB)) for i in range(per)]
            picked = jnp.concatenate(rows, axis=0) > 0.5
            valid = picked if valid is None else (valid & picked)

        def logits(mp):
            c = maps[mp]
            kt = k_ref[pl.ds(koff, TK), LANES * c["kg"]:LANES * (c["kg"] + 1)]
            s = lax.dot_general(kt, qx_ref[mp], _NT, preferred_element_type=F32)
            if kind == "near":
                s = s + bias_ref[c["h"], 1]
            elif kind == "diag":
                s = s + bias_ref[c["h"], 0]
            if valid is not None:
                s = jnp.where(valid, s, NEG_INF)
            return s

        def softmax(mp, s):
            m_prev = m_ref[mp]
            m_new = jnp.maximum(m_prev, jnp.max(s, axis=0, keepdims=True))
            alpha = jnp.exp2(m_prev - m_new)
            p = jnp.exp2(s - m_new)
            l_ref[mp] = alpha * l_ref[mp] + jnp.sum(p, axis=0, keepdims=True)
            m_ref[mp] = m_new
            return p.astype(BF16), alpha

        def values(mp, p, alpha):
            c = maps[mp]
            vt = vt_ref[c["vrow"]:c["vrow"] + HEAD_DIM, pl.ds(koff, TK)]
            acc_ref[mp] = alpha * acc_ref[mp] + jnp.dot(vt, p, preferred_element_type=F32)

        nm = len(maps)
        ss = [logits(mp) for mp in range(nm)]

        def two_pass():
            pa = [softmax(mp, ss[mp]) for mp in range(nm)]
            for mp in range(nm):
                values(mp, *pa[mp])

        if kind != "far":
            two_pass()
            return

        stabs = [m_ref[mp] for mp in range(nm)]
        ps = [jnp.exp2(ss[mp] - stabs[mp]) for mp in range(nm)]
        m_news = [jnp.maximum(stabs[mp], jnp.max(ss[mp], axis=0, keepdims=True)) for mp in range(nm)]
        growth = m_news[0] - stabs[0]
        for mp in range(1, nm):
            growth = jnp.maximum(growth, m_news[mp] - stabs[mp])
        safe = jnp.max(growth) <= FLASH_MAX_GROWTH
        psums = [jnp.sum(ps[mp], axis=0, keepdims=True) for mp in range(nm)]
        pvs = [jnp.dot(vt_ref[maps[mp]["vrow"]:maps[mp]["vrow"] + HEAD_DIM, pl.ds(koff, TK)], ps[mp].astype(BF16),
                       preferred_element_type=F32) for mp in range(nm)]

        @pl.when(safe)
        def _():
            for mp in range(nm):
                alpha = jnp.exp2(stabs[mp] - m_news[mp])
                l_ref[mp] = (l_ref[mp] + psums[mp]) * alpha
                acc_ref[mp] = (acc_ref[mp] + pvs[mp]) * alpha
                m_ref[mp] = m_news[mp]

        @pl.when(jnp.logical_not(safe))
        def _():
            two_pass()

    step(n, "diag")

    @pl.when(n >= 1)
    def _():
        step(n - 1, "near")

    nfar = jnp.maximum(n - 1, 0)
    FW = FLASH_FAR_BLOCKS

    def far_body(j, carry):
        step(FW * j, "far", FW * TB)
        return carry

    lax.fori_loop(0, nfar // FW, far_body, 0)

    def rest_body(j, carry):
        step(j, "far")
        return carry

    lax.fori_loop((nfar // FW) * FW, nfar, rest_body, 0)

    def normalised(mp):
        return acc_ref[mp] * (1.0 / jnp.maximum(l_ref[mp], TINY))

    if mode == "diff":
        lq = lam_ref[...]
        lam = (jnp.exp(jnp.sum(lq[0:1] * lq[1:2], axis=1, keepdims=True))
               - jnp.exp(jnp.sum(lq[2:3] * lq[3:4], axis=1, keepdims=True)) + lambda_init)
        for g in range(DIFF_HEADS // 2):
            res = []
            for h in (2 * g, 2 * g + 1):
                o = normalised(2 * h) - lam * normalised(2 * h + 1)
                ms = jnp.sum(o * o, axis=0, keepdims=True) * (1.0 / (2 * DIFF_DIM))
                res.append(o * lax.rsqrt(ms + DIFF_SUBLN_EPS))
            out = jnp.concatenate(res, axis=0) * g_ref[...] * (1.0 - lambda_init)
            o_ref[:, LANES * g:LANES * (g + 1)] = out.T.astype(o_ref.dtype)
    else:
        for g in range(NSA_HEADS // 2):
            out = jnp.concatenate([normalised(2 * g), normalised(2 * g + 1)], axis=0)
            o_ref[:, LANES * g:LANES * (g + 1)] = out.T.astype(o_ref.dtype)


def _flash_attention(mode, qsrc, q_col, ksrc, k_col, vt, vt_rows, vt_blk, bias, extra, *, B, S, lambda_init=0.0):
    TB = FLASH_BLOCK
    q_spec = pl.BlockSpec((None, TB, 2 * LANES), lambda b, n: (b, n, q_col))
    bias_spec = pl.BlockSpec(bias.shape, lambda b, n: (0, 0, 0, 0))
    vt_spec = pl.BlockSpec((vt_rows, S), lambda b, n: (vt_blk, b))
    if mode == "diff":
        lam4, gsub = extra
        nmaps = 2 * DIFF_HEADS
        in_specs = [q_spec,
                    pl.BlockSpec((None, S, 2 * LANES), lambda b, n: (b, 0, k_col)),
                    vt_spec, bias_spec,
                    pl.BlockSpec(lam4.shape, lambda b, n: (0, 0)),
                    pl.BlockSpec(gsub.shape, lambda b, n: (0, 0))]
        args = [qsrc, ksrc, vt, bias, lam4, gsub]
    else:
        (sel,) = extra
        nmaps = NSA_HEADS
        in_specs = [q_spec,
                    pl.BlockSpec((None, S, LANES), lambda b, n: (b, 0, k_col)),
                    vt_spec,
                    pl.BlockSpec((None, LANES, TB), lambda b, n: (b, 0, n)),
                    bias_spec]
        args = [qsrc, ksrc, vt, sel, bias]
    return pl.pallas_call(
        functools.partial(_flash_kernel, mode=mode, lambda_init=lambda_init),
        grid=(B, S // TB),
        in_specs=in_specs,
        out_specs=pl.BlockSpec((None, TB, 2 * LANES), lambda b, n: (b, n, 0)),
        out_shape=jax.ShapeDtypeStruct((B, S, 2 * LANES), BF16),
        scratch_shapes=[pltpu.VMEM((nmaps, TB, LANES), BF16),
                        pltpu.VMEM((nmaps, 1, TB), F32),
                        pltpu.VMEM((nmaps, 1, TB), F32),
                        pltpu.VMEM((nmaps, HEAD_DIM, TB), F32)],
        compiler_params=_cparams("arbitrary", "arbitrary"),
        name="flash_" + mode,
    )(*args)


def _compress_kernel(ck_ref, cv_ref, pk_ref, pv_ref, w1k_ref, w1v_ref, w2_ref, o_ref):
    half = NSA_CMP_STRIDE * HEAD_DIM
    ck = ck_ref[...]
    cv = cv_ref[...]
    nrow = ck.shape[0]

    def mm(a, w):
        return jnp.dot(a, w, preferred_element_type=F32)

    top = mm(ck, w1k_ref[0:half, :]) + mm(cv, w1v_ref[0:half, :])
    bot = mm(ck, w1k_ref[half:2 * half, :]) + mm(cv, w1v_ref[half:2 * half, :])
    posw = (mm(pk_ref[...], w1k_ref[...]) + mm(pv_ref[...], w1v_ref[...]))[0:1]
    pre = top + pltpu.roll(bot, nrow - 1, 0) + posw
    kcv = mm(jax.nn.gelu(pre).astype(BF16), w2_ref[...])
    o_ref[0:LANES, :] = jnp.zeros((LANES, LANES), o_ref.dtype)
    o_ref[LANES:LANES + nrow, :] = kcv.astype(o_ref.dtype)


def _nsa_compress(ck, cv, pk, pv, w1k, w1v, w2, ncp):
    B, nchunk, width = ck.shape
    assert ncp == LANES + nchunk

    def full(a):
        return pl.BlockSpec(a.shape, lambda b: (0,) * a.ndim)

    chunk = pl.BlockSpec((None, nchunk, width), lambda b: (b, 0, 0))
    return pl.pallas_call(
        _compress_kernel,
        grid=(B,),
        in_specs=[chunk, chunk, full(pk), full(pv), full(w1k), full(w1v), full(w2)],
        out_specs=pl.BlockSpec((None, ncp, LANES), lambda b: (b, 0, 0)),
        out_shape=jax.ShapeDtypeStruct((B, ncp, LANES), BF16),
        compiler_params=_cparams("arbitrary"),
        name="nsa_compress",
    )(ck, cv, pk, pv, w1k, w1v, w2)


def _cmp_kernel(q_ref, kcv_ref, bias_ref, agg_ref, oc_ref, sel_ref, s_scr, *, n_cmp, ncp):
    n = pl.program_id(1)
    QB = Q_BLOCK
    kcv = kcv_ref[...]
    cp = lax.broadcasted_iota(I32, (QB, ncp), 1)
    qpos = n * QB + lax.broadcasted_iota(I32, (QB, ncp), 0)
    valid = ((cp >= LANES) & (cp < LANES + n_cmp)
             & (qpos - NSA_CMP_STRIDE * (cp - LANES) - (NSA_CMP_LEN - 1) >= 0))
    woff = pl.multiple_of((n // 16) * LANES, LANES)
    scale = HEAD_DIM ** -0.5

    def logits(h):
        g = h // 2
        qx = _lane_window(q_ref[:, LANES * g:LANES * (g + 1)].astype(F32), 0, HEAD_DIM, h % 2 == 1, scale)
        s_scr[h] = lax.dot_general(qx, kcv, _NT, preferred_element_type=F32)
        s_scr[h, :, pl.ds(woff, 2 * LANES)] += bias_ref[h]

    def softmax(h):
        s = jnp.where(valid, s_scr[h], NEG_INF)
        m = jnp.max(s, axis=1, keepdims=True)
        p = jnp.where(valid, jnp.exp(s - m), 0.0)
        den = jnp.sum(p, axis=1, keepdims=True)
        return p * (1.0 / jnp.maximum(den, TINY))

    for h in range(NSA_HEADS):
        logits(h)
    ps = [softmax(h) for h in range(NSA_HEADS)]
    outs = [jnp.dot(p.astype(BF16), kcv, preferred_element_type=F32) for p in ps]
    for g in range(NSA_HEADS // 2):
        oc_ref[:, LANES * g:LANES * (g + 1)] = _pair_halves(outs[2 * g], outs[2 * g + 1], 1, 1).astype(oc_ref.dtype)

    psum = (ps[0] + ps[1]) + (ps[2] + ps[3])
    imp = jnp.dot(psum.astype(BF16), agg_ref[...], preferred_element_type=F32)

    v = imp.T
    jb = lax.broadcasted_iota(I32, v.shape, 0)
    qp = n * QB + lax.broadcasted_iota(I32, v.shape, 1)
    cur = qp >> 6
    forced = (jb == 0) | (jb == cur) | (jb == cur - 1)
    v = jnp.where(forced, -jnp.inf, jnp.where(jb * NSA_SLC_LEN <= qp, v, -1.0))
    rid = jb.astype(F32)
    picked = jnp.where(forced, 1.0, 0.0)
    for _ in range(NSA_TOPK - 3):
        mx = jnp.max(v, axis=0, keepdims=True)
        first = jnp.min(jnp.where(v == mx, rid, float(LANES)), axis=0, keepdims=True)
        hit = rid == first
        picked = jnp.where(hit, 1.0, picked)
        v = jnp.where(hit, -jnp.inf, v)
    sel_ref[...] = picked


def _nsa_cmp_select(dsrc, kcv, bias, agg, *, B, S, n_cmp, ncp):
    return pl.pallas_call(
        functools.partial(_cmp_kernel, n_cmp=n_cmp, ncp=ncp),
        grid=(B, S // Q_BLOCK),
        in_specs=[pl.BlockSpec((None, Q_BLOCK, 2 * LANES), lambda b, n: (b, n, 0)),
                  pl.BlockSpec((None, ncp, LANES), lambda b, n: (b, 0, 0)),
                  pl.BlockSpec((NSA_HEADS, None, Q_BLOCK, 2 * LANES), lambda b, n: (0, n % 16, 0, 0)),
                  pl.BlockSpec((ncp, LANES), lambda b, n: (0, 0))],
        out_specs=[pl.BlockSpec((None, Q_BLOCK, 2 * LANES), lambda b, n: (b, n, 0)),
                   pl.BlockSpec((None, LANES, Q_BLOCK), lambda b, n: (b, 0, n))],
        out_shape=[jax.ShapeDtypeStruct((B, S, 2 * LANES), BF16),
                   jax.ShapeDtypeStruct((B, LANES, S), F32)],
        scratch_shapes=[pltpu.VMEM((NSA_HEADS, Q_BLOCK, ncp), F32)],
        compiler_params=_cparams("arbitrary", "arbitrary"),
        name="nsa_cmp_select",
    )(dsrc, kcv, bias, agg)


def _merge_kernel(x_ref, g_ref, ya_ref, yb_ref, yc_ref, oc_ref, os_ref, ow_ref, dg_ref,
                  wg_ref, bg_ref, wb_ref, wo_ref, o_ref):
    x = x_ref[...]
    h = _rms(x, g_ref[...]).astype(BF16)
    tm = x.shape[0]
    lane = lax.broadcasted_iota(I32, (tm, LANES), 1)
    gd = jax.nn.sigmoid(dg_ref[...].astype(F32))
    yd = []
    for g in range(NSA_HEADS // 2):
        acc = jnp.zeros((tm, LANES), F32)
        for br, src in enumerate((oc_ref, os_ref, ow_ref)):
            ge = gd[:, 3 * (2 * g) + br:3 * (2 * g) + br + 1]
            go = gd[:, 3 * (2 * g + 1) + br:3 * (2 * g + 1) + br + 1]
            acc = acc + jnp.where(lane < 64, ge, go) * src[:, LANES * g:LANES * (g + 1)].astype(F32)
        yd.append(acc)
    y_d = jnp.concatenate(yd, axis=1).astype(BF16)
    ys = (ya_ref[...], yb_ref[...], yc_ref[...], y_d)
    merged = jnp.zeros((tm, D_MODEL), F32)
    for br in range(4):
        gate = jax.nn.sigmoid(jnp.dot(h, wg_ref[br], preferred_element_type=F32) + bg_ref[br:br + 1, :])
        merged = merged + gate * jnp.dot(ys[br], wb_ref[br], preferred_element_type=F32)
    o_ref[...] = x + jnp.dot(merged.astype(BF16), wo_ref[...], preferred_element_type=F32)


def _merge(x, g, ya, yb, yc, oc, osel, ow, dproj, wg, bg, wb, wo):
    T = x.shape[0]
    tm = 256

    def tile(width, col=0):
        return pl.BlockSpec((tm, width), lambda i: (i, col))

    def full(a):
        return pl.BlockSpec(a.shape, lambda i: (0,) * a.ndim)

    return pl.pallas_call(
        _merge_kernel,
        grid=(T // tm,),
        in_specs=[tile(D_MODEL), full(g), tile(256), tile(256), tile(256), tile(256), tile(256), tile(256),
                  tile(LANES, 5), full(wg), full(bg), full(wb), full(wo)],
        out_specs=tile(D_MODEL),
        out_shape=jax.ShapeDtypeStruct((T, D_MODEL), F32),
        compiler_params=_cparams("arbitrary"),
        name="merge",
    )(x, g, ya, yb, yc, oc, osel, ow, dproj, wg, bg, wb, wo)


def _route(hn, w_split, b_router):
    lane = lax.broadcasted_iota(I32, (hn.shape[0], LANES), 1)
    hi = hn.astype(BF16)
    lo = (hn - hi.astype(F32)).astype(BF16)
    a = jnp.dot(hi, w_split, preferred_element_type=F32)
    b = jnp.dot(lo, w_split[:, 0:LANES], preferred_element_type=F32)
    logits = (a[:, 0:LANES] + a[:, LANES:2 * LANES]) + b + b_router
    is_grp = (lane >= N_EXPERTS) & (lane < N_EXPERTS + N_GROUPS)
    lanef = lane.astype(F32)
    lg = jnp.where(is_grp, logits, -jnp.inf)
    mg = jnp.max(lg, axis=1, keepdims=True)
    gsel = jnp.min(jnp.where(lg == mg, lanef, 2.0 * LANES), axis=1, keepdims=True) - N_EXPERTS
    g_w = 1.0 / jnp.sum(jnp.where(is_grp, jnp.exp(lg - mg), 0.0), axis=1, keepdims=True)
    in_grp = (lane < N_EXPERTS) & ((lane >> 3).astype(F32) == gsel)
    v1 = jnp.where(in_grp, logits, -jnp.inf)
    m1 = jnp.max(v1, axis=1, keepdims=True)
    i1 = jnp.min(jnp.where(v1 == m1, lanef, 2.0 * LANES), axis=1, keepdims=True)
    v2 = jnp.where(lanef == i1, -jnp.inf, v1)
    m2 = jnp.max(v2, axis=1, keepdims=True)
    i2 = jnp.min(jnp.where(v2 == m2, lanef, 2.0 * LANES), axis=1, keepdims=True)
    e2 = jnp.exp(m2 - m1)
    w1 = g_w / (1.0 + e2)
    w2 = g_w * e2 / (1.0 + e2)
    first, second = lanef == i1, lanef == i2
    comb = jnp.where(first, w1, 0.0) + jnp.where(second, w2, 0.0)
    return comb, jnp.where(first | second, 1.0, 0.0)


def _moe_kernel(x_ref, g_ref, wr_ref, br_ref, wg_ref, wu_ref, wd_ref, fg_ref, o_ref,
                hn_ref, comb_ref, acc_ref, *, final_norm):
    e = pl.program_id(1)
    tm = x_ref.shape[0]
    lane = lax.broadcasted_iota(I32, (tm, LANES), 1)

    @pl.when(e == 0)
    def _():
        hn = _rms(x_ref[...], g_ref[...])
        hn_ref[...] = hn.astype(BF16)
        comb_ref[...] = _route(hn, wr_ref[...], br_ref[...])[0]
        acc_ref[...] = jnp.zeros_like(acc_ref)

    hn = hn_ref[...]
    gt = jnp.dot(hn, wg_ref[...], preferred_element_type=F32)
    up = jnp.dot(hn, wu_ref[...], preferred_element_type=F32)
    he = (gt * jax.nn.sigmoid(gt) * up).astype(BF16)
    ye = jnp.dot(he, wd_ref[...], preferred_element_type=F32)
    cw = jnp.sum(jnp.where(lane == e, comb_ref[...], 0.0), axis=1, keepdims=True)
    acc_ref[...] += cw * ye

    @pl.when(e == N_EXPERTS - 1)
    def _():
        y = x_ref[...] + acc_ref[...]
        if final_norm:
            y = _rms(y, fg_ref[...])
        o_ref[...] = y


def _moe(x, g, w_router, b_router, wg, wu, wd, final_g, final_norm):
    T = x.shape[0]
    tm = 1024
    return pl.pallas_call(
        functools.partial(_moe_kernel, final_norm=final_norm),
        grid=(T // tm, N_EXPERTS),
        in_specs=[pl.BlockSpec((tm, D_MODEL), lambda i, e: (i, 0)),
                  pl.BlockSpec((1, D_MODEL), lambda i, e: (0, 0)),
                  pl.BlockSpec((D_MODEL, 2 * LANES), lambda i, e: (0, 0)),
                  pl.BlockSpec((1, LANES), lambda i, e: (0, 0)),
                  pl.BlockSpec((None, D_MODEL, EXPERT_FF), lambda i, e: (e, 0, 0)),
                  pl.BlockSpec((None, D_MODEL, EXPERT_FF), lambda i, e: (e, 0, 0)),
                  pl.BlockSpec((None, EXPERT_FF, D_MODEL), lambda i, e: (e, 0, 0)),
                  pl.BlockSpec((1, D_MODEL), lambda i, e: (0, 0))],
        out_specs=pl.BlockSpec((tm, D_MODEL), lambda i, e: (i, 0)),
        out_shape=jax.ShapeDtypeStruct((T, D_MODEL), F32),
        scratch_shapes=[pltpu.VMEM((tm, D_MODEL), BF16),
                        pltpu.VMEM((tm, LANES), F32),
                        pltpu.VMEM((tm, D_MODEL), F32)],
        compiler_params=_cparams("arbitrary", "arbitrary"),
        name="moe",
    )(x, g, w_router, b_router, wg, wu, wd, final_g)


def _slot_matrix(pos_t, val_t, ntok, half):
    nslot = MOE_CAP // 2
    slot = (lax.broadcasted_iota(I32, (nslot, ntok), 0) + half * nslot).astype(F32)
    blocks = []
    for e in range(N_EXPERTS):
        hit = pos_t[e:e + 1, :] == slot
        blocks.append(jnp.where(hit, 1.0 if val_t is None else val_t[e:e + 1, :], 0.0).astype(BF16))
    return jnp.concatenate(blocks, axis=0)


def _dispatch_kernel(x_ref, g_ref, wr_ref, br_ref, tri_ref, xd_ref, pos_ref, wt_ref, cnt_ref):
    tm = x_ref.shape[0]
    hn = _rms(x_ref[...], g_ref[...])
    comb, routed = _route(hn, wr_ref[...], br_ref[...])
    rank = jnp.dot(tri_ref[...], routed.astype(BF16), preferred_element_type=F32)
    pos = jnp.where(routed > 0.0, rank, -1.0)
    pos_t = pos.T
    pos_ref[...] = pos_t
    wt_ref[...] = comb.T
    counts = jnp.sum(routed, axis=0, keepdims=True)
    cnt_ref[...] = jnp.broadcast_to(counts, cnt_ref.shape)
    hnb = hn.astype(BF16)
    nslot = MOE_CAP // 2
    lower = jnp.dot(_slot_matrix(pos_t, None, tm, 0), hnb, preferred_element_type=F32)
    xd_ref[:, 0:nslot, :] = lower.reshape(N_EXPERTS, nslot, D_MODEL).astype(xd_ref.dtype)

    busy = jnp.max(counts) > nslot

    @pl.when(busy)
    def _():
        upper = jnp.dot(_slot_matrix(pos_t, None, tm, 1), hnb, preferred_element_type=F32)
        xd_ref[:, nslot:MOE_CAP, :] = upper.reshape(N_EXPERTS, nslot, D_MODEL).astype(xd_ref.dtype)

    @pl.when(jnp.logical_not(busy))
    def _():
        xd_ref[:, nslot:MOE_CAP, :] = jnp.zeros((N_EXPERTS, nslot, D_MODEL), xd_ref.dtype)


def _expert_kernel(busy_ref, x_ref, wg_ref, wu_ref, wd_ref, o_ref):
    ntile, _, nslot, _ = x_ref.shape

    def ffn(half):
        x = x_ref[:, half].reshape(ntile * nslot, D_MODEL)
        gt = jnp.dot(x, wg_ref[...], preferred_element_type=F32)
        up = jnp.dot(x, wu_ref[...], preferred_element_type=F32)
        he = (gt * jax.nn.sigmoid(gt) * up).astype(BF16)
        ye = jnp.dot(he, wd_ref[...], preferred_element_type=F32)
        o_ref[:, half] = ye.reshape(ntile, nslot, D_MODEL).astype(o_ref.dtype)

    ffn(0)
    busy = busy_ref[pl.program_id(0), pl.program_id(1)] > 0

    @pl.when(busy)
    def _():
        ffn(1)

    @pl.when(jnp.logical_not(busy))
    def _():
        o_ref[:, 1] = jnp.zeros((ntile, nslot, D_MODEL), o_ref.dtype)


def _combine_kernel(busy_ref, x_ref, yd_ref, pos_ref, wt_ref, fg_ref, o_ref, *, final_norm):
    tm = x_ref.shape[0]
    nslot = MOE_CAP // 2

    def gathered(half):
        weights = _slot_matrix(pos_ref[...], wt_ref[...], tm, half)
        yd = yd_ref[:, half * nslot:(half + 1) * nslot, :].reshape(N_EXPERTS * nslot, D_MODEL)
        return lax.dot_general(weights, yd, (((0,), (0,)), ((), ())), preferred_element_type=F32)

    o_ref[...] = x_ref[...] + gathered(0)

    @pl.when(busy_ref[pl.program_id(0)] > 0)
    def _():
        o_ref[...] += gathered(1)

    if final_norm:
        o_ref[...] = _rms(o_ref[...], fg_ref[...])


def _moe_dispatch(x, g, w_router, b_router):
    T = x.shape[0]
    tm = MOE_TILE
    nt = T // tm
    tri = jnp.asarray(np.tril(np.ones((tm, tm), np.float32), -1), BF16)
    xd, pos_t, w_t, cnt = pl.pallas_call(
        _dispatch_kernel,
        grid=(nt,),
        in_specs=[pl.BlockSpec((tm, D_MODEL), lambda i: (i, 0)),
                  pl.BlockSpec((1, D_MODEL), lambda i: (0, 0)),
                  pl.BlockSpec((D_MODEL, 2 * LANES), lambda i: (0, 0)),
                  pl.BlockSpec((1, LANES), lambda i: (0, 0)),
                  pl.BlockSpec((tm, tm), lambda i: (0, 0))],
        out_specs=[pl.BlockSpec((N_EXPERTS, None, MOE_CAP, D_MODEL), lambda i: (0, i, 0, 0)),
                   pl.BlockSpec((LANES, tm), lambda i: (0, i)),
                   pl.BlockSpec((LANES, tm), lambda i: (0, i)),
                   pl.BlockSpec((None, 8, LANES), lambda i: (i, 0, 0))],
        out_shape=[jax.ShapeDtypeStruct((N_EXPERTS, nt, MOE_CAP, D_MODEL), BF16),
                   jax.ShapeDtypeStruct((LANES, T), F32),
                   jax.ShapeDtypeStruct((LANES, T), F32),
                   jax.ShapeDtypeStruct((nt, 8, LANES), F32)],
        compiler_params=_cparams("arbitrary"),
        name="moe_dispatch",
    )(x, g, w_router, b_router, tri)
    return xd, pos_t, w_t, cnt[:, 0, 0:N_EXPERTS]


def _moe_finish(x, xd, pos_t, w_t, counts, wg, wu, wd, final_g, final_norm):
    T = x.shape[0]
    tm = MOE_TILE
    nt = T // tm
    tps = min(MOE_TILES_PER_STEP, nt)
    assert nt % tps == 0
    nslot = MOE_CAP // 2
    hot = counts > nslot
    busy_blocks = jnp.any(hot.reshape(nt // tps, tps, N_EXPERTS), axis=1).T.astype(I32)
    busy_tiles = jnp.any(hot, axis=1).astype(I32)
    halves = (N_EXPERTS, nt, 2, nslot, D_MODEL)
    yd = pl.pallas_call(
        _expert_kernel,
        grid_spec=pltpu.PrefetchScalarGridSpec(
            num_scalar_prefetch=1,
            grid=(N_EXPERTS, nt // tps),
            in_specs=[pl.BlockSpec((None, tps, 2, nslot, D_MODEL), lambda e, c, b: (e, c, 0, 0, 0)),
                      pl.BlockSpec((None, D_MODEL, EXPERT_FF), lambda e, c, b: (e, 0, 0)),
                      pl.BlockSpec((None, D_MODEL, EXPERT_FF), lambda e, c, b: (e, 0, 0)),
                      pl.BlockSpec((None, EXPERT_FF, D_MODEL), lambda e, c, b: (e, 0, 0))],
            out_specs=pl.BlockSpec((None, tps, 2, nslot, D_MODEL), lambda e, c, b: (e, c, 0, 0, 0))),
        out_shape=jax.ShapeDtypeStruct(halves, BF16),
        compiler_params=_cparams("arbitrary", "arbitrary"),
        name="moe_experts",
    )(busy_blocks, xd.reshape(halves), wg, wu, wd)
    return pl.pallas_call(
        functools.partial(_combine_kernel, final_norm=final_norm),
        grid_spec=pltpu.PrefetchScalarGridSpec(
            num_scalar_prefetch=1,
            grid=(nt,),
            in_specs=[pl.BlockSpec((tm, D_MODEL), lambda i, b: (i, 0)),
                      pl.BlockSpec((N_EXPERTS, None, MOE_CAP, D_MODEL), lambda i, b: (0, i, 0, 0)),
                      pl.BlockSpec((LANES, tm), lambda i, b: (0, i)),
                      pl.BlockSpec((LANES, tm), lambda i, b: (0, i)),
                      pl.BlockSpec((1, D_MODEL), lambda i, b: (0, 0))],
            out_specs=pl.BlockSpec((tm, D_MODEL), lambda i, b: (i, 0))),
        out_shape=jax.ShapeDtypeStruct((T, D_MODEL), F32),
        compiler_params=_cparams("arbitrary"),
        name="moe_combine",
    )(busy_tiles, x, yd.reshape(N_EXPERTS, nt, MOE_CAP, D_MODEL), pos_t, w_t, final_g)


def _block_diag(w):
    nb, bs, _ = w.shape
    out = jnp.zeros((nb * bs, nb * bs), w.dtype)
    for i in range(nb):
        out = out.at[i * bs:(i + 1) * bs, i * bs:(i + 1) * bs].set(w[i])
    return out


def _selection_aggregator_np(n_cmp, n_slc):
    ratio_s = NSA_SLC_LEN // NSA_CMP_STRIDE
    ratio_c = NSA_CMP_LEN // NSA_CMP_STRIDE
    w = np.convolve(np.ones(ratio_s), np.ones(ratio_c)).astype(np.float32)
    cj = np.arange(n_slc)[:, None] * ratio_s - (ratio_c - 1) + np.arange(w.size)[None, :]
    jj = np.broadcast_to(np.arange(n_slc)[:, None], cj.shape)
    ww = np.broadcast_to(w[None, :], cj.shape)
    keep = (cj >= 0) & (cj < n_cmp)
    agg = np.zeros((n_cmp, n_slc), np.float32)
    np.add.at(agg, (cj[keep], jj[keep]), ww[keep])
    return agg


def _row(v):
    return v.reshape(1, -1).astype(F32)


def _diff_lambda_init(layer):
    return 0.8 - 0.6 * math.exp(-0.3 * layer)


def _mixer_layer(x, l, B, S, tiles, p):
    (bias_swa, bias_diff, bias_slc, bias_win, bias_cmp, agg_pad, n_cmp, ncp) = tiles
    w_in = jnp.pad(p["w_in"][l], ((0, 0), (0, N_IN_PAD - p["w_in"].shape[2]))).astype(BF16)
    w_t = jnp.concatenate([w_in[:, 1536:1792], w_in[:, 2176:2304]], axis=1).T
    a_xg, b_qkv, c_qkv, d_all, vals_t = _in_proj(x, _row(p["mix_norm_g"][l]), w_in, w_t)

    y_a = _rglru(a_xg, p["conv_w"][l].astype(F32), _row(p["conv_b"][l]),
                 _block_diag(p["lru_w_a"][l]).astype(BF16), _row(p["lru_b_a"][l]),
                 _block_diag(p["lru_w_x"][l]).astype(BF16), _row(p["lru_b_x"][l]),
                 _row(p["lru_lambda"][l]), B, S)

    b3 = b_qkv.reshape(B, S, 512)
    swa_cfg = tuple(((h // 2) * 64, (h % 2) != (h // 2), h // 2) for h in range(SWA_HEADS))
    y_b = _banded_attention(b3, 0, b3, 2, b3, 3, bias_swa, _row(p["swa_sinks"][l]),
                            B=B, S=S, window=SWA_WINDOW, q_cfg=swa_cfg)

    c3 = c_qkv.reshape(B, S, 768)
    lam4 = jnp.stack([p["diff_lq1"][l], p["diff_lk1"][l], p["diff_lq2"][l], p["diff_lk2"][l]]).astype(F32)
    gsub = jnp.tile(p["diff_subln_g"][l].reshape(-1, 1).astype(F32), (2, 1))
    y_c = _flash_attention("diff", c3, 0, c3, 1, vals_t, 256, 0, bias_diff, (lam4, gsub), B=B, S=S,
                           lambda_init=_diff_lambda_init(l))

    d3 = d_all.reshape(B, S, 768)
    nchunk = S // NSA_CMP_STRIDE
    ck = d3[:, :, 256:320].reshape(B, nchunk, NSA_CMP_STRIDE * HEAD_DIM)
    cv = d3[:, :, 320:384].reshape(B, nchunk, NSA_CMP_STRIDE * HEAD_DIM)

    def pos8(a):
        return jnp.broadcast_to(a.reshape(1, -1), (8, a.size)).astype(BF16)

    w1k = jnp.pad(p["nsa_w1_k"][l], ((0, 0), (0, HEAD_DIM))).astype(BF16)
    w1v = jnp.pad(p["nsa_w1_v"][l], ((0, 0), (HEAD_DIM, 0))).astype(BF16)
    w2 = _block_diag(jnp.stack([p["nsa_w2_k"][l], p["nsa_w2_v"][l]])).astype(BF16)
    kcv = _nsa_compress(ck, cv, pos8(p["nsa_pos_k"][l]), pos8(p["nsa_pos_v"][l]), w1k, w1v, w2, ncp)
    o_c, sel = _nsa_cmp_select(d3, kcv, bias_cmp, agg_pad, B=B, S=S, n_cmp=n_cmp, ncp=ncp)
    o_s = _flash_attention("nsa", d3, 0, d3, 3, vals_t, 128, 2, bias_slc, (sel,), B=B, S=S)
    win_cfg = tuple((0, h % 2 == 1, 1) for h in range(NSA_HEADS))
    o_w = _banded_attention(d3, 0, d3, 4, None, None, bias_win, None, B=B, S=S, window=NSA_WINDOW, q_cfg=win_cfg)

    T = B * S
    return _merge(x, _row(p["mix_norm_g"][l]), y_a, y_b.reshape(T, 256), y_c.reshape(T, 256),
                  o_c.reshape(T, 256), o_s.reshape(T, 256), o_w.reshape(T, 256), d_all,
                  p["w_gate"][l].astype(BF16), p["b_gate"][l].astype(F32), p["w_branch"][l].astype(BF16),
                  p["w_out"][l].astype(BF16))


def _moe_layer(x, l, p, final_norm):
    w_router = jnp.concatenate([p["w_router_exp"][l], p["w_router_grp"][l]], axis=1)
    w_router = jnp.pad(w_router, ((0, 0), (0, LANES - w_router.shape[1]))).astype(F32)
    w_hi = w_router.astype(BF16)
    w_router = jnp.concatenate([w_hi, (w_router - w_hi.astype(F32)).astype(BF16)], axis=1)
    b_router = jnp.concatenate([p["b_router_exp"][l], p["b_router_grp"][l]])
    b_router = jnp.pad(b_router, (0, LANES - b_router.shape[0])).reshape(1, LANES).astype(F32)
    g = _row(p["ffn_norm_g"][l])
    fg = _row(p["final_norm_g"])
    wg, wu, wd = (p[k][l].astype(BF16) for k in ("w_exp_gate", "w_exp_up", "w_exp_down"))
    xd, pos_t, w_t, counts = _moe_dispatch(x, g, w_router, b_router)
    return lax.cond(jnp.max(counts) > MOE_CAP,
                    lambda: _moe(x, g, w_router, b_router, wg, wu, wd, fg, final_norm),
                    lambda: _moe_finish(x, xd, pos_t, w_t, counts, wg, wu, wd, fg, final_norm))


def _forward(p):
    x = p["x"]
    B, S, D = x.shape
    depth = p["w_in"].shape[0]
    assert D == D_MODEL and S % (16 * Q_BLOCK) == 0 and S // NSA_SLC_LEN <= LANES
    rel = p["rel_bias"].astype(F32)
    n_cmp = (S - NSA_CMP_LEN) // NSA_CMP_STRIDE + 1
    ncp = LANES + S // NSA_CMP_STRIDE
    agg = _selection_aggregator_np(n_cmp, S // NSA_SLC_LEN)
    agg_pad = np.zeros((ncp, LANES), np.float32)
    agg_pad[LANES:LANES + n_cmp, :agg.shape[1]] = agg
    tiles = (
        _bias_tiles(rel, _banded_buckets(1), 0, SWA_HEADS, False),
        _bias_tiles(rel, _flash_buckets(), SWA_HEADS, DIFF_HEADS, True, LOG2E),
        _bias_tiles(rel, _flash_buckets(), SWA_HEADS + DIFF_HEADS, NSA_HEADS, True, LOG2E),
        _bias_tiles(rel, _banded_buckets(NSA_WINDOW // Q_BLOCK), SWA_HEADS + DIFF_HEADS, NSA_HEADS, False),
        _bias_tiles(rel, _cmp_buckets(), SWA_HEADS + DIFF_HEADS, NSA_HEADS, True),
        jnp.asarray(agg_pad, BF16), n_cmp, ncp,
    )
    xt = x.reshape(B * S, D).astype(F32)
    for l in range(depth):
        xt = _mixer_layer(xt, l, B, S, tiles, p)
        xt = _moe_layer(xt, l, p, final_norm=(l == depth - 1))
    return xt.reshape(B, S, D).astype(x.dtype)


def kernel(x, rel_bias, final_norm_g, mix_norm_g, w_in, conv_w, conv_b, lru_w_a, lru_b_a, lru_w_x, lru_b_x,
           lru_lambda, swa_sinks, diff_lq1, diff_lk1, diff_lq2, diff_lk2, diff_subln_g, nsa_pos_k, nsa_w1_k,
           nsa_w2_k, nsa_pos_v, nsa_w1_v, nsa_w2_v, w_gate, b_gate, w_branch, w_out, ffn_norm_g, w_router_grp,
           b_router_grp, w_router_exp, b_router_exp, w_exp_gate, w_exp_up, w_exp_down):
    return _forward(dict(
        x=x, rel_bias=rel_bias, final_norm_g=final_norm_g, mix_norm_g=mix_norm_g, w_in=w_in, conv_w=conv_w,
        conv_b=conv_b, lru_w_a=lru_w_a, lru_b_a=lru_b_a, lru_w_x=lru_w_x, lru_b_x=lru_b_x, lru_lambda=lru_lambda,
        swa_sinks=swa_sinks, diff_lq1=diff_lq1, diff_lk1=diff_lk1, diff_lq2=diff_lq2, diff_lk2=diff_lk2,
        diff_subln_g=diff_subln_g, nsa_pos_k=nsa_pos_k, nsa_w1_k=nsa_w1_k, nsa_w2_k=nsa_w2_k, nsa_pos_v=nsa_pos_v,
        nsa_w1_v=nsa_w1_v, nsa_w2_v=nsa_w2_v, w_gate=w_gate, b_gate=b_gate, w_branch=w_branch, w_out=w_out,
        ffn_norm_g=ffn_norm_g, w_router_grp=w_router_grp, b_router_grp=b_router_grp, w_router_exp=w_router_exp,
        b_router_exp=b_router_exp, w_exp_gate=w_exp_gate, w_exp_up=w_exp_up, w_exp_down=w_exp_down))
```

```python
import functools
import math

import numpy as np
import jax
import jax.numpy as jnp
from jax import lax
from jax.experimental import pallas as pl
from jax.experimental.pallas import tpu as pltpu

F32 = jnp.float32
BF16 = jnp.bfloat16
I32 = jnp.int32

D_MODEL = 1024
HEAD_DIM = 64
MIX_WIDTH = 256
LRU_C = 8.0
CONV_WIDTH = 4
SWA_HEADS = 4
SWA_WINDOW = 128
DIFF_HEADS = 4
DIFF_DIM = 32
DIFF_SUBLN_EPS = 1e-5
NSA_HEADS = 4
NSA_CMP_LEN = 32
NSA_CMP_STRIDE = 16
NSA_SLC_LEN = 64
NSA_TOPK = 16
NSA_WINDOW = 512
NSA_FORCED_SCORE = 1e4
REL_BUCKETS = 32
REL_MAX_DIST = 128
N_GROUPS = 4
EXPERTS_PER_GROUP = 8
N_EXPERTS = 32
EXPERT_FF = 256
NORM_EPS = 1e-6
NEG_INF = -1e30
TINY = 1e-30
LOG2E = math.log2(math.e)

LANES = 128
Q_BLOCK = 128
BANDED_Q_BLOCKS = 2
FLASH_BLOCK = 256
MOE_TILE = 512
MOE_CAP = 128
MOE_TILES_PER_STEP = 8
FLASH_MAX_GROWTH = 64.0
FLASH_FAR_BLOCKS = 4
N_IN_PAD = 2560
VMEM_LIMIT = 56 * 1024 * 1024

_NT = (((1,), (1,)), ((), ()))


def _cparams(*sem):
    return pltpu.CompilerParams(dimension_semantics=sem, vmem_limit_bytes=VMEM_LIMIT)


def _t5_bucket_np(dist):
    n = np.maximum(dist, 0)
    exact = REL_BUCKETS // 2
    scaled = (np.log(np.maximum(n, 1).astype(np.float32) / np.float32(exact))
              / np.float32(math.log(REL_MAX_DIST / exact))).astype(np.float32)
    large = np.minimum(exact + (scaled * np.float32(REL_BUCKETS - exact)).astype(np.int32), REL_BUCKETS - 1)
    return np.where(n < exact, n, large).astype(np.int32)


def _bias_tile_kernel(tab_ref, idx_ref, o_ref, *, head0, shift, scale):
    h = pl.program_id(0) + head0
    idx = idx_ref[0]
    acc = jnp.zeros(idx.shape, F32)
    for b in range(REL_BUCKETS):
        acc = jnp.where(idx == b, tab_ref[b, h], acc)
    if shift:
        acc = acc - tab_ref[REL_BUCKETS - 1, h]
    if scale != 1.0:
        acc = acc * scale
    o_ref[0, 0] = acc


def _bias_tiles(rel_bias, buckets, head0, nheads, shift, scale=1.0):
    R, M, N = buckets.shape
    return pl.pallas_call(
        functools.partial(_bias_tile_kernel, head0=head0, shift=shift, scale=scale),
        grid=(nheads, R),
        in_specs=[pl.BlockSpec(memory_space=pltpu.SMEM),
                  pl.BlockSpec((1, M, N), lambda h, r: (r, 0, 0))],
        out_specs=pl.BlockSpec((1, 1, M, N), lambda h, r: (h, r, 0, 0)),
        out_shape=jax.ShapeDtypeStruct((nheads, R, M, N), F32),
        compiler_params=_cparams("arbitrary", "arbitrary"),
        name="bias_tiles",
    )(rel_bias, jnp.asarray(buckets))


def _banded_buckets(nback):
    i = np.arange(BANDED_Q_BLOCKS * Q_BLOCK)[None, :]
    j = np.arange((nback + BANDED_Q_BLOCKS) * Q_BLOCK)[:, None]
    return _t5_bucket_np(i - j + nback * Q_BLOCK)[None]


def _flash_buckets():
    k = np.arange(FLASH_BLOCK)[:, None]
    q = np.arange(FLASH_BLOCK)[None, :]
    return np.stack([_t5_bucket_np(q - k), _t5_bucket_np(q - k + FLASH_BLOCK)])


def _cmp_buckets():
    i = np.arange(Q_BLOCK)[None, :, None]
    c2 = np.arange(2 * LANES)[None, None, :]
    r = np.arange(16)[:, None, None]
    dist = i - NSA_CMP_STRIDE * (c2 - LANES - 8 * r) - (NSA_CMP_LEN - 1)
    return _t5_bucket_np(dist)


def _rms(x, g):
    return x * lax.rsqrt(jnp.mean(x * x, axis=-1, keepdims=True) + NORM_EPS) * g


def _proj_kernel(x_ref, g_ref, w_ref, wt_ref, oa_ref, ob_ref, oc_ref, od_ref, ot_ref):
    h = _rms(x_ref[...], g_ref[...]).astype(BF16)
    oa_ref[...] = jnp.dot(h, w_ref[:, 0:512], preferred_element_type=F32)
    ob_ref[...] = jnp.dot(h, w_ref[:, 512:1024], preferred_element_type=F32).astype(BF16)
    oc_ref[...] = jnp.dot(h, w_ref[:, 1024:1792], preferred_element_type=F32).astype(BF16)
    od_ref[...] = jnp.dot(h, w_ref[:, 1792:2560], preferred_element_type=F32).astype(BF16)
    ot_ref[...] = lax.dot_general(wt_ref[...], h, _NT, preferred_element_type=F32).astype(BF16)


def _in_proj(x, g, w, wt):
    T = x.shape[0]
    tm = 512
    nt = wt.shape[0]
    return pl.pallas_call(
        _proj_kernel,
        grid=(T // tm,),
        in_specs=[pl.BlockSpec((tm, D_MODEL), lambda i: (i, 0)),
                  pl.BlockSpec((1, D_MODEL), lambda i: (0, 0)),
                  pl.BlockSpec((D_MODEL, N_IN_PAD), lambda i: (0, 0)),
                  pl.BlockSpec((nt, D_MODEL), lambda i: (0, 0))],
        out_specs=[pl.BlockSpec((tm, 512), lambda i: (i, 0)),
                   pl.BlockSpec((tm, 512), lambda i: (i, 0)),
                   pl.BlockSpec((tm, 768), lambda i: (i, 0)),
                   pl.BlockSpec((tm, 768), lambda i: (i, 0)),
                   pl.BlockSpec((nt, tm), lambda i: (0, i))],
        out_shape=[jax.ShapeDtypeStruct((T, 512), F32),
                   jax.ShapeDtypeStruct((T, 512), BF16),
                   jax.ShapeDtypeStruct((T, 768), BF16),
                   jax.ShapeDtypeStruct((T, 768), BF16),
                   jax.ShapeDtypeStruct((nt, T), BF16)],
        compiler_params=_cparams("arbitrary"),
        name="in_proj",
    )(x, g, w, wt)


def _lru_kernel(xg_ref, cw_ref, cb_ref, wa_ref, ba_ref, wx_ref, bx_ref, lam_ref, o_ref,
                ext_ref, h_ref, *, Tt):
    t = pl.program_id(1)

    W = MIX_WIDTH

    @pl.when(t == 0)
    def _():
        ext_ref[0:8, :] = jnp.zeros((8, W), F32)
        h_ref[...] = jnp.zeros_like(h_ref)

    x = xg_ref[:, 0:W]
    gate = xg_ref[:, W:2 * W]
    ext_ref[8:Tt + 8, :] = x
    row = lax.broadcasted_iota(I32, (Tt, W), 0)
    xc = cb_ref[...] + x * cw_ref[CONV_WIDTH - 1:CONV_WIDTH, :]
    for s in range(1, CONV_WIDTH):
        xc = xc + ext_ref[8 - s:8 - s + Tt, :] * cw_ref[CONV_WIDTH - 1 - s:CONV_WIDTH - s, :]
    ext_ref[0:8, :] = ext_ref[Tt:Tt + 8, :]

    xcb = xc.astype(BF16)
    r = jax.nn.sigmoid(jnp.dot(xcb, wa_ref[...], preferred_element_type=F32) + ba_ref[...])
    ig = jax.nn.sigmoid(jnp.dot(xcb, wx_ref[...], preferred_element_type=F32) + bx_ref[...])
    z = -lam_ref[...]
    softplus = jnp.maximum(z, 0.0) + jnp.log1p(jnp.exp(-jnp.abs(z)))
    log_a = (-LRU_C) * r * softplus
    a = jnp.exp(log_a)
    mult = jnp.sqrt(1.0 - a * a)
    mult = jnp.where((row + t * Tt) == 0, 1.0, mult)
    b = mult * ig * xc

    d = 1
    while d < Tt:
        keep = row >= d
        b = jnp.where(keep, a * pltpu.roll(b, d, 0) + b, b)
        a = jnp.where(keep, a * pltpu.roll(a, d, 0), a)
        d *= 2
    hs = a * h_ref[7:8, :] + b
    h_ref[...] = hs[Tt - 8:Tt]
    o_ref[...] = (hs * jax.nn.gelu(gate)).astype(o_ref.dtype)


def _rglru(a_xg, conv_w, conv_b, wa_bd, b_a, wx_bd, b_x, lam, B, S):
    Tt = 256
    W = MIX_WIDTH
    vec = pl.BlockSpec((1, W), lambda b, t: (0, 0))
    mat = pl.BlockSpec((W, W), lambda b, t: (0, 0))
    out = pl.pallas_call(
        functools.partial(_lru_kernel, Tt=Tt),
        grid=(B, S // Tt),
        in_specs=[pl.BlockSpec((None, Tt, 2 * W), lambda b, t: (b, t, 0)),
                  pl.BlockSpec((CONV_WIDTH, W), lambda b, t: (0, 0)),
                  vec, mat, vec, mat, vec, vec],
        out_specs=pl.BlockSpec((None, Tt, W), lambda b, t: (b, t, 0)),
        out_shape=jax.ShapeDtypeStruct((B, S, W), BF16),
        scratch_shapes=[pltpu.VMEM((Tt + 8, W), F32), pltpu.VMEM((8, W), F32)],
        compiler_params=_cparams("arbitrary", "arbitrary"),
        name="rglru",
    )(a_xg.reshape(B, S, 2 * W), conv_w, conv_b, wa_bd, b_a, wx_bd, b_x, lam)
    return out.reshape(B * S, W)


def _lane_window(q128, lo, width, roll, scale):
    q = q128 * scale
    if roll:
        q = pltpu.roll(q, LANES // 2, 1)
    lane = lax.broadcasted_iota(I32, q.shape, 1)
    return jnp.where((lane >= lo) & (lane < lo + width), q, 0.0).astype(BF16)


def _pair_halves(even, odd, even_half, odd_half):
    if even_half == 1:
        even = pltpu.roll(even, LANES // 2, 1)
    if odd_half == 0:
        odd = pltpu.roll(odd, LANES // 2, 1)
    lane = lax.broadcasted_iota(I32, even.shape, 1)
    return jnp.where(lane < LANES // 2, even, odd)


def _banded_kernel(*refs, nback, window, q_cfg, has_sink, shared_kv):
    nk = nback + BANDED_Q_BLOCKS
    TQ = BANDED_Q_BLOCKS * Q_BLOCK
    q_ref = refs[0]
    k_refs = refs[1:1 + nk]
    pos = 1 + nk
    if shared_kv:
        v_refs = k_refs
    else:
        v_refs = refs[pos:pos + nk]
        pos += nk
    bias_ref = refs[pos]
    pos += 1
    sink_ref = None
    if has_sink:
        sink_ref = refs[pos]
        pos += 1
    o_ref = refs[pos]

    n = pl.program_id(1)
    KW = nk * Q_BLOCK
    kcat = jnp.concatenate([k_refs[nk - 1 - i][...] for i in range(nk)], axis=0)
    vcat = kcat if shared_kv else jnp.concatenate([v_refs[nk - 1 - i][...] for i in range(nk)], axis=0)
    vt = vcat.astype(F32).T.astype(BF16)
    jj = lax.broadcasted_iota(I32, (KW, TQ), 0)
    ii = lax.broadcasted_iota(I32, (KW, TQ), 1)
    dist = ii - jj + nback * Q_BLOCK
    valid = (dist >= 0) & (dist < window) & (jj >= (nback - BANDED_Q_BLOCKS * n) * Q_BLOCK)
    scale = HEAD_DIM ** -0.5

    def logits(h):
        lo, roll, _ = q_cfg[h]
        g = h // 2
        qx = _lane_window(q_ref[:, LANES * g:LANES * (g + 1)].astype(F32), lo, HEAD_DIM, roll, scale)
        s = lax.dot_general(kcat, qx, _NT, preferred_element_type=F32) + bias_ref[h, 0]
        return jnp.where(valid, s, NEG_INF)

    def softmax(h, s):
        m = jnp.max(s, axis=0, keepdims=True)
        if has_sink:
            sink = sink_ref[0:1, h:h + 1]
            m = jnp.maximum(m, sink)
        p = jnp.exp(s - m)
        den = jnp.sum(p, axis=0, keepdims=True)
        if has_sink:
            den = den + jnp.exp(sink - m)
        return p.astype(BF16), 1.0 / jnp.maximum(den, TINY)

    def values(h, p, inv):
        half = q_cfg[h][2]
        return jnp.dot(vt[HEAD_DIM * half:HEAD_DIM * (half + 1)], p, preferred_element_type=F32) * inv

    nh = len(q_cfg)
    ss = [logits(h) for h in range(nh)]
    pr = [softmax(h, ss[h]) for h in range(nh)]
    outs = [values(h, *pr[h]) for h in range(nh)]
    for g in range(nh // 2):
        pair = jnp.concatenate([outs[2 * g], outs[2 * g + 1]], axis=0)
        o_ref[:, LANES * g:LANES * (g + 1)] = pair.T.astype(o_ref.dtype)


def _banded_attention(qsrc, q_col, ksrc, k_col, vsrc, v_col, bias, sinks, *, B, S, window, q_cfg):
    nback = -(-window // Q_BLOCK)
    nq = BANDED_Q_BLOCKS
    nk = nback + nq
    TQ = nq * Q_BLOCK
    shared_kv = vsrc is None
    has_sink = sinks is not None

    def kv_spec(col, i):
        return pl.BlockSpec((None, Q_BLOCK, LANES), lambda b, n: (b, jnp.maximum(nq * n + nq - 1 - i, 0), col))

    in_specs = [pl.BlockSpec((None, TQ, 2 * LANES), lambda b, n: (b, n, q_col))]
    args = [qsrc]
    in_specs += [kv_spec(k_col, i) for i in range(nk)]
    args += [ksrc] * nk
    if not shared_kv:
        in_specs += [kv_spec(v_col, i) for i in range(nk)]
        args += [vsrc] * nk
    in_specs.append(pl.BlockSpec(bias.shape, lambda b, n: (0, 0, 0, 0)))
    args.append(bias)
    if has_sink:
        in_specs.append(pl.BlockSpec(sinks.shape, lambda b, n: (0, 0)))
        args.append(sinks)
    return pl.pallas_call(
        functools.partial(_banded_kernel, nback=nback, window=window, q_cfg=q_cfg,
                          has_sink=has_sink, shared_kv=shared_kv),
        grid=(B, S // TQ),
        in_specs=in_specs,
        out_specs=pl.BlockSpec((None, TQ, 2 * LANES), lambda b, n: (b, n, 0)),
        out_shape=jax.ShapeDtypeStruct((B, S, 2 * LANES), BF16),
        compiler_params=_cparams("arbitrary", "arbitrary"),
        name="banded_attn",
    )(*args)


def _flash_kernel(*refs, mode, lambda_init):
    TB = FLASH_BLOCK
    if mode == "diff":
        (q_ref, k_ref, vt_ref, bias_ref, lam_ref, g_ref, o_ref,
         qx_ref, m_ref, l_ref, acc_ref) = refs
        sel_ref = None
        maps = [dict(g=h // 2, lo=(h % 2) * 64 + 32 * mm, w=DIFF_DIM, roll=False, kg=h // 2, vrow=HEAD_DIM * h, h=h)
                for h in range(DIFF_HEADS) for mm in range(2)]
        scale = DIFF_DIM ** -0.5
    else:
        (q_ref, k_ref, vt_ref, sel_ref, bias_ref, o_ref,
         qx_ref, m_ref, l_ref, acc_ref) = refs
        maps = [dict(g=h // 2, lo=0, w=HEAD_DIM, roll=(h % 2 == 1), kg=0, vrow=HEAD_DIM, h=h)
                for h in range(NSA_HEADS)]
        scale = HEAD_DIM ** -0.5
    n = pl.program_id(1)

    for mp, c in enumerate(maps):
        qx_ref[mp] = _lane_window(q_ref[:, LANES * c["g"]:LANES * (c["g"] + 1)].astype(F32),
                                  c["lo"], c["w"], c["roll"], scale * LOG2E)
    m_ref[...] = jnp.full(m_ref.shape, NEG_INF, F32)
    l_ref[...] = jnp.zeros_like(l_ref)
    acc_ref[...] = jnp.zeros_like(acc_ref)

    key = lax.broadcasted_iota(I32, (TB, TB), 0)
    qry = lax.broadcasted_iota(I32, (TB, TB), 1)

    def step(j, kind, TK=TB):
        koff = pl.multiple_of(j * TB, TB)
        valid = None
        if kind == "diag":
            valid = qry >= key
        if sel_ref is not None:
            per = TK // NSA_SLC_LEN
            base = j * (TB // NSA_SLC_LEN)
            rows = [jnp.broadcast_to(sel_ref[pl.ds(base + i, 1), :], (NSA_SLC_LEN, TB)) for i in range(per)]
            picked = jnp.concatenate(rows, axis=0) > 0.5
            valid = picked if valid is None else (valid & picked)

        def logits(mp):
            c = maps[mp]
            kt = k_ref[pl.ds(koff, TK), LANES * c["kg"]:LANES * (c["kg"] + 1)]
            s = lax.dot_general(kt, qx_ref[mp], _NT, preferred_element_type=F32)
            if kind == "near":
                s = s + bias_ref[c["h"], 1]
            elif kind == "diag":
                s = s + bias_ref[c["h"], 0]
            if valid is not None:
                s = jnp.where(valid, s, NEG_INF)
            return s

        def softmax(mp, s):
            m_prev = m_ref[mp]
            m_new = jnp.maximum(m_prev, jnp.max(s, axis=0, keepdims=True))
            alpha = jnp.exp2(m_prev - m_new)
            p = jnp.exp2(s - m_new)
            l_ref[mp] = alpha * l_ref[mp] + jnp.sum(p, axis=0, keepdims=True)
            m_ref[mp] = m_new
            return p.astype(BF16), alpha

        def values(mp, p, alpha):
            c = maps[mp]
            vt = vt_ref[c["vrow"]:c["vrow"] + HEAD_DIM, pl.ds(koff, TK)]
            acc_ref[mp] = alpha * acc_ref[mp] + jnp.dot(vt, p, preferred_element_type=F32)

        nm = len(maps)
        ss = [logits(mp) for mp in range(nm)]

        def two_pass():
            pa = [softmax(mp, ss[mp]) for mp in range(nm)]
            for mp in range(nm):
                values(mp, *pa[mp])

        if kind != "far":
            two_pass()
            return

        stabs = [m_ref[mp] for mp in range(nm)]
        ps = [jnp.exp2(ss[mp] - stabs[mp]) for mp in range(nm)]
        m_news = [jnp.maximum(stabs[mp], jnp.max(ss[mp], axis=0, keepdims=True)) for mp in range(nm)]
        growth = m_news[0] - stabs[0]
        for mp in range(1, nm):
            growth = jnp.maximum(growth, m_news[mp] - stabs[mp])
        safe = jnp.max(growth) <= FLASH_MAX_GROWTH
        psums = [jnp.sum(ps[mp], axis=0, keepdims=True) for mp in range(nm)]
        pvs = [jnp.dot(vt_ref[maps[mp]["vrow"]:maps[mp]["vrow"] + HEAD_DIM, pl.ds(koff, TK)], ps[mp].astype(BF16),
                       preferred_element_type=F32) for mp in range(nm)]

        @pl.when(safe)
        def _():
            for mp in range(nm):
                alpha = jnp.exp2(stabs[mp] - m_news[mp])
                l_ref[mp] = (l_ref[mp] + psums[mp]) * alpha
                acc_ref[mp] = (acc_ref[mp] + pvs[mp]) * alpha
                m_ref[mp] = m_news[mp]

        @pl.when(jnp.logical_not(safe))
        def _():
            two_pass()

    step(n, "diag")

    @pl.when(n >= 1)
    def _():
        step(n - 1, "near")

    nfar = jnp.maximum(n - 1, 0)
    FW = FLASH_FAR_BLOCKS

    def far_body(j, carry):
        step(FW * j, "far", FW * TB)
        return carry

    lax.fori_loop(0, nfar // FW, far_body, 0)

    done = (nfar // FW) * FW
    rest = nfar - done

    @pl.when(rest >= FW // 2)
    def _():
        step(done, "far", (FW // 2) * TB)

    @pl.when(rest % (FW // 2) == 1)
    def _():
        step(nfar - 1, "far")

    def normalised(mp):
        return acc_ref[mp] * (1.0 / jnp.maximum(l_ref[mp], TINY))

    if mode == "diff":
        lq = lam_ref[...]
        lam = (jnp.exp(jnp.sum(lq[0:1] * lq[1:2], axis=1, keepdims=True))
               - jnp.exp(jnp.sum(lq[2:3] * lq[3:4], axis=1, keepdims=True)) + lambda_init)
        for g in range(DIFF_HEADS // 2):
            res = []
            for h in (2 * g, 2 * g + 1):
                o = normalised(2 * h) - lam * normalised(2 * h + 1)
                ms = jnp.sum(o * o, axis=0, keepdims=True) * (1.0 / (2 * DIFF_DIM))
                res.append(o * lax.rsqrt(ms + DIFF_SUBLN_EPS))
            out = jnp.concatenate(res, axis=0) * g_ref[...] * (1.0 - lambda_init)
            o_ref[:, LANES * g:LANES * (g + 1)] = out.T.astype(o_ref.dtype)
    else:
        for g in range(NSA_HEADS // 2):
            out = jnp.concatenate([normalised(2 * g), normalised(2 * g + 1)], axis=0)
            o_ref[:, LANES * g:LANES * (g + 1)] = out.T.astype(o_ref.dtype)


def _flash_attention(mode, qsrc, q_col, ksrc, k_col, vt, vt_rows, vt_blk, bias, extra, *, B, S, lambda_init=0.0):
    TB = FLASH_BLOCK
    q_spec = pl.BlockSpec((None, TB, 2 * LANES), lambda b, n: (b, n, q_col))
    bias_spec = pl.BlockSpec(bias.shape, lambda b, n: (0, 0, 0, 0))
    vt_spec = pl.BlockSpec((vt_rows, S), lambda b, n: (vt_blk, b))
    if mode == "diff":
        lam4, gsub = extra
        nmaps = 2 * DIFF_HEADS
        in_specs = [q_spec,
                    pl.BlockSpec((None, S, 2 * LANES), lambda b, n: (b, 0, k_col)),
                    vt_spec, bias_spec,
                    pl.BlockSpec(lam4.shape, lambda b, n: (0, 0)),
                    pl.BlockSpec(gsub.shape, lambda b, n: (0, 0))]
        args = [qsrc, ksrc, vt, bias, lam4, gsub]
    else:
        (sel,) = extra
        nmaps = NSA_HEADS
        in_specs = [q_spec,
                    pl.BlockSpec((None, S, LANES), lambda b, n: (b, 0, k_col)),
                    vt_spec,
                    pl.BlockSpec((None, LANES, TB), lambda b, n: (b, 0, n)),
                    bias_spec]
        args = [qsrc, ksrc, vt, sel, bias]
    return pl.pallas_call(
        functools.partial(_flash_kernel, mode=mode, lambda_init=lambda_init),
        grid=(B, S // TB),
        in_specs=in_specs,
        out_specs=pl.BlockSpec((None, TB, 2 * LANES), lambda b, n: (b, n, 0)),
        out_shape=jax.ShapeDtypeStruct((B, S, 2 * LANES), BF16),
        scratch_shapes=[pltpu.VMEM((nmaps, TB, LANES), BF16),
                        pltpu.VMEM((nmaps, 1, TB), F32),
                        pltpu.VMEM((nmaps, 1, TB), F32),
                        pltpu.VMEM((nmaps, HEAD_DIM, TB), F32)],
        compiler_params=_cparams("arbitrary", "arbitrary"),
        name="flash_" + mode,
    )(*args)


def _compress_kernel(ck_ref, cv_ref, pk_ref, pv_ref, w1k_ref, w1v_ref, w2_ref, o_ref):
    half = NSA_CMP_STRIDE * HEAD_DIM
    ck = ck_ref[...]
    cv = cv_ref[...]
    nrow = ck.shape[0]

    def mm(a, w):
        return jnp.dot(a, w, preferred_element_type=F32)

    top = mm(ck, w1k_ref[0:half, :]) + mm(cv, w1v_ref[0:half, :])
    bot = mm(ck, w1k_ref[half:2 * half, :]) + mm(cv, w1v_ref[half:2 * half, :])
    posw = (mm(pk_ref[...], w1k_ref[...]) + mm(pv_ref[...], w1v_ref[...]))[0:1]
    pre = top + pltpu.roll(bot, nrow - 1, 0) + posw
    kcv = mm(jax.nn.gelu(pre).astype(BF16), w2_ref[...])
    o_ref[0:LANES, :] = jnp.zeros((LANES, LANES), o_ref.dtype)
    o_ref[LANES:LANES + nrow, :] = kcv.astype(o_ref.dtype)


def _nsa_compress(ck, cv, pk, pv, w1k, w1v, w2, ncp):
    B, nchunk, width = ck.shape
    assert ncp == LANES + nchunk

    def full(a):
        return pl.BlockSpec(a.shape, lambda b: (0,) * a.ndim)

    chunk = pl.BlockSpec((None, nchunk, width), lambda b: (b, 0, 0))
    return pl.pallas_call(
        _compress_kernel,
        grid=(B,),
        in_specs=[chunk, chunk, full(pk), full(pv), full(w1k), full(w1v), full(w2)],
        out_specs=pl.BlockSpec((None, ncp, LANES), lambda b: (b, 0, 0)),
        out_shape=jax.ShapeDtypeStruct((B, ncp, LANES), BF16),
        compiler_params=_cparams("arbitrary"),
        name="nsa_compress",
    )(ck, cv, pk, pv, w1k, w1v, w2)


def _cmp_kernel(q_ref, kcv_ref, bias_ref, agg_ref, oc_ref, sel_ref, s_scr, *, n_cmp, ncp):
    n = pl.program_id(1)
    QB = Q_BLOCK
    kcv = kcv_ref[...]
    cp = lax.broadcasted_iota(I32, (QB, ncp), 1)
    qpos = n * QB + lax.broadcasted_iota(I32, (QB, ncp), 0)
    valid = ((cp >= LANES) & (cp < LANES + n_cmp)
             & (qpos - NSA_CMP_STRIDE * (cp - LANES) - (NSA_CMP_LEN - 1) >= 0))
    woff = pl.multiple_of((n // 16) * LANES, LANES)
    scale = HEAD_DIM ** -0.5

    def logits(h):
        g = h // 2
        qx = _lane_window(q_ref[:, LANES * g:LANES * (g + 1)].astype(F32), 0, HEAD_DIM, h % 2 == 1, scale)
        s_scr[h] = lax.dot_general(qx, kcv, _NT, preferred_element_type=F32)
        s_scr[h, :, pl.ds(woff, 2 * LANES)] += bias_ref[h]

    def softmax(h):
        s = jnp.where(valid, s_scr[h], NEG_INF)
        m = jnp.max(s, axis=1, keepdims=True)
        p = jnp.where(valid, jnp.exp(s - m), 0.0)
        den = jnp.sum(p, axis=1, keepdims=True)
        return p * (1.0 / jnp.maximum(den, TINY))

    for h in range(NSA_HEADS):
        logits(h)
    ps = [softmax(h) for h in range(NSA_HEADS)]
    outs = [jnp.dot(p.astype(BF16), kcv, preferred_element_type=F32) for p in ps]
    for g in range(NSA_HEADS // 2):
        oc_ref[:, LANES * g:LANES * (g + 1)] = _pair_halves(outs[2 * g], outs[2 * g + 1], 1, 1).astype(oc_ref.dtype)

    psum = (ps[0] + ps[1]) + (ps[2] + ps[3])
    imp = jnp.dot(psum.astype(BF16), agg_ref[...], preferred_element_type=F32)

    v = imp.T
    jb = lax.broadcasted_iota(I32, v.shape, 0)
    qp = n * QB + lax.broadcasted_iota(I32, v.shape, 1)
    cur = qp >> 6
    forced = (jb == 0) | (jb == cur) | (jb == cur - 1)
    v = jnp.where(forced, -jnp.inf, jnp.where(jb * NSA_SLC_LEN <= qp, v, -1.0))
    rid = jb.astype(F32)
    picked = jnp.where(forced, 1.0, 0.0)
    for _ in range(NSA_TOPK - 3):
        mx = jnp.max(v, axis=0, keepdims=True)
        first = jnp.min(jnp.where(v == mx, rid, float(LANES)), axis=0, keepdims=True)
        hit = rid == first
        picked = jnp.where(hit, 1.0, picked)
        v = jnp.where(hit, -jnp.inf, v)
    sel_ref[...] = picked


def _nsa_cmp_select(dsrc, kcv, bias, agg, *, B, S, n_cmp, ncp):
    return pl.pallas_call(
        functools.partial(_cmp_kernel, n_cmp=n_cmp, ncp=ncp),
        grid=(B, S // Q_BLOCK),
        in_specs=[pl.BlockSpec((None, Q_BLOCK, 2 * LANES), lambda b, n: (b, n, 0)),
                  pl.BlockSpec((None, ncp, LANES), lambda b, n: (b, 0, 0)),
                  pl.BlockSpec((NSA_HEADS, None, Q_BLOCK, 2 * LANES), lambda b, n: (0, n % 16, 0, 0)),
                  pl.BlockSpec((ncp, LANES), lambda b, n: (0, 0))],
        out_specs=[pl.BlockSpec((None, Q_BLOCK, 2 * LANES), lambda b, n: (b, n, 0)),
                   pl.BlockSpec((None, LANES, Q_BLOCK), lambda b, n: (b, 0, n))],
        out_shape=[jax.ShapeDtypeStruct((B, S, 2 * LANES), BF16),
                   jax.ShapeDtypeStruct((B, LANES, S), F32)],
        scratch_shapes=[pltpu.VMEM((NSA_HEADS, Q_BLOCK, ncp), F32)],
        compiler_params=_cparams("arbitrary", "arbitrary"),
        name="nsa_cmp_select",
    )(dsrc, kcv, bias, agg)


def _merge_kernel(x_ref, g_ref, ya_ref, yb_ref, yc_ref, oc_ref, os_ref, ow_ref, dg_ref,
                  wg_ref, bg_ref, wb_ref, wo_ref, o_ref):
    x = x_ref[...]
    h = _rms(x, g_ref[...]).astype(BF16)
    tm = x.shape[0]
    lane = lax.broadcasted_iota(I32, (tm, LANES), 1)
    gd = jax.nn.sigmoid(dg_ref[...].astype(F32))
    yd = []
    for g in range(NSA_HEADS // 2):
        acc = jnp.zeros((tm, LANES), F32)
        for br, src in enumerate((oc_ref, os_ref, ow_ref)):
            ge = gd[:, 3 * (2 * g) + br:3 * (2 * g) + br + 1]
            go = gd[:, 3 * (2 * g + 1) + br:3 * (2 * g + 1) + br + 1]
            acc = acc + jnp.where(lane < 64, ge, go) * src[:, LANES * g:LANES * (g + 1)].astype(F32)
        yd.append(acc)
    y_d = jnp.concatenate(yd, axis=1).astype(BF16)
    ys = (ya_ref[...], yb_ref[...], yc_ref[...], y_d)
    merged = jnp.zeros((tm, D_MODEL), F32)
    for br in range(4):
        gate = jax.nn.sigmoid(jnp.dot(h, wg_ref[br], preferred_element_type=F32) + bg_ref[br:br + 1, :])
        merged = merged + gate * jnp.dot(ys[br], wb_ref[br], preferred_element_type=F32)
    o_ref[...] = x + jnp.dot(merged.astype(BF16), wo_ref[...], preferred_element_type=F32)


def _merge(x, g, ya, yb, yc, oc, osel, ow, dproj, wg, bg, wb, wo):
    T = x.shape[0]
    tm = 256

    def tile(width, col=0):
        return pl.BlockSpec((tm, width), lambda i: (i, col))

    def full(a):
        return pl.BlockSpec(a.shape, lambda i: (0,) * a.ndim)

    return pl.pallas_call(
        _merge_kernel,
        grid=(T // tm,),
        in_specs=[tile(D_MODEL), full(g), tile(256), tile(256), tile(256), tile(256), tile(256), tile(256),
                  tile(LANES, 5), full(wg), full(bg), full(wb), full(wo)],
        out_specs=tile(D_MODEL),
        out_shape=jax.ShapeDtypeStruct((T, D_MODEL), F32),
        compiler_params=_cparams("arbitrary"),
        name="merge",
    )(x, g, ya, yb, yc, oc, osel, ow, dproj, wg, bg, wb, wo)


def _route(hn, w_split, b_router):
    lane = lax.broadcasted_iota(I32, (hn.shape[0], LANES), 1)
    hi = hn.astype(BF16)
    lo = (hn - hi.astype(F32)).astype(BF16)
    a = jnp.dot(hi, w_split, preferred_element_type=F32)
    b = jnp.dot(lo, w_split[:, 0:LANES], preferred_element_type=F32)
    logits = (a[:, 0:LANES] + a[:, LANES:2 * LANES]) + b + b_router
    is_grp = (lane >= N_EXPERTS) & (lane < N_EXPERTS + N_GROUPS)
    lanef = lane.astype(F32)
    lg = jnp.where(is_grp, logits, -jnp.inf)
    mg = jnp.max(lg, axis=1, keepdims=True)
    gsel = jnp.min(jnp.where(lg == mg, lanef, 2.0 * LANES), axis=1, keepdims=True) - N_EXPERTS
    g_w = 1.0 / jnp.sum(jnp.where(is_grp, jnp.exp(lg - mg), 0.0), axis=1, keepdims=True)
    in_grp = (lane < N_EXPERTS) & ((lane >> 3).astype(F32) == gsel)
    v1 = jnp.where(in_grp, logits, -jnp.inf)
    m1 = jnp.max(v1, axis=1, keepdims=True)
    i1 = jnp.min(jnp.where(v1 == m1, lanef, 2.0 * LANES), axis=1, keepdims=True)
    v2 = jnp.where(lanef == i1, -jnp.inf, v1)
    m2 = jnp.max(v2, axis=1, keepdims=True)
    i2 = jnp.min(jnp.where(v2 == m2, lanef, 2.0 * LANES), axis=1, keepdims=True)
    e2 = jnp.exp(m2 - m1)
    w1 = g_w / (1.0 + e2)
    w2 = g_w * e2 / (1.0 + e2)
    first, second = lanef == i1, lanef == i2
    comb = jnp.where(first, w1, 0.0) + jnp.where(second, w2, 0.0)
    return comb, jnp.where(first | second, 1.0, 0.0)


def _moe_kernel(x_ref, g_ref, wr_ref, br_ref, wg_ref, wu_ref, wd_ref, fg_ref, o_ref,
                hn_ref, comb_ref, acc_ref, *, final_norm):
    e = pl.program_id(1)
    tm = x_ref.shape[0]
    lane = lax.broadcasted_iota(I32, (tm, LANES), 1)

    @pl.when(e == 0)
    def _():
        hn = _rms(x_ref[...], g_ref[...])
        hn_ref[...] = hn.astype(BF16)
        comb_ref[...] = _route(hn, wr_ref[...], br_ref[...])[0]
        acc_ref[...] = jnp.zeros_like(acc_ref)

    hn = hn_ref[...]
    gt = jnp.dot(hn, wg_ref[...], preferred_element_type=F32)
    up = jnp.dot(hn, wu_ref[...], preferred_element_type=F32)
    he = (gt * jax.nn.sigmoid(gt) * up).astype(BF16)
    ye = jnp.dot(he, wd_ref[...], preferred_element_type=F32)
    cw = jnp.sum(jnp.where(lane == e, comb_ref[...], 0.0), axis=1, keepdims=True)
    acc_ref[...] += cw * ye

    @pl.when(e == N_EXPERTS - 1)
    def _():
        y = x_ref[...] + acc_ref[...]
        if final_norm:
            y = _rms(y, fg_ref[...])
        o_ref[...] = y


def _moe(x, g, w_router, b_router, wg, wu, wd, final_g, final_norm):
    T = x.shape[0]
    tm = 1024
    return pl.pallas_call(
        functools.partial(_moe_kernel, final_norm=final_norm),
        grid=(T // tm, N_EXPERTS),
        in_specs=[pl.BlockSpec((tm, D_MODEL), lambda i, e: (i, 0)),
                  pl.BlockSpec((1, D_MODEL), lambda i, e: (0, 0)),
                  pl.BlockSpec((D_MODEL, 2 * LANES), lambda i, e: (0, 0)),
                  pl.BlockSpec((1, LANES), lambda i, e: (0, 0)),
                  pl.BlockSpec((None, D_MODEL, EXPERT_FF), lambda i, e: (e, 0, 0)),
                  pl.BlockSpec((None, D_MODEL, EXPERT_FF), lambda i, e: (e, 0, 0)),
                  pl.BlockSpec((None, EXPERT_FF, D_MODEL), lambda i, e: (e, 0, 0)),
                  pl.BlockSpec((1, D_MODEL), lambda i, e: (0, 0))],
        out_specs=pl.BlockSpec((tm, D_MODEL), lambda i, e: (i, 0)),
        out_shape=jax.ShapeDtypeStruct((T, D_MODEL), F32),
        scratch_shapes=[pltpu.VMEM((tm, D_MODEL), BF16),
                        pltpu.VMEM((tm, LANES), F32),
                        pltpu.VMEM((tm, D_MODEL), F32)],
        compiler_params=_cparams("arbitrary", "arbitrary"),
        name="moe",
    )(x, g, w_router, b_router, wg, wu, wd, final_g)


def _slot_matrix(pos_t, val_t, ntok, half):
    nslot = MOE_CAP // 2
    slot = (lax.broadcasted_iota(I32, (nslot, ntok), 0) + half * nslot).astype(F32)
    blocks = []
    for e in range(N_EXPERTS):
        hit = pos_t[e:e + 1, :] == slot
        blocks.append(jnp.where(hit, 1.0 if val_t is None else val_t[e:e + 1, :], 0.0).astype(BF16))
    return jnp.concatenate(blocks, axis=0)


def _dispatch_kernel(x_ref, g_ref, wr_ref, br_ref, tri_ref, xd_ref, pos_ref, wt_ref, cnt_ref):
    tm = x_ref.shape[0]
    hn = _rms(x_ref[...], g_ref[...])
    comb, routed = _route(hn, wr_ref[...], br_ref[...])
    rank = jnp.dot(tri_ref[...], routed.astype(BF16), preferred_element_type=F32)
    pos = jnp.where(routed > 0.0, rank, -1.0)
    pos_t = pos.T
    pos_ref[...] = pos_t
    wt_ref[...] = comb.T
    counts = jnp.sum(routed, axis=0, keepdims=True)
    cnt_ref[...] = jnp.broadcast_to(counts, cnt_ref.shape)
    hnb = hn.astype(BF16)
    nslot = MOE_CAP // 2
    lower = jnp.dot(_slot_matrix(pos_t, None, tm, 0), hnb, preferred_element_type=F32)
    xd_ref[:, 0:nslot, :] = lower.reshape(N_EXPERTS, nslot, D_MODEL).astype(xd_ref.dtype)

    busy = jnp.max(counts) > nslot

    @pl.when(busy)
    def _():
        upper = jnp.dot(_slot_matrix(pos_t, None, tm, 1), hnb, preferred_element_type=F32)
        xd_ref[:, nslot:MOE_CAP, :] = upper.reshape(N_EXPERTS, nslot, D_MODEL).astype(xd_ref.dtype)

    @pl.when(jnp.logical_not(busy))
    def _():
        xd_ref[:, nslot:MOE_CAP, :] = jnp.zeros((N_EXPERTS, nslot, D_MODEL), xd_ref.dtype)


def _expert_kernel(busy_ref, x_ref, wg_ref, wu_ref, wd_ref, o_ref):
    ntile, _, nslot, _ = x_ref.shape

    def ffn(half):
        x = x_ref[:, half].reshape(ntile * nslot, D_MODEL)
        gt = jnp.dot(x, wg_ref[...], preferred_element_type=F32)
        up = jnp.dot(x, wu_ref[...], preferred_element_type=F32)
        he = (gt * jax.nn.sigmoid(gt) * up).astype(BF16)
        ye = jnp.dot(he, wd_ref[...], preferred_element_type=F32)
        o_ref[:, half] = ye.reshape(ntile, nslot, D_MODEL).astype(o_ref.dtype)

    ffn(0)
    busy = busy_ref[pl.program_id(0), pl.program_id(1)] > 0

    @pl.when(busy)
    def _():
        ffn(1)

    @pl.when(jnp.logical_not(busy))
    def _():
        o_ref[:, 1] = jnp.zeros((ntile, nslot, D_MODEL), o_ref.dtype)


def _combine_kernel(busy_ref, x_ref, yd_ref, pos_ref, wt_ref, fg_ref, o_ref, *, final_norm):
    tm = x_ref.shape[0]
    nslot = MOE_CAP // 2

    def gathered(half):
        weights = _slot_matrix(pos_ref[...], wt_ref[...], tm, half)
        yd = yd_ref[:, half * nslot:(half + 1) * nslot, :].reshape(N_EXPERTS * nslot, D_MODEL)
        return lax.dot_general(weights, yd, (((0,), (0,)), ((), ())), preferred_element_type=F32)

    o_ref[...] = x_ref[...] + gathered(0)

    @pl.when(busy_ref[pl.program_id(0)] > 0)
    def _():
        o_ref[...] += gathered(1)

    if final_norm:
        o_ref[...] = _rms(o_ref[...], fg_ref[...])


def _moe_dispatch(x, g, w_router, b_router):
    T = x.shape[0]
    tm = MOE_TILE
    nt = T // tm
    tri = jnp.asarray(np.tril(np.ones((tm, tm), np.float32), -1), BF16)
    xd, pos_t, w_t, cnt = pl.pallas_call(
        _dispatch_kernel,
        grid=(nt,),
        in_specs=[pl.BlockSpec((tm, D_MODEL), lambda i: (i, 0)),
                  pl.BlockSpec((1, D_MODEL), lambda i: (0, 0)),
                  pl.BlockSpec((D_MODEL, 2 * LANES), lambda i: (0, 0)),
                  pl.BlockSpec((1, LANES), lambda i: (0, 0)),
                  pl.BlockSpec((tm, tm), lambda i: (0, 0))],
        out_specs=[pl.BlockSpec((N_EXPERTS, None, MOE_CAP, D_MODEL), lambda i: (0, i, 0, 0)),
                   pl.BlockSpec((LANES, tm), lambda i: (0, i)),
                   pl.BlockSpec((LANES, tm), lambda i: (0, i)),
                   pl.BlockSpec((None, 8, LANES), lambda i: (i, 0, 0))],
        out_shape=[jax.ShapeDtypeStruct((N_EXPERTS, nt, MOE_CAP, D_MODEL), BF16),
                   jax.ShapeDtypeStruct((LANES, T), F32),
                   jax.ShapeDtypeStruct((LANES, T), F32),
                   jax.ShapeDtypeStruct((nt, 8, LANES), F32)],
        compiler_params=_cparams("arbitrary"),
        name="moe_dispatch",
    )(x, g, w_router, b_router, tri)
    return xd, pos_t, w_t, cnt[:, 0, 0:N_EXPERTS]


def _moe_finish(x, xd, pos_t, w_t, counts, wg, wu, wd, final_g, final_norm):
    T = x.shape[0]
    tm = MOE_TILE
    nt = T // tm
    tps = min(MOE_TILES_PER_STEP, nt)
    assert nt % tps == 0
    nslot = MOE_CAP // 2
    hot = counts > nslot
    busy_blocks = jnp.any(hot.reshape(nt // tps, tps, N_EXPERTS), axis=1).T.astype(I32)
    busy_tiles = jnp.any(hot, axis=1).astype(I32)
    halves = (N_EXPERTS, nt, 2, nslot, D_MODEL)
    yd = pl.pallas_call(
        _expert_kernel,
        grid_spec=pltpu.PrefetchScalarGridSpec(
            num_scalar_prefetch=1,
            grid=(N_EXPERTS, nt // tps),
            in_specs=[pl.BlockSpec((None, tps, 2, nslot, D_MODEL), lambda e, c, b: (e, c, 0, 0, 0)),
                      pl.BlockSpec((None, D_MODEL, EXPERT_FF), lambda e, c, b: (e, 0, 0)),
                      pl.BlockSpec((None, D_MODEL, EXPERT_FF), lambda e, c, b: (e, 0, 0)),
                      pl.BlockSpec((None, EXPERT_FF, D_MODEL), lambda e, c, b: (e, 0, 0))],
            out_specs=pl.BlockSpec((None, tps, 2, nslot, D_MODEL), lambda e, c, b: (e, c, 0, 0, 0))),
        out_shape=jax.ShapeDtypeStruct(halves, BF16),
        compiler_params=_cparams("arbitrary", "arbitrary"),
        name="moe_experts",
    )(busy_blocks, xd.reshape(halves), wg, wu, wd)
    return pl.pallas_call(
        functools.partial(_combine_kernel, final_norm=final_norm),
        grid_spec=pltpu.PrefetchScalarGridSpec(
            num_scalar_prefetch=1,
            grid=(nt,),
            in_specs=[pl.BlockSpec((tm, D_MODEL), lambda i, b: (i, 0)),
                      pl.BlockSpec((N_EXPERTS, None, MOE_CAP, D_MODEL), lambda i, b: (0, i, 0, 0)),
                      pl.BlockSpec((LANES, tm), lambda i, b: (0, i)),
                      pl.BlockSpec((LANES, tm), lambda i, b: (0, i)),
                      pl.BlockSpec((1, D_MODEL), lambda i, b: (0, 0))],
            out_specs=pl.BlockSpec((tm, D_MODEL), lambda i, b: (i, 0))),
        out_shape=jax.ShapeDtypeStruct((T, D_MODEL), F32),
        compiler_params=_cparams("arbitrary"),
        name="moe_combine",
    )(busy_tiles, x, yd.reshape(N_EXPERTS, nt, MOE_CAP, D_MODEL), pos_t, w_t, final_g)


def _block_diag(w):
    nb, bs, _ = w.shape
    out = jnp.zeros((nb * bs, nb * bs), w.dtype)
    for i in range(nb):
        out = out.at[i * bs:(i + 1) * bs, i * bs:(i + 1) * bs].set(w[i])
    return out


def _selection_aggregator_np(n_cmp, n_slc):
    ratio_s = NSA_SLC_LEN // NSA_CMP_STRIDE
    ratio_c = NSA_CMP_LEN // NSA_CMP_STRIDE
    w = np.convolve(np.ones(ratio_s), np.ones(ratio_c)).astype(np.float32)
    cj = np.arange(n_slc)[:, None] * ratio_s - (ratio_c - 1) + np.arange(w.size)[None, :]
    jj = np.broadcast_to(np.arange(n_slc)[:, None], cj.shape)
    ww = np.broadcast_to(w[None, :], cj.shape)
    keep = (cj >= 0) & (cj < n_cmp)
    agg = np.zeros((n_cmp, n_slc), np.float32)
    np.add.at(agg, (cj[keep], jj[keep]), ww[keep])
    return agg


def _row(v):
    return v.reshape(1, -1).astype(F32)


def _diff_lambda_init(layer):
    return 0.8 - 0.6 * math.exp(-0.3 * layer)


def _mixer_layer(x, l, B, S, tiles, p):
    (bias_swa, bias_diff, bias_slc, bias_win, bias_cmp, agg_pad, n_cmp, ncp) = tiles
    w_in = jnp.pad(p["w_in"][l], ((0, 0), (0, N_IN_PAD - p["w_in"].shape[2]))).astype(BF16)
    w_t = jnp.concatenate([w_in[:, 1536:1792], w_in[:, 2176:2304]], axis=1).T
    a_xg, b_qkv, c_qkv, d_all, vals_t = _in_proj(x, _row(p["mix_norm_g"][l]), w_in, w_t)

    y_a = _rglru(a_xg, p["conv_w"][l].astype(F32), _row(p["conv_b"][l]),
                 _block_diag(p["lru_w_a"][l]).astype(BF16), _row(p["lru_b_a"][l]),
                 _block_diag(p["lru_w_x"][l]).astype(BF16), _row(p["lru_b_x"][l]),
                 _row(p["lru_lambda"][l]), B, S)

    b3 = b_qkv.reshape(B, S, 512)
    swa_cfg = tuple(((h // 2) * 64, (h % 2) != (h // 2), h // 2) for h in range(SWA_HEADS))
    y_b = _banded_attention(b3, 0, b3, 2, b3, 3, bias_swa, _row(p["swa_sinks"][l]),
                            B=B, S=S, window=SWA_WINDOW, q_cfg=swa_cfg)

    c3 = c_qkv.reshape(B, S, 768)
    lam4 = jnp.stack([p["diff_lq1"][l], p["diff_lk1"][l], p["diff_lq2"][l], p["diff_lk2"][l]]).astype(F32)
    gsub = jnp.tile(p["diff_subln_g"][l].reshape(-1, 1).astype(F32), (2, 1))
    y_c = _flash_attention("diff", c3, 0, c3, 1, vals_t, 256, 0, bias_diff, (lam4, gsub), B=B, S=S,
                           lambda_init=_diff_lambda_init(l))

    d3 = d_all.reshape(B, S, 768)
    nchunk = S // NSA_CMP_STRIDE
    ck = d3[:, :, 256:320].reshape(B, nchunk, NSA_CMP_STRIDE * HEAD_DIM)
    cv = d3[:, :, 320:384].reshape(B, nchunk, NSA_CMP_STRIDE * HEAD_DIM)

    def pos8(a):
        return jnp.broadcast_to(a.reshape(1, -1), (8, a.size)).astype(BF16)

    w1k = jnp.pad(p["nsa_w1_k"][l], ((0, 0), (0, HEAD_DIM))).astype(BF16)
    w1v = jnp.pad(p["nsa_w1_v"][l], ((0, 0), (HEAD_DIM, 0))).astype(BF16)
    w2 = _block_diag(jnp.stack([p["nsa_w2_k"][l], p["nsa_w2_v"][l]])).astype(BF16)
    kcv = _nsa_compress(ck, cv, pos8(p["nsa_pos_k"][l]), pos8(p["nsa_pos_v"][l]), w1k, w1v, w2, ncp)
    o_c, sel = _nsa_cmp_select(d3, kcv, bias_cmp, agg_pad, B=B, S=S, n_cmp=n_cmp, ncp=ncp)
    o_s = _flash_attention("nsa", d3, 0, d3, 3, vals_t, 128, 2, bias_slc, (sel,), B=B, S=S)
    win_cfg = tuple((0, h % 2 == 1, 1) for h in range(NSA_HEADS))
    o_w = _banded_attention(d3, 0, d3, 4, None, None, bias_win, None, B=B, S=S, window=NSA_WINDOW, q_cfg=win_cfg)

    T = B * S
    return _merge(x, _row(p["mix_norm_g"][l]), y_a, y_b.reshape(T, 256), y_c.reshape(T, 256),
                  o_c.reshape(T, 256), o_s.reshape(T, 256), o_w.reshape(T, 256), d_all,
                  p["w_gate"][l].astype(BF16), p["b_gate"][l].astype(F32), p["w_branch"][l].astype(BF16),
                  p["w_out"][l].astype(BF16))


def _moe_layer(x, l, p, final_norm):
    w_router = jnp.concatenate([p["w_router_exp"][l], p["w_router_grp"][l]], axis=1)
    w_router = jnp.pad(w_router, ((0, 0), (0, LANES - w_router.shape[1]))).astype(F32)
    w_hi = w_router.astype(BF16)
    w_router = jnp.concatenate([w_hi, (w_router - w_hi.astype(F32)).astype(BF16)], axis=1)
    b_router = jnp.concatenate([p["b_router_exp"][l], p["b_router_grp"][l]])
    b_router = jnp.pad(b_router, (0, LANES - b_router.shape[0])).reshape(1, LANES).astype(F32)
    g = _row(p["ffn_norm_g"][l])
    fg = _row(p["final_norm_g"])
    wg, wu, wd = (p[k][l].astype(BF16) for k in ("w_exp_gate", "w_exp_up", "w_exp_down"))
    xd, pos_t, w_t, counts = _moe_dispatch(x, g, w_router, b_router)
    return lax.cond(jnp.max(counts) > MOE_CAP,
                    lambda: _moe(x, g, w_router, b_router, wg, wu, wd, fg, final_norm),
                    lambda: _moe_finish(x, xd, pos_t, w_t, counts, wg, wu, wd, fg, final_norm))


def _forward(p):
    x = p["x"]
    B, S, D = x.shape
    depth = p["w_in"].shape[0]
    assert D == D_MODEL and S % (16 * Q_BLOCK) == 0 and S // NSA_SLC_LEN <= LANES
    rel = p["rel_bias"].astype(F32)
    n_cmp = (S - NSA_CMP_LEN) // NSA_CMP_STRIDE + 1
    ncp = LANES + S // NSA_CMP_STRIDE
    agg = _selection_aggregator_np(n_cmp, S // NSA_SLC_LEN)
    agg_pad = np.zeros((ncp, LANES), np.float32)
    agg_pad[LANES:LANES + n_cmp, :agg.shape[1]] = agg
    tiles = (
        _bias_tiles(rel, _banded_buckets(1), 0, SWA_HEADS, False),
        _bias_tiles(rel, _flash_buckets(), SWA_HEADS, DIFF_HEADS, True, LOG2E),
        _bias_tiles(rel, _flash_buckets(), SWA_HEADS + DIFF_HEADS, NSA_HEADS, True, LOG2E),
        _bias_tiles(rel, _banded_buckets(NSA_WINDOW // Q_BLOCK), SWA_HEADS + DIFF_HEADS, NSA_HEADS, False),
        _bias_tiles(rel, _cmp_buckets(), SWA_HEADS + DIFF_HEADS, NSA_HEADS, True),
        jnp.asarray(agg_pad, BF16), n_cmp, ncp,
    )
    xt = x.reshape(B * S, D).astype(F32)
    for l in range(depth):
        xt = _mixer_layer(xt, l, B, S, tiles, p)
        xt = _moe_layer(xt, l, p, final_norm=(l == depth - 1))
    return xt.reshape(B, S, D).astype(x.dtype)


def kernel(x, rel_bias, final_norm_g, mix_norm_g, w_in, conv_w, conv_b, lru_w_a, lru_b_a, lru_w_x, lru_b_x,
           lru_lambda, swa_sinks, diff_lq1, diff_lk1, diff_lq2, diff_lk2, diff_subln_g, nsa_pos_k, nsa_w1_k,
           nsa_w2_k, nsa_pos_v, nsa_w1_v, nsa_w2_v, w_gate, b_gate, w_branch, w_out, ffn_norm_g, w_router_grp,
           b_router_grp, w_router_exp, b_router_exp, w_exp_gate, w_exp_up, w_exp_down):
    return _forward(dict(
        x=x, rel_bias=rel_bias, final_norm_g=final_norm_g, mix_norm_g=mix_norm_g, w_in=w_in, conv_w=conv_w,
        conv_b=conv_b, lru_w_a=lru_w_a, lru_b_a=lru_b_a, lru_w_x=lru_w_x, lru_b_x=lru_b_x, lru_lambda=lru_lambda,
        swa_sinks=swa_sinks, diff_lq1=diff_lq1, diff_lk1=diff_lk1, diff_lq2=diff_lq2, diff_lk2=diff_lk2,
        diff_subln_g=diff_subln_g, nsa_pos_k=nsa_pos_k, nsa_w1_k=nsa_w1_k, nsa_w2_k=nsa_w2_k, nsa_pos_v=nsa_pos_v,
        nsa_w1_v=nsa_w1_v, nsa_w2_v=nsa_w2_v, w_gate=w_gate, b_gate=b_gate, w_branch=w_branch, w_out=w_out,
        ffn_norm_g=ffn_norm_g, w_router_grp=w_router_grp, b_router_grp=b_router_grp, w_router_exp=w_router_exp,
        b_router_exp=b_router_exp, w_exp_gate=w_exp_gate, w_exp_up=w_exp_up, w_exp_down=w_exp_down))
```

```python
import functools
import math

import numpy as np
import jax
import jax.numpy as jnp
from jax import lax
from jax.experimental import pallas as pl
from jax.experimental.pallas import tpu as pltpu

F32 = jnp.float32
BF16 = jnp.bfloat16
I32 = jnp.int32

D_MODEL = 1024
HEAD_DIM = 64
MIX_WIDTH = 256
LRU_C = 8.0
CONV_WIDTH = 4
SWA_HEADS = 4
SWA_WINDOW = 128
DIFF_HEADS = 4
DIFF_DIM = 32
DIFF_SUBLN_EPS = 1e-5
NSA_HEADS = 4
NSA_CMP_LEN = 32
NSA_CMP_STRIDE = 16
NSA_SLC_LEN = 64
NSA_TOPK = 16
NSA_WINDOW = 512
NSA_FORCED_SCORE = 1e4
REL_BUCKETS = 32
REL_MAX_DIST = 128
N_GROUPS = 4
EXPERTS_PER_GROUP = 8
N_EXPERTS = 32
EXPERT_FF = 256
NORM_EPS = 1e-6
NEG_INF = -1e30
TINY = 1e-30
LOG2E = math.log2(math.e)

LANES = 128
Q_BLOCK = 128
BANDED_Q_BLOCKS = 2
FLASH_BLOCK = 256
MOE_TILE = 512
MOE_CAP = 128
MOE_TILES_PER_STEP = 8
FLASH_MAX_GROWTH = 64.0
FLASH_FAR_BLOCKS = 8
N_IN_PAD = 2560
VMEM_LIMIT = 56 * 1024 * 1024

_NT = (((1,), (1,)), ((), ()))


def _cparams(*sem):
    return pltpu.CompilerParams(dimension_semantics=sem, vmem_limit_bytes=VMEM_LIMIT)


def _t5_bucket_np(dist):
    n = np.maximum(dist, 0)
    exact = REL_BUCKETS // 2
    scaled = (np.log(np.maximum(n, 1).astype(np.float32) / np.float32(exact))
              / np.float32(math.log(REL_MAX_DIST / exact))).astype(np.float32)
    large = np.minimum(exact + (scaled * np.float32(REL_BUCKETS - exact)).astype(np.int32), REL_BUCKETS - 1)
    return np.where(n < exact, n, large).astype(np.int32)


def _bias_tile_kernel(tab_ref, idx_ref, o_ref, *, head0, shift, scale):
    h = pl.program_id(0) + head0
    idx = idx_ref[0]
    acc = jnp.zeros(idx.shape, F32)
    for b in range(REL_BUCKETS):
        acc = jnp.where(idx == b, tab_ref[b, h], acc)
    if shift:
        acc = acc - tab_ref[REL_BUCKETS - 1, h]
    if scale != 1.0:
        acc = acc * scale
    o_ref[0, 0] = acc


def _bias_tiles(rel_bias, buckets, head0, nheads, shift, scale=1.0):
    R, M, N = buckets.shape
    return pl.pallas_call(
        functools.partial(_bias_tile_kernel, head0=head0, shift=shift, scale=scale),
        grid=(nheads, R),
        in_specs=[pl.BlockSpec(memory_space=pltpu.SMEM),
                  pl.BlockSpec((1, M, N), lambda h, r: (r, 0, 0))],
        out_specs=pl.BlockSpec((1, 1, M, N), lambda h, r: (h, r, 0, 0)),
        out_shape=jax.ShapeDtypeStruct((nheads, R, M, N), F32),
        compiler_params=_cparams("arbitrary", "arbitrary"),
        name="bias_tiles",
    )(rel_bias, jnp.asarray(buckets))


def _banded_buckets(nback):
    i = np.arange(BANDED_Q_BLOCKS * Q_BLOCK)[None, :]
    j = np.arange((nback + BANDED_Q_BLOCKS) * Q_BLOCK)[:, None]
    return _t5_bucket_np(i - j + nback * Q_BLOCK)[None]


def _flash_buckets():
    k = np.arange(FLASH_BLOCK)[:, None]
    q = np.arange(FLASH_BLOCK)[None, :]
    return np.stack([_t5_bucket_np(q - k), _t5_bucket_np(q - k + FLASH_BLOCK)])


def _cmp_buckets():
    i = np.arange(Q_BLOCK)[None, :, None]
    c2 = np.arange(2 * LANES)[None, None, :]
    r = np.arange(16)[:, None, None]
    dist = i - NSA_CMP_STRIDE * (c2 - LANES - 8 * r) - (NSA_CMP_LEN - 1)
    return _t5_bucket_np(dist)


def _rms(x, g):
    return x * lax.rsqrt(jnp.mean(x * x, axis=-1, keepdims=True) + NORM_EPS) * g


def _proj_kernel(x_ref, g_ref, w_ref, wt_ref, oa_ref, ob_ref, oc_ref, od_ref, ot_ref):
    h = _rms(x_ref[...], g_ref[...]).astype(BF16)
    oa_ref[...] = jnp.dot(h, w_ref[:, 0:512], preferred_element_type=F32)
    ob_ref[...] = jnp.dot(h, w_ref[:, 512:1024], preferred_element_type=F32).astype(BF16)
    oc_ref[...] = jnp.dot(h, w_ref[:, 1024:1792], preferred_element_type=F32).astype(BF16)
    od_ref[...] = jnp.dot(h, w_ref[:, 1792:2560], preferred_element_type=F32).astype(BF16)
    ot_ref[...] = lax.dot_general(wt_ref[...], h, _NT, preferred_element_type=F32).astype(BF16)


def _in_proj(x, g, w, wt):
    T = x.shape[0]
    tm = 512
    nt = wt.shape[0]
    return pl.pallas_call(
        _proj_kernel,
        grid=(T // tm,),
        in_specs=[pl.BlockSpec((tm, D_MODEL), lambda i: (i, 0)),
                  pl.BlockSpec((1, D_MODEL), lambda i: (0, 0)),
                  pl.BlockSpec((D_MODEL, N_IN_PAD), lambda i: (0, 0)),
                  pl.BlockSpec((nt, D_MODEL), lambda i: (0, 0))],
        out_specs=[pl.BlockSpec((tm, 512), lambda i: (i, 0)),
                   pl.BlockSpec((tm, 512), lambda i: (i, 0)),
                   pl.BlockSpec((tm, 768), lambda i: (i, 0)),
                   pl.BlockSpec((tm, 768), lambda i: (i, 0)),
                   pl.BlockSpec((nt, tm), lambda i: (0, i))],
        out_shape=[jax.ShapeDtypeStruct((T, 512), F32),
                   jax.ShapeDtypeStruct((T, 512), BF16),
                   jax.ShapeDtypeStruct((T, 768), BF16),
                   jax.ShapeDtypeStruct((T, 768), BF16),
                   jax.ShapeDtypeStruct((nt, T), BF16)],
        compiler_params=_cparams("arbitrary"),
        name="in_proj",
    )(x, g, w, wt)


def _lru_kernel(xg_ref, cw_ref, cb_ref, wa_ref, ba_ref, wx_ref, bx_ref, lam_ref, o_ref,
                ext_ref, h_ref, *, Tt):
    t = pl.program_id(1)

    W = MIX_WIDTH

    @pl.when(t == 0)
    def _():
        ext_ref[0:8, :] = jnp.zeros((8, W), F32)
        h_ref[...] = jnp.zeros_like(h_ref)

    x = xg_ref[:, 0:W]
    gate = xg_ref[:, W:2 * W]
    ext_ref[8:Tt + 8, :] = x
    row = lax.broadcasted_iota(I32, (Tt, W), 0)
    xc = cb_ref[...] + x * cw_ref[CONV_WIDTH - 1:CONV_WIDTH, :]
    for s in range(1, CONV_WIDTH):
        xc = xc + ext_ref[8 - s:8 - s + Tt, :] * cw_ref[CONV_WIDTH - 1 - s:CONV_WIDTH - s, :]
    ext_ref[0:8, :] = ext_ref[Tt:Tt + 8, :]

    xcb = xc.astype(BF16)
    r = jax.nn.sigmoid(jnp.dot(xcb, wa_ref[...], preferred_element_type=F32) + ba_ref[...])
    ig = jax.nn.sigmoid(jnp.dot(xcb, wx_ref[...], preferred_element_type=F32) + bx_ref[...])
    z = -lam_ref[...]
    softplus = jnp.maximum(z, 0.0) + jnp.log1p(jnp.exp(-jnp.abs(z)))
    log_a = (-LRU_C) * r * softplus
    a = jnp.exp(log_a)
    mult = jnp.sqrt(1.0 - a * a)
    mult = jnp.where((row + t * Tt) == 0, 1.0, mult)
    b = mult * ig * xc

    d = 1
    while d < Tt:
        keep = row >= d
        b = jnp.where(keep, a * pltpu.roll(b, d, 0) + b, b)
        a = jnp.where(keep, a * pltpu.roll(a, d, 0), a)
        d *= 2
    hs = a * h_ref[7:8, :] + b
    h_ref[...] = hs[Tt - 8:Tt]
    o_ref[...] = (hs * jax.nn.gelu(gate)).astype(o_ref.dtype)


def _rglru(a_xg, conv_w, conv_b, wa_bd, b_a, wx_bd, b_x, lam, B, S):
    Tt = 256
    W = MIX_WIDTH
    vec = pl.BlockSpec((1, W), lambda b, t: (0, 0))
    mat = pl.BlockSpec((W, W), lambda b, t: (0, 0))
    out = pl.pallas_call(
        functools.partial(_lru_kernel, Tt=Tt),
        grid=(B, S // Tt),
        in_specs=[pl.BlockSpec((None, Tt, 2 * W), lambda b, t: (b, t, 0)),
                  pl.BlockSpec((CONV_WIDTH, W), lambda b, t: (0, 0)),
                  vec, mat, vec, mat, vec, vec],
        out_specs=pl.BlockSpec((None, Tt, W), lambda b, t: (b, t, 0)),
        out_shape=jax.ShapeDtypeStruct((B, S, W), BF16),
        scratch_shapes=[pltpu.VMEM((Tt + 8, W), F32), pltpu.VMEM((8, W), F32)],
        compiler_params=_cparams("arbitrary", "arbitrary"),
        name="rglru",
    )(a_xg.reshape(B, S, 2 * W), conv_w, conv_b, wa_bd, b_a, wx_bd, b_x, lam)
    return out.reshape(B * S, W)


def _lane_window(q128, lo, width, roll, scale):
    q = q128 * scale
    if roll:
        q = pltpu.roll(q, LANES // 2, 1)
    lane = lax.broadcasted_iota(I32, q.shape, 1)
    return jnp.where((lane >= lo) & (lane < lo + width), q, 0.0).astype(BF16)


def _pair_halves(even, odd, even_half, odd_half):
    if even_half == 1:
        even = pltpu.roll(even, LANES // 2, 1)
    if odd_half == 0:
        odd = pltpu.roll(odd, LANES // 2, 1)
    lane = lax.broadcasted_iota(I32, even.shape, 1)
    return jnp.where(lane < LANES // 2, even, odd)


def _banded_kernel(*refs, nback, window, q_cfg, has_sink, shared_kv):
    nk = nback + BANDED_Q_BLOCKS
    TQ = BANDED_Q_BLOCKS * Q_BLOCK
    q_ref = refs[0]
    k_refs = refs[1:1 + nk]
    pos = 1 + nk
    if shared_kv:
        v_refs = k_refs
    else:
        v_refs = refs[pos:pos + nk]
        pos += nk
    bias_ref = refs[pos]
    pos += 1
    sink_ref = None
    if has_sink:
        sink_ref = refs[pos]
        pos += 1
    o_ref = refs[pos]

    n = pl.program_id(1)
    KW = nk * Q_BLOCK
    kcat = jnp.concatenate([k_refs[nk - 1 - i][...] for i in range(nk)], axis=0)
    vcat = kcat if shared_kv else jnp.concatenate([v_refs[nk - 1 - i][...] for i in range(nk)], axis=0)
    vt = vcat.astype(F32).T.astype(BF16)
    jj = lax.broadcasted_iota(I32, (KW, TQ), 0)
    ii = lax.broadcasted_iota(I32, (KW, TQ), 1)
    dist = ii - jj + nback * Q_BLOCK
    valid = (dist >= 0) & (dist < window) & (jj >= (nback - BANDED_Q_BLOCKS * n) * Q_BLOCK)
    scale = HEAD_DIM ** -0.5

    def logits(h):
        lo, roll, _ = q_cfg[h]
        g = h // 2
        qx = _lane_window(q_ref[:, LANES * g:LANES * (g + 1)].astype(F32), lo, HEAD_DIM, roll, scale)
        s = lax.dot_general(kcat, qx, _NT, preferred_element_type=F32) + bias_ref[h, 0]
        return jnp.where(valid, s, NEG_INF)

    def softmax(h, s):
        m = jnp.max(s, axis=0, keepdims=True)
        if has_sink:
            sink = sink_ref[0:1, h:h + 1]
            m = jnp.maximum(m, sink)
        p = jnp.exp(s - m)
        den = jnp.sum(p, axis=0, keepdims=True)
        if has_sink:
            den = den + jnp.exp(sink - m)
        return p.astype(BF16), 1.0 / jnp.maximum(den, TINY)

    def values(h, p, inv):
        half = q_cfg[h][2]
        return jnp.dot(vt[HEAD_DIM * half:HEAD_DIM * (half + 1)], p, preferred_element_type=F32) * inv

    nh = len(q_cfg)
    ss = [logits(h) for h in range(nh)]
    pr = [softmax(h, ss[h]) for h in range(nh)]
    outs = [values(h, *pr[h]) for h in range(nh)]
    for g in range(nh // 2):
        pair = jnp.concatenate([outs[2 * g], outs[2 * g + 1]], axis=0)
        o_ref[:, LANES * g:LANES * (g + 1)] = pair.T.astype(o_ref.dtype)


def _banded_attention(qsrc, q_col, ksrc, k_col, vsrc, v_col, bias, sinks, *, B, S, window, q_cfg):
    nback = -(-window // Q_BLOCK)
    nq = BANDED_Q_BLOCKS
    nk = nback + nq
    TQ = nq * Q_BLOCK
    shared_kv = vsrc is None
    has_sink = sinks is not None

    def kv_spec(col, i):
        return pl.BlockSpec((None, Q_BLOCK, LANES), lambda b, n: (b, jnp.maximum(nq * n + nq - 1 - i, 0), col))

    in_specs = [pl.BlockSpec((None, TQ, 2 * LANES), lambda b, n: (b, n, q_col))]
    args = [qsrc]
    in_specs += [kv_spec(k_col, i) for i in range(nk)]
    args += [ksrc] * nk
    if not shared_kv:
        in_specs += [kv_spec(v_col, i) for i in range(nk)]
        args += [vsrc] * nk
    in_specs.append(pl.BlockSpec(bias.shape, lambda b, n: (0, 0, 0, 0)))
    args.append(bias)
    if has_sink:
        in_specs.append(pl.BlockSpec(sinks.shape, lambda b, n: (0, 0)))
        args.append(sinks)
    return pl.pallas_call(
        functools.partial(_banded_kernel, nback=nback, window=window, q_cfg=q_cfg,
                          has_sink=has_sink, shared_kv=shared_kv),
        grid=(B, S // TQ),
        in_specs=in_specs,
        out_specs=pl.BlockSpec((None, TQ, 2 * LANES), lambda b, n: (b, n, 0)),
        out_shape=jax.ShapeDtypeStruct((B, S, 2 * LANES), BF16),
        compiler_params=_cparams("arbitrary", "arbitrary"),
        name="banded_attn",
    )(*args)


def _flash_kernel(*refs, mode, lambda_init):
    TB = FLASH_BLOCK
    if mode == "diff":
        (q_ref, k_ref, vt_ref, bias_ref, lam_ref, g_ref, o_ref,
         qx_ref, m_ref, l_ref, acc_ref) = refs
        sel_ref = None
        maps = [dict(g=h // 2, lo=(h % 2) * 64 + 32 * mm, w=DIFF_DIM, roll=False, kg=h // 2, vrow=HEAD_DIM * h, h=h)
                for h in range(DIFF_HEADS) for mm in range(2)]
        scale = DIFF_DIM ** -0.5
    else:
        (q_ref, k_ref, vt_ref, sel_ref, bias_ref, o_ref,
         qx_ref, m_ref, l_ref, acc_ref) = refs
        maps = [dict(g=h // 2, lo=0, w=HEAD_DIM, roll=(h % 2 == 1), kg=0, vrow=HEAD_DIM, h=h)
                for h in range(NSA_HEADS)]
        scale = HEAD_DIM ** -0.5
    n = pl.program_id(1)

    for mp, c in enumerate(maps):
        qx_ref[mp] = _lane_window(q_ref[:, LANES * c["g"]:LANES * (c["g"] + 1)].astype(F32),
                                  c["lo"], c["w"], c["roll"], scale * LOG2E)
    m_ref[...] = jnp.full(m_ref.shape, NEG_INF, F32)
    l_ref[...] = jnp.zeros_like(l_ref)
    acc_ref[...] = jnp.zeros_like(acc_ref)

    key = lax.broadcasted_iota(I32, (TB, TB), 0)
    qry = lax.broadcasted_iota(I32, (TB, TB), 1)

    def step(j, kind, TK=TB):
        koff = pl.multiple_of(j * TB, TB)
        valid = None
        if kind == "diag":
            valid = qry >= key
        if sel_ref is not None:
            per = TK // NSA_SLC_LEN
            base = j * (TB // NSA_SLC_LEN)
            rows = [jnp.broadcast_to(sel_ref[pl.ds(base + i, 1), :], (NSA_SLC_LEN, TB)) for i in range(per)]
            picked = jnp.concatenate(rows, axis=0) > 0.5
            valid = picked if valid is None else (valid & picked)

        def logits(mp):
            c = maps[mp]
            kt = k_ref[pl.ds(koff, TK), LANES * c["kg"]:LANES * (c["kg"] + 1)]
            s = lax.dot_general(kt, qx_ref[mp], _NT, preferred_element_type=F32)
            if kind == "near":
                s = s + bias_ref[c["h"], 1]
            elif kind == "diag":
                s = s + bias_ref[c["h"], 0]
            if valid is not None:
                s = jnp.where(valid, s, NEG_INF)
            return s

        def softmax(mp, s):
            m_prev = m_ref[mp]
            m_new = jnp.maximum(m_prev, jnp.max(s, axis=0, keepdims=True))
            alpha = jnp.exp2(m_prev - m_new)
            p = jnp.exp2(s - m_new)
            l_ref[mp] = alpha * l_ref[mp] + jnp.sum(p, axis=0, keepdims=True)
            m_ref[mp] = m_new
            return p.astype(BF16), alpha

        def values(mp, p, alpha):
            c = maps[mp]
            vt = vt_ref[c["vrow"]:c["vrow"] + HEAD_DIM, pl.ds(koff, TK)]
            acc_ref[mp] = alpha * acc_ref[mp] + jnp.dot(vt, p, preferred_element_type=F32)

        nm = len(maps)
        ss = [logits(mp) for mp in range(nm)]

        def two_pass():
            pa = [softmax(mp, ss[mp]) for mp in range(nm)]
            for mp in range(nm):
                values(mp, *pa[mp])

        if kind == "diag":
            two_pass()
            return

        stabs = [m_ref[mp] for mp in range(nm)]
        ps = [jnp.exp2(ss[mp] - stabs[mp]) for mp in range(nm)]
        m_news = [jnp.maximum(stabs[mp], jnp.max(ss[mp], axis=0, keepdims=True)) for mp in range(nm)]
        growth = m_news[0] - stabs[0]
        for mp in range(1, nm):
            growth = jnp.maximum(growth, m_news[mp] - stabs[mp])
        safe = jnp.max(growth) <= FLASH_MAX_GROWTH
        psums = [jnp.sum(ps[mp], axis=0, keepdims=True) for mp in range(nm)]
        pvs = [jnp.dot(vt_ref[maps[mp]["vrow"]:maps[mp]["vrow"] + HEAD_DIM, pl.ds(koff, TK)], ps[mp].astype(BF16),
                       preferred_element_type=F32) for mp in range(nm)]

        @pl.when(safe)
        def _():
            for mp in range(nm):
                alpha = jnp.exp2(stabs[mp] - m_news[mp])
                l_ref[mp] = (l_ref[mp] + psums[mp]) * alpha
                acc_ref[mp] = (acc_ref[mp] + pvs[mp]) * alpha
                m_ref[mp] = m_news[mp]

        @pl.when(jnp.logical_not(safe))
        def _():
            two_pass()

    step(n, "diag")

    @pl.when(n >= 1)
    def _():
        step(n - 1, "near")

    nfar = jnp.maximum(n - 1, 0)
    FW = FLASH_FAR_BLOCKS

    def far_body(j, carry):
        step(FW * j, "far", FW * TB)
        return carry

    lax.fori_loop(0, nfar // FW, far_body, 0)

    done = (nfar // FW) * FW
    rest = nfar - done
    size = FW // 2
    while size >= 1:
        def tail_step(size=size):
            step(done + rest - rest % (2 * size), "far", size * TB)

        pl.when(rest % (2 * size) >= size)(tail_step)
        size //= 2

    def normalised(mp):
        return acc_ref[mp] * (1.0 / jnp.maximum(l_ref[mp], TINY))

    if mode == "diff":
        lq = lam_ref[...]
        lam = (jnp.exp(jnp.sum(lq[0:1] * lq[1:2], axis=1, keepdims=True))
               - jnp.exp(jnp.sum(lq[2:3] * lq[3:4], axis=1, keepdims=True)) + lambda_init)
        for g in range(DIFF_HEADS // 2):
            res = []
            for h in (2 * g, 2 * g + 1):
                o = normalised(2 * h) - lam * normalised(2 * h + 1)
                ms = jnp.sum(o * o, axis=0, keepdims=True) * (1.0 / (2 * DIFF_DIM))
                res.append(o * lax.rsqrt(ms + DIFF_SUBLN_EPS))
            out = jnp.concatenate(res, axis=0) * g_ref[...] * (1.0 - lambda_init)
            o_ref[:, LANES * g:LANES * (g + 1)] = out.T.astype(o_ref.dtype)
    else:
        for g in range(NSA_HEADS // 2):
            out = jnp.concatenate([normalised(2 * g), normalised(2 * g + 1)], axis=0)
            o_ref[:, LANES * g:LANES * (g + 1)] = out.T.astype(o_ref.dtype)


def _flash_attention(mode, qsrc, q_col, ksrc, k_col, vt, vt_rows, vt_blk, bias, extra, *, B, S, lambda_init=0.0):
    TB = FLASH_BLOCK
    q_spec = pl.BlockSpec((None, TB, 2 * LANES), lambda b, n: (b, n, q_col))
    bias_spec = pl.BlockSpec(bias.shape, lambda b, n: (0, 0, 0, 0))
    vt_spec = pl.BlockSpec((vt_rows, S), lambda b, n: (vt_blk, b))
    if mode == "diff":
        lam4, gsub = extra
        nmaps = 2 * DIFF_HEADS
        in_specs = [q_spec,
                    pl.BlockSpec((None, S, 2 * LANES), lambda b, n: (b, 0, k_col)),
                    vt_spec, bias_spec,
                    pl.BlockSpec(lam4.shape, lambda b, n: (0, 0)),
                    pl.BlockSpec(gsub.shape, lambda b, n: (0, 0))]
        args = [qsrc, ksrc, vt, bias, lam4, gsub]
    else:
        (sel,) = extra
        nmaps = NSA_HEADS
        in_specs = [q_spec,
                    pl.BlockSpec((None, S, LANES), lambda b, n: (b, 0, k_col)),
                    vt_spec,
                    pl.BlockSpec((None, LANES, TB), lambda b, n: (b, 0, n)),
                    bias_spec]
        args = [qsrc, ksrc, vt, sel, bias]
    return pl.pallas_call(
        functools.partial(_flash_kernel, mode=mode, lambda_init=lambda_init),
        grid=(B, S // TB),
        in_specs=in_specs,
        out_specs=pl.BlockSpec((None, TB, 2 * LANES), lambda b, n: (b, n, 0)),
        out_shape=jax.ShapeDtypeStruct((B, S, 2 * LANES), BF16),
        scratch_shapes=[pltpu.VMEM((nmaps, TB, LANES), BF16),
                        pltpu.VMEM((nmaps, 1, TB), F32),
                        pltpu.VMEM((nmaps, 1, TB), F32),
                        pltpu.VMEM((nmaps, HEAD_DIM, TB), F32)],
        compiler_params=_cparams("arbitrary", "arbitrary"),
        name="flash_" + mode,
    )(*args)


def _compress_kernel(ck_ref, cv_ref, pk_ref, pv_ref, w1k_ref, w1v_ref, w2_ref, o_ref):
    half = NSA_CMP_STRIDE * HEAD_DIM
    ck = ck_ref[...]
    cv = cv_ref[...]
    nrow = ck.shape[0]

    def mm(a, w):
        return jnp.dot(a, w, preferred_element_type=F32)

    top = mm(ck, w1k_ref[0:half, :]) + mm(cv, w1v_ref[0:half, :])
    bot = mm(ck, w1k_ref[half:2 * half, :]) + mm(cv, w1v_ref[half:2 * half, :])
    posw = (mm(pk_ref[...], w1k_ref[...]) + mm(pv_ref[...], w1v_ref[...]))[0:1]
    pre = top + pltpu.roll(bot, nrow - 1, 0) + posw
    kcv = mm(jax.nn.gelu(pre).astype(BF16), w2_ref[...])
    o_ref[0:LANES, :] = jnp.zeros((LANES, LANES), o_ref.dtype)
    o_ref[LANES:LANES + nrow, :] = kcv.astype(o_ref.dtype)


def _nsa_compress(ck, cv, pk, pv, w1k, w1v, w2, ncp):
    B, nchunk, width = ck.shape
    assert ncp == LANES + nchunk

    def full(a):
        return pl.BlockSpec(a.shape, lambda b: (0,) * a.ndim)

    chunk = pl.BlockSpec((None, nchunk, width), lambda b: (b, 0, 0))
    return pl.pallas_call(
        _compress_kernel,
        grid=(B,),
        in_specs=[chunk, chunk, full(pk), full(pv), full(w1k), full(w1v), full(w2)],
        out_specs=pl.BlockSpec((None, ncp, LANES), lambda b: (b, 0, 0)),
        out_shape=jax.ShapeDtypeStruct((B, ncp, LANES), BF16),
        compiler_params=_cparams("arbitrary"),
        name="nsa_compress",
    )(ck, cv, pk, pv, w1k, w1v, w2)


def _cmp_kernel(q_ref, kcv_ref, bias_ref, agg_ref, oc_ref, sel_ref, s_scr, *, n_cmp, ncp):
    n = pl.program_id(1)
    QB = Q_BLOCK
    kcv = kcv_ref[...]
    cp = lax.broadcasted_iota(I32, (QB, ncp), 1)
    qpos = n * QB + lax.broadcasted_iota(I32, (QB, ncp), 0)
    valid = ((cp >= LANES) & (cp < LANES + n_cmp)
             & (qpos - NSA_CMP_STRIDE * (cp - LANES) - (NSA_CMP_LEN - 1) >= 0))
    woff = pl.multiple_of((n // 16) * LANES, LANES)
    scale = HEAD_DIM ** -0.5

    def logits(h):
        g = h // 2
        qx = _lane_window(q_ref[:, LANES * g:LANES * (g + 1)].astype(F32), 0, HEAD_DIM, h % 2 == 1, scale)
        s_scr[h] = lax.dot_general(qx, kcv, _NT, preferred_element_type=F32)
        s_scr[h, :, pl.ds(woff, 2 * LANES)] += bias_ref[h]

    def softmax(h):
        s = jnp.where(valid, s_scr[h], NEG_INF)
        m = jnp.max(s, axis=1, keepdims=True)
        p = jnp.where(valid, jnp.exp(s - m), 0.0)
        den = jnp.sum(p, axis=1, keepdims=True)
        return p * (1.0 / jnp.maximum(den, TINY))

    for h in range(NSA_HEADS):
        logits(h)
    ps = [softmax(h) for h in range(NSA_HEADS)]
    outs = [jnp.dot(p.astype(BF16), kcv, preferred_element_type=F32) for p in ps]
    for g in range(NSA_HEADS // 2):
        oc_ref[:, LANES * g:LANES * (g + 1)] = _pair_halves(outs[2 * g], outs[2 * g + 1], 1, 1).astype(oc_ref.dtype)

    psum = (ps[0] + ps[1]) + (ps[2] + ps[3])
    imp = jnp.dot(psum.astype(BF16), agg_ref[...], preferred_element_type=F32)

    v = imp.T
    jb = lax.broadcasted_iota(I32, v.shape, 0)
    qp = n * QB + lax.broadcasted_iota(I32, v.shape, 1)
    cur = qp >> 6
    forced = (jb == 0) | (jb == cur) | (jb == cur - 1)
    v = jnp.where(forced, -jnp.inf, jnp.where(jb * NSA_SLC_LEN <= qp, v, -1.0))
    rid = jb.astype(F32)
    picked = jnp.where(forced, 1.0, 0.0)
    for _ in range(NSA_TOPK - 3):
        mx = jnp.max(v, axis=0, keepdims=True)
        first = jnp.min(jnp.where(v == mx, rid, float(LANES)), axis=0, keepdims=True)
        hit = rid == first
        picked = jnp.where(hit, 1.0, picked)
        v = jnp.where(hit, -jnp.inf, v)
    sel_ref[...] = picked


def _nsa_cmp_select(dsrc, kcv, bias, agg, *, B, S, n_cmp, ncp):
    return pl.pallas_call(
        functools.partial(_cmp_kernel, n_cmp=n_cmp, ncp=ncp),
        grid=(B, S // Q_BLOCK),
        in_specs=[pl.BlockSpec((None, Q_BLOCK, 2 * LANES), lambda b, n: (b, n, 0)),
                  pl.BlockSpec((None, ncp, LANES), lambda b, n: (b, 0, 0)),
                  pl.BlockSpec((NSA_HEADS, None, Q_BLOCK, 2 * LANES), lambda b, n: (0, n % 16, 0, 0)),
                  pl.BlockSpec((ncp, LANES), lambda b, n: (0, 0))],
        out_specs=[pl.BlockSpec((None, Q_BLOCK, 2 * LANES), lambda b, n: (b, n, 0)),
                   pl.BlockSpec((None, LANES, Q_BLOCK), lambda b, n: (b, 0, n))],
        out_shape=[jax.ShapeDtypeStruct((B, S, 2 * LANES), BF16),
                   jax.ShapeDtypeStruct((B, LANES, S), F32)],
        scratch_shapes=[pltpu.VMEM((NSA_HEADS, Q_BLOCK, ncp), F32)],
        compiler_params=_cparams("arbitrary", "arbitrary"),
        name="nsa_cmp_select",
    )(dsrc, kcv, bias, agg)


def _merge_kernel(x_ref, g_ref, ya_ref, yb_ref, yc_ref, oc_ref, os_ref, ow_ref, dg_ref,
                  wg_ref, bg_ref, wb_ref, wo_ref, o_ref):
    x = x_ref[...]
    h = _rms(x, g_ref[...]).astype(BF16)
    tm = x.shape[0]
    lane = lax.broadcasted_iota(I32, (tm, LANES), 1)
    gd = jax.nn.sigmoid(dg_ref[...].astype(F32))
    yd = []
    for g in range(NSA_HEADS // 2):
        acc = jnp.zeros((tm, LANES), F32)
        for br, src in enumerate((oc_ref, os_ref, ow_ref)):
            ge = gd[:, 3 * (2 * g) + br:3 * (2 * g) + br + 1]
            go = gd[:, 3 * (2 * g + 1) + br:3 * (2 * g + 1) + br + 1]
            acc = acc + jnp.where(lane < 64, ge, go) * src[:, LANES * g:LANES * (g + 1)].astype(F32)
        yd.append(acc)
    y_d = jnp.concatenate(yd, axis=1).astype(BF16)
    ys = (ya_ref[...], yb_ref[...], yc_ref[...], y_d)
    merged = jnp.zeros((tm, D_MODEL), F32)
    for br in range(4):
        gate = jax.nn.sigmoid(jnp.dot(h, wg_ref[br], preferred_element_type=F32) + bg_ref[br:br + 1, :])
        merged = merged + gate * jnp.dot(ys[br], wb_ref[br], preferred_element_type=F32)
    o_ref[...] = x + jnp.dot(merged.astype(BF16), wo_ref[...], preferred_element_type=F32)


def _merge(x, g, ya, yb, yc, oc, osel, ow, dproj, wg, bg, wb, wo):
    T = x.shape[0]
    tm = 256

    def tile(width, col=0):
        return pl.BlockSpec((tm, width), lambda i: (i, col))

    def full(a):
        return pl.BlockSpec(a.shape, lambda i: (0,) * a.ndim)

    return pl.pallas_call(
        _merge_kernel,
        grid=(T // tm,),
        in_specs=[tile(D_MODEL), full(g), tile(256), tile(256), tile(256), tile(256), tile(256), tile(256),
                  tile(LANES, 5), full(wg), full(bg), full(wb), full(wo)],
        out_specs=tile(D_MODEL),
        out_shape=jax.ShapeDtypeStruct((T, D_MODEL), F32),
        compiler_params=_cparams("arbitrary"),
        name="merge",
    )(x, g, ya, yb, yc, oc, osel, ow, dproj, wg, bg, wb, wo)


def _route(hn, w_split, b_router):
    lane = lax.broadcasted_iota(I32, (hn.shape[0], LANES), 1)
    hi = hn.astype(BF16)
    lo = (hn - hi.astype(F32)).astype(BF16)
    a = jnp.dot(hi, w_split, preferred_element_type=F32)
    b = jnp.dot(lo, w_split[:, 0:LANES], preferred_element_type=F32)
    logits = (a[:, 0:LANES] + a[:, LANES:2 * LANES]) + b + b_router
    is_grp = (lane >= N_EXPERTS) & (lane < N_EXPERTS + N_GROUPS)
    lanef = lane.astype(F32)
    lg = jnp.where(is_grp, logits, -jnp.inf)
    mg = jnp.max(lg, axis=1, keepdims=True)
    gsel = jnp.min(jnp.where(lg == mg, lanef, 2.0 * LANES), axis=1, keepdims=True) - N_EXPERTS
    g_w = 1.0 / jnp.sum(jnp.where(is_grp, jnp.exp(lg - mg), 0.0), axis=1, keepdims=True)
    in_grp = (lane < N_EXPERTS) & ((lane >> 3).astype(F32) == gsel)
    v1 = jnp.where(in_grp, logits, -jnp.inf)
    m1 = jnp.max(v1, axis=1, keepdims=True)
    i1 = jnp.min(jnp.where(v1 == m1, lanef, 2.0 * LANES), axis=1, keepdims=True)
    v2 = jnp.where(lanef == i1, -jnp.inf, v1)
    m2 = jnp.max(v2, axis=1, keepdims=True)
    i2 = jnp.min(jnp.where(v2 == m2, lanef, 2.0 * LANES), axis=1, keepdims=True)
    e2 = jnp.exp(m2 - m1)
    w1 = g_w / (1.0 + e2)
    w2 = g_w * e2 / (1.0 + e2)
    first, second = lanef == i1, lanef == i2
    comb = jnp.where(first, w1, 0.0) + jnp.where(second, w2, 0.0)
    return comb, jnp.where(first | second, 1.0, 0.0)


def _moe_kernel(x_ref, g_ref, wr_ref, br_ref, wg_ref, wu_ref, wd_ref, fg_ref, o_ref,
                hn_ref, comb_ref, acc_ref, *, final_norm):
    e = pl.program_id(1)
    tm = x_ref.shape[0]
    lane = lax.broadcasted_iota(I32, (tm, LANES), 1)

    @pl.when(e == 0)
    def _():
        hn = _rms(x_ref[...], g_ref[...])
        hn_ref[...] = hn.astype(BF16)
        comb_ref[...] = _route(hn, wr_ref[...], br_ref[...])[0]
        acc_ref[...] = jnp.zeros_like(acc_ref)

    hn = hn_ref[...]
    gt = jnp.dot(hn, wg_ref[...], preferred_element_type=F32)
    up = jnp.dot(hn, wu_ref[...], preferred_element_type=F32)
    he = (gt * jax.nn.sigmoid(gt) * up).astype(BF16)
    ye = jnp.dot(he, wd_ref[...], preferred_element_type=F32)
    cw = jnp.sum(jnp.where(lane == e, comb_ref[...], 0.0), axis=1, keepdims=True)
    acc_ref[...] += cw * ye

    @pl.when(e == N_EXPERTS - 1)
    def _():
        y = x_ref[...] + acc_ref[...]
        if final_norm:
            y = _rms(y, fg_ref[...])
        o_ref[...] = y


def _moe(x, g, w_router, b_router, wg, wu, wd, final_g, final_norm):
    T = x.shape[0]
    tm = 1024
    return pl.pallas_call(
        functools.partial(_moe_kernel, final_norm=final_norm),
        grid=(T // tm, N_EXPERTS),
        in_specs=[pl.BlockSpec((tm, D_MODEL), lambda i, e: (i, 0)),
                  pl.BlockSpec((1, D_MODEL), lambda i, e: (0, 0)),
                  pl.BlockSpec((D_MODEL, 2 * LANES), lambda i, e: (0, 0)),
                  pl.BlockSpec((1, LANES), lambda i, e: (0, 0)),
                  pl.BlockSpec((None, D_MODEL, EXPERT_FF), lambda i, e: (e, 0, 0)),
                  pl.BlockSpec((None, D_MODEL, EXPERT_FF), lambda i, e: (e, 0, 0)),
                  pl.BlockSpec((None, EXPERT_FF, D_MODEL), lambda i, e: (e, 0, 0)),
                  pl.BlockSpec((1, D_MODEL), lambda i, e: (0, 0))],
        out_specs=pl.BlockSpec((tm, D_MODEL), lambda i, e: (i, 0)),
        out_shape=jax.ShapeDtypeStruct((T, D_MODEL), F32),
        scratch_shapes=[pltpu.VMEM((tm, D_MODEL), BF16),
                        pltpu.VMEM((tm, LANES), F32),
                        pltpu.VMEM((tm, D_MODEL), F32)],
        compiler_params=_cparams("arbitrary", "arbitrary"),
        name="moe",
    )(x, g, w_router, b_router, wg, wu, wd, final_g)


def _slot_matrix(pos_t, val_t, ntok, half):
    nslot = MOE_CAP // 2
    slot = (lax.broadcasted_iota(I32, (nslot, ntok), 0) + half * nslot).astype(F32)
    blocks = []
    for e in range(N_EXPERTS):
        hit = pos_t[e:e + 1, :] == slot
        blocks.append(jnp.where(hit, 1.0 if val_t is None else val_t[e:e + 1, :], 0.0).astype(BF16))
    return jnp.concatenate(blocks, axis=0)


def _dispatch_kernel(x_ref, g_ref, wr_ref, br_ref, tri_ref, xd_ref, pos_ref, wt_ref, cnt_ref):
    tm = x_ref.shape[0]
    hn = _rms(x_ref[...], g_ref[...])
    comb, routed = _route(hn, wr_ref[...], br_ref[...])
    rank = jnp.dot(tri_ref[...], routed.astype(BF16), preferred_element_type=F32)
    pos = jnp.where(routed > 0.0, rank, -1.0)
    pos_t = pos.T
    pos_ref[...] = pos_t
    wt_ref[...] = comb.T
    counts = jnp.sum(routed, axis=0, keepdims=True)
    cnt_ref[...] = jnp.broadcast_to(counts, cnt_ref.shape)
    hnb = hn.astype(BF16)
    nslot = MOE_CAP // 2
    lower = jnp.dot(_slot_matrix(pos_t, None, tm, 0), hnb, preferred_element_type=F32)
    xd_ref[:, 0:nslot, :] = lower.reshape(N_EXPERTS, nslot, D_MODEL).astype(xd_ref.dtype)

    busy = jnp.max(counts) > nslot

    @pl.when(busy)
    def _():
        upper = jnp.dot(_slot_matrix(pos_t, None, tm, 1), hnb, preferred_element_type=F32)
        xd_ref[:, nslot:MOE_CAP, :] = upper.reshape(N_EXPERTS, nslot, D_MODEL).astype(xd_ref.dtype)

    @pl.when(jnp.logical_not(busy))
    def _():
        xd_ref[:, nslot:MOE_CAP, :] = jnp.zeros((N_EXPERTS, nslot, D_MODEL), xd_ref.dtype)


def _expert_kernel(busy_ref, x_ref, wg_ref, wu_ref, wd_ref, o_ref):
    ntile, _, nslot, _ = x_ref.shape

    def ffn(half):
        x = x_ref[:, half].reshape(ntile * nslot, D_MODEL)
        gt = jnp.dot(x, wg_ref[...], preferred_element_type=F32)
        up = jnp.dot(x, wu_ref[...], preferred_element_type=F32)
        he = (gt * jax.nn.sigmoid(gt) * up).astype(BF16)
        ye = jnp.dot(he, wd_ref[...], preferred_element_type=F32)
        o_ref[:, half] = ye.reshape(ntile, nslot, D_MODEL).astype(o_ref.dtype)

    ffn(0)
    busy = busy_ref[pl.program_id(0), pl.program_id(1)] > 0

    @pl.when(busy)
    def _():
        ffn(1)

    @pl.when(jnp.logical_not(busy))
    def _():
        o_ref[:, 1] = jnp.zeros((ntile, nslot, D_MODEL), o_ref.dtype)


def _combine_kernel(busy_ref, x_ref, yd_ref, pos_ref, wt_ref, fg_ref, o_ref, *, final_norm):
    tm = x_ref.shape[0]
    nslot = MOE_CAP // 2

    def gathered(half):
        weights = _slot_matrix(pos_ref[...], wt_ref[...], tm, half)
        yd = yd_ref[:, half * nslot:(half + 1) * nslot, :].reshape(N_EXPERTS * nslot, D_MODEL)
        return lax.dot_general(weights, yd, (((0,), (0,)), ((), ())), preferred_element_type=F32)

    o_ref[...] = x_ref[...] + gathered(0)

    @pl.when(busy_ref[pl.program_id(0)] > 0)
    def _():
        o_ref[...] += gathered(1)

    if final_norm:
        o_ref[...] = _rms(o_ref[...], fg_ref[...])


def _moe_dispatch(x, g, w_router, b_router):
    T = x.shape[0]
    tm = MOE_TILE
    nt = T // tm
    tri = jnp.asarray(np.tril(np.ones((tm, tm), np.float32), -1), BF16)
    xd, pos_t, w_t, cnt = pl.pallas_call(
        _dispatch_kernel,
        grid=(nt,),
        in_specs=[pl.BlockSpec((tm, D_MODEL), lambda i: (i, 0)),
                  pl.BlockSpec((1, D_MODEL), lambda i: (0, 0)),
                  pl.BlockSpec((D_MODEL, 2 * LANES), lambda i: (0, 0)),
                  pl.BlockSpec((1, LANES), lambda i: (0, 0)),
                  pl.BlockSpec((tm, tm), lambda i: (0, 0))],
        out_specs=[pl.BlockSpec((N_EXPERTS, None, MOE_CAP, D_MODEL), lambda i: (0, i, 0, 0)),
                   pl.BlockSpec((LANES, tm), lambda i: (0, i)),
                   pl.BlockSpec((LANES, tm), lambda i: (0, i)),
                   pl.BlockSpec((None, 8, LANES), lambda i: (i, 0, 0))],
        out_shape=[jax.ShapeDtypeStruct((N_EXPERTS, nt, MOE_CAP, D_MODEL), BF16),
                   jax.ShapeDtypeStruct((LANES, T), F32),
                   jax.ShapeDtypeStruct((LANES, T), F32),
                   jax.ShapeDtypeStruct((nt, 8, LANES), F32)],
        compiler_params=_cparams("arbitrary"),
        name="moe_dispatch",
    )(x, g, w_router, b_router, tri)
    return xd, pos_t, w_t, cnt[:, 0, 0:N_EXPERTS]


def _moe_finish(x, xd, pos_t, w_t, counts, wg, wu, wd, final_g, final_norm):
    T = x.shape[0]
    tm = MOE_TILE
    nt = T // tm
    tps = min(MOE_TILES_PER_STEP, nt)
    assert nt % tps == 0
    nslot = MOE_CAP // 2
    hot = counts > nslot
    busy_blocks = jnp.any(hot.reshape(nt // tps, tps, N_EXPERTS), axis=1).T.astype(I32)
    busy_tiles = jnp.any(hot, axis=1).astype(I32)
    halves = (N_EXPERTS, nt, 2, nslot, D_MODEL)
    yd = pl.pallas_call(
        _expert_kernel,
        grid_spec=pltpu.PrefetchScalarGridSpec(
            num_scalar_prefetch=1,
            grid=(N_EXPERTS, nt // tps),
            in_specs=[pl.BlockSpec((None, tps, 2, nslot, D_MODEL), lambda e, c, b: (e, c, 0, 0, 0)),
                      pl.BlockSpec((None, D_MODEL, EXPERT_FF), lambda e, c, b: (e, 0, 0)),
                      pl.BlockSpec((None, D_MODEL, EXPERT_FF), lambda e, c, b: (e, 0, 0)),
                      pl.BlockSpec((None, EXPERT_FF, D_MODEL), lambda e, c, b: (e, 0, 0))],
            out_specs=pl.BlockSpec((None, tps, 2, nslot, D_MODEL), lambda e, c, b: (e, c, 0, 0, 0))),
        out_shape=jax.ShapeDtypeStruct(halves, BF16),
        compiler_params=_cparams("arbitrary", "arbitrary"),
        name="moe_experts",
    )(busy_blocks, xd.reshape(halves), wg, wu, wd)
    return pl.pallas_call(
        functools.partial(_combine_kernel, final_norm=final_norm),
        grid_spec=pltpu.PrefetchScalarGridSpec(
            num_scalar_prefetch=1,
            grid=(nt,),
            in_specs=[pl.BlockSpec((tm, D_MODEL), lambda i, b: (i, 0)),
                      pl.BlockSpec((N_EXPERTS, None, MOE_CAP, D_MODEL), lambda i, b: (0, i, 0, 0)),
                      pl.BlockSpec((LANES, tm), lambda i, b: (0, i)),
                      pl.BlockSpec((LANES, tm), lambda i, b: (0, i)),
                      pl.BlockSpec((1, D_MODEL), lambda i, b: (0, 0))],
            out_specs=pl.BlockSpec((tm, D_MODEL), lambda i, b: (i, 0))),
        out_shape=jax.ShapeDtypeStruct((T, D_MODEL), F32),
        compiler_params=_cparams("arbitrary"),
        name="moe_combine",
    )(busy_tiles, x, yd.reshape(N_EXPERTS, nt, MOE_CAP, D_MODEL), pos_t, w_t, final_g)


def _block_diag(w):
    nb, bs, _ = w.shape
    out = jnp.zeros((nb * bs, nb * bs), w.dtype)
    for i in range(nb):
        out = out.at[i * bs:(i + 1) * bs, i * bs:(i + 1) * bs].set(w[i])
    return out


def _selection_aggregator_np(n_cmp, n_slc):
    ratio_s = NSA_SLC_LEN // NSA_CMP_STRIDE
    ratio_c = NSA_CMP_LEN // NSA_CMP_STRIDE
    w = np.convolve(np.ones(ratio_s), np.ones(ratio_c)).astype(np.float32)
    cj = np.arange(n_slc)[:, None] * ratio_s - (ratio_c - 1) + np.arange(w.size)[None, :]
    jj = np.broadcast_to(np.arange(n_slc)[:, None], cj.shape)
    ww = np.broadcast_to(w[None, :], cj.shape)
    keep = (cj >= 0) & (cj < n_cmp)
    agg = np.zeros((n_cmp, n_slc), np.float32)
    np.add.at(agg, (cj[keep], jj[keep]), ww[keep])
    return agg


def _row(v):
    return v.reshape(1, -1).astype(F32)


def _diff_lambda_init(layer):
    return 0.8 - 0.6 * math.exp(-0.3 * layer)


def _mixer_layer(x, l, B, S, tiles, p):
    (bias_swa, bias_diff, bias_slc, bias_win, bias_cmp, agg_pad, n_cmp, ncp) = tiles
    w_in = jnp.pad(p["w_in"][l], ((0, 0), (0, N_IN_PAD - p["w_in"].shape[2]))).astype(BF16)
    w_t = jnp.concatenate([w_in[:, 1536:1792], w_in[:, 2176:2304]], axis=1).T
    a_xg, b_qkv, c_qkv, d_all, vals_t = _in_proj(x, _row(p["mix_norm_g"][l]), w_in, w_t)

    y_a = _rglru(a_xg, p["conv_w"][l].astype(F32), _row(p["conv_b"][l]),
                 _block_diag(p["lru_w_a"][l]).astype(BF16), _row(p["lru_b_a"][l]),
                 _block_diag(p["lru_w_x"][l]).astype(BF16), _row(p["lru_b_x"][l]),
                 _row(p["lru_lambda"][l]), B, S)

    b3 = b_qkv.reshape(B, S, 512)
    swa_cfg = tuple(((h // 2) * 64, (h % 2) != (h // 2), h // 2) for h in range(SWA_HEADS))
    y_b = _banded_attention(b3, 0, b3, 2, b3, 3, bias_swa, _row(p["swa_sinks"][l]),
                            B=B, S=S, window=SWA_WINDOW, q_cfg=swa_cfg)

    c3 = c_qkv.reshape(B, S, 768)
    lam4 = jnp.stack([p["diff_lq1"][l], p["diff_lk1"][l], p["diff_lq2"][l], p["diff_lk2"][l]]).astype(F32)
    gsub = jnp.tile(p["diff_subln_g"][l].reshape(-1, 1).astype(F32), (2, 1))
    y_c = _flash_attention("diff", c3, 0, c3, 1, vals_t, 256, 0, bias_diff, (lam4, gsub), B=B, S=S,
                           lambda_init=_diff_lambda_init(l))

    d3 = d_all.reshape(B, S, 768)
    nchunk = S // NSA_CMP_STRIDE
    ck = d3[:, :, 256:320].reshape(B, nchunk, NSA_CMP_STRIDE * HEAD_DIM)
    cv = d3[:, :, 320:384].reshape(B, nchunk, NSA_CMP_STRIDE * HEAD_DIM)

    def pos8(a):
        return jnp.broadcast_to(a.reshape(1, -1), (8, a.size)).astype(BF16)

    w1k = jnp.pad(p["nsa_w1_k"][l], ((0, 0), (0, HEAD_DIM))).astype(BF16)
    w1v = jnp.pad(p["nsa_w1_v"][l], ((0, 0), (HEAD_DIM, 0))).astype(BF16)
    w2 = _block_diag(jnp.stack([p["nsa_w2_k"][l], p["nsa_w2_v"][l]])).astype(BF16)
    kcv = _nsa_compress(ck, cv, pos8(p["nsa_pos_k"][l]), pos8(p["nsa_pos_v"][l]), w1k, w1v, w2, ncp)
    o_c, sel = _nsa_cmp_select(d3, kcv, bias_cmp, agg_pad, B=B, S=S, n_cmp=n_cmp, ncp=ncp)
    o_s = _flash_attention("nsa", d3, 0, d3, 3, vals_t, 128, 2, bias_slc, (sel,), B=B, S=S)
    win_cfg = tuple((0, h % 2 == 1, 1) for h in range(NSA_HEADS))
    o_w = _banded_attention(d3, 0, d3, 4, None, None, bias_win, None, B=B, S=S, window=NSA_WINDOW, q_cfg=win_cfg)

    T = B * S
    return _merge(x, _row(p["mix_norm_g"][l]), y_a, y_b.reshape(T, 256), y_c.reshape(T, 256),
                  o_c.reshape(T, 256), o_s.reshape(T, 256), o_w.reshape(T, 256), d_all,
                  p["w_gate"][l].astype(BF16), p["b_gate"][l].astype(F32), p["w_branch"][l].astype(BF16),
                  p["w_out"][l].astype(BF16))


def _moe_layer(x, l, p, final_norm):
    w_router = jnp.concatenate([p["w_router_exp"][l], p["w_router_grp"][l]], axis=1)
    w_router = jnp.pad(w_router, ((0, 0), (0, LANES - w_router.shape[1]))).astype(F32)
    w_hi = w_router.astype(BF16)
    w_router = jnp.concatenate([w_hi, (w_router - w_hi.astype(F32)).astype(BF16)], axis=1)
    b_router = jnp.concatenate([p["b_router_exp"][l], p["b_router_grp"][l]])
    b_router = jnp.pad(b_router, (0, LANES - b_router.shape[0])).reshape(1, LANES).astype(F32)
    g = _row(p["ffn_norm_g"][l])
    fg = _row(p["final_norm_g"])
    wg, wu, wd = (p[k][l].astype(BF16) for k in ("w_exp_gate", "w_exp_up", "w_exp_down"))
    xd, pos_t, w_t, counts = _moe_dispatch(x, g, w_router, b_router)
    return lax.cond(jnp.max(counts) > MOE_CAP,
                    lambda: _moe(x, g, w_router, b_router, wg, wu, wd, fg, final_norm),
                    lambda: _moe_finish(x, xd, pos_t, w_t, counts, wg, wu, wd, fg, final_norm))


def _forward(p):
    x = p["x"]
    B, S, D = x.shape
    depth = p["w_in"].shape[0]
    assert D == D_MODEL and S % (16 * Q_BLOCK) == 0 and S // NSA_SLC_LEN <= LANES
    rel = p["rel_bias"].astype(F32)
    n_cmp = (S - NSA_CMP_LEN) // NSA_CMP_STRIDE + 1
    ncp = LANES + S // NSA_CMP_STRIDE
    agg = _selection_aggregator_np(n_cmp, S // NSA_SLC_LEN)
    agg_pad = np.zeros((ncp, LANES), np.float32)
    agg_pad[LANES:LANES + n_cmp, :agg.shape[1]] = agg
    tiles = (
        _bias_tiles(rel, _banded_buckets(1), 0, SWA_HEADS, False),
        _bias_tiles(rel, _flash_buckets(), SWA_HEADS, DIFF_HEADS, True, LOG2E),
        _bias_tiles(rel, _flash_buckets(), SWA_HEADS + DIFF_HEADS, NSA_HEADS, True, LOG2E),
        _bias_tiles(rel, _banded_buckets(NSA_WINDOW // Q_BLOCK), SWA_HEADS + DIFF_HEADS, NSA_HEADS, False),
        _bias_tiles(rel, _cmp_buckets(), SWA_HEADS + DIFF_HEADS, NSA_HEADS, True),
        jnp.asarray(agg_pad, BF16), n_cmp, ncp,
    )
    xt = x.reshape(B * S, D).astype(F32)
    for l in range(depth):
        xt = _mixer_layer(xt, l, B, S, tiles, p)
        xt = _moe_layer(xt, l, p, final_norm=(l == depth - 1))
    return xt.reshape(B, S, D).astype(x.dtype)


def kernel(x, rel_bias, final_norm_g, mix_norm_g, w_in, conv_w, conv_b, lru_w_a, lru_b_a, lru_w_x, lru_b_x,
           lru_lambda, swa_sinks, diff_lq1, diff_lk1, diff_lq2, diff_lk2, diff_subln_g, nsa_pos_k, nsa_w1_k,
           nsa_w2_k, nsa_pos_v, nsa_w1_v, nsa_w2_v, w_gate, b_gate, w_branch, w_out, ffn_norm_g, w_router_grp,
           b_router_grp, w_router_exp, b_router_exp, w_exp_gate, w_exp_up, w_exp_down):
    return _forward(dict(
        x=x, rel_bias=rel_bias, final_norm_g=final_norm_g, mix_norm_g=mix_norm_g, w_in=w_in, conv_w=conv_w,
        conv_b=conv_b, lru_w_a=lru_w_a, lru_b_a=lru_b_a, lru_w_x=lru_w_x, lru_b_x=lru_b_x, lru_lambda=lru_lambda,
        swa_sinks=swa_sinks, diff_lq1=diff_lq1, diff_lk1=diff_lk1, diff_lq2=diff_lq2, diff_lk2=diff_lk2,
        diff_subln_g=diff_subln_g, nsa_pos_k=nsa_pos_k, nsa_w1_k=nsa_w1_k, nsa_w2_k=nsa_w2_k, nsa_pos_v=nsa_pos_v,
        nsa_w1_v=nsa_w1_v, nsa_w2_v=nsa_w2_v, w_gate=w_gate, b_gate=b_gate, w_branch=w_branch, w_out=w_out,
        ffn_norm_g=ffn_norm_g, w_router_grp=w_router_grp, b_router_grp=b_router_grp, w_router_exp=w_router_exp,
        b_router_exp=b_router_exp, w_exp_gate=w_exp_gate, w_exp_up=w_exp_up, w_exp_down=w_exp_down))
```

```python
import functools
import math

import numpy as np
import jax
import jax.numpy as jnp
from jax import lax
from jax.experimental import pallas as pl
from jax.experimental.pallas import tpu as pltpu

F32 = jnp.float32
BF16 = jnp.bfloat16
I32 = jnp.int32

D_MODEL = 1024
HEAD_DIM = 64
MIX_WIDTH = 256
LRU_C = 8.0
CONV_WIDTH = 4
SWA_HEADS = 4
SWA_WINDOW = 128
DIFF_HEADS = 4
DIFF_DIM = 32
DIFF_SUBLN_EPS = 1e-5
NSA_HEADS = 4
NSA_CMP_LEN = 32
NSA_CMP_STRIDE = 16
NSA_SLC_LEN = 64
NSA_TOPK = 16
NSA_WINDOW = 512
NSA_FORCED_SCORE = 1e4
REL_BUCKETS = 32
REL_MAX_DIST = 128
N_GROUPS = 4
EXPERTS_PER_GROUP = 8
N_EXPERTS = 32
EXPERT_FF = 256
NORM_EPS = 1e-6
NEG_INF = -1e30
TINY = 1e-30
LOG2E = math.log2(math.e)

LANES = 128
Q_BLOCK = 128
CMP_Q = 256
BANDED_Q_BLOCKS = 2
FLASH_BLOCK = 256
MOE_TILE = 512
MOE_CAP = 128
MOE_TILES_PER_STEP = 8
FLASH_MAX_GROWTH = 64.0
FLASH_FAR_BLOCKS = 8
N_IN_PAD = 2560
VMEM_LIMIT = 56 * 1024 * 1024

_NT = (((1,), (1,)), ((), ()))


def _cparams(*sem):
    return pltpu.CompilerParams(dimension_semantics=sem, vmem_limit_bytes=VMEM_LIMIT)


def _t5_bucket_np(dist):
    n = np.maximum(dist, 0)
    exact = REL_BUCKETS // 2
    scaled = (np.log(np.maximum(n, 1).astype(np.float32) / np.float32(exact))
              / np.float32(math.log(REL_MAX_DIST / exact))).astype(np.float32)
    large = np.minimum(exact + (scaled * np.float32(REL_BUCKETS - exact)).astype(np.int32), REL_BUCKETS - 1)
    return np.where(n < exact, n, large).astype(np.int32)


def _bias_tile_kernel(tab_ref, idx_ref, o_ref, *, head0, shift, scale):
    h = pl.program_id(0) + head0
    idx = idx_ref[0]
    acc = jnp.zeros(idx.shape, F32)
    for b in range(REL_BUCKETS):
        acc = jnp.where(idx == b, tab_ref[b, h], acc)
    if shift:
        acc = acc - tab_ref[REL_BUCKETS - 1, h]
    if scale != 1.0:
        acc = acc * scale
    o_ref[0, 0] = acc


def _bias_tiles(rel_bias, buckets, head0, nheads, shift, scale=1.0):
    R, M, N = buckets.shape
    return pl.pallas_call(
        functools.partial(_bias_tile_kernel, head0=head0, shift=shift, scale=scale),
        grid=(nheads, R),
        in_specs=[pl.BlockSpec(memory_space=pltpu.SMEM),
                  pl.BlockSpec((1, M, N), lambda h, r: (r, 0, 0))],
        out_specs=pl.BlockSpec((1, 1, M, N), lambda h, r: (h, r, 0, 0)),
        out_shape=jax.ShapeDtypeStruct((nheads, R, M, N), F32),
        compiler_params=_cparams("arbitrary", "arbitrary"),
        name="bias_tiles",
    )(rel_bias, jnp.asarray(buckets))


def _banded_buckets(nback):
    i = np.arange(BANDED_Q_BLOCKS * Q_BLOCK)[None, :]
    j = np.arange((nback + BANDED_Q_BLOCKS) * Q_BLOCK)[:, None]
    return _t5_bucket_np(i - j + nback * Q_BLOCK)[None]


def _flash_buckets():
    k = np.arange(FLASH_BLOCK)[:, None]
    q = np.arange(FLASH_BLOCK)[None, :]
    return np.stack([_t5_bucket_np(q - k), _t5_bucket_np(q - k + FLASH_BLOCK)])


def _cmp_buckets():
    i = np.arange(CMP_Q)[None, :, None]
    c2 = np.arange(2 * LANES)[None, None, :]
    r = np.arange(16 * LANES // CMP_Q)[:, None, None]
    dist = i - NSA_CMP_STRIDE * (c2 - LANES - (CMP_Q // NSA_CMP_STRIDE) * r) - (NSA_CMP_LEN - 1)
    return _t5_bucket_np(dist)


def _rms(x, g):
    return x * lax.rsqrt(jnp.mean(x * x, axis=-1, keepdims=True) + NORM_EPS) * g


def _proj_kernel(x_ref, g_ref, w_ref, wt_ref, oa_ref, ob_ref, oc_ref, od_ref, ot_ref):
    h = _rms(x_ref[...], g_ref[...]).astype(BF16)
    oa_ref[...] = jnp.dot(h, w_ref[:, 0:512], preferred_element_type=F32)
    ob_ref[...] = jnp.dot(h, w_ref[:, 512:1024], preferred_element_type=F32).astype(BF16)
    oc_ref[...] = jnp.dot(h, w_ref[:, 1024:1792], preferred_element_type=F32).astype(BF16)
    od_ref[...] = jnp.dot(h, w_ref[:, 1792:2560], preferred_element_type=F32).astype(BF16)
    ot_ref[...] = lax.dot_general(wt_ref[...], h, _NT, preferred_element_type=F32).astype(BF16)


def _in_proj(x, g, w, wt):
    T = x.shape[0]
    tm = 512
    nt = wt.shape[0]
    return pl.pallas_call(
        _proj_kernel,
        grid=(T // tm,),
        in_specs=[pl.BlockSpec((tm, D_MODEL), lambda i: (i, 0)),
                  pl.BlockSpec((1, D_MODEL), lambda i: (0, 0)),
                  pl.BlockSpec((D_MODEL, N_IN_PAD), lambda i: (0, 0)),
                  pl.BlockSpec((nt, D_MODEL), lambda i: (0, 0))],
        out_specs=[pl.BlockSpec((tm, 512), lambda i: (i, 0)),
                   pl.BlockSpec((tm, 512), lambda i: (i, 0)),
                   pl.BlockSpec((tm, 768), lambda i: (i, 0)),
                   pl.BlockSpec((tm, 768), lambda i: (i, 0)),
                   pl.BlockSpec((nt, tm), lambda i: (0, i))],
        out_shape=[jax.ShapeDtypeStruct((T, 512), F32),
                   jax.ShapeDtypeStruct((T, 512), BF16),
                   jax.ShapeDtypeStruct((T, 768), BF16),
                   jax.ShapeDtypeStruct((T, 768), BF16),
                   jax.ShapeDtypeStruct((nt, T), BF16)],
        compiler_params=_cparams("arbitrary"),
        name="in_proj",
    )(x, g, w, wt)


def _lru_kernel(xg_ref, cw_ref, cb_ref, wa_ref, ba_ref, wx_ref, bx_ref, lam_ref, o_ref,
                ext_ref, h_ref, *, Tt):
    t = pl.program_id(1)

    W = MIX_WIDTH

    @pl.when(t == 0)
    def _():
        ext_ref[0:8, :] = jnp.zeros((8, W), F32)
        h_ref[...] = jnp.zeros_like(h_ref)

    x = xg_ref[:, 0:W]
    gate = xg_ref[:, W:2 * W]
    ext_ref[8:Tt + 8, :] = x
    row = lax.broadcasted_iota(I32, (Tt, W), 0)
    xc = cb_ref[...] + x * cw_ref[CONV_WIDTH - 1:CONV_WIDTH, :]
    for s in range(1, CONV_WIDTH):
        xc = xc + ext_ref[8 - s:8 - s + Tt, :] * cw_ref[CONV_WIDTH - 1 - s:CONV_WIDTH - s, :]
    ext_ref[0:8, :] = ext_ref[Tt:Tt + 8, :]

    xcb = xc.astype(BF16)
    r = jax.nn.sigmoid(jnp.dot(xcb, wa_ref[...], preferred_element_type=F32) + ba_ref[...])
    ig = jax.nn.sigmoid(jnp.dot(xcb, wx_ref[...], preferred_element_type=F32) + bx_ref[...])
    z = -lam_ref[...]
    softplus = jnp.maximum(z, 0.0) + jnp.log1p(jnp.exp(-jnp.abs(z)))
    log_a = (-LRU_C) * r * softplus
    a = jnp.exp(log_a)
    mult = jnp.sqrt(1.0 - a * a)
    mult = jnp.where((row + t * Tt) == 0, 1.0, mult)
    b = mult * ig * xc

    d = 1
    while d < Tt:
        keep = row >= d
        b = jnp.where(keep, a * pltpu.roll(b, d, 0) + b, b)
        a = jnp.where(keep, a * pltpu.roll(a, d, 0), a)
        d *= 2
    hs = a * h_ref[7:8, :] + b
    h_ref[...] = hs[Tt - 8:Tt]
    o_ref[...] = (hs * jax.nn.gelu(gate)).astype(o_ref.dtype)


def _rglru(a_xg, conv_w, conv_b, wa_bd, b_a, wx_bd, b_x, lam, B, S):
    Tt = 256
    W = MIX_WIDTH
    vec = pl.BlockSpec((1, W), lambda b, t: (0, 0))
    mat = pl.BlockSpec((W, W), lambda b, t: (0, 0))
    out = pl.pallas_call(
        functools.partial(_lru_kernel, Tt=Tt),
        grid=(B, S // Tt),
        in_specs=[pl.BlockSpec((None, Tt, 2 * W), lambda b, t: (b, t, 0)),
                  pl.BlockSpec((CONV_WIDTH, W), lambda b, t: (0, 0)),
                  vec, mat, vec, mat, vec, vec],
        out_specs=pl.BlockSpec((None, Tt, W), lambda b, t: (b, t, 0)),
        out_shape=jax.ShapeDtypeStruct((B, S, W), BF16),
        scratch_shapes=[pltpu.VMEM((Tt + 8, W), F32), pltpu.VMEM((8, W), F32)],
        compiler_params=_cparams("arbitrary", "arbitrary"),
        name="rglru",
    )(a_xg.reshape(B, S, 2 * W), conv_w, conv_b, wa_bd, b_a, wx_bd, b_x, lam)
    return out.reshape(B * S, W)


def _lane_window(q128, lo, width, roll, scale):
    q = q128 * scale
    if roll:
        q = pltpu.roll(q, LANES // 2, 1)
    lane = lax.broadcasted_iota(I32, q.shape, 1)
    return jnp.where((lane >= lo) & (lane < lo + width), q, 0.0).astype(BF16)


def _pair_halves(even, odd, even_half, odd_half):
    if even_half == 1:
        even = pltpu.roll(even, LANES // 2, 1)
    if odd_half == 0:
        odd = pltpu.roll(odd, LANES // 2, 1)
    lane = lax.broadcasted_iota(I32, even.shape, 1)
    return jnp.where(lane < LANES // 2, even, odd)


def _banded_kernel(*refs, nback, window, q_cfg, has_sink, shared_kv):
    nk = nback + BANDED_Q_BLOCKS
    TQ = BANDED_Q_BLOCKS * Q_BLOCK
    q_ref = refs[0]
    k_refs = refs[1:1 + nk]
    pos = 1 + nk
    if shared_kv:
        v_refs = k_refs
    else:
        v_refs = refs[pos:pos + nk]
        pos += nk
    bias_ref = refs[pos]
    pos += 1
    sink_ref = None
    if has_sink:
        sink_ref = refs[pos]
        pos += 1
    o_ref = refs[pos]

    n = pl.program_id(1)
    KW = nk * Q_BLOCK
    kcat = jnp.concatenate([k_refs[nk - 1 - i][...] for i in range(nk)], axis=0)
    vcat = kcat if shared_kv else jnp.concatenate([v_refs[nk - 1 - i][...] for i in range(nk)], axis=0)
    vt = vcat.astype(F32).T.astype(BF16)
    jj = lax.broadcasted_iota(I32, (KW, TQ), 0)
    ii = lax.broadcasted_iota(I32, (KW, TQ), 1)
    dist = ii - jj + nback * Q_BLOCK
    valid = (dist >= 0) & (dist < window) & (jj >= (nback - BANDED_Q_BLOCKS * n) * Q_BLOCK)
    scale = HEAD_DIM ** -0.5

    def logits(h):
        lo, roll, _ = q_cfg[h]
        g = h // 2
        qx = _lane_window(q_ref[:, LANES * g:LANES * (g + 1)].astype(F32), lo, HEAD_DIM, roll, scale)
        s = lax.dot_general(kcat, qx, _NT, preferred_element_type=F32) + bias_ref[h, 0]
        return jnp.where(valid, s, NEG_INF)

    def softmax(h, s):
        m = jnp.max(s, axis=0, keepdims=True)
        if has_sink:
            sink = sink_ref[0:1, h:h + 1]
            m = jnp.maximum(m, sink)
        p = jnp.exp(s - m)
        den = jnp.sum(p, axis=0, keepdims=True)
        if has_sink:
            den = den + jnp.exp(sink - m)
        return p.astype(BF16), 1.0 / jnp.maximum(den, TINY)

    def values(h, p, inv):
        half = q_cfg[h][2]
        return jnp.dot(vt[HEAD_DIM * half:HEAD_DIM * (half + 1)], p, preferred_element_type=F32) * inv

    nh = len(q_cfg)
    ss = [logits(h) for h in range(nh)]
    pr = [softmax(h, ss[h]) for h in range(nh)]
    outs = [values(h, *pr[h]) for h in range(nh)]
    for g in range(nh // 2):
        pair = jnp.concatenate([outs[2 * g], outs[2 * g + 1]], axis=0)
        o_ref[:, LANES * g:LANES * (g + 1)] = pair.T.astype(o_ref.dtype)


def _banded_attention(qsrc, q_col, ksrc, k_col, vsrc, v_col, bias, sinks, *, B, S, window, q_cfg):
    nback = -(-window // Q_BLOCK)
    nq = BANDED_Q_BLOCKS
    nk = nback + nq
    TQ = nq * Q_BLOCK
    shared_kv = vsrc is None
    has_sink = sinks is not None

    def kv_spec(col, i):
        return pl.BlockSpec((None, Q_BLOCK, LANES), lambda b, n: (b, jnp.maximum(nq * n + nq - 1 - i, 0), col))

    in_specs = [pl.BlockSpec((None, TQ, 2 * LANES), lambda b, n: (b, n, q_col))]
    args = [qsrc]
    in_specs += [kv_spec(k_col, i) for i in range(nk)]
    args += [ksrc] * nk
    if not shared_kv:
        in_specs += [kv_spec(v_col, i) for i in range(nk)]
        args += [vsrc] * nk
    in_specs.append(pl.BlockSpec(bias.shape, lambda b, n: (0, 0, 0, 0)))
    args.append(bias)
    if has_sink:
        in_specs.append(pl.BlockSpec(sinks.shape, lambda b, n: (0, 0)))
        args.append(sinks)
    return pl.pallas_call(
        functools.partial(_banded_kernel, nback=nback, window=window, q_cfg=q_cfg,
                          has_sink=has_sink, shared_kv=shared_kv),
        grid=(B, S // TQ),
        in_specs=in_specs,
        out_specs=pl.BlockSpec((None, TQ, 2 * LANES), lambda b, n: (b, n, 0)),
        out_shape=jax.ShapeDtypeStruct((B, S, 2 * LANES), BF16),
        compiler_params=_cparams("arbitrary", "arbitrary"),
        name="banded_attn",
    )(*args)


def _flash_kernel(*refs, mode, lambda_init):
    TB = FLASH_BLOCK
    if mode == "diff":
        (q_ref, k_ref, vt_ref, bias_ref, lam_ref, g_ref, o_ref,
         qx_ref, m_ref, l_ref, acc_ref) = refs
        sel_ref = None
        maps = [dict(g=h // 2, lo=(h % 2) * 64 + 32 * mm, w=DIFF_DIM, roll=False, kg=h // 2, vrow=HEAD_DIM * h, h=h)
                for h in range(DIFF_HEADS) for mm in range(2)]
        scale = DIFF_DIM ** -0.5
    else:
        (q_ref, k_ref, vt_ref, sel_ref, bias_ref, o_ref,
         qx_ref, m_ref, l_ref, acc_ref) = refs
        maps = [dict(g=h // 2, lo=0, w=HEAD_DIM, roll=(h % 2 == 1), kg=0, vrow=HEAD_DIM, h=h)
                for h in range(NSA_HEADS)]
        scale = HEAD_DIM ** -0.5
    n = pl.program_id(1)

    for mp, c in enumerate(maps):
        qx_ref[mp] = _lane_window(q_ref[:, LANES * c["g"]:LANES * (c["g"] + 1)].astype(F32),
                                  c["lo"], c["w"], c["roll"], scale * LOG2E)
    m_ref[...] = jnp.full(m_ref.shape, NEG_INF, F32)
    l_ref[...] = jnp.zeros_like(l_ref)
    acc_ref[...] = jnp.zeros_like(acc_ref)

    key = lax.broadcasted_iota(I32, (TB, TB), 0)
    qry = lax.broadcasted_iota(I32, (TB, TB), 1)

    def step(j, kind, TK=TB):
        koff = pl.multiple_of(j * TB, TB)
        valid = None
        if kind == "diag":
            valid = qry >= key
        if sel_ref is not None:
            per = TK // NSA_SLC_LEN
            base = j * (TB // NSA_SLC_LEN)
            rows = [jnp.broadcast_to(sel_ref[pl.ds(base + i, 1), :], (NSA_SLC_LEN, TB)) for i in range(per)]
            picked = jnp.concatenate(rows, axis=0) > 0.5
            valid = picked if valid is None else (valid & picked)

        def logits(mp):
            c = maps[mp]
            kt = k_ref[pl.ds(koff, TK), LANES * c["kg"]:LANES * (c["kg"] + 1)]
            s = lax.dot_general(kt, qx_ref[mp], _NT, preferred_element_type=F32)
            if kind == "near":
                s = s + bias_ref[c["h"], 1]
            elif kind == "diag":
                s = s + bias_ref[c["h"], 0]
            if valid is not None:
                s = jnp.where(valid, s, NEG_INF)
            return s

        def softmax(mp, s):
            m_prev = m_ref[mp]
            m_new = jnp.maximum(m_prev, jnp.max(s, axis=0, keepdims=True))
            alpha = jnp.exp2(m_prev - m_new)
            p = jnp.exp2(s - m_new)
            l_ref[mp] = alpha * l_ref[mp] + jnp.sum(p, axis=0, keepdims=True)
            m_ref[mp] = m_new
            return p.astype(BF16), alpha

        def values(mp, p, alpha):
            c = maps[mp]
            vt = vt_ref[c["vrow"]:c["vrow"] + HEAD_DIM, pl.ds(koff, TK)]
            acc_ref[mp] = alpha * acc_ref[mp] + jnp.dot(vt, p, preferred_element_type=F32)

        nm = len(maps)
        ss = [logits(mp) for mp in range(nm)]

        def two_pass():
            pa = [softmax(mp, ss[mp]) for mp in range(nm)]
            for mp in range(nm):
                values(mp, *pa[mp])

        if kind == "diag":
            two_pass()
            return

        stabs = [m_ref[mp] for mp in range(nm)]
        ps = [jnp.exp2(ss[mp] - stabs[mp]) for mp in range(nm)]
        m_news = [jnp.maximum(stabs[mp], jnp.max(ss[mp], axis=0, keepdims=True)) for mp in range(nm)]
        growth = m_news[0] - stabs[0]
        for mp in range(1, nm):
            growth = jnp.maximum(growth, m_news[mp] - stabs[mp])
        safe = jnp.max(growth) <= FLASH_MAX_GROWTH
        psums = [jnp.sum(ps[mp], axis=0, keepdims=True) for mp in range(nm)]
        pvs = [jnp.dot(vt_ref[maps[mp]["vrow"]:maps[mp]["vrow"] + HEAD_DIM, pl.ds(koff, TK)], ps[mp].astype(BF16),
                       preferred_element_type=F32) for mp in range(nm)]

        @pl.when(safe)
        def _():
            for mp in range(nm):
                alpha = jnp.exp2(stabs[mp] - m_news[mp])
                l_ref[mp] = (l_ref[mp] + psums[mp]) * alpha
                acc_ref[mp] = (acc_ref[mp] + pvs[mp]) * alpha
                m_ref[mp] = m_news[mp]

        @pl.when(jnp.logical_not(safe))
        def _():
            two_pass()

    step(n, "diag")

    @pl.when(n >= 1)
    def _():
        step(n - 1, "near")

    nfar = jnp.maximum(n - 1, 0)
    FW = FLASH_FAR_BLOCKS

    def far_body(j, carry):
        step(FW * j, "far", FW * TB)
        return carry

    lax.fori_loop(0, nfar // FW, far_body, 0)

    done = (nfar // FW) * FW
    rest = nfar - done
    size = FW // 2
    while size >= 1:
        def tail_step(size=size):
            step(done + rest - rest % (2 * size), "far", size * TB)

        pl.when(rest % (2 * size) >= size)(tail_step)
        size //= 2

    def normalised(mp):
        return acc_ref[mp] * (1.0 / jnp.maximum(l_ref[mp], TINY))

    if mode == "diff":
        lq = lam_ref[...]
        lam = (jnp.exp(jnp.sum(lq[0:1] * lq[1:2], axis=1, keepdims=True))
               - jnp.exp(jnp.sum(lq[2:3] * lq[3:4], axis=1, keepdims=True)) + lambda_init)
        for g in range(DIFF_HEADS // 2):
            res = []
            for h in (2 * g, 2 * g + 1):
                o = normalised(2 * h) - lam * normalised(2 * h + 1)
                ms = jnp.sum(o * o, axis=0, keepdims=True) * (1.0 / (2 * DIFF_DIM))
                res.append(o * lax.rsqrt(ms + DIFF_SUBLN_EPS))
            out = jnp.concatenate(res, axis=0) * g_ref[...] * (1.0 - lambda_init)
            o_ref[:, LANES * g:LANES * (g + 1)] = out.T.astype(o_ref.dtype)
    else:
        for g in range(NSA_HEADS // 2):
            out = jnp.concatenate([normalised(2 * g), normalised(2 * g + 1)], axis=0)
            o_ref[:, LANES * g:LANES * (g + 1)] = out.T.astype(o_ref.dtype)


def _flash_attention(mode, qsrc, q_col, ksrc, k_col, vt, vt_rows, vt_blk, bias, extra, *, B, S, lambda_init=0.0):
    TB = FLASH_BLOCK
    q_spec = pl.BlockSpec((None, TB, 2 * LANES), lambda b, n: (b, n, q_col))
    bias_spec = pl.BlockSpec(bias.shape, lambda b, n: (0, 0, 0, 0))
    vt_spec = pl.BlockSpec((vt_rows, S), lambda b, n: (vt_blk, b))
    if mode == "diff":
        lam4, gsub = extra
        nmaps = 2 * DIFF_HEADS
        in_specs = [q_spec,
                    pl.BlockSpec((None, S, 2 * LANES), lambda b, n: (b, 0, k_col)),
                    vt_spec, bias_spec,
                    pl.BlockSpec(lam4.shape, lambda b, n: (0, 0)),
                    pl.BlockSpec(gsub.shape, lambda b, n: (0, 0))]
        args = [qsrc, ksrc, vt, bias, lam4, gsub]
    else:
        (sel,) = extra
        nmaps = NSA_HEADS
        in_specs = [q_spec,
                    pl.BlockSpec((None, S, LANES), lambda b, n: (b, 0, k_col)),
                    vt_spec,
                    pl.BlockSpec((None, LANES, TB), lambda b, n: (b, 0, n)),
                    bias_spec]
        args = [qsrc, ksrc, vt, sel, bias]
    return pl.pallas_call(
        functools.partial(_flash_kernel, mode=mode, lambda_init=lambda_init),
        grid=(B, S // TB),
        in_specs=in_specs,
        out_specs=pl.BlockSpec((None, TB, 2 * LANES), lambda b, n: (b, n, 0)),
        out_shape=jax.ShapeDtypeStruct((B, S, 2 * LANES), BF16),
        scratch_shapes=[pltpu.VMEM((nmaps, TB, LANES), BF16),
                        pltpu.VMEM((nmaps, 1, TB), F32),
                        pltpu.VMEM((nmaps, 1, TB), F32),
                        pltpu.VMEM((nmaps, HEAD_DIM, TB), F32)],
        compiler_params=_cparams("arbitrary", "arbitrary"),
        name="flash_" + mode,
    )(*args)


def _compress_kernel(ck_ref, cv_ref, pk_ref, pv_ref, w1k_ref, w1v_ref, w2_ref, o_ref):
    half = NSA_CMP_STRIDE * HEAD_DIM
    ck = ck_ref[...]
    cv = cv_ref[...]
    nrow = ck.shape[0]

    def mm(a, w):
        return jnp.dot(a, w, preferred_element_type=F32)

    top = mm(ck, w1k_ref[0:half, :]) + mm(cv, w1v_ref[0:half, :])
    bot = mm(ck, w1k_ref[half:2 * half, :]) + mm(cv, w1v_ref[half:2 * half, :])
    posw = (mm(pk_ref[...], w1k_ref[...]) + mm(pv_ref[...], w1v_ref[...]))[0:1]
    pre = top + pltpu.roll(bot, nrow - 1, 0) + posw
    kcv = mm(jax.nn.gelu(pre).astype(BF16), w2_ref[...])
    o_ref[0:LANES, :] = jnp.zeros((LANES, LANES), o_ref.dtype)
    o_ref[LANES:LANES + nrow, :] = kcv.astype(o_ref.dtype)


def _nsa_compress(ck, cv, pk, pv, w1k, w1v, w2, ncp):
    B, nchunk, width = ck.shape
    assert ncp == LANES + nchunk

    def full(a):
        return pl.BlockSpec(a.shape, lambda b: (0,) * a.ndim)

    chunk = pl.BlockSpec((None, nchunk, width), lambda b: (b, 0, 0))
    return pl.pallas_call(
        _compress_kernel,
        grid=(B,),
        in_specs=[chunk, chunk, full(pk), full(pv), full(w1k), full(w1v), full(w2)],
        out_specs=pl.BlockSpec((None, ncp, LANES), lambda b: (b, 0, 0)),
        out_shape=jax.ShapeDtypeStruct((B, ncp, LANES), BF16),
        compiler_params=_cparams("arbitrary"),
        name="nsa_compress",
    )(ck, cv, pk, pv, w1k, w1v, w2)


def _cmp_kernel(q_ref, kcv_ref, bias_ref, agg_ref, oc_ref, sel_ref, s_scr, *, n_cmp, ncp):
    n = pl.program_id(1)
    QB = CMP_Q
    kcv = kcv_ref[...]
    cp = lax.broadcasted_iota(I32, (QB, ncp), 1)
    qpos = n * QB + lax.broadcasted_iota(I32, (QB, ncp), 0)
    valid = ((cp >= LANES) & (cp < LANES + n_cmp)
             & (qpos - NSA_CMP_STRIDE * (cp - LANES) - (NSA_CMP_LEN - 1) >= 0))
    woff = pl.multiple_of((n // (16 * LANES // CMP_Q)) * LANES, LANES)
    scale = HEAD_DIM ** -0.5

    def logits(h):
        g = h // 2
        qx = _lane_window(q_ref[:, LANES * g:LANES * (g + 1)].astype(F32), 0, HEAD_DIM, h % 2 == 1, scale)
        s_scr[h] = lax.dot_general(qx, kcv, _NT, preferred_element_type=F32)
        s_scr[h, :, pl.ds(woff, 2 * LANES)] += bias_ref[h]

    def softmax(h):
        s = jnp.where(valid, s_scr[h], NEG_INF)
        m = jnp.max(s, axis=1, keepdims=True)
        p = jnp.where(valid, jnp.exp(s - m), 0.0)
        den = jnp.sum(p, axis=1, keepdims=True)
        return p * (1.0 / jnp.maximum(den, TINY))

    for h in range(NSA_HEADS):
        logits(h)
    ps = [softmax(h) for h in range(NSA_HEADS)]
    outs = [jnp.dot(p.astype(BF16), kcv, preferred_element_type=F32) for p in ps]
    for g in range(NSA_HEADS // 2):
        oc_ref[:, LANES * g:LANES * (g + 1)] = _pair_halves(outs[2 * g], outs[2 * g + 1], 1, 1).astype(oc_ref.dtype)

    psum = (ps[0] + ps[1]) + (ps[2] + ps[3])
    imp = jnp.dot(psum.astype(BF16), agg_ref[...], preferred_element_type=F32)

    v = imp.T
    jb = lax.broadcasted_iota(I32, v.shape, 0)
    qp = n * QB + lax.broadcasted_iota(I32, v.shape, 1)
    cur = qp >> 6
    forced = (jb == 0) | (jb == cur) | (jb == cur - 1)
    v = jnp.where(forced, -jnp.inf, jnp.where(jb * NSA_SLC_LEN <= qp, v, -1.0))
    rid = jb.astype(F32)
    picked = jnp.where(forced, 1.0, 0.0)
    for _ in range(NSA_TOPK - 3):
        mx = jnp.max(v, axis=0, keepdims=True)
        first = jnp.min(jnp.where(v == mx, rid, float(LANES)), axis=0, keepdims=True)
        hit = rid == first
        picked = jnp.where(hit, 1.0, picked)
        v = jnp.where(hit, -jnp.inf, v)
    sel_ref[...] = picked


def _nsa_cmp_select(dsrc, kcv, bias, agg, *, B, S, n_cmp, ncp):
    return pl.pallas_call(
        functools.partial(_cmp_kernel, n_cmp=n_cmp, ncp=ncp),
        grid=(B, S // CMP_Q),
        in_specs=[pl.BlockSpec((None, CMP_Q, 2 * LANES), lambda b, n: (b, n, 0)),
                  pl.BlockSpec((None, ncp, LANES), lambda b, n: (b, 0, 0)),
                  pl.BlockSpec((NSA_HEADS, None, CMP_Q, 2 * LANES), lambda b, n: (0, n % (16 * LANES // CMP_Q), 0, 0)),
                  pl.BlockSpec((ncp, LANES), lambda b, n: (0, 0))],
        out_specs=[pl.BlockSpec((None, CMP_Q, 2 * LANES), lambda b, n: (b, n, 0)),
                   pl.BlockSpec((None, LANES, CMP_Q), lambda b, n: (b, 0, n))],
        out_shape=[jax.ShapeDtypeStruct((B, S, 2 * LANES), BF16),
                   jax.ShapeDtypeStruct((B, LANES, S), F32)],
        scratch_shapes=[pltpu.VMEM((NSA_HEADS, CMP_Q, ncp), F32)],
        compiler_params=_cparams("arbitrary", "arbitrary"),
        name="nsa_cmp_select",
    )(dsrc, kcv, bias, agg)


def _merge_kernel(x_ref, g_ref, ya_ref, yb_ref, yc_ref, oc_ref, os_ref, ow_ref, dg_ref,
                  wg_ref, bg_ref, wb_ref, wo_ref, o_ref):
    x = x_ref[...]
    h = _rms(x, g_ref[...]).astype(BF16)
    tm = x.shape[0]
    lane = lax.broadcasted_iota(I32, (tm, LANES), 1)
    gd = jax.nn.sigmoid(dg_ref[...].astype(F32))
    yd = []
    for g in range(NSA_HEADS // 2):
        acc = jnp.zeros((tm, LANES), F32)
        for br, src in enumerate((oc_ref, os_ref, ow_ref)):
            ge = gd[:, 3 * (2 * g) + br:3 * (2 * g) + br + 1]
            go = gd[:, 3 * (2 * g + 1) + br:3 * (2 * g + 1) + br + 1]
            acc = acc + jnp.where(lane < 64, ge, go) * src[:, LANES * g:LANES * (g + 1)].astype(F32)
        yd.append(acc)
    y_d = jnp.concatenate(yd, axis=1).astype(BF16)
    ys = (ya_ref[...], yb_ref[...], yc_ref[...], y_d)
    merged = jnp.zeros((tm, D_MODEL), F32)
    for br in range(4):
        gate = jax.nn.sigmoid(jnp.dot(h, wg_ref[br], preferred_element_type=F32) + bg_ref[br:br + 1, :])
        merged = merged + gate * jnp.dot(ys[br], wb_ref[br], preferred_element_type=F32)
    o_ref[...] = x + jnp.dot(merged.astype(BF16), wo_ref[...], preferred_element_type=F32)


def _merge(x, g, ya, yb, yc, oc, osel, ow, dproj, wg, bg, wb, wo):
    T = x.shape[0]
    tm = 256

    def tile(width, col=0):
        return pl.BlockSpec((tm, width), lambda i: (i, col))

    def full(a):
        return pl.BlockSpec(a.shape, lambda i: (0,) * a.ndim)

    return pl.pallas_call(
        _merge_kernel,
        grid=(T // tm,),
        in_specs=[tile(D_MODEL), full(g), tile(256), tile(256), tile(256), tile(256), tile(256), tile(256),
                  tile(LANES, 5), full(wg), full(bg), full(wb), full(wo)],
        out_specs=tile(D_MODEL),
        out_shape=jax.ShapeDtypeStruct((T, D_MODEL), F32),
        compiler_params=_cparams("arbitrary"),
        name="merge",
    )(x, g, ya, yb, yc, oc, osel, ow, dproj, wg, bg, wb, wo)


def _route(hn, w_split, b_router):
    lane = lax.broadcasted_iota(I32, (hn.shape[0], LANES), 1)
    hi = hn.astype(BF16)
    lo = (hn - hi.astype(F32)).astype(BF16)
    a = jnp.dot(hi, w_split, preferred_element_type=F32)
    b = jnp.dot(lo, w_split[:, 0:LANES], preferred_element_type=F32)
    logits = (a[:, 0:LANES] + a[:, LANES:2 * LANES]) + b + b_router
    is_grp = (lane >= N_EXPERTS) & (lane < N_EXPERTS + N_GROUPS)
    lanef = lane.astype(F32)
    lg = jnp.where(is_grp, logits, -jnp.inf)
    mg = jnp.max(lg, axis=1, keepdims=True)
    gsel = jnp.min(jnp.where(lg == mg, lanef, 2.0 * LANES), axis=1, keepdims=True) - N_EXPERTS
    g_w = 1.0 / jnp.sum(jnp.where(is_grp, jnp.exp(lg - mg), 0.0), axis=1, keepdims=True)
    in_grp = (lane < N_EXPERTS) & ((lane >> 3).astype(F32) == gsel)
    v1 = jnp.where(in_grp, logits, -jnp.inf)
    m1 = jnp.max(v1, axis=1, keepdims=True)
    i1 = jnp.min(jnp.where(v1 == m1, lanef, 2.0 * LANES), axis=1, keepdims=True)
    v2 = jnp.where(lanef == i1, -jnp.inf, v1)
    m2 = jnp.max(v2, axis=1, keepdims=True)
    i2 = jnp.min(jnp.where(v2 == m2, lanef, 2.0 * LANES), axis=1, keepdims=True)
    e2 = jnp.exp(m2 - m1)
    w1 = g_w / (1.0 + e2)
    w2 = g_w * e2 / (1.0 + e2)
    first, second = lanef == i1, lanef == i2
    comb = jnp.where(first, w1, 0.0) + jnp.where(second, w2, 0.0)
    return comb, jnp.where(first | second, 1.0, 0.0)


def _moe_kernel(x_ref, g_ref, wr_ref, br_ref, wg_ref, wu_ref, wd_ref, fg_ref, o_ref,
                hn_ref, comb_ref, acc_ref, *, final_norm):
    e = pl.program_id(1)
    tm = x_ref.shape[0]
    lane = lax.broadcasted_iota(I32, (tm, LANES), 1)

    @pl.when(e == 0)
    def _():
        hn = _rms(x_ref[...], g_ref[...])
        hn_ref[...] = hn.astype(BF16)
        comb_ref[...] = _route(hn, wr_ref[...], br_ref[...])[0]
        acc_ref[...] = jnp.zeros_like(acc_ref)

    hn = hn_ref[...]
    gt = jnp.dot(hn, wg_ref[...], preferred_element_type=F32)
    up = jnp.dot(hn, wu_ref[...], preferred_element_type=F32)
    he = (gt * jax.nn.sigmoid(gt) * up).astype(BF16)
    ye = jnp.dot(he, wd_ref[...], preferred_element_type=F32)
    cw = jnp.sum(jnp.where(lane == e, comb_ref[...], 0.0), axis=1, keepdims=True)
    acc_ref[...] += cw * ye

    @pl.when(e == N_EXPERTS - 1)
    def _():
        y = x_ref[...] + acc_ref[...]
        if final_norm:
            y = _rms(y, fg_ref[...])
        o_ref[...] = y


def _moe(x, g, w_router, b_router, wg, wu, wd, final_g, final_norm):
    T = x.shape[0]
    tm = 1024
    return pl.pallas_call(
        functools.partial(_moe_kernel, final_norm=final_norm),
        grid=(T // tm, N_EXPERTS),
        in_specs=[pl.BlockSpec((tm, D_MODEL), lambda i, e: (i, 0)),
                  pl.BlockSpec((1, D_MODEL), lambda i, e: (0, 0)),
                  pl.BlockSpec((D_MODEL, 2 * LANES), lambda i, e: (0, 0)),
                  pl.BlockSpec((1, LANES), lambda i, e: (0, 0)),
                  pl.BlockSpec((None, D_MODEL, EXPERT_FF), lambda i, e: (e, 0, 0)),
                  pl.BlockSpec((None, D_MODEL, EXPERT_FF), lambda i, e: (e, 0, 0)),
                  pl.BlockSpec((None, EXPERT_FF, D_MODEL), lambda i, e: (e, 0, 0)),
                  pl.BlockSpec((1, D_MODEL), lambda i, e: (0, 0))],
        out_specs=pl.BlockSpec((tm, D_MODEL), lambda i, e: (i, 0)),
        out_shape=jax.ShapeDtypeStruct((T, D_MODEL), F32),
        scratch_shapes=[pltpu.VMEM((tm, D_MODEL), BF16),
                        pltpu.VMEM((tm, LANES), F32),
                        pltpu.VMEM((tm, D_MODEL), F32)],
        compiler_params=_cparams("arbitrary", "arbitrary"),
        name="moe",
    )(x, g, w_router, b_router, wg, wu, wd, final_g)


def _slot_matrix(pos_t, val_t, ntok, half):
    nslot = MOE_CAP // 2
    slot = (lax.broadcasted_iota(I32, (nslot, ntok), 0) + half * nslot).astype(F32)
    blocks = []
    for e in range(N_EXPERTS):
        hit = pos_t[e:e + 1, :] == slot
        blocks.append(jnp.where(hit, 1.0 if val_t is None else val_t[e:e + 1, :], 0.0).astype(BF16))
    return jnp.concatenate(blocks, axis=0)


def _dispatch_kernel(x_ref, g_ref, wr_ref, br_ref, tri_ref, xd_ref, pos_ref, wt_ref, cnt_ref):
    tm = x_ref.shape[0]
    hn = _rms(x_ref[...], g_ref[...])
    comb, routed = _route(hn, wr_ref[...], br_ref[...])
    rank = jnp.dot(tri_ref[...], routed.astype(BF16), preferred_element_type=F32)
    pos = jnp.where(routed > 0.0, rank, -1.0)
    pos_t = pos.T
    pos_ref[...] = pos_t
    wt_ref[...] = comb.T
    counts = jnp.sum(routed, axis=0, keepdims=True)
    cnt_ref[...] = jnp.broadcast_to(counts, cnt_ref.shape)
    hnb = hn.astype(BF16)
    nslot = MOE_CAP // 2
    lower = jnp.dot(_slot_matrix(pos_t, None, tm, 0), hnb, preferred_element_type=F32)
    xd_ref[:, 0:nslot, :] = lower.reshape(N_EXPERTS, nslot, D_MODEL).astype(xd_ref.dtype)

    busy = jnp.max(counts) > nslot

    @pl.when(busy)
    def _():
        upper = jnp.dot(_slot_matrix(pos_t, None, tm, 1), hnb, preferred_element_type=F32)
        xd_ref[:, nslot:MOE_CAP, :] = upper.reshape(N_EXPERTS, nslot, D_MODEL).astype(xd_ref.dtype)

    @pl.when(jnp.logical_not(busy))
    def _():
        xd_ref[:, nslot:MOE_CAP, :] = jnp.zeros((N_EXPERTS, nslot, D_MODEL), xd_ref.dtype)


def _expert_kernel(busy_ref, xlo_ref, xhi_ref, wg_ref, wu_ref, wd_ref, olo_ref, ohi_ref):
    ntile, nslot, _ = xlo_ref.shape

    def ffn(x_ref, o_ref):
        x = x_ref[...].reshape(ntile * nslot, D_MODEL)
        gt = jnp.dot(x, wg_ref[...], preferred_element_type=F32)
        up = jnp.dot(x, wu_ref[...], preferred_element_type=F32)
        he = (gt * jax.nn.sigmoid(gt) * up).astype(BF16)
        ye = jnp.dot(he, wd_ref[...], preferred_element_type=F32)
        o_ref[...] = ye.reshape(ntile, nslot, D_MODEL).astype(o_ref.dtype)

    ffn(xlo_ref, olo_ref)
    busy = busy_ref[pl.program_id(0), pl.program_id(1)] > 0

    @pl.when(busy)
    def _():
        ffn(xhi_ref, ohi_ref)

    @pl.when(jnp.logical_not(busy))
    def _():
        ohi_ref[...] = jnp.zeros(ohi_ref.shape, ohi_ref.dtype)


def _combine_kernel(busy_ref, x_ref, ylo_ref, yhi_ref, pos_ref, wt_ref, fg_ref, o_ref, *, final_norm):
    tm = x_ref.shape[0]
    nslot = MOE_CAP // 2

    def gathered(y_ref, half):
        weights = _slot_matrix(pos_ref[...], wt_ref[...], tm, half)
        yd = y_ref[...].reshape(N_EXPERTS * nslot, D_MODEL)
        return lax.dot_general(weights, yd, (((0,), (0,)), ((), ())), preferred_element_type=F32)

    o_ref[...] = x_ref[...] + gathered(ylo_ref, 0)

    @pl.when(busy_ref[pl.program_id(0)] > 0)
    def _():
        o_ref[...] += gathered(yhi_ref, 1)

    if final_norm:
        o_ref[...] = _rms(o_ref[...], fg_ref[...])


def _moe_dispatch(x, g, w_router, b_router):
    T = x.shape[0]
    tm = MOE_TILE
    nt = T // tm
    tri = jnp.asarray(np.tril(np.ones((tm, tm), np.float32), -1), BF16)
    xd, pos_t, w_t, cnt = pl.pallas_call(
        _dispatch_kernel,
        grid=(nt,),
        in_specs=[pl.BlockSpec((tm, D_MODEL), lambda i: (i, 0)),
                  pl.BlockSpec((1, D_MODEL), lambda i: (0, 0)),
                  pl.BlockSpec((D_MODEL, 2 * LANES), lambda i: (0, 0)),
                  pl.BlockSpec((1, LANES), lambda i: (0, 0)),
                  pl.BlockSpec((tm, tm), lambda i: (0, 0))],
        out_specs=[pl.BlockSpec((N_EXPERTS, None, MOE_CAP, D_MODEL), lambda i: (0, i, 0, 0)),
                   pl.BlockSpec((LANES, tm), lambda i: (0, i)),
                   pl.BlockSpec((LANES, tm), lambda i: (0, i)),
                   pl.BlockSpec((None, 8, LANES), lambda i: (i, 0, 0))],
        out_shape=[jax.ShapeDtypeStruct((N_EXPERTS, nt, MOE_CAP, D_MODEL), BF16),
                   jax.ShapeDtypeStruct((LANES, T), F32),
                   jax.ShapeDtypeStruct((LANES, T), F32),
                   jax.ShapeDtypeStruct((nt, 8, LANES), F32)],
        compiler_params=_cparams("arbitrary"),
        name="moe_dispatch",
    )(x, g, w_router, b_router, tri)
    return xd, pos_t, w_t, cnt[:, 0, 0:N_EXPERTS]


def _moe_finish(x, xd, pos_t, w_t, counts, wg, wu, wd, final_g, final_norm):
    T = x.shape[0]
    tm = MOE_TILE
    nt = T // tm
    tps = min(MOE_TILES_PER_STEP, nt)
    assert nt % tps == 0
    nslot = MOE_CAP // 2
    hot = counts > nslot
    busy_blocks = jnp.any(hot.reshape(nt // tps, tps, N_EXPERTS), axis=1).T.astype(I32)
    busy_tiles = jnp.any(hot, axis=1).astype(I32)
    halves = xd.reshape(N_EXPERTS, nt, 2, nslot, D_MODEL)
    half_out = jax.ShapeDtypeStruct((N_EXPERTS, nt, nslot, D_MODEL), BF16)
    yd_lo, yd_hi = pl.pallas_call(
        _expert_kernel,
        grid_spec=pltpu.PrefetchScalarGridSpec(
            num_scalar_prefetch=1,
            grid=(N_EXPERTS, nt // tps),
            in_specs=[pl.BlockSpec((None, tps, None, nslot, D_MODEL), lambda e, c, b: (e, c, 0, 0, 0)),
                      pl.BlockSpec((None, tps, None, nslot, D_MODEL),
                                   lambda e, c, b: (e, jnp.where(b[e, c] > 0, c, 0), 1, 0, 0)),
                      pl.BlockSpec((None, D_MODEL, EXPERT_FF), lambda e, c, b: (e, 0, 0)),
                      pl.BlockSpec((None, D_MODEL, EXPERT_FF), lambda e, c, b: (e, 0, 0)),
                      pl.BlockSpec((None, EXPERT_FF, D_MODEL), lambda e, c, b: (e, 0, 0))],
            out_specs=[pl.BlockSpec((None, tps, nslot, D_MODEL), lambda e, c, b: (e, c, 0, 0)),
                       pl.BlockSpec((None, tps, nslot, D_MODEL), lambda e, c, b: (e, c, 0, 0))]),
        out_shape=[half_out, half_out],
        compiler_params=_cparams("arbitrary", "arbitrary"),
        name="moe_experts",
    )(busy_blocks, halves, halves, wg, wu, wd)
    return pl.pallas_call(
        functools.partial(_combine_kernel, final_norm=final_norm),
        grid_spec=pltpu.PrefetchScalarGridSpec(
            num_scalar_prefetch=1,
            grid=(nt,),
            in_specs=[pl.BlockSpec((tm, D_MODEL), lambda i, b: (i, 0)),
                      pl.BlockSpec((N_EXPERTS, None, nslot, D_MODEL), lambda i, b: (0, i, 0, 0)),
                      pl.BlockSpec((N_EXPERTS, None, nslot, D_MODEL),
                                   lambda i, b: (0, jnp.where(b[i] > 0, i, 0), 0, 0)),
                      pl.BlockSpec((LANES, tm), lambda i, b: (0, i)),
                      pl.BlockSpec((LANES, tm), lambda i, b: (0, i)),
                      pl.BlockSpec((1, D_MODEL), lambda i, b: (0, 0))],
            out_specs=pl.BlockSpec((tm, D_MODEL), lambda i, b: (i, 0))),
        out_shape=jax.ShapeDtypeStruct((T, D_MODEL), F32),
        compiler_params=_cparams("arbitrary"),
        name="moe_combine",
    )(busy_tiles, x, yd_lo, yd_hi, pos_t, w_t, final_g)


def _block_diag(w):
    nb, bs, _ = w.shape
    out = jnp.zeros((nb * bs, nb * bs), w.dtype)
    for i in range(nb):
        out = out.at[i * bs:(i + 1) * bs, i * bs:(i + 1) * bs].set(w[i])
    return out


def _selection_aggregator_np(n_cmp, n_slc):
    ratio_s = NSA_SLC_LEN // NSA_CMP_STRIDE
    ratio_c = NSA_CMP_LEN // NSA_CMP_STRIDE
    w = np.convolve(np.ones(ratio_s), np.ones(ratio_c)).astype(np.float32)
    cj = np.arange(n_slc)[:, None] * ratio_s - (ratio_c - 1) + np.arange(w.size)[None, :]
    jj = np.broadcast_to(np.arange(n_slc)[:, None], cj.shape)
    ww = np.broadcast_to(w[None, :], cj.shape)
    keep = (cj >= 0) & (cj < n_cmp)
    agg = np.zeros((n_cmp, n_slc), np.float32)
    np.add.at(agg, (cj[keep], jj[keep]), ww[keep])
    return agg


def _row(v):
    return v.reshape(1, -1).astype(F32)


def _diff_lambda_init(layer):
    return 0.8 - 0.6 * math.exp(-0.3 * layer)


def _mixer_layer(x, l, B, S, tiles, p):
    (bias_swa, bias_diff, bias_slc, bias_win, bias_cmp, agg_pad, n_cmp, ncp) = tiles
    w_in = jnp.pad(p["w_in"][l], ((0, 0), (0, N_IN_PAD - p["w_in"].shape[2]))).astype(BF16)
    w_t = jnp.concatenate([w_in[:, 1536:1792], w_in[:, 2176:2304]], axis=1).T
    a_xg, b_qkv, c_qkv, d_all, vals_t = _in_proj(x, _row(p["mix_norm_g"][l]), w_in, w_t)

    y_a = _rglru(a_xg, p["conv_w"][l].astype(F32), _row(p["conv_b"][l]),
                 _block_diag(p["lru_w_a"][l]).astype(BF16), _row(p["lru_b_a"][l]),
                 _block_diag(p["lru_w_x"][l]).astype(BF16), _row(p["lru_b_x"][l]),
                 _row(p["lru_lambda"][l]), B, S)

    b3 = b_qkv.reshape(B, S, 512)
    swa_cfg = tuple(((h // 2) * 64, (h % 2) != (h // 2), h // 2) for h in range(SWA_HEADS))
    y_b = _banded_attention(b3, 0, b3, 2, b3, 3, bias_swa, _row(p["swa_sinks"][l]),
                            B=B, S=S, window=SWA_WINDOW, q_cfg=swa_cfg)

    c3 = c_qkv.reshape(B, S, 768)
    lam4 = jnp.stack([p["diff_lq1"][l], p["diff_lk1"][l], p["diff_lq2"][l], p["diff_lk2"][l]]).astype(F32)
    gsub = jnp.tile(p["diff_subln_g"][l].reshape(-1, 1).astype(F32), (2, 1))
    y_c = _flash_attention("diff", c3, 0, c3, 1, vals_t, 256, 0, bias_diff, (lam4, gsub), B=B, S=S,
                           lambda_init=_diff_lambda_init(l))

    d3 = d_all.reshape(B, S, 768)
    nchunk = S // NSA_CMP_STRIDE
    ck = d3[:, :, 256:320].reshape(B, nchunk, NSA_CMP_STRIDE * HEAD_DIM)
    cv = d3[:, :, 320:384].reshape(B, nchunk, NSA_CMP_STRIDE * HEAD_DIM)

    def pos8(a):
        return jnp.broadcast_to(a.reshape(1, -1), (8, a.size)).astype(BF16)

    w1k = jnp.pad(p["nsa_w1_k"][l], ((0, 0), (0, HEAD_DIM))).astype(BF16)
    w1v = jnp.pad(p["nsa_w1_v"][l], ((0, 0), (HEAD_DIM, 0))).astype(BF16)
    w2 = _block_diag(jnp.stack([p["nsa_w2_k"][l], p["nsa_w2_v"][l]])).astype(BF16)
    kcv = _nsa_compress(ck, cv, pos8(p["nsa_pos_k"][l]), pos8(p["nsa_pos_v"][l]), w1k, w1v, w2, ncp)
    o_c, sel = _nsa_cmp_select(d3, kcv, bias_cmp, agg_pad, B=B, S=S, n_cmp=n_cmp, ncp=ncp)
    o_s = _flash_attention("nsa", d3, 0, d3, 3, vals_t, 128, 2, bias_slc, (sel,), B=B, S=S)
    win_cfg = tuple((0, h % 2 == 1, 1) for h in range(NSA_HEADS))
    o_w = _banded_attention(d3, 0, d3, 4, None, None, bias_win, None, B=B, S=S, window=NSA_WINDOW, q_cfg=win_cfg)

    T = B * S
    return _merge(x, _row(p["mix_norm_g"][l]), y_a, y_b.reshape(T, 256), y_c.reshape(T, 256),
                  o_c.reshape(T, 256), o_s.reshape(T, 256), o_w.reshape(T, 256), d_all,
                  p["w_gate"][l].astype(BF16), p["b_gate"][l].astype(F32), p["w_branch"][l].astype(BF16),
                  p["w_out"][l].astype(BF16))


def _moe_layer(x, l, p, final_norm):
    w_router = jnp.concatenate([p["w_router_exp"][l], p["w_router_grp"][l]], axis=1)
    w_router = jnp.pad(w_router, ((0, 0), (0, LANES - w_router.shape[1]))).astype(F32)
    w_hi = w_router.astype(BF16)
    w_router = jnp.concatenate([w_hi, (w_router - w_hi.astype(F32)).astype(BF16)], axis=1)
    b_router = jnp.concatenate([p["b_router_exp"][l], p["b_router_grp"][l]])
    b_router = jnp.pad(b_router, (0, LANES - b_router.shape[0])).reshape(1, LANES).astype(F32)
    g = _row(p["ffn_norm_g"][l])
    fg = _row(p["final_norm_g"])
    wg, wu, wd = (p[k][l].astype(BF16) for k in ("w_exp_gate", "w_exp_up", "w_exp_down"))
    xd, pos_t, w_t, counts = _moe_dispatch(x, g, w_router, b_router)
    return lax.cond(jnp.max(counts) > MOE_CAP,
                    lambda: _moe(x, g, w_router, b_router, wg, wu, wd, fg, final_norm),
                    lambda: _moe_finish(x, xd, pos_t, w_t, counts, wg, wu, wd, fg, final_norm))


def _forward(p):
    x = p["x"]
    B, S, D = x.shape
    depth = p["w_in"].shape[0]
    assert D == D_MODEL and S % (16 * Q_BLOCK) == 0 and S // NSA_SLC_LEN <= LANES
    rel = p["rel_bias"].astype(F32)
    n_cmp = (S - NSA_CMP_LEN) // NSA_CMP_STRIDE + 1
    ncp = LANES + S // NSA_CMP_STRIDE
    agg = _selection_aggregator_np(n_cmp, S // NSA_SLC_LEN)
    agg_pad = np.zeros((ncp, LANES), np.float32)
    agg_pad[LANES:LANES + n_cmp, :agg.shape[1]] = agg
    tiles = (
        _bias_tiles(rel, _banded_buckets(1), 0, SWA_HEADS, False),
        _bias_tiles(rel, _flash_buckets(), SWA_HEADS, DIFF_HEADS, True, LOG2E),
        _bias_tiles(rel, _flash_buckets(), SWA_HEADS + DIFF_HEADS, NSA_HEADS, True, LOG2E),
        _bias_tiles(rel, _banded_buckets(NSA_WINDOW // Q_BLOCK), SWA_HEADS + DIFF_HEADS, NSA_HEADS, False),
        _bias_tiles(rel, _cmp_buckets(), SWA_HEADS + DIFF_HEADS, NSA_HEADS, True),
        jnp.asarray(agg_pad, BF16), n_cmp, ncp,
    )
    xt = x.reshape(B * S, D).astype(F32)
    for l in range(depth):
        xt = _mixer_layer(xt, l, B, S, tiles, p)
        xt = _moe_layer(xt, l, p, final_norm=(l == depth - 1))
    return xt.reshape(B, S, D).astype(x.dtype)


def kernel(x, rel_bias, final_norm_g, mix_norm_g, w_in, conv_w, conv_b, lru_w_a, lru_b_a, lru_w_x, lru_b_x,
           lru_lambda, swa_sinks, diff_lq1, diff_lk1, diff_lq2, diff_lk2, diff_subln_g, nsa_pos_k, nsa_w1_k,
           nsa_w2_k, nsa_pos_v, nsa_w1_v, nsa_w2_v, w_gate, b_gate, w_branch, w_out, ffn_norm_g, w_router_grp,
           b_router_grp, w_router_exp, b_router_exp, w_exp_gate, w_exp_up, w_exp_down):
    return _forward(dict(
        x=x, rel_bias=rel_bias, final_norm_g=final_norm_g, mix_norm_g=mix_norm_g, w_in=w_in, conv_w=conv_w,
        conv_b=conv_b, lru_w_a=lru_w_a, lru_b_a=lru_b_a, lru_w_x=lru_w_x, lru_b_x=lru_b_x, lru_lambda=lru_lambda,
        swa_sinks=swa_sinks, diff_lq1=diff_lq1, diff_lk1=diff_lk1, diff_lq2=diff_lq2, diff_lk2=diff_lk2,
        diff_subln_g=diff_subln_g, nsa_pos_k=nsa_pos_k, nsa_w1_k=nsa_w1_k, nsa_w2_k=nsa_w2_k, nsa_pos_v=nsa_pos_v,
        nsa_w1_v=nsa_w1_v, nsa_w2_v=nsa_w2_v, w_gate=w_gate, b_gate=b_gate, w_branch=w_branch, w_out=w_out,
        ffn_norm_g=ffn_norm_g, w_router_grp=w_router_grp, b_router_grp=b_router_grp, w_router_exp=w_router_exp,
        b_router_exp=b_router_exp, w_exp_gate=w_exp_gate, w_exp_up=w_exp_up, w_exp_down=w_exp_down))
```

```python
import functools
import math

import numpy as np
import jax
import jax.numpy as jnp
from jax import lax
from jax.experimental import pallas as pl
from jax.experimental.pallas import tpu as pltpu

F32 = jnp.float32
BF16 = jnp.bfloat16
I32 = jnp.int32

D_MODEL = 1024
HEAD_DIM = 64
MIX_WIDTH = 256
LRU_C = 8.0
CONV_WIDTH = 4
SWA_HEADS = 4
SWA_WINDOW = 128
DIFF_HEADS = 4
DIFF_DIM = 32
DIFF_SUBLN_EPS = 1e-5
NSA_HEADS = 4
NSA_CMP_LEN = 32
NSA_CMP_STRIDE = 16
NSA_SLC_LEN = 64
NSA_TOPK = 16
NSA_WINDOW = 512
NSA_FORCED_SCORE = 1e4
REL_BUCKETS = 32
REL_MAX_DIST = 128
N_GROUPS = 4
EXPERTS_PER_GROUP = 8
N_EXPERTS = 32
EXPERT_FF = 256
NORM_EPS = 1e-6
NEG_INF = -1e30
TINY = 1e-30
LOG2E = math.log2(math.e)

LANES = 128
Q_BLOCK = 128
CMP_Q = 256
BANDED_Q_BLOCKS = 2
FLASH_BLOCK = 256
MOE_TILE = 512
MOE_CAP = 128
MOE_TILES_PER_STEP = 8
FLASH_MAX_GROWTH = 64.0
FLASH_FAR_BLOCKS = 8
N_IN_PAD = 2560
VMEM_LIMIT = 56 * 1024 * 1024

_NT = (((1,), (1,)), ((), ()))


def _cparams(*sem):
    return pltpu.CompilerParams(dimension_semantics=sem, vmem_limit_bytes=VMEM_LIMIT)


def _t5_bucket_np(dist):
    n = np.maximum(dist, 0)
    exact = REL_BUCKETS // 2
    scaled = (np.log(np.maximum(n, 1).astype(np.float32) / np.float32(exact))
              / np.float32(math.log(REL_MAX_DIST / exact))).astype(np.float32)
    large = np.minimum(exact + (scaled * np.float32(REL_BUCKETS - exact)).astype(np.int32), REL_BUCKETS - 1)
    return np.where(n < exact, n, large).astype(np.int32)


def _bias_tile_kernel(tab_ref, idx_ref, o_ref, *, head0, shift, scale):
    h = pl.program_id(0) + head0
    idx = idx_ref[0]
    acc = jnp.zeros(idx.shape, F32)
    for b in range(REL_BUCKETS):
        acc = jnp.where(idx == b, tab_ref[b, h], acc)
    if shift:
        acc = acc - tab_ref[REL_BUCKETS - 1, h]
    if scale != 1.0:
        acc = acc * scale
    o_ref[0, 0] = acc


def _bias_tiles(rel_bias, buckets, head0, nheads, shift, scale=1.0):
    R, M, N = buckets.shape
    return pl.pallas_call(
        functools.partial(_bias_tile_kernel, head0=head0, shift=shift, scale=scale),
        grid=(nheads, R),
        in_specs=[pl.BlockSpec(memory_space=pltpu.SMEM),
                  pl.BlockSpec((1, M, N), lambda h, r: (r, 0, 0))],
        out_specs=pl.BlockSpec((1, 1, M, N), lambda h, r: (h, r, 0, 0)),
        out_shape=jax.ShapeDtypeStruct((nheads, R, M, N), F32),
        compiler_params=_cparams("arbitrary", "arbitrary"),
        name="bias_tiles",
    )(rel_bias, jnp.asarray(buckets))


def _banded_buckets(nback):
    i = np.arange(BANDED_Q_BLOCKS * Q_BLOCK)[None, :]
    j = np.arange((nback + BANDED_Q_BLOCKS) * Q_BLOCK)[:, None]
    return _t5_bucket_np(i - j + nback * Q_BLOCK)[None]


def _flash_buckets():
    k = np.arange(FLASH_BLOCK)[:, None]
    q = np.arange(FLASH_BLOCK)[None, :]
    return np.stack([_t5_bucket_np(q - k), _t5_bucket_np(q - k + FLASH_BLOCK)])


def _cmp_buckets():
    i = np.arange(CMP_Q)[None, :, None]
    c2 = np.arange(2 * LANES)[None, None, :]
    r = np.arange(16 * LANES // CMP_Q)[:, None, None]
    dist = i - NSA_CMP_STRIDE * (c2 - LANES - (CMP_Q // NSA_CMP_STRIDE) * r) - (NSA_CMP_LEN - 1)
    return _t5_bucket_np(dist)


def _rms(x, g):
    return x * lax.rsqrt(jnp.mean(x * x, axis=-1, keepdims=True) + NORM_EPS) * g


def _proj_kernel(x_ref, g_ref, w_ref, wt_ref, oa_ref, ob_ref, oc_ref, od_ref, ot_ref):
    h = _rms(x_ref[...], g_ref[...]).astype(BF16)
    oa_ref[...] = jnp.dot(h, w_ref[:, 0:512], preferred_element_type=F32)
    ob_ref[...] = jnp.dot(h, w_ref[:, 512:1024], preferred_element_type=F32).astype(BF16)
    oc_ref[...] = jnp.dot(h, w_ref[:, 1024:1792], preferred_element_type=F32).astype(BF16)
    od_ref[...] = jnp.dot(h, w_ref[:, 1792:2560], preferred_element_type=F32).astype(BF16)
    ot_ref[...] = lax.dot_general(wt_ref[...], h, _NT, preferred_element_type=F32).astype(BF16)


def _in_proj(x, g, w, wt):
    T = x.shape[0]
    tm = 512
    nt = wt.shape[0]
    return pl.pallas_call(
        _proj_kernel,
        grid=(T // tm,),
        in_specs=[pl.BlockSpec((tm, D_MODEL), lambda i: (i, 0)),
                  pl.BlockSpec((1, D_MODEL), lambda i: (0, 0)),
                  pl.BlockSpec((D_MODEL, N_IN_PAD), lambda i: (0, 0)),
                  pl.BlockSpec((nt, D_MODEL), lambda i: (0, 0))],
        out_specs=[pl.BlockSpec((tm, 512), lambda i: (i, 0)),
                   pl.BlockSpec((tm, 512), lambda i: (i, 0)),
                   pl.BlockSpec((tm, 768), lambda i: (i, 0)),
                   pl.BlockSpec((tm, 768), lambda i: (i, 0)),
                   pl.BlockSpec((nt, tm), lambda i: (0, i))],
        out_shape=[jax.ShapeDtypeStruct((T, 512), F32),
                   jax.ShapeDtypeStruct((T, 512), BF16),
                   jax.ShapeDtypeStruct((T, 768), BF16),
                   jax.ShapeDtypeStruct((T, 768), BF16),
                   jax.ShapeDtypeStruct((nt, T), BF16)],
        compiler_params=_cparams("arbitrary"),
        name="in_proj",
    )(x, g, w, wt)


def _lru_kernel(xg_ref, cw_ref, cb_ref, wa_ref, ba_ref, wx_ref, bx_ref, lam_ref, o_ref,
                ext_ref, h_ref, *, Tt):
    t = pl.program_id(1)

    W = MIX_WIDTH

    @pl.when(t == 0)
    def _():
        ext_ref[0:8, :] = jnp.zeros((8, W), F32)
        h_ref[...] = jnp.zeros_like(h_ref)

    x = xg_ref[:, 0:W]
    gate = xg_ref[:, W:2 * W]
    ext_ref[8:Tt + 8, :] = x
    row = lax.broadcasted_iota(I32, (Tt, W), 0)
    xc = cb_ref[...] + x * cw_ref[CONV_WIDTH - 1:CONV_WIDTH, :]
    for s in range(1, CONV_WIDTH):
        xc = xc + ext_ref[8 - s:8 - s + Tt, :] * cw_ref[CONV_WIDTH - 1 - s:CONV_WIDTH - s, :]
    ext_ref[0:8, :] = ext_ref[Tt:Tt + 8, :]

    xcb = xc.astype(BF16)
    r = jax.nn.sigmoid(jnp.dot(xcb, wa_ref[...], preferred_element_type=F32) + ba_ref[...])
    ig = jax.nn.sigmoid(jnp.dot(xcb, wx_ref[...], preferred_element_type=F32) + bx_ref[...])
    z = -lam_ref[...]
    softplus = jnp.maximum(z, 0.0) + jnp.log1p(jnp.exp(-jnp.abs(z)))
    log_a = (-LRU_C) * r * softplus
    a = jnp.exp(log_a)
    mult = jnp.sqrt(1.0 - a * a)
    mult = jnp.where((row + t * Tt) == 0, 1.0, mult)
    b = mult * ig * xc

    d = 1
    while d < Tt:
        keep = row >= d
        b = jnp.where(keep, a * pltpu.roll(b, d, 0) + b, b)
        a = jnp.where(keep, a * pltpu.roll(a, d, 0), a)
        d *= 2
    hs = a * h_ref[7:8, :] + b
    h_ref[...] = hs[Tt - 8:Tt]
    o_ref[...] = (hs * jax.nn.gelu(gate)).astype(o_ref.dtype)


def _rglru(a_xg, conv_w, conv_b, wa_bd, b_a, wx_bd, b_x, lam, B, S):
    Tt = 256
    W = MIX_WIDTH
    vec = pl.BlockSpec((1, W), lambda b, t: (0, 0))
    mat = pl.BlockSpec((W, W), lambda b, t: (0, 0))
    out = pl.pallas_call(
        functools.partial(_lru_kernel, Tt=Tt),
        grid=(B, S // Tt),
        in_specs=[pl.BlockSpec((None, Tt, 2 * W), lambda b, t: (b, t, 0)),
                  pl.BlockSpec((CONV_WIDTH, W), lambda b, t: (0, 0)),
                  vec, mat, vec, mat, vec, vec],
        out_specs=pl.BlockSpec((None, Tt, W), lambda b, t: (b, t, 0)),
        out_shape=jax.ShapeDtypeStruct((B, S, W), BF16),
        scratch_shapes=[pltpu.VMEM((Tt + 8, W), F32), pltpu.VMEM((8, W), F32)],
        compiler_params=_cparams("arbitrary", "arbitrary"),
        name="rglru",
    )(a_xg.reshape(B, S, 2 * W), conv_w, conv_b, wa_bd, b_a, wx_bd, b_x, lam)
    return out.reshape(B * S, W)


def _lane_window(q128, lo, width, roll, scale):
    q = q128 * scale
    if roll:
        q = pltpu.roll(q, LANES // 2, 1)
    lane = lax.broadcasted_iota(I32, q.shape, 1)
    return jnp.where((lane >= lo) & (lane < lo + width), q, 0.0).astype(BF16)


def _pair_halves(even, odd, even_half, odd_half):
    if even_half == 1:
        even = pltpu.roll(even, LANES // 2, 1)
    if odd_half == 0:
        odd = pltpu.roll(odd, LANES // 2, 1)
    lane = lax.broadcasted_iota(I32, even.shape, 1)
    return jnp.where(lane < LANES // 2, even, odd)


def _banded_kernel(*refs, nback, window, q_cfg, has_sink, shared_kv):
    nk = nback + BANDED_Q_BLOCKS
    TQ = BANDED_Q_BLOCKS * Q_BLOCK
    q_ref = refs[0]
    k_refs = refs[1:1 + nk]
    pos = 1 + nk
    if shared_kv:
        v_refs = k_refs
    else:
        v_refs = refs[pos:pos + nk]
        pos += nk
    bias_ref = refs[pos]
    pos += 1
    sink_ref = None
    if has_sink:
        sink_ref = refs[pos]
        pos += 1
    o_ref = refs[pos]

    n = pl.program_id(1)
    KW = nk * Q_BLOCK
    kcat = jnp.concatenate([k_refs[nk - 1 - i][...] for i in range(nk)], axis=0)
    vcat = kcat if shared_kv else jnp.concatenate([v_refs[nk - 1 - i][...] for i in range(nk)], axis=0)
    vt = vcat.astype(F32).T.astype(BF16)
    jj = lax.broadcasted_iota(I32, (KW, TQ), 0)
    ii = lax.broadcasted_iota(I32, (KW, TQ), 1)
    dist = ii - jj + nback * Q_BLOCK
    valid = (dist >= 0) & (dist < window) & (jj >= (nback - BANDED_Q_BLOCKS * n) * Q_BLOCK)
    scale = HEAD_DIM ** -0.5

    def logits(h):
        lo, roll, _ = q_cfg[h]
        g = h // 2
        qx = _lane_window(q_ref[:, LANES * g:LANES * (g + 1)].astype(F32), lo, HEAD_DIM, roll, scale)
        s = lax.dot_general(kcat, qx, _NT, preferred_element_type=F32) + bias_ref[h, 0]
        return jnp.where(valid, s, NEG_INF)

    def softmax(h, s):
        m = jnp.max(s, axis=0, keepdims=True)
        if has_sink:
            sink = sink_ref[0:1, h:h + 1]
            m = jnp.maximum(m, sink)
        p = jnp.exp(s - m)
        den = jnp.sum(p, axis=0, keepdims=True)
        if has_sink:
            den = den + jnp.exp(sink - m)
        return p.astype(BF16), 1.0 / jnp.maximum(den, TINY)

    def values(h, p, inv):
        half = q_cfg[h][2]
        return jnp.dot(vt[HEAD_DIM * half:HEAD_DIM * (half + 1)], p, preferred_element_type=F32) * inv

    nh = len(q_cfg)
    ss = [logits(h) for h in range(nh)]
    pr = [softmax(h, ss[h]) for h in range(nh)]
    outs = [values(h, *pr[h]) for h in range(nh)]
    for g in range(nh // 2):
        pair = jnp.concatenate([outs[2 * g], outs[2 * g + 1]], axis=0)
        o_ref[:, LANES * g:LANES * (g + 1)] = pair.T.astype(o_ref.dtype)


def _banded_attention(qsrc, q_col, ksrc, k_col, vsrc, v_col, bias, sinks, *, B, S, window, q_cfg):
    nback = -(-window // Q_BLOCK)
    nq = BANDED_Q_BLOCKS
    nk = nback + nq
    TQ = nq * Q_BLOCK
    shared_kv = vsrc is None
    has_sink = sinks is not None

    def kv_spec(col, i):
        return pl.BlockSpec((None, Q_BLOCK, LANES), lambda b, n: (b, jnp.maximum(nq * n + nq - 1 - i, 0), col))

    in_specs = [pl.BlockSpec((None, TQ, 2 * LANES), lambda b, n: (b, n, q_col))]
    args = [qsrc]
    in_specs += [kv_spec(k_col, i) for i in range(nk)]
    args += [ksrc] * nk
    if not shared_kv:
        in_specs += [kv_spec(v_col, i) for i in range(nk)]
        args += [vsrc] * nk
    in_specs.append(pl.BlockSpec(bias.shape, lambda b, n: (0, 0, 0, 0)))
    args.append(bias)
    if has_sink:
        in_specs.append(pl.BlockSpec(sinks.shape, lambda b, n: (0, 0)))
        args.append(sinks)
    return pl.pallas_call(
        functools.partial(_banded_kernel, nback=nback, window=window, q_cfg=q_cfg,
                          has_sink=has_sink, shared_kv=shared_kv),
        grid=(B, S // TQ),
        in_specs=in_specs,
        out_specs=pl.BlockSpec((None, TQ, 2 * LANES), lambda b, n: (b, n, 0)),
        out_shape=jax.ShapeDtypeStruct((B, S, 2 * LANES), BF16),
        compiler_params=_cparams("arbitrary", "arbitrary"),
        name="banded_attn",
    )(*args)


def _flash_kernel(*refs, mode, lambda_init):
    TB = FLASH_BLOCK
    if mode == "diff":
        (q_ref, k_ref, vt_ref, bias_ref, lam_ref, g_ref, o_ref,
         qx_ref, m_ref, l_ref, acc_ref) = refs
        sel_ref = None
        maps = [dict(g=h // 2, lo=(h % 2) * 64 + 32 * mm, w=DIFF_DIM, roll=False, kg=h // 2, vrow=HEAD_DIM * h, h=h)
                for h in range(DIFF_HEADS) for mm in range(2)]
        scale = DIFF_DIM ** -0.5
    else:
        (q_ref, k_ref, vt_ref, sel_ref, bias_ref, o_ref,
         qx_ref, m_ref, l_ref, acc_ref) = refs
        maps = [dict(g=h // 2, lo=0, w=HEAD_DIM, roll=(h % 2 == 1), kg=0, vrow=HEAD_DIM, h=h)
                for h in range(NSA_HEADS)]
        scale = HEAD_DIM ** -0.5
    n = pl.program_id(1)

    for mp, c in enumerate(maps):
        qx_ref[mp] = _lane_window(q_ref[:, LANES * c["g"]:LANES * (c["g"] + 1)].astype(F32),
                                  c["lo"], c["w"], c["roll"], scale * LOG2E)
    m_ref[...] = jnp.full(m_ref.shape, NEG_INF, F32)
    l_ref[...] = jnp.zeros_like(l_ref)
    acc_ref[...] = jnp.zeros_like(acc_ref)

    key = lax.broadcasted_iota(I32, (TB, TB), 0)
    qry = lax.broadcasted_iota(I32, (TB, TB), 1)

    def step(j, kind, TK=TB):
        koff = pl.multiple_of(j * TB, TB)
        valid = None
        if kind == "diag":
            valid = qry >= key
        if sel_ref is not None:
            per = TK // NSA_SLC_LEN
            base = j * (TB // NSA_SLC_LEN)
            rows = [jnp.broadcast_to(sel_ref[pl.ds(base + i, 1), :], (NSA_SLC_LEN, TB)) for i in range(per)]
            picked = jnp.concatenate(rows, axis=0) > 0.5
            valid = picked if valid is None else (valid & picked)

        def logits(mp):
            c = maps[mp]
            kt = k_ref[pl.ds(koff, TK), LANES * c["kg"]:LANES * (c["kg"] + 1)]
            s = lax.dot_general(kt, qx_ref[mp], _NT, preferred_element_type=F32)
            if kind == "near":
                s = s + bias_ref[c["h"], 1]
            elif kind == "diag":
                s = s + bias_ref[c["h"], 0]
            if valid is not None:
                s = jnp.where(valid, s, NEG_INF)
            return s

        def softmax(mp, s):
            m_prev = m_ref[mp]
            m_new = jnp.maximum(m_prev, jnp.max(s, axis=0, keepdims=True))
            alpha = jnp.exp2(m_prev - m_new)
            p = jnp.exp2(s - m_new)
            l_ref[mp] = alpha * l_ref[mp] + jnp.sum(p, axis=0, keepdims=True)
            m_ref[mp] = m_new
            return p.astype(BF16), alpha

        def values(mp, p, alpha):
            c = maps[mp]
            vt = vt_ref[c["vrow"]:c["vrow"] + HEAD_DIM, pl.ds(koff, TK)]
            acc_ref[mp] = alpha * acc_ref[mp] + jnp.dot(vt, p, preferred_element_type=F32)

        nm = len(maps)
        ss = [logits(mp) for mp in range(nm)]

        def two_pass():
            pa = [softmax(mp, ss[mp]) for mp in range(nm)]
            for mp in range(nm):
                values(mp, *pa[mp])

        if kind == "diag":
            two_pass()
            return

        stabs = [m_ref[mp] for mp in range(nm)]
        ps = [jnp.exp2(ss[mp] - stabs[mp]) for mp in range(nm)]
        m_news = [jnp.maximum(stabs[mp], jnp.max(ss[mp], axis=0, keepdims=True)) for mp in range(nm)]
        growth = m_news[0] - stabs[0]
        for mp in range(1, nm):
            growth = jnp.maximum(growth, m_news[mp] - stabs[mp])
        safe = jnp.max(growth) <= FLASH_MAX_GROWTH
        psums = [jnp.sum(ps[mp], axis=0, keepdims=True) for mp in range(nm)]
        pvs = [jnp.dot(vt_ref[maps[mp]["vrow"]:maps[mp]["vrow"] + HEAD_DIM, pl.ds(koff, TK)], ps[mp].astype(BF16),
                       preferred_element_type=F32) for mp in range(nm)]

        @pl.when(safe)
        def _():
            for mp in range(nm):
                alpha = jnp.exp2(stabs[mp] - m_news[mp])
                l_ref[mp] = (l_ref[mp] + psums[mp]) * alpha
                acc_ref[mp] = (acc_ref[mp] + pvs[mp]) * alpha
                m_ref[mp] = m_news[mp]

        @pl.when(jnp.logical_not(safe))
        def _():
            two_pass()

    step(n, "diag")

    @pl.when(n >= 1)
    def _():
        step(n - 1, "near")

    nfar = jnp.maximum(n - 1, 0)
    FW = FLASH_FAR_BLOCKS

    def far_body(j, carry):
        step(FW * j, "far", FW * TB)
        return carry

    lax.fori_loop(0, nfar // FW, far_body, 0)

    done = (nfar // FW) * FW
    rest = nfar - done
    size = FW // 2
    while size >= 1:
        def tail_step(size=size):
            step(done + rest - rest % (2 * size), "far", size * TB)

        pl.when(rest % (2 * size) >= size)(tail_step)
        size //= 2

    def normalised(mp):
        return acc_ref[mp] * (1.0 / jnp.maximum(l_ref[mp], TINY))

    if mode == "diff":
        lq = lam_ref[...]
        lam = (jnp.exp(jnp.sum(lq[0:1] * lq[1:2], axis=1, keepdims=True))
               - jnp.exp(jnp.sum(lq[2:3] * lq[3:4], axis=1, keepdims=True)) + lambda_init)
        for g in range(DIFF_HEADS // 2):
            res = []
            for h in (2 * g, 2 * g + 1):
                o = normalised(2 * h) - lam * normalised(2 * h + 1)
                ms = jnp.sum(o * o, axis=0, keepdims=True) * (1.0 / (2 * DIFF_DIM))
                res.append(o * lax.rsqrt(ms + DIFF_SUBLN_EPS))
            out = jnp.concatenate(res, axis=0) * g_ref[...] * (1.0 - lambda_init)
            o_ref[:, LANES * g:LANES * (g + 1)] = out.T.astype(o_ref.dtype)
    else:
        for g in range(NSA_HEADS // 2):
            out = jnp.concatenate([normalised(2 * g), normalised(2 * g + 1)], axis=0)
            o_ref[:, LANES * g:LANES * (g + 1)] = out.T.astype(o_ref.dtype)


def _flash_attention(mode, qsrc, q_col, ksrc, k_col, vt, vt_rows, vt_blk, bias, extra, *, B, S, lambda_init=0.0):
    TB = FLASH_BLOCK
    q_spec = pl.BlockSpec((None, TB, 2 * LANES), lambda b, n: (b, n, q_col))
    bias_spec = pl.BlockSpec(bias.shape, lambda b, n: (0, 0, 0, 0))
    vt_spec = pl.BlockSpec((vt_rows, S), lambda b, n: (vt_blk, b))
    if mode == "diff":
        lam4, gsub = extra
        nmaps = 2 * DIFF_HEADS
        in_specs = [q_spec,
                    pl.BlockSpec((None, S, 2 * LANES), lambda b, n: (b, 0, k_col)),
                    vt_spec, bias_spec,
                    pl.BlockSpec(lam4.shape, lambda b, n: (0, 0)),
                    pl.BlockSpec(gsub.shape, lambda b, n: (0, 0))]
        args = [qsrc, ksrc, vt, bias, lam4, gsub]
    else:
        (sel,) = extra
        nmaps = NSA_HEADS
        in_specs = [q_spec,
                    pl.BlockSpec((None, S, LANES), lambda b, n: (b, 0, k_col)),
                    vt_spec,
                    pl.BlockSpec((None, LANES, TB), lambda b, n: (b, 0, n)),
                    bias_spec]
        args = [qsrc, ksrc, vt, sel, bias]
    return pl.pallas_call(
        functools.partial(_flash_kernel, mode=mode, lambda_init=lambda_init),
        grid=(B, S // TB),
        in_specs=in_specs,
        out_specs=pl.BlockSpec((None, TB, 2 * LANES), lambda b, n: (b, n, 0)),
        out_shape=jax.ShapeDtypeStruct((B, S, 2 * LANES), BF16),
        scratch_shapes=[pltpu.VMEM((nmaps, TB, LANES), BF16),
                        pltpu.VMEM((nmaps, 1, TB), F32),
                        pltpu.VMEM((nmaps, 1, TB), F32),
                        pltpu.VMEM((nmaps, HEAD_DIM, TB), F32)],
        compiler_params=_cparams("arbitrary", "arbitrary"),
        name="flash_" + mode,
    )(*args)


def _compress_kernel(ck_ref, cv_ref, pk_ref, pv_ref, w1k_ref, w1v_ref, w2_ref, o_ref):
    half = NSA_CMP_STRIDE * HEAD_DIM
    ck = ck_ref[...]
    cv = cv_ref[...]
    nrow = ck.shape[0]

    def mm(a, w):
        return jnp.dot(a, w, preferred_element_type=F32)

    top = mm(ck, w1k_ref[0:half, :]) + mm(cv, w1v_ref[0:half, :])
    bot = mm(ck, w1k_ref[half:2 * half, :]) + mm(cv, w1v_ref[half:2 * half, :])
    posw = (mm(pk_ref[...], w1k_ref[...]) + mm(pv_ref[...], w1v_ref[...]))[0:1]
    pre = top + pltpu.roll(bot, nrow - 1, 0) + posw
    kcv = mm(jax.nn.gelu(pre).astype(BF16), w2_ref[...])
    o_ref[0:LANES, :] = jnp.zeros((LANES, LANES), o_ref.dtype)
    o_ref[LANES:LANES + nrow, :] = kcv.astype(o_ref.dtype)


def _nsa_compress(ck, cv, pk, pv, w1k, w1v, w2, ncp):
    B, nchunk, width = ck.shape
    assert ncp == LANES + nchunk

    def full(a):
        return pl.BlockSpec(a.shape, lambda b: (0,) * a.ndim)

    chunk = pl.BlockSpec((None, nchunk, width), lambda b: (b, 0, 0))
    return pl.pallas_call(
        _compress_kernel,
        grid=(B,),
        in_specs=[chunk, chunk, full(pk), full(pv), full(w1k), full(w1v), full(w2)],
        out_specs=pl.BlockSpec((None, ncp, LANES), lambda b: (b, 0, 0)),
        out_shape=jax.ShapeDtypeStruct((B, ncp, LANES), BF16),
        compiler_params=_cparams("arbitrary"),
        name="nsa_compress",
    )(ck, cv, pk, pv, w1k, w1v, w2)


def _cmp_kernel(q_ref, kcv_ref, bias_ref, agg_ref, oc_ref, sel_ref, s_scr, *, n_cmp, ncp):
    n = pl.program_id(1)
    QB = CMP_Q
    kcv = kcv_ref[...]
    cp = lax.broadcasted_iota(I32, (QB, ncp), 1)
    qpos = n * QB + lax.broadcasted_iota(I32, (QB, ncp), 0)
    valid = ((cp >= LANES) & (cp < LANES + n_cmp)
             & (qpos - NSA_CMP_STRIDE * (cp - LANES) - (NSA_CMP_LEN - 1) >= 0))
    woff = pl.multiple_of((n // (16 * LANES // CMP_Q)) * LANES, LANES)
    scale = HEAD_DIM ** -0.5

    def logits(h):
        g = h // 2
        qx = _lane_window(q_ref[:, LANES * g:LANES * (g + 1)].astype(F32), 0, HEAD_DIM, h % 2 == 1, scale)
        s_scr[h] = lax.dot_general(qx, kcv, _NT, preferred_element_type=F32)
        s_scr[h, :, pl.ds(woff, 2 * LANES)] += bias_ref[h]

    def softmax(h):
        s = jnp.where(valid, s_scr[h], NEG_INF)
        m = jnp.max(s, axis=1, keepdims=True)
        p = jnp.where(valid, jnp.exp(s - m), 0.0)
        den = jnp.sum(p, axis=1, keepdims=True)
        return p * (1.0 / jnp.maximum(den, TINY))

    for h in range(NSA_HEADS):
        logits(h)
    ps = [softmax(h) for h in range(NSA_HEADS)]
    outs = [jnp.dot(p.astype(BF16), kcv, preferred_element_type=F32) for p in ps]
    for g in range(NSA_HEADS // 2):
        oc_ref[:, LANES * g:LANES * (g + 1)] = _pair_halves(outs[2 * g], outs[2 * g + 1], 1, 1).astype(oc_ref.dtype)

    psum = (ps[0] + ps[1]) + (ps[2] + ps[3])
    imp = jnp.dot(psum.astype(BF16), agg_ref[...], preferred_element_type=F32)

    v = imp.T
    jb = lax.broadcasted_iota(I32, v.shape, 0)
    qp = n * QB + lax.broadcasted_iota(I32, v.shape, 1)
    cur = qp >> 6
    forced = (jb == 0) | (jb == cur) | (jb == cur - 1)
    v = jnp.where(forced, -jnp.inf, jnp.where(jb * NSA_SLC_LEN <= qp, v, -1.0))
    rid = jb.astype(F32)
    picked = jnp.where(forced, 1.0, 0.0)
    for _ in range(NSA_TOPK - 3):
        mx = jnp.max(v, axis=0, keepdims=True)
        first = jnp.min(jnp.where(v == mx, rid, float(LANES)), axis=0, keepdims=True)
        hit = rid == first
        picked = jnp.where(hit, 1.0, picked)
        v = jnp.where(hit, -jnp.inf, v)
    sel_ref[...] = picked


def _nsa_cmp_select(dsrc, kcv, bias, agg, *, B, S, n_cmp, ncp):
    return pl.pallas_call(
        functools.partial(_cmp_kernel, n_cmp=n_cmp, ncp=ncp),
        grid=(B, S // CMP_Q),
        in_specs=[pl.BlockSpec((None, CMP_Q, 2 * LANES), lambda b, n: (b, n, 0)),
                  pl.BlockSpec((None, ncp, LANES), lambda b, n: (b, 0, 0)),
                  pl.BlockSpec((NSA_HEADS, None, CMP_Q, 2 * LANES), lambda b, n: (0, n % (16 * LANES // CMP_Q), 0, 0)),
                  pl.BlockSpec((ncp, LANES), lambda b, n: (0, 0))],
        out_specs=[pl.BlockSpec((None, CMP_Q, 2 * LANES), lambda b, n: (b, n, 0)),
                   pl.BlockSpec((None, LANES, CMP_Q), lambda b, n: (b, 0, n))],
        out_shape=[jax.ShapeDtypeStruct((B, S, 2 * LANES), BF16),
                   jax.ShapeDtypeStruct((B, LANES, S), F32)],
        scratch_shapes=[pltpu.VMEM((NSA_HEADS, CMP_Q, ncp), F32)],
        compiler_params=_cparams("arbitrary", "arbitrary"),
        name="nsa_cmp_select",
    )(dsrc, kcv, bias, agg)


def _merge_kernel(x_ref, g_ref, ya_ref, yb_ref, yc_ref, oc_ref, os_ref, ow_ref, dg_ref,
                  wg_ref, bg_ref, wb_ref, wo_ref, o_ref):
    x = x_ref[...]
    h = _rms(x, g_ref[...]).astype(BF16)
    tm = x.shape[0]
    lane = lax.broadcasted_iota(I32, (tm, LANES), 1)
    gd = jax.nn.sigmoid(dg_ref[...].astype(F32))
    yd = []
    for g in range(NSA_HEADS // 2):
        acc = jnp.zeros((tm, LANES), F32)
        for br, src in enumerate((oc_ref, os_ref, ow_ref)):
            ge = gd[:, 3 * (2 * g) + br:3 * (2 * g) + br + 1]
            go = gd[:, 3 * (2 * g + 1) + br:3 * (2 * g + 1) + br + 1]
            acc = acc + jnp.where(lane < 64, ge, go) * src[:, LANES * g:LANES * (g + 1)].astype(F32)
        yd.append(acc)
    y_d = jnp.concatenate(yd, axis=1).astype(BF16)
    ys = (ya_ref[...], yb_ref[...], yc_ref[...], y_d)
    merged = jnp.zeros((tm, D_MODEL), F32)
    for br in range(4):
        gate = jax.nn.sigmoid(jnp.dot(h, wg_ref[br], preferred_element_type=F32) + bg_ref[br:br + 1, :])
        merged = merged + gate * jnp.dot(ys[br], wb_ref[br], preferred_element_type=F32)
    o_ref[...] = x + jnp.dot(merged.astype(BF16), wo_ref[...], preferred_element_type=F32)


def _merge(x, g, ya, yb, yc, oc, osel, ow, dproj, wg, bg, wb, wo):
    T = x.shape[0]
    tm = 256

    def tile(width, col=0):
        return pl.BlockSpec((tm, width), lambda i: (i, col))

    def full(a):
        return pl.BlockSpec(a.shape, lambda i: (0,) * a.ndim)

    return pl.pallas_call(
        _merge_kernel,
        grid=(T // tm,),
        in_specs=[tile(D_MODEL), full(g), tile(256), tile(256), tile(256), tile(256), tile(256), tile(256),
                  tile(LANES, 5), full(wg), full(bg), full(wb), full(wo)],
        out_specs=tile(D_MODEL),
        out_shape=jax.ShapeDtypeStruct((T, D_MODEL), F32),
        compiler_params=_cparams("arbitrary"),
        name="merge",
    )(x, g, ya, yb, yc, oc, osel, ow, dproj, wg, bg, wb, wo)


def _route(hn, w_split, b_router):
    lane = lax.broadcasted_iota(I32, (hn.shape[0], LANES), 1)
    hi = hn.astype(BF16)
    lo = (hn - hi.astype(F32)).astype(BF16)
    a = jnp.dot(hi, w_split, preferred_element_type=F32)
    b = jnp.dot(lo, w_split[:, 0:LANES], preferred_element_type=F32)
    logits = (a[:, 0:LANES] + a[:, LANES:2 * LANES]) + b + b_router
    is_grp = (lane >= N_EXPERTS) & (lane < N_EXPERTS + N_GROUPS)
    lanef = lane.astype(F32)
    lg = jnp.where(is_grp, logits, -jnp.inf)
    mg = jnp.max(lg, axis=1, keepdims=True)
    gsel = jnp.min(jnp.where(lg == mg, lanef, 2.0 * LANES), axis=1, keepdims=True) - N_EXPERTS
    g_w = 1.0 / jnp.sum(jnp.where(is_grp, jnp.exp(lg - mg), 0.0), axis=1, keepdims=True)
    in_grp = (lane < N_EXPERTS) & ((lane >> 3).astype(F32) == gsel)
    v1 = jnp.where(in_grp, logits, -jnp.inf)
    m1 = jnp.max(v1, axis=1, keepdims=True)
    i1 = jnp.min(jnp.where(v1 == m1, lanef, 2.0 * LANES), axis=1, keepdims=True)
    v2 = jnp.where(lanef == i1, -jnp.inf, v1)
    m2 = jnp.max(v2, axis=1, keepdims=True)
    i2 = jnp.min(jnp.where(v2 == m2, lanef, 2.0 * LANES), axis=1, keepdims=True)
    e2 = jnp.exp(m2 - m1)
    w1 = g_w / (1.0 + e2)
    w2 = g_w * e2 / (1.0 + e2)
    first, second = lanef == i1, lanef == i2
    comb = jnp.where(first, w1, 0.0) + jnp.where(second, w2, 0.0)
    return comb, jnp.where(first | second, 1.0, 0.0)


def _moe_kernel(x_ref, g_ref, wr_ref, br_ref, wg_ref, wu_ref, wd_ref, fg_ref, o_ref,
                hn_ref, comb_ref, acc_ref, *, final_norm):
    e = pl.program_id(1)
    tm = x_ref.shape[0]
    lane = lax.broadcasted_iota(I32, (tm, LANES), 1)

    @pl.when(e == 0)
    def _():
        hn = _rms(x_ref[...], g_ref[...])
        hn_ref[...] = hn.astype(BF16)
        comb_ref[...] = _route(hn, wr_ref[...], br_ref[...])[0]
        acc_ref[...] = jnp.zeros_like(acc_ref)

    hn = hn_ref[...]
    gt = jnp.dot(hn, wg_ref[...], preferred_element_type=F32)
    up = jnp.dot(hn, wu_ref[...], preferred_element_type=F32)
    he = (gt * jax.nn.sigmoid(gt) * up).astype(BF16)
    ye = jnp.dot(he, wd_ref[...], preferred_element_type=F32)
    cw = jnp.sum(jnp.where(lane == e, comb_ref[...], 0.0), axis=1, keepdims=True)
    acc_ref[...] += cw * ye

    @pl.when(e == N_EXPERTS - 1)
    def _():
        y = x_ref[...] + acc_ref[...]
        if final_norm:
            y = _rms(y, fg_ref[...])
        o_ref[...] = y


def _moe(x, g, w_router, b_router, wg, wu, wd, final_g, final_norm):
    T = x.shape[0]
    tm = 1024
    return pl.pallas_call(
        functools.partial(_moe_kernel, final_norm=final_norm),
        grid=(T // tm, N_EXPERTS),
        in_specs=[pl.BlockSpec((tm, D_MODEL), lambda i, e: (i, 0)),
                  pl.BlockSpec((1, D_MODEL), lambda i, e: (0, 0)),
                  pl.BlockSpec((D_MODEL, 2 * LANES), lambda i, e: (0, 0)),
                  pl.BlockSpec((1, LANES), lambda i, e: (0, 0)),
                  pl.BlockSpec((None, D_MODEL, EXPERT_FF), lambda i, e: (e, 0, 0)),
                  pl.BlockSpec((None, D_MODEL, EXPERT_FF), lambda i, e: (e, 0, 0)),
                  pl.BlockSpec((None, EXPERT_FF, D_MODEL), lambda i, e: (e, 0, 0)),
                  pl.BlockSpec((1, D_MODEL), lambda i, e: (0, 0))],
        out_specs=pl.BlockSpec((tm, D_MODEL), lambda i, e: (i, 0)),
        out_shape=jax.ShapeDtypeStruct((T, D_MODEL), F32),
        scratch_shapes=[pltpu.VMEM((tm, D_MODEL), BF16),
                        pltpu.VMEM((tm, LANES), F32),
                        pltpu.VMEM((tm, D_MODEL), F32)],
        compiler_params=_cparams("arbitrary", "arbitrary"),
        name="moe",
    )(x, g, w_router, b_router, wg, wu, wd, final_g)


def _slot_matrix(pos_t, val_t, ntok, half):
    nslot = MOE_CAP // 2
    slot = (lax.broadcasted_iota(I32, (nslot, ntok), 0) + half * nslot).astype(F32)
    blocks = []
    for e in range(N_EXPERTS):
        hit = pos_t[e:e + 1, :] == slot
        blocks.append(jnp.where(hit, 1.0 if val_t is None else val_t[e:e + 1, :], 0.0).astype(BF16))
    return jnp.concatenate(blocks, axis=0)


def _dispatch_kernel(x_ref, g_ref, wr_ref, br_ref, tri_ref, xd_ref, pos_ref, wt_ref, cnt_ref):
    tm = x_ref.shape[0]
    hn = _rms(x_ref[...], g_ref[...])
    comb, routed = _route(hn, wr_ref[...], br_ref[...])
    rank = jnp.dot(tri_ref[...], routed.astype(BF16), preferred_element_type=F32)
    pos = jnp.where(routed > 0.0, rank, -1.0)
    pos_t = pos.T
    pos_ref[...] = pos_t
    wt_ref[...] = comb.T
    counts = jnp.sum(routed, axis=0, keepdims=True)
    cnt_ref[...] = jnp.broadcast_to(counts, cnt_ref.shape)
    hnb = hn.astype(BF16)
    nslot = MOE_CAP // 2
    lower = jnp.dot(_slot_matrix(pos_t, None, tm, 0), hnb, preferred_element_type=F32)
    xd_ref[:, 0:nslot, :] = lower.reshape(N_EXPERTS, nslot, D_MODEL).astype(xd_ref.dtype)

    busy = jnp.max(counts) > nslot

    @pl.when(busy)
    def _():
        upper = jnp.dot(_slot_matrix(pos_t, None, tm, 1), hnb, preferred_element_type=F32)
        xd_ref[:, nslot:MOE_CAP, :] = upper.reshape(N_EXPERTS, nslot, D_MODEL).astype(xd_ref.dtype)

    @pl.when(jnp.logical_not(busy))
    def _():
        xd_ref[:, nslot:MOE_CAP, :] = jnp.zeros((N_EXPERTS, nslot, D_MODEL), xd_ref.dtype)


def _expert_kernel(busy_ref, xlo_ref, xhi_ref, wg_ref, wu_ref, wd_ref, olo_ref, ohi_ref, wgb_ref, wub_ref, wdb_ref):
    ntile, nslot, _ = xlo_ref.shape

    @pl.when(pl.program_id(1) == 0)
    def _():
        wgb_ref[...] = wg_ref[...].astype(BF16)
        wub_ref[...] = wu_ref[...].astype(BF16)
        wdb_ref[...] = wd_ref[...].astype(BF16)

    def ffn(x_ref, o_ref):
        x = x_ref[...].reshape(ntile * nslot, D_MODEL)
        gt = jnp.dot(x, wgb_ref[...], preferred_element_type=F32)
        up = jnp.dot(x, wub_ref[...], preferred_element_type=F32)
        he = (gt * jax.nn.sigmoid(gt) * up).astype(BF16)
        ye = jnp.dot(he, wdb_ref[...], preferred_element_type=F32)
        o_ref[...] = ye.reshape(ntile, nslot, D_MODEL).astype(o_ref.dtype)

    ffn(xlo_ref, olo_ref)
    busy = busy_ref[pl.program_id(0), pl.program_id(1)] > 0

    @pl.when(busy)
    def _():
        ffn(xhi_ref, ohi_ref)

    @pl.when(jnp.logical_not(busy))
    def _():
        ohi_ref[...] = jnp.zeros(ohi_ref.shape, ohi_ref.dtype)


def _combine_kernel(busy_ref, x_ref, ylo_ref, yhi_ref, pos_ref, wt_ref, fg_ref, o_ref, *, final_norm):
    tm = x_ref.shape[0]
    nslot = MOE_CAP // 2

    def gathered(y_ref, half):
        weights = _slot_matrix(pos_ref[...], wt_ref[...], tm, half)
        yd = y_ref[...].reshape(N_EXPERTS * nslot, D_MODEL)
        return lax.dot_general(weights, yd, (((0,), (0,)), ((), ())), preferred_element_type=F32)

    o_ref[...] = x_ref[...] + gathered(ylo_ref, 0)

    @pl.when(busy_ref[pl.program_id(0)] > 0)
    def _():
        o_ref[...] += gathered(yhi_ref, 1)

    if final_norm:
        o_ref[...] = _rms(o_ref[...], fg_ref[...])


def _moe_dispatch(x, g, w_router, b_router):
    T = x.shape[0]
    tm = MOE_TILE
    nt = T // tm
    tri = jnp.asarray(np.tril(np.ones((tm, tm), np.float32), -1), BF16)
    xd, pos_t, w_t, cnt = pl.pallas_call(
        _dispatch_kernel,
        grid=(nt,),
        in_specs=[pl.BlockSpec((tm, D_MODEL), lambda i: (i, 0)),
                  pl.BlockSpec((1, D_MODEL), lambda i: (0, 0)),
                  pl.BlockSpec((D_MODEL, 2 * LANES), lambda i: (0, 0)),
                  pl.BlockSpec((1, LANES), lambda i: (0, 0)),
                  pl.BlockSpec((tm, tm), lambda i: (0, 0))],
        out_specs=[pl.BlockSpec((N_EXPERTS, None, MOE_CAP, D_MODEL), lambda i: (0, i, 0, 0)),
                   pl.BlockSpec((LANES, tm), lambda i: (0, i)),
                   pl.BlockSpec((LANES, tm), lambda i: (0, i)),
                   pl.BlockSpec((None, 8, LANES), lambda i: (i, 0, 0))],
        out_shape=[jax.ShapeDtypeStruct((N_EXPERTS, nt, MOE_CAP, D_MODEL), BF16),
                   jax.ShapeDtypeStruct((LANES, T), F32),
                   jax.ShapeDtypeStruct((LANES, T), F32),
                   jax.ShapeDtypeStruct((nt, 8, LANES), F32)],
        compiler_params=_cparams("arbitrary"),
        name="moe_dispatch",
    )(x, g, w_router, b_router, tri)
    return xd, pos_t, w_t, cnt[:, 0, 0:N_EXPERTS]


def _moe_finish(x, xd, pos_t, w_t, counts, wg, wu, wd, final_g, final_norm):
    T = x.shape[0]
    tm = MOE_TILE
    nt = T // tm
    tps = min(MOE_TILES_PER_STEP, nt)
    assert nt % tps == 0
    nslot = MOE_CAP // 2
    hot = counts > nslot
    busy_blocks = jnp.any(hot.reshape(nt // tps, tps, N_EXPERTS), axis=1).T.astype(I32)
    busy_tiles = jnp.any(hot, axis=1).astype(I32)
    halves = xd.reshape(N_EXPERTS, nt, 2, nslot, D_MODEL)
    half_out = jax.ShapeDtypeStruct((N_EXPERTS, nt, nslot, D_MODEL), BF16)
    yd_lo, yd_hi = pl.pallas_call(
        _expert_kernel,
        grid_spec=pltpu.PrefetchScalarGridSpec(
            num_scalar_prefetch=1,
            grid=(N_EXPERTS, nt // tps),
            in_specs=[pl.BlockSpec((None, tps, None, nslot, D_MODEL), lambda e, c, b: (e, c, 0, 0, 0)),
                      pl.BlockSpec((None, tps, None, nslot, D_MODEL),
                                   lambda e, c, b: (e, jnp.where(b[e, c] > 0, c, 0), 1, 0, 0)),
                      pl.BlockSpec((None, D_MODEL, EXPERT_FF), lambda e, c, b: (e, 0, 0)),
                      pl.BlockSpec((None, D_MODEL, EXPERT_FF), lambda e, c, b: (e, 0, 0)),
                      pl.BlockSpec((None, EXPERT_FF, D_MODEL), lambda e, c, b: (e, 0, 0))],
            out_specs=[pl.BlockSpec((None, tps, nslot, D_MODEL), lambda e, c, b: (e, c, 0, 0)),
                       pl.BlockSpec((None, tps, nslot, D_MODEL), lambda e, c, b: (e, c, 0, 0))],
            scratch_shapes=[pltpu.VMEM((D_MODEL, EXPERT_FF), BF16), pltpu.VMEM((D_MODEL, EXPERT_FF), BF16),
                            pltpu.VMEM((EXPERT_FF, D_MODEL), BF16)]),
        out_shape=[half_out, half_out],
        compiler_params=_cparams("arbitrary", "arbitrary"),
        name="moe_experts",
    )(busy_blocks, halves, halves, wg, wu, wd)
    return pl.pallas_call(
        functools.partial(_combine_kernel, final_norm=final_norm),
        grid_spec=pltpu.PrefetchScalarGridSpec(
            num_scalar_prefetch=1,
            grid=(nt,),
            in_specs=[pl.BlockSpec((tm, D_MODEL), lambda i, b: (i, 0)),
                      pl.BlockSpec((N_EXPERTS, None, nslot, D_MODEL), lambda i, b: (0, i, 0, 0)),
                      pl.BlockSpec((N_EXPERTS, None, nslot, D_MODEL),
                                   lambda i, b: (0, jnp.where(b[i] > 0, i, 0), 0, 0)),
                      pl.BlockSpec((LANES, tm), lambda i, b: (0, i)),
                      pl.BlockSpec((LANES, tm), lambda i, b: (0, i)),
                      pl.BlockSpec((1, D_MODEL), lambda i, b: (0, 0))],
            out_specs=pl.BlockSpec((tm, D_MODEL), lambda i, b: (i, 0))),
        out_shape=jax.ShapeDtypeStruct((T, D_MODEL), F32),
        compiler_params=_cparams("arbitrary"),
        name="moe_combine",
    )(busy_tiles, x, yd_lo, yd_hi, pos_t, w_t, final_g)


def _block_diag(w):
    nb, bs, _ = w.shape
    out = jnp.zeros((nb * bs, nb * bs), w.dtype)
    for i in range(nb):
        out = out.at[i * bs:(i + 1) * bs, i * bs:(i + 1) * bs].set(w[i])
    return out


def _selection_aggregator_np(n_cmp, n_slc):
    ratio_s = NSA_SLC_LEN // NSA_CMP_STRIDE
    ratio_c = NSA_CMP_LEN // NSA_CMP_STRIDE
    w = np.convolve(np.ones(ratio_s), np.ones(ratio_c)).astype(np.float32)
    cj = np.arange(n_slc)[:, None] * ratio_s - (ratio_c - 1) + np.arange(w.size)[None, :]
    jj = np.broadcast_to(np.arange(n_slc)[:, None], cj.shape)
    ww = np.broadcast_to(w[None, :], cj.shape)
    keep = (cj >= 0) & (cj < n_cmp)
    agg = np.zeros((n_cmp, n_slc), np.float32)
    np.add.at(agg, (cj[keep], jj[keep]), ww[keep])
    return agg


def _row(v):
    return v.reshape(1, -1).astype(F32)


def _diff_lambda_init(layer):
    return 0.8 - 0.6 * math.exp(-0.3 * layer)


def _mixer_layer(x, l, B, S, tiles, p):
    (bias_swa, bias_diff, bias_slc, bias_win, bias_cmp, agg_pad, n_cmp, ncp) = tiles
    w_in = jnp.pad(p["w_in"][l], ((0, 0), (0, N_IN_PAD - p["w_in"].shape[2]))).astype(BF16)
    w_t = jnp.concatenate([w_in[:, 1536:1792], w_in[:, 2176:2304]], axis=1).T
    a_xg, b_qkv, c_qkv, d_all, vals_t = _in_proj(x, _row(p["mix_norm_g"][l]), w_in, w_t)

    y_a = _rglru(a_xg, p["conv_w"][l].astype(F32), _row(p["conv_b"][l]),
                 _block_diag(p["lru_w_a"][l]).astype(BF16), _row(p["lru_b_a"][l]),
                 _block_diag(p["lru_w_x"][l]).astype(BF16), _row(p["lru_b_x"][l]),
                 _row(p["lru_lambda"][l]), B, S)

    b3 = b_qkv.reshape(B, S, 512)
    swa_cfg = tuple(((h // 2) * 64, (h % 2) != (h // 2), h // 2) for h in range(SWA_HEADS))
    y_b = _banded_attention(b3, 0, b3, 2, b3, 3, bias_swa, _row(p["swa_sinks"][l]),
                            B=B, S=S, window=SWA_WINDOW, q_cfg=swa_cfg)

    c3 = c_qkv.reshape(B, S, 768)
    lam4 = jnp.stack([p["diff_lq1"][l], p["diff_lk1"][l], p["diff_lq2"][l], p["diff_lk2"][l]]).astype(F32)
    gsub = jnp.tile(p["diff_subln_g"][l].reshape(-1, 1).astype(F32), (2, 1))
    y_c = _flash_attention("diff", c3, 0, c3, 1, vals_t, 256, 0, bias_diff, (lam4, gsub), B=B, S=S,
                           lambda_init=_diff_lambda_init(l))

    d3 = d_all.reshape(B, S, 768)
    nchunk = S // NSA_CMP_STRIDE
    ck = d3[:, :, 256:320].reshape(B, nchunk, NSA_CMP_STRIDE * HEAD_DIM)
    cv = d3[:, :, 320:384].reshape(B, nchunk, NSA_CMP_STRIDE * HEAD_DIM)

    def pos8(a):
        return jnp.broadcast_to(a.reshape(1, -1), (8, a.size)).astype(BF16)

    w1k = jnp.pad(p["nsa_w1_k"][l], ((0, 0), (0, HEAD_DIM))).astype(BF16)
    w1v = jnp.pad(p["nsa_w1_v"][l], ((0, 0), (HEAD_DIM, 0))).astype(BF16)
    w2 = _block_diag(jnp.stack([p["nsa_w2_k"][l], p["nsa_w2_v"][l]])).astype(BF16)
    kcv = _nsa_compress(ck, cv, pos8(p["nsa_pos_k"][l]), pos8(p["nsa_pos_v"][l]), w1k, w1v, w2, ncp)
    o_c, sel = _nsa_cmp_select(d3, kcv, bias_cmp, agg_pad, B=B, S=S, n_cmp=n_cmp, ncp=ncp)
    o_s = _flash_attention("nsa", d3, 0, d3, 3, vals_t, 128, 2, bias_slc, (sel,), B=B, S=S)
    win_cfg = tuple((0, h % 2 == 1, 1) for h in range(NSA_HEADS))
    o_w = _banded_attention(d3, 0, d3, 4, None, None, bias_win, None, B=B, S=S, window=NSA_WINDOW, q_cfg=win_cfg)

    T = B * S
    return _merge(x, _row(p["mix_norm_g"][l]), y_a, y_b.reshape(T, 256), y_c.reshape(T, 256),
                  o_c.reshape(T, 256), o_s.reshape(T, 256), o_w.reshape(T, 256), d_all,
                  p["w_gate"][l].astype(BF16), p["b_gate"][l].astype(F32), p["w_branch"][l].astype(BF16),
                  p["w_out"][l].astype(BF16))


def _moe_layer(x, l, p, final_norm):
    w_router = jnp.concatenate([p["w_router_exp"][l], p["w_router_grp"][l]], axis=1)
    w_router = jnp.pad(w_router, ((0, 0), (0, LANES - w_router.shape[1]))).astype(F32)
    w_hi = w_router.astype(BF16)
    w_router = jnp.concatenate([w_hi, (w_router - w_hi.astype(F32)).astype(BF16)], axis=1)
    b_router = jnp.concatenate([p["b_router_exp"][l], p["b_router_grp"][l]])
    b_router = jnp.pad(b_router, (0, LANES - b_router.shape[0])).reshape(1, LANES).astype(F32)
    g = _row(p["ffn_norm_g"][l])
    fg = _row(p["final_norm_g"])
    wg, wu, wd = (p[k][l].astype(F32) for k in ("w_exp_gate", "w_exp_up", "w_exp_down"))
    xd, pos_t, w_t, counts = _moe_dispatch(x, g, w_router, b_router)
    return lax.cond(jnp.max(counts) > MOE_CAP,
                    lambda: _moe(x, g, w_router, b_router, wg.astype(BF16), wu.astype(BF16), wd.astype(BF16),
                                 fg, final_norm),
                    lambda: _moe_finish(x, xd, pos_t, w_t, counts, wg, wu, wd, fg, final_norm))


def _forward(p):
    x = p["x"]
    B, S, D = x.shape
    depth = p["w_in"].shape[0]
    assert D == D_MODEL and S % (16 * Q_BLOCK) == 0 and S // NSA_SLC_LEN <= LANES
    rel = p["rel_bias"].astype(F32)
    n_cmp = (S - NSA_CMP_LEN) // NSA_CMP_STRIDE + 1
    ncp = LANES + S // NSA_CMP_STRIDE
    agg = _selection_aggregator_np(n_cmp, S // NSA_SLC_LEN)
    agg_pad = np.zeros((ncp, LANES), np.float32)
    agg_pad[LANES:LANES + n_cmp, :agg.shape[1]] = agg
    tiles = (
        _bias_tiles(rel, _banded_buckets(1), 0, SWA_HEADS, False),
        _bias_tiles(rel, _flash_buckets(), SWA_HEADS, DIFF_HEADS, True, LOG2E),
        _bias_tiles(rel, _flash_buckets(), SWA_HEADS + DIFF_HEADS, NSA_HEADS, True, LOG2E),
        _bias_tiles(rel, _banded_buckets(NSA_WINDOW // Q_BLOCK), SWA_HEADS + DIFF_HEADS, NSA_HEADS, False),
        _bias_tiles(rel, _cmp_buckets(), SWA_HEADS + DIFF_HEADS, NSA_HEADS, True),
        jnp.asarray(agg_pad, BF16), n_cmp, ncp,
    )
    xt = x.reshape(B * S, D).astype(F32)
    for l in range(depth):
        xt = _mixer_layer(xt, l, B, S, tiles, p)
        xt = _moe_layer(xt, l, p, final_norm=(l == depth - 1))
    return xt.reshape(B, S, D).astype(x.dtype)


def kernel(x, rel_bias, final_norm_g, mix_norm_g, w_in, conv_w, conv_b, lru_w_a, lru_b_a, lru_w_x, lru_b_x,
           lru_lambda, swa_sinks, diff_lq1, diff_lk1, diff_lq2, diff_lk2, diff_subln_g, nsa_pos_k, nsa_w1_k,
           nsa_w2_k, nsa_pos_v, nsa_w1_v, nsa_w2_v, w_gate, b_gate, w_branch, w_out, ffn_norm_g, w_router_grp,
           b_router_grp, w_router_exp, b_router_exp, w_exp_gate, w_exp_up, w_exp_down):
    return _forward(dict(
        x=x, rel_bias=rel_bias, final_norm_g=final_norm_g, mix_norm_g=mix_norm_g, w_in=w_in, conv_w=conv_w,
        conv_b=conv_b, lru_w_a=lru_w_a, lru_b_a=lru_b_a, lru_w_x=lru_w_x, lru_b_x=lru_b_x, lru_lambda=lru_lambda,
        swa_sinks=swa_sinks, diff_lq1=diff_lq1, diff_lk1=diff_lk1, diff_lq2=diff_lq2, diff_lk2=diff_lk2,
        diff_subln_g=diff_subln_g, nsa_pos_k=nsa_pos_k, nsa_w1_k=nsa_w1_k, nsa_w2_k=nsa_w2_k, nsa_pos_v=nsa_pos_v,
        nsa_w1_v=nsa_w1_v, nsa_w2_v=nsa_w2_v, w_gate=w_gate, b_gate=b_gate, w_branch=w_branch, w_out=w_out,
        ffn_norm_g=ffn_norm_g, w_router_grp=w_router_grp, b_router_grp=b_router_grp, w_router_exp=w_router_exp,
        b_router_exp=b_router_exp, w_exp_gate=w_exp_gate, w_exp_up=w_exp_up, w_exp_down=w_exp_down))
```

```python
import functools
import math

import numpy as np
import jax
import jax.numpy as jnp
from jax import lax
from jax.experimental import pallas as pl
from jax.experimental.pallas import tpu as pltpu

F32 = jnp.float32
BF16 = jnp.bfloat16
I32 = jnp.int32

D_MODEL = 1024
HEAD_DIM = 64
MIX_WIDTH = 256
LRU_C = 8.0
CONV_WIDTH = 4
SWA_HEADS = 4
SWA_WINDOW = 128
DIFF_HEADS = 4
DIFF_DIM = 32
DIFF_SUBLN_EPS = 1e-5
NSA_HEADS = 4
NSA_CMP_LEN = 32
NSA_CMP_STRIDE = 16
NSA_SLC_LEN = 64
NSA_TOPK = 16
NSA_WINDOW = 512
NSA_FORCED_SCORE = 1e4
REL_BUCKETS = 32
REL_MAX_DIST = 128
N_GROUPS = 4
EXPERTS_PER_GROUP = 8
N_EXPERTS = 32
EXPERT_FF = 256
NORM_EPS = 1e-6
NEG_INF = -1e30
TINY = 1e-30
LOG2E = math.log2(math.e)

LANES = 128
Q_BLOCK = 128
CMP_Q = 256
BANDED_Q_BLOCKS = 2
FLASH_BLOCK = 256
MOE_TILE = 512
MOE_CAP = 128
MOE_TILES_PER_STEP = 8
FLASH_MAX_GROWTH = 64.0
FLASH_FAR_BLOCKS = 8
N_IN_PAD = 2560
VMEM_LIMIT = 56 * 1024 * 1024

_NT = (((1,), (1,)), ((), ()))


def _cparams(*sem):
    return pltpu.CompilerParams(dimension_semantics=sem, vmem_limit_bytes=VMEM_LIMIT)


def _t5_bucket_np(dist):
    n = np.maximum(dist, 0)
    exact = REL_BUCKETS // 2
    scaled = (np.log(np.maximum(n, 1).astype(np.float32) / np.float32(exact))
              / np.float32(math.log(REL_MAX_DIST / exact))).astype(np.float32)
    large = np.minimum(exact + (scaled * np.float32(REL_BUCKETS - exact)).astype(np.int32), REL_BUCKETS - 1)
    return np.where(n < exact, n, large).astype(np.int32)


def _bias_tile_kernel(tab_ref, idx_ref, o_ref, *, head0, shift, scale):
    h = pl.program_id(0) + head0
    idx = idx_ref[0]
    acc = jnp.zeros(idx.shape, F32)
    for b in range(REL_BUCKETS):
        acc = jnp.where(idx == b, tab_ref[b, h], acc)
    if shift:
        acc = acc - tab_ref[REL_BUCKETS - 1, h]
    if scale != 1.0:
        acc = acc * scale
    o_ref[0, 0] = acc


def _bias_tiles(rel_bias, buckets, head0, nheads, shift, scale=1.0):
    R, M, N = buckets.shape
    return pl.pallas_call(
        functools.partial(_bias_tile_kernel, head0=head0, shift=shift, scale=scale),
        grid=(nheads, R),
        in_specs=[pl.BlockSpec(memory_space=pltpu.SMEM),
                  pl.BlockSpec((1, M, N), lambda h, r: (r, 0, 0))],
        out_specs=pl.BlockSpec((1, 1, M, N), lambda h, r: (h, r, 0, 0)),
        out_shape=jax.ShapeDtypeStruct((nheads, R, M, N), F32),
        compiler_params=_cparams("arbitrary", "arbitrary"),
        name="bias_tiles",
    )(rel_bias, jnp.asarray(buckets))


def _banded_buckets(nback):
    i = np.arange(BANDED_Q_BLOCKS * Q_BLOCK)[None, :]
    j = np.arange((nback + BANDED_Q_BLOCKS) * Q_BLOCK)[:, None]
    return _t5_bucket_np(i - j + nback * Q_BLOCK)[None]


def _flash_buckets():
    k = np.arange(FLASH_BLOCK)[:, None]
    q = np.arange(FLASH_BLOCK)[None, :]
    return np.stack([_t5_bucket_np(q - k), _t5_bucket_np(q - k + FLASH_BLOCK)])


def _cmp_buckets():
    i = np.arange(CMP_Q)[None, :, None]
    c2 = np.arange(2 * LANES)[None, None, :]
    r = np.arange(16 * LANES // CMP_Q)[:, None, None]
    dist = i - NSA_CMP_STRIDE * (c2 - LANES - (CMP_Q // NSA_CMP_STRIDE) * r) - (NSA_CMP_LEN - 1)
    return _t5_bucket_np(dist)


def _rms(x, g):
    return x * lax.rsqrt(jnp.mean(x * x, axis=-1, keepdims=True) + NORM_EPS) * g


def _proj_kernel(x_ref, g_ref, w_ref, wt_ref, oa_ref, ob_ref, oc_ref, od_ref, ot_ref):
    h = _rms(x_ref[...], g_ref[...]).astype(BF16)
    oa_ref[...] = jnp.dot(h, w_ref[:, 0:512], preferred_element_type=F32)
    ob_ref[...] = jnp.dot(h, w_ref[:, 512:1024], preferred_element_type=F32).astype(BF16)
    oc_ref[...] = jnp.dot(h, w_ref[:, 1024:1792], preferred_element_type=F32).astype(BF16)
    od_ref[...] = jnp.dot(h, w_ref[:, 1792:2560], preferred_element_type=F32).astype(BF16)
    ot_ref[...] = lax.dot_general(wt_ref[...], h, _NT, preferred_element_type=F32).astype(BF16)


def _in_proj(x, g, w, wt):
    T = x.shape[0]
    tm = 512
    nt = wt.shape[0]
    return pl.pallas_call(
        _proj_kernel,
        grid=(T // tm,),
        in_specs=[pl.BlockSpec((tm, D_MODEL), lambda i: (i, 0)),
                  pl.BlockSpec((1, D_MODEL), lambda i: (0, 0)),
                  pl.BlockSpec((D_MODEL, N_IN_PAD), lambda i: (0, 0)),
                  pl.BlockSpec((nt, D_MODEL), lambda i: (0, 0))],
        out_specs=[pl.BlockSpec((tm, 512), lambda i: (i, 0)),
                   pl.BlockSpec((tm, 512), lambda i: (i, 0)),
                   pl.BlockSpec((tm, 768), lambda i: (i, 0)),
                   pl.BlockSpec((tm, 768), lambda i: (i, 0)),
                   pl.BlockSpec((nt, tm), lambda i: (0, i))],
        out_shape=[jax.ShapeDtypeStruct((T, 512), F32),
                   jax.ShapeDtypeStruct((T, 512), BF16),
                   jax.ShapeDtypeStruct((T, 768), BF16),
                   jax.ShapeDtypeStruct((T, 768), BF16),
                   jax.ShapeDtypeStruct((nt, T), BF16)],
        compiler_params=_cparams("arbitrary"),
        name="in_proj",
    )(x, g, w, wt)


def _lru_kernel(xg_ref, cw_ref, cb_ref, wa_ref, ba_ref, wx_ref, bx_ref, lam_ref, o_ref,
                ext_ref, h_ref, *, Tt):
    t = pl.program_id(1)

    W = MIX_WIDTH

    @pl.when(t == 0)
    def _():
        ext_ref[0:8, :] = jnp.zeros((8, W), F32)
        h_ref[...] = jnp.zeros_like(h_ref)

    x = xg_ref[:, 0:W]
    gate = xg_ref[:, W:2 * W]
    ext_ref[8:Tt + 8, :] = x
    row = lax.broadcasted_iota(I32, (Tt, W), 0)
    xc = cb_ref[...] + x * cw_ref[CONV_WIDTH - 1:CONV_WIDTH, :]
    for s in range(1, CONV_WIDTH):
        xc = xc + ext_ref[8 - s:8 - s + Tt, :] * cw_ref[CONV_WIDTH - 1 - s:CONV_WIDTH - s, :]
    ext_ref[0:8, :] = ext_ref[Tt:Tt + 8, :]

    xcb = xc.astype(BF16)
    r = jax.nn.sigmoid(jnp.dot(xcb, wa_ref[...], preferred_element_type=F32) + ba_ref[...])
    ig = jax.nn.sigmoid(jnp.dot(xcb, wx_ref[...], preferred_element_type=F32) + bx_ref[...])
    z = -lam_ref[...]
    softplus = jnp.maximum(z, 0.0) + jnp.log1p(jnp.exp(-jnp.abs(z)))
    log_a = (-LRU_C) * r * softplus
    a = jnp.exp(log_a)
    mult = jnp.sqrt(1.0 - a * a)
    mult = jnp.where((row + t * Tt) == 0, 1.0, mult)
    b = mult * ig * xc

    d = 1
    while d < Tt:
        keep = row >= d
        b = jnp.where(keep, a * pltpu.roll(b, d, 0) + b, b)
        a = jnp.where(keep, a * pltpu.roll(a, d, 0), a)
        d *= 2
    hs = a * h_ref[7:8, :] + b
    h_ref[...] = hs[Tt - 8:Tt]
    o_ref[...] = (hs * jax.nn.gelu(gate)).astype(o_ref.dtype)


def _rglru(a_xg, conv_w, conv_b, wa_bd, b_a, wx_bd, b_x, lam, B, S):
    Tt = 256
    W = MIX_WIDTH
    vec = pl.BlockSpec((1, W), lambda b, t: (0, 0))
    mat = pl.BlockSpec((W, W), lambda b, t: (0, 0))
    out = pl.pallas_call(
        functools.partial(_lru_kernel, Tt=Tt),
        grid=(B, S // Tt),
        in_specs=[pl.BlockSpec((None, Tt, 2 * W), lambda b, t: (b, t, 0)),
                  pl.BlockSpec((CONV_WIDTH, W), lambda b, t: (0, 0)),
                  vec, mat, vec, mat, vec, vec],
        out_specs=pl.BlockSpec((None, Tt, W), lambda b, t: (b, t, 0)),
        out_shape=jax.ShapeDtypeStruct((B, S, W), BF16),
        scratch_shapes=[pltpu.VMEM((Tt + 8, W), F32), pltpu.VMEM((8, W), F32)],
        compiler_params=_cparams("arbitrary", "arbitrary"),
        name="rglru",
    )(a_xg.reshape(B, S, 2 * W), conv_w, conv_b, wa_bd, b_a, wx_bd, b_x, lam)
    return out.reshape(B * S, W)


def _lane_window(q128, lo, width, roll, scale):
    q = q128 * scale
    if roll:
        q = pltpu.roll(q, LANES // 2, 1)
    lane = lax.broadcasted_iota(I32, q.shape, 1)
    return jnp.where((lane >= lo) & (lane < lo + width), q, 0.0).astype(BF16)


def _pair_halves(even, odd, even_half, odd_half):
    if even_half == 1:
        even = pltpu.roll(even, LANES // 2, 1)
    if odd_half == 0:
        odd = pltpu.roll(odd, LANES // 2, 1)
    lane = lax.broadcasted_iota(I32, even.shape, 1)
    return jnp.where(lane < LANES // 2, even, odd)


def _banded_kernel(*refs, nback, window, q_cfg, has_sink, shared_kv):
    nk = nback + BANDED_Q_BLOCKS
    TQ = BANDED_Q_BLOCKS * Q_BLOCK
    q_ref = refs[0]
    k_refs = refs[1:1 + nk]
    pos = 1 + nk
    if shared_kv:
        v_refs = k_refs
    else:
        v_refs = refs[pos:pos + nk]
        pos += nk
    bias_ref = refs[pos]
    pos += 1
    sink_ref = None
    if has_sink:
        sink_ref = refs[pos]
        pos += 1
    o_ref = refs[pos]

    n = pl.program_id(1)
    KW = nk * Q_BLOCK
    kcat = jnp.concatenate([k_refs[nk - 1 - i][...] for i in range(nk)], axis=0)
    vcat = kcat if shared_kv else jnp.concatenate([v_refs[nk - 1 - i][...] for i in range(nk)], axis=0)
    vt = vcat.astype(F32).T.astype(BF16)
    jj = lax.broadcasted_iota(I32, (KW, TQ), 0)
    ii = lax.broadcasted_iota(I32, (KW, TQ), 1)
    dist = ii - jj + nback * Q_BLOCK
    valid = (dist >= 0) & (dist < window) & (jj >= (nback - BANDED_Q_BLOCKS * n) * Q_BLOCK)
    scale = HEAD_DIM ** -0.5

    def logits(h):
        lo, roll, _ = q_cfg[h]
        g = h // 2
        qx = _lane_window(q_ref[:, LANES * g:LANES * (g + 1)].astype(F32), lo, HEAD_DIM, roll, scale)
        s = lax.dot_general(kcat, qx, _NT, preferred_element_type=F32) + bias_ref[h, 0]
        return jnp.where(valid, s, NEG_INF)

    def softmax(h, s):
        m = jnp.max(s, axis=0, keepdims=True)
        if has_sink:
            sink = sink_ref[0:1, h:h + 1]
            m = jnp.maximum(m, sink)
        p = jnp.exp(s - m)
        den = jnp.sum(p, axis=0, keepdims=True)
        if has_sink:
            den = den + jnp.exp(sink - m)
        return p.astype(BF16), 1.0 / jnp.maximum(den, TINY)

    def values(h, p, inv):
        half = q_cfg[h][2]
        return jnp.dot(vt[HEAD_DIM * half:HEAD_DIM * (half + 1)], p, preferred_element_type=F32) * inv

    nh = len(q_cfg)
    ss = [logits(h) for h in range(nh)]
    pr = [softmax(h, ss[h]) for h in range(nh)]
    outs = [values(h, *pr[h]) for h in range(nh)]
    for g in range(nh // 2):
        pair = jnp.concatenate([outs[2 * g], outs[2 * g + 1]], axis=0)
        o_ref[:, LANES * g:LANES * (g + 1)] = pair.T.astype(o_ref.dtype)


def _banded_attention(qsrc, q_col, ksrc, k_col, vsrc, v_col, bias, sinks, *, B, S, window, q_cfg):
    nback = -(-window // Q_BLOCK)
    nq = BANDED_Q_BLOCKS
    nk = nback + nq
    TQ = nq * Q_BLOCK
    shared_kv = vsrc is None
    has_sink = sinks is not None

    def kv_spec(col, i):
        return pl.BlockSpec((None, Q_BLOCK, LANES), lambda b, n: (b, jnp.maximum(nq * n + nq - 1 - i, 0), col))

    in_specs = [pl.BlockSpec((None, TQ, 2 * LANES), lambda b, n: (b, n, q_col))]
    args = [qsrc]
    in_specs += [kv_spec(k_col, i) for i in range(nk)]
    args += [ksrc] * nk
    if not shared_kv:
        in_specs += [kv_spec(v_col, i) for i in range(nk)]
        args += [vsrc] * nk
    in_specs.append(pl.BlockSpec(bias.shape, lambda b, n: (0, 0, 0, 0)))
    args.append(bias)
    if has_sink:
        in_specs.append(pl.BlockSpec(sinks.shape, lambda b, n: (0, 0)))
        args.append(sinks)
    return pl.pallas_call(
        functools.partial(_banded_kernel, nback=nback, window=window, q_cfg=q_cfg,
                          has_sink=has_sink, shared_kv=shared_kv),
        grid=(B, S // TQ),
        in_specs=in_specs,
        out_specs=pl.BlockSpec((None, TQ, 2 * LANES), lambda b, n: (b, n, 0)),
        out_shape=jax.ShapeDtypeStruct((B, S, 2 * LANES), BF16),
        compiler_params=_cparams("arbitrary", "arbitrary"),
        name="banded_attn",
    )(*args)


def _flash_kernel(*refs, mode, lambda_init):
    TB = FLASH_BLOCK
    if mode == "diff":
        (q_ref, k_ref, vt_ref, bias_ref, lam_ref, g_ref, o_ref,
         qx_ref, m_ref, l_ref, acc_ref) = refs
        sel_ref = None
        maps = [dict(g=h // 2, lo=(h % 2) * 64 + 32 * mm, w=DIFF_DIM, roll=False, kg=h // 2, vrow=HEAD_DIM * h, h=h)
                for h in range(DIFF_HEADS) for mm in range(2)]
        scale = DIFF_DIM ** -0.5
    else:
        (q_ref, k_ref, vt_ref, pen_ref, blk_ref, bias_ref, o_ref,
         qx_ref, m_ref, l_ref, acc_ref) = refs
        sel_ref = pen_ref
        maps = [dict(g=h // 2, lo=0, w=HEAD_DIM, roll=(h % 2 == 1), kg=0, vrow=HEAD_DIM, h=h)
                for h in range(NSA_HEADS)]
        scale = HEAD_DIM ** -0.5
    n = pl.program_id(1)

    for mp, c in enumerate(maps):
        qx_ref[mp, :, 0:LANES] = _lane_window(q_ref[:, LANES * c["g"]:LANES * (c["g"] + 1)].astype(F32),
                                              c["lo"], c["w"], c["roll"], scale * LOG2E)
        if sel_ref is not None:
            qx_ref[mp, :, LANES:2 * LANES] = pen_ref[...]
    m_ref[...] = jnp.full(m_ref.shape, NEG_INF, F32)
    l_ref[...] = jnp.zeros_like(l_ref)
    acc_ref[...] = jnp.zeros_like(acc_ref)

    key = lax.broadcasted_iota(I32, (TB, TB), 0)
    qry = lax.broadcasted_iota(I32, (TB, TB), 1)

    def step(j, kind, TK=TB):
        koff = pl.multiple_of(j * TB, TB)
        valid = None
        if kind == "diag":
            valid = qry >= key

        def logits(mp):
            c = maps[mp]
            kt = k_ref[pl.ds(koff, TK), LANES * c["kg"]:LANES * (c["kg"] + 1)]
            if sel_ref is not None:
                kt = jnp.concatenate([kt, blk_ref[pl.ds(koff, TK), :]], axis=1)
            s = lax.dot_general(kt, qx_ref[mp], _NT, preferred_element_type=F32)
            if kind == "near":
                s = s + bias_ref[c["h"], 1]
            elif kind == "diag":
                s = s + bias_ref[c["h"], 0]
            if valid is not None:
                s = jnp.where(valid, s, NEG_INF)
            return s

        def softmax(mp, s):
            m_prev = m_ref[mp]
            m_new = jnp.maximum(m_prev, jnp.max(s, axis=0, keepdims=True))
            alpha = jnp.exp2(m_prev - m_new)
            p = jnp.exp2(s - m_new)
            l_ref[mp] = alpha * l_ref[mp] + jnp.sum(p, axis=0, keepdims=True)
            m_ref[mp] = m_new
            return p.astype(BF16), alpha

        def values(mp, p, alpha):
            c = maps[mp]
            vt = vt_ref[c["vrow"]:c["vrow"] + HEAD_DIM, pl.ds(koff, TK)]
            acc_ref[mp] = alpha * acc_ref[mp] + jnp.dot(vt, p, preferred_element_type=F32)

        nm = len(maps)
        ss = [logits(mp) for mp in range(nm)]

        def two_pass():
            pa = [softmax(mp, ss[mp]) for mp in range(nm)]
            for mp in range(nm):
                values(mp, *pa[mp])

        if kind == "diag":
            two_pass()
            return

        stabs = [m_ref[mp] for mp in range(nm)]
        ps = [jnp.exp2(ss[mp] - stabs[mp]) for mp in range(nm)]
        m_news = [jnp.maximum(stabs[mp], jnp.max(ss[mp], axis=0, keepdims=True)) for mp in range(nm)]
        growth = m_news[0] - stabs[0]
        for mp in range(1, nm):
            growth = jnp.maximum(growth, m_news[mp] - stabs[mp])
        safe = jnp.max(growth) <= FLASH_MAX_GROWTH
        psums = [jnp.sum(ps[mp], axis=0, keepdims=True) for mp in range(nm)]
        pvs = [jnp.dot(vt_ref[maps[mp]["vrow"]:maps[mp]["vrow"] + HEAD_DIM, pl.ds(koff, TK)], ps[mp].astype(BF16),
                       preferred_element_type=F32) for mp in range(nm)]

        @pl.when(safe)
        def _():
            for mp in range(nm):
                alpha = jnp.exp2(stabs[mp] - m_news[mp])
                l_ref[mp] = (l_ref[mp] + psums[mp]) * alpha
                acc_ref[mp] = (acc_ref[mp] + pvs[mp]) * alpha
                m_ref[mp] = m_news[mp]

        @pl.when(jnp.logical_not(safe))
        def _():
            two_pass()

    step(n, "diag")

    @pl.when(n >= 1)
    def _():
        step(n - 1, "near")

    nfar = jnp.maximum(n - 1, 0)
    FW = FLASH_FAR_BLOCKS

    def far_body(j, carry):
        step(FW * j, "far", FW * TB)
        return carry

    lax.fori_loop(0, nfar // FW, far_body, 0)

    done = (nfar // FW) * FW
    rest = nfar - done
    size = FW // 2
    while size >= 1:
        def tail_step(size=size):
            step(done + rest - rest % (2 * size), "far", size * TB)

        pl.when(rest % (2 * size) >= size)(tail_step)
        size //= 2

    def normalised(mp):
        return acc_ref[mp] * (1.0 / jnp.maximum(l_ref[mp], TINY))

    if mode == "diff":
        lq = lam_ref[...]
        lam = (jnp.exp(jnp.sum(lq[0:1] * lq[1:2], axis=1, keepdims=True))
               - jnp.exp(jnp.sum(lq[2:3] * lq[3:4], axis=1, keepdims=True)) + lambda_init)
        for g in range(DIFF_HEADS // 2):
            res = []
            for h in (2 * g, 2 * g + 1):
                o = normalised(2 * h) - lam * normalised(2 * h + 1)
                ms = jnp.sum(o * o, axis=0, keepdims=True) * (1.0 / (2 * DIFF_DIM))
                res.append(o * lax.rsqrt(ms + DIFF_SUBLN_EPS))
            out = jnp.concatenate(res, axis=0) * g_ref[...] * (1.0 - lambda_init)
            o_ref[:, LANES * g:LANES * (g + 1)] = out.T.astype(o_ref.dtype)
    else:
        for g in range(NSA_HEADS // 2):
            out = jnp.concatenate([normalised(2 * g), normalised(2 * g + 1)], axis=0)
            o_ref[:, LANES * g:LANES * (g + 1)] = out.T.astype(o_ref.dtype)


def _flash_attention(mode, qsrc, q_col, ksrc, k_col, vt, vt_rows, vt_blk, bias, extra, *, B, S, lambda_init=0.0):
    TB = FLASH_BLOCK
    q_spec = pl.BlockSpec((None, TB, 2 * LANES), lambda b, n: (b, n, q_col))
    bias_spec = pl.BlockSpec(bias.shape, lambda b, n: (0, 0, 0, 0))
    vt_spec = pl.BlockSpec((vt_rows, S), lambda b, n: (vt_blk, b))
    if mode == "diff":
        lam4, gsub = extra
        nmaps = 2 * DIFF_HEADS
        in_specs = [q_spec,
                    pl.BlockSpec((None, S, 2 * LANES), lambda b, n: (b, 0, k_col)),
                    vt_spec, bias_spec,
                    pl.BlockSpec(lam4.shape, lambda b, n: (0, 0)),
                    pl.BlockSpec(gsub.shape, lambda b, n: (0, 0))]
        args = [qsrc, ksrc, vt, bias, lam4, gsub]
    else:
        (pen,) = extra
        nmaps = NSA_HEADS
        blk = jnp.asarray(np.arange(S)[:, None] // NSA_SLC_LEN == np.arange(LANES)[None, :], BF16)
        in_specs = [q_spec,
                    pl.BlockSpec((None, S, LANES), lambda b, n: (b, 0, k_col)),
                    vt_spec,
                    pl.BlockSpec((None, TB, LANES), lambda b, n: (b, n, 0)),
                    pl.BlockSpec((S, LANES), lambda b, n: (0, 0)),
                    bias_spec]
        args = [qsrc, ksrc, vt, pen, blk, bias]
    return pl.pallas_call(
        functools.partial(_flash_kernel, mode=mode, lambda_init=lambda_init),
        grid=(B, S // TB),
        in_specs=in_specs,
        out_specs=pl.BlockSpec((None, TB, 2 * LANES), lambda b, n: (b, n, 0)),
        out_shape=jax.ShapeDtypeStruct((B, S, 2 * LANES), BF16),
        scratch_shapes=[pltpu.VMEM((nmaps, TB, LANES if mode == "diff" else 2 * LANES), BF16),
                        pltpu.VMEM((nmaps, 1, TB), F32),
                        pltpu.VMEM((nmaps, 1, TB), F32),
                        pltpu.VMEM((nmaps, HEAD_DIM, TB), F32)],
        compiler_params=_cparams("arbitrary", "arbitrary"),
        name="flash_" + mode,
    )(*args)


def _compress_kernel(ck_ref, cv_ref, pk_ref, pv_ref, w1k_ref, w1v_ref, w2_ref, o_ref):
    half = NSA_CMP_STRIDE * HEAD_DIM
    ck = ck_ref[...]
    cv = cv_ref[...]
    nrow = ck.shape[0]

    def mm(a, w):
        return jnp.dot(a, w, preferred_element_type=F32)

    top = mm(ck, w1k_ref[0:half, :]) + mm(cv, w1v_ref[0:half, :])
    bot = mm(ck, w1k_ref[half:2 * half, :]) + mm(cv, w1v_ref[half:2 * half, :])
    posw = (mm(pk_ref[...], w1k_ref[...]) + mm(pv_ref[...], w1v_ref[...]))[0:1]
    pre = top + pltpu.roll(bot, nrow - 1, 0) + posw
    kcv = mm(jax.nn.gelu(pre).astype(BF16), w2_ref[...])
    o_ref[0:LANES, :] = jnp.zeros((LANES, LANES), o_ref.dtype)
    o_ref[LANES:LANES + nrow, :] = kcv.astype(o_ref.dtype)


def _nsa_compress(ck, cv, pk, pv, w1k, w1v, w2, ncp):
    B, nchunk, width = ck.shape
    assert ncp == LANES + nchunk

    def full(a):
        return pl.BlockSpec(a.shape, lambda b: (0,) * a.ndim)

    chunk = pl.BlockSpec((None, nchunk, width), lambda b: (b, 0, 0))
    return pl.pallas_call(
        _compress_kernel,
        grid=(B,),
        in_specs=[chunk, chunk, full(pk), full(pv), full(w1k), full(w1v), full(w2)],
        out_specs=pl.BlockSpec((None, ncp, LANES), lambda b: (b, 0, 0)),
        out_shape=jax.ShapeDtypeStruct((B, ncp, LANES), BF16),
        compiler_params=_cparams("arbitrary"),
        name="nsa_compress",
    )(ck, cv, pk, pv, w1k, w1v, w2)


def _cmp_kernel(q_ref, kcv_ref, bias_ref, agg_ref, oc_ref, sel_ref, s_scr, *, n_cmp, ncp):
    n = pl.program_id(1)
    QB = CMP_Q
    kcv = kcv_ref[...]
    cp = lax.broadcasted_iota(I32, (QB, ncp), 1)
    qpos = n * QB + lax.broadcasted_iota(I32, (QB, ncp), 0)
    valid = ((cp >= LANES) & (cp < LANES + n_cmp)
             & (qpos - NSA_CMP_STRIDE * (cp - LANES) - (NSA_CMP_LEN - 1) >= 0))
    woff = pl.multiple_of((n // (16 * LANES // CMP_Q)) * LANES, LANES)
    scale = HEAD_DIM ** -0.5

    def logits(h):
        g = h // 2
        qx = _lane_window(q_ref[:, LANES * g:LANES * (g + 1)].astype(F32), 0, HEAD_DIM, h % 2 == 1, scale)
        s_scr[h] = lax.dot_general(qx, kcv, _NT, preferred_element_type=F32)
        s_scr[h, :, pl.ds(woff, 2 * LANES)] += bias_ref[h]

    def softmax(h):
        s = jnp.where(valid, s_scr[h], NEG_INF)
        m = jnp.max(s, axis=1, keepdims=True)
        p = jnp.where(valid, jnp.exp(s - m), 0.0)
        den = jnp.sum(p, axis=1, keepdims=True)
        return p * (1.0 / jnp.maximum(den, TINY))

    for h in range(NSA_HEADS):
        logits(h)
    ps = [softmax(h) for h in range(NSA_HEADS)]
    outs = [jnp.dot(p.astype(BF16), kcv, preferred_element_type=F32) for p in ps]
    for g in range(NSA_HEADS // 2):
        oc_ref[:, LANES * g:LANES * (g + 1)] = _pair_halves(outs[2 * g], outs[2 * g + 1], 1, 1).astype(oc_ref.dtype)

    psum = (ps[0] + ps[1]) + (ps[2] + ps[3])
    imp = jnp.dot(psum.astype(BF16), agg_ref[...], preferred_element_type=F32)

    v = imp.T
    jb = lax.broadcasted_iota(I32, v.shape, 0)
    qp = n * QB + lax.broadcasted_iota(I32, v.shape, 1)
    cur = qp >> 6
    forced = (jb == 0) | (jb == cur) | (jb == cur - 1)
    v = jnp.where(forced, -jnp.inf, jnp.where(jb * NSA_SLC_LEN <= qp, v, -1.0))
    rid = jb.astype(F32)
    picked = jnp.where(forced, 1.0, 0.0)
    for _ in range(NSA_TOPK - 3):
        mx = jnp.max(v, axis=0, keepdims=True)
        first = jnp.min(jnp.where(v == mx, rid, float(LANES)), axis=0, keepdims=True)
        hit = rid == first
        picked = jnp.where(hit, 1.0, picked)
        v = jnp.where(hit, -jnp.inf, v)
    sel_ref[...] = jnp.where(picked.T > 0.5, 0.0, NEG_INF).astype(sel_ref.dtype)


def _nsa_cmp_select(dsrc, kcv, bias, agg, *, B, S, n_cmp, ncp):
    return pl.pallas_call(
        functools.partial(_cmp_kernel, n_cmp=n_cmp, ncp=ncp),
        grid=(B, S // CMP_Q),
        in_specs=[pl.BlockSpec((None, CMP_Q, 2 * LANES), lambda b, n: (b, n, 0)),
                  pl.BlockSpec((None, ncp, LANES), lambda b, n: (b, 0, 0)),
                  pl.BlockSpec((NSA_HEADS, None, CMP_Q, 2 * LANES), lambda b, n: (0, n % (16 * LANES // CMP_Q), 0, 0)),
                  pl.BlockSpec((ncp, LANES), lambda b, n: (0, 0))],
        out_specs=[pl.BlockSpec((None, CMP_Q, 2 * LANES), lambda b, n: (b, n, 0)),
                   pl.BlockSpec((None, CMP_Q, LANES), lambda b, n: (b, n, 0))],
        out_shape=[jax.ShapeDtypeStruct((B, S, 2 * LANES), BF16),
                   jax.ShapeDtypeStruct((B, S, LANES), BF16)],
        scratch_shapes=[pltpu.VMEM((NSA_HEADS, CMP_Q, ncp), F32)],
        compiler_params=_cparams("arbitrary", "arbitrary"),
        name="nsa_cmp_select",
    )(dsrc, kcv, bias, agg)


def _merge_kernel(x_ref, g_ref, ya_ref, yb_ref, yc_ref, oc_ref, os_ref, ow_ref, dg_ref,
                  wg_ref, bg_ref, wb_ref, wo_ref, o_ref):
    x = x_ref[...]
    h = _rms(x, g_ref[...]).astype(BF16)
    tm = x.shape[0]
    lane = lax.broadcasted_iota(I32, (tm, LANES), 1)
    gd = jax.nn.sigmoid(dg_ref[...].astype(F32))
    yd = []
    for g in range(NSA_HEADS // 2):
        acc = jnp.zeros((tm, LANES), F32)
        for br, src in enumerate((oc_ref, os_ref, ow_ref)):
            ge = gd[:, 3 * (2 * g) + br:3 * (2 * g) + br + 1]
            go = gd[:, 3 * (2 * g + 1) + br:3 * (2 * g + 1) + br + 1]
            acc = acc + jnp.where(lane < 64, ge, go) * src[:, LANES * g:LANES * (g + 1)].astype(F32)
        yd.append(acc)
    y_d = jnp.concatenate(yd, axis=1).astype(BF16)
    ys = (ya_ref[...], yb_ref[...], yc_ref[...], y_d)
    merged = jnp.zeros((tm, D_MODEL), F32)
    for br in range(4):
        gate = jax.nn.sigmoid(jnp.dot(h, wg_ref[br], preferred_element_type=F32) + bg_ref[br:br + 1, :])
        merged = merged + gate * jnp.dot(ys[br], wb_ref[br], preferred_element_type=F32)
    o_ref[...] = x + jnp.dot(merged.astype(BF16), wo_ref[...], preferred_element_type=F32)


def _merge(x, g, ya, yb, yc, oc, osel, ow, dproj, wg, bg, wb, wo):
    T = x.shape[0]
    tm = 256

    def tile(width, col=0):
        return pl.BlockSpec((tm, width), lambda i: (i, col))

    def full(a):
        return pl.BlockSpec(a.shape, lambda i: (0,) * a.ndim)

    return pl.pallas_call(
        _merge_kernel,
        grid=(T // tm,),
        in_specs=[tile(D_MODEL), full(g), tile(256), tile(256), tile(256), tile(256), tile(256), tile(256),
                  tile(LANES, 5), full(wg), full(bg), full(wb), full(wo)],
        out_specs=tile(D_MODEL),
        out_shape=jax.ShapeDtypeStruct((T, D_MODEL), F32),
        compiler_params=_cparams("arbitrary"),
        name="merge",
    )(x, g, ya, yb, yc, oc, osel, ow, dproj, wg, bg, wb, wo)


def _route(hn, w_split, b_router):
    lane = lax.broadcasted_iota(I32, (hn.shape[0], LANES), 1)
    hi = hn.astype(BF16)
    lo = (hn - hi.astype(F32)).astype(BF16)
    a = jnp.dot(hi, w_split, preferred_element_type=F32)
    b = jnp.dot(lo, w_split[:, 0:LANES], preferred_element_type=F32)
    logits = (a[:, 0:LANES] + a[:, LANES:2 * LANES]) + b + b_router
    is_grp = (lane >= N_EXPERTS) & (lane < N_EXPERTS + N_GROUPS)
    lanef = lane.astype(F32)
    lg = jnp.where(is_grp, logits, -jnp.inf)
    mg = jnp.max(lg, axis=1, keepdims=True)
    gsel = jnp.min(jnp.where(lg == mg, lanef, 2.0 * LANES), axis=1, keepdims=True) - N_EXPERTS
    g_w = 1.0 / jnp.sum(jnp.where(is_grp, jnp.exp(lg - mg), 0.0), axis=1, keepdims=True)
    in_grp = (lane < N_EXPERTS) & ((lane >> 3).astype(F32) == gsel)
    v1 = jnp.where(in_grp, logits, -jnp.inf)
    m1 = jnp.max(v1, axis=1, keepdims=True)
    i1 = jnp.min(jnp.where(v1 == m1, lanef, 2.0 * LANES), axis=1, keepdims=True)
    v2 = jnp.where(lanef == i1, -jnp.inf, v1)
    m2 = jnp.max(v2, axis=1, keepdims=True)
    i2 = jnp.min(jnp.where(v2 == m2, lanef, 2.0 * LANES), axis=1, keepdims=True)
    e2 = jnp.exp(m2 - m1)
    w1 = g_w / (1.0 + e2)
    w2 = g_w * e2 / (1.0 + e2)
    first, second = lanef == i1, lanef == i2
    comb = jnp.where(first, w1, 0.0) + jnp.where(second, w2, 0.0)
    return comb, jnp.where(first | second, 1.0, 0.0)


def _moe_kernel(x_ref, g_ref, wr_ref, br_ref, wg_ref, wu_ref, wd_ref, fg_ref, o_ref,
                hn_ref, comb_ref, acc_ref, *, final_norm):
    e = pl.program_id(1)
    tm = x_ref.shape[0]
    lane = lax.broadcasted_iota(I32, (tm, LANES), 1)

    @pl.when(e == 0)
    def _():
        hn = _rms(x_ref[...], g_ref[...])
        hn_ref[...] = hn.astype(BF16)
        comb_ref[...] = _route(hn, wr_ref[...], br_ref[...])[0]
        acc_ref[...] = jnp.zeros_like(acc_ref)

    hn = hn_ref[...]
    gt = jnp.dot(hn, wg_ref[...], preferred_element_type=F32)
    up = jnp.dot(hn, wu_ref[...], preferred_element_type=F32)
    he = (gt * jax.nn.sigmoid(gt) * up).astype(BF16)
    ye = jnp.dot(he, wd_ref[...], preferred_element_type=F32)
    cw = jnp.sum(jnp.where(lane == e, comb_ref[...], 0.0), axis=1, keepdims=True)
    acc_ref[...] += cw * ye

    @pl.when(e == N_EXPERTS - 1)
    def _():
        y = x_ref[...] + acc_ref[...]
        if final_norm:
            y = _rms(y, fg_ref[...])
        o_ref[...] = y


def _moe(x, g, w_router, b_router, wg, wu, wd, final_g, final_norm):
    T = x.shape[0]
    tm = 1024
    return pl.pallas_call(
        functools.partial(_moe_kernel, final_norm=final_norm),
        grid=(T // tm, N_EXPERTS),
        in_specs=[pl.BlockSpec((tm, D_MODEL), lambda i, e: (i, 0)),
                  pl.BlockSpec((1, D_MODEL), lambda i, e: (0, 0)),
                  pl.BlockSpec((D_MODEL, 2 * LANES), lambda i, e: (0, 0)),
                  pl.BlockSpec((1, LANES), lambda i, e: (0, 0)),
                  pl.BlockSpec((None, D_MODEL, EXPERT_FF), lambda i, e: (e, 0, 0)),
                  pl.BlockSpec((None, D_MODEL, EXPERT_FF), lambda i, e: (e, 0, 0)),
                  pl.BlockSpec((None, EXPERT_FF, D_MODEL), lambda i, e: (e, 0, 0)),
                  pl.BlockSpec((1, D_MODEL), lambda i, e: (0, 0))],
        out_specs=pl.BlockSpec((tm, D_MODEL), lambda i, e: (i, 0)),
        out_shape=jax.ShapeDtypeStruct((T, D_MODEL), F32),
        scratch_shapes=[pltpu.VMEM((tm, D_MODEL), BF16),
                        pltpu.VMEM((tm, LANES), F32),
                        pltpu.VMEM((tm, D_MODEL), F32)],
        compiler_params=_cparams("arbitrary", "arbitrary"),
        name="moe",
    )(x, g, w_router, b_router, wg, wu, wd, final_g)


def _slot_matrix(pos_t, val_t, ntok, half):
    nslot = MOE_CAP // 2
    slot = (lax.broadcasted_iota(I32, (nslot, ntok), 0) + half * nslot).astype(F32)
    blocks = []
    for e in range(N_EXPERTS):
        hit = pos_t[e:e + 1, :] == slot
        blocks.append(jnp.where(hit, 1.0 if val_t is None else val_t[e:e + 1, :], 0.0).astype(BF16))
    return jnp.concatenate(blocks, axis=0)


ROUTE_ROWS = 48


def _route_t(hn, w_split_t, b_col):
    nr = ROUTE_ROWS
    hi = hn.astype(BF16)
    lo = (hn - hi.astype(F32)).astype(BF16)
    a = lax.dot_general(w_split_t, hi, _NT, preferred_element_type=F32)
    b = lax.dot_general(w_split_t[0:LANES], lo, _NT, preferred_element_type=F32)
    logits = (a[0:nr] + a[LANES:LANES + nr]) + b[0:nr] + b_col[0:nr]
    row = lax.broadcasted_iota(I32, logits.shape, 0)
    rowf = row.astype(F32)
    is_grp = (row >= N_EXPERTS) & (row < N_EXPERTS + N_GROUPS)
    lg = jnp.where(is_grp, logits, -jnp.inf)
    mg = jnp.max(lg, axis=0, keepdims=True)
    gsel = jnp.min(jnp.where(lg == mg, rowf, 2.0 * LANES), axis=0, keepdims=True) - N_EXPERTS
    g_w = 1.0 / jnp.sum(jnp.where(is_grp, jnp.exp(lg - mg), 0.0), axis=0, keepdims=True)
    in_grp = (row < N_EXPERTS) & ((row >> 3).astype(F32) == gsel)
    v1 = jnp.where(in_grp, logits, -jnp.inf)
    m1 = jnp.max(v1, axis=0, keepdims=True)
    i1 = jnp.min(jnp.where(v1 == m1, rowf, 2.0 * LANES), axis=0, keepdims=True)
    v2 = jnp.where(rowf == i1, -jnp.inf, v1)
    m2 = jnp.max(v2, axis=0, keepdims=True)
    i2 = jnp.min(jnp.where(v2 == m2, rowf, 2.0 * LANES), axis=0, keepdims=True)
    e2 = jnp.exp(m2 - m1)
    w1 = g_w / (1.0 + e2)
    w2 = g_w * e2 / (1.0 + e2)
    first, second = rowf == i1, rowf == i2
    comb = jnp.where(first, w1, 0.0) + jnp.where(second, w2, 0.0)
    return comb, jnp.where(first | second, 1.0, 0.0)


def _dispatch_kernel(x_ref, g_ref, wr_ref, br_ref, tri_ref, xd_ref, pos_ref, wt_ref, cnt_ref):
    tm = x_ref.shape[0]
    nr = ROUTE_ROWS
    hn = _rms(x_ref[...], g_ref[...])
    comb_t, routed_t = _route_t(hn, wr_ref[...], br_ref[...])
    rank_t = jnp.dot(routed_t.astype(BF16), tri_ref[...], preferred_element_type=F32)
    pos_t = jnp.where(routed_t > 0.0, rank_t, -1.0)
    pos_ref[0:nr, :] = pos_t
    pos_ref[nr:LANES, :] = jnp.full((LANES - nr, tm), -1.0, F32)
    wt_ref[0:nr, :] = comb_t
    wt_ref[nr:LANES, :] = jnp.zeros((LANES - nr, tm), F32)
    counts = jnp.sum(routed_t, axis=1, keepdims=True)
    cnt_ref[0:nr, :] = jnp.broadcast_to(counts, (nr, LANES))
    cnt_ref[nr:LANES, :] = jnp.zeros((LANES - nr, LANES), F32)
    hnb = hn.astype(BF16)
    nslot = MOE_CAP // 2
    lower = jnp.dot(_slot_matrix(pos_t, None, tm, 0), hnb, preferred_element_type=F32)
    xd_ref[:, 0:nslot, :] = lower.reshape(N_EXPERTS, nslot, D_MODEL).astype(xd_ref.dtype)

    busy = jnp.max(counts) > nslot

    @pl.when(busy)
    def _():
        upper = jnp.dot(_slot_matrix(pos_t, None, tm, 1), hnb, preferred_element_type=F32)
        xd_ref[:, nslot:MOE_CAP, :] = upper.reshape(N_EXPERTS, nslot, D_MODEL).astype(xd_ref.dtype)

    @pl.when(jnp.logical_not(busy))
    def _():
        xd_ref[:, nslot:MOE_CAP, :] = jnp.zeros((N_EXPERTS, nslot, D_MODEL), xd_ref.dtype)


def _expert_kernel(busy_ref, xlo_ref, xhi_ref, wg_ref, wu_ref, wd_ref, olo_ref, ohi_ref):
    ntile, nslot, _ = xlo_ref.shape

    def ffn(x_ref, o_ref):
        x = x_ref[...].reshape(ntile * nslot, D_MODEL)
        gt = jnp.dot(x, wg_ref[...], preferred_element_type=F32)
        up = jnp.dot(x, wu_ref[...], preferred_element_type=F32)
        he = (gt * jax.nn.sigmoid(gt) * up).astype(BF16)
        ye = jnp.dot(he, wd_ref[...], preferred_element_type=F32)
        o_ref[...] = ye.reshape(ntile, nslot, D_MODEL).astype(o_ref.dtype)

    ffn(xlo_ref, olo_ref)
    busy = busy_ref[pl.program_id(0), pl.program_id(1)] > 0

    @pl.when(busy)
    def _():
        ffn(xhi_ref, ohi_ref)

    @pl.when(jnp.logical_not(busy))
    def _():
        ohi_ref[...] = jnp.zeros(ohi_ref.shape, ohi_ref.dtype)


def _combine_kernel(busy_ref, x_ref, ylo_ref, yhi_ref, pos_ref, wt_ref, fg_ref, o_ref, *, final_norm):
    tm = x_ref.shape[0]
    nslot = MOE_CAP // 2

    def gathered(y_ref, half):
        weights = _slot_matrix(pos_ref[...], wt_ref[...], tm, half)
        yd = y_ref[...].reshape(N_EXPERTS * nslot, D_MODEL)
        return lax.dot_general(weights, yd, (((0,), (0,)), ((), ())), preferred_element_type=F32)

    o_ref[...] = x_ref[...] + gathered(ylo_ref, 0)

    @pl.when(busy_ref[pl.program_id(0)] > 0)
    def _():
        o_ref[...] += gathered(yhi_ref, 1)

    if final_norm:
        o_ref[...] = _rms(o_ref[...], fg_ref[...])


def _moe_dispatch(x, g, w_router, b_router):
    T = x.shape[0]
    tm = MOE_TILE
    nt = T // tm
    tri = jnp.asarray(np.triu(np.ones((tm, tm), np.float32), 1), BF16)
    xd, pos_t, w_t, cnt = pl.pallas_call(
        _dispatch_kernel,
        grid=(nt,),
        in_specs=[pl.BlockSpec((tm, D_MODEL), lambda i: (i, 0)),
                  pl.BlockSpec((1, D_MODEL), lambda i: (0, 0)),
                  pl.BlockSpec((2 * LANES, D_MODEL), lambda i: (0, 0)),
                  pl.BlockSpec((LANES, 1), lambda i: (0, 0)),
                  pl.BlockSpec((tm, tm), lambda i: (0, 0))],
        out_specs=[pl.BlockSpec((N_EXPERTS, None, MOE_CAP, D_MODEL), lambda i: (0, i, 0, 0)),
                   pl.BlockSpec((LANES, tm), lambda i: (0, i)),
                   pl.BlockSpec((LANES, tm), lambda i: (0, i)),
                   pl.BlockSpec((None, LANES, LANES), lambda i: (i, 0, 0))],
        out_shape=[jax.ShapeDtypeStruct((N_EXPERTS, nt, MOE_CAP, D_MODEL), BF16),
                   jax.ShapeDtypeStruct((LANES, T), F32),
                   jax.ShapeDtypeStruct((LANES, T), F32),
                   jax.ShapeDtypeStruct((nt, LANES, LANES), F32)],
        compiler_params=_cparams("arbitrary"),
        name="moe_dispatch",
    )(x, g, w_router.T, b_router.reshape(LANES, 1), tri)
    return xd, pos_t, w_t, cnt[:, 0:N_EXPERTS, 0]


def _moe_finish(x, xd, pos_t, w_t, counts, wg, wu, wd, final_g, final_norm):
    T = x.shape[0]
    tm = MOE_TILE
    nt = T // tm
    tps = min(MOE_TILES_PER_STEP, nt)
    assert nt % tps == 0
    nslot = MOE_CAP // 2
    hot = counts > nslot
    busy_blocks = jnp.any(hot.reshape(nt // tps, tps, N_EXPERTS), axis=1).T.astype(I32)
    busy_tiles = jnp.any(hot, axis=1).astype(I32)
    halves = xd.reshape(N_EXPERTS, nt, 2, nslot, D_MODEL)
    half_out = jax.ShapeDtypeStruct((N_EXPERTS, nt, nslot, D_MODEL), BF16)
    yd_lo, yd_hi = pl.pallas_call(
        _expert_kernel,
        grid_spec=pltpu.PrefetchScalarGridSpec(
            num_scalar_prefetch=1,
            grid=(N_EXPERTS, nt // tps),
            in_specs=[pl.BlockSpec((None, tps, None, nslot, D_MODEL), lambda e, c, b: (e, c, 0, 0, 0)),
                      pl.BlockSpec((None, tps, None, nslot, D_MODEL),
                                   lambda e, c, b: (e, jnp.where(b[e, c] > 0, c, 0), 1, 0, 0)),
                      pl.BlockSpec((None, D_MODEL, EXPERT_FF), lambda e, c, b: (e, 0, 0)),
                      pl.BlockSpec((None, D_MODEL, EXPERT_FF), lambda e, c, b: (e, 0, 0)),
                      pl.BlockSpec((None, EXPERT_FF, D_MODEL), lambda e, c, b: (e, 0, 0))],
            out_specs=[pl.BlockSpec((None, tps, nslot, D_MODEL), lambda e, c, b: (e, c, 0, 0)),
                       pl.BlockSpec((None, tps, nslot, D_MODEL), lambda e, c, b: (e, c, 0, 0))]),
        out_shape=[half_out, half_out],
        compiler_params=_cparams("arbitrary", "arbitrary"),
        name="moe_experts",
    )(busy_blocks, halves, halves, wg, wu, wd)
    return pl.pallas_call(
        functools.partial(_combine_kernel, final_norm=final_norm),
        grid_spec=pltpu.PrefetchScalarGridSpec(
            num_scalar_prefetch=1,
            grid=(nt,),
            in_specs=[pl.BlockSpec((tm, D_MODEL), lambda i, b: (i, 0)),
                      pl.BlockSpec((N_EXPERTS, None, nslot, D_MODEL), lambda i, b: (0, i, 0, 0)),
                      pl.BlockSpec((N_EXPERTS, None, nslot, D_MODEL),
                                   lambda i, b: (0, jnp.where(b[i] > 0, i, 0), 0, 0)),
                      pl.BlockSpec((LANES, tm), lambda i, b: (0, i)),
                      pl.BlockSpec((LANES, tm), lambda i, b: (0, i)),
                      pl.BlockSpec((1, D_MODEL), lambda i, b: (0, 0))],
            out_specs=pl.BlockSpec((tm, D_MODEL), lambda i, b: (i, 0))),
        out_shape=jax.ShapeDtypeStruct((T, D_MODEL), F32),
        compiler_params=_cparams("arbitrary"),
        name="moe_combine",
    )(busy_tiles, x, yd_lo, yd_hi, pos_t, w_t, final_g)


def _block_diag(w):
    nb, bs, _ = w.shape
    out = jnp.zeros((nb * bs, nb * bs), w.dtype)
    for i in range(nb):
        out = out.at[i * bs:(i + 1) * bs, i * bs:(i + 1) * bs].set(w[i])
    return out


def _selection_aggregator_np(n_cmp, n_slc):
    ratio_s = NSA_SLC_LEN // NSA_CMP_STRIDE
    ratio_c = NSA_CMP_LEN // NSA_CMP_STRIDE
    w = np.convolve(np.ones(ratio_s), np.ones(ratio_c)).astype(np.float32)
    cj = np.arange(n_slc)[:, None] * ratio_s - (ratio_c - 1) + np.arange(w.size)[None, :]
    jj = np.broadcast_to(np.arange(n_slc)[:, None], cj.shape)
    ww = np.broadcast_to(w[None, :], cj.shape)
    keep = (cj >= 0) & (cj < n_cmp)
    agg = np.zeros((n_cmp, n_slc), np.float32)
    np.add.at(agg, (cj[keep], jj[keep]), ww[keep])
    return agg


def _row(v):
    return v.reshape(1, -1).astype(F32)


def _diff_lambda_init(layer):
    return 0.8 - 0.6 * math.exp(-0.3 * layer)


def _mixer_layer(x, l, B, S, tiles, p):
    (bias_swa, bias_diff, bias_slc, bias_win, bias_cmp, agg_pad, n_cmp, ncp) = tiles
    w_in = jnp.pad(p["w_in"][l], ((0, 0), (0, N_IN_PAD - p["w_in"].shape[2]))).astype(BF16)
    w_t = jnp.concatenate([w_in[:, 1536:1792], w_in[:, 2176:2304]], axis=1).T
    a_xg, b_qkv, c_qkv, d_all, vals_t = _in_proj(x, _row(p["mix_norm_g"][l]), w_in, w_t)

    y_a = _rglru(a_xg, p["conv_w"][l].astype(F32), _row(p["conv_b"][l]),
                 _block_diag(p["lru_w_a"][l]).astype(BF16), _row(p["lru_b_a"][l]),
                 _block_diag(p["lru_w_x"][l]).astype(BF16), _row(p["lru_b_x"][l]),
                 _row(p["lru_lambda"][l]), B, S)

    b3 = b_qkv.reshape(B, S, 512)
    swa_cfg = tuple(((h // 2) * 64, (h % 2) != (h // 2), h // 2) for h in range(SWA_HEADS))
    y_b = _banded_attention(b3, 0, b3, 2, b3, 3, bias_swa, _row(p["swa_sinks"][l]),
                            B=B, S=S, window=SWA_WINDOW, q_cfg=swa_cfg)

    c3 = c_qkv.reshape(B, S, 768)
    lam4 = jnp.stack([p["diff_lq1"][l], p["diff_lk1"][l], p["diff_lq2"][l], p["diff_lk2"][l]]).astype(F32)
    gsub = jnp.tile(p["diff_subln_g"][l].reshape(-1, 1).astype(F32), (2, 1))
    y_c = _flash_attention("diff", c3, 0, c3, 1, vals_t, 256, 0, bias_diff, (lam4, gsub), B=B, S=S,
                           lambda_init=_diff_lambda_init(l))

    d3 = d_all.reshape(B, S, 768)
    nchunk = S // NSA_CMP_STRIDE
    ck = d3[:, :, 256:320].reshape(B, nchunk, NSA_CMP_STRIDE * HEAD_DIM)
    cv = d3[:, :, 320:384].reshape(B, nchunk, NSA_CMP_STRIDE * HEAD_DIM)

    def pos8(a):
        return jnp.broadcast_to(a.reshape(1, -1), (8, a.size)).astype(BF16)

    w1k = jnp.pad(p["nsa_w1_k"][l], ((0, 0), (0, HEAD_DIM))).astype(BF16)
    w1v = jnp.pad(p["nsa_w1_v"][l], ((0, 0), (HEAD_DIM, 0))).astype(BF16)
    w2 = _block_diag(jnp.stack([p["nsa_w2_k"][l], p["nsa_w2_v"][l]])).astype(BF16)
    kcv = _nsa_compress(ck, cv, pos8(p["nsa_pos_k"][l]), pos8(p["nsa_pos_v"][l]), w1k, w1v, w2, ncp)
    o_c, sel = _nsa_cmp_select(d3, kcv, bias_cmp, agg_pad, B=B, S=S, n_cmp=n_cmp, ncp=ncp)
    o_s = _flash_attention("nsa", d3, 0, d3, 3, vals_t, 128, 2, bias_slc, (sel,), B=B, S=S)
    win_cfg = tuple((0, h % 2 == 1, 1) for h in range(NSA_HEADS))
    o_w = _banded_attention(d3, 0, d3, 4, None, None, bias_win, None, B=B, S=S, window=NSA_WINDOW, q_cfg=win_cfg)

    T = B * S
    return _merge(x, _row(p["mix_norm_g"][l]), y_a, y_b.reshape(T, 256), y_c.reshape(T, 256),
                  o_c.reshape(T, 256), o_s.reshape(T, 256), o_w.reshape(T, 256), d_all,
                  p["w_gate"][l].astype(BF16), p["b_gate"][l].astype(F32), p["w_branch"][l].astype(BF16),
                  p["w_out"][l].astype(BF16))


def _moe_layer(x, l, p, final_norm):
    w_router = jnp.concatenate([p["w_router_exp"][l], p["w_router_grp"][l]], axis=1)
    w_router = jnp.pad(w_router, ((0, 0), (0, LANES - w_router.shape[1]))).astype(F32)
    w_hi = w_router.astype(BF16)
    w_router = jnp.concatenate([w_hi, (w_router - w_hi.astype(F32)).astype(BF16)], axis=1)
    b_router = jnp.concatenate([p["b_router_exp"][l], p["b_router_grp"][l]])
    b_router = jnp.pad(b_router, (0, LANES - b_router.shape[0])).reshape(1, LANES).astype(F32)
    g = _row(p["ffn_norm_g"][l])
    fg = _row(p["final_norm_g"])
    wg, wu, wd = (p[k][l].astype(BF16) for k in ("w_exp_gate", "w_exp_up", "w_exp_down"))
    xd, pos_t, w_t, counts = _moe_dispatch(x, g, w_router, b_router)
    return lax.cond(jnp.max(counts) > MOE_CAP,
                    lambda: _moe(x, g, w_router, b_router, wg, wu, wd, fg, final_norm),
                    lambda: _moe_finish(x, xd, pos_t, w_t, counts, wg, wu, wd, fg, final_norm))


def _forward(p):
    x = p["x"]
    B, S, D = x.shape
    depth = p["w_in"].shape[0]
    assert D == D_MODEL and S % (16 * Q_BLOCK) == 0 and S // NSA_SLC_LEN <= LANES
    rel = p["rel_bias"].astype(F32)
    n_cmp = (S - NSA_CMP_LEN) // NSA_CMP_STRIDE + 1
    ncp = LANES + S // NSA_CMP_STRIDE
    agg = _selection_aggregator_np(n_cmp, S // NSA_SLC_LEN)
    agg_pad = np.zeros((ncp, LANES), np.float32)
    agg_pad[LANES:LANES + n_cmp, :agg.shape[1]] = agg
    tiles = (
        _bias_tiles(rel, _banded_buckets(1), 0, SWA_HEADS, False),
        _bias_tiles(rel, _flash_buckets(), SWA_HEADS, DIFF_HEADS, True, LOG2E),
        _bias_tiles(rel, _flash_buckets(), SWA_HEADS + DIFF_HEADS, NSA_HEADS, True, LOG2E),
        _bias_tiles(rel, _banded_buckets(NSA_WINDOW // Q_BLOCK), SWA_HEADS + DIFF_HEADS, NSA_HEADS, False),
        _bias_tiles(rel, _cmp_buckets(), SWA_HEADS + DIFF_HEADS, NSA_HEADS, True),
        jnp.asarray(agg_pad, BF16), n_cmp, ncp,
    )
    xt = x.reshape(B * S, D).astype(F32)
    for l in range(depth):
        xt = _mixer_layer(xt, l, B, S, tiles, p)
        xt = _moe_layer(xt, l, p, final_norm=(l == depth - 1))
    return xt.reshape(B, S, D).astype(x.dtype)


def kernel(x, rel_bias, final_norm_g, mix_norm_g, w_in, conv_w, conv_b, lru_w_a, lru_b_a, lru_w_x, lru_b_x,
           lru_lambda, swa_sinks, diff_lq1, diff_lk1, diff_lq2, diff_lk2, diff_subln_g, nsa_pos_k, nsa_w1_k,
           nsa_w2_k, nsa_pos_v, nsa_w1_v, nsa_w2_v, w_gate, b_gate, w_branch, w_out, ffn_norm_g, w_router_grp,
           b_router_grp, w_router_exp, b_router_exp, w_exp_gate, w_exp_up, w_exp_down):
    return _forward(dict(
        x=x, rel_bias=rel_bias, final_norm_g=final_norm_g, mix_norm_g=mix_norm_g, w_in=w_in, conv_w=conv_w,
        conv_b=conv_b, lru_w_a=lru_w_a, lru_b_a=lru_b_a, lru_w_x=lru_w_x, lru_b_x=lru_b_x, lru_lambda=lru_lambda,
        swa_sinks=swa_sinks, diff_lq1=diff_lq1, diff_lk1=diff_lk1, diff_lq2=diff_lq2, diff_lk2=diff_lk2,
        diff_subln_g=diff_subln_g, nsa_pos_k=nsa_pos_k, nsa_w1_k=nsa_w1_k, nsa_w2_k=nsa_w2_k, nsa_pos_v=nsa_pos_v,
        nsa_w1_v=nsa_w1_v, nsa_w2_v=nsa_w2_v, w_gate=w_gate, b_gate=b_gate, w_branch=w_branch, w_out=w_out,
        ffn_norm_g=ffn_norm_g, w_router_grp=w_router_grp, b_router_grp=b_router_grp, w_router_exp=w_router_exp,
        b_router_exp=b_router_exp, w_exp_gate=w_exp_gate, w_exp_up=w_exp_up, w_exp_down=w_exp_down))
```

```python
import functools
import math

import numpy as np
import jax
import jax.numpy as jnp
from jax import lax
from jax.experimental import pallas as pl
from jax.experimental.pallas import tpu as pltpu

F32 = jnp.float32
BF16 = jnp.bfloat16
I32 = jnp.int32

D_MODEL = 1024
HEAD_DIM = 64
MIX_WIDTH = 256
LRU_C = 8.0
CONV_WIDTH = 4
SWA_HEADS = 4
SWA_WINDOW = 128
DIFF_HEADS = 4
DIFF_DIM = 32
DIFF_SUBLN_EPS = 1e-5
NSA_HEADS = 4
NSA_CMP_LEN = 32
NSA_CMP_STRIDE = 16
NSA_SLC_LEN = 64
NSA_TOPK = 16
NSA_WINDOW = 512
NSA_FORCED_SCORE = 1e4
REL_BUCKETS = 32
REL_MAX_DIST = 128
N_GROUPS = 4
EXPERTS_PER_GROUP = 8
N_EXPERTS = 32
EXPERT_FF = 256
NORM_EPS = 1e-6
NEG_INF = -1e30
TINY = 1e-30
LOG2E = math.log2(math.e)

LANES = 128
Q_BLOCK = 128
CMP_Q = 512
BANDED_Q_BLOCKS = 4
FLASH_BLOCK = 256
MOE_TILE = 512
MOE_CAP = 128
MOE_TILES_PER_STEP = 8
FLASH_MAX_GROWTH = 64.0
FLASH_FAR_BLOCKS = 8
N_IN_PAD = 2560
VMEM_LIMIT = 56 * 1024 * 1024

_NT = (((1,), (1,)), ((), ()))


def _cparams(*sem):
    return pltpu.CompilerParams(dimension_semantics=sem, vmem_limit_bytes=VMEM_LIMIT)


def _t5_bucket_np(dist):
    n = np.maximum(dist, 0)
    exact = REL_BUCKETS // 2
    scaled = (np.log(np.maximum(n, 1).astype(np.float32) / np.float32(exact))
              / np.float32(math.log(REL_MAX_DIST / exact))).astype(np.float32)
    large = np.minimum(exact + (scaled * np.float32(REL_BUCKETS - exact)).astype(np.int32), REL_BUCKETS - 1)
    return np.where(n < exact, n, large).astype(np.int32)


def _bias_tile_kernel(tab_ref, idx_ref, o_ref, *, head0, shift, scale):
    h = pl.program_id(0) + head0
    idx = idx_ref[0]
    acc = jnp.zeros(idx.shape, F32)
    for b in range(REL_BUCKETS):
        acc = jnp.where(idx == b, tab_ref[b, h], acc)
    if shift:
        acc = acc - tab_ref[REL_BUCKETS - 1, h]
    if scale != 1.0:
        acc = acc * scale
    o_ref[0, 0] = acc


def _bias_tiles(rel_bias, buckets, head0, nheads, shift, scale=1.0):
    R, M, N = buckets.shape
    return pl.pallas_call(
        functools.partial(_bias_tile_kernel, head0=head0, shift=shift, scale=scale),
        grid=(nheads, R),
        in_specs=[pl.BlockSpec(memory_space=pltpu.SMEM),
                  pl.BlockSpec((1, M, N), lambda h, r: (r, 0, 0))],
        out_specs=pl.BlockSpec((1, 1, M, N), lambda h, r: (h, r, 0, 0)),
        out_shape=jax.ShapeDtypeStruct((nheads, R, M, N), F32),
        compiler_params=_cparams("arbitrary", "arbitrary"),
        name="bias_tiles",
    )(rel_bias, jnp.asarray(buckets))


def _banded_buckets(nback):
    i = np.arange(BANDED_Q_BLOCKS * Q_BLOCK)[None, :]
    j = np.arange((nback + BANDED_Q_BLOCKS) * Q_BLOCK)[:, None]
    return _t5_bucket_np(i - j + nback * Q_BLOCK)[None]


def _flash_buckets():
    k = np.arange(FLASH_BLOCK)[:, None]
    q = np.arange(FLASH_BLOCK)[None, :]
    return np.stack([_t5_bucket_np(q - k), _t5_bucket_np(q - k + FLASH_BLOCK)])


def _cmp_buckets():
    i = np.arange(CMP_Q)[None, :, None]
    c2 = np.arange(2 * LANES)[None, None, :]
    r = np.arange(16 * LANES // CMP_Q)[:, None, None]
    dist = i - NSA_CMP_STRIDE * (c2 - LANES - (CMP_Q // NSA_CMP_STRIDE) * r) - (NSA_CMP_LEN - 1)
    return _t5_bucket_np(dist)


def _rms(x, g):
    return x * lax.rsqrt(jnp.mean(x * x, axis=-1, keepdims=True) + NORM_EPS) * g


def _proj_kernel(x_ref, g_ref, w_ref, wt_ref, oa_ref, ob_ref, oc_ref, od_ref, ot_ref):
    h = _rms(x_ref[...], g_ref[...]).astype(BF16)
    oa_ref[...] = jnp.dot(h, w_ref[:, 0:512], preferred_element_type=F32)
    ob_ref[...] = jnp.dot(h, w_ref[:, 512:1024], preferred_element_type=F32).astype(BF16)
    oc_ref[...] = jnp.dot(h, w_ref[:, 1024:1792], preferred_element_type=F32).astype(BF16)
    od_ref[...] = jnp.dot(h, w_ref[:, 1792:2560], preferred_element_type=F32).astype(BF16)
    ot_ref[...] = lax.dot_general(wt_ref[...], h, _NT, preferred_element_type=F32).astype(BF16)


def _in_proj(x, g, w, wt):
    T = x.shape[0]
    tm = 512
    nt = wt.shape[0]
    return pl.pallas_call(
        _proj_kernel,
        grid=(T // tm,),
        in_specs=[pl.BlockSpec((tm, D_MODEL), lambda i: (i, 0)),
                  pl.BlockSpec((1, D_MODEL), lambda i: (0, 0)),
                  pl.BlockSpec((D_MODEL, N_IN_PAD), lambda i: (0, 0)),
                  pl.BlockSpec((nt, D_MODEL), lambda i: (0, 0))],
        out_specs=[pl.BlockSpec((tm, 512), lambda i: (i, 0)),
                   pl.BlockSpec((tm, 512), lambda i: (i, 0)),
                   pl.BlockSpec((tm, 768), lambda i: (i, 0)),
                   pl.BlockSpec((tm, 768), lambda i: (i, 0)),
                   pl.BlockSpec((nt, tm), lambda i: (0, i))],
        out_shape=[jax.ShapeDtypeStruct((T, 512), F32),
                   jax.ShapeDtypeStruct((T, 512), BF16),
                   jax.ShapeDtypeStruct((T, 768), BF16),
                   jax.ShapeDtypeStruct((T, 768), BF16),
                   jax.ShapeDtypeStruct((nt, T), BF16)],
        compiler_params=_cparams("arbitrary"),
        name="in_proj",
    )(x, g, w, wt)


def _lru_kernel(xg_ref, cw_ref, cb_ref, wa_ref, ba_ref, wx_ref, bx_ref, lam_ref, o_ref,
                ext_ref, h_ref, *, Tt):
    t = pl.program_id(1)

    W = MIX_WIDTH

    @pl.when(t == 0)
    def _():
        ext_ref[0:8, :] = jnp.zeros((8, W), F32)
        h_ref[...] = jnp.zeros_like(h_ref)

    x = xg_ref[:, 0:W]
    gate = xg_ref[:, W:2 * W]
    ext_ref[8:Tt + 8, :] = x
    row = lax.broadcasted_iota(I32, (Tt, W), 0)
    xc = cb_ref[...] + x * cw_ref[CONV_WIDTH - 1:CONV_WIDTH, :]
    for s in range(1, CONV_WIDTH):
        xc = xc + ext_ref[8 - s:8 - s + Tt, :] * cw_ref[CONV_WIDTH - 1 - s:CONV_WIDTH - s, :]
    ext_ref[0:8, :] = ext_ref[Tt:Tt + 8, :]

    xcb = xc.astype(BF16)
    r = jax.nn.sigmoid(jnp.dot(xcb, wa_ref[...], preferred_element_type=F32) + ba_ref[...])
    ig = jax.nn.sigmoid(jnp.dot(xcb, wx_ref[...], preferred_element_type=F32) + bx_ref[...])
    z = -lam_ref[...]
    softplus = jnp.maximum(z, 0.0) + jnp.log1p(jnp.exp(-jnp.abs(z)))
    log_a = (-LRU_C) * r * softplus
    a = jnp.exp(log_a)
    mult = jnp.sqrt(1.0 - a * a)
    mult = jnp.where((row + t * Tt) == 0, 1.0, mult)
    b = mult * ig * xc

    d = 1
    while d < Tt:
        keep = row >= d
        b = jnp.where(keep, a * pltpu.roll(b, d, 0) + b, b)
        a = jnp.where(keep, a * pltpu.roll(a, d, 0), a)
        d *= 2
    hs = a * h_ref[7:8, :] + b
    h_ref[...] = hs[Tt - 8:Tt]
    o_ref[...] = (hs * jax.nn.gelu(gate)).astype(o_ref.dtype)


def _rglru(a_xg, conv_w, conv_b, wa_bd, b_a, wx_bd, b_x, lam, B, S):
    Tt = 256
    W = MIX_WIDTH
    vec = pl.BlockSpec((1, W), lambda b, t: (0, 0))
    mat = pl.BlockSpec((W, W), lambda b, t: (0, 0))
    out = pl.pallas_call(
        functools.partial(_lru_kernel, Tt=Tt),
        grid=(B, S // Tt),
        in_specs=[pl.BlockSpec((None, Tt, 2 * W), lambda b, t: (b, t, 0)),
                  pl.BlockSpec((CONV_WIDTH, W), lambda b, t: (0, 0)),
                  vec, mat, vec, mat, vec, vec],
        out_specs=pl.BlockSpec((None, Tt, W), lambda b, t: (b, t, 0)),
        out_shape=jax.ShapeDtypeStruct((B, S, W), BF16),
        scratch_shapes=[pltpu.VMEM((Tt + 8, W), F32), pltpu.VMEM((8, W), F32)],
        compiler_params=_cparams("arbitrary", "arbitrary"),
        name="rglru",
    )(a_xg.reshape(B, S, 2 * W), conv_w, conv_b, wa_bd, b_a, wx_bd, b_x, lam)
    return out.reshape(B * S, W)


def _lane_window(q128, lo, width, roll, scale):
    q = q128 * scale
    if roll:
        q = pltpu.roll(q, LANES // 2, 1)
    lane = lax.broadcasted_iota(I32, q.shape, 1)
    return jnp.where((lane >= lo) & (lane < lo + width), q, 0.0).astype(BF16)


def _pair_halves(even, odd, even_half, odd_half):
    if even_half == 1:
        even = pltpu.roll(even, LANES // 2, 1)
    if odd_half == 0:
        odd = pltpu.roll(odd, LANES // 2, 1)
    lane = lax.broadcasted_iota(I32, even.shape, 1)
    return jnp.where(lane < LANES // 2, even, odd)


def _banded_kernel(*refs, nback, window, q_cfg, has_sink, shared_kv):
    nk = nback + BANDED_Q_BLOCKS
    TQ = BANDED_Q_BLOCKS * Q_BLOCK
    q_ref = refs[0]
    k_refs = refs[1:1 + nk]
    pos = 1 + nk
    if shared_kv:
        v_refs = k_refs
    else:
        v_refs = refs[pos:pos + nk]
        pos += nk
    bias_ref = refs[pos]
    pos += 1
    sink_ref = None
    if has_sink:
        sink_ref = refs[pos]
        pos += 1
    o_ref = refs[pos]

    n = pl.program_id(1)
    KW = nk * Q_BLOCK
    kcat = jnp.concatenate([k_refs[nk - 1 - i][...] for i in range(nk)], axis=0)
    vcat = kcat if shared_kv else jnp.concatenate([v_refs[nk - 1 - i][...] for i in range(nk)], axis=0)
    vt = vcat.astype(F32).T.astype(BF16)
    jj = lax.broadcasted_iota(I32, (KW, TQ), 0)
    ii = lax.broadcasted_iota(I32, (KW, TQ), 1)
    dist = ii - jj + nback * Q_BLOCK
    valid = (dist >= 0) & (dist < window) & (jj >= (nback - BANDED_Q_BLOCKS * n) * Q_BLOCK)
    scale = HEAD_DIM ** -0.5

    def logits(h):
        lo, roll, _ = q_cfg[h]
        g = h // 2
        qx = _lane_window(q_ref[:, LANES * g:LANES * (g + 1)].astype(F32), lo, HEAD_DIM, roll, scale)
        s = lax.dot_general(kcat, qx, _NT, preferred_element_type=F32) + bias_ref[h, 0]
        return jnp.where(valid, s, NEG_INF)

    def softmax(h, s):
        m = jnp.max(s, axis=0, keepdims=True)
        if has_sink:
            sink = sink_ref[0:1, h:h + 1]
            m = jnp.maximum(m, sink)
        p = jnp.exp(s - m)
        den = jnp.sum(p, axis=0, keepdims=True)
        if has_sink:
            den = den + jnp.exp(sink - m)
        return p.astype(BF16), 1.0 / jnp.maximum(den, TINY)

    def values(h, p, inv):
        half = q_cfg[h][2]
        return jnp.dot(vt[HEAD_DIM * half:HEAD_DIM * (half + 1)], p, preferred_element_type=F32) * inv

    nh = len(q_cfg)
    ss = [logits(h) for h in range(nh)]
    pr = [softmax(h, ss[h]) for h in range(nh)]
    outs = [values(h, *pr[h]) for h in range(nh)]
    for g in range(nh // 2):
        pair = jnp.concatenate([outs[2 * g], outs[2 * g + 1]], axis=0)
        o_ref[:, LANES * g:LANES * (g + 1)] = pair.T.astype(o_ref.dtype)


def _banded_attention(qsrc, q_col, ksrc, k_col, vsrc, v_col, bias, sinks, *, B, S, window, q_cfg):
    nback = -(-window // Q_BLOCK)
    nq = BANDED_Q_BLOCKS
    nk = nback + nq
    TQ = nq * Q_BLOCK
    shared_kv = vsrc is None
    has_sink = sinks is not None

    def kv_spec(col, i):
        return pl.BlockSpec((None, Q_BLOCK, LANES), lambda b, n: (b, jnp.maximum(nq * n + nq - 1 - i, 0), col))

    in_specs = [pl.BlockSpec((None, TQ, 2 * LANES), lambda b, n: (b, n, q_col))]
    args = [qsrc]
    in_specs += [kv_spec(k_col, i) for i in range(nk)]
    args += [ksrc] * nk
    if not shared_kv:
        in_specs += [kv_spec(v_col, i) for i in range(nk)]
        args += [vsrc] * nk
    in_specs.append(pl.BlockSpec(bias.shape, lambda b, n: (0, 0, 0, 0)))
    args.append(bias)
    if has_sink:
        in_specs.append(pl.BlockSpec(sinks.shape, lambda b, n: (0, 0)))
        args.append(sinks)
    return pl.pallas_call(
        functools.partial(_banded_kernel, nback=nback, window=window, q_cfg=q_cfg,
                          has_sink=has_sink, shared_kv=shared_kv),
        grid=(B, S // TQ),
        in_specs=in_specs,
        out_specs=pl.BlockSpec((None, TQ, 2 * LANES), lambda b, n: (b, n, 0)),
        out_shape=jax.ShapeDtypeStruct((B, S, 2 * LANES), BF16),
        compiler_params=_cparams("arbitrary", "arbitrary"),
        name="banded_attn",
    )(*args)


def _flash_kernel(*refs, mode, lambda_init):
    TB = FLASH_BLOCK
    if mode == "diff":
        (q_ref, k_ref, vt_ref, bias_ref, lam_ref, g_ref, o_ref,
         qx_ref, m_ref, l_ref, acc_ref) = refs
        sel_ref = None
        maps = [dict(g=h // 2, lo=(h % 2) * 64 + 32 * mm, w=DIFF_DIM, roll=False, kg=h // 2, vrow=HEAD_DIM * h, h=h)
                for h in range(DIFF_HEADS) for mm in range(2)]
        scale = DIFF_DIM ** -0.5
    else:
        (q_ref, k_ref, vt_ref, pen_ref, blk_ref, bias_ref, o_ref,
         qx_ref, m_ref, l_ref, acc_ref) = refs
        sel_ref = pen_ref
        maps = [dict(g=h // 2, lo=0, w=HEAD_DIM, roll=(h % 2 == 1), kg=0, vrow=HEAD_DIM, h=h)
                for h in range(NSA_HEADS)]
        scale = HEAD_DIM ** -0.5
    n = pl.program_id(1)

    for mp, c in enumerate(maps):
        qx_ref[mp, :, 0:LANES] = _lane_window(q_ref[:, LANES * c["g"]:LANES * (c["g"] + 1)].astype(F32),
                                              c["lo"], c["w"], c["roll"], scale * LOG2E)
        if sel_ref is not None:
            qx_ref[mp, :, LANES:2 * LANES] = pen_ref[...]
    m_ref[...] = jnp.full(m_ref.shape, NEG_INF, F32)
    l_ref[...] = jnp.zeros_like(l_ref)
    acc_ref[...] = jnp.zeros_like(acc_ref)

    key = lax.broadcasted_iota(I32, (TB, TB), 0)
    qry = lax.broadcasted_iota(I32, (TB, TB), 1)

    def step(j, kind, TK=TB):
        koff = pl.multiple_of(j * TB, TB)
        valid = None
        if kind == "diag":
            valid = qry >= key

        def logits(mp):
            c = maps[mp]
            kt = k_ref[pl.ds(koff, TK), LANES * c["kg"]:LANES * (c["kg"] + 1)]
            if sel_ref is not None:
                kt = jnp.concatenate([kt, blk_ref[pl.ds(koff, TK), :]], axis=1)
            s = lax.dot_general(kt, qx_ref[mp], _NT, preferred_element_type=F32)
            if kind == "near":
                s = s + bias_ref[c["h"], 1]
            elif kind == "diag":
                s = s + bias_ref[c["h"], 0]
            if valid is not None:
                s = jnp.where(valid, s, NEG_INF)
            return s

        def softmax(mp, s):
            m_prev = m_ref[mp]
            m_new = jnp.maximum(m_prev, jnp.max(s, axis=0, keepdims=True))
            alpha = jnp.exp2(m_prev - m_new)
            p = jnp.exp2(s - m_new)
            l_ref[mp] = alpha * l_ref[mp] + jnp.sum(p, axis=0, keepdims=True)
            m_ref[mp] = m_new
            return p.astype(BF16), alpha

        def values(mp, p, alpha):
            c = maps[mp]
            vt = vt_ref[c["vrow"]:c["vrow"] + HEAD_DIM, pl.ds(koff, TK)]
            acc_ref[mp] = alpha * acc_ref[mp] + jnp.dot(vt, p, preferred_element_type=F32)

        nm = len(maps)
        ss = [logits(mp) for mp in range(nm)]

        def two_pass():
            pa = [softmax(mp, ss[mp]) for mp in range(nm)]
            for mp in range(nm):
                values(mp, *pa[mp])

        if kind == "diag":
            two_pass()
            return

        stabs = [m_ref[mp] for mp in range(nm)]
        ps = [jnp.exp2(ss[mp] - stabs[mp]) for mp in range(nm)]
        m_news = [jnp.maximum(stabs[mp], jnp.max(ss[mp], axis=0, keepdims=True)) for mp in range(nm)]
        growth = m_news[0] - stabs[0]
        for mp in range(1, nm):
            growth = jnp.maximum(growth, m_news[mp] - stabs[mp])
        safe = jnp.max(growth) <= FLASH_MAX_GROWTH
        psums = [jnp.sum(ps[mp], axis=0, keepdims=True) for mp in range(nm)]
        pvs = [jnp.dot(vt_ref[maps[mp]["vrow"]:maps[mp]["vrow"] + HEAD_DIM, pl.ds(koff, TK)], ps[mp].astype(BF16),
                       preferred_element_type=F32) for mp in range(nm)]

        @pl.when(safe)
        def _():
            for mp in range(nm):
                alpha = jnp.exp2(stabs[mp] - m_news[mp])
                l_ref[mp] = (l_ref[mp] + psums[mp]) * alpha
                acc_ref[mp] = (acc_ref[mp] + pvs[mp]) * alpha
                m_ref[mp] = m_news[mp]

        @pl.when(jnp.logical_not(safe))
        def _():
            two_pass()

    step(n, "diag")

    @pl.when(n >= 1)
    def _():
        step(n - 1, "near")

    nfar = jnp.maximum(n - 1, 0)
    FW = FLASH_FAR_BLOCKS

    def far_body(j, carry):
        step(FW * j, "far", FW * TB)
        return carry

    lax.fori_loop(0, nfar // FW, far_body, 0)

    done = (nfar // FW) * FW
    rest = nfar - done
    size = FW // 2
    while size >= 1:
        def tail_step(size=size):
            step(done + rest - rest % (2 * size), "far", size * TB)

        pl.when(rest % (2 * size) >= size)(tail_step)
        size //= 2

    def normalised(mp):
        return acc_ref[mp] * (1.0 / jnp.maximum(l_ref[mp], TINY))

    if mode == "diff":
        lq = lam_ref[...]
        lam = (jnp.exp(jnp.sum(lq[0:1] * lq[1:2], axis=1, keepdims=True))
               - jnp.exp(jnp.sum(lq[2:3] * lq[3:4], axis=1, keepdims=True)) + lambda_init)
        for g in range(DIFF_HEADS // 2):
            res = []
            for h in (2 * g, 2 * g + 1):
                o = normalised(2 * h) - lam * normalised(2 * h + 1)
                ms = jnp.sum(o * o, axis=0, keepdims=True) * (1.0 / (2 * DIFF_DIM))
                res.append(o * lax.rsqrt(ms + DIFF_SUBLN_EPS))
            out = jnp.concatenate(res, axis=0) * g_ref[...] * (1.0 - lambda_init)
            o_ref[:, LANES * g:LANES * (g + 1)] = out.T.astype(o_ref.dtype)
    else:
        for g in range(NSA_HEADS // 2):
            out = jnp.concatenate([normalised(2 * g), normalised(2 * g + 1)], axis=0)
            o_ref[:, LANES * g:LANES * (g + 1)] = out.T.astype(o_ref.dtype)


def _flash_attention(mode, qsrc, q_col, ksrc, k_col, vt, vt_rows, vt_blk, bias, extra, *, B, S, lambda_init=0.0):
    TB = FLASH_BLOCK
    q_spec = pl.BlockSpec((None, TB, 2 * LANES), lambda b, n: (b, n, q_col))
    bias_spec = pl.BlockSpec(bias.shape, lambda b, n: (0, 0, 0, 0))
    vt_spec = pl.BlockSpec((vt_rows, S), lambda b, n: (vt_blk, b))
    if mode == "diff":
        lam4, gsub = extra
        nmaps = 2 * DIFF_HEADS
        in_specs = [q_spec,
                    pl.BlockSpec((None, S, 2 * LANES), lambda b, n: (b, 0, k_col)),
                    vt_spec, bias_spec,
                    pl.BlockSpec(lam4.shape, lambda b, n: (0, 0)),
                    pl.BlockSpec(gsub.shape, lambda b, n: (0, 0))]
        args = [qsrc, ksrc, vt, bias, lam4, gsub]
    else:
        (pen,) = extra
        nmaps = NSA_HEADS
        blk = jnp.asarray(np.arange(S)[:, None] // NSA_SLC_LEN == np.arange(LANES)[None, :], BF16)
        in_specs = [q_spec,
                    pl.BlockSpec((None, S, LANES), lambda b, n: (b, 0, k_col)),
                    vt_spec,
                    pl.BlockSpec((None, TB, LANES), lambda b, n: (b, n, 0)),
                    pl.BlockSpec((S, LANES), lambda b, n: (0, 0)),
                    bias_spec]
        args = [qsrc, ksrc, vt, pen, blk, bias]
    return pl.pallas_call(
        functools.partial(_flash_kernel, mode=mode, lambda_init=lambda_init),
        grid=(B, S // TB),
        in_specs=in_specs,
        out_specs=pl.BlockSpec((None, TB, 2 * LANES), lambda b, n: (b, n, 0)),
        out_shape=jax.ShapeDtypeStruct((B, S, 2 * LANES), BF16),
        scratch_shapes=[pltpu.VMEM((nmaps, TB, LANES if mode == "diff" else 2 * LANES), BF16),
                        pltpu.VMEM((nmaps, 1, TB), F32),
                        pltpu.VMEM((nmaps, 1, TB), F32),
                        pltpu.VMEM((nmaps, HEAD_DIM, TB), F32)],
        compiler_params=_cparams("arbitrary", "arbitrary"),
        name="flash_" + mode,
    )(*args)


def _compress_kernel(ck_ref, cv_ref, pk_ref, pv_ref, w1k_ref, w1v_ref, w2_ref, o_ref):
    half = NSA_CMP_STRIDE * HEAD_DIM
    ck = ck_ref[...]
    cv = cv_ref[...]
    nrow = ck.shape[0]

    def mm(a, w):
        return jnp.dot(a, w, preferred_element_type=F32)

    top = mm(ck, w1k_ref[0:half, :]) + mm(cv, w1v_ref[0:half, :])
    bot = mm(ck, w1k_ref[half:2 * half, :]) + mm(cv, w1v_ref[half:2 * half, :])
    posw = (mm(pk_ref[...], w1k_ref[...]) + mm(pv_ref[...], w1v_ref[...]))[0:1]
    pre = top + pltpu.roll(bot, nrow - 1, 0) + posw
    kcv = mm(jax.nn.gelu(pre).astype(BF16), w2_ref[...])
    o_ref[0:LANES, :] = jnp.zeros((LANES, LANES), o_ref.dtype)
    o_ref[LANES:LANES + nrow, :] = kcv.astype(o_ref.dtype)


def _nsa_compress(ck, cv, pk, pv, w1k, w1v, w2, ncp):
    B, nchunk, width = ck.shape
    assert ncp == LANES + nchunk

    def full(a):
        return pl.BlockSpec(a.shape, lambda b: (0,) * a.ndim)

    chunk = pl.BlockSpec((None, nchunk, width), lambda b: (b, 0, 0))
    return pl.pallas_call(
        _compress_kernel,
        grid=(B,),
        in_specs=[chunk, chunk, full(pk), full(pv), full(w1k), full(w1v), full(w2)],
        out_specs=pl.BlockSpec((None, ncp, LANES), lambda b: (b, 0, 0)),
        out_shape=jax.ShapeDtypeStruct((B, ncp, LANES), BF16),
        compiler_params=_cparams("arbitrary"),
        name="nsa_compress",
    )(ck, cv, pk, pv, w1k, w1v, w2)


def _cmp_kernel(q_ref, kcv_ref, bias_ref, agg_ref, oc_ref, sel_ref, s_scr, *, n_cmp, ncp):
    n = pl.program_id(1)
    QB = CMP_Q
    kcv = kcv_ref[...]
    cp = lax.broadcasted_iota(I32, (QB, ncp), 1)
    qpos = n * QB + lax.broadcasted_iota(I32, (QB, ncp), 0)
    valid = ((cp >= LANES) & (cp < LANES + n_cmp)
             & (qpos - NSA_CMP_STRIDE * (cp - LANES) - (NSA_CMP_LEN - 1) >= 0))
    woff = pl.multiple_of((n // (16 * LANES // CMP_Q)) * LANES, LANES)
    scale = HEAD_DIM ** -0.5

    def logits(h):
        g = h // 2
        qx = _lane_window(q_ref[:, LANES * g:LANES * (g + 1)].astype(F32), 0, HEAD_DIM, h % 2 == 1, scale)
        s_scr[h] = lax.dot_general(qx, kcv, _NT, preferred_element_type=F32)
        s_scr[h, :, pl.ds(woff, 2 * LANES)] += bias_ref[h]

    def softmax(h):
        s = jnp.where(valid, s_scr[h], NEG_INF)
        m = jnp.max(s, axis=1, keepdims=True)
        p = jnp.where(valid, jnp.exp(s - m), 0.0)
        den = jnp.sum(p, axis=1, keepdims=True)
        return p * (1.0 / jnp.maximum(den, TINY))

    for h in range(NSA_HEADS):
        logits(h)
    ps = [softmax(h) for h in range(NSA_HEADS)]
    outs = [jnp.dot(p.astype(BF16), kcv, preferred_element_type=F32) for p in ps]
    for g in range(NSA_HEADS // 2):
        oc_ref[:, LANES * g:LANES * (g + 1)] = _pair_halves(outs[2 * g], outs[2 * g + 1], 1, 1).astype(oc_ref.dtype)

    psum = (ps[0] + ps[1]) + (ps[2] + ps[3])
    imp = jnp.dot(psum.astype(BF16), agg_ref[...], preferred_element_type=F32)

    v = imp.T
    jb = lax.broadcasted_iota(I32, v.shape, 0)
    qp = n * QB + lax.broadcasted_iota(I32, v.shape, 1)
    cur = qp >> 6
    forced = (jb == 0) | (jb == cur) | (jb == cur - 1)
    v = jnp.where(forced, -jnp.inf, jnp.where(jb * NSA_SLC_LEN <= qp, v, -1.0))
    rid = jb.astype(F32)
    picked = jnp.where(forced, 1.0, 0.0)
    for _ in range(NSA_TOPK - 3):
        mx = jnp.max(v, axis=0, keepdims=True)
        first = jnp.min(jnp.where(v == mx, rid, float(LANES)), axis=0, keepdims=True)
        hit = rid == first
        picked = jnp.where(hit, 1.0, picked)
        v = jnp.where(hit, -jnp.inf, v)
    sel_ref[...] = jnp.where(picked.T > 0.5, 0.0, NEG_INF).astype(sel_ref.dtype)


def _nsa_cmp_select(dsrc, kcv, bias, agg, *, B, S, n_cmp, ncp):
    return pl.pallas_call(
        functools.partial(_cmp_kernel, n_cmp=n_cmp, ncp=ncp),
        grid=(B, S // CMP_Q),
        in_specs=[pl.BlockSpec((None, CMP_Q, 2 * LANES), lambda b, n: (b, n, 0)),
                  pl.BlockSpec((None, ncp, LANES), lambda b, n: (b, 0, 0)),
                  pl.BlockSpec((NSA_HEADS, None, CMP_Q, 2 * LANES), lambda b, n: (0, n % (16 * LANES // CMP_Q), 0, 0)),
                  pl.BlockSpec((ncp, LANES), lambda b, n: (0, 0))],
        out_specs=[pl.BlockSpec((None, CMP_Q, 2 * LANES), lambda b, n: (b, n, 0)),
                   pl.BlockSpec((None, CMP_Q, LANES), lambda b, n: (b, n, 0))],
        out_shape=[jax.ShapeDtypeStruct((B, S, 2 * LANES), BF16),
                   jax.ShapeDtypeStruct((B, S, LANES), BF16)],
        scratch_shapes=[pltpu.VMEM((NSA_HEADS, CMP_Q, ncp), F32)],
        compiler_params=_cparams("arbitrary", "arbitrary"),
        name="nsa_cmp_select",
    )(dsrc, kcv, bias, agg)


def _merge_kernel(x_ref, g_ref, ya_ref, yb_ref, yc_ref, oc_ref, os_ref, ow_ref, dg_ref,
                  wg_ref, bg_ref, wb_ref, wo_ref, o_ref):
    x = x_ref[...]
    h = _rms(x, g_ref[...]).astype(BF16)
    tm = x.shape[0]
    lane = lax.broadcasted_iota(I32, (tm, LANES), 1)
    gd = jax.nn.sigmoid(dg_ref[...].astype(F32))
    yd = []
    for g in range(NSA_HEADS // 2):
        acc = jnp.zeros((tm, LANES), F32)
        for br, src in enumerate((oc_ref, os_ref, ow_ref)):
            ge = gd[:, 3 * (2 * g) + br:3 * (2 * g) + br + 1]
            go = gd[:, 3 * (2 * g + 1) + br:3 * (2 * g + 1) + br + 1]
            acc = acc + jnp.where(lane < 64, ge, go) * src[:, LANES * g:LANES * (g + 1)].astype(F32)
        yd.append(acc)
    y_d = jnp.concatenate(yd, axis=1).astype(BF16)
    ys = (ya_ref[...], yb_ref[...], yc_ref[...], y_d)
    merged = jnp.zeros((tm, D_MODEL), F32)
    for br in range(4):
        gate = jax.nn.sigmoid(jnp.dot(h, wg_ref[br], preferred_element_type=F32) + bg_ref[br:br + 1, :])
        merged = merged + gate * jnp.dot(ys[br], wb_ref[br], preferred_element_type=F32)
    o_ref[...] = x + jnp.dot(merged.astype(BF16), wo_ref[...], preferred_element_type=F32)


def _merge(x, g, ya, yb, yc, oc, osel, ow, dproj, wg, bg, wb, wo):
    T = x.shape[0]
    tm = 256

    def tile(width, col=0):
        return pl.BlockSpec((tm, width), lambda i: (i, col))

    def full(a):
        return pl.BlockSpec(a.shape, lambda i: (0,) * a.ndim)

    return pl.pallas_call(
        _merge_kernel,
        grid=(T // tm,),
        in_specs=[tile(D_MODEL), full(g), tile(256), tile(256), tile(256), tile(256), tile(256), tile(256),
                  tile(LANES, 5), full(wg), full(bg), full(wb), full(wo)],
        out_specs=tile(D_MODEL),
        out_shape=jax.ShapeDtypeStruct((T, D_MODEL), F32),
        compiler_params=_cparams("arbitrary"),
        name="merge",
    )(x, g, ya, yb, yc, oc, osel, ow, dproj, wg, bg, wb, wo)


def _route(hn, w_split, b_router):
    lane = lax.broadcasted_iota(I32, (hn.shape[0], LANES), 1)
    hi = hn.astype(BF16)
    lo = (hn - hi.astype(F32)).astype(BF16)
    a = jnp.dot(hi, w_split, preferred_element_type=F32)
    b = jnp.dot(lo, w_split[:, 0:LANES], preferred_element_type=F32)
    logits = (a[:, 0:LANES] + a[:, LANES:2 * LANES]) + b + b_router
    is_grp = (lane >= N_EXPERTS) & (lane < N_EXPERTS + N_GROUPS)
    lanef = lane.astype(F32)
    lg = jnp.where(is_grp, logits, -jnp.inf)
    mg = jnp.max(lg, axis=1, keepdims=True)
    gsel = jnp.min(jnp.where(lg == mg, lanef, 2.0 * LANES), axis=1, keepdims=True) - N_EXPERTS
    g_w = 1.0 / jnp.sum(jnp.where(is_grp, jnp.exp(lg - mg), 0.0), axis=1, keepdims=True)
    in_grp = (lane < N_EXPERTS) & ((lane >> 3).astype(F32) == gsel)
    v1 = jnp.where(in_grp, logits, -jnp.inf)
    m1 = jnp.max(v1, axis=1, keepdims=True)
    i1 = jnp.min(jnp.where(v1 == m1, lanef, 2.0 * LANES), axis=1, keepdims=True)
    v2 = jnp.where(lanef == i1, -jnp.inf, v1)
    m2 = jnp.max(v2, axis=1, keepdims=True)
    i2 = jnp.min(jnp.where(v2 == m2, lanef, 2.0 * LANES), axis=1, keepdims=True)
    e2 = jnp.exp(m2 - m1)
    w1 = g_w / (1.0 + e2)
    w2 = g_w * e2 / (1.0 + e2)
    first, second = lanef == i1, lanef == i2
    comb = jnp.where(first, w1, 0.0) + jnp.where(second, w2, 0.0)
    return comb, jnp.where(first | second, 1.0, 0.0)


def _moe_kernel(x_ref, g_ref, wr_ref, br_ref, wg_ref, wu_ref, wd_ref, fg_ref, o_ref,
                hn_ref, comb_ref, acc_ref, *, final_norm):
    e = pl.program_id(1)
    tm = x_ref.shape[0]
    lane = lax.broadcasted_iota(I32, (tm, LANES), 1)

    @pl.when(e == 0)
    def _():
        hn = _rms(x_ref[...], g_ref[...])
        hn_ref[...] = hn.astype(BF16)
        comb_ref[...] = _route(hn, wr_ref[...], br_ref[...])[0]
        acc_ref[...] = jnp.zeros_like(acc_ref)

    hn = hn_ref[...]
    gt = jnp.dot(hn, wg_ref[...], preferred_element_type=F32)
    up = jnp.dot(hn, wu_ref[...], preferred_element_type=F32)
    he = (gt * jax.nn.sigmoid(gt) * up).astype(BF16)
    ye = jnp.dot(he, wd_ref[...], preferred_element_type=F32)
    cw = jnp.sum(jnp.where(lane == e, comb_ref[...], 0.0), axis=1, keepdims=True)
    acc_ref[...] += cw * ye

    @pl.when(e == N_EXPERTS - 1)
    def _():
        y = x_ref[...] + acc_ref[...]
        if final_norm:
            y = _rms(y, fg_ref[...])
        o_ref[...] = y


def _moe(x, g, w_router, b_router, wg, wu, wd, final_g, final_norm):
    T = x.shape[0]
    tm = 1024
    return pl.pallas_call(
        functools.partial(_moe_kernel, final_norm=final_norm),
        grid=(T // tm, N_EXPERTS),
        in_specs=[pl.BlockSpec((tm, D_MODEL), lambda i, e: (i, 0)),
                  pl.BlockSpec((1, D_MODEL), lambda i, e: (0, 0)),
                  pl.BlockSpec((D_MODEL, 2 * LANES), lambda i, e: (0, 0)),
                  pl.BlockSpec((1, LANES), lambda i, e: (0, 0)),
                  pl.BlockSpec((None, D_MODEL, EXPERT_FF), lambda i, e: (e, 0, 0)),
                  pl.BlockSpec((None, D_MODEL, EXPERT_FF), lambda i, e: (e, 0, 0)),
                  pl.BlockSpec((None, EXPERT_FF, D_MODEL), lambda i, e: (e, 0, 0)),
                  pl.BlockSpec((1, D_MODEL), lambda i, e: (0, 0))],
        out_specs=pl.BlockSpec((tm, D_MODEL), lambda i, e: (i, 0)),
        out_shape=jax.ShapeDtypeStruct((T, D_MODEL), F32),
        scratch_shapes=[pltpu.VMEM((tm, D_MODEL), BF16),
                        pltpu.VMEM((tm, LANES), F32),
                        pltpu.VMEM((tm, D_MODEL), F32)],
        compiler_params=_cparams("arbitrary", "arbitrary"),
        name="moe",
    )(x, g, w_router, b_router, wg, wu, wd, final_g)


def _slot_matrix(pos_t, val_t, ntok, half):
    nslot = MOE_CAP // 2
    slot = (lax.broadcasted_iota(I32, (nslot, ntok), 0) + half * nslot).astype(F32)
    blocks = []
    for e in range(N_EXPERTS):
        hit = pos_t[e:e + 1, :] == slot
        blocks.append(jnp.where(hit, 1.0 if val_t is None else val_t[e:e + 1, :], 0.0).astype(BF16))
    return jnp.concatenate(blocks, axis=0)


ROUTE_ROWS = 48


def _route_t(hn, w_split_t, b_col):
    nr = ROUTE_ROWS
    hi = hn.astype(BF16)
    lo = (hn - hi.astype(F32)).astype(BF16)
    a = lax.dot_general(w_split_t, hi, _NT, preferred_element_type=F32)
    b = lax.dot_general(w_split_t[0:LANES], lo, _NT, preferred_element_type=F32)
    logits = (a[0:nr] + a[LANES:LANES + nr]) + b[0:nr] + b_col[0:nr]
    row = lax.broadcasted_iota(I32, logits.shape, 0)
    rowf = row.astype(F32)
    is_grp = (row >= N_EXPERTS) & (row < N_EXPERTS + N_GROUPS)
    lg = jnp.where(is_grp, logits, -jnp.inf)
    mg = jnp.max(lg, axis=0, keepdims=True)
    gsel = jnp.min(jnp.where(lg == mg, rowf, 2.0 * LANES), axis=0, keepdims=True) - N_EXPERTS
    g_w = 1.0 / jnp.sum(jnp.where(is_grp, jnp.exp(lg - mg), 0.0), axis=0, keepdims=True)
    in_grp = (row < N_EXPERTS) & ((row >> 3).astype(F32) == gsel)
    v1 = jnp.where(in_grp, logits, -jnp.inf)
    m1 = jnp.max(v1, axis=0, keepdims=True)
    i1 = jnp.min(jnp.where(v1 == m1, rowf, 2.0 * LANES), axis=0, keepdims=True)
    v2 = jnp.where(rowf == i1, -jnp.inf, v1)
    m2 = jnp.max(v2, axis=0, keepdims=True)
    i2 = jnp.min(jnp.where(v2 == m2, rowf, 2.0 * LANES), axis=0, keepdims=True)
    e2 = jnp.exp(m2 - m1)
    w1 = g_w / (1.0 + e2)
    w2 = g_w * e2 / (1.0 + e2)
    first, second = rowf == i1, rowf == i2
    comb = jnp.where(first, w1, 0.0) + jnp.where(second, w2, 0.0)
    return comb, jnp.where(first | second, 1.0, 0.0)


def _dispatch_kernel(x_ref, g_ref, wr_ref, br_ref, tri_ref, xd_ref, pos_ref, wt_ref, cnt_ref):
    tm = x_ref.shape[0]
    nr = ROUTE_ROWS
    hn = _rms(x_ref[...], g_ref[...])
    comb_t, routed_t = _route_t(hn, wr_ref[...], br_ref[...])
    rank_t = jnp.dot(routed_t.astype(BF16), tri_ref[...], preferred_element_type=F32)
    pos_t = jnp.where(routed_t > 0.0, rank_t, -1.0)
    pos_ref[0:nr, :] = pos_t
    pos_ref[nr:LANES, :] = jnp.full((LANES - nr, tm), -1.0, F32)
    wt_ref[0:nr, :] = comb_t
    wt_ref[nr:LANES, :] = jnp.zeros((LANES - nr, tm), F32)
    counts = jnp.sum(routed_t, axis=1, keepdims=True)
    cnt_ref[0:nr, :] = jnp.broadcast_to(counts, (nr, LANES))
    cnt_ref[nr:LANES, :] = jnp.zeros((LANES - nr, LANES), F32)
    hnb = hn.astype(BF16)
    nslot = MOE_CAP // 2
    lower = jnp.dot(_slot_matrix(pos_t, None, tm, 0), hnb, preferred_element_type=F32)
    xd_ref[:, 0:nslot, :] = lower.reshape(N_EXPERTS, nslot, D_MODEL).astype(xd_ref.dtype)

    busy = jnp.max(counts) > nslot

    @pl.when(busy)
    def _():
        upper = jnp.dot(_slot_matrix(pos_t, None, tm, 1), hnb, preferred_element_type=F32)
        xd_ref[:, nslot:MOE_CAP, :] = upper.reshape(N_EXPERTS, nslot, D_MODEL).astype(xd_ref.dtype)

    @pl.when(jnp.logical_not(busy))
    def _():
        xd_ref[:, nslot:MOE_CAP, :] = jnp.zeros((N_EXPERTS, nslot, D_MODEL), xd_ref.dtype)


def _expert_kernel(busy_ref, xlo_ref, xhi_ref, wg_ref, wu_ref, wd_ref, olo_ref, ohi_ref):
    ntile, nslot, _ = xlo_ref.shape

    def ffn(x_ref, o_ref):
        x = x_ref[...].reshape(ntile * nslot, D_MODEL)
        gt = jnp.dot(x, wg_ref[...], preferred_element_type=F32)
        up = jnp.dot(x, wu_ref[...], preferred_element_type=F32)
        he = (gt * jax.nn.sigmoid(gt) * up).astype(BF16)
        ye = jnp.dot(he, wd_ref[...], preferred_element_type=F32)
        o_ref[...] = ye.reshape(ntile, nslot, D_MODEL).astype(o_ref.dtype)

    ffn(xlo_ref, olo_ref)
    busy = busy_ref[pl.program_id(0), pl.program_id(1)] > 0

    @pl.when(busy)
    def _():
        ffn(xhi_ref, ohi_ref)

    @pl.when(jnp.logical_not(busy))
    def _():
        ohi_ref[...] = jnp.zeros(ohi_ref.shape, ohi_ref.dtype)


def _combine_kernel(busy_ref, x_ref, ylo_ref, yhi_ref, pos_ref, wt_ref, fg_ref, o_ref, *, final_norm):
    tm = x_ref.shape[0]
    nslot = MOE_CAP // 2

    def gathered(y_ref, half):
        weights = _slot_matrix(pos_ref[...], wt_ref[...], tm, half)
        yd = y_ref[...].reshape(N_EXPERTS * nslot, D_MODEL)
        return lax.dot_general(weights, yd, (((0,), (0,)), ((), ())), preferred_element_type=F32)

    o_ref[...] = x_ref[...] + gathered(ylo_ref, 0)

    @pl.when(busy_ref[pl.program_id(0)] > 0)
    def _():
        o_ref[...] += gathered(yhi_ref, 1)

    if final_norm:
        o_ref[...] = _rms(o_ref[...], fg_ref[...])


def _moe_dispatch(x, g, w_router, b_router):
    T = x.shape[0]
    tm = MOE_TILE
    nt = T // tm
    tri = jnp.asarray(np.triu(np.ones((tm, tm), np.float32), 1), BF16)
    xd, pos_t, w_t, cnt = pl.pallas_call(
        _dispatch_kernel,
        grid=(nt,),
        in_specs=[pl.BlockSpec((tm, D_MODEL), lambda i: (i, 0)),
                  pl.BlockSpec((1, D_MODEL), lambda i: (0, 0)),
                  pl.BlockSpec((2 * LANES, D_MODEL), lambda i: (0, 0)),
                  pl.BlockSpec((LANES, 1), lambda i: (0, 0)),
                  pl.BlockSpec((tm, tm), lambda i: (0, 0))],
        out_specs=[pl.BlockSpec((N_EXPERTS, None, MOE_CAP, D_MODEL), lambda i: (0, i, 0, 0)),
                   pl.BlockSpec((LANES, tm), lambda i: (0, i)),
                   pl.BlockSpec((LANES, tm), lambda i: (0, i)),
                   pl.BlockSpec((None, LANES, LANES), lambda i: (i, 0, 0))],
        out_shape=[jax.ShapeDtypeStruct((N_EXPERTS, nt, MOE_CAP, D_MODEL), BF16),
                   jax.ShapeDtypeStruct((LANES, T), F32),
                   jax.ShapeDtypeStruct((LANES, T), F32),
                   jax.ShapeDtypeStruct((nt, LANES, LANES), F32)],
        compiler_params=_cparams("arbitrary"),
        name="moe_dispatch",
    )(x, g, w_router.T, b_router.reshape(LANES, 1), tri)
    return xd, pos_t, w_t, cnt[:, 0:N_EXPERTS, 0]


def _moe_finish(x, xd, pos_t, w_t, counts, wg, wu, wd, final_g, final_norm):
    T = x.shape[0]
    tm = MOE_TILE
    nt = T // tm
    tps = min(MOE_TILES_PER_STEP, nt)
    assert nt % tps == 0
    nslot = MOE_CAP // 2
    hot = counts > nslot
    busy_blocks = jnp.any(hot.reshape(nt // tps, tps, N_EXPERTS), axis=1).T.astype(I32)
    busy_tiles = jnp.any(hot, axis=1).astype(I32)
    halves = xd.reshape(N_EXPERTS, nt, 2, nslot, D_MODEL)
    half_out = jax.ShapeDtypeStruct((N_EXPERTS, nt, nslot, D_MODEL), BF16)
    yd_lo, yd_hi = pl.pallas_call(
        _expert_kernel,
        grid_spec=pltpu.PrefetchScalarGridSpec(
            num_scalar_prefetch=1,
            grid=(N_EXPERTS, nt // tps),
            in_specs=[pl.BlockSpec((None, tps, None, nslot, D_MODEL), lambda e, c, b: (e, c, 0, 0, 0)),
                      pl.BlockSpec((None, tps, None, nslot, D_MODEL),
                                   lambda e, c, b: (e, jnp.where(b[e, c] > 0, c, 0), 1, 0, 0)),
                      pl.BlockSpec((None, D_MODEL, EXPERT_FF), lambda e, c, b: (e, 0, 0)),
                      pl.BlockSpec((None, D_MODEL, EXPERT_FF), lambda e, c, b: (e, 0, 0)),
                      pl.BlockSpec((None, EXPERT_FF, D_MODEL), lambda e, c, b: (e, 0, 0))],
            out_specs=[pl.BlockSpec((None, tps, nslot, D_MODEL), lambda e, c, b: (e, c, 0, 0)),
                       pl.BlockSpec((None, tps, nslot, D_MODEL), lambda e, c, b: (e, c, 0, 0))]),
        out_shape=[half_out, half_out],
        compiler_params=_cparams("arbitrary", "arbitrary"),
        name="moe_experts",
    )(busy_blocks, halves, halves, wg, wu, wd)
    return pl.pallas_call(
        functools.partial(_combine_kernel, final_norm=final_norm),
        grid_spec=pltpu.PrefetchScalarGridSpec(
            num_scalar_prefetch=1,
            grid=(nt,),
            in_specs=[pl.BlockSpec((tm, D_MODEL), lambda i, b: (i, 0)),
                      pl.BlockSpec((N_EXPERTS, None, nslot, D_MODEL), lambda i, b: (0, i, 0, 0)),
                      pl.BlockSpec((N_EXPERTS, None, nslot, D_MODEL),
                                   lambda i, b: (0, jnp.where(b[i] > 0, i, 0), 0, 0)),
                      pl.BlockSpec((LANES, tm), lambda i, b: (0, i)),
                      pl.BlockSpec((LANES, tm), lambda i, b: (0, i)),
                      pl.BlockSpec((1, D_MODEL), lambda i, b: (0, 0))],
            out_specs=pl.BlockSpec((tm, D_MODEL), lambda i, b: (i, 0))),
        out_shape=jax.ShapeDtypeStruct((T, D_MODEL), F32),
        compiler_params=_cparams("arbitrary"),
        name="moe_combine",
    )(busy_tiles, x, yd_lo, yd_hi, pos_t, w_t, final_g)


def _block_diag(w):
    nb, bs, _ = w.shape
    out = jnp.zeros((nb * bs, nb * bs), w.dtype)
    for i in range(nb):
        out = out.at[i * bs:(i + 1) * bs, i * bs:(i + 1) * bs].set(w[i])
    return out


def _selection_aggregator_np(n_cmp, n_slc):
    ratio_s = NSA_SLC_LEN // NSA_CMP_STRIDE
    ratio_c = NSA_CMP_LEN // NSA_CMP_STRIDE
    w = np.convolve(np.ones(ratio_s), np.ones(ratio_c)).astype(np.float32)
    cj = np.arange(n_slc)[:, None] * ratio_s - (ratio_c - 1) + np.arange(w.size)[None, :]
    jj = np.broadcast_to(np.arange(n_slc)[:, None], cj.shape)
    ww = np.broadcast_to(w[None, :], cj.shape)
    keep = (cj >= 0) & (cj < n_cmp)
    agg = np.zeros((n_cmp, n_slc), np.float32)
    np.add.at(agg, (cj[keep], jj[keep]), ww[keep])
    return agg


def _row(v):
    return v.reshape(1, -1).astype(F32)


def _diff_lambda_init(layer):
    return 0.8 - 0.6 * math.exp(-0.3 * layer)


def _mixer_layer(x, l, B, S, tiles, p):
    (bias_swa, bias_diff, bias_slc, bias_win, bias_cmp, agg_pad, n_cmp, ncp) = tiles
    w_in = jnp.pad(p["w_in"][l], ((0, 0), (0, N_IN_PAD - p["w_in"].shape[2]))).astype(BF16)
    w_t = jnp.concatenate([w_in[:, 1536:1792], w_in[:, 2176:2304]], axis=1).T
    a_xg, b_qkv, c_qkv, d_all, vals_t = _in_proj(x, _row(p["mix_norm_g"][l]), w_in, w_t)

    y_a = _rglru(a_xg, p["conv_w"][l].astype(F32), _row(p["conv_b"][l]),
                 _block_diag(p["lru_w_a"][l]).astype(BF16), _row(p["lru_b_a"][l]),
                 _block_diag(p["lru_w_x"][l]).astype(BF16), _row(p["lru_b_x"][l]),
                 _row(p["lru_lambda"][l]), B, S)

    b3 = b_qkv.reshape(B, S, 512)
    swa_cfg = tuple(((h // 2) * 64, (h % 2) != (h // 2), h // 2) for h in range(SWA_HEADS))
    y_b = _banded_attention(b3, 0, b3, 2, b3, 3, bias_swa, _row(p["swa_sinks"][l]),
                            B=B, S=S, window=SWA_WINDOW, q_cfg=swa_cfg)

    c3 = c_qkv.reshape(B, S, 768)
    lam4 = jnp.stack([p["diff_lq1"][l], p["diff_lk1"][l], p["diff_lq2"][l], p["diff_lk2"][l]]).astype(F32)
    gsub = jnp.tile(p["diff_subln_g"][l].reshape(-1, 1).astype(F32), (2, 1))
    y_c = _flash_attention("diff", c3, 0, c3, 1, vals_t, 256, 0, bias_diff, (lam4, gsub), B=B, S=S,
                           lambda_init=_diff_lambda_init(l))

    d3 = d_all.reshape(B, S, 768)
    nchunk = S // NSA_CMP_STRIDE
    ck = d3[:, :, 256:320].reshape(B, nchunk, NSA_CMP_STRIDE * HEAD_DIM)
    cv = d3[:, :, 320:384].reshape(B, nchunk, NSA_CMP_STRIDE * HEAD_DIM)

    def pos8(a):
        return jnp.broadcast_to(a.reshape(1, -1), (8, a.size)).astype(BF16)

    w1k = jnp.pad(p["nsa_w1_k"][l], ((0, 0), (0, HEAD_DIM))).astype(BF16)
    w1v = jnp.pad(p["nsa_w1_v"][l], ((0, 0), (HEAD_DIM, 0))).astype(BF16)
    w2 = _block_diag(jnp.stack([p["nsa_w2_k"][l], p["nsa_w2_v"][l]])).astype(BF16)
    kcv = _nsa_compress(ck, cv, pos8(p["nsa_pos_k"][l]), pos8(p["nsa_pos_v"][l]), w1k, w1v, w2, ncp)
    o_c, sel = _nsa_cmp_select(d3, kcv, bias_cmp, agg_pad, B=B, S=S, n_cmp=n_cmp, ncp=ncp)
    o_s = _flash_attention("nsa", d3, 0, d3, 3, vals_t, 128, 2, bias_slc, (sel,), B=B, S=S)
    win_cfg = tuple((0, h % 2 == 1, 1) for h in range(NSA_HEADS))
    o_w = _banded_attention(d3, 0, d3, 4, None, None, bias_win, None, B=B, S=S, window=NSA_WINDOW, q_cfg=win_cfg)

    T = B * S
    return _merge(x, _row(p["mix_norm_g"][l]), y_a, y_b.reshape(T, 256), y_c.reshape(T, 256),
                  o_c.reshape(T, 256), o_s.reshape(T, 256), o_w.reshape(T, 256), d_all,
                  p["w_gate"][l].astype(BF16), p["b_gate"][l].astype(F32), p["w_branch"][l].astype(BF16),
                  p["w_out"][l].astype(BF16))


def _moe_layer(x, l, p, final_norm):
    w_router = jnp.concatenate([p["w_router_exp"][l], p["w_router_grp"][l]], axis=1)
    w_router = jnp.pad(w_router, ((0, 0), (0, LANES - w_router.shape[1]))).astype(F32)
    w_hi = w_router.astype(BF16)
    w_router = jnp.concatenate([w_hi, (w_router - w_hi.astype(F32)).astype(BF16)], axis=1)
    b_router = jnp.concatenate([p["b_router_exp"][l], p["b_router_grp"][l]])
    b_router = jnp.pad(b_router, (0, LANES - b_router.shape[0])).reshape(1, LANES).astype(F32)
    g = _row(p["ffn_norm_g"][l])
    fg = _row(p["final_norm_g"])
    wg, wu, wd = (p[k][l].astype(BF16) for k in ("w_exp_gate", "w_exp_up", "w_exp_down"))
    xd, pos_t, w_t, counts = _moe_dispatch(x, g, w_router, b_router)
    return lax.cond(jnp.max(counts) > MOE_CAP,
                    lambda: _moe(x, g, w_router, b_router, wg, wu, wd, fg, final_norm),
                    lambda: _moe_finish(x, xd, pos_t, w_t, counts, wg, wu, wd, fg, final_norm))


def _forward(p):
    x = p["x"]
    B, S, D = x.shape
    depth = p["w_in"].shape[0]
    assert D == D_MODEL and S % (16 * Q_BLOCK) == 0 and S // NSA_SLC_LEN <= LANES
    rel = p["rel_bias"].astype(F32)
    n_cmp = (S - NSA_CMP_LEN) // NSA_CMP_STRIDE + 1
    ncp = LANES + S // NSA_CMP_STRIDE
    agg = _selection_aggregator_np(n_cmp, S // NSA_SLC_LEN)
    agg_pad = np.zeros((ncp, LANES), np.float32)
    agg_pad[LANES:LANES + n_cmp, :agg.shape[1]] = agg
    tiles = (
        _bias_tiles(rel, _banded_buckets(1), 0, SWA_HEADS, False),
        _bias_tiles(rel, _flash_buckets(), SWA_HEADS, DIFF_HEADS, True, LOG2E),
        _bias_tiles(rel, _flash_buckets(), SWA_HEADS + DIFF_HEADS, NSA_HEADS, True, LOG2E),
        _bias_tiles(rel, _banded_buckets(NSA_WINDOW // Q_BLOCK), SWA_HEADS + DIFF_HEADS, NSA_HEADS, False),
        _bias_tiles(rel, _cmp_buckets(), SWA_HEADS + DIFF_HEADS, NSA_HEADS, True),
        jnp.asarray(agg_pad, BF16), n_cmp, ncp,
    )
    xt = x.reshape(B * S, D).astype(F32)
    for l in range(depth):
        xt = _mixer_layer(xt, l, B, S, tiles, p)
        xt = _moe_layer(xt, l, p, final_norm=(l == depth - 1))
    return xt.reshape(B, S, D).astype(x.dtype)


def kernel(x, rel_bias, final_norm_g, mix_norm_g, w_in, conv_w, conv_b, lru_w_a, lru_b_a, lru_w_x, lru_b_x,
           lru_lambda, swa_sinks, diff_lq1, diff_lk1, diff_lq2, diff_lk2, diff_subln_g, nsa_pos_k, nsa_w1_k,
           nsa_w2_k, nsa_pos_v, nsa_w1_v, nsa_w2_v, w_gate, b_gate, w_branch, w_out, ffn_norm_g, w_router_grp,
           b_router_grp, w_router_exp, b_router_exp, w_exp_gate, w_exp_up, w_exp_down):
    return _forward(dict(
        x=x, rel_bias=rel_bias, final_norm_g=final_norm_g, mix_norm_g=mix_norm_g, w_in=w_in, conv_w=conv_w,
        conv_b=conv_b, lru_w_a=lru_w_a, lru_b_a=lru_b_a, lru_w_x=lru_w_x, lru_b_x=lru_b_x, lru_lambda=lru_lambda,
        swa_sinks=swa_sinks, diff_lq1=diff_lq1, diff_lk1=diff_lk1, diff_lq2=diff_lq2, diff_lk2=diff_lk2,
        diff_subln_g=diff_subln_g, nsa_pos_k=nsa_pos_k, nsa_w1_k=nsa_w1_k, nsa_w2_k=nsa_w2_k, nsa_pos_v=nsa_pos_v,
        nsa_w1_v=nsa_w1_v, nsa_w2_v=nsa_w2_v, w_gate=w_gate, b_gate=b_gate, w_branch=w_branch, w_out=w_out,
        ffn_norm_g=ffn_norm_g, w_router_grp=w_router_grp, b_router_grp=b_router_grp, w_router_exp=w_router_exp,
        b_router_exp=b_router_exp, w_exp_gate=w_exp_gate, w_exp_up=w_exp_up, w_exp_down=w_exp_down))
```

```python
import functools
import math

import numpy as np
import jax
import jax.numpy as jnp
from jax import lax
from jax.experimental import pallas as pl
from jax.experimental.pallas import tpu as pltpu

F32 = jnp.float32
BF16 = jnp.bfloat16
I32 = jnp.int32

D_MODEL = 1024
HEAD_DIM = 64
MIX_WIDTH = 256
LRU_C = 8.0
CONV_WIDTH = 4
SWA_HEADS = 4
SWA_WINDOW = 128
DIFF_HEADS = 4
DIFF_DIM = 32
DIFF_SUBLN_EPS = 1e-5
NSA_HEADS = 4
NSA_CMP_LEN = 32
NSA_CMP_STRIDE = 16
NSA_SLC_LEN = 64
NSA_TOPK = 16
NSA_WINDOW = 512
NSA_FORCED_SCORE = 1e4
REL_BUCKETS = 32
REL_MAX_DIST = 128
N_GROUPS = 4
EXPERTS_PER_GROUP = 8
N_EXPERTS = 32
EXPERT_FF = 256
NORM_EPS = 1e-6
NEG_INF = -1e30
TINY = 1e-30
LOG2E = math.log2(math.e)

LANES = 128
Q_BLOCK = 128
CMP_Q = 512
BANDED_Q_BLOCKS = 4
FLASH_BLOCK = 256
MOE_TILE = 512
MOE_CAP = 128
MOE_TILES_PER_STEP = 8
FLASH_MAX_GROWTH = 64.0
FLASH_FAR_BLOCKS = 8
N_IN_PAD = 2560
VMEM_LIMIT = 56 * 1024 * 1024

_NT = (((1,), (1,)), ((), ()))


def _cparams(*sem):
    return pltpu.CompilerParams(dimension_semantics=sem, vmem_limit_bytes=VMEM_LIMIT)


def _t5_bucket_np(dist):
    n = np.maximum(dist, 0)
    exact = REL_BUCKETS // 2
    scaled = (np.log(np.maximum(n, 1).astype(np.float32) / np.float32(exact))
              / np.float32(math.log(REL_MAX_DIST / exact))).astype(np.float32)
    large = np.minimum(exact + (scaled * np.float32(REL_BUCKETS - exact)).astype(np.int32), REL_BUCKETS - 1)
    return np.where(n < exact, n, large).astype(np.int32)


def _bias_tile_kernel(tab_ref, idx_ref, o_ref, *, head0, shift, scale):
    h = pl.program_id(0) + head0
    idx = idx_ref[0]
    acc = jnp.zeros(idx.shape, F32)
    for b in range(REL_BUCKETS):
        acc = jnp.where(idx == b, tab_ref[b, h], acc)
    if shift:
        acc = acc - tab_ref[REL_BUCKETS - 1, h]
    if scale != 1.0:
        acc = acc * scale
    o_ref[0, 0] = acc


def _bias_tiles(rel_bias, buckets, head0, nheads, shift, scale=1.0):
    R, M, N = buckets.shape
    return pl.pallas_call(
        functools.partial(_bias_tile_kernel, head0=head0, shift=shift, scale=scale),
        grid=(nheads, R),
        in_specs=[pl.BlockSpec(memory_space=pltpu.SMEM),
                  pl.BlockSpec((1, M, N), lambda h, r: (r, 0, 0))],
        out_specs=pl.BlockSpec((1, 1, M, N), lambda h, r: (h, r, 0, 0)),
        out_shape=jax.ShapeDtypeStruct((nheads, R, M, N), F32),
        compiler_params=_cparams("arbitrary", "arbitrary"),
        name="bias_tiles",
    )(rel_bias, jnp.asarray(buckets))


def _banded_buckets(nback):
    i = np.arange(BANDED_Q_BLOCKS * Q_BLOCK)[None, :]
    j = np.arange((nback + BANDED_Q_BLOCKS) * Q_BLOCK)[:, None]
    return _t5_bucket_np(i - j + nback * Q_BLOCK)[None]


def _flash_buckets():
    k = np.arange(FLASH_BLOCK)[:, None]
    q = np.arange(FLASH_BLOCK)[None, :]
    return np.stack([_t5_bucket_np(q - k), _t5_bucket_np(q - k + FLASH_BLOCK)])


def _cmp_buckets():
    i = np.arange(CMP_Q)[None, :, None]
    c2 = np.arange(2 * LANES)[None, None, :]
    r = np.arange(16 * LANES // CMP_Q)[:, None, None]
    dist = i - NSA_CMP_STRIDE * (c2 - LANES - (CMP_Q // NSA_CMP_STRIDE) * r) - (NSA_CMP_LEN - 1)
    return _t5_bucket_np(dist)


def _rms(x, g):
    return x * lax.rsqrt(jnp.mean(x * x, axis=-1, keepdims=True) + NORM_EPS) * g


def _proj_kernel(x_ref, g_ref, w_ref, wt_ref, oa_ref, ob_ref, oc_ref, od_ref, ot_ref):
    h = _rms(x_ref[...], g_ref[...]).astype(BF16)
    oa_ref[...] = jnp.dot(h, w_ref[:, 0:512], preferred_element_type=F32)
    ob_ref[...] = jnp.dot(h, w_ref[:, 512:1024], preferred_element_type=F32).astype(BF16)
    oc_ref[...] = jnp.dot(h, w_ref[:, 1024:1792], preferred_element_type=F32).astype(BF16)
    od_ref[...] = jnp.dot(h, w_ref[:, 1792:2560], preferred_element_type=F32).astype(BF16)
    ot_ref[...] = lax.dot_general(wt_ref[...], h, _NT, preferred_element_type=F32).astype(BF16)


def _in_proj(x, g, w, wt):
    T = x.shape[0]
    tm = 512
    nt = wt.shape[0]
    return pl.pallas_call(
        _proj_kernel,
        grid=(T // tm,),
        in_specs=[pl.BlockSpec((tm, D_MODEL), lambda i: (i, 0)),
                  pl.BlockSpec((1, D_MODEL), lambda i: (0, 0)),
                  pl.BlockSpec((D_MODEL, N_IN_PAD), lambda i: (0, 0)),
                  pl.BlockSpec((nt, D_MODEL), lambda i: (0, 0))],
        out_specs=[pl.BlockSpec((tm, 512), lambda i: (i, 0)),
                   pl.BlockSpec((tm, 512), lambda i: (i, 0)),
                   pl.BlockSpec((tm, 768), lambda i: (i, 0)),
                   pl.BlockSpec((tm, 768), lambda i: (i, 0)),
                   pl.BlockSpec((nt, tm), lambda i: (0, i))],
        out_shape=[jax.ShapeDtypeStruct((T, 512), F32),
                   jax.ShapeDtypeStruct((T, 512), BF16),
                   jax.ShapeDtypeStruct((T, 768), BF16),
                   jax.ShapeDtypeStruct((T, 768), BF16),
                   jax.ShapeDtypeStruct((nt, T), BF16)],
        compiler_params=_cparams("arbitrary"),
        name="in_proj",
    )(x, g, w, wt)


def _lru_kernel(xg_ref, cw_ref, cb_ref, wa_ref, ba_ref, wx_ref, bx_ref, lam_ref, o_ref,
                ext_ref, h_ref, *, Tt):
    t = pl.program_id(1)

    W = MIX_WIDTH

    @pl.when(t == 0)
    def _():
        ext_ref[0:8, :] = jnp.zeros((8, W), F32)
        h_ref[...] = jnp.zeros_like(h_ref)

    x = xg_ref[:, 0:W]
    gate = xg_ref[:, W:2 * W]
    ext_ref[8:Tt + 8, :] = x
    row = lax.broadcasted_iota(I32, (Tt, W), 0)
    xc = cb_ref[...] + x * cw_ref[CONV_WIDTH - 1:CONV_WIDTH, :]
    for s in range(1, CONV_WIDTH):
        xc = xc + ext_ref[8 - s:8 - s + Tt, :] * cw_ref[CONV_WIDTH - 1 - s:CONV_WIDTH - s, :]
    ext_ref[0:8, :] = ext_ref[Tt:Tt + 8, :]

    xcb = xc.astype(BF16)
    r = jax.nn.sigmoid(jnp.dot(xcb, wa_ref[...], preferred_element_type=F32) + ba_ref[...])
    ig = jax.nn.sigmoid(jnp.dot(xcb, wx_ref[...], preferred_element_type=F32) + bx_ref[...])
    z = -lam_ref[...]
    softplus = jnp.maximum(z, 0.0) + jnp.log1p(jnp.exp(-jnp.abs(z)))
    log_a = (-LRU_C) * r * softplus
    a = jnp.exp(log_a)
    mult = jnp.sqrt(1.0 - a * a)
    mult = jnp.where((row + t * Tt) == 0, 1.0, mult)
    b = mult * ig * xc

    d = 1
    while d < Tt:
        keep = row >= d
        b = jnp.where(keep, a * pltpu.roll(b, d, 0) + b, b)
        a = jnp.where(keep, a * pltpu.roll(a, d, 0), a)
        d *= 2
    hs = a * h_ref[7:8, :] + b
    h_ref[...] = hs[Tt - 8:Tt]
    o_ref[...] = (hs * jax.nn.gelu(gate)).astype(o_ref.dtype)


def _rglru(a_xg, conv_w, conv_b, wa_bd, b_a, wx_bd, b_x, lam, B, S):
    Tt = 256
    W = MIX_WIDTH
    vec = pl.BlockSpec((1, W), lambda b, t: (0, 0))
    mat = pl.BlockSpec((W, W), lambda b, t: (0, 0))
    out = pl.pallas_call(
        functools.partial(_lru_kernel, Tt=Tt),
        grid=(B, S // Tt),
        in_specs=[pl.BlockSpec((None, Tt, 2 * W), lambda b, t: (b, t, 0)),
                  pl.BlockSpec((CONV_WIDTH, W), lambda b, t: (0, 0)),
                  vec, mat, vec, mat, vec, vec],
        out_specs=pl.BlockSpec((None, Tt, W), lambda b, t: (b, t, 0)),
        out_shape=jax.ShapeDtypeStruct((B, S, W), BF16),
        scratch_shapes=[pltpu.VMEM((Tt + 8, W), F32), pltpu.VMEM((8, W), F32)],
        compiler_params=_cparams("arbitrary", "arbitrary"),
        name="rglru",
    )(a_xg.reshape(B, S, 2 * W), conv_w, conv_b, wa_bd, b_a, wx_bd, b_x, lam)
    return out.reshape(B * S, W)


def _lane_window(q128, lo, width, roll, scale):
    q = q128 * scale
    if roll:
        q = pltpu.roll(q, LANES // 2, 1)
    lane = lax.broadcasted_iota(I32, q.shape, 1)
    return jnp.where((lane >= lo) & (lane < lo + width), q, 0.0).astype(BF16)


def _pair_halves(even, odd, even_half, odd_half):
    if even_half == 1:
        even = pltpu.roll(even, LANES // 2, 1)
    if odd_half == 0:
        odd = pltpu.roll(odd, LANES // 2, 1)
    lane = lax.broadcasted_iota(I32, even.shape, 1)
    return jnp.where(lane < LANES // 2, even, odd)


def _banded_kernel(*refs, nback, window, q_cfg, has_sink, shared_kv):
    nk = nback + BANDED_Q_BLOCKS
    TQ = BANDED_Q_BLOCKS * Q_BLOCK
    q_ref = refs[0]
    k_refs = refs[1:1 + nk]
    pos = 1 + nk
    if shared_kv:
        v_refs = k_refs
    else:
        v_refs = refs[pos:pos + nk]
        pos += nk
    bias_ref = refs[pos]
    pos += 1
    sink_ref = None
    if has_sink:
        sink_ref = refs[pos]
        pos += 1
    o_ref = refs[pos]

    n = pl.program_id(1)
    KW = nk * Q_BLOCK
    kcat = jnp.concatenate([k_refs[nk - 1 - i][...] for i in range(nk)], axis=0)
    vcat = kcat if shared_kv else jnp.concatenate([v_refs[nk - 1 - i][...] for i in range(nk)], axis=0)
    vt = vcat.astype(F32).T.astype(BF16)
    jj = lax.broadcasted_iota(I32, (KW, TQ), 0)
    ii = lax.broadcasted_iota(I32, (KW, TQ), 1)
    dist = ii - jj + nback * Q_BLOCK
    valid = (dist >= 0) & (dist < window) & (jj >= (nback - BANDED_Q_BLOCKS * n) * Q_BLOCK)
    scale = HEAD_DIM ** -0.5

    def logits(h):
        lo, roll, _ = q_cfg[h]
        g = h // 2
        qx = _lane_window(q_ref[:, LANES * g:LANES * (g + 1)].astype(F32), lo, HEAD_DIM, roll, scale)
        s = lax.dot_general(kcat, qx, _NT, preferred_element_type=F32) + bias_ref[h, 0]
        return jnp.where(valid, s, NEG_INF)

    def softmax(h, s):
        m = jnp.max(s, axis=0, keepdims=True)
        if has_sink:
            sink = sink_ref[0:1, h:h + 1]
            m = jnp.maximum(m, sink)
        p = jnp.exp(s - m)
        den = jnp.sum(p, axis=0, keepdims=True)
        if has_sink:
            den = den + jnp.exp(sink - m)
        return p.astype(BF16), 1.0 / jnp.maximum(den, TINY)

    def values(h, p, inv):
        half = q_cfg[h][2]
        return jnp.dot(vt[HEAD_DIM * half:HEAD_DIM * (half + 1)], p, preferred_element_type=F32) * inv

    nh = len(q_cfg)
    ss = [logits(h) for h in range(nh)]
    pr = [softmax(h, ss[h]) for h in range(nh)]
    outs = [values(h, *pr[h]) for h in range(nh)]
    for g in range(nh // 2):
        pair = jnp.concatenate([outs[2 * g], outs[2 * g + 1]], axis=0)
        o_ref[:, LANES * g:LANES * (g + 1)] = pair.T.astype(o_ref.dtype)


def _banded_attention(qsrc, q_col, ksrc, k_col, vsrc, v_col, bias, sinks, *, B, S, window, q_cfg):
    nback = -(-window // Q_BLOCK)
    nq = BANDED_Q_BLOCKS
    nk = nback + nq
    TQ = nq * Q_BLOCK
    shared_kv = vsrc is None
    has_sink = sinks is not None

    def kv_spec(col, i):
        return pl.BlockSpec((None, Q_BLOCK, LANES), lambda b, n: (b, jnp.maximum(nq * n + nq - 1 - i, 0), col))

    in_specs = [pl.BlockSpec((None, TQ, 2 * LANES), lambda b, n: (b, n, q_col))]
    args = [qsrc]
    in_specs += [kv_spec(k_col, i) for i in range(nk)]
    args += [ksrc] * nk
    if not shared_kv:
        in_specs += [kv_spec(v_col, i) for i in range(nk)]
        args += [vsrc] * nk
    in_specs.append(pl.BlockSpec(bias.shape, lambda b, n: (0, 0, 0, 0)))
    args.append(bias)
    if has_sink:
        in_specs.append(pl.BlockSpec(sinks.shape, lambda b, n: (0, 0)))
        args.append(sinks)
    return pl.pallas_call(
        functools.partial(_banded_kernel, nback=nback, window=window, q_cfg=q_cfg,
                          has_sink=has_sink, shared_kv=shared_kv),
        grid=(B, S // TQ),
        in_specs=in_specs,
        out_specs=pl.BlockSpec((None, TQ, 2 * LANES), lambda b, n: (b, n, 0)),
        out_shape=jax.ShapeDtypeStruct((B, S, 2 * LANES), BF16),
        compiler_params=_cparams("arbitrary", "arbitrary"),
        name="banded_attn",
    )(*args)


def _flash_kernel(*refs, mode, lambda_init):
    TB = FLASH_BLOCK
    if mode == "diff":
        (q_ref, k_ref, vt_ref, bias_ref, lam_ref, g_ref, o_ref,
         qx_ref, m_ref, l_ref, acc_ref) = refs
        sel_ref = None
        maps = [dict(g=h // 2, lo=(h % 2) * 64 + 32 * mm, w=DIFF_DIM, roll=False, kg=h // 2, vrow=HEAD_DIM * h, h=h)
                for h in range(DIFF_HEADS) for mm in range(2)]
        scale = DIFF_DIM ** -0.5
    else:
        (q_ref, k_ref, vt_ref, pen_ref, blk_ref, bias_ref, o_ref,
         qx_ref, m_ref, l_ref, acc_ref) = refs
        sel_ref = pen_ref
        maps = [dict(g=h // 2, lo=0, w=HEAD_DIM, roll=(h % 2 == 1), kg=0, vrow=HEAD_DIM, h=h)
                for h in range(NSA_HEADS)]
        scale = HEAD_DIM ** -0.5
    n = pl.program_id(1)

    for mp, c in enumerate(maps):
        qx_ref[mp, :, 0:LANES] = _lane_window(q_ref[:, LANES * c["g"]:LANES * (c["g"] + 1)].astype(F32),
                                              c["lo"], c["w"], c["roll"], scale * LOG2E)
        if sel_ref is not None:
            qx_ref[mp, :, LANES:2 * LANES] = pen_ref[...]
    m_ref[...] = jnp.full(m_ref.shape, NEG_INF, F32)
    l_ref[...] = jnp.zeros_like(l_ref)
    acc_ref[...] = jnp.zeros_like(acc_ref)

    key = lax.broadcasted_iota(I32, (TB, TB), 0)
    qry = lax.broadcasted_iota(I32, (TB, TB), 1)

    def step(j, kind, TK=TB):
        koff = pl.multiple_of(j * TB, TB)
        valid = None
        if kind == "diag":
            valid = qry >= key

        def logits(mp):
            c = maps[mp]
            kt = k_ref[pl.ds(koff, TK), LANES * c["kg"]:LANES * (c["kg"] + 1)]
            if sel_ref is not None:
                kt = jnp.concatenate([kt, blk_ref[pl.ds(koff, TK), :]], axis=1)
            s = lax.dot_general(kt, qx_ref[mp], _NT, preferred_element_type=F32)
            if kind == "near":
                s = s + bias_ref[c["h"], 1]
            elif kind == "diag":
                s = s + bias_ref[c["h"], 0]
            if valid is not None:
                s = jnp.where(valid, s, NEG_INF)
            return s

        def softmax(mp, s):
            m_prev = m_ref[mp]
            m_new = jnp.maximum(m_prev, jnp.max(s, axis=0, keepdims=True))
            alpha = jnp.exp2(m_prev - m_new)
            p = jnp.exp2(s - m_new)
            l_ref[mp] = alpha * l_ref[mp] + jnp.sum(p, axis=0, keepdims=True)
            m_ref[mp] = m_new
            return p.astype(BF16), alpha

        def values(mp, p, alpha):
            c = maps[mp]
            vt = vt_ref[c["vrow"]:c["vrow"] + HEAD_DIM, pl.ds(koff, TK)]
            acc_ref[mp] = alpha * acc_ref[mp] + jnp.dot(vt, p, preferred_element_type=F32)

        nm = len(maps)
        ss = [logits(mp) for mp in range(nm)]

        def two_pass():
            pa = [softmax(mp, ss[mp]) for mp in range(nm)]
            for mp in range(nm):
                values(mp, *pa[mp])

        if kind == "diag":
            two_pass()
            return

        stabs = [m_ref[mp] for mp in range(nm)]
        ps = [jnp.exp2(ss[mp] - stabs[mp]) for mp in range(nm)]
        m_news = [jnp.maximum(stabs[mp], jnp.max(ss[mp], axis=0, keepdims=True)) for mp in range(nm)]
        growth = m_news[0] - stabs[0]
        for mp in range(1, nm):
            growth = jnp.maximum(growth, m_news[mp] - stabs[mp])
        safe = jnp.max(growth) <= FLASH_MAX_GROWTH
        psums = [jnp.sum(ps[mp], axis=0, keepdims=True) for mp in range(nm)]
        pvs = [jnp.dot(vt_ref[maps[mp]["vrow"]:maps[mp]["vrow"] + HEAD_DIM, pl.ds(koff, TK)], ps[mp].astype(BF16),
                       preferred_element_type=F32) for mp in range(nm)]

        @pl.when(safe)
        def _():
            for mp in range(nm):
                alpha = jnp.exp2(stabs[mp] - m_news[mp])
                l_ref[mp] = (l_ref[mp] + psums[mp]) * alpha
                acc_ref[mp] = (acc_ref[mp] + pvs[mp]) * alpha
                m_ref[mp] = m_news[mp]

        @pl.when(jnp.logical_not(safe))
        def _():
            two_pass()

    step(n, "diag")

    @pl.when(n >= 1)
    def _():
        step(n - 1, "near")

    nfar = jnp.maximum(n - 1, 0)
    FW = FLASH_FAR_BLOCKS * (1 if mode == "diff" else 2)

    def far_body(j, carry):
        step(FW * j, "far", FW * TB)
        return carry

    lax.fori_loop(0, nfar // FW, far_body, 0)

    done = (nfar // FW) * FW
    rest = nfar - done
    size = FW // 2
    while size >= 1:
        def tail_step(size=size):
            step(done + rest - rest % (2 * size), "far", size * TB)

        pl.when(rest % (2 * size) >= size)(tail_step)
        size //= 2

    def normalised(mp):
        return acc_ref[mp] * (1.0 / jnp.maximum(l_ref[mp], TINY))

    if mode == "diff":
        lq = lam_ref[...]
        lam = (jnp.exp(jnp.sum(lq[0:1] * lq[1:2], axis=1, keepdims=True))
               - jnp.exp(jnp.sum(lq[2:3] * lq[3:4], axis=1, keepdims=True)) + lambda_init)
        for g in range(DIFF_HEADS // 2):
            res = []
            for h in (2 * g, 2 * g + 1):
                o = normalised(2 * h) - lam * normalised(2 * h + 1)
                ms = jnp.sum(o * o, axis=0, keepdims=True) * (1.0 / (2 * DIFF_DIM))
                res.append(o * lax.rsqrt(ms + DIFF_SUBLN_EPS))
            out = jnp.concatenate(res, axis=0) * g_ref[...] * (1.0 - lambda_init)
            o_ref[:, LANES * g:LANES * (g + 1)] = out.T.astype(o_ref.dtype)
    else:
        for g in range(NSA_HEADS // 2):
            out = jnp.concatenate([normalised(2 * g), normalised(2 * g + 1)], axis=0)
            o_ref[:, LANES * g:LANES * (g + 1)] = out.T.astype(o_ref.dtype)


def _flash_attention(mode, qsrc, q_col, ksrc, k_col, vt, vt_rows, vt_blk, bias, extra, *, B, S, lambda_init=0.0):
    TB = FLASH_BLOCK
    q_spec = pl.BlockSpec((None, TB, 2 * LANES), lambda b, n: (b, n, q_col))
    bias_spec = pl.BlockSpec(bias.shape, lambda b, n: (0, 0, 0, 0))
    vt_spec = pl.BlockSpec((vt_rows, S), lambda b, n: (vt_blk, b))
    if mode == "diff":
        lam4, gsub = extra
        nmaps = 2 * DIFF_HEADS
        in_specs = [q_spec,
                    pl.BlockSpec((None, S, 2 * LANES), lambda b, n: (b, 0, k_col)),
                    vt_spec, bias_spec,
                    pl.BlockSpec(lam4.shape, lambda b, n: (0, 0)),
                    pl.BlockSpec(gsub.shape, lambda b, n: (0, 0))]
        args = [qsrc, ksrc, vt, bias, lam4, gsub]
    else:
        (pen,) = extra
        nmaps = NSA_HEADS
        blk = jnp.asarray(np.arange(S)[:, None] // NSA_SLC_LEN == np.arange(LANES)[None, :], BF16)
        in_specs = [q_spec,
                    pl.BlockSpec((None, S, LANES), lambda b, n: (b, 0, k_col)),
                    vt_spec,
                    pl.BlockSpec((None, TB, LANES), lambda b, n: (b, n, 0)),
                    pl.BlockSpec((S, LANES), lambda b, n: (0, 0)),
                    bias_spec]
        args = [qsrc, ksrc, vt, pen, blk, bias]
    return pl.pallas_call(
        functools.partial(_flash_kernel, mode=mode, lambda_init=lambda_init),
        grid=(B, S // TB),
        in_specs=in_specs,
        out_specs=pl.BlockSpec((None, TB, 2 * LANES), lambda b, n: (b, n, 0)),
        out_shape=jax.ShapeDtypeStruct((B, S, 2 * LANES), BF16),
        scratch_shapes=[pltpu.VMEM((nmaps, TB, LANES if mode == "diff" else 2 * LANES), BF16),
                        pltpu.VMEM((nmaps, 1, TB), F32),
                        pltpu.VMEM((nmaps, 1, TB), F32),
                        pltpu.VMEM((nmaps, HEAD_DIM, TB), F32)],
        compiler_params=_cparams("arbitrary", "arbitrary"),
        name="flash_" + mode,
    )(*args)


def _compress_kernel(ck_ref, cv_ref, pk_ref, pv_ref, w1k_ref, w1v_ref, w2_ref, o_ref):
    half = NSA_CMP_STRIDE * HEAD_DIM
    ck = ck_ref[...]
    cv = cv_ref[...]
    nrow = ck.shape[0]

    def mm(a, w):
        return jnp.dot(a, w, preferred_element_type=F32)

    top = mm(ck, w1k_ref[0:half, :]) + mm(cv, w1v_ref[0:half, :])
    bot = mm(ck, w1k_ref[half:2 * half, :]) + mm(cv, w1v_ref[half:2 * half, :])
    posw = (mm(pk_ref[...], w1k_ref[...]) + mm(pv_ref[...], w1v_ref[...]))[0:1]
    pre = top + pltpu.roll(bot, nrow - 1, 0) + posw
    kcv = mm(jax.nn.gelu(pre).astype(BF16), w2_ref[...])
    o_ref[0:LANES, :] = jnp.zeros((LANES, LANES), o_ref.dtype)
    o_ref[LANES:LANES + nrow, :] = kcv.astype(o_ref.dtype)


def _nsa_compress(ck, cv, pk, pv, w1k, w1v, w2, ncp):
    B, nchunk, width = ck.shape
    assert ncp == LANES + nchunk

    def full(a):
        return pl.BlockSpec(a.shape, lambda b: (0,) * a.ndim)

    chunk = pl.BlockSpec((None, nchunk, width), lambda b: (b, 0, 0))
    return pl.pallas_call(
        _compress_kernel,
        grid=(B,),
        in_specs=[chunk, chunk, full(pk), full(pv), full(w1k), full(w1v), full(w2)],
        out_specs=pl.BlockSpec((None, ncp, LANES), lambda b: (b, 0, 0)),
        out_shape=jax.ShapeDtypeStruct((B, ncp, LANES), BF16),
        compiler_params=_cparams("arbitrary"),
        name="nsa_compress",
    )(ck, cv, pk, pv, w1k, w1v, w2)


def _cmp_kernel(q_ref, kcv_ref, bias_ref, agg_ref, oc_ref, sel_ref, s_scr, *, n_cmp, ncp):
    n = pl.program_id(1)
    QB = CMP_Q
    kcv = kcv_ref[...]
    cp = lax.broadcasted_iota(I32, (QB, ncp), 1)
    qpos = n * QB + lax.broadcasted_iota(I32, (QB, ncp), 0)
    valid = ((cp >= LANES) & (cp < LANES + n_cmp)
             & (qpos - NSA_CMP_STRIDE * (cp - LANES) - (NSA_CMP_LEN - 1) >= 0))
    woff = pl.multiple_of((n // (16 * LANES // CMP_Q)) * LANES, LANES)
    scale = HEAD_DIM ** -0.5

    def logits(h):
        g = h // 2
        qx = _lane_window(q_ref[:, LANES * g:LANES * (g + 1)].astype(F32), 0, HEAD_DIM, h % 2 == 1, scale)
        s_scr[h] = lax.dot_general(qx, kcv, _NT, preferred_element_type=F32)
        s_scr[h, :, pl.ds(woff, 2 * LANES)] += bias_ref[h]

    def softmax(h):
        s = jnp.where(valid, s_scr[h], NEG_INF)
        m = jnp.max(s, axis=1, keepdims=True)
        p = jnp.where(valid, jnp.exp(s - m), 0.0)
        den = jnp.sum(p, axis=1, keepdims=True)
        return p * (1.0 / jnp.maximum(den, TINY))

    for h in range(NSA_HEADS):
        logits(h)
    ps = [softmax(h) for h in range(NSA_HEADS)]
    outs = [jnp.dot(p.astype(BF16), kcv, preferred_element_type=F32) for p in ps]
    for g in range(NSA_HEADS // 2):
        oc_ref[:, LANES * g:LANES * (g + 1)] = _pair_halves(outs[2 * g], outs[2 * g + 1], 1, 1).astype(oc_ref.dtype)

    psum = (ps[0] + ps[1]) + (ps[2] + ps[3])
    imp = jnp.dot(psum.astype(BF16), agg_ref[...], preferred_element_type=F32)

    v = imp.T
    jb = lax.broadcasted_iota(I32, v.shape, 0)
    qp = n * QB + lax.broadcasted_iota(I32, v.shape, 1)
    cur = qp >> 6
    forced = (jb == 0) | (jb == cur) | (jb == cur - 1)
    v = jnp.where(forced, -jnp.inf, jnp.where(jb * NSA_SLC_LEN <= qp, v, -1.0))
    rid = jb.astype(F32)
    picked = jnp.where(forced, 1.0, 0.0)
    for _ in range(NSA_TOPK - 3):
        mx = jnp.max(v, axis=0, keepdims=True)
        first = jnp.min(jnp.where(v == mx, rid, float(LANES)), axis=0, keepdims=True)
        hit = rid == first
        picked = jnp.where(hit, 1.0, picked)
        v = jnp.where(hit, -jnp.inf, v)
    sel_ref[...] = jnp.where(picked.T > 0.5, 0.0, NEG_INF).astype(sel_ref.dtype)


def _nsa_cmp_select(dsrc, kcv, bias, agg, *, B, S, n_cmp, ncp):
    return pl.pallas_call(
        functools.partial(_cmp_kernel, n_cmp=n_cmp, ncp=ncp),
        grid=(B, S // CMP_Q),
        in_specs=[pl.BlockSpec((None, CMP_Q, 2 * LANES), lambda b, n: (b, n, 0)),
                  pl.BlockSpec((None, ncp, LANES), lambda b, n: (b, 0, 0)),
                  pl.BlockSpec((NSA_HEADS, None, CMP_Q, 2 * LANES), lambda b, n: (0, n % (16 * LANES // CMP_Q), 0, 0)),
                  pl.BlockSpec((ncp, LANES), lambda b, n: (0, 0))],
        out_specs=[pl.BlockSpec((None, CMP_Q, 2 * LANES), lambda b, n: (b, n, 0)),
                   pl.BlockSpec((None, CMP_Q, LANES), lambda b, n: (b, n, 0))],
        out_shape=[jax.ShapeDtypeStruct((B, S, 2 * LANES), BF16),
                   jax.ShapeDtypeStruct((B, S, LANES), BF16)],
        scratch_shapes=[pltpu.VMEM((NSA_HEADS, CMP_Q, ncp), F32)],
        compiler_params=_cparams("arbitrary", "arbitrary"),
        name="nsa_cmp_select",
    )(dsrc, kcv, bias, agg)


def _merge_kernel(x_ref, g_ref, ya_ref, yb_ref, yc_ref, oc_ref, os_ref, ow_ref, dg_ref,
                  wg_ref, bg_ref, wb_ref, wo_ref, o_ref):
    x = x_ref[...]
    h = _rms(x, g_ref[...]).astype(BF16)
    tm = x.shape[0]
    lane = lax.broadcasted_iota(I32, (tm, LANES), 1)
    gd = jax.nn.sigmoid(dg_ref[...].astype(F32))
    yd = []
    for g in range(NSA_HEADS // 2):
        acc = jnp.zeros((tm, LANES), F32)
        for br, src in enumerate((oc_ref, os_ref, ow_ref)):
            ge = gd[:, 3 * (2 * g) + br:3 * (2 * g) + br + 1]
            go = gd[:, 3 * (2 * g + 1) + br:3 * (2 * g + 1) + br + 1]
            acc = acc + jnp.where(lane < 64, ge, go) * src[:, LANES * g:LANES * (g + 1)].astype(F32)
        yd.append(acc)
    y_d = jnp.concatenate(yd, axis=1).astype(BF16)
    ys = (ya_ref[...], yb_ref[...], yc_ref[...], y_d)
    merged = jnp.zeros((tm, D_MODEL), F32)
    for br in range(4):
        gate = jax.nn.sigmoid(jnp.dot(h, wg_ref[br], preferred_element_type=F32) + bg_ref[br:br + 1, :])
        merged = merged + gate * jnp.dot(ys[br], wb_ref[br], preferred_element_type=F32)
    o_ref[...] = x + jnp.dot(merged.astype(BF16), wo_ref[...], preferred_element_type=F32)


def _merge(x, g, ya, yb, yc, oc, osel, ow, dproj, wg, bg, wb, wo):
    T = x.shape[0]
    tm = 256

    def tile(width, col=0):
        return pl.BlockSpec((tm, width), lambda i: (i, col))

    def full(a):
        return pl.BlockSpec(a.shape, lambda i: (0,) * a.ndim)

    return pl.pallas_call(
        _merge_kernel,
        grid=(T // tm,),
        in_specs=[tile(D_MODEL), full(g), tile(256), tile(256), tile(256), tile(256), tile(256), tile(256),
                  tile(LANES, 5), full(wg), full(bg), full(wb), full(wo)],
        out_specs=tile(D_MODEL),
        out_shape=jax.ShapeDtypeStruct((T, D_MODEL), F32),
        compiler_params=_cparams("arbitrary"),
        name="merge",
    )(x, g, ya, yb, yc, oc, osel, ow, dproj, wg, bg, wb, wo)


def _route(hn, w_split, b_router):
    lane = lax.broadcasted_iota(I32, (hn.shape[0], LANES), 1)
    hi = hn.astype(BF16)
    lo = (hn - hi.astype(F32)).astype(BF16)
    a = jnp.dot(hi, w_split, preferred_element_type=F32)
    b = jnp.dot(lo, w_split[:, 0:LANES], preferred_element_type=F32)
    logits = (a[:, 0:LANES] + a[:, LANES:2 * LANES]) + b + b_router
    is_grp = (lane >= N_EXPERTS) & (lane < N_EXPERTS + N_GROUPS)
    lanef = lane.astype(F32)
    lg = jnp.where(is_grp, logits, -jnp.inf)
    mg = jnp.max(lg, axis=1, keepdims=True)
    gsel = jnp.min(jnp.where(lg == mg, lanef, 2.0 * LANES), axis=1, keepdims=True) - N_EXPERTS
    g_w = 1.0 / jnp.sum(jnp.where(is_grp, jnp.exp(lg - mg), 0.0), axis=1, keepdims=True)
    in_grp = (lane < N_EXPERTS) & ((lane >> 3).astype(F32) == gsel)
    v1 = jnp.where(in_grp, logits, -jnp.inf)
    m1 = jnp.max(v1, axis=1, keepdims=True)
    i1 = jnp.min(jnp.where(v1 == m1, lanef, 2.0 * LANES), axis=1, keepdims=True)
    v2 = jnp.where(lanef == i1, -jnp.inf, v1)
    m2 = jnp.max(v2, axis=1, keepdims=True)
    i2 = jnp.min(jnp.where(v2 == m2, lanef, 2.0 * LANES), axis=1, keepdims=True)
    e2 = jnp.exp(m2 - m1)
    w1 = g_w / (1.0 + e2)
    w2 = g_w * e2 / (1.0 + e2)
    first, second = lanef == i1, lanef == i2
    comb = jnp.where(first, w1, 0.0) + jnp.where(second, w2, 0.0)
    return comb, jnp.where(first | second, 1.0, 0.0)


def _moe_kernel(x_ref, g_ref, wr_ref, br_ref, wg_ref, wu_ref, wd_ref, fg_ref, o_ref,
                hn_ref, comb_ref, acc_ref, *, final_norm):
    e = pl.program_id(1)
    tm = x_ref.shape[0]
    lane = lax.broadcasted_iota(I32, (tm, LANES), 1)

    @pl.when(e == 0)
    def _():
        hn = _rms(x_ref[...], g_ref[...])
        hn_ref[...] = hn.astype(BF16)
        comb_ref[...] = _route(hn, wr_ref[...], br_ref[...])[0]
        acc_ref[...] = jnp.zeros_like(acc_ref)

    hn = hn_ref[...]
    gt = jnp.dot(hn, wg_ref[...], preferred_element_type=F32)
    up = jnp.dot(hn, wu_ref[...], preferred_element_type=F32)
    he = (gt * jax.nn.sigmoid(gt) * up).astype(BF16)
    ye = jnp.dot(he, wd_ref[...], preferred_element_type=F32)
    cw = jnp.sum(jnp.where(lane == e, comb_ref[...], 0.0), axis=1, keepdims=True)
    acc_ref[...] += cw * ye

    @pl.when(e == N_EXPERTS - 1)
    def _():
        y = x_ref[...] + acc_ref[...]
        if final_norm:
            y = _rms(y, fg_ref[...])
        o_ref[...] = y


def _moe(x, g, w_router, b_router, wg, wu, wd, final_g, final_norm):
    T = x.shape[0]
    tm = 1024
    return pl.pallas_call(
        functools.partial(_moe_kernel, final_norm=final_norm),
        grid=(T // tm, N_EXPERTS),
        in_specs=[pl.BlockSpec((tm, D_MODEL), lambda i, e: (i, 0)),
                  pl.BlockSpec((1, D_MODEL), lambda i, e: (0, 0)),
                  pl.BlockSpec((D_MODEL, 2 * LANES), lambda i, e: (0, 0)),
                  pl.BlockSpec((1, LANES), lambda i, e: (0, 0)),
                  pl.BlockSpec((None, D_MODEL, EXPERT_FF), lambda i, e: (e, 0, 0)),
                  pl.BlockSpec((None, D_MODEL, EXPERT_FF), lambda i, e: (e, 0, 0)),
                  pl.BlockSpec((None, EXPERT_FF, D_MODEL), lambda i, e: (e, 0, 0)),
                  pl.BlockSpec((1, D_MODEL), lambda i, e: (0, 0))],
        out_specs=pl.BlockSpec((tm, D_MODEL), lambda i, e: (i, 0)),
        out_shape=jax.ShapeDtypeStruct((T, D_MODEL), F32),
        scratch_shapes=[pltpu.VMEM((tm, D_MODEL), BF16),
                        pltpu.VMEM((tm, LANES), F32),
                        pltpu.VMEM((tm, D_MODEL), F32)],
        compiler_params=_cparams("arbitrary", "arbitrary"),
        name="moe",
    )(x, g, w_router, b_router, wg, wu, wd, final_g)


def _slot_matrix(pos_t, val_t, ntok, half):
    nslot = MOE_CAP // 2
    slot = (lax.broadcasted_iota(I32, (nslot, ntok), 0) + half * nslot).astype(F32)
    blocks = []
    for e in range(N_EXPERTS):
        hit = pos_t[e:e + 1, :] == slot
        blocks.append(jnp.where(hit, 1.0 if val_t is None else val_t[e:e + 1, :], 0.0).astype(BF16))
    return jnp.concatenate(blocks, axis=0)


ROUTE_ROWS = 48


def _route_t(hn, w_split_t, b_col):
    nr = ROUTE_ROWS
    hi = hn.astype(BF16)
    lo = (hn - hi.astype(F32)).astype(BF16)
    a = lax.dot_general(w_split_t, hi, _NT, preferred_element_type=F32)
    b = lax.dot_general(w_split_t[0:LANES], lo, _NT, preferred_element_type=F32)
    logits = (a[0:nr] + a[LANES:LANES + nr]) + b[0:nr] + b_col[0:nr]
    row = lax.broadcasted_iota(I32, logits.shape, 0)
    rowf = row.astype(F32)
    is_grp = (row >= N_EXPERTS) & (row < N_EXPERTS + N_GROUPS)
    lg = jnp.where(is_grp, logits, -jnp.inf)
    mg = jnp.max(lg, axis=0, keepdims=True)
    gsel = jnp.min(jnp.where(lg == mg, rowf, 2.0 * LANES), axis=0, keepdims=True) - N_EXPERTS
    g_w = 1.0 / jnp.sum(jnp.where(is_grp, jnp.exp(lg - mg), 0.0), axis=0, keepdims=True)
    in_grp = (row < N_EXPERTS) & ((row >> 3).astype(F32) == gsel)
    v1 = jnp.where(in_grp, logits, -jnp.inf)
    m1 = jnp.max(v1, axis=0, keepdims=True)
    i1 = jnp.min(jnp.where(v1 == m1, rowf, 2.0 * LANES), axis=0, keepdims=True)
    v2 = jnp.where(rowf == i1, -jnp.inf, v1)
    m2 = jnp.max(v2, axis=0, keepdims=True)
    i2 = jnp.min(jnp.where(v2 == m2, rowf, 2.0 * LANES), axis=0, keepdims=True)
    e2 = jnp.exp(m2 - m1)
    w1 = g_w / (1.0 + e2)
    w2 = g_w * e2 / (1.0 + e2)
    first, second = rowf == i1, rowf == i2
    comb = jnp.where(first, w1, 0.0) + jnp.where(second, w2, 0.0)
    return comb, jnp.where(first | second, 1.0, 0.0)


def _dispatch_kernel(x_ref, g_ref, wr_ref, br_ref, tri_ref, xd_ref, pos_ref, wt_ref, cnt_ref):
    tm = x_ref.shape[0]
    nr = ROUTE_ROWS
    hn = _rms(x_ref[...], g_ref[...])
    comb_t, routed_t = _route_t(hn, wr_ref[...], br_ref[...])
    rank_t = jnp.dot(routed_t.astype(BF16), tri_ref[...], preferred_element_type=F32)
    pos_t = jnp.where(routed_t > 0.0, rank_t, -1.0)
    pos_ref[0:nr, :] = pos_t
    pos_ref[nr:LANES, :] = jnp.full((LANES - nr, tm), -1.0, F32)
    wt_ref[0:nr, :] = comb_t
    wt_ref[nr:LANES, :] = jnp.zeros((LANES - nr, tm), F32)
    counts = jnp.sum(routed_t, axis=1, keepdims=True)
    cnt_ref[0:nr, :] = jnp.broadcast_to(counts, (nr, LANES))
    cnt_ref[nr:LANES, :] = jnp.zeros((LANES - nr, LANES), F32)
    hnb = hn.astype(BF16)
    nslot = MOE_CAP // 2
    lower = jnp.dot(_slot_matrix(pos_t, None, tm, 0), hnb, preferred_element_type=F32)
    xd_ref[:, 0:nslot, :] = lower.reshape(N_EXPERTS, nslot, D_MODEL).astype(xd_ref.dtype)

    busy = jnp.max(counts) > nslot

    @pl.when(busy)
    def _():
        upper = jnp.dot(_slot_matrix(pos_t, None, tm, 1), hnb, preferred_element_type=F32)
        xd_ref[:, nslot:MOE_CAP, :] = upper.reshape(N_EXPERTS, nslot, D_MODEL).astype(xd_ref.dtype)

    @pl.when(jnp.logical_not(busy))
    def _():
        xd_ref[:, nslot:MOE_CAP, :] = jnp.zeros((N_EXPERTS, nslot, D_MODEL), xd_ref.dtype)


def _expert_kernel(busy_ref, xlo_ref, xhi_ref, wg_ref, wu_ref, wd_ref, olo_ref, ohi_ref):
    ntile, nslot, _ = xlo_ref.shape

    def ffn(x_ref, o_ref):
        x = x_ref[...].reshape(ntile * nslot, D_MODEL)
        gt = jnp.dot(x, wg_ref[...], preferred_element_type=F32)
        up = jnp.dot(x, wu_ref[...], preferred_element_type=F32)
        he = (gt * jax.nn.sigmoid(gt) * up).astype(BF16)
        ye = jnp.dot(he, wd_ref[...], preferred_element_type=F32)
        o_ref[...] = ye.reshape(ntile, nslot, D_MODEL).astype(o_ref.dtype)

    ffn(xlo_ref, olo_ref)
    busy = busy_ref[pl.program_id(0), pl.program_id(1)] > 0

    @pl.when(busy)
    def _():
        ffn(xhi_ref, ohi_ref)

    @pl.when(jnp.logical_not(busy))
    def _():
        ohi_ref[...] = jnp.zeros(ohi_ref.shape, ohi_ref.dtype)


def _combine_kernel(busy_ref, x_ref, ylo_ref, yhi_ref, pos_ref, wt_ref, fg_ref, o_ref, *, final_norm):
    tm = x_ref.shape[0]
    nslot = MOE_CAP // 2

    def gathered(y_ref, half):
        weights = _slot_matrix(pos_ref[...], wt_ref[...], tm, half)
        yd = y_ref[...].reshape(N_EXPERTS * nslot, D_MODEL)
        return lax.dot_general(weights, yd, (((0,), (0,)), ((), ())), preferred_element_type=F32)

    o_ref[...] = x_ref[...] + gathered(ylo_ref, 0)

    @pl.when(busy_ref[pl.program_id(0)] > 0)
    def _():
        o_ref[...] += gathered(yhi_ref, 1)

    if final_norm:
        o_ref[...] = _rms(o_ref[...], fg_ref[...])


def _moe_dispatch(x, g, w_router, b_router):
    T = x.shape[0]
    tm = MOE_TILE
    nt = T // tm
    tri = jnp.asarray(np.triu(np.ones((tm, tm), np.float32), 1), BF16)
    xd, pos_t, w_t, cnt = pl.pallas_call(
        _dispatch_kernel,
        grid=(nt,),
        in_specs=[pl.BlockSpec((tm, D_MODEL), lambda i: (i, 0)),
                  pl.BlockSpec((1, D_MODEL), lambda i: (0, 0)),
                  pl.BlockSpec((2 * LANES, D_MODEL), lambda i: (0, 0)),
                  pl.BlockSpec((LANES, 1), lambda i: (0, 0)),
                  pl.BlockSpec((tm, tm), lambda i: (0, 0))],
        out_specs=[pl.BlockSpec((N_EXPERTS, None, MOE_CAP, D_MODEL), lambda i: (0, i, 0, 0)),
                   pl.BlockSpec((LANES, tm), lambda i: (0, i)),
                   pl.BlockSpec((LANES, tm), lambda i: (0, i)),
                   pl.BlockSpec((None, LANES, LANES), lambda i: (i, 0, 0))],
        out_shape=[jax.ShapeDtypeStruct((N_EXPERTS, nt, MOE_CAP, D_MODEL), BF16),
                   jax.ShapeDtypeStruct((LANES, T), F32),
                   jax.ShapeDtypeStruct((LANES, T), F32),
                   jax.ShapeDtypeStruct((nt, LANES, LANES), F32)],
        compiler_params=_cparams("arbitrary"),
        name="moe_dispatch",
    )(x, g, w_router.T, b_router.reshape(LANES, 1), tri)
    return xd, pos_t, w_t, cnt[:, 0:N_EXPERTS, 0]


def _moe_finish(x, xd, pos_t, w_t, counts, wg, wu, wd, final_g, final_norm):
    T = x.shape[0]
    tm = MOE_TILE
    nt = T // tm
    tps = min(MOE_TILES_PER_STEP, nt)
    assert nt % tps == 0
    nslot = MOE_CAP // 2
    hot = counts > nslot
    busy_blocks = jnp.any(hot.reshape(nt // tps, tps, N_EXPERTS), axis=1).T.astype(I32)
    busy_tiles = jnp.any(hot, axis=1).astype(I32)
    halves = xd.reshape(N_EXPERTS, nt, 2, nslot, D_MODEL)
    half_out = jax.ShapeDtypeStruct((N_EXPERTS, nt, nslot, D_MODEL), BF16)
    yd_lo, yd_hi = pl.pallas_call(
        _expert_kernel,
        grid_spec=pltpu.PrefetchScalarGridSpec(
            num_scalar_prefetch=1,
            grid=(N_EXPERTS, nt // tps),
            in_specs=[pl.BlockSpec((None, tps, None, nslot, D_MODEL), lambda e, c, b: (e, c, 0, 0, 0)),
                      pl.BlockSpec((None, tps, None, nslot, D_MODEL),
                                   lambda e, c, b: (e, jnp.where(b[e, c] > 0, c, 0), 1, 0, 0)),
                      pl.BlockSpec((None, D_MODEL, EXPERT_FF), lambda e, c, b: (e, 0, 0)),
                      pl.BlockSpec((None, D_MODEL, EXPERT_FF), lambda e, c, b: (e, 0, 0)),
                      pl.BlockSpec((None, EXPERT_FF, D_MODEL), lambda e, c, b: (e, 0, 0))],
            out_specs=[pl.BlockSpec((None, tps, nslot, D_MODEL), lambda e, c, b: (e, c, 0, 0)),
                       pl.BlockSpec((None, tps, nslot, D_MODEL), lambda e, c, b: (e, c, 0, 0))]),
        out_shape=[half_out, half_out],
        compiler_params=_cparams("arbitrary", "arbitrary"),
        name="moe_experts",
    )(busy_blocks, halves, halves, wg, wu, wd)
    return pl.pallas_call(
        functools.partial(_combine_kernel, final_norm=final_norm),
        grid_spec=pltpu.PrefetchScalarGridSpec(
            num_scalar_prefetch=1,
            grid=(nt,),
            in_specs=[pl.BlockSpec((tm, D_MODEL), lambda i, b: (i, 0)),
                      pl.BlockSpec((N_EXPERTS, None, nslot, D_MODEL), lambda i, b: (0, i, 0, 0)),
                      pl.BlockSpec((N_EXPERTS, None, nslot, D_MODEL),
                                   lambda i, b: (0, jnp.where(b[i] > 0, i, 0), 0, 0)),
                      pl.BlockSpec((LANES, tm), lambda i, b: (0, i)),
                      pl.BlockSpec((LANES, tm), lambda i, b: (0, i)),
                      pl.BlockSpec((1, D_MODEL), lambda i, b: (0, 0))],
            out_specs=pl.BlockSpec((tm, D_MODEL), lambda i, b: (i, 0))),
        out_shape=jax.ShapeDtypeStruct((T, D_MODEL), F32),
        compiler_params=_cparams("arbitrary"),
        name="moe_combine",
    )(busy_tiles, x, yd_lo, yd_hi, pos_t, w_t, final_g)


def _block_diag(w):
    nb, bs, _ = w.shape
    out = jnp.zeros((nb * bs, nb * bs), w.dtype)
    for i in range(nb):
        out = out.at[i * bs:(i + 1) * bs, i * bs:(i + 1) * bs].set(w[i])
    return out


def _selection_aggregator_np(n_cmp, n_slc):
    ratio_s = NSA_SLC_LEN // NSA_CMP_STRIDE
    ratio_c = NSA_CMP_LEN // NSA_CMP_STRIDE
    w = np.convolve(np.ones(ratio_s), np.ones(ratio_c)).astype(np.float32)
    cj = np.arange(n_slc)[:, None] * ratio_s - (ratio_c - 1) + np.arange(w.size)[None, :]
    jj = np.broadcast_to(np.arange(n_slc)[:, None], cj.shape)
    ww = np.broadcast_to(w[None, :], cj.shape)
    keep = (cj >= 0) & (cj < n_cmp)
    agg = np.zeros((n_cmp, n_slc), np.float32)
    np.add.at(agg, (cj[keep], jj[keep]), ww[keep])
    return agg


def _row(v):
    return v.reshape(1, -1).astype(F32)


def _diff_lambda_init(layer):
    return 0.8 - 0.6 * math.exp(-0.3 * layer)


def _mixer_layer(x, l, B, S, tiles, p):
    (bias_swa, bias_diff, bias_slc, bias_win, bias_cmp, agg_pad, n_cmp, ncp) = tiles
    w_in = jnp.pad(p["w_in"][l], ((0, 0), (0, N_IN_PAD - p["w_in"].shape[2]))).astype(BF16)
    w_t = jnp.concatenate([w_in[:, 1536:1792], w_in[:, 2176:2304]], axis=1).T
    a_xg, b_qkv, c_qkv, d_all, vals_t = _in_proj(x, _row(p["mix_norm_g"][l]), w_in, w_t)

    y_a = _rglru(a_xg, p["conv_w"][l].astype(F32), _row(p["conv_b"][l]),
                 _block_diag(p["lru_w_a"][l]).astype(BF16), _row(p["lru_b_a"][l]),
                 _block_diag(p["lru_w_x"][l]).astype(BF16), _row(p["lru_b_x"][l]),
                 _row(p["lru_lambda"][l]), B, S)

    b3 = b_qkv.reshape(B, S, 512)
    swa_cfg = tuple(((h // 2) * 64, (h % 2) != (h // 2), h // 2) for h in range(SWA_HEADS))
    y_b = _banded_attention(b3, 0, b3, 2, b3, 3, bias_swa, _row(p["swa_sinks"][l]),
                            B=B, S=S, window=SWA_WINDOW, q_cfg=swa_cfg)

    c3 = c_qkv.reshape(B, S, 768)
    lam4 = jnp.stack([p["diff_lq1"][l], p["diff_lk1"][l], p["diff_lq2"][l], p["diff_lk2"][l]]).astype(F32)
    gsub = jnp.tile(p["diff_subln_g"][l].reshape(-1, 1).astype(F32), (2, 1))
    y_c = _flash_attention("diff", c3, 0, c3, 1, vals_t, 256, 0, bias_diff, (lam4, gsub), B=B, S=S,
                           lambda_init=_diff_lambda_init(l))

    d3 = d_all.reshape(B, S, 768)
    nchunk = S // NSA_CMP_STRIDE
    ck = d3[:, :, 256:320].reshape(B, nchunk, NSA_CMP_STRIDE * HEAD_DIM)
    cv = d3[:, :, 320:384].reshape(B, nchunk, NSA_CMP_STRIDE * HEAD_DIM)

    def pos8(a):
        return jnp.broadcast_to(a.reshape(1, -1), (8, a.size)).astype(BF16)

    w1k = jnp.pad(p["nsa_w1_k"][l], ((0, 0), (0, HEAD_DIM))).astype(BF16)
    w1v = jnp.pad(p["nsa_w1_v"][l], ((0, 0), (HEAD_DIM, 0))).astype(BF16)
    w2 = _block_diag(jnp.stack([p["nsa_w2_k"][l], p["nsa_w2_v"][l]])).astype(BF16)
    kcv = _nsa_compress(ck, cv, pos8(p["nsa_pos_k"][l]), pos8(p["nsa_pos_v"][l]), w1k, w1v, w2, ncp)
    o_c, sel = _nsa_cmp_select(d3, kcv, bias_cmp, agg_pad, B=B, S=S, n_cmp=n_cmp, ncp=ncp)
    o_s = _flash_attention("nsa", d3, 0, d3, 3, vals_t, 128, 2, bias_slc, (sel,), B=B, S=S)
    win_cfg = tuple((0, h % 2 == 1, 1) for h in range(NSA_HEADS))
    o_w = _banded_attention(d3, 0, d3, 4, None, None, bias_win, None, B=B, S=S, window=NSA_WINDOW, q_cfg=win_cfg)

    T = B * S
    return _merge(x, _row(p["mix_norm_g"][l]), y_a, y_b.reshape(T, 256), y_c.reshape(T, 256),
                  o_c.reshape(T, 256), o_s.reshape(T, 256), o_w.reshape(T, 256), d_all,
                  p["w_gate"][l].astype(BF16), p["b_gate"][l].astype(F32), p["w_branch"][l].astype(BF16),
                  p["w_out"][l].astype(BF16))


def _moe_layer(x, l, p, final_norm):
    w_router = jnp.concatenate([p["w_router_exp"][l], p["w_router_grp"][l]], axis=1)
    w_router = jnp.pad(w_router, ((0, 0), (0, LANES - w_router.shape[1]))).astype(F32)
    w_hi = w_router.astype(BF16)
    w_router = jnp.concatenate([w_hi, (w_router - w_hi.astype(F32)).astype(BF16)], axis=1)
    b_router = jnp.concatenate([p["b_router_exp"][l], p["b_router_grp"][l]])
    b_router = jnp.pad(b_router, (0, LANES - b_router.shape[0])).reshape(1, LANES).astype(F32)
    g = _row(p["ffn_norm_g"][l])
    fg = _row(p["final_norm_g"])
    wg, wu, wd = (p[k][l].astype(BF16) for k in ("w_exp_gate", "w_exp_up", "w_exp_down"))
    xd, pos_t, w_t, counts = _moe_dispatch(x, g, w_router, b_router)
    return lax.cond(jnp.max(counts) > MOE_CAP,
                    lambda: _moe(x, g, w_router, b_router, wg, wu, wd, fg, final_norm),
                    lambda: _moe_finish(x, xd, pos_t, w_t, counts, wg, wu, wd, fg, final_norm))


def _forward(p):
    x = p["x"]
    B, S, D = x.shape
    depth = p["w_in"].shape[0]
    assert D == D_MODEL and S % (16 * Q_BLOCK) == 0 and S // NSA_SLC_LEN <= LANES
    rel = p["rel_bias"].astype(F32)
    n_cmp = (S - NSA_CMP_LEN) // NSA_CMP_STRIDE + 1
    ncp = LANES + S // NSA_CMP_STRIDE
    agg = _selection_aggregator_np(n_cmp, S // NSA_SLC_LEN)
    agg_pad = np.zeros((ncp, LANES), np.float32)
    agg_pad[LANES:LANES + n_cmp, :agg.shape[1]] = agg
    tiles = (
        _bias_tiles(rel, _banded_buckets(1), 0, SWA_HEADS, False),
        _bias_tiles(rel, _flash_buckets(), SWA_HEADS, DIFF_HEADS, True, LOG2E),
        _bias_tiles(rel, _flash_buckets(), SWA_HEADS + DIFF_HEADS, NSA_HEADS, True, LOG2E),
        _bias_tiles(rel, _banded_buckets(NSA_WINDOW // Q_BLOCK), SWA_HEADS + DIFF_HEADS, NSA_HEADS, False),
        _bias_tiles(rel, _cmp_buckets(), SWA_HEADS + DIFF_HEADS, NSA_HEADS, True),
        jnp.asarray(agg_pad, BF16), n_cmp, ncp,
    )
    xt = x.reshape(B * S, D).astype(F32)
    for l in range(depth):
        xt = _mixer_layer(xt, l, B, S, tiles, p)
        xt = _moe_layer(xt, l, p, final_norm=(l == depth - 1))
    return xt.reshape(B, S, D).astype(x.dtype)


def kernel(x, rel_bias, final_norm_g, mix_norm_g, w_in, conv_w, conv_b, lru_w_a, lru_b_a, lru_w_x, lru_b_x,
           lru_lambda, swa_sinks, diff_lq1, diff_lk1, diff_lq2, diff_lk2, diff_subln_g, nsa_pos_k, nsa_w1_k,
           nsa_w2_k, nsa_pos_v, nsa_w1_v, nsa_w2_v, w_gate, b_gate, w_branch, w_out, ffn_norm_g, w_router_grp,
           b_router_grp, w_router_exp, b_router_exp, w_exp_gate, w_exp_up, w_exp_down):
    return _forward(dict(
        x=x, rel_bias=rel_bias, final_norm_g=final_norm_g, mix_norm_g=mix_norm_g, w_in=w_in, conv_w=conv_w,
        conv_b=conv_b, lru_w_a=lru_w_a, lru_b_a=lru_b_a, lru_w_x=lru_w_x, lru_b_x=lru_b_x, lru_lambda=lru_lambda,
        swa_sinks=swa_sinks, diff_lq1=diff_lq1, diff_lk1=diff_lk1, diff_lq2=diff_lq2, diff_lk2=diff_lk2,
        diff_subln_g=diff_subln_g, nsa_pos_k=nsa_pos_k, nsa_w1_k=nsa_w1_k, nsa_w2_k=nsa_w2_k, nsa_pos_v=nsa_pos_v,
        nsa_w1_v=nsa_w1_v, nsa_w2_v=nsa_w2_v, w_gate=w_gate, b_gate=b_gate, w_branch=w_branch, w_out=w_out,
        ffn_norm_g=ffn_norm_g, w_router_grp=w_router_grp, b_router_grp=b_router_grp, w_router_exp=w_router_exp,
        b_router_exp=b_router_exp, w_exp_gate=w_exp_gate, w_exp_up=w_exp_up, w_exp_down=w_exp_down))
```

```python
import functools
import math

import numpy as np
import jax
import jax.numpy as jnp
from jax import lax
from jax.experimental import pallas as pl
from jax.experimental.pallas import tpu as pltpu

F32 = jnp.float32
BF16 = jnp.bfloat16
I32 = jnp.int32

D_MODEL = 1024
HEAD_DIM = 64
MIX_WIDTH = 256
LRU_C = 8.0
CONV_WIDTH = 4
SWA_HEADS = 4
SWA_WINDOW = 128
DIFF_HEADS = 4
DIFF_DIM = 32
DIFF_SUBLN_EPS = 1e-5
NSA_HEADS = 4
NSA_CMP_LEN = 32
NSA_CMP_STRIDE = 16
NSA_SLC_LEN = 64
NSA_TOPK = 16
NSA_WINDOW = 512
NSA_FORCED_SCORE = 1e4
REL_BUCKETS = 32
REL_MAX_DIST = 128
N_GROUPS = 4
EXPERTS_PER_GROUP = 8
N_EXPERTS = 32
EXPERT_FF = 256
NORM_EPS = 1e-6
NEG_INF = -1e30
TINY = 1e-30
LOG2E = math.log2(math.e)

LANES = 128
Q_BLOCK = 128
CMP_Q = 512
BANDED_Q_BLOCKS = 4
FLASH_BLOCK = 256
MOE_TILE = 512
MOE_CAP = 128
MOE_TILES_PER_STEP = 8
FLASH_MAX_GROWTH = 64.0
FLASH_FAR_BLOCKS = 8
N_IN_PAD = 2560
VMEM_LIMIT = 56 * 1024 * 1024

_NT = (((1,), (1,)), ((), ()))


def _cparams(*sem):
    return pltpu.CompilerParams(dimension_semantics=sem, vmem_limit_bytes=VMEM_LIMIT)


def _t5_bucket_np(dist):
    n = np.maximum(dist, 0)
    exact = REL_BUCKETS // 2
    scaled = (np.log(np.maximum(n, 1).astype(np.float32) / np.float32(exact))
              / np.float32(math.log(REL_MAX_DIST / exact))).astype(np.float32)
    large = np.minimum(exact + (scaled * np.float32(REL_BUCKETS - exact)).astype(np.int32), REL_BUCKETS - 1)
    return np.where(n < exact, n, large).astype(np.int32)


def _bias_tile_kernel(tab_ref, idx_ref, o_ref, *, head0, shift, scale):
    h = pl.program_id(0) + head0
    idx = idx_ref[0]
    acc = jnp.zeros(idx.shape, F32)
    for b in range(REL_BUCKETS):
        acc = jnp.where(idx == b, tab_ref[b, h], acc)
    if shift:
        acc = acc - tab_ref[REL_BUCKETS - 1, h]
    if scale != 1.0:
        acc = acc * scale
    o_ref[0, 0] = acc


def _bias_tiles(rel_bias, buckets, head0, nheads, shift, scale=1.0):
    R, M, N = buckets.shape
    return pl.pallas_call(
        functools.partial(_bias_tile_kernel, head0=head0, shift=shift, scale=scale),
        grid=(nheads, R),
        in_specs=[pl.BlockSpec(memory_space=pltpu.SMEM),
                  pl.BlockSpec((1, M, N), lambda h, r: (r, 0, 0))],
        out_specs=pl.BlockSpec((1, 1, M, N), lambda h, r: (h, r, 0, 0)),
        out_shape=jax.ShapeDtypeStruct((nheads, R, M, N), F32),
        compiler_params=_cparams("arbitrary", "arbitrary"),
        name="bias_tiles",
    )(rel_bias, jnp.asarray(buckets))


def _banded_buckets(nback):
    i = np.arange(BANDED_Q_BLOCKS * Q_BLOCK)[None, :]
    j = np.arange((nback + BANDED_Q_BLOCKS) * Q_BLOCK)[:, None]
    return _t5_bucket_np(i - j + nback * Q_BLOCK)[None]


def _flash_buckets():
    k = np.arange(FLASH_BLOCK)[:, None]
    q = np.arange(FLASH_BLOCK)[None, :]
    return np.stack([_t5_bucket_np(q - k), _t5_bucket_np(q - k + FLASH_BLOCK)])


def _cmp_buckets():
    i = np.arange(CMP_Q)[None, :, None]
    c2 = np.arange(2 * LANES)[None, None, :]
    r = np.arange(16 * LANES // CMP_Q)[:, None, None]
    dist = i - NSA_CMP_STRIDE * (c2 - LANES - (CMP_Q // NSA_CMP_STRIDE) * r) - (NSA_CMP_LEN - 1)
    return _t5_bucket_np(dist)


def _rms(x, g):
    return x * lax.rsqrt(jnp.mean(x * x, axis=-1, keepdims=True) + NORM_EPS) * g


def _proj_kernel(x_ref, g_ref, w_ref, wt_ref, oa_ref, ob_ref, oc_ref, od_ref, ot_ref):
    h = _rms(x_ref[...], g_ref[...]).astype(BF16)
    oa_ref[...] = jnp.dot(h, w_ref[:, 0:512], preferred_element_type=F32)
    ob_ref[...] = jnp.dot(h, w_ref[:, 512:1024], preferred_element_type=F32).astype(BF16)
    oc_ref[...] = jnp.dot(h, w_ref[:, 1024:1792], preferred_element_type=F32).astype(BF16)
    od_ref[...] = jnp.dot(h, w_ref[:, 1792:2560], preferred_element_type=F32).astype(BF16)
    ot_ref[...] = lax.dot_general(wt_ref[...], h, _NT, preferred_element_type=F32).astype(BF16)


def _in_proj(x, g, w, wt):
    T = x.shape[0]
    tm = 512
    nt = wt.shape[0]
    return pl.pallas_call(
        _proj_kernel,
        grid=(T // tm,),
        in_specs=[pl.BlockSpec((tm, D_MODEL), lambda i: (i, 0)),
                  pl.BlockSpec((1, D_MODEL), lambda i: (0, 0)),
                  pl.BlockSpec((D_MODEL, N_IN_PAD), lambda i: (0, 0)),
                  pl.BlockSpec((nt, D_MODEL), lambda i: (0, 0))],
        out_specs=[pl.BlockSpec((tm, 512), lambda i: (i, 0)),
                   pl.BlockSpec((tm, 512), lambda i: (i, 0)),
                   pl.BlockSpec((tm, 768), lambda i: (i, 0)),
                   pl.BlockSpec((tm, 768), lambda i: (i, 0)),
                   pl.BlockSpec((nt, tm), lambda i: (0, i))],
        out_shape=[jax.ShapeDtypeStruct((T, 512), F32),
                   jax.ShapeDtypeStruct((T, 512), BF16),
                   jax.ShapeDtypeStruct((T, 768), BF16),
                   jax.ShapeDtypeStruct((T, 768), BF16),
                   jax.ShapeDtypeStruct((nt, T), BF16)],
        compiler_params=_cparams("arbitrary"),
        name="in_proj",
    )(x, g, w, wt)


def _lru_kernel(xg_ref, cw_ref, cb_ref, wa_ref, ba_ref, wx_ref, bx_ref, lam_ref, o_ref,
                ext_ref, h_ref, *, Tt):
    t = pl.program_id(1)

    W = MIX_WIDTH

    @pl.when(t == 0)
    def _():
        ext_ref[0:8, :] = jnp.zeros((8, W), F32)
        h_ref[...] = jnp.zeros_like(h_ref)

    x = xg_ref[:, 0:W]
    gate = xg_ref[:, W:2 * W]
    ext_ref[8:Tt + 8, :] = x
    row = lax.broadcasted_iota(I32, (Tt, W), 0)
    xc = cb_ref[...] + x * cw_ref[CONV_WIDTH - 1:CONV_WIDTH, :]
    for s in range(1, CONV_WIDTH):
        xc = xc + ext_ref[8 - s:8 - s + Tt, :] * cw_ref[CONV_WIDTH - 1 - s:CONV_WIDTH - s, :]
    ext_ref[0:8, :] = ext_ref[Tt:Tt + 8, :]

    xcb = xc.astype(BF16)
    r = jax.nn.sigmoid(jnp.dot(xcb, wa_ref[...], preferred_element_type=F32) + ba_ref[...])
    ig = jax.nn.sigmoid(jnp.dot(xcb, wx_ref[...], preferred_element_type=F32) + bx_ref[...])
    z = -lam_ref[...]
    softplus = jnp.maximum(z, 0.0) + jnp.log1p(jnp.exp(-jnp.abs(z)))
    log_a = (-LRU_C) * r * softplus
    a = jnp.exp(log_a)
    mult = jnp.sqrt(1.0 - a * a)
    mult = jnp.where((row + t * Tt) == 0, 1.0, mult)
    b = mult * ig * xc

    d = 1
    while d < Tt:
        keep = row >= d
        b = jnp.where(keep, a * pltpu.roll(b, d, 0) + b, b)
        a = jnp.where(keep, a * pltpu.roll(a, d, 0), a)
        d *= 2
    hs = a * h_ref[7:8, :] + b
    h_ref[...] = hs[Tt - 8:Tt]
    o_ref[...] = (hs * jax.nn.gelu(gate)).astype(o_ref.dtype)


def _rglru(a_xg, conv_w, conv_b, wa_bd, b_a, wx_bd, b_x, lam, B, S):
    Tt = 256
    W = MIX_WIDTH
    vec = pl.BlockSpec((1, W), lambda b, t: (0, 0))
    mat = pl.BlockSpec((W, W), lambda b, t: (0, 0))
    out = pl.pallas_call(
        functools.partial(_lru_kernel, Tt=Tt),
        grid=(B, S // Tt),
        in_specs=[pl.BlockSpec((None, Tt, 2 * W), lambda b, t: (b, t, 0)),
                  pl.BlockSpec((CONV_WIDTH, W), lambda b, t: (0, 0)),
                  vec, mat, vec, mat, vec, vec],
        out_specs=pl.BlockSpec((None, Tt, W), lambda b, t: (b, t, 0)),
        out_shape=jax.ShapeDtypeStruct((B, S, W), BF16),
        scratch_shapes=[pltpu.VMEM((Tt + 8, W), F32), pltpu.VMEM((8, W), F32)],
        compiler_params=_cparams("arbitrary", "arbitrary"),
        name="rglru",
    )(a_xg.reshape(B, S, 2 * W), conv_w, conv_b, wa_bd, b_a, wx_bd, b_x, lam)
    return out.reshape(B * S, W)


def _lane_window(q128, lo, width, roll, scale):
    q = q128 * scale
    if roll:
        q = pltpu.roll(q, LANES // 2, 1)
    lane = lax.broadcasted_iota(I32, q.shape, 1)
    return jnp.where((lane >= lo) & (lane < lo + width), q, 0.0).astype(BF16)


def _pair_halves(even, odd, even_half, odd_half):
    if even_half == 1:
        even = pltpu.roll(even, LANES // 2, 1)
    if odd_half == 0:
        odd = pltpu.roll(odd, LANES // 2, 1)
    lane = lax.broadcasted_iota(I32, even.shape, 1)
    return jnp.where(lane < LANES // 2, even, odd)


def _banded_kernel(*refs, nback, window, q_cfg, has_sink, shared_kv):
    nk = nback + BANDED_Q_BLOCKS
    TQ = BANDED_Q_BLOCKS * Q_BLOCK
    q_ref = refs[0]
    k_refs = refs[1:1 + nk]
    pos = 1 + nk
    if shared_kv:
        v_refs = k_refs
    else:
        v_refs = refs[pos:pos + nk]
        pos += nk
    bias_ref = refs[pos]
    pos += 1
    sink_ref = None
    if has_sink:
        sink_ref = refs[pos]
        pos += 1
    o_ref = refs[pos]

    n = pl.program_id(1)
    KW = nk * Q_BLOCK
    kcat = jnp.concatenate([k_refs[nk - 1 - i][...] for i in range(nk)], axis=0)
    vcat = kcat if shared_kv else jnp.concatenate([v_refs[nk - 1 - i][...] for i in range(nk)], axis=0)
    vt = vcat.astype(F32).T.astype(BF16)
    jj = lax.broadcasted_iota(I32, (KW, TQ), 0)
    ii = lax.broadcasted_iota(I32, (KW, TQ), 1)
    dist = ii - jj + nback * Q_BLOCK
    valid = (dist >= 0) & (dist < window) & (jj >= (nback - BANDED_Q_BLOCKS * n) * Q_BLOCK)
    scale = HEAD_DIM ** -0.5

    def logits(h):
        lo, roll, _ = q_cfg[h]
        g = h // 2
        qx = _lane_window(q_ref[:, LANES * g:LANES * (g + 1)].astype(F32), lo, HEAD_DIM, roll, scale)
        s = lax.dot_general(kcat, qx, _NT, preferred_element_type=F32) + bias_ref[h, 0]
        return jnp.where(valid, s, NEG_INF)

    def softmax(h, s):
        m = jnp.max(s, axis=0, keepdims=True)
        if has_sink:
            sink = sink_ref[0:1, h:h + 1]
            m = jnp.maximum(m, sink)
        p = jnp.exp(s - m)
        den = jnp.sum(p, axis=0, keepdims=True)
        if has_sink:
            den = den + jnp.exp(sink - m)
        return p.astype(BF16), 1.0 / jnp.maximum(den, TINY)

    def values(h, p, inv):
        half = q_cfg[h][2]
        return jnp.dot(vt[HEAD_DIM * half:HEAD_DIM * (half + 1)], p, preferred_element_type=F32) * inv

    nh = len(q_cfg)
    ss = [logits(h) for h in range(nh)]
    pr = [softmax(h, ss[h]) for h in range(nh)]
    outs = [values(h, *pr[h]) for h in range(nh)]
    for g in range(nh // 2):
        pair = jnp.concatenate([outs[2 * g], outs[2 * g + 1]], axis=0)
        o_ref[:, LANES * g:LANES * (g + 1)] = pair.T.astype(o_ref.dtype)


def _banded_attention(qsrc, q_col, ksrc, k_col, vsrc, v_col, bias, sinks, *, B, S, window, q_cfg):
    nback = -(-window // Q_BLOCK)
    nq = BANDED_Q_BLOCKS
    nk = nback + nq
    TQ = nq * Q_BLOCK
    shared_kv = vsrc is None
    has_sink = sinks is not None

    def kv_spec(col, i):
        return pl.BlockSpec((None, Q_BLOCK, LANES), lambda b, n: (b, jnp.maximum(nq * n + nq - 1 - i, 0), col))

    in_specs = [pl.BlockSpec((None, TQ, 2 * LANES), lambda b, n: (b, n, q_col))]
    args = [qsrc]
    in_specs += [kv_spec(k_col, i) for i in range(nk)]
    args += [ksrc] * nk
    if not shared_kv:
        in_specs += [kv_spec(v_col, i) for i in range(nk)]
        args += [vsrc] * nk
    in_specs.append(pl.BlockSpec(bias.shape, lambda b, n: (0, 0, 0, 0)))
    args.append(bias)
    if has_sink:
        in_specs.append(pl.BlockSpec(sinks.shape, lambda b, n: (0, 0)))
        args.append(sinks)
    return pl.pallas_call(
        functools.partial(_banded_kernel, nback=nback, window=window, q_cfg=q_cfg,
                          has_sink=has_sink, shared_kv=shared_kv),
        grid=(B, S // TQ),
        in_specs=in_specs,
        out_specs=pl.BlockSpec((None, TQ, 2 * LANES), lambda b, n: (b, n, 0)),
        out_shape=jax.ShapeDtypeStruct((B, S, 2 * LANES), BF16),
        compiler_params=_cparams("arbitrary", "arbitrary"),
        name="banded_attn",
    )(*args)


def _flash_kernel(*refs, mode, lambda_init):
    TB = FLASH_BLOCK
    if mode == "diff":
        (q_ref, k_ref, vt_ref, bias_ref, lam_ref, g_ref, o_ref,
         qx_ref, m_ref, l_ref, acc_ref) = refs
        sel_ref = None
        maps = [dict(g=h // 2, lo=(h % 2) * 64 + 32 * mm, w=DIFF_DIM, roll=False, kg=h // 2, vrow=HEAD_DIM * h, h=h)
                for h in range(DIFF_HEADS) for mm in range(2)]
        scale = DIFF_DIM ** -0.5
    else:
        (q_ref, k_ref, vt_ref, pen_ref, blk_ref, bias_ref, o_ref,
         qx_ref, m_ref, l_ref, acc_ref) = refs
        sel_ref = pen_ref
        maps = [dict(g=h // 2, lo=0, w=HEAD_DIM, roll=(h % 2 == 1), kg=0, vrow=HEAD_DIM, h=h)
                for h in range(NSA_HEADS)]
        scale = HEAD_DIM ** -0.5
    n = pl.program_id(1)

    for mp, c in enumerate(maps):
        qx_ref[mp, :, 0:LANES] = _lane_window(q_ref[:, LANES * c["g"]:LANES * (c["g"] + 1)].astype(F32),
                                              c["lo"], c["w"], c["roll"], scale * LOG2E)
        if sel_ref is not None:
            qx_ref[mp, :, LANES:2 * LANES] = pen_ref[...]
    m_ref[...] = jnp.full(m_ref.shape, NEG_INF, F32)
    l_ref[...] = jnp.zeros_like(l_ref)
    acc_ref[...] = jnp.zeros_like(acc_ref)

    key = lax.broadcasted_iota(I32, (TB, TB), 0)
    qry = lax.broadcasted_iota(I32, (TB, TB), 1)

    def step(j, kind, TK=TB):
        koff = pl.multiple_of(j * TB, TB)
        valid = None
        if kind == "diag":
            valid = qry >= key

        def logits(mp):
            c = maps[mp]
            kt = k_ref[pl.ds(koff, TK), LANES * c["kg"]:LANES * (c["kg"] + 1)]
            if sel_ref is not None:
                kt = jnp.concatenate([kt, blk_ref[pl.ds(koff, TK), :]], axis=1)
            s = lax.dot_general(kt, qx_ref[mp], _NT, preferred_element_type=F32)
            if kind == "near":
                s = s + bias_ref[c["h"], 1]
            elif kind == "diag":
                s = s + bias_ref[c["h"], 0]
            if valid is not None:
                s = jnp.where(valid, s, NEG_INF)
            return s

        def softmax(mp, s):
            m_prev = m_ref[mp]
            m_new = jnp.maximum(m_prev, jnp.max(s, axis=0, keepdims=True))
            alpha = jnp.exp2(m_prev - m_new)
            p = jnp.exp2(s - m_new)
            l_ref[mp] = alpha * l_ref[mp] + jnp.sum(p, axis=0, keepdims=True)
            m_ref[mp] = m_new
            return p.astype(BF16), alpha

        def values(mp, p, alpha):
            c = maps[mp]
            vt = vt_ref[c["vrow"]:c["vrow"] + HEAD_DIM, pl.ds(koff, TK)]
            acc_ref[mp] = alpha * acc_ref[mp] + jnp.dot(vt, p, preferred_element_type=F32)

        nm = len(maps)
        ss = [logits(mp) for mp in range(nm)]

        def two_pass():
            pa = [softmax(mp, ss[mp]) for mp in range(nm)]
            for mp in range(nm):
                values(mp, *pa[mp])

        if kind == "diag":
            two_pass()
            return

        stabs = [m_ref[mp] for mp in range(nm)]
        ps = [jnp.exp2(ss[mp] - stabs[mp]) for mp in range(nm)]
        m_news = [jnp.maximum(stabs[mp], jnp.max(ss[mp], axis=0, keepdims=True)) for mp in range(nm)]
        growth = m_news[0] - stabs[0]
        for mp in range(1, nm):
            growth = jnp.maximum(growth, m_news[mp] - stabs[mp])
        safe = jnp.max(growth) <= FLASH_MAX_GROWTH
        psums = [jnp.sum(ps[mp], axis=0, keepdims=True) for mp in range(nm)]
        pvs = [jnp.dot(vt_ref[maps[mp]["vrow"]:maps[mp]["vrow"] + HEAD_DIM, pl.ds(koff, TK)], ps[mp].astype(BF16),
                       preferred_element_type=F32) for mp in range(nm)]

        @pl.when(safe)
        def _():
            for mp in range(nm):
                alpha = jnp.exp2(stabs[mp] - m_news[mp])
                l_ref[mp] = (l_ref[mp] + psums[mp]) * alpha
                acc_ref[mp] = (acc_ref[mp] + pvs[mp]) * alpha
                m_ref[mp] = m_news[mp]

        @pl.when(jnp.logical_not(safe))
        def _():
            two_pass()

    step(n, "diag")

    @pl.when(n >= 1)
    def _():
        step(n - 1, "near")

    nfar = jnp.maximum(n - 1, 0)
    FW = FLASH_FAR_BLOCKS * (1 if mode == "diff" else 2)

    def far_body(j, carry):
        step(FW * j, "far", FW * TB)
        return carry

    lax.fori_loop(0, nfar // FW, far_body, 0)

    done = (nfar // FW) * FW
    rest = nfar - done
    size = FW // 2
    while size >= 1:
        def tail_step(size=size):
            step(done + rest - rest % (2 * size), "far", size * TB)

        pl.when(rest % (2 * size) >= size)(tail_step)
        size //= 2

    def normalised(mp):
        return acc_ref[mp] * (1.0 / jnp.maximum(l_ref[mp], TINY))

    if mode == "diff":
        lq = lam_ref[...]
        lam = (jnp.exp(jnp.sum(lq[0:1] * lq[1:2], axis=1, keepdims=True))
               - jnp.exp(jnp.sum(lq[2:3] * lq[3:4], axis=1, keepdims=True)) + lambda_init)
        for g in range(DIFF_HEADS // 2):
            res = []
            for h in (2 * g, 2 * g + 1):
                o = normalised(2 * h) - lam * normalised(2 * h + 1)
                ms = jnp.sum(o * o, axis=0, keepdims=True) * (1.0 / (2 * DIFF_DIM))
                res.append(o * lax.rsqrt(ms + DIFF_SUBLN_EPS))
            out = jnp.concatenate(res, axis=0) * g_ref[...] * (1.0 - lambda_init)
            o_ref[:, LANES * g:LANES * (g + 1)] = out.T.astype(o_ref.dtype)
    else:
        for g in range(NSA_HEADS // 2):
            out = jnp.concatenate([normalised(2 * g), normalised(2 * g + 1)], axis=0)
            o_ref[:, LANES * g:LANES * (g + 1)] = out.T.astype(o_ref.dtype)


def _flash_attention(mode, qsrc, q_col, ksrc, k_col, vt, vt_rows, vt_blk, bias, extra, *, B, S, lambda_init=0.0):
    TB = FLASH_BLOCK
    q_spec = pl.BlockSpec((None, TB, 2 * LANES), lambda b, n: (b, n, q_col))
    bias_spec = pl.BlockSpec(bias.shape, lambda b, n: (0, 0, 0, 0))
    vt_spec = pl.BlockSpec((vt_rows, S), lambda b, n: (vt_blk, b))
    if mode == "diff":
        lam4, gsub = extra
        nmaps = 2 * DIFF_HEADS
        in_specs = [q_spec,
                    pl.BlockSpec((None, S, 2 * LANES), lambda b, n: (b, 0, k_col)),
                    vt_spec, bias_spec,
                    pl.BlockSpec(lam4.shape, lambda b, n: (0, 0)),
                    pl.BlockSpec(gsub.shape, lambda b, n: (0, 0))]
        args = [qsrc, ksrc, vt, bias, lam4, gsub]
    else:
        (pen,) = extra
        nmaps = NSA_HEADS
        blk = jnp.asarray(np.arange(S)[:, None] // NSA_SLC_LEN == np.arange(LANES)[None, :], BF16)
        in_specs = [q_spec,
                    pl.BlockSpec((None, S, LANES), lambda b, n: (b, 0, k_col)),
                    vt_spec,
                    pl.BlockSpec((None, TB, LANES), lambda b, n: (b, n, 0)),
                    pl.BlockSpec((S, LANES), lambda b, n: (0, 0)),
                    bias_spec]
        args = [qsrc, ksrc, vt, pen, blk, bias]
    return pl.pallas_call(
        functools.partial(_flash_kernel, mode=mode, lambda_init=lambda_init),
        grid=(B, S // TB),
        in_specs=in_specs,
        out_specs=pl.BlockSpec((None, TB, 2 * LANES), lambda b, n: (b, n, 0)),
        out_shape=jax.ShapeDtypeStruct((B, S, 2 * LANES), BF16),
        scratch_shapes=[pltpu.VMEM((nmaps, TB, LANES if mode == "diff" else 2 * LANES), BF16),
                        pltpu.VMEM((nmaps, 1, TB), F32),
                        pltpu.VMEM((nmaps, 1, TB), F32),
                        pltpu.VMEM((nmaps, HEAD_DIM, TB), F32)],
        compiler_params=_cparams("arbitrary", "arbitrary"),
        name="flash_" + mode,
    )(*args)


def _compress_kernel(ck_ref, cv_ref, pk_ref, pv_ref, w1k_ref, w1v_ref, w2_ref, o_ref):
    half = NSA_CMP_STRIDE * HEAD_DIM
    ck = ck_ref[...]
    cv = cv_ref[...]
    nrow = ck.shape[0]

    def mm(a, w):
        return jnp.dot(a, w, preferred_element_type=F32)

    top = mm(ck, w1k_ref[0:half, :]) + mm(cv, w1v_ref[0:half, :])
    bot = mm(ck, w1k_ref[half:2 * half, :]) + mm(cv, w1v_ref[half:2 * half, :])
    posw = (mm(pk_ref[...], w1k_ref[...]) + mm(pv_ref[...], w1v_ref[...]))[0:1]
    pre = top + pltpu.roll(bot, nrow - 1, 0) + posw
    kcv = mm(jax.nn.gelu(pre).astype(BF16), w2_ref[...])
    o_ref[0:LANES, :] = jnp.zeros((LANES, LANES), o_ref.dtype)
    o_ref[LANES:LANES + nrow, :] = kcv.astype(o_ref.dtype)


def _nsa_compress(ck, cv, pk, pv, w1k, w1v, w2, ncp):
    B, nchunk, width = ck.shape
    assert ncp == LANES + nchunk

    def full(a):
        return pl.BlockSpec(a.shape, lambda b: (0,) * a.ndim)

    chunk = pl.BlockSpec((None, nchunk, width), lambda b: (b, 0, 0))
    return pl.pallas_call(
        _compress_kernel,
        grid=(B,),
        in_specs=[chunk, chunk, full(pk), full(pv), full(w1k), full(w1v), full(w2)],
        out_specs=pl.BlockSpec((None, ncp, LANES), lambda b: (b, 0, 0)),
        out_shape=jax.ShapeDtypeStruct((B, ncp, LANES), BF16),
        compiler_params=_cparams("arbitrary"),
        name="nsa_compress",
    )(ck, cv, pk, pv, w1k, w1v, w2)


def _cmp_kernel(q_ref, kcv_ref, bias_ref, agg_ref, oc_ref, sel_ref, s_scr, *, n_cmp, ncp):
    n = pl.program_id(1)
    QB = CMP_Q
    kcv = kcv_ref[...]
    cp = lax.broadcasted_iota(I32, (QB, ncp), 1)
    qpos = n * QB + lax.broadcasted_iota(I32, (QB, ncp), 0)
    valid = ((cp >= LANES) & (cp < LANES + n_cmp)
             & (qpos - NSA_CMP_STRIDE * (cp - LANES) - (NSA_CMP_LEN - 1) >= 0))
    woff = pl.multiple_of((n // (16 * LANES // CMP_Q)) * LANES, LANES)
    scale = HEAD_DIM ** -0.5

    def logits(h):
        g = h // 2
        qx = _lane_window(q_ref[:, LANES * g:LANES * (g + 1)].astype(F32), 0, HEAD_DIM, h % 2 == 1, scale)
        s_scr[h] = lax.dot_general(qx, kcv, _NT, preferred_element_type=F32)
        s_scr[h, :, pl.ds(woff, 2 * LANES)] += bias_ref[h]

    def softmax(h):
        s = jnp.where(valid, s_scr[h], NEG_INF)
        m = jnp.max(s, axis=1, keepdims=True)
        p = jnp.where(valid, jnp.exp(s - m), 0.0)
        den = jnp.sum(p, axis=1, keepdims=True)
        return p * (1.0 / jnp.maximum(den, TINY))

    for h in range(NSA_HEADS):
        logits(h)
    ps = [softmax(h) for h in range(NSA_HEADS)]
    outs = [jnp.dot(p.astype(BF16), kcv, preferred_element_type=F32) for p in ps]
    for g in range(NSA_HEADS // 2):
        oc_ref[:, LANES * g:LANES * (g + 1)] = _pair_halves(outs[2 * g], outs[2 * g + 1], 1, 1).astype(oc_ref.dtype)

    psum = (ps[0] + ps[1]) + (ps[2] + ps[3])
    imp = jnp.dot(psum.astype(BF16), agg_ref[...], preferred_element_type=F32)

    v = imp.T
    jb = lax.broadcasted_iota(I32, v.shape, 0)
    qp = n * QB + lax.broadcasted_iota(I32, v.shape, 1)
    cur = qp >> 6
    forced = (jb == 0) | (jb == cur) | (jb == cur - 1)
    v = jnp.where(forced, -jnp.inf, jnp.where(jb * NSA_SLC_LEN <= qp, v, -1.0))
    rid = jb.astype(F32)
    picked = jnp.where(forced, 1.0, 0.0)
    for _ in range(NSA_TOPK - 3):
        mx = jnp.max(v, axis=0, keepdims=True)
        first = jnp.min(jnp.where(v == mx, rid, float(LANES)), axis=0, keepdims=True)
        hit = rid == first
        picked = jnp.where(hit, 1.0, picked)
        v = jnp.where(hit, -jnp.inf, v)
    sel_ref[...] = jnp.where(picked.T > 0.5, 0.0, NEG_INF).astype(sel_ref.dtype)


def _nsa_cmp_select(dsrc, kcv, bias, agg, *, B, S, n_cmp, ncp):
    return pl.pallas_call(
        functools.partial(_cmp_kernel, n_cmp=n_cmp, ncp=ncp),
        grid=(B, S // CMP_Q),
        in_specs=[pl.BlockSpec((None, CMP_Q, 2 * LANES), lambda b, n: (b, n, 0)),
                  pl.BlockSpec((None, ncp, LANES), lambda b, n: (b, 0, 0)),
                  pl.BlockSpec((NSA_HEADS, None, CMP_Q, 2 * LANES), lambda b, n: (0, n % (16 * LANES // CMP_Q), 0, 0)),
                  pl.BlockSpec((ncp, LANES), lambda b, n: (0, 0))],
        out_specs=[pl.BlockSpec((None, CMP_Q, 2 * LANES), lambda b, n: (b, n, 0)),
                   pl.BlockSpec((None, CMP_Q, LANES), lambda b, n: (b, n, 0))],
        out_shape=[jax.ShapeDtypeStruct((B, S, 2 * LANES), BF16),
                   jax.ShapeDtypeStruct((B, S, LANES), BF16)],
        scratch_shapes=[pltpu.VMEM((NSA_HEADS, CMP_Q, ncp), F32)],
        compiler_params=_cparams("arbitrary", "arbitrary"),
        name="nsa_cmp_select",
    )(dsrc, kcv, bias, agg)


def _merge_kernel(x_ref, g_ref, ya_ref, yb_ref, yc_ref, oc_ref, os_ref, ow_ref, dg_ref,
                  wg_ref, bg_ref, wb_ref, wo_ref, o_ref):
    x = x_ref[...]
    h = _rms(x, g_ref[...]).astype(BF16)
    tm = x.shape[0]
    lane = lax.broadcasted_iota(I32, (tm, LANES), 1)
    gd = jax.nn.sigmoid(dg_ref[...].astype(F32))
    yd = []
    for g in range(NSA_HEADS // 2):
        acc = jnp.zeros((tm, LANES), F32)
        for br, src in enumerate((oc_ref, os_ref, ow_ref)):
            ge = gd[:, 3 * (2 * g) + br:3 * (2 * g) + br + 1]
            go = gd[:, 3 * (2 * g + 1) + br:3 * (2 * g + 1) + br + 1]
            acc = acc + jnp.where(lane < 64, ge, go) * src[:, LANES * g:LANES * (g + 1)].astype(F32)
        yd.append(acc)
    y_d = jnp.concatenate(yd, axis=1).astype(BF16)
    ys = (ya_ref[...], yb_ref[...], yc_ref[...], y_d)
    merged = jnp.zeros((tm, D_MODEL), F32)
    for br in range(4):
        gate = jax.nn.sigmoid(jnp.dot(h, wg_ref[br], preferred_element_type=F32) + bg_ref[br:br + 1, :])
        merged = merged + gate * jnp.dot(ys[br], wb_ref[br], preferred_element_type=F32)
    o_ref[...] = x + jnp.dot(merged.astype(BF16), wo_ref[...], preferred_element_type=F32)


def _merge(x, g, ya, yb, yc, oc, osel, ow, dproj, wg, bg, wb, wo):
    T = x.shape[0]
    tm = 256

    def tile(width, col=0):
        return pl.BlockSpec((tm, width), lambda i: (i, col))

    def full(a):
        return pl.BlockSpec(a.shape, lambda i: (0,) * a.ndim)

    return pl.pallas_call(
        _merge_kernel,
        grid=(T // tm,),
        in_specs=[tile(D_MODEL), full(g), tile(256), tile(256), tile(256), tile(256), tile(256), tile(256),
                  tile(LANES, 5), full(wg), full(bg), full(wb), full(wo)],
        out_specs=tile(D_MODEL),
        out_shape=jax.ShapeDtypeStruct((T, D_MODEL), F32),
        compiler_params=_cparams("arbitrary"),
        name="merge",
    )(x, g, ya, yb, yc, oc, osel, ow, dproj, wg, bg, wb, wo)


def _route(hn, w_split, b_router):
    lane = lax.broadcasted_iota(I32, (hn.shape[0], LANES), 1)
    hi = hn.astype(BF16)
    lo = (hn - hi.astype(F32)).astype(BF16)
    a = jnp.dot(hi, w_split, preferred_element_type=F32)
    b = jnp.dot(lo, w_split[:, 0:LANES], preferred_element_type=F32)
    logits = (a[:, 0:LANES] + a[:, LANES:2 * LANES]) + b + b_router
    is_grp = (lane >= N_EXPERTS) & (lane < N_EXPERTS + N_GROUPS)
    lanef = lane.astype(F32)
    lg = jnp.where(is_grp, logits, -jnp.inf)
    mg = jnp.max(lg, axis=1, keepdims=True)
    gsel = jnp.min(jnp.where(lg == mg, lanef, 2.0 * LANES), axis=1, keepdims=True) - N_EXPERTS
    g_w = 1.0 / jnp.sum(jnp.where(is_grp, jnp.exp(lg - mg), 0.0), axis=1, keepdims=True)
    in_grp = (lane < N_EXPERTS) & ((lane >> 3).astype(F32) == gsel)
    v1 = jnp.where(in_grp, logits, -jnp.inf)
    m1 = jnp.max(v1, axis=1, keepdims=True)
    i1 = jnp.min(jnp.where(v1 == m1, lanef, 2.0 * LANES), axis=1, keepdims=True)
    v2 = jnp.where(lanef == i1, -jnp.inf, v1)
    m2 = jnp.max(v2, axis=1, keepdims=True)
    i2 = jnp.min(jnp.where(v2 == m2, lanef, 2.0 * LANES), axis=1, keepdims=True)
    e2 = jnp.exp(m2 - m1)
    w1 = g_w / (1.0 + e2)
    w2 = g_w * e2 / (1.0 + e2)
    first, second = lanef == i1, lanef == i2
    comb = jnp.where(first, w1, 0.0) + jnp.where(second, w2, 0.0)
    return comb, jnp.where(first | second, 1.0, 0.0)


def _moe_kernel(x_ref, g_ref, wr_ref, br_ref, wg_ref, wu_ref, wd_ref, fg_ref, o_ref,
                hn_ref, comb_ref, acc_ref, *, final_norm):
    e = pl.program_id(1)
    tm = x_ref.shape[0]
    lane = lax.broadcasted_iota(I32, (tm, LANES), 1)

    @pl.when(e == 0)
    def _():
        hn = _rms(x_ref[...], g_ref[...])
        hn_ref[...] = hn.astype(BF16)
        comb_ref[...] = _route(hn, wr_ref[...], br_ref[...])[0]
        acc_ref[...] = jnp.zeros_like(acc_ref)

    hn = hn_ref[...]
    gt = jnp.dot(hn, wg_ref[...], preferred_element_type=F32)
    up = jnp.dot(hn, wu_ref[...], preferred_element_type=F32)
    he = (gt * jax.nn.sigmoid(gt) * up).astype(BF16)
    ye = jnp.dot(he, wd_ref[...], preferred_element_type=F32)
    cw = jnp.sum(jnp.where(lane == e, comb_ref[...], 0.0), axis=1, keepdims=True)
    acc_ref[...] += cw * ye

    @pl.when(e == N_EXPERTS - 1)
    def _():
        y = x_ref[...] + acc_ref[...]
        if final_norm:
            y = _rms(y, fg_ref[...])
        o_ref[...] = y


def _moe(x, g, w_router, b_router, wg, wu, wd, final_g, final_norm):
    T = x.shape[0]
    tm = 1024
    return pl.pallas_call(
        functools.partial(_moe_kernel, final_norm=final_norm),
        grid=(T // tm, N_EXPERTS),
        in_specs=[pl.BlockSpec((tm, D_MODEL), lambda i, e: (i, 0)),
                  pl.BlockSpec((1, D_MODEL), lambda i, e: (0, 0)),
                  pl.BlockSpec((D_MODEL, 2 * LANES), lambda i, e: (0, 0)),
                  pl.BlockSpec((1, LANES), lambda i, e: (0, 0)),
                  pl.BlockSpec((None, D_MODEL, EXPERT_FF), lambda i, e: (e, 0, 0)),
                  pl.BlockSpec((None, D_MODEL, EXPERT_FF), lambda i, e: (e, 0, 0)),
                  pl.BlockSpec((None, EXPERT_FF, D_MODEL), lambda i, e: (e, 0, 0)),
                  pl.BlockSpec((1, D_MODEL), lambda i, e: (0, 0))],
        out_specs=pl.BlockSpec((tm, D_MODEL), lambda i, e: (i, 0)),
        out_shape=jax.ShapeDtypeStruct((T, D_MODEL), F32),
        scratch_shapes=[pltpu.VMEM((tm, D_MODEL), BF16),
                        pltpu.VMEM((tm, LANES), F32),
                        pltpu.VMEM((tm, D_MODEL), F32)],
        compiler_params=_cparams("arbitrary", "arbitrary"),
        name="moe",
    )(x, g, w_router, b_router, wg, wu, wd, final_g)


def _slot_matrix(pos_t, val_t, ntok, half, experts=range(N_EXPERTS)):
    nslot = MOE_CAP // 2
    slot = (lax.broadcasted_iota(I32, (nslot, ntok), 0) + half * nslot).astype(F32)
    blocks = []
    for e in experts:
        hit = pos_t[e:e + 1, :] == slot
        blocks.append(jnp.where(hit, 1.0 if val_t is None else val_t[e:e + 1, :], 0.0).astype(BF16))
    return jnp.concatenate(blocks, axis=0)


ROUTE_ROWS = 48


def _route_t(hn, w_split_t, b_col):
    nr = ROUTE_ROWS
    hi = hn.astype(BF16)
    lo = (hn - hi.astype(F32)).astype(BF16)
    a = lax.dot_general(w_split_t, hi, _NT, preferred_element_type=F32)
    b = lax.dot_general(w_split_t[0:LANES], lo, _NT, preferred_element_type=F32)
    logits = (a[0:nr] + a[LANES:LANES + nr]) + b[0:nr] + b_col[0:nr]
    row = lax.broadcasted_iota(I32, logits.shape, 0)
    rowf = row.astype(F32)
    is_grp = (row >= N_EXPERTS) & (row < N_EXPERTS + N_GROUPS)
    lg = jnp.where(is_grp, logits, -jnp.inf)
    mg = jnp.max(lg, axis=0, keepdims=True)
    gsel = jnp.min(jnp.where(lg == mg, rowf, 2.0 * LANES), axis=0, keepdims=True) - N_EXPERTS
    g_w = 1.0 / jnp.sum(jnp.where(is_grp, jnp.exp(lg - mg), 0.0), axis=0, keepdims=True)
    in_grp = (row < N_EXPERTS) & ((row >> 3).astype(F32) == gsel)
    v1 = jnp.where(in_grp, logits, -jnp.inf)
    m1 = jnp.max(v1, axis=0, keepdims=True)
    i1 = jnp.min(jnp.where(v1 == m1, rowf, 2.0 * LANES), axis=0, keepdims=True)
    v2 = jnp.where(rowf == i1, -jnp.inf, v1)
    m2 = jnp.max(v2, axis=0, keepdims=True)
    i2 = jnp.min(jnp.where(v2 == m2, rowf, 2.0 * LANES), axis=0, keepdims=True)
    e2 = jnp.exp(m2 - m1)
    w1 = g_w / (1.0 + e2)
    w2 = g_w * e2 / (1.0 + e2)
    first, second = rowf == i1, rowf == i2
    comb = jnp.where(first, w1, 0.0) + jnp.where(second, w2, 0.0)
    return comb, jnp.where(first | second, 1.0, 0.0)


def _dispatch_kernel(x_ref, g_ref, wr_ref, br_ref, tri_ref, xd_ref, pos_ref, wt_ref, cnt_ref):
    tm = x_ref.shape[0]
    nr = ROUTE_ROWS
    hn = _rms(x_ref[...], g_ref[...])
    comb_t, routed_t = _route_t(hn, wr_ref[...], br_ref[...])
    rank_t = jnp.dot(routed_t.astype(BF16), tri_ref[...], preferred_element_type=F32)
    pos_t = jnp.where(routed_t > 0.0, rank_t, -1.0)
    pos_ref[0:nr, :] = pos_t
    pos_ref[nr:LANES, :] = jnp.full((LANES - nr, tm), -1.0, F32)
    wt_ref[0:nr, :] = comb_t
    wt_ref[nr:LANES, :] = jnp.zeros((LANES - nr, tm), F32)
    counts = jnp.sum(routed_t, axis=1, keepdims=True)
    cnt_ref[0:nr, :] = jnp.broadcast_to(counts, (nr, LANES))
    cnt_ref[nr:LANES, :] = jnp.zeros((LANES - nr, LANES), F32)
    hnb = hn.astype(BF16)
    nslot = MOE_CAP // 2
    lower = jnp.dot(_slot_matrix(pos_t, None, tm, 0), hnb, preferred_element_type=F32)
    xd_ref[:, 0:nslot, :] = lower.reshape(N_EXPERTS, nslot, D_MODEL).astype(xd_ref.dtype)

    for grp in range(N_GROUPS):
        e0 = grp * EXPERTS_PER_GROUP
        busy = jnp.max(counts[e0:e0 + EXPERTS_PER_GROUP]) > nslot

        def fill(e0=e0):
            upper = jnp.dot(_slot_matrix(pos_t, None, tm, 1, range(e0, e0 + EXPERTS_PER_GROUP)), hnb,
                            preferred_element_type=F32)
            xd_ref[e0:e0 + EXPERTS_PER_GROUP, nslot:MOE_CAP, :] = (
                upper.reshape(EXPERTS_PER_GROUP, nslot, D_MODEL).astype(xd_ref.dtype))

        def clear(e0=e0):
            xd_ref[e0:e0 + EXPERTS_PER_GROUP, nslot:MOE_CAP, :] = jnp.zeros((EXPERTS_PER_GROUP, nslot, D_MODEL),
                                                                           xd_ref.dtype)

        pl.when(busy)(fill)
        pl.when(jnp.logical_not(busy))(clear)


def _expert_kernel(busy_ref, xlo_ref, xhi_ref, wg_ref, wu_ref, wd_ref, olo_ref, ohi_ref):
    ntile, nslot, _ = xlo_ref.shape

    def ffn(x_ref, o_ref):
        x = x_ref[...].reshape(ntile * nslot, D_MODEL)
        gt = jnp.dot(x, wg_ref[...], preferred_element_type=F32)
        up = jnp.dot(x, wu_ref[...], preferred_element_type=F32)
        he = (gt * jax.nn.sigmoid(gt) * up).astype(BF16)
        ye = jnp.dot(he, wd_ref[...], preferred_element_type=F32)
        o_ref[...] = ye.reshape(ntile, nslot, D_MODEL).astype(o_ref.dtype)

    ffn(xlo_ref, olo_ref)
    busy = busy_ref[pl.program_id(0), pl.program_id(1)] > 0

    @pl.when(busy)
    def _():
        ffn(xhi_ref, ohi_ref)

    @pl.when(jnp.logical_not(busy))
    def _():
        ohi_ref[...] = jnp.zeros(ohi_ref.shape, ohi_ref.dtype)


def _combine_kernel(busy_ref, x_ref, ylo_ref, yhi_ref, pos_ref, wt_ref, fg_ref, o_ref, *, final_norm):
    tm = x_ref.shape[0]
    nslot = MOE_CAP // 2

    def gathered(y_ref, half, e0, ne):
        weights = _slot_matrix(pos_ref[...], wt_ref[...], tm, half, range(e0, e0 + ne))
        yd = y_ref[e0:e0 + ne].reshape(ne * nslot, D_MODEL)
        return lax.dot_general(weights, yd, (((0,), (0,)), ((), ())), preferred_element_type=F32)

    o_ref[...] = x_ref[...] + gathered(ylo_ref, 0, 0, N_EXPERTS)

    for grp in range(N_GROUPS):
        def add(grp=grp):
            o_ref[...] += gathered(yhi_ref, 1, grp * EXPERTS_PER_GROUP, EXPERTS_PER_GROUP)

        pl.when(busy_ref[pl.program_id(0), grp] > 0)(add)

    if final_norm:
        o_ref[...] = _rms(o_ref[...], fg_ref[...])


def _moe_dispatch(x, g, w_router, b_router):
    T = x.shape[0]
    tm = MOE_TILE
    nt = T // tm
    tri = jnp.asarray(np.triu(np.ones((tm, tm), np.float32), 1), BF16)
    xd, pos_t, w_t, cnt = pl.pallas_call(
        _dispatch_kernel,
        grid=(nt,),
        in_specs=[pl.BlockSpec((tm, D_MODEL), lambda i: (i, 0)),
                  pl.BlockSpec((1, D_MODEL), lambda i: (0, 0)),
                  pl.BlockSpec((2 * LANES, D_MODEL), lambda i: (0, 0)),
                  pl.BlockSpec((LANES, 1), lambda i: (0, 0)),
                  pl.BlockSpec((tm, tm), lambda i: (0, 0))],
        out_specs=[pl.BlockSpec((N_EXPERTS, None, MOE_CAP, D_MODEL), lambda i: (0, i, 0, 0)),
                   pl.BlockSpec((LANES, tm), lambda i: (0, i)),
                   pl.BlockSpec((LANES, tm), lambda i: (0, i)),
                   pl.BlockSpec((None, LANES, LANES), lambda i: (i, 0, 0))],
        out_shape=[jax.ShapeDtypeStruct((N_EXPERTS, nt, MOE_CAP, D_MODEL), BF16),
                   jax.ShapeDtypeStruct((LANES, T), F32),
                   jax.ShapeDtypeStruct((LANES, T), F32),
                   jax.ShapeDtypeStruct((nt, LANES, LANES), F32)],
        compiler_params=_cparams("arbitrary"),
        name="moe_dispatch",
    )(x, g, w_router.T, b_router.reshape(LANES, 1), tri)
    return xd, pos_t, w_t, cnt[:, 0:N_EXPERTS, 0]


def _moe_finish(x, xd, pos_t, w_t, counts, wg, wu, wd, final_g, final_norm):
    T = x.shape[0]
    tm = MOE_TILE
    nt = T // tm
    tps = min(MOE_TILES_PER_STEP, nt)
    assert nt % tps == 0
    nslot = MOE_CAP // 2
    hot = counts > nslot
    busy_blocks = jnp.any(hot.reshape(nt // tps, tps, N_EXPERTS), axis=1).T.astype(I32)
    busy_tiles = jnp.any(hot.reshape(nt, N_GROUPS, EXPERTS_PER_GROUP), axis=2).astype(I32)
    halves = xd.reshape(N_EXPERTS, nt, 2, nslot, D_MODEL)
    half_out = jax.ShapeDtypeStruct((N_EXPERTS, nt, nslot, D_MODEL), BF16)
    yd_lo, yd_hi = pl.pallas_call(
        _expert_kernel,
        grid_spec=pltpu.PrefetchScalarGridSpec(
            num_scalar_prefetch=1,
            grid=(N_EXPERTS, nt // tps),
            in_specs=[pl.BlockSpec((None, tps, None, nslot, D_MODEL), lambda e, c, b: (e, c, 0, 0, 0)),
                      pl.BlockSpec((None, tps, None, nslot, D_MODEL),
                                   lambda e, c, b: (e, jnp.where(b[e, c] > 0, c, 0), 1, 0, 0)),
                      pl.BlockSpec((None, D_MODEL, EXPERT_FF), lambda e, c, b: (e, 0, 0)),
                      pl.BlockSpec((None, D_MODEL, EXPERT_FF), lambda e, c, b: (e, 0, 0)),
                      pl.BlockSpec((None, EXPERT_FF, D_MODEL), lambda e, c, b: (e, 0, 0))],
            out_specs=[pl.BlockSpec((None, tps, nslot, D_MODEL), lambda e, c, b: (e, c, 0, 0)),
                       pl.BlockSpec((None, tps, nslot, D_MODEL), lambda e, c, b: (e, c, 0, 0))]),
        out_shape=[half_out, half_out],
        compiler_params=_cparams("arbitrary", "arbitrary"),
        name="moe_experts",
    )(busy_blocks, halves, halves, wg, wu, wd)
    return pl.pallas_call(
        functools.partial(_combine_kernel, final_norm=final_norm),
        grid_spec=pltpu.PrefetchScalarGridSpec(
            num_scalar_prefetch=1,
            grid=(nt,),
            in_specs=[pl.BlockSpec((tm, D_MODEL), lambda i, b: (i, 0)),
                      pl.BlockSpec((N_EXPERTS, None, nslot, D_MODEL), lambda i, b: (0, i, 0, 0)),
                      pl.BlockSpec((N_EXPERTS, None, nslot, D_MODEL),
                                   lambda i, b: (0, jnp.where(b[i, 0] + b[i, 1] + b[i, 2] + b[i, 3] > 0, i, 0), 0, 0)),
                      pl.BlockSpec((LANES, tm), lambda i, b: (0, i)),
                      pl.BlockSpec((LANES, tm), lambda i, b: (0, i)),
                      pl.BlockSpec((1, D_MODEL), lambda i, b: (0, 0))],
            out_specs=pl.BlockSpec((tm, D_MODEL), lambda i, b: (i, 0))),
        out_shape=jax.ShapeDtypeStruct((T, D_MODEL), F32),
        compiler_params=_cparams("arbitrary"),
        name="moe_combine",
    )(busy_tiles, x, yd_lo, yd_hi, pos_t, w_t, final_g)


def _block_diag(w):
    nb, bs, _ = w.shape
    out = jnp.zeros((nb * bs, nb * bs), w.dtype)
    for i in range(nb):
        out = out.at[i * bs:(i + 1) * bs, i * bs:(i + 1) * bs].set(w[i])
    return out


def _selection_aggregator_np(n_cmp, n_slc):
    ratio_s = NSA_SLC_LEN // NSA_CMP_STRIDE
    ratio_c = NSA_CMP_LEN // NSA_CMP_STRIDE
    w = np.convolve(np.ones(ratio_s), np.ones(ratio_c)).astype(np.float32)
    cj = np.arange(n_slc)[:, None] * ratio_s - (ratio_c - 1) + np.arange(w.size)[None, :]
    jj = np.broadcast_to(np.arange(n_slc)[:, None], cj.shape)
    ww = np.broadcast_to(w[None, :], cj.shape)
    keep = (cj >= 0) & (cj < n_cmp)
    agg = np.zeros((n_cmp, n_slc), np.float32)
    np.add.at(agg, (cj[keep], jj[keep]), ww[keep])
    return agg


def _row(v):
    return v.reshape(1, -1).astype(F32)


def _diff_lambda_init(layer):
    return 0.8 - 0.6 * math.exp(-0.3 * layer)


def _mixer_layer(x, l, B, S, tiles, p):
    (bias_swa, bias_diff, bias_slc, bias_win, bias_cmp, agg_pad, n_cmp, ncp) = tiles
    w_in = jnp.pad(p["w_in"][l], ((0, 0), (0, N_IN_PAD - p["w_in"].shape[2]))).astype(BF16)
    w_t = jnp.concatenate([w_in[:, 1536:1792], w_in[:, 2176:2304]], axis=1).T
    a_xg, b_qkv, c_qkv, d_all, vals_t = _in_proj(x, _row(p["mix_norm_g"][l]), w_in, w_t)

    y_a = _rglru(a_xg, p["conv_w"][l].astype(F32), _row(p["conv_b"][l]),
                 _block_diag(p["lru_w_a"][l]).astype(BF16), _row(p["lru_b_a"][l]),
                 _block_diag(p["lru_w_x"][l]).astype(BF16), _row(p["lru_b_x"][l]),
                 _row(p["lru_lambda"][l]), B, S)

    b3 = b_qkv.reshape(B, S, 512)
    swa_cfg = tuple(((h // 2) * 64, (h % 2) != (h // 2), h // 2) for h in range(SWA_HEADS))
    y_b = _banded_attention(b3, 0, b3, 2, b3, 3, bias_swa, _row(p["swa_sinks"][l]),
                            B=B, S=S, window=SWA_WINDOW, q_cfg=swa_cfg)

    c3 = c_qkv.reshape(B, S, 768)
    lam4 = jnp.stack([p["diff_lq1"][l], p["diff_lk1"][l], p["diff_lq2"][l], p["diff_lk2"][l]]).astype(F32)
    gsub = jnp.tile(p["diff_subln_g"][l].reshape(-1, 1).astype(F32), (2, 1))
    y_c = _flash_attention("diff", c3, 0, c3, 1, vals_t, 256, 0, bias_diff, (lam4, gsub), B=B, S=S,
                           lambda_init=_diff_lambda_init(l))

    d3 = d_all.reshape(B, S, 768)
    nchunk = S // NSA_CMP_STRIDE
    ck = d3[:, :, 256:320].reshape(B, nchunk, NSA_CMP_STRIDE * HEAD_DIM)
    cv = d3[:, :, 320:384].reshape(B, nchunk, NSA_CMP_STRIDE * HEAD_DIM)

    def pos8(a):
        return jnp.broadcast_to(a.reshape(1, -1), (8, a.size)).astype(BF16)

    w1k = jnp.pad(p["nsa_w1_k"][l], ((0, 0), (0, HEAD_DIM))).astype(BF16)
    w1v = jnp.pad(p["nsa_w1_v"][l], ((0, 0), (HEAD_DIM, 0))).astype(BF16)
    w2 = _block_diag(jnp.stack([p["nsa_w2_k"][l], p["nsa_w2_v"][l]])).astype(BF16)
    kcv = _nsa_compress(ck, cv, pos8(p["nsa_pos_k"][l]), pos8(p["nsa_pos_v"][l]), w1k, w1v, w2, ncp)
    o_c, sel = _nsa_cmp_select(d3, kcv, bias_cmp, agg_pad, B=B, S=S, n_cmp=n_cmp, ncp=ncp)
    o_s = _flash_attention("nsa", d3, 0, d3, 3, vals_t, 128, 2, bias_slc, (sel,), B=B, S=S)
    win_cfg = tuple((0, h % 2 == 1, 1) for h in range(NSA_HEADS))
    o_w = _banded_attention(d3, 0, d3, 4, None, None, bias_win, None, B=B, S=S, window=NSA_WINDOW, q_cfg=win_cfg)

    T = B * S
    return _merge(x, _row(p["mix_norm_g"][l]), y_a, y_b.reshape(T, 256), y_c.reshape(T, 256),
                  o_c.reshape(T, 256), o_s.reshape(T, 256), o_w.reshape(T, 256), d_all,
                  p["w_gate"][l].astype(BF16), p["b_gate"][l].astype(F32), p["w_branch"][l].astype(BF16),
                  p["w_out"][l].astype(BF16))


def _moe_layer(x, l, p, final_norm):
    w_router = jnp.concatenate([p["w_router_exp"][l], p["w_router_grp"][l]], axis=1)
    w_router = jnp.pad(w_router, ((0, 0), (0, LANES - w_router.shape[1]))).astype(F32)
    w_hi = w_router.astype(BF16)
    w_router = jnp.concatenate([w_hi, (w_router - w_hi.astype(F32)).astype(BF16)], axis=1)
    b_router = jnp.concatenate([p["b_router_exp"][l], p["b_router_grp"][l]])
    b_router = jnp.pad(b_router, (0, LANES - b_router.shape[0])).reshape(1, LANES).astype(F32)
    g = _row(p["ffn_norm_g"][l])
    fg = _row(p["final_norm_g"])
    wg, wu, wd = (p[k][l].astype(BF16) for k in ("w_exp_gate", "w_exp_up", "w_exp_down"))
    xd, pos_t, w_t, counts = _moe_dispatch(x, g, w_router, b_router)
    return lax.cond(jnp.max(counts) > MOE_CAP,
                    lambda: _moe(x, g, w_router, b_router, wg, wu, wd, fg, final_norm),
                    lambda: _moe_finish(x, xd, pos_t, w_t, counts, wg, wu, wd, fg, final_norm))


def _forward(p):
    x = p["x"]
    B, S, D = x.shape
    depth = p["w_in"].shape[0]
    assert D == D_MODEL and S % (16 * Q_BLOCK) == 0 and S // NSA_SLC_LEN <= LANES
    rel = p["rel_bias"].astype(F32)
    n_cmp = (S - NSA_CMP_LEN) // NSA_CMP_STRIDE + 1
    ncp = LANES + S // NSA_CMP_STRIDE
    agg = _selection_aggregator_np(n_cmp, S // NSA_SLC_LEN)
    agg_pad = np.zeros((ncp, LANES), np.float32)
    agg_pad[LANES:LANES + n_cmp, :agg.shape[1]] = agg
    tiles = (
        _bias_tiles(rel, _banded_buckets(1), 0, SWA_HEADS, False),
        _bias_tiles(rel, _flash_buckets(), SWA_HEADS, DIFF_HEADS, True, LOG2E),
        _bias_tiles(rel, _flash_buckets(), SWA_HEADS + DIFF_HEADS, NSA_HEADS, True, LOG2E),
        _bias_tiles(rel, _banded_buckets(NSA_WINDOW // Q_BLOCK), SWA_HEADS + DIFF_HEADS, NSA_HEADS, False),
        _bias_tiles(rel, _cmp_buckets(), SWA_HEADS + DIFF_HEADS, NSA_HEADS, True),
        jnp.asarray(agg_pad, BF16), n_cmp, ncp,
    )
    xt = x.reshape(B * S, D).astype(F32)
    for l in range(depth):
        xt = _mixer_layer(xt, l, B, S, tiles, p)
        xt = _moe_layer(xt, l, p, final_norm=(l == depth - 1))
    return xt.reshape(B, S, D).astype(x.dtype)


def kernel(x, rel_bias, final_norm_g, mix_norm_g, w_in, conv_w, conv_b, lru_w_a, lru_b_a, lru_w_x, lru_b_x,
           lru_lambda, swa_sinks, diff_lq1, diff_lk1, diff_lq2, diff_lk2, diff_subln_g, nsa_pos_k, nsa_w1_k,
           nsa_w2_k, nsa_pos_v, nsa_w1_v, nsa_w2_v, w_gate, b_gate, w_branch, w_out, ffn_norm_g, w_router_grp,
           b_router_grp, w_router_exp, b_router_exp, w_exp_gate, w_exp_up, w_exp_down):
    return _forward(dict(
        x=x, rel_bias=rel_bias, final_norm_g=final_norm_g, mix_norm_g=mix_norm_g, w_in=w_in, conv_w=conv_w,
        conv_b=conv_b, lru_w_a=lru_w_a, lru_b_a=lru_b_a, lru_w_x=lru_w_x, lru_b_x=lru_b_x, lru_lambda=lru_lambda,
        swa_sinks=swa_sinks, diff_lq1=diff_lq1, diff_lk1=diff_lk1, diff_lq2=diff_lq2, diff_lk2=diff_lk2,
        diff_subln_g=diff_subln_g, nsa_pos_k=nsa_pos_k, nsa_w1_k=nsa_w1_k, nsa_w2_k=nsa_w2_k, nsa_pos_v=nsa_pos_v,
        nsa_w1_v=nsa_w1_v, nsa_w2_v=nsa_w2_v, w_gate=w_gate, b_gate=b_gate, w_branch=w_branch, w_out=w_out,
        ffn_norm_g=ffn_norm_g, w_router_grp=w_router_grp, b_router_grp=b_router_grp, w_router_exp=w_router_exp,
        b_router_exp=b_router_exp, w_exp_gate=w_exp_gate, w_exp_up=w_exp_up, w_exp_down=w_exp_down))
```

```python
import functools
import math

import numpy as np
import jax
import jax.numpy as jnp
from jax import lax
from jax.experimental import pallas as pl
from jax.experimental.pallas import tpu as pltpu

F32 = jnp.float32
BF16 = jnp.bfloat16
I32 = jnp.int32

D_MODEL = 1024
HEAD_DIM = 64
MIX_WIDTH = 256
LRU_C = 8.0
CONV_WIDTH = 4
SWA_HEADS = 4
SWA_WINDOW = 128
DIFF_HEADS = 4
DIFF_DIM = 32
DIFF_SUBLN_EPS = 1e-5
NSA_HEADS = 4
NSA_CMP_LEN = 32
NSA_CMP_STRIDE = 16
NSA_SLC_LEN = 64
NSA_TOPK = 16
NSA_WINDOW = 512
NSA_FORCED_SCORE = 1e4
REL_BUCKETS = 32
REL_MAX_DIST = 128
N_GROUPS = 4
EXPERTS_PER_GROUP = 8
N_EXPERTS = 32
EXPERT_FF = 256
NORM_EPS = 1e-6
NEG_INF = -1e30
TINY = 1e-30
LOG2E = math.log2(math.e)

LANES = 128
Q_BLOCK = 128
CMP_Q = 512
BANDED_Q_BLOCKS = 4
FLASH_BLOCK = 256
MOE_TILE = 512
MOE_CAP = 128
MOE_TILES_PER_STEP = 8
FLASH_PV_LAG = 4
FLASH_MAX_GROWTH = 64.0
FLASH_FAR_BLOCKS = 8
N_IN_PAD = 2560
VMEM_LIMIT = 56 * 1024 * 1024

_NT = (((1,), (1,)), ((), ()))


def _cparams(*sem):
    return pltpu.CompilerParams(dimension_semantics=sem, vmem_limit_bytes=VMEM_LIMIT)


def _t5_bucket_np(dist):
    n = np.maximum(dist, 0)
    exact = REL_BUCKETS // 2
    scaled = (np.log(np.maximum(n, 1).astype(np.float32) / np.float32(exact))
              / np.float32(math.log(REL_MAX_DIST / exact))).astype(np.float32)
    large = np.minimum(exact + (scaled * np.float32(REL_BUCKETS - exact)).astype(np.int32), REL_BUCKETS - 1)
    return np.where(n < exact, n, large).astype(np.int32)


def _bias_tile_kernel(tab_ref, idx_ref, o_ref, *, head0, shift, scale):
    h = pl.program_id(0) + head0
    idx = idx_ref[0]
    acc = jnp.zeros(idx.shape, F32)
    for b in range(REL_BUCKETS):
        acc = jnp.where(idx == b, tab_ref[b, h], acc)
    if shift:
        acc = acc - tab_ref[REL_BUCKETS - 1, h]
    if scale != 1.0:
        acc = acc * scale
    o_ref[0, 0] = acc


def _bias_tiles(rel_bias, buckets, head0, nheads, shift, scale=1.0):
    R, M, N = buckets.shape
    return pl.pallas_call(
        functools.partial(_bias_tile_kernel, head0=head0, shift=shift, scale=scale),
        grid=(nheads, R),
        in_specs=[pl.BlockSpec(memory_space=pltpu.SMEM),
                  pl.BlockSpec((1, M, N), lambda h, r: (r, 0, 0))],
        out_specs=pl.BlockSpec((1, 1, M, N), lambda h, r: (h, r, 0, 0)),
        out_shape=jax.ShapeDtypeStruct((nheads, R, M, N), F32),
        compiler_params=_cparams("arbitrary", "arbitrary"),
        name="bias_tiles",
    )(rel_bias, jnp.asarray(buckets))


def _banded_buckets(nback):
    i = np.arange(BANDED_Q_BLOCKS * Q_BLOCK)[None, :]
    j = np.arange((nback + BANDED_Q_BLOCKS) * Q_BLOCK)[:, None]
    return _t5_bucket_np(i - j + nback * Q_BLOCK)[None]


def _flash_buckets():
    k = np.arange(FLASH_BLOCK)[:, None]
    q = np.arange(FLASH_BLOCK)[None, :]
    return np.stack([_t5_bucket_np(q - k), _t5_bucket_np(q - k + FLASH_BLOCK)])


def _cmp_buckets():
    i = np.arange(CMP_Q)[None, :, None]
    c2 = np.arange(2 * LANES)[None, None, :]
    r = np.arange(16 * LANES // CMP_Q)[:, None, None]
    dist = i - NSA_CMP_STRIDE * (c2 - LANES - (CMP_Q // NSA_CMP_STRIDE) * r) - (NSA_CMP_LEN - 1)
    return _t5_bucket_np(dist)


def _rms(x, g):
    return x * lax.rsqrt(jnp.mean(x * x, axis=-1, keepdims=True) + NORM_EPS) * g


def _proj_kernel(x_ref, g_ref, w_ref, wt_ref, oa_ref, ob_ref, oc_ref, od_ref, ot_ref):
    h = _rms(x_ref[...], g_ref[...]).astype(BF16)
    oa_ref[...] = jnp.dot(h, w_ref[:, 0:512], preferred_element_type=F32)
    ob_ref[...] = jnp.dot(h, w_ref[:, 512:1024], preferred_element_type=F32).astype(BF16)
    oc_ref[...] = jnp.dot(h, w_ref[:, 1024:1792], preferred_element_type=F32).astype(BF16)
    od_ref[...] = jnp.dot(h, w_ref[:, 1792:2560], preferred_element_type=F32).astype(BF16)
    ot_ref[...] = lax.dot_general(wt_ref[...], h, _NT, preferred_element_type=F32).astype(BF16)


def _in_proj(x, g, w, wt):
    T = x.shape[0]
    tm = 512
    nt = wt.shape[0]
    return pl.pallas_call(
        _proj_kernel,
        grid=(T // tm,),
        in_specs=[pl.BlockSpec((tm, D_MODEL), lambda i: (i, 0)),
                  pl.BlockSpec((1, D_MODEL), lambda i: (0, 0)),
                  pl.BlockSpec((D_MODEL, N_IN_PAD), lambda i: (0, 0)),
                  pl.BlockSpec((nt, D_MODEL), lambda i: (0, 0))],
        out_specs=[pl.BlockSpec((tm, 512), lambda i: (i, 0)),
                   pl.BlockSpec((tm, 512), lambda i: (i, 0)),
                   pl.BlockSpec((tm, 768), lambda i: (i, 0)),
                   pl.BlockSpec((tm, 768), lambda i: (i, 0)),
                   pl.BlockSpec((nt, tm), lambda i: (0, i))],
        out_shape=[jax.ShapeDtypeStruct((T, 512), F32),
                   jax.ShapeDtypeStruct((T, 512), BF16),
                   jax.ShapeDtypeStruct((T, 768), BF16),
                   jax.ShapeDtypeStruct((T, 768), BF16),
                   jax.ShapeDtypeStruct((nt, T), BF16)],
        compiler_params=_cparams("arbitrary"),
        name="in_proj",
    )(x, g, w, wt)


def _lru_kernel(xg_ref, cw_ref, cb_ref, wa_ref, ba_ref, wx_ref, bx_ref, lam_ref, o_ref,
                ext_ref, h_ref, *, Tt):
    t = pl.program_id(1)

    W = MIX_WIDTH

    @pl.when(t == 0)
    def _():
        ext_ref[0:8, :] = jnp.zeros((8, W), F32)
        h_ref[...] = jnp.zeros_like(h_ref)

    x = xg_ref[:, 0:W]
    gate = xg_ref[:, W:2 * W]
    ext_ref[8:Tt + 8, :] = x
    row = lax.broadcasted_iota(I32, (Tt, W), 0)
    xc = cb_ref[...] + x * cw_ref[CONV_WIDTH - 1:CONV_WIDTH, :]
    for s in range(1, CONV_WIDTH):
        xc = xc + ext_ref[8 - s:8 - s + Tt, :] * cw_ref[CONV_WIDTH - 1 - s:CONV_WIDTH - s, :]
    ext_ref[0:8, :] = ext_ref[Tt:Tt + 8, :]

    xcb = xc.astype(BF16)
    r = jax.nn.sigmoid(jnp.dot(xcb, wa_ref[...], preferred_element_type=F32) + ba_ref[...])
    ig = jax.nn.sigmoid(jnp.dot(xcb, wx_ref[...], preferred_element_type=F32) + bx_ref[...])
    z = -lam_ref[...]
    softplus = jnp.maximum(z, 0.0) + jnp.log1p(jnp.exp(-jnp.abs(z)))
    log_a = (-LRU_C) * r * softplus
    a = jnp.exp(log_a)
    mult = jnp.sqrt(1.0 - a * a)
    mult = jnp.where((row + t * Tt) == 0, 1.0, mult)
    b = mult * ig * xc

    d = 1
    while d < Tt:
        keep = row >= d
        b = jnp.where(keep, a * pltpu.roll(b, d, 0) + b, b)
        a = jnp.where(keep, a * pltpu.roll(a, d, 0), a)
        d *= 2
    hs = a * h_ref[7:8, :] + b
    h_ref[...] = hs[Tt - 8:Tt]
    o_ref[...] = (hs * jax.nn.gelu(gate)).astype(o_ref.dtype)


def _rglru(a_xg, conv_w, conv_b, wa_bd, b_a, wx_bd, b_x, lam, B, S):
    Tt = 256
    W = MIX_WIDTH
    vec = pl.BlockSpec((1, W), lambda b, t: (0, 0))
    mat = pl.BlockSpec((W, W), lambda b, t: (0, 0))
    out = pl.pallas_call(
        functools.partial(_lru_kernel, Tt=Tt),
        grid=(B, S // Tt),
        in_specs=[pl.BlockSpec((None, Tt, 2 * W), lambda b, t: (b, t, 0)),
                  pl.BlockSpec((CONV_WIDTH, W), lambda b, t: (0, 0)),
                  vec, mat, vec, mat, vec, vec],
        out_specs=pl.BlockSpec((None, Tt, W), lambda b, t: (b, t, 0)),
        out_shape=jax.ShapeDtypeStruct((B, S, W), BF16),
        scratch_shapes=[pltpu.VMEM((Tt + 8, W), F32), pltpu.VMEM((8, W), F32)],
        compiler_params=_cparams("arbitrary", "arbitrary"),
        name="rglru",
    )(a_xg.reshape(B, S, 2 * W), conv_w, conv_b, wa_bd, b_a, wx_bd, b_x, lam)
    return out.reshape(B * S, W)


def _lane_window(q128, lo, width, roll, scale):
    q = q128 * scale
    if roll:
        q = pltpu.roll(q, LANES // 2, 1)
    lane = lax.broadcasted_iota(I32, q.shape, 1)
    return jnp.where((lane >= lo) & (lane < lo + width), q, 0.0).astype(BF16)


def _pair_halves(even, odd, even_half, odd_half):
    if even_half == 1:
        even = pltpu.roll(even, LANES // 2, 1)
    if odd_half == 0:
        odd = pltpu.roll(odd, LANES // 2, 1)
    lane = lax.broadcasted_iota(I32, even.shape, 1)
    return jnp.where(lane < LANES // 2, even, odd)


def _banded_kernel(*refs, nback, window, q_cfg, has_sink, shared_kv):
    nk = nback + BANDED_Q_BLOCKS
    TQ = BANDED_Q_BLOCKS * Q_BLOCK
    q_ref = refs[0]
    k_refs = refs[1:1 + nk]
    pos = 1 + nk
    if shared_kv:
        v_refs = k_refs
    else:
        v_refs = refs[pos:pos + nk]
        pos += nk
    bias_ref = refs[pos]
    pos += 1
    sink_ref = None
    if has_sink:
        sink_ref = refs[pos]
        pos += 1
    o_ref = refs[pos]

    n = pl.program_id(1)
    KW = nk * Q_BLOCK
    kcat = jnp.concatenate([k_refs[nk - 1 - i][...] for i in range(nk)], axis=0)
    vcat = kcat if shared_kv else jnp.concatenate([v_refs[nk - 1 - i][...] for i in range(nk)], axis=0)
    vt = vcat.astype(F32).T.astype(BF16)
    jj = lax.broadcasted_iota(I32, (KW, TQ), 0)
    ii = lax.broadcasted_iota(I32, (KW, TQ), 1)
    dist = ii - jj + nback * Q_BLOCK
    valid = (dist >= 0) & (dist < window) & (jj >= (nback - BANDED_Q_BLOCKS * n) * Q_BLOCK)
    scale = HEAD_DIM ** -0.5

    def logits(h):
        lo, roll, _ = q_cfg[h]
        g = h // 2
        qx = _lane_window(q_ref[:, LANES * g:LANES * (g + 1)].astype(F32), lo, HEAD_DIM, roll, scale)
        s = lax.dot_general(kcat, qx, _NT, preferred_element_type=F32) + bias_ref[h, 0]
        return jnp.where(valid, s, NEG_INF)

    def softmax(h, s):
        m = jnp.max(s, axis=0, keepdims=True)
        if has_sink:
            sink = sink_ref[0:1, h:h + 1]
            m = jnp.maximum(m, sink)
        p = jnp.exp(s - m)
        den = jnp.sum(p, axis=0, keepdims=True)
        if has_sink:
            den = den + jnp.exp(sink - m)
        return p.astype(BF16), 1.0 / jnp.maximum(den, TINY)

    def values(h, p, inv):
        half = q_cfg[h][2]
        return jnp.dot(vt[HEAD_DIM * half:HEAD_DIM * (half + 1)], p, preferred_element_type=F32) * inv

    nh = len(q_cfg)
    ss = [logits(h) for h in range(nh)]
    pr = [softmax(h, ss[h]) for h in range(nh)]
    outs = [values(h, *pr[h]) for h in range(nh)]
    for g in range(nh // 2):
        pair = jnp.concatenate([outs[2 * g], outs[2 * g + 1]], axis=0)
        o_ref[:, LANES * g:LANES * (g + 1)] = pair.T.astype(o_ref.dtype)


def _banded_attention(qsrc, q_col, ksrc, k_col, vsrc, v_col, bias, sinks, *, B, S, window, q_cfg):
    nback = -(-window // Q_BLOCK)
    nq = BANDED_Q_BLOCKS
    nk = nback + nq
    TQ = nq * Q_BLOCK
    shared_kv = vsrc is None
    has_sink = sinks is not None

    def kv_spec(col, i):
        return pl.BlockSpec((None, Q_BLOCK, LANES), lambda b, n: (b, jnp.maximum(nq * n + nq - 1 - i, 0), col))

    in_specs = [pl.BlockSpec((None, TQ, 2 * LANES), lambda b, n: (b, n, q_col))]
    args = [qsrc]
    in_specs += [kv_spec(k_col, i) for i in range(nk)]
    args += [ksrc] * nk
    if not shared_kv:
        in_specs += [kv_spec(v_col, i) for i in range(nk)]
        args += [vsrc] * nk
    in_specs.append(pl.BlockSpec(bias.shape, lambda b, n: (0, 0, 0, 0)))
    args.append(bias)
    if has_sink:
        in_specs.append(pl.BlockSpec(sinks.shape, lambda b, n: (0, 0)))
        args.append(sinks)
    return pl.pallas_call(
        functools.partial(_banded_kernel, nback=nback, window=window, q_cfg=q_cfg,
                          has_sink=has_sink, shared_kv=shared_kv),
        grid=(B, S // TQ),
        in_specs=in_specs,
        out_specs=pl.BlockSpec((None, TQ, 2 * LANES), lambda b, n: (b, n, 0)),
        out_shape=jax.ShapeDtypeStruct((B, S, 2 * LANES), BF16),
        compiler_params=_cparams("arbitrary", "arbitrary"),
        name="banded_attn",
    )(*args)


def _flash_kernel(*refs, mode, lambda_init):
    TB = FLASH_BLOCK
    if mode == "diff":
        (q_ref, k_ref, vt_ref, bias_ref, lam_ref, g_ref, o_ref,
         qx_ref, m_ref, l_ref, acc_ref) = refs
        sel_ref = None
        maps = [dict(g=h // 2, lo=(h % 2) * 64 + 32 * mm, w=DIFF_DIM, roll=False, kg=h // 2, vrow=HEAD_DIM * h, h=h)
                for h in range(DIFF_HEADS) for mm in range(2)]
        scale = DIFF_DIM ** -0.5
    else:
        (q_ref, k_ref, vt_ref, pen_ref, blk_ref, bias_ref, o_ref,
         qx_ref, m_ref, l_ref, acc_ref) = refs
        sel_ref = pen_ref
        maps = [dict(g=h // 2, lo=0, w=HEAD_DIM, roll=(h % 2 == 1), kg=0, vrow=HEAD_DIM, h=h)
                for h in range(NSA_HEADS)]
        scale = HEAD_DIM ** -0.5
    n = pl.program_id(1)

    for mp, c in enumerate(maps):
        qx_ref[mp, :, 0:LANES] = _lane_window(q_ref[:, LANES * c["g"]:LANES * (c["g"] + 1)].astype(F32),
                                              c["lo"], c["w"], c["roll"], scale * LOG2E)
        if sel_ref is not None:
            qx_ref[mp, :, LANES:2 * LANES] = pen_ref[...]
    m_ref[...] = jnp.full(m_ref.shape, NEG_INF, F32)
    l_ref[...] = jnp.zeros_like(l_ref)
    acc_ref[...] = jnp.zeros_like(acc_ref)

    key = lax.broadcasted_iota(I32, (TB, TB), 0)
    qry = lax.broadcasted_iota(I32, (TB, TB), 1)

    def step(j, kind, TK=TB):
        koff = pl.multiple_of(j * TB, TB)
        valid = None
        if kind == "diag":
            valid = qry >= key

        def logits(mp):
            c = maps[mp]
            kt = k_ref[pl.ds(koff, TK), LANES * c["kg"]:LANES * (c["kg"] + 1)]
            if sel_ref is not None:
                kt = jnp.concatenate([kt, blk_ref[pl.ds(koff, TK), :]], axis=1)
            s = lax.dot_general(kt, qx_ref[mp], _NT, preferred_element_type=F32)
            if kind == "near":
                s = s + bias_ref[c["h"], 1]
            elif kind == "diag":
                s = s + bias_ref[c["h"], 0]
            if valid is not None:
                s = jnp.where(valid, s, NEG_INF)
            return s

        def softmax(mp, s):
            m_prev = m_ref[mp]
            m_new = jnp.maximum(m_prev, jnp.max(s, axis=0, keepdims=True))
            alpha = jnp.exp2(m_prev - m_new)
            p = jnp.exp2(s - m_new)
            l_ref[mp] = alpha * l_ref[mp] + jnp.sum(p, axis=0, keepdims=True)
            m_ref[mp] = m_new
            return p.astype(BF16), alpha

        def values(mp, p, alpha):
            c = maps[mp]
            vt = vt_ref[c["vrow"]:c["vrow"] + HEAD_DIM, pl.ds(koff, TK)]
            acc_ref[mp] = alpha * acc_ref[mp] + jnp.dot(vt, p, preferred_element_type=F32)

        nm = len(maps)
        ss = [logits(mp) for mp in range(nm)]

        def two_pass():
            pa = [softmax(mp, ss[mp]) for mp in range(nm)]
            for mp in range(nm):
                values(mp, *pa[mp])

        if kind == "diag":
            two_pass()
            return

        stabs = [m_ref[mp] for mp in range(nm)]
        def zero_from(v):
            u = lax.bitcast_convert_type(v[0:1, :], jnp.uint32)
            return lax.shift_right_logical(lax.shift_right_logical(u, jnp.uint32(16)), jnp.uint32(16)).astype(F32)

        def value_product(mp, p):
            return jnp.dot(vt_ref[maps[mp]["vrow"]:maps[mp]["vrow"] + HEAD_DIM, pl.ds(koff, TK)], p.astype(BF16),
                           preferred_element_type=F32)

        ps, pvs = [], []
        for mp in range(nm):
            stab = stabs[mp] if mp < FLASH_PV_LAG else stabs[mp] + zero_from(pvs[mp - FLASH_PV_LAG])
            ps.append(jnp.exp2(ss[mp] - stab))
            pvs.append(value_product(mp, ps[mp]))
        m_news = [jnp.maximum(stabs[mp], jnp.max(ss[mp], axis=0, keepdims=True)) for mp in range(nm)]
        growth = m_news[0] - stabs[0]
        for mp in range(1, nm):
            growth = jnp.maximum(growth, m_news[mp] - stabs[mp])
        safe = jnp.max(growth) <= FLASH_MAX_GROWTH
        psums = [jnp.sum(ps[mp], axis=0, keepdims=True) for mp in range(nm)]

        @pl.when(safe)
        def _():
            for mp in range(nm):
                alpha = jnp.exp2(stabs[mp] - m_news[mp])
                l_ref[mp] = (l_ref[mp] + psums[mp]) * alpha
                acc_ref[mp] = (acc_ref[mp] + pvs[mp]) * alpha
                m_ref[mp] = m_news[mp]

        @pl.when(jnp.logical_not(safe))
        def _():
            two_pass()

    step(n, "diag")

    @pl.when(n >= 1)
    def _():
        step(n - 1, "near")

    nfar = jnp.maximum(n - 1, 0)
    FW = FLASH_FAR_BLOCKS * (1 if mode == "diff" else 2)

    def far_body(j, carry):
        step(FW * j, "far", FW * TB)
        return carry

    lax.fori_loop(0, nfar // FW, far_body, 0)

    done = (nfar // FW) * FW
    rest = nfar - done
    size = FW // 2
    while size >= 1:
        def tail_step(size=size):
            step(done + rest - rest % (2 * size), "far", size * TB)

        pl.when(rest % (2 * size) >= size)(tail_step)
        size //= 2

    def normalised(mp):
        return acc_ref[mp] * (1.0 / jnp.maximum(l_ref[mp], TINY))

    if mode == "diff":
        lq = lam_ref[...]
        lam = (jnp.exp(jnp.sum(lq[0:1] * lq[1:2], axis=1, keepdims=True))
               - jnp.exp(jnp.sum(lq[2:3] * lq[3:4], axis=1, keepdims=True)) + lambda_init)
        for g in range(DIFF_HEADS // 2):
            res = []
            for h in (2 * g, 2 * g + 1):
                o = normalised(2 * h) - lam * normalised(2 * h + 1)
                ms = jnp.sum(o * o, axis=0, keepdims=True) * (1.0 / (2 * DIFF_DIM))
                res.append(o * lax.rsqrt(ms + DIFF_SUBLN_EPS))
            out = jnp.concatenate(res, axis=0) * g_ref[...] * (1.0 - lambda_init)
            o_ref[:, LANES * g:LANES * (g + 1)] = out.T.astype(o_ref.dtype)
    else:
        for g in range(NSA_HEADS // 2):
            out = jnp.concatenate([normalised(2 * g), normalised(2 * g + 1)], axis=0)
            o_ref[:, LANES * g:LANES * (g + 1)] = out.T.astype(o_ref.dtype)


def _flash_attention(mode, qsrc, q_col, ksrc, k_col, vt, vt_rows, vt_blk, bias, extra, *, B, S, lambda_init=0.0):
    TB = FLASH_BLOCK
    q_spec = pl.BlockSpec((None, TB, 2 * LANES), lambda b, n: (b, n, q_col))
    bias_spec = pl.BlockSpec(bias.shape, lambda b, n: (0, 0, 0, 0))
    vt_spec = pl.BlockSpec((vt_rows, S), lambda b, n: (vt_blk, b))
    if mode == "diff":
        lam4, gsub = extra
        nmaps = 2 * DIFF_HEADS
        in_specs = [q_spec,
                    pl.BlockSpec((None, S, 2 * LANES), lambda b, n: (b, 0, k_col)),
                    vt_spec, bias_spec,
                    pl.BlockSpec(lam4.shape, lambda b, n: (0, 0)),
                    pl.BlockSpec(gsub.shape, lambda b, n: (0, 0))]
        args = [qsrc, ksrc, vt, bias, lam4, gsub]
    else:
        (pen,) = extra
        nmaps = NSA_HEADS
        blk = jnp.asarray(np.arange(S)[:, None] // NSA_SLC_LEN == np.arange(LANES)[None, :], BF16)
        in_specs = [q_spec,
                    pl.BlockSpec((None, S, LANES), lambda b, n: (b, 0, k_col)),
                    vt_spec,
                    pl.BlockSpec((None, TB, LANES), lambda b, n: (b, n, 0)),
                    pl.BlockSpec((S, LANES), lambda b, n: (0, 0)),
                    bias_spec]
        args = [qsrc, ksrc, vt, pen, blk, bias]
    return pl.pallas_call(
        functools.partial(_flash_kernel, mode=mode, lambda_init=lambda_init),
        grid=(B, S // TB),
        in_specs=in_specs,
        out_specs=pl.BlockSpec((None, TB, 2 * LANES), lambda b, n: (b, n, 0)),
        out_shape=jax.ShapeDtypeStruct((B, S, 2 * LANES), BF16),
        scratch_shapes=[pltpu.VMEM((nmaps, TB, LANES if mode == "diff" else 2 * LANES), BF16),
                        pltpu.VMEM((nmaps, 1, TB), F32),
                        pltpu.VMEM((nmaps, 1, TB), F32),
                        pltpu.VMEM((nmaps, HEAD_DIM, TB), F32)],
        compiler_params=_cparams("arbitrary", "arbitrary"),
        name="flash_" + mode,
    )(*args)


def _compress_kernel(ck_ref, cv_ref, pk_ref, pv_ref, w1k_ref, w1v_ref, w2_ref, o_ref):
    half = NSA_CMP_STRIDE * HEAD_DIM
    ck = ck_ref[...]
    cv = cv_ref[...]
    nrow = ck.shape[0]

    def mm(a, w):
        return jnp.dot(a, w, preferred_element_type=F32)

    top = mm(ck, w1k_ref[0:half, :]) + mm(cv, w1v_ref[0:half, :])
    bot = mm(ck, w1k_ref[half:2 * half, :]) + mm(cv, w1v_ref[half:2 * half, :])
    posw = (mm(pk_ref[...], w1k_ref[...]) + mm(pv_ref[...], w1v_ref[...]))[0:1]
    pre = top + pltpu.roll(bot, nrow - 1, 0) + posw
    kcv = mm(jax.nn.gelu(pre).astype(BF16), w2_ref[...])
    o_ref[0:LANES, :] = jnp.zeros((LANES, LANES), o_ref.dtype)
    o_ref[LANES:LANES + nrow, :] = kcv.astype(o_ref.dtype)


def _nsa_compress(ck, cv, pk, pv, w1k, w1v, w2, ncp):
    B, nchunk, width = ck.shape
    assert ncp == LANES + nchunk

    def full(a):
        return pl.BlockSpec(a.shape, lambda b: (0,) * a.ndim)

    chunk = pl.BlockSpec((None, nchunk, width), lambda b: (b, 0, 0))
    return pl.pallas_call(
        _compress_kernel,
        grid=(B,),
        in_specs=[chunk, chunk, full(pk), full(pv), full(w1k), full(w1v), full(w2)],
        out_specs=pl.BlockSpec((None, ncp, LANES), lambda b: (b, 0, 0)),
        out_shape=jax.ShapeDtypeStruct((B, ncp, LANES), BF16),
        compiler_params=_cparams("arbitrary"),
        name="nsa_compress",
    )(ck, cv, pk, pv, w1k, w1v, w2)


def _cmp_kernel(q_ref, kcv_ref, bias_ref, agg_ref, oc_ref, sel_ref, s_scr, *, n_cmp, ncp):
    n = pl.program_id(1)
    QB = CMP_Q
    kcv = kcv_ref[...]
    cp = lax.broadcasted_iota(I32, (QB, ncp), 1)
    qpos = n * QB + lax.broadcasted_iota(I32, (QB, ncp), 0)
    valid = ((cp >= LANES) & (cp < LANES + n_cmp)
             & (qpos - NSA_CMP_STRIDE * (cp - LANES) - (NSA_CMP_LEN - 1) >= 0))
    woff = pl.multiple_of((n // (16 * LANES // CMP_Q)) * LANES, LANES)
    scale = HEAD_DIM ** -0.5

    def logits(h):
        g = h // 2
        qx = _lane_window(q_ref[:, LANES * g:LANES * (g + 1)].astype(F32), 0, HEAD_DIM, h % 2 == 1, scale)
        s_scr[h] = lax.dot_general(qx, kcv, _NT, preferred_element_type=F32)
        s_scr[h, :, pl.ds(woff, 2 * LANES)] += bias_ref[h]

    def softmax(h):
        s = jnp.where(valid, s_scr[h], NEG_INF)
        m = jnp.max(s, axis=1, keepdims=True)
        p = jnp.where(valid, jnp.exp(s - m), 0.0)
        den = jnp.sum(p, axis=1, keepdims=True)
        return p * (1.0 / jnp.maximum(den, TINY))

    for h in range(NSA_HEADS):
        logits(h)
    ps = [softmax(h) for h in range(NSA_HEADS)]
    outs = [jnp.dot(p.astype(BF16), kcv, preferred_element_type=F32) for p in ps]
    for g in range(NSA_HEADS // 2):
        oc_ref[:, LANES * g:LANES * (g + 1)] = _pair_halves(outs[2 * g], outs[2 * g + 1], 1, 1).astype(oc_ref.dtype)

    psum = (ps[0] + ps[1]) + (ps[2] + ps[3])
    imp = jnp.dot(psum.astype(BF16), agg_ref[...], preferred_element_type=F32)

    v = imp.T
    jb = lax.broadcasted_iota(I32, v.shape, 0)
    qp = n * QB + lax.broadcasted_iota(I32, v.shape, 1)
    cur = qp >> 6
    forced = (jb == 0) | (jb == cur) | (jb == cur - 1)
    v = jnp.where(forced, -jnp.inf, jnp.where(jb * NSA_SLC_LEN <= qp, v, -1.0))
    rid = jb.astype(F32)
    picked = jnp.where(forced, 1.0, 0.0)
    for _ in range(NSA_TOPK - 3):
        mx = jnp.max(v, axis=0, keepdims=True)
        first = jnp.min(jnp.where(v == mx, rid, float(LANES)), axis=0, keepdims=True)
        hit = rid == first
        picked = jnp.where(hit, 1.0, picked)
        v = jnp.where(hit, -jnp.inf, v)
    sel_ref[...] = jnp.where(picked.T > 0.5, 0.0, NEG_INF).astype(sel_ref.dtype)


def _nsa_cmp_select(dsrc, kcv, bias, agg, *, B, S, n_cmp, ncp):
    return pl.pallas_call(
        functools.partial(_cmp_kernel, n_cmp=n_cmp, ncp=ncp),
        grid=(B, S // CMP_Q),
        in_specs=[pl.BlockSpec((None, CMP_Q, 2 * LANES), lambda b, n: (b, n, 0)),
                  pl.BlockSpec((None, ncp, LANES), lambda b, n: (b, 0, 0)),
                  pl.BlockSpec((NSA_HEADS, None, CMP_Q, 2 * LANES), lambda b, n: (0, n % (16 * LANES // CMP_Q), 0, 0)),
                  pl.BlockSpec((ncp, LANES), lambda b, n: (0, 0))],
        out_specs=[pl.BlockSpec((None, CMP_Q, 2 * LANES), lambda b, n: (b, n, 0)),
                   pl.BlockSpec((None, CMP_Q, LANES), lambda b, n: (b, n, 0))],
        out_shape=[jax.ShapeDtypeStruct((B, S, 2 * LANES), BF16),
                   jax.ShapeDtypeStruct((B, S, LANES), BF16)],
        scratch_shapes=[pltpu.VMEM((NSA_HEADS, CMP_Q, ncp), F32)],
        compiler_params=_cparams("arbitrary", "arbitrary"),
        name="nsa_cmp_select",
    )(dsrc, kcv, bias, agg)


def _merge_kernel(x_ref, g_ref, ya_ref, yb_ref, yc_ref, oc_ref, os_ref, ow_ref, dg_ref,
                  wg_ref, bg_ref, wb_ref, wo_ref, o_ref):
    x = x_ref[...]
    h = _rms(x, g_ref[...]).astype(BF16)
    tm = x.shape[0]
    lane = lax.broadcasted_iota(I32, (tm, LANES), 1)
    gd = jax.nn.sigmoid(dg_ref[...].astype(F32))
    yd = []
    for g in range(NSA_HEADS // 2):
        acc = jnp.zeros((tm, LANES), F32)
        for br, src in enumerate((oc_ref, os_ref, ow_ref)):
            ge = gd[:, 3 * (2 * g) + br:3 * (2 * g) + br + 1]
            go = gd[:, 3 * (2 * g + 1) + br:3 * (2 * g + 1) + br + 1]
            acc = acc + jnp.where(lane < 64, ge, go) * src[:, LANES * g:LANES * (g + 1)].astype(F32)
        yd.append(acc)
    y_d = jnp.concatenate(yd, axis=1).astype(BF16)
    ys = (ya_ref[...], yb_ref[...], yc_ref[...], y_d)
    merged = jnp.zeros((tm, D_MODEL), F32)
    for br in range(4):
        gate = jax.nn.sigmoid(jnp.dot(h, wg_ref[br], preferred_element_type=F32) + bg_ref[br:br + 1, :])
        merged = merged + gate * jnp.dot(ys[br], wb_ref[br], preferred_element_type=F32)
    o_ref[...] = x + jnp.dot(merged.astype(BF16), wo_ref[...], preferred_element_type=F32)


def _merge(x, g, ya, yb, yc, oc, osel, ow, dproj, wg, bg, wb, wo):
    T = x.shape[0]
    tm = 256

    def tile(width, col=0):
        return pl.BlockSpec((tm, width), lambda i: (i, col))

    def full(a):
        return pl.BlockSpec(a.shape, lambda i: (0,) * a.ndim)

    return pl.pallas_call(
        _merge_kernel,
        grid=(T // tm,),
        in_specs=[tile(D_MODEL), full(g), tile(256), tile(256), tile(256), tile(256), tile(256), tile(256),
                  tile(LANES, 5), full(wg), full(bg), full(wb), full(wo)],
        out_specs=tile(D_MODEL),
        out_shape=jax.ShapeDtypeStruct((T, D_MODEL), F32),
        compiler_params=_cparams("arbitrary"),
        name="merge",
    )(x, g, ya, yb, yc, oc, osel, ow, dproj, wg, bg, wb, wo)


def _route(hn, w_split, b_router):
    lane = lax.broadcasted_iota(I32, (hn.shape[0], LANES), 1)
    hi = hn.astype(BF16)
    lo = (hn - hi.astype(F32)).astype(BF16)
    a = jnp.dot(hi, w_split, preferred_element_type=F32)
    b = jnp.dot(lo, w_split[:, 0:LANES], preferred_element_type=F32)
    logits = (a[:, 0:LANES] + a[:, LANES:2 * LANES]) + b + b_router
    is_grp = (lane >= N_EXPERTS) & (lane < N_EXPERTS + N_GROUPS)
    lanef = lane.astype(F32)
    lg = jnp.where(is_grp, logits, -jnp.inf)
    mg = jnp.max(lg, axis=1, keepdims=True)
    gsel = jnp.min(jnp.where(lg == mg, lanef, 2.0 * LANES), axis=1, keepdims=True) - N_EXPERTS
    g_w = 1.0 / jnp.sum(jnp.where(is_grp, jnp.exp(lg - mg), 0.0), axis=1, keepdims=True)
    in_grp = (lane < N_EXPERTS) & ((lane >> 3).astype(F32) == gsel)
    v1 = jnp.where(in_grp, logits, -jnp.inf)
    m1 = jnp.max(v1, axis=1, keepdims=True)
    i1 = jnp.min(jnp.where(v1 == m1, lanef, 2.0 * LANES), axis=1, keepdims=True)
    v2 = jnp.where(lanef == i1, -jnp.inf, v1)
    m2 = jnp.max(v2, axis=1, keepdims=True)
    i2 = jnp.min(jnp.where(v2 == m2, lanef, 2.0 * LANES), axis=1, keepdims=True)
    e2 = jnp.exp(m2 - m1)
    w1 = g_w / (1.0 + e2)
    w2 = g_w * e2 / (1.0 + e2)
    first, second = lanef == i1, lanef == i2
    comb = jnp.where(first, w1, 0.0) + jnp.where(second, w2, 0.0)
    return comb, jnp.where(first | second, 1.0, 0.0)


def _moe_kernel(x_ref, g_ref, wr_ref, br_ref, wg_ref, wu_ref, wd_ref, fg_ref, o_ref,
                hn_ref, comb_ref, acc_ref, *, final_norm):
    e = pl.program_id(1)
    tm = x_ref.shape[0]
    lane = lax.broadcasted_iota(I32, (tm, LANES), 1)

    @pl.when(e == 0)
    def _():
        hn = _rms(x_ref[...], g_ref[...])
        hn_ref[...] = hn.astype(BF16)
        comb_ref[...] = _route(hn, wr_ref[...], br_ref[...])[0]
        acc_ref[...] = jnp.zeros_like(acc_ref)

    hn = hn_ref[...]
    gt = jnp.dot(hn, wg_ref[...], preferred_element_type=F32)
    up = jnp.dot(hn, wu_ref[...], preferred_element_type=F32)
    he = (gt * jax.nn.sigmoid(gt) * up).astype(BF16)
    ye = jnp.dot(he, wd_ref[...], preferred_element_type=F32)
    cw = jnp.sum(jnp.where(lane == e, comb_ref[...], 0.0), axis=1, keepdims=True)
    acc_ref[...] += cw * ye

    @pl.when(e == N_EXPERTS - 1)
    def _():
        y = x_ref[...] + acc_ref[...]
        if final_norm:
            y = _rms(y, fg_ref[...])
        o_ref[...] = y


def _moe(x, g, w_router, b_router, wg, wu, wd, final_g, final_norm):
    T = x.shape[0]
    tm = 1024
    return pl.pallas_call(
        functools.partial(_moe_kernel, final_norm=final_norm),
        grid=(T // tm, N_EXPERTS),
        in_specs=[pl.BlockSpec((tm, D_MODEL), lambda i, e: (i, 0)),
                  pl.BlockSpec((1, D_MODEL), lambda i, e: (0, 0)),
                  pl.BlockSpec((D_MODEL, 2 * LANES), lambda i, e: (0, 0)),
                  pl.BlockSpec((1, LANES), lambda i, e: (0, 0)),
                  pl.BlockSpec((None, D_MODEL, EXPERT_FF), lambda i, e: (e, 0, 0)),
                  pl.BlockSpec((None, D_MODEL, EXPERT_FF), lambda i, e: (e, 0, 0)),
                  pl.BlockSpec((None, EXPERT_FF, D_MODEL), lambda i, e: (e, 0, 0)),
                  pl.BlockSpec((1, D_MODEL), lambda i, e: (0, 0))],
        out_specs=pl.BlockSpec((tm, D_MODEL), lambda i, e: (i, 0)),
        out_shape=jax.ShapeDtypeStruct((T, D_MODEL), F32),
        scratch_shapes=[pltpu.VMEM((tm, D_MODEL), BF16),
                        pltpu.VMEM((tm, LANES), F32),
                        pltpu.VMEM((tm, D_MODEL), F32)],
        compiler_params=_cparams("arbitrary", "arbitrary"),
        name="moe",
    )(x, g, w_router, b_router, wg, wu, wd, final_g)


def _slot_matrix(pos_t, val_t, ntok, half, experts=range(N_EXPERTS)):
    nslot = MOE_CAP // 2
    slot = (lax.broadcasted_iota(I32, (nslot, ntok), 0) + half * nslot).astype(F32)
    blocks = []
    for e in experts:
        hit = pos_t[e:e + 1, :] == slot
        blocks.append(jnp.where(hit, 1.0 if val_t is None else val_t[e:e + 1, :], 0.0).astype(BF16))
    return jnp.concatenate(blocks, axis=0)


ROUTE_ROWS = 48


def _route_t(hn, w_split_t, b_col):
    nr = ROUTE_ROWS
    hi = hn.astype(BF16)
    lo = (hn - hi.astype(F32)).astype(BF16)
    a = lax.dot_general(w_split_t, hi, _NT, preferred_element_type=F32)
    b = lax.dot_general(w_split_t[0:LANES], lo, _NT, preferred_element_type=F32)
    logits = (a[0:nr] + a[LANES:LANES + nr]) + b[0:nr] + b_col[0:nr]
    row = lax.broadcasted_iota(I32, logits.shape, 0)
    rowf = row.astype(F32)
    is_grp = (row >= N_EXPERTS) & (row < N_EXPERTS + N_GROUPS)
    lg = jnp.where(is_grp, logits, -jnp.inf)
    mg = jnp.max(lg, axis=0, keepdims=True)
    gsel = jnp.min(jnp.where(lg == mg, rowf, 2.0 * LANES), axis=0, keepdims=True) - N_EXPERTS
    g_w = 1.0 / jnp.sum(jnp.where(is_grp, jnp.exp(lg - mg), 0.0), axis=0, keepdims=True)
    in_grp = (row < N_EXPERTS) & ((row >> 3).astype(F32) == gsel)
    v1 = jnp.where(in_grp, logits, -jnp.inf)
    m1 = jnp.max(v1, axis=0, keepdims=True)
    i1 = jnp.min(jnp.where(v1 == m1, rowf, 2.0 * LANES), axis=0, keepdims=True)
    v2 = jnp.where(rowf == i1, -jnp.inf, v1)
    m2 = jnp.max(v2, axis=0, keepdims=True)
    i2 = jnp.min(jnp.where(v2 == m2, rowf, 2.0 * LANES), axis=0, keepdims=True)
    e2 = jnp.exp(m2 - m1)
    w1 = g_w / (1.0 + e2)
    w2 = g_w * e2 / (1.0 + e2)
    first, second = rowf == i1, rowf == i2
    comb = jnp.where(first, w1, 0.0) + jnp.where(second, w2, 0.0)
    return comb, jnp.where(first | second, 1.0, 0.0)


def _dispatch_kernel(x_ref, g_ref, wr_ref, br_ref, tri_ref, xd_ref, pos_ref, wt_ref, cnt_ref):
    tm = x_ref.shape[0]
    nr = ROUTE_ROWS
    hn = _rms(x_ref[...], g_ref[...])
    comb_t, routed_t = _route_t(hn, wr_ref[...], br_ref[...])
    rank_t = jnp.dot(routed_t.astype(BF16), tri_ref[...], preferred_element_type=F32)
    pos_t = jnp.where(routed_t > 0.0, rank_t, -1.0)
    pos_ref[0:nr, :] = pos_t
    pos_ref[nr:LANES, :] = jnp.full((LANES - nr, tm), -1.0, F32)
    wt_ref[0:nr, :] = comb_t
    wt_ref[nr:LANES, :] = jnp.zeros((LANES - nr, tm), F32)
    counts = jnp.sum(routed_t, axis=1, keepdims=True)
    cnt_ref[0:nr, :] = jnp.broadcast_to(counts, (nr, LANES))
    cnt_ref[nr:LANES, :] = jnp.zeros((LANES - nr, LANES), F32)
    hnb = hn.astype(BF16)
    nslot = MOE_CAP // 2
    lower = jnp.dot(_slot_matrix(pos_t, None, tm, 0), hnb, preferred_element_type=F32)
    xd_ref[:, 0:nslot, :] = lower.reshape(N_EXPERTS, nslot, D_MODEL).astype(xd_ref.dtype)

    for grp in range(N_GROUPS):
        e0 = grp * EXPERTS_PER_GROUP
        busy = jnp.max(counts[e0:e0 + EXPERTS_PER_GROUP]) > nslot

        def fill(e0=e0):
            upper = jnp.dot(_slot_matrix(pos_t, None, tm, 1, range(e0, e0 + EXPERTS_PER_GROUP)), hnb,
                            preferred_element_type=F32)
            xd_ref[e0:e0 + EXPERTS_PER_GROUP, nslot:MOE_CAP, :] = (
                upper.reshape(EXPERTS_PER_GROUP, nslot, D_MODEL).astype(xd_ref.dtype))

        def clear(e0=e0):
            xd_ref[e0:e0 + EXPERTS_PER_GROUP, nslot:MOE_CAP, :] = jnp.zeros((EXPERTS_PER_GROUP, nslot, D_MODEL),
                                                                           xd_ref.dtype)

        pl.when(busy)(fill)
        pl.when(jnp.logical_not(busy))(clear)


def _expert_kernel(busy_ref, xlo_ref, xhi_ref, wg_ref, wu_ref, wd_ref, olo_ref, ohi_ref):
    ntile, nslot, _ = xlo_ref.shape

    def ffn(x_ref, o_ref):
        x = x_ref[...].reshape(ntile * nslot, D_MODEL)
        gt = jnp.dot(x, wg_ref[...], preferred_element_type=F32)
        up = jnp.dot(x, wu_ref[...], preferred_element_type=F32)
        he = (gt * jax.nn.sigmoid(gt) * up).astype(BF16)
        ye = jnp.dot(he, wd_ref[...], preferred_element_type=F32)
        o_ref[...] = ye.reshape(ntile, nslot, D_MODEL).astype(o_ref.dtype)

    ffn(xlo_ref, olo_ref)
    busy = busy_ref[pl.program_id(0), pl.program_id(1)] > 0

    @pl.when(busy)
    def _():
        ffn(xhi_ref, ohi_ref)

    @pl.when(jnp.logical_not(busy))
    def _():
        ohi_ref[...] = jnp.zeros(ohi_ref.shape, ohi_ref.dtype)


def _combine_kernel(busy_ref, x_ref, ylo_ref, yhi_ref, pos_ref, wt_ref, fg_ref, o_ref, *, final_norm):
    tm = x_ref.shape[0]
    nslot = MOE_CAP // 2

    def gathered(y_ref, half, e0, ne):
        weights = _slot_matrix(pos_ref[...], wt_ref[...], tm, half, range(e0, e0 + ne))
        yd = y_ref[e0:e0 + ne].reshape(ne * nslot, D_MODEL)
        return lax.dot_general(weights, yd, (((0,), (0,)), ((), ())), preferred_element_type=F32)

    o_ref[...] = x_ref[...] + gathered(ylo_ref, 0, 0, N_EXPERTS)

    for grp in range(N_GROUPS):
        def add(grp=grp):
            o_ref[...] += gathered(yhi_ref, 1, grp * EXPERTS_PER_GROUP, EXPERTS_PER_GROUP)

        pl.when(busy_ref[pl.program_id(0), grp] > 0)(add)

    if final_norm:
        o_ref[...] = _rms(o_ref[...], fg_ref[...])


def _moe_dispatch(x, g, w_router, b_router):
    T = x.shape[0]
    tm = MOE_TILE
    nt = T // tm
    tri = jnp.asarray(np.triu(np.ones((tm, tm), np.float32), 1), BF16)
    xd, pos_t, w_t, cnt = pl.pallas_call(
        _dispatch_kernel,
        grid=(nt,),
        in_specs=[pl.BlockSpec((tm, D_MODEL), lambda i: (i, 0)),
                  pl.BlockSpec((1, D_MODEL), lambda i: (0, 0)),
                  pl.BlockSpec((2 * LANES, D_MODEL), lambda i: (0, 0)),
                  pl.BlockSpec((LANES, 1), lambda i: (0, 0)),
                  pl.BlockSpec((tm, tm), lambda i: (0, 0))],
        out_specs=[pl.BlockSpec((N_EXPERTS, None, MOE_CAP, D_MODEL), lambda i: (0, i, 0, 0)),
                   pl.BlockSpec((LANES, tm), lambda i: (0, i)),
                   pl.BlockSpec((LANES, tm), lambda i: (0, i)),
                   pl.BlockSpec((None, LANES, LANES), lambda i: (i, 0, 0))],
        out_shape=[jax.ShapeDtypeStruct((N_EXPERTS, nt, MOE_CAP, D_MODEL), BF16),
                   jax.ShapeDtypeStruct((LANES, T), F32),
                   jax.ShapeDtypeStruct((LANES, T), F32),
                   jax.ShapeDtypeStruct((nt, LANES, LANES), F32)],
        compiler_params=_cparams("arbitrary"),
        name="moe_dispatch",
    )(x, g, w_router.T, b_router.reshape(LANES, 1), tri)
    return xd, pos_t, w_t, cnt[:, 0:N_EXPERTS, 0]


def _moe_finish(x, xd, pos_t, w_t, counts, wg, wu, wd, final_g, final_norm):
    T = x.shape[0]
    tm = MOE_TILE
    nt = T // tm
    tps = min(MOE_TILES_PER_STEP, nt)
    assert nt % tps == 0
    nslot = MOE_CAP // 2
    hot = counts > nslot
    busy_blocks = jnp.any(hot.reshape(nt // tps, tps, N_EXPERTS), axis=1).T.astype(I32)
    busy_tiles = jnp.any(hot.reshape(nt, N_GROUPS, EXPERTS_PER_GROUP), axis=2).astype(I32)
    halves = xd.reshape(N_EXPERTS, nt, 2, nslot, D_MODEL)
    half_out = jax.ShapeDtypeStruct((N_EXPERTS, nt, nslot, D_MODEL), BF16)
    yd_lo, yd_hi = pl.pallas_call(
        _expert_kernel,
        grid_spec=pltpu.PrefetchScalarGridSpec(
            num_scalar_prefetch=1,
            grid=(N_EXPERTS, nt // tps),
            in_specs=[pl.BlockSpec((None, tps, None, nslot, D_MODEL), lambda e, c, b: (e, c, 0, 0, 0)),
                      pl.BlockSpec((None, tps, None, nslot, D_MODEL),
                                   lambda e, c, b: (e, jnp.where(b[e, c] > 0, c, 0), 1, 0, 0)),
                      pl.BlockSpec((None, D_MODEL, EXPERT_FF), lambda e, c, b: (e, 0, 0)),
                      pl.BlockSpec((None, D_MODEL, EXPERT_FF), lambda e, c, b: (e, 0, 0)),
                      pl.BlockSpec((None, EXPERT_FF, D_MODEL), lambda e, c, b: (e, 0, 0))],
            out_specs=[pl.BlockSpec((None, tps, nslot, D_MODEL), lambda e, c, b: (e, c, 0, 0)),
                       pl.BlockSpec((None, tps, nslot, D_MODEL), lambda e, c, b: (e, c, 0, 0))]),
        out_shape=[half_out, half_out],
        compiler_params=_cparams("arbitrary", "arbitrary"),
        name="moe_experts",
    )(busy_blocks, halves, halves, wg, wu, wd)
    return pl.pallas_call(
        functools.partial(_combine_kernel, final_norm=final_norm),
        grid_spec=pltpu.PrefetchScalarGridSpec(
            num_scalar_prefetch=1,
            grid=(nt,),
            in_specs=[pl.BlockSpec((tm, D_MODEL), lambda i, b: (i, 0)),
                      pl.BlockSpec((N_EXPERTS, None, nslot, D_MODEL), lambda i, b: (0, i, 0, 0)),
                      pl.BlockSpec((N_EXPERTS, None, nslot, D_MODEL),
                                   lambda i, b: (0, jnp.where(b[i, 0] + b[i, 1] + b[i, 2] + b[i, 3] > 0, i, 0), 0, 0)),
                      pl.BlockSpec((LANES, tm), lambda i, b: (0, i)),
                      pl.BlockSpec((LANES, tm), lambda i, b: (0, i)),
                      pl.BlockSpec((1, D_MODEL), lambda i, b: (0, 0))],
            out_specs=pl.BlockSpec((tm, D_MODEL), lambda i, b: (i, 0))),
        out_shape=jax.ShapeDtypeStruct((T, D_MODEL), F32),
        compiler_params=_cparams("arbitrary"),
        name="moe_combine",
    )(busy_tiles, x, yd_lo, yd_hi, pos_t, w_t, final_g)


def _block_diag(w):
    nb, bs, _ = w.shape
    out = jnp.zeros((nb * bs, nb * bs), w.dtype)
    for i in range(nb):
        out = out.at[i * bs:(i + 1) * bs, i * bs:(i + 1) * bs].set(w[i])
    return out


def _selection_aggregator_np(n_cmp, n_slc):
    ratio_s = NSA_SLC_LEN // NSA_CMP_STRIDE
    ratio_c = NSA_CMP_LEN // NSA_CMP_STRIDE
    w = np.convolve(np.ones(ratio_s), np.ones(ratio_c)).astype(np.float32)
    cj = np.arange(n_slc)[:, None] * ratio_s - (ratio_c - 1) + np.arange(w.size)[None, :]
    jj = np.broadcast_to(np.arange(n_slc)[:, None], cj.shape)
    ww = np.broadcast_to(w[None, :], cj.shape)
    keep = (cj >= 0) & (cj < n_cmp)
    agg = np.zeros((n_cmp, n_slc), np.float32)
    np.add.at(agg, (cj[keep], jj[keep]), ww[keep])
    return agg


def _row(v):
    return v.reshape(1, -1).astype(F32)


def _diff_lambda_init(layer):
    return 0.8 - 0.6 * math.exp(-0.3 * layer)


def _mixer_layer(x, l, B, S, tiles, p):
    (bias_swa, bias_diff, bias_slc, bias_win, bias_cmp, agg_pad, n_cmp, ncp) = tiles
    w_in = jnp.pad(p["w_in"][l], ((0, 0), (0, N_IN_PAD - p["w_in"].shape[2]))).astype(BF16)
    w_t = jnp.concatenate([w_in[:, 1536:1792], w_in[:, 2176:2304]], axis=1).T
    a_xg, b_qkv, c_qkv, d_all, vals_t = _in_proj(x, _row(p["mix_norm_g"][l]), w_in, w_t)

    y_a = _rglru(a_xg, p["conv_w"][l].astype(F32), _row(p["conv_b"][l]),
                 _block_diag(p["lru_w_a"][l]).astype(BF16), _row(p["lru_b_a"][l]),
                 _block_diag(p["lru_w_x"][l]).astype(BF16), _row(p["lru_b_x"][l]),
                 _row(p["lru_lambda"][l]), B, S)

    b3 = b_qkv.reshape(B, S, 512)
    swa_cfg = tuple(((h // 2) * 64, (h % 2) != (h // 2), h // 2) for h in range(SWA_HEADS))
    y_b = _banded_attention(b3, 0, b3, 2, b3, 3, bias_swa, _row(p["swa_sinks"][l]),
                            B=B, S=S, window=SWA_WINDOW, q_cfg=swa_cfg)

    c3 = c_qkv.reshape(B, S, 768)
    lam4 = jnp.stack([p["diff_lq1"][l], p["diff_lk1"][l], p["diff_lq2"][l], p["diff_lk2"][l]]).astype(F32)
    gsub = jnp.tile(p["diff_subln_g"][l].reshape(-1, 1).astype(F32), (2, 1))
    y_c = _flash_attention("diff", c3, 0, c3, 1, vals_t, 256, 0, bias_diff, (lam4, gsub), B=B, S=S,
                           lambda_init=_diff_lambda_init(l))

    d3 = d_all.reshape(B, S, 768)
    nchunk = S // NSA_CMP_STRIDE
    ck = d3[:, :, 256:320].reshape(B, nchunk, NSA_CMP_STRIDE * HEAD_DIM)
    cv = d3[:, :, 320:384].reshape(B, nchunk, NSA_CMP_STRIDE * HEAD_DIM)

    def pos8(a):
        return jnp.broadcast_to(a.reshape(1, -1), (8, a.size)).astype(BF16)

    w1k = jnp.pad(p["nsa_w1_k"][l], ((0, 0), (0, HEAD_DIM))).astype(BF16)
    w1v = jnp.pad(p["nsa_w1_v"][l], ((0, 0), (HEAD_DIM, 0))).astype(BF16)
    w2 = _block_diag(jnp.stack([p["nsa_w2_k"][l], p["nsa_w2_v"][l]])).astype(BF16)
    kcv = _nsa_compress(ck, cv, pos8(p["nsa_pos_k"][l]), pos8(p["nsa_pos_v"][l]), w1k, w1v, w2, ncp)
    o_c, sel = _nsa_cmp_select(d3, kcv, bias_cmp, agg_pad, B=B, S=S, n_cmp=n_cmp, ncp=ncp)
    o_s = _flash_attention("nsa", d3, 0, d3, 3, vals_t, 128, 2, bias_slc, (sel,), B=B, S=S)
    win_cfg = tuple((0, h % 2 == 1, 1) for h in range(NSA_HEADS))
    o_w = _banded_attention(d3, 0, d3, 4, None, None, bias_win, None, B=B, S=S, window=NSA_WINDOW, q_cfg=win_cfg)

    T = B * S
    return _merge(x, _row(p["mix_norm_g"][l]), y_a, y_b.reshape(T, 256), y_c.reshape(T, 256),
                  o_c.reshape(T, 256), o_s.reshape(T, 256), o_w.reshape(T, 256), d_all,
                  p["w_gate"][l].astype(BF16), p["b_gate"][l].astype(F32), p["w_branch"][l].astype(BF16),
                  p["w_out"][l].astype(BF16))


def _moe_layer(x, l, p, final_norm):
    w_router = jnp.concatenate([p["w_router_exp"][l], p["w_router_grp"][l]], axis=1)
    w_router = jnp.pad(w_router, ((0, 0), (0, LANES - w_router.shape[1]))).astype(F32)
    w_hi = w_router.astype(BF16)
    w_router = jnp.concatenate([w_hi, (w_router - w_hi.astype(F32)).astype(BF16)], axis=1)
    b_router = jnp.concatenate([p["b_router_exp"][l], p["b_router_grp"][l]])
    b_router = jnp.pad(b_router, (0, LANES - b_router.shape[0])).reshape(1, LANES).astype(F32)
    g = _row(p["ffn_norm_g"][l])
    fg = _row(p["final_norm_g"])
    wg, wu, wd = (p[k][l].astype(BF16) for k in ("w_exp_gate", "w_exp_up", "w_exp_down"))
    xd, pos_t, w_t, counts = _moe_dispatch(x, g, w_router, b_router)
    return lax.cond(jnp.max(counts) > MOE_CAP,
                    lambda: _moe(x, g, w_router, b_router, wg, wu, wd, fg, final_norm),
                    lambda: _moe_finish(x, xd, pos_t, w_t, counts, wg, wu, wd, fg, final_norm))


def _forward(p):
    x = p["x"]
    B, S, D = x.shape
    depth = p["w_in"].shape[0]
    assert D == D_MODEL and S % (16 * Q_BLOCK) == 0 and S // NSA_SLC_LEN <= LANES
    rel = p["rel_bias"].astype(F32)
    n_cmp = (S - NSA_CMP_LEN) // NSA_CMP_STRIDE + 1
    ncp = LANES + S // NSA_CMP_STRIDE
    agg = _selection_aggregator_np(n_cmp, S // NSA_SLC_LEN)
    agg_pad = np.zeros((ncp, LANES), np.float32)
    agg_pad[LANES:LANES + n_cmp, :agg.shape[1]] = agg
    tiles = (
        _bias_tiles(rel, _banded_buckets(1), 0, SWA_HEADS, False),
        _bias_tiles(rel, _flash_buckets(), SWA_HEADS, DIFF_HEADS, True, LOG2E),
        _bias_tiles(rel, _flash_buckets(), SWA_HEADS + DIFF_HEADS, NSA_HEADS, True, LOG2E),
        _bias_tiles(rel, _banded_buckets(NSA_WINDOW // Q_BLOCK), SWA_HEADS + DIFF_HEADS, NSA_HEADS, False),
        _bias_tiles(rel, _cmp_buckets(), SWA_HEADS + DIFF_HEADS, NSA_HEADS, True),
        jnp.asarray(agg_pad, BF16), n_cmp, ncp,
    )
    xt = x.reshape(B * S, D).astype(F32)
    for l in range(depth):
        xt = _mixer_layer(xt, l, B, S, tiles, p)
        xt = _moe_layer(xt, l, p, final_norm=(l == depth - 1))
    return xt.reshape(B, S, D).astype(x.dtype)


def kernel(x, rel_bias, final_norm_g, mix_norm_g, w_in, conv_w, conv_b, lru_w_a, lru_b_a, lru_w_x, lru_b_x,
           lru_lambda, swa_sinks, diff_lq1, diff_lk1, diff_lq2, diff_lk2, diff_subln_g, nsa_pos_k, nsa_w1_k,
           nsa_w2_k, nsa_pos_v, nsa_w1_v, nsa_w2_v, w_gate, b_gate, w_branch, w_out, ffn_norm_g, w_router_grp,
           b_router_grp, w_router_exp, b_router_exp, w_exp_gate, w_exp_up, w_exp_down):
    return _forward(dict(
        x=x, rel_bias=rel_bias, final_norm_g=final_norm_g, mix_norm_g=mix_norm_g, w_in=w_in, conv_w=conv_w,
        conv_b=conv_b, lru_w_a=lru_w_a, lru_b_a=lru_b_a, lru_w_x=lru_w_x, lru_b_x=lru_b_x, lru_lambda=lru_lambda,
        swa_sinks=swa_sinks, diff_lq1=diff_lq1, diff_lk1=diff_lk1, diff_lq2=diff_lq2, diff_lk2=diff_lk2,
        diff_subln_g=diff_subln_g, nsa_pos_k=nsa_pos_k, nsa_w1_k=nsa_w1_k, nsa_w2_k=nsa_w2_k, nsa_pos_v=nsa_pos_v,
        nsa_w1_v=nsa_w1_v, nsa_w2_v=nsa_w2_v, w_gate=w_gate, b_gate=b_gate, w_branch=w_branch, w_out=w_out,
        ffn_norm_g=ffn_norm_g, w_router_grp=w_router_grp, b_router_grp=b_router_grp, w_router_exp=w_router_exp,
        b_router_exp=b_router_exp, w_exp_gate=w_exp_gate, w_exp_up=w_exp_up, w_exp_down=w_exp_down))
```
